```python
import math
import jax, jax.numpy as jnp
from jax import lax
import numpy as np

D_MODEL = 1024
BATCH = 1
SEQ = 16384
DEPTH = 1

EPS = 1e-6
GLA_HEADS = 4
GLA_DK = 48
GLA_DV = 96
GLA_QK_WIDTH = GLA_HEADS * GLA_DK
GLA_WIDTH = GLA_HEADS * GLA_DV
GLA_RANK = 16
GLA_GATE_NORM = 16.0
GLA_CHUNK = 64
FOX_HEADS = 6
FOX_DH = 64
FOX_WIDTH = FOX_HEADS * FOX_DH
FOX_BLOCK = 128
MEM_LEN = 256
MEM_HEADS = 4
MEM_DH = 64
MEM_WIDTH = MEM_HEADS * MEM_DH
MIX_WIDTH = GLA_WIDTH + FOX_WIDTH + MEM_WIDTH
IN_SPLITS = (GLA_QK_WIDTH, GLA_QK_WIDTH, GLA_WIDTH, GLA_RANK, GLA_WIDTH,
             FOX_WIDTH, FOX_WIDTH, FOX_WIDTH, FOX_HEADS, FOX_WIDTH,
             MEM_WIDTH, MEM_WIDTH)
IN_COLS = sum(IN_SPLITS)

kernel_name = "hymba_gla_fox_memory_block"


def rmsnorm(x, g):
    xf = x.astype(jnp.float32)
    y = xf * lax.rsqrt(jnp.mean(xf * xf, axis=-1, keepdims=True) + EPS)
    return (y * g.astype(jnp.float32)).astype(x.dtype)


def gla_chunked(q, k, v, log_a):
    B, T, H, K = q.shape
    V = v.shape[-1]
    C = GLA_CHUNK
    N = T // C
    f32 = jnp.float32

    def chunks(a):
        return a.astype(f32).reshape(B, N, C, H, -1).transpose(1, 0, 3, 2, 4)

    qc = chunks(q) * (K ** -0.5)
    kc, vc, gc = chunks(k), chunks(v), chunks(log_a)
    b = jnp.cumsum(gc, axis=3)
    b_last = b[:, :, :, -1:, :]
    q_dec = qc * jnp.exp(b)
    k_dec = kc * jnp.exp(-b)
    k_to_end = kc * jnp.exp(b_last - b)
    causal = jnp.tril(jnp.ones((C, C), f32))
    attn = jnp.einsum('nbhck,nbhsk->nbhcs', q_dec, k_dec) * causal
    o_intra = jnp.einsum('nbhcs,nbhsv->nbhcv', attn, vc)
    kv_chunk = jnp.einsum('nbhsk,nbhsv->nbhkv', k_to_end, vc)
    decay_chunk = jnp.exp(b_last[:, :, :, 0, :])

    def step(S, inp):
        kv_n, dec_n = inp
        return dec_n[..., None] * S + kv_n, S

    _, S_prev = lax.scan(step, jnp.zeros((B, H, K, V), f32), (kv_chunk, decay_chunk))
    o_inter = jnp.einsum('nbhck,nbhkv->nbhcv', q_dec, S_prev)
    o = (o_intra + o_inter).transpose(1, 0, 3, 2, 4).reshape(B, T, H, V)
    return o


def fox_attention(q, k, v, log_f):
    B, T, H, D = q.shape
    nb = T // FOX_BLOCK
    scale = D ** -0.5
    c = jnp.cumsum(log_f.astype(jnp.float32), axis=1).transpose(0, 2, 1)
    kh = k.transpose(0, 2, 1, 3)
    vh = v.transpose(0, 2, 1, 3)
    q_blocks = q.transpose(0, 2, 1, 3).reshape(B, H, nb, FOX_BLOCK, D).transpose(2, 0, 1, 3, 4)
    c_blocks = c.reshape(B, H, nb, FOX_BLOCK).transpose(2, 0, 1, 3)
    starts = jnp.arange(nb, dtype=jnp.int32) * FOX_BLOCK
    key_pos = jnp.arange(T, dtype=jnp.int32)

    def one_block(args):
        qb, cb, start = args
        s = jnp.einsum('bhqd,bhkd->bhqk', qb, kh).astype(jnp.float32) * scale
        s = s + cb[..., :, None] - c[..., None, :]
        q_pos = start + jnp.arange(FOX_BLOCK, dtype=jnp.int32)
        mask = key_pos[None, :] <= q_pos[:, None]
        p = jax.nn.softmax(jnp.where(mask, s, -jnp.inf), axis=-1)
        return jnp.einsum('bhqk,bhkd->bhqd', p.astype(vh.dtype), vh)

    out = lax.map(one_block, (q_blocks, c_blocks, starts))
    return out.transpose(1, 0, 3, 2, 4).reshape(B, T, H * D)


def memory_attention(q, mk, mv):
    B, T, H, D = q.shape
    s = jnp.einsum('bthd,bmhd->bhtm', q, mk).astype(jnp.float32) * (D ** -0.5)
    p = jax.nn.softmax(s, axis=-1)
    o = jnp.einsum('bhtm,bmhd->bthd', p.astype(mv.dtype), mv)
    return o.reshape(B, T, H * D)


def hybrid_layer(x, mem, norm_g, w_in, w_alpha_up, b_alpha, b_forget, gla_norm_g,
                 mem_norm_g, w_mem_kv, w_out):
    B, T, _ = x.shape
    h = rmsnorm(x, norm_g)
    proj = h @ w_in
    idx = [int(i) for i in np.cumsum(IN_SPLITS)[:-1]]
    (g_q, g_k, g_v, g_lr, g_gate, f_q, f_k, f_v, f_fg, f_gate,
     m_q, m_gate) = jnp.split(proj, idx, axis=-1)

    z_alpha = g_lr @ w_alpha_up + b_alpha
    log_a = jax.nn.log_sigmoid(z_alpha.astype(jnp.float32)) / GLA_GATE_NORM
    o_gla = gla_chunked(g_q.reshape(B, T, GLA_HEADS, GLA_DK),
                        g_k.reshape(B, T, GLA_HEADS, GLA_DK),
                        g_v.reshape(B, T, GLA_HEADS, GLA_DV),
                        log_a.reshape(B, T, GLA_HEADS, GLA_DK))
    o_gla = rmsnorm(o_gla, gla_norm_g).reshape(B, T, GLA_WIDTH).astype(x.dtype)
    o_gla = o_gla * jax.nn.silu(g_gate)

    log_f = jax.nn.log_sigmoid((f_fg + b_forget).astype(jnp.float32))
    o_fox = fox_attention(f_q.reshape(B, T, FOX_HEADS, FOX_DH),
                          f_k.reshape(B, T, FOX_HEADS, FOX_DH),
                          f_v.reshape(B, T, FOX_HEADS, FOX_DH), log_f)
    o_fox = o_fox.astype(x.dtype) * jax.nn.silu(f_gate)

    mkv = rmsnorm(mem, mem_norm_g) @ w_mem_kv
    M = mem.shape[1]
    mk, mv = jnp.split(mkv, 2, axis=-1)
    o_mem = memory_attention(m_q.reshape(B, T, MEM_HEADS, MEM_DH),
                             mk.reshape(B, M, MEM_HEADS, MEM_DH),
                             mv.reshape(B, M, MEM_HEADS, MEM_DH))
    o_mem = o_mem.astype(x.dtype) * jax.nn.silu(m_gate)

    mixed = jnp.concatenate([o_gla, o_fox, o_mem], axis=-1)
    return x + mixed @ w_out


def setup_inputs(seed: int = 0) -> dict:
    key = jax.random.key(seed)
    ks = jax.random.split(key, 14)
    f32 = jnp.float32
    nrm = lambda k, s: jax.random.normal(k, s, f32)
    return {
        "x": nrm(ks[0], (BATCH, SEQ, D_MODEL)),
        "mem": nrm(ks[1], (BATCH, MEM_LEN, D_MODEL)),
        "norm_g": 1.0 + 0.02 * nrm(ks[2], (DEPTH, D_MODEL)),
        "w_in": nrm(ks[3], (DEPTH, D_MODEL, IN_COLS)) * D_MODEL ** -0.5,
        "w_alpha_up": nrm(ks[4], (DEPTH, GLA_RANK, GLA_QK_WIDTH)) * GLA_RANK ** -0.5,
        "b_alpha": 0.1 * nrm(ks[5], (DEPTH, GLA_QK_WIDTH)),
        "b_forget": 2.0 + 0.5 * nrm(ks[6], (DEPTH, FOX_HEADS)),
        "gla_norm_g": 1.0 + 0.02 * nrm(ks[7], (DEPTH, GLA_DV)),
        "mem_norm_g": 1.0 + 0.02 * nrm(ks[8], (DEPTH, D_MODEL)),
        "w_mem_kv": nrm(ks[9], (DEPTH, D_MODEL, 2 * MEM_WIDTH)) * D_MODEL ** -0.5,
        "w_out": nrm(ks[10], (DEPTH, MIX_WIDTH, D_MODEL)) * MIX_WIDTH ** -0.5,
        "final_norm_g": 1.0 + 0.02 * nrm(ks[11], (D_MODEL,)),
    }


def reference(x, mem, norm_g, w_in, w_alpha_up, b_alpha, b_forget, gla_norm_g,
              mem_norm_g, w_mem_kv, w_out, final_norm_g):
    h = x
    for layer in range(DEPTH):
        h = hybrid_layer(h, mem, norm_g[layer], w_in[layer], w_alpha_up[layer],
                         b_alpha[layer], b_forget[layer], gla_norm_g[layer],
                         mem_norm_g[layer], w_mem_kv[layer], w_out[layer])
    return rmsnorm(h, final_norm_g)
```

```python
import functools

import jax
import jax.numpy as jnp
from jax import lax
from jax.experimental import pallas as pl
from jax.experimental.pallas import tpu as pltpu

F32 = jnp.float32
BF16 = jnp.bfloat16

EPS = 1e-6
LANES = 128
SUBLANES = 8

D_MODEL = 1024
GLA_HEADS, GLA_DK, GLA_DV, GLA_RANK = 4, 48, 96, 16
GLA_DK_PAD = 64
GLA_DV_PAD = LANES
GLA_GATE_NORM = 16.0
GLA_CHUNK = 64
FOX_HEADS, FOX_DH = 6, 64
MEM_HEADS, MEM_DH = 4, 64
HEAD_PAIR = 2
GLA_QK_W = GLA_HEADS * GLA_DK_PAD
GLA_V_W = GLA_HEADS * GLA_DV_PAD
FOX_W = FOX_HEADS * FOX_DH
MEM_W = MEM_HEADS * MEM_DH
SMALL_W = LANES
FG_LANE0 = 0
LR_LANE0 = SUBLANES

_GROUPS = (("gq", GLA_QK_W), ("gk", GLA_QK_W), ("gv", GLA_V_W), ("gg", GLA_V_W),
           ("fq", FOX_W), ("fk", FOX_W), ("fv", FOX_W), ("fgate", FOX_W),
           ("mq", MEM_W), ("mg", MEM_W), ("small", SMALL_W))
_OFF = {}
_o = 0
for _n, _w in _GROUPS:
    _OFF[_n] = (_o, _o + _w)
    _o += _w
IN_COLS_PAD = _o

PROJ_ROWS = 512
GLA_ROWS = 512
FOX_BLOCK = 512
OUT_ROWS = 512
VMEM_LIMIT = 56 * 1024 * 1024

NEG_BIG = -1e30


def _log_sigmoid(z):
    return jnp.minimum(z, 0.0) - jnp.log(1.0 + jnp.exp(-jnp.abs(z)))


def _silu(z):
    return z / (1.0 + jnp.exp(-z))


def _rms_scale(v, width):
    return lax.rsqrt(jnp.sum(v * v, axis=-1, keepdims=True) * (1.0 / width) + EPS)


def _proj_kernel(x_ref, g_ref, w_ref, wa_ref, ba_ref, bf_ref,
                 gq_ref, gk_ref, gv_ref, gg_ref, fq_ref, fk_ref, fv_ref, fgate_ref,
                 mq_ref, mg_ref, loga_ref, crow_ref, carry_ref):
    rows = x_ref.shape[0]

    @pl.when(pl.program_id(0) == 0)
    def _():
        carry_ref[...] = jnp.zeros_like(carry_ref)

    x = x_ref[...]
    xn = (x * _rms_scale(x, D_MODEL) * g_ref[...]).astype(BF16)

    def proj(name):
        lo, hi = _OFF[name]
        return jnp.dot(xn, w_ref[:, lo:hi], preferred_element_type=F32)

    gq_ref[...] = proj("gq").astype(BF16)
    gk_ref[...] = proj("gk").astype(BF16)
    gv_ref[...] = proj("gv").astype(BF16)
    gg_ref[...] = _silu(proj("gg")).astype(BF16)
    fq_ref[...] = (proj("fq") * FOX_DH ** -0.5).astype(BF16)
    fk_ref[...] = proj("fk").astype(BF16)
    fv_ref[...] = proj("fv").astype(BF16)
    fgate_ref[...] = _silu(proj("fgate")).astype(BF16)
    mq_ref[...] = (proj("mq") * MEM_DH ** -0.5).astype(BF16)
    mg_ref[...] = _silu(proj("mg")).astype(BF16)

    small = proj("small")
    z = jnp.dot(small.astype(BF16), wa_ref[...], preferred_element_type=F32) + ba_ref[...]
    loga_ref[...] = _log_sigmoid(z) * (1.0 / GLA_GATE_NORM)

    logf = _log_sigmoid(small + bf_ref[...])
    c = logf.T[0:SUBLANES, :]
    lane = lax.broadcasted_iota(jnp.int32, c.shape, 1)
    shift = 1
    while shift < rows:
        c = c + jnp.where(lane >= shift, pltpu.roll(c, shift, axis=1), 0.0)
        shift *= 2
    c = c + carry_ref[:, 0:1]
    crow_ref[...] = c
    carry_ref[...] = jnp.broadcast_to(c[:, rows - 1:rows], carry_ref.shape)


def _memkv_kernel(mem_ref, g_ref, w_ref, mk_ref, mv_ref):
    m = mem_ref[...]
    mn = (m * _rms_scale(m, D_MODEL) * g_ref[...]).astype(BF16)
    kv = jnp.dot(mn, w_ref[...], preferred_element_type=F32)
    mk_ref[...] = kv[:, :MEM_W].astype(BF16)
    mv_ref[...] = kv[:, MEM_W:].astype(BF16)


def _gla_kernel(q_ref, k_ref, v_ref, loga_ref, gate_ref, ng_ref, o_ref, s_ref):
    C = GLA_CHUNK
    n_chunks = q_ref.shape[0] // C

    @pl.when(pl.program_id(0) == 0)
    def _():
        s_ref[...] = jnp.zeros_like(s_ref)

    row = lax.broadcasted_iota(jnp.int32, (C, LANES), 0)
    lane = lax.broadcasted_iota(jnp.int32, (C, LANES), 1)
    causal = (lax.broadcasted_iota(jnp.int32, (C, C), 0)
              >= lax.broadcasted_iota(jnp.int32, (C, C), 1))
    sq_row = lax.broadcasted_iota(jnp.int32, (LANES, LANES), 0)
    sq_lane = lax.broadcasted_iota(jnp.int32, (LANES, LANES), 1)
    eye = sq_row == sq_lane
    scale = GLA_DK ** -0.5
    nt = (((1,), (1,)), ((), ()))
    tn = (((0,), (0,)), ((), ()))

    def chunk(ci, carry):
        r0 = pl.multiple_of(ci * C, C)
        rs = pl.ds(r0, C)
        for p in range(GLA_HEADS // HEAD_PAIR):
            ls = slice(p * LANES, (p + 1) * LANES)
            b = loga_ref[rs, ls]
            shift = 1
            while shift < C:
                b = b + jnp.where(row >= shift, pltpu.roll(b, shift, axis=0), 0.0)
                shift *= 2
            b_last = b[C - 1:C, :]
            q2 = q_ref[rs, ls].astype(F32) * scale
            k2 = k_ref[rs, ls].astype(F32)
            qd = q2 * jnp.exp(b)
            kd = (k2 * jnp.exp(-b)).astype(BF16)
            ke = (k2 * jnp.exp(b_last - b)).astype(BF16)
            dcol = jnp.exp(jnp.sum(jnp.where(eye, jnp.broadcast_to(b_last, (LANES, LANES)), 0.0),
                                   axis=1, keepdims=True))
            for hh in range(HEAD_PAIR):
                h = p * HEAD_PAIR + hh
                own_lane = (lane < GLA_DK_PAD) if hh == 0 else (lane >= GLA_DK_PAD)
                own_row = (sq_row < GLA_DK_PAD) if hh == 0 else (sq_row >= GLA_DK_PAD)
                qh = jnp.where(own_lane, qd, 0.0).astype(BF16)
                attn = lax.dot_general(qh, kd, nt, preferred_element_type=F32)
                attn = jnp.where(causal, attn, 0.0).astype(BF16)
                vs = slice(h * GLA_DV_PAD, (h + 1) * GLA_DV_PAD)
                vh = v_ref[rs, vs]
                s_prev = s_ref[h]
                o = (jnp.dot(attn, vh, preferred_element_type=F32)
                     + jnp.dot(qh, s_prev.astype(BF16), preferred_element_type=F32))
                kv = lax.dot_general(ke, vh, tn, preferred_element_type=F32)
                s_ref[h] = dcol * s_prev + jnp.where(own_row, kv, 0.0)
                on = o * _rms_scale(o, GLA_DV) * ng_ref[...]
                o_ref[rs, vs] = (on * gate_ref[rs, vs].astype(F32)).astype(BF16)
        return carry

    lax.fori_loop(0, n_chunks, chunk, 0)


def _fox_kernel(q_ref, k_ref, v_ref, crow_ref, gate_ref, o_ref, m_ref, acc_ref):
    blk = q_ref.shape[0]
    pair = pl.program_id(0)
    qi = pl.program_id(1)
    q0 = pl.multiple_of(qi * blk, blk)
    lane = lax.broadcasted_iota(jnp.int32, (1, LANES), 1)
    lo_lanes = lane < FOX_DH
    nt = (((1,), (1,)), ((), ()))
    reps = blk // LANES

    q = q_ref[...]
    zero = jnp.zeros_like(q)
    qh = (jnp.where(lo_lanes, q, zero), jnp.where(lo_lanes, zero, q))
    m_ref[...] = jnp.full_like(m_ref, NEG_BIG)
    acc_ref[...] = jnp.zeros_like(acc_ref)

    def head_row(c8, h):
        sub = lax.broadcasted_iota(jnp.int32, c8.shape, 0)
        return jnp.sum(jnp.where(sub == h, c8, 0.0), axis=0, keepdims=True)

    c_q0 = [head_row(crow_ref[:, pl.ds(q0, LANES)], pair * HEAD_PAIR + hh)[:, 0:1]
            for hh in range(HEAD_PAIR)]

    def step(k0, masked):
        ks = pl.ds(k0, blk)
        kb = k_ref[ks, :]
        vb = v_ref[ks, :]
        one = jnp.ones_like(vb)
        vaug = (jnp.where(lo_lanes, vb, one), jnp.where(lo_lanes, one, vb))
        for hh in range(HEAD_PAIR):
            h = pair * HEAD_PAIR + hh
            bias = c_q0[hh] - head_row(crow_ref[:, ks], h)
            s = lax.dot_general(qh[hh], kb, nt, preferred_element_type=F32) + bias
            if masked:
                qpos = lax.broadcasted_iota(jnp.int32, (blk, blk), 0)
                kpos = lax.broadcasted_iota(jnp.int32, (blk, blk), 1)
                s = jnp.where(kpos <= qpos, s, NEG_BIG)
            m_prev = m_ref[hh]
            m_new = jnp.maximum(m_prev, jnp.max(s, axis=1, keepdims=True))
            p = jnp.exp(s - pltpu.repeat(m_new, reps, axis=1))
            alpha = jnp.exp(m_prev - m_new)
            pv = jnp.dot(p.astype(BF16), vaug[hh], preferred_element_type=F32)
            acc_ref[hh] = alpha * acc_ref[hh] + pv
            m_ref[hh] = m_new

    def body(j, carry):
        step(pl.multiple_of(j * blk, blk), masked=False)
        return carry

    lax.fori_loop(0, qi, body, 0)
    step(q0, masked=True)

    outs = []
    for hh in range(HEAD_PAIR):
        acc = acc_ref[hh]
        outs.append(acc / pltpu.roll(acc, FOX_DH, axis=1))
    o = jnp.where(lo_lanes, outs[0], outs[1])
    o_ref[...] = (o * gate_ref[...].astype(F32)).astype(BF16)


def _out_kernel(x_ref, gla_ref, fox_ref, mq_ref, mg_ref, mk_ref, mv_ref,
                wg_ref, wf_ref, wm_ref, fg_ref, o_ref):
    lane = lax.broadcasted_iota(jnp.int32, (1, LANES), 1)
    lo_lanes = lane < MEM_DH
    nt = (((1,), (1,)), ((), ()))
    mem_parts = []
    for p in range(MEM_HEADS // HEAD_PAIR):
        ls = slice(p * LANES, (p + 1) * LANES)
        q = mq_ref[:, ls]
        kb = mk_ref[:, ls]
        vb = mv_ref[:, ls]
        zero = jnp.zeros_like(q)
        one = jnp.ones_like(vb)
        qh = (jnp.where(lo_lanes, q, zero), jnp.where(lo_lanes, zero, q))
        vaug = (jnp.where(lo_lanes, vb, one), jnp.where(lo_lanes, one, vb))
        outs = []
        for hh in range(HEAD_PAIR):
            s = lax.dot_general(qh[hh], kb, nt, preferred_element_type=F32)
            pexp = jnp.exp(s - jnp.max(s, axis=1, keepdims=True))
            pv = jnp.dot(pexp.astype(BF16), vaug[hh], preferred_element_type=F32)
            outs.append(pv / pltpu.roll(pv, MEM_DH, axis=1))
        o = jnp.where(lo_lanes, outs[0], outs[1])
        mem_parts.append((o * mg_ref[:, ls].astype(F32)).astype(BF16))
    mem_mixed = jnp.concatenate(mem_parts, axis=1)
    y = (x_ref[...]
         + jnp.dot(gla_ref[...], wg_ref[...], preferred_element_type=F32)
         + jnp.dot(fox_ref[...], wf_ref[...], preferred_element_type=F32)
         + jnp.dot(mem_mixed, wm_ref[...], preferred_element_type=F32))
    o_ref[...] = y * _rms_scale(y, D_MODEL) * fg_ref[...]


def _pad_heads(w, heads, d, d_pad):
    lead = w.shape[:-1]
    w = w.reshape(lead + (heads, d))
    w = jnp.pad(w, [(0, 0)] * len(lead) + [(0, 0), (0, d_pad - d)])
    return w.reshape(lead + (heads * d_pad,))


def _pad_rows(w, heads, d, d_pad):
    return _pad_heads(w.T, heads, d, d_pad).T


def _layout_weights(w_in, w_alpha_up, b_alpha, b_forget, gla_norm_g, w_out):
    qk, gw = GLA_HEADS * GLA_DK, GLA_HEADS * GLA_DV
    sizes = (qk, qk, gw, GLA_RANK, gw, FOX_W, FOX_W, FOX_W, FOX_HEADS, FOX_W, MEM_W, MEM_W)
    parts = []
    o = 0
    for s in sizes:
        parts.append(w_in[:, o:o + s])
        o += s
    g_q, g_k, g_v, g_lr, g_gate, f_q, f_k, f_v, f_fg, f_gate, m_q, m_gate = parts
    small = jnp.zeros((D_MODEL, SMALL_W), F32)
    small = small.at[:, FG_LANE0:FG_LANE0 + FOX_HEADS].set(f_fg)
    small = small.at[:, LR_LANE0:LR_LANE0 + GLA_RANK].set(g_lr)
    w_all = jnp.concatenate([
        _pad_heads(g_q, GLA_HEADS, GLA_DK, GLA_DK_PAD),
        _pad_heads(g_k, GLA_HEADS, GLA_DK, GLA_DK_PAD),
        _pad_heads(g_v, GLA_HEADS, GLA_DV, GLA_DV_PAD),
        _pad_heads(g_gate, GLA_HEADS, GLA_DV, GLA_DV_PAD),
        f_q, f_k, f_v, f_gate, m_q, m_gate, small], axis=1).astype(BF16)
    wa = jnp.zeros((SMALL_W, GLA_QK_W), F32)
    wa = wa.at[LR_LANE0:LR_LANE0 + GLA_RANK, :].set(
        _pad_heads(w_alpha_up, GLA_HEADS, GLA_DK, GLA_DK_PAD)).astype(BF16)
    ba = _pad_heads(b_alpha[None, :], GLA_HEADS, GLA_DK, GLA_DK_PAD)
    bf = jnp.zeros((1, SMALL_W), F32).at[0, FG_LANE0:FG_LANE0 + FOX_HEADS].set(b_forget)
    ng = jnp.pad(gla_norm_g, (0, GLA_DV_PAD - GLA_DV))[None, :]
    wo_g = _pad_rows(w_out[:gw], GLA_HEADS, GLA_DV, GLA_DV_PAD).astype(BF16)
    wo_f = w_out[gw:gw + FOX_W].astype(BF16)
    wo_m = w_out[gw + FOX_W:].astype(BF16)
    return w_all, wa, ba, bf, ng, wo_g, wo_f, wo_m


def _params(*sem):
    return pltpu.CompilerParams(dimension_semantics=sem, vmem_limit_bytes=VMEM_LIMIT)


def _layer(x, mem, norm_g, w_in, w_alpha_up, b_alpha, b_forget, gla_norm_g,
           mem_norm_g, w_mem_kv, w_out, out_g):
    T = x.shape[0]
    M = mem.shape[0]
    w_all, wa, ba, bf, ng, wo_g, wo_f, wo_m = _layout_weights(
        w_in, w_alpha_up, b_alpha, b_forget, gla_norm_g, w_out)

    def rows(width, n=PROJ_ROWS):
        return pl.BlockSpec((n, width), lambda i: (i, 0))

    def whole(shape):
        return pl.BlockSpec(shape, lambda i: (0,) * len(shape))

    bshape = lambda w: jax.ShapeDtypeStruct((T, w), BF16)
    (gq, gk, gv, gg, fq, fk, fv, fgate, mq, mg, loga, crow) = pl.pallas_call(
        _proj_kernel,
        grid=(T // PROJ_ROWS,),
        in_specs=[rows(D_MODEL), whole((1, D_MODEL)), whole((D_MODEL, IN_COLS_PAD)),
                  whole((SMALL_W, GLA_QK_W)), whole((1, GLA_QK_W)), whole((1, SMALL_W))],
        out_specs=[rows(GLA_QK_W), rows(GLA_QK_W), rows(GLA_V_W), rows(GLA_V_W),
                   rows(FOX_W), rows(FOX_W), rows(FOX_W), rows(FOX_W),
                   rows(MEM_W), rows(MEM_W), rows(GLA_QK_W),
                   pl.BlockSpec((SUBLANES, PROJ_ROWS), lambda i: (0, i))],
        out_shape=[bshape(GLA_QK_W), bshape(GLA_QK_W), bshape(GLA_V_W), bshape(GLA_V_W),
                   bshape(FOX_W), bshape(FOX_W), bshape(FOX_W), bshape(FOX_W),
                   bshape(MEM_W), bshape(MEM_W),
                   jax.ShapeDtypeStruct((T, GLA_QK_W), F32),
                   jax.ShapeDtypeStruct((SUBLANES, T), F32)],
        scratch_shapes=[pltpu.VMEM((SUBLANES, LANES), F32)],
        compiler_params=_params("arbitrary"),
        name="proj",
    )(x, norm_g[None, :], w_all, wa, ba, bf)

    mk, mv = pl.pallas_call(
        _memkv_kernel,
        out_shape=[jax.ShapeDtypeStruct((M, MEM_W), BF16)] * 2,
        compiler_params=pltpu.CompilerParams(vmem_limit_bytes=VMEM_LIMIT),
        name="memkv",
    )(mem, mem_norm_g[None, :], w_mem_kv.astype(BF16))

    gla = pl.pallas_call(
        _gla_kernel,
        grid=(T // GLA_ROWS,),
        in_specs=[rows(GLA_QK_W, GLA_ROWS), rows(GLA_QK_W, GLA_ROWS), rows(GLA_V_W, GLA_ROWS),
                  rows(GLA_QK_W, GLA_ROWS), rows(GLA_V_W, GLA_ROWS), whole((1, GLA_DV_PAD))],
        out_specs=rows(GLA_V_W, GLA_ROWS),
        out_shape=bshape(GLA_V_W),
        scratch_shapes=[pltpu.VMEM((GLA_HEADS, LANES, LANES), F32)],
        compiler_params=_params("arbitrary"),
        name="gla",
    )(gq, gk, gv, loga, gg, ng)

    pair_rows = pl.BlockSpec((FOX_BLOCK, LANES), lambda p, i: (i, p))
    pair_all = pl.BlockSpec((T, LANES), lambda p, i: (0, p))
    fox = pl.pallas_call(
        _fox_kernel,
        grid=(FOX_HEADS // HEAD_PAIR, T // FOX_BLOCK),
        in_specs=[pair_rows, pair_all, pair_all,
                  pl.BlockSpec((SUBLANES, T), lambda p, i: (0, 0)), pair_rows],
        out_specs=pair_rows,
        out_shape=bshape(FOX_W),
        scratch_shapes=[pltpu.VMEM((HEAD_PAIR, FOX_BLOCK, LANES), F32),
                        pltpu.VMEM((HEAD_PAIR, FOX_BLOCK, LANES), F32)],
        compiler_params=_params("arbitrary", "arbitrary"),
        name="fox",
    )(fq, fk, fv, crow, fgate)

    out = pl.pallas_call(
        _out_kernel,
        grid=(T // OUT_ROWS,),
        in_specs=[rows(D_MODEL, OUT_ROWS), rows(GLA_V_W, OUT_ROWS), rows(FOX_W, OUT_ROWS),
                  rows(MEM_W, OUT_ROWS), rows(MEM_W, OUT_ROWS),
                  whole((M, MEM_W)), whole((M, MEM_W)),
                  whole((GLA_V_W, D_MODEL)), whole((FOX_W, D_MODEL)), whole((MEM_W, D_MODEL)),
                  whole((1, D_MODEL))],
        out_specs=rows(D_MODEL, OUT_ROWS),
        out_shape=jax.ShapeDtypeStruct((T, D_MODEL), F32),
        compiler_params=_params("arbitrary"),
        name="out",
    )(x, gla, fox, mq, mg, mk, mv, wo_g, wo_f, wo_m, out_g[None, :])
    return out


def kernel(x, mem, norm_g, w_in, w_alpha_up, b_alpha, b_forget, gla_norm_g, mem_norm_g,
           w_mem_kv, w_out, final_norm_g):
    assert x.shape[0] == 1 and mem.shape[0] == 1 and norm_g.shape[0] == 1
    assert x.shape[1] % max(PROJ_ROWS, GLA_ROWS, FOX_BLOCK, OUT_ROWS) == 0
    out = _layer(x[0], mem[0], norm_g[0], w_in[0], w_alpha_up[0], b_alpha[0], b_forget[0],
                 gla_norm_g[0], mem_norm_g[0], w_mem_kv[0], w_out[0], final_norm_g)
    return out[None]
```

```python
import functools

import jax
import jax.numpy as jnp
from jax import lax
from jax.experimental import pallas as pl
from jax.experimental.pallas import tpu as pltpu

F32 = jnp.float32
BF16 = jnp.bfloat16

EPS = 1e-6
LANES = 128
SUBLANES = 8

D_MODEL = 1024
GLA_HEADS, GLA_DK, GLA_DV, GLA_RANK = 4, 48, 96, 16
GLA_DK_PAD = 64
GLA_DV_PAD = LANES
GLA_GATE_NORM = 16.0
GLA_CHUNK = 64
FOX_HEADS, FOX_DH = 6, 64
MEM_HEADS, MEM_DH = 4, 64
HEAD_PAIR = 2
GLA_QK_W = GLA_HEADS * GLA_DK_PAD
GLA_V_W = GLA_HEADS * GLA_DV_PAD
FOX_W = FOX_HEADS * FOX_DH
MEM_W = MEM_HEADS * MEM_DH
SMALL_W = LANES
FG_LANE0 = 0
LR_LANE0 = SUBLANES

_GROUPS = (("gq", GLA_QK_W), ("gk", GLA_QK_W), ("gv", GLA_V_W), ("gg", GLA_V_W),
           ("fq", FOX_W), ("fk", FOX_W), ("fv", FOX_W), ("fgate", FOX_W),
           ("mq", MEM_W), ("mg", MEM_W), ("small", SMALL_W))
_OFF = {}
_o = 0
for _n, _w in _GROUPS:
    _OFF[_n] = (_o, _o + _w)
    _o += _w
IN_COLS_PAD = _o

PROJ_ROWS = 512
GLA_ROWS = 512
FOX_BLOCK = 512
OUT_ROWS = 512
VMEM_LIMIT = 56 * 1024 * 1024

NEG_BIG = -1e30
FOX_SKIP_NATS = 116.0
NORM_SLACK = 1.02


def _log_sigmoid(z):
    return jnp.minimum(z, 0.0) - jnp.log(1.0 + jnp.exp(-jnp.abs(z)))


def _silu(z):
    return z / (1.0 + jnp.exp(-z))


def _rms_scale(v, width):
    return lax.rsqrt(jnp.sum(v * v, axis=-1, keepdims=True) * (1.0 / width) + EPS)


def _proj_kernel(x_ref, g_ref, w_ref, wa_ref, ba_ref, bf_ref, seg_ref,
                 gq_ref, gk_ref, gv_ref, gg_ref, fq_ref, fk_ref, fv_ref, fgate_ref,
                 mq_ref, mg_ref, loga_ref, crow_ref, qn2_ref, kn2_ref, carry_ref):
    rows = x_ref.shape[0]

    @pl.when(pl.program_id(0) == 0)
    def _():
        carry_ref[...] = jnp.zeros_like(carry_ref)

    x = x_ref[...]
    xn = (x * _rms_scale(x, D_MODEL) * g_ref[...]).astype(BF16)

    def proj(name):
        lo, hi = _OFF[name]
        return jnp.dot(xn, w_ref[:, lo:hi], preferred_element_type=F32)

    gq_ref[...] = proj("gq").astype(BF16)
    gk_ref[...] = proj("gk").astype(BF16)
    gv_ref[...] = proj("gv").astype(BF16)
    gg_ref[...] = _silu(proj("gg")).astype(BF16)
    fq = (proj("fq") * FOX_DH ** -0.5).astype(BF16)
    fk = proj("fk").astype(BF16)
    fq_ref[...] = fq
    fk_ref[...] = fk

    def max_sq_norm(v):
        v32 = v.astype(F32)
        n2 = jnp.dot((v32 * v32).astype(BF16), seg_ref[...], preferred_element_type=F32)
        return jnp.max(n2, axis=0, keepdims=True)

    qn2_ref[0] = max_sq_norm(fq)
    kn2_ref[0] = max_sq_norm(fk)
    fv_ref[...] = proj("fv").astype(BF16)
    fgate_ref[...] = _silu(proj("fgate")).astype(BF16)
    mq_ref[...] = (proj("mq") * MEM_DH ** -0.5).astype(BF16)
    mg_ref[...] = _silu(proj("mg")).astype(BF16)

    small = proj("small")
    z = jnp.dot(small.astype(BF16), wa_ref[...], preferred_element_type=F32) + ba_ref[...]
    loga_ref[...] = _log_sigmoid(z) * (1.0 / GLA_GATE_NORM)

    logf = _log_sigmoid(small + bf_ref[...])
    c = logf.T[0:SUBLANES, :]
    lane = lax.broadcasted_iota(jnp.int32, c.shape, 1)
    shift = 1
    while shift < rows:
        c = c + jnp.where(lane >= shift, pltpu.roll(c, shift, axis=1), 0.0)
        shift *= 2
    c = c + carry_ref[:, 0:1]
    crow_ref[...] = c
    carry_ref[...] = jnp.broadcast_to(c[:, rows - 1:rows], carry_ref.shape)


def _memkv_kernel(mem_ref, g_ref, w_ref, mk_ref, mv_ref):
    m = mem_ref[...]
    mn = (m * _rms_scale(m, D_MODEL) * g_ref[...]).astype(BF16)
    kv = jnp.dot(mn, w_ref[...], preferred_element_type=F32)
    mk_ref[...] = kv[:, :MEM_W].astype(BF16)
    mv_ref[...] = kv[:, MEM_W:].astype(BF16)


def _gla_kernel(q_ref, k_ref, v_ref, loga_ref, gate_ref, ng_ref, o_ref, s_ref):
    C = GLA_CHUNK
    n_chunks = q_ref.shape[0] // C

    @pl.when(pl.program_id(0) == 0)
    def _():
        s_ref[...] = jnp.zeros_like(s_ref)

    row = lax.broadcasted_iota(jnp.int32, (C, LANES), 0)
    lane = lax.broadcasted_iota(jnp.int32, (C, LANES), 1)
    causal = (lax.broadcasted_iota(jnp.int32, (C, C), 0)
              >= lax.broadcasted_iota(jnp.int32, (C, C), 1))
    sq_row = lax.broadcasted_iota(jnp.int32, (LANES, LANES), 0)
    sq_lane = lax.broadcasted_iota(jnp.int32, (LANES, LANES), 1)
    eye = sq_row == sq_lane
    scale = GLA_DK ** -0.5
    nt = (((1,), (1,)), ((), ()))
    tn = (((0,), (0,)), ((), ()))

    def chunk(ci, carry):
        r0 = pl.multiple_of(ci * C, C)
        rs = pl.ds(r0, C)
        for p in range(GLA_HEADS // HEAD_PAIR):
            ls = slice(p * LANES, (p + 1) * LANES)
            b = loga_ref[rs, ls]
            shift = 1
            while shift < C:
                b = b + jnp.where(row >= shift, pltpu.roll(b, shift, axis=0), 0.0)
                shift *= 2
            b_last = b[C - 1:C, :]
            q2 = q_ref[rs, ls].astype(F32) * scale
            k2 = k_ref[rs, ls].astype(F32)
            qd = q2 * jnp.exp(b)
            kd = (k2 * jnp.exp(-b)).astype(BF16)
            ke = (k2 * jnp.exp(b_last - b)).astype(BF16)
            dcol = jnp.exp(jnp.sum(jnp.where(eye, jnp.broadcast_to(b_last, (LANES, LANES)), 0.0),
                                   axis=1, keepdims=True))
            for hh in range(HEAD_PAIR):
                h = p * HEAD_PAIR + hh
                own_lane = (lane < GLA_DK_PAD) if hh == 0 else (lane >= GLA_DK_PAD)
                own_row = (sq_row < GLA_DK_PAD) if hh == 0 else (sq_row >= GLA_DK_PAD)
                qh = jnp.where(own_lane, qd, 0.0).astype(BF16)
                attn = lax.dot_general(qh, kd, nt, preferred_element_type=F32)
                attn = jnp.where(causal, attn, 0.0).astype(BF16)
                vs = slice(h * GLA_DV_PAD, (h + 1) * GLA_DV_PAD)
                vh = v_ref[rs, vs]
                s_prev = s_ref[h]
                o = (jnp.dot(attn, vh, preferred_element_type=F32)
                     + jnp.dot(qh, s_prev.astype(BF16), preferred_element_type=F32))
                kv = lax.dot_general(ke, vh, tn, preferred_element_type=F32)
                s_ref[h] = dcol * s_prev + jnp.where(own_row, kv, 0.0)
                on = o * _rms_scale(o, GLA_DV) * ng_ref[...]
                o_ref[rs, vs] = (on * gate_ref[rs, vs].astype(F32)).astype(BF16)
        return carry

    lax.fori_loop(0, n_chunks, chunk, 0)


def _fox_kernel(lim_ref, cend_ref, q_ref, k_ref, v_ref, crow_ref, gate_ref, o_ref, m_ref, acc_ref):
    blk = q_ref.shape[0]
    pair = pl.program_id(0)
    qi = pl.program_id(1)
    nblk = pl.num_programs(1)
    q0 = pl.multiple_of(qi * blk, blk)
    lane = lax.broadcasted_iota(jnp.int32, (1, LANES), 1)
    lo_lanes = lane < FOX_DH
    nt = (((1,), (1,)), ((), ()))
    reps = blk // LANES

    q = q_ref[...]
    zero = jnp.zeros_like(q)
    qh = (jnp.where(lo_lanes, q, zero), jnp.where(lo_lanes, zero, q))
    m_ref[...] = jnp.full_like(m_ref, NEG_BIG)
    acc_ref[...] = jnp.zeros_like(acc_ref)

    def head_row(c8, h):
        sub = lax.broadcasted_iota(jnp.int32, c8.shape, 0)
        return jnp.sum(jnp.where(sub == h, c8, 0.0), axis=0, keepdims=True)

    c_q0 = [head_row(crow_ref[:, pl.ds(q0, LANES)], pair * HEAD_PAIR + hh)[:, 0:1]
            for hh in range(HEAD_PAIR)]

    def step(k0, masked):
        ks = pl.ds(k0, blk)
        kb = k_ref[ks, :]
        vb = v_ref[ks, :]
        one = jnp.ones_like(vb)
        vaug = (jnp.where(lo_lanes, vb, one), jnp.where(lo_lanes, one, vb))
        for hh in range(HEAD_PAIR):
            h = pair * HEAD_PAIR + hh
            bias = c_q0[hh] - head_row(crow_ref[:, ks], h)
            s = lax.dot_general(qh[hh], kb, nt, preferred_element_type=F32) + bias
            if masked:
                qpos = lax.broadcasted_iota(jnp.int32, (blk, blk), 0)
                kpos = lax.broadcasted_iota(jnp.int32, (blk, blk), 1)
                s = jnp.where(kpos <= qpos, s, NEG_BIG)
            m_prev = m_ref[hh]
            m_new = jnp.maximum(m_prev, jnp.max(s, axis=1, keepdims=True))
            p = jnp.exp(s - pltpu.repeat(m_new, reps, axis=1))
            alpha = jnp.exp(m_prev - m_new)
            pv = jnp.dot(p.astype(BF16), vaug[hh], preferred_element_type=F32)
            acc_ref[hh] = alpha * acc_ref[hh] + pv
            m_ref[hh] = m_new

    step(q0, masked=True)

    def live(j):
        jj = jnp.maximum(j, 0)
        prev = jnp.maximum(qi - 1, 0)
        keep = False
        for hh in range(HEAD_PAIR):
            base = (pair * HEAD_PAIR + hh) * nblk
            bound = cend_ref[base + prev] - cend_ref[base + jj] + lim_ref[base + qi]
            keep = jnp.logical_or(keep, jnp.logical_not(bound <= 0.0))
        return jnp.logical_and(j >= 0, keep)

    def body(j):
        step(pl.multiple_of(j * blk, blk), masked=False)
        return j - 1

    lax.while_loop(live, body, qi - 1)

    outs = []
    for hh in range(HEAD_PAIR):
        acc = acc_ref[hh]
        outs.append(acc / pltpu.roll(acc, FOX_DH, axis=1))
    o = jnp.where(lo_lanes, outs[0], outs[1])
    o_ref[...] = (o * gate_ref[...].astype(F32)).astype(BF16)


def _out_kernel(x_ref, gla_ref, fox_ref, mq_ref, mg_ref, mk_ref, mv_ref,
                wg_ref, wf_ref, wm_ref, fg_ref, o_ref):
    lane = lax.broadcasted_iota(jnp.int32, (1, LANES), 1)
    lo_lanes = lane < MEM_DH
    nt = (((1,), (1,)), ((), ()))
    mem_parts = []
    for p in range(MEM_HEADS // HEAD_PAIR):
        ls = slice(p * LANES, (p + 1) * LANES)
        q = mq_ref[:, ls]
        kb = mk_ref[:, ls]
        vb = mv_ref[:, ls]
        zero = jnp.zeros_like(q)
        one = jnp.ones_like(vb)
        qh = (jnp.where(lo_lanes, q, zero), jnp.where(lo_lanes, zero, q))
        vaug = (jnp.where(lo_lanes, vb, one), jnp.where(lo_lanes, one, vb))
        outs = []
        for hh in range(HEAD_PAIR):
            s = lax.dot_general(qh[hh], kb, nt, preferred_element_type=F32)
            pexp = jnp.exp(s - jnp.max(s, axis=1, keepdims=True))
            pv = jnp.dot(pexp.astype(BF16), vaug[hh], preferred_element_type=F32)
            outs.append(pv / pltpu.roll(pv, MEM_DH, axis=1))
        o = jnp.where(lo_lanes, outs[0], outs[1])
        mem_parts.append((o * mg_ref[:, ls].astype(F32)).astype(BF16))
    mem_mixed = jnp.concatenate(mem_parts, axis=1)
    y = (x_ref[...]
         + jnp.dot(gla_ref[...], wg_ref[...], preferred_element_type=F32)
         + jnp.dot(fox_ref[...], wf_ref[...], preferred_element_type=F32)
         + jnp.dot(mem_mixed, wm_ref[...], preferred_element_type=F32))
    o_ref[...] = y * _rms_scale(y, D_MODEL) * fg_ref[...]


def _pad_heads(w, heads, d, d_pad):
    lead = w.shape[:-1]
    w = w.reshape(lead + (heads, d))
    w = jnp.pad(w, [(0, 0)] * len(lead) + [(0, 0), (0, d_pad - d)])
    return w.reshape(lead + (heads * d_pad,))


def _pad_rows(w, heads, d, d_pad):
    return _pad_heads(w.T, heads, d, d_pad).T


def _layout_weights(w_in, w_alpha_up, b_alpha, b_forget, gla_norm_g, w_out):
    qk, gw = GLA_HEADS * GLA_DK, GLA_HEADS * GLA_DV
    sizes = (qk, qk, gw, GLA_RANK, gw, FOX_W, FOX_W, FOX_W, FOX_HEADS, FOX_W, MEM_W, MEM_W)
    parts = []
    o = 0
    for s in sizes:
        parts.append(w_in[:, o:o + s])
        o += s
    g_q, g_k, g_v, g_lr, g_gate, f_q, f_k, f_v, f_fg, f_gate, m_q, m_gate = parts
    small = jnp.zeros((D_MODEL, SMALL_W), F32)
    small = small.at[:, FG_LANE0:FG_LANE0 + FOX_HEADS].set(f_fg)
    small = small.at[:, LR_LANE0:LR_LANE0 + GLA_RANK].set(g_lr)
    w_all = jnp.concatenate([
        _pad_heads(g_q, GLA_HEADS, GLA_DK, GLA_DK_PAD),
        _pad_heads(g_k, GLA_HEADS, GLA_DK, GLA_DK_PAD),
        _pad_heads(g_v, GLA_HEADS, GLA_DV, GLA_DV_PAD),
        _pad_heads(g_gate, GLA_HEADS, GLA_DV, GLA_DV_PAD),
        f_q, f_k, f_v, f_gate, m_q, m_gate, small], axis=1).astype(BF16)
    wa = jnp.zeros((SMALL_W, GLA_QK_W), F32)
    wa = wa.at[LR_LANE0:LR_LANE0 + GLA_RANK, :].set(
        _pad_heads(w_alpha_up, GLA_HEADS, GLA_DK, GLA_DK_PAD)).astype(BF16)
    ba = _pad_heads(b_alpha[None, :], GLA_HEADS, GLA_DK, GLA_DK_PAD)
    bf = jnp.zeros((1, SMALL_W), F32).at[0, FG_LANE0:FG_LANE0 + FOX_HEADS].set(b_forget)
    ng = jnp.pad(gla_norm_g, (0, GLA_DV_PAD - GLA_DV))[None, :]
    wo_g = _pad_rows(w_out[:gw], GLA_HEADS, GLA_DV, GLA_DV_PAD).astype(BF16)
    wo_f = w_out[gw:gw + FOX_W].astype(BF16)
    wo_m = w_out[gw + FOX_W:].astype(BF16)
    return w_all, wa, ba, bf, ng, wo_g, wo_f, wo_m


def _fox_limits(qn2, kn2, crow, blk):
    nblk = crow.shape[1] // blk
    nproj = qn2.shape[0]
    assert nblk % nproj == 0
    qn = jnp.sqrt(qn2[:, 0, :FOX_HEADS])
    kmax = jnp.sqrt(jnp.max(kn2[:, 0, :FOX_HEADS], axis=0))
    lim = FOX_SKIP_NATS + 2.0 * NORM_SLACK * jnp.repeat(qn, nblk // nproj, axis=0) * kmax[None, :]
    cend = crow[:FOX_HEADS, blk - 1::blk]
    return lim.T.reshape(-1), cend.reshape(-1)


def _params(*sem):
    return pltpu.CompilerParams(dimension_semantics=sem, vmem_limit_bytes=VMEM_LIMIT)


def _layer(x, mem, norm_g, w_in, w_alpha_up, b_alpha, b_forget, gla_norm_g,
           mem_norm_g, w_mem_kv, w_out, out_g):
    T = x.shape[0]
    M = mem.shape[0]
    w_all, wa, ba, bf, ng, wo_g, wo_f, wo_m = _layout_weights(
        w_in, w_alpha_up, b_alpha, b_forget, gla_norm_g, w_out)

    def rows(width, n=PROJ_ROWS):
        return pl.BlockSpec((n, width), lambda i: (i, 0))

    def whole(shape):
        return pl.BlockSpec(shape, lambda i: (0,) * len(shape))

    bshape = lambda w: jax.ShapeDtypeStruct((T, w), BF16)
    nproj = T // PROJ_ROWS
    stat_spec = pl.BlockSpec((1, 1, LANES), lambda i: (i, 0, 0))
    stat_shape = jax.ShapeDtypeStruct((nproj, 1, LANES), F32)
    seg = (jnp.arange(FOX_W)[:, None] // FOX_DH == jnp.arange(LANES)[None, :]).astype(BF16)
    (gq, gk, gv, gg, fq, fk, fv, fgate, mq, mg, loga, crow, qn2, kn2) = pl.pallas_call(
        _proj_kernel,
        grid=(nproj,),
        in_specs=[rows(D_MODEL), whole((1, D_MODEL)), whole((D_MODEL, IN_COLS_PAD)),
                  whole((SMALL_W, GLA_QK_W)), whole((1, GLA_QK_W)), whole((1, SMALL_W)),
                  whole((FOX_W, LANES))],
        out_specs=[rows(GLA_QK_W), rows(GLA_QK_W), rows(GLA_V_W), rows(GLA_V_W),
                   rows(FOX_W), rows(FOX_W), rows(FOX_W), rows(FOX_W),
                   rows(MEM_W), rows(MEM_W), rows(GLA_QK_W),
                   pl.BlockSpec((SUBLANES, PROJ_ROWS), lambda i: (0, i)),
                   stat_spec, stat_spec],
        out_shape=[bshape(GLA_QK_W), bshape(GLA_QK_W), bshape(GLA_V_W), bshape(GLA_V_W),
                   bshape(FOX_W), bshape(FOX_W), bshape(FOX_W), bshape(FOX_W),
                   bshape(MEM_W), bshape(MEM_W),
                   jax.ShapeDtypeStruct((T, GLA_QK_W), F32),
                   jax.ShapeDtypeStruct((SUBLANES, T), F32),
                   stat_shape, stat_shape],
        scratch_shapes=[pltpu.VMEM((SUBLANES, LANES), F32)],
        compiler_params=_params("arbitrary"),
        name="proj",
    )(x, norm_g[None, :], w_all, wa, ba, bf, seg)

    mk, mv = pl.pallas_call(
        _memkv_kernel,
        out_shape=[jax.ShapeDtypeStruct((M, MEM_W), BF16)] * 2,
        compiler_params=pltpu.CompilerParams(vmem_limit_bytes=VMEM_LIMIT),
        name="memkv",
    )(mem, mem_norm_g[None, :], w_mem_kv.astype(BF16))

    gla = pl.pallas_call(
        _gla_kernel,
        grid=(T // GLA_ROWS,),
        in_specs=[rows(GLA_QK_W, GLA_ROWS), rows(GLA_QK_W, GLA_ROWS), rows(GLA_V_W, GLA_ROWS),
                  rows(GLA_QK_W, GLA_ROWS), rows(GLA_V_W, GLA_ROWS), whole((1, GLA_DV_PAD))],
        out_specs=rows(GLA_V_W, GLA_ROWS),
        out_shape=bshape(GLA_V_W),
        scratch_shapes=[pltpu.VMEM((GLA_HEADS, LANES, LANES), F32)],
        compiler_params=_params("arbitrary"),
        name="gla",
    )(gq, gk, gv, loga, gg, ng)

    lim, cend = _fox_limits(qn2, kn2, crow, FOX_BLOCK)
    pair_rows = pl.BlockSpec((FOX_BLOCK, LANES), lambda p, i, *_: (i, p))
    pair_all = pl.BlockSpec((T, LANES), lambda p, i, *_: (0, p))
    fox = pl.pallas_call(
        _fox_kernel,
        grid_spec=pltpu.PrefetchScalarGridSpec(
            num_scalar_prefetch=2,
            grid=(FOX_HEADS // HEAD_PAIR, T // FOX_BLOCK),
            in_specs=[pair_rows, pair_all, pair_all,
                      pl.BlockSpec((SUBLANES, T), lambda p, i, *_: (0, 0)), pair_rows],
            out_specs=pair_rows,
            scratch_shapes=[pltpu.VMEM((HEAD_PAIR, FOX_BLOCK, LANES), F32),
                            pltpu.VMEM((HEAD_PAIR, FOX_BLOCK, LANES), F32)]),
        out_shape=bshape(FOX_W),
        compiler_params=_params("arbitrary", "arbitrary"),
        name="fox",
    )(lim, cend, fq, fk, fv, crow, fgate)

    out = pl.pallas_call(
        _out_kernel,
        grid=(T // OUT_ROWS,),
        in_specs=[rows(D_MODEL, OUT_ROWS), rows(GLA_V_W, OUT_ROWS), rows(FOX_W, OUT_ROWS),
                  rows(MEM_W, OUT_ROWS), rows(MEM_W, OUT_ROWS),
                  whole((M, MEM_W)), whole((M, MEM_W)),
                  whole((GLA_V_W, D_MODEL)), whole((FOX_W, D_MODEL)), whole((MEM_W, D_MODEL)),
                  whole((1, D_MODEL))],
        out_specs=rows(D_MODEL, OUT_ROWS),
        out_shape=jax.ShapeDtypeStruct((T, D_MODEL), F32),
        compiler_params=_params("arbitrary"),
        name="out",
    )(x, gla, fox, mq, mg, mk, mv, wo_g, wo_f, wo_m, out_g[None, :])
    return out


def kernel(x, mem, norm_g, w_in, w_alpha_up, b_alpha, b_forget, gla_norm_g, mem_norm_g,
           w_mem_kv, w_out, final_norm_g):
    assert x.shape[0] == 1 and mem.shape[0] == 1 and norm_g.shape[0] == 1
    assert x.shape[1] % max(PROJ_ROWS, GLA_ROWS, FOX_BLOCK, OUT_ROWS) == 0
    out = _layer(x[0], mem[0], norm_g[0], w_in[0], w_alpha_up[0], b_alpha[0], b_forget[0],
                 gla_norm_g[0], mem_norm_g[0], w_mem_kv[0], w_out[0], final_norm_g)
    return out[None]
```

```python
import functools

import jax
import jax.numpy as jnp
from jax import lax
from jax.experimental import pallas as pl
from jax.experimental.pallas import tpu as pltpu

F32 = jnp.float32
BF16 = jnp.bfloat16

EPS = 1e-6
LANES = 128
SUBLANES = 8

D_MODEL = 1024
GLA_HEADS, GLA_DK, GLA_DV, GLA_RANK = 4, 48, 96, 16
GLA_DK_PAD = 64
GLA_DV_PAD = LANES
GLA_GATE_NORM = 16.0
GLA_CHUNK = 64
FOX_HEADS, FOX_DH = 6, 64
MEM_HEADS, MEM_DH = 4, 64
HEAD_PAIR = 2
GLA_QK_W = GLA_HEADS * GLA_DK_PAD
GLA_V_W = GLA_HEADS * GLA_DV_PAD
FOX_W = FOX_HEADS * FOX_DH
MEM_W = MEM_HEADS * MEM_DH
SMALL_W = LANES
FG_LANE0 = 0
LR_LANE0 = SUBLANES

_GROUPS = (("gq", GLA_QK_W), ("gk", GLA_QK_W), ("gv", GLA_V_W), ("gg", GLA_V_W),
           ("fq", FOX_W), ("fk", FOX_W), ("fv", FOX_W), ("fgate", FOX_W),
           ("mq", MEM_W), ("mg", MEM_W), ("small", SMALL_W))
_OFF = {}
_o = 0
for _n, _w in _GROUPS:
    _OFF[_n] = (_o, _o + _w)
    _o += _w
IN_COLS_PAD = _o

PROJ_ROWS = 512
GLA_ROWS = 512
FOX_BLOCK = 512
OUT_ROWS = 512
VMEM_LIMIT = 56 * 1024 * 1024

NEG_BIG = -1e30
FOX_SKIP_NATS = 105.0
NORM_SLACK = 1.02


def _log_sigmoid(z):
    return jnp.minimum(z, 0.0) - jnp.log(1.0 + jnp.exp(-jnp.abs(z)))


def _silu(z):
    return z / (1.0 + jnp.exp(-z))


def _rms_scale(v, width):
    return lax.rsqrt(jnp.sum(v * v, axis=-1, keepdims=True) * (1.0 / width) + EPS)


def _proj_kernel(x_ref, g_ref, w_ref, wa_ref, ba_ref, bf_ref, seg_ref,
                 gq_ref, gk_ref, gv_ref, gg_ref, fq_ref, fk_ref, fv_ref, fgate_ref,
                 mq_ref, mg_ref, loga_ref, crow_ref, qn2_ref, kn2_ref, carry_ref):
    rows = x_ref.shape[0]

    @pl.when(pl.program_id(0) == 0)
    def _():
        carry_ref[...] = jnp.zeros_like(carry_ref)

    x = x_ref[...]
    xn = (x * _rms_scale(x, D_MODEL) * g_ref[...]).astype(BF16)

    def proj(first, last):
        lo, hi = _OFF[first][0], _OFF[last][1]
        y = jnp.dot(xn, w_ref[:, lo:hi], preferred_element_type=F32)
        return lambda name: y[:, _OFF[name][0] - lo:_OFF[name][1] - lo]

    tail = proj("mq", "small")
    small = tail("small")
    logf = _log_sigmoid(small + bf_ref[...])
    c = logf.T[0:SUBLANES, :]
    lane = lax.broadcasted_iota(jnp.int32, c.shape, 1)
    shift = 1
    while shift < rows:
        c = c + jnp.where(lane >= shift, pltpu.roll(c, shift, axis=1), 0.0)
        shift *= 2
    c = c + carry_ref[:, 0:1]
    crow_ref[...] = c
    carry_ref[...] = jnp.broadcast_to(c[:, rows - 1:rows], carry_ref.shape)

    z = jnp.dot(small.astype(BF16), wa_ref[...], preferred_element_type=F32) + ba_ref[...]
    loga_ref[...] = _log_sigmoid(z) * (1.0 / GLA_GATE_NORM)
    mq_ref[...] = (tail("mq") * MEM_DH ** -0.5).astype(BF16)
    mg_ref[...] = _silu(tail("mg")).astype(BF16)

    fox = proj("fq", "fgate")
    fq = (fox("fq") * FOX_DH ** -0.5).astype(BF16)
    fk = fox("fk").astype(BF16)
    fq_ref[...] = fq
    fk_ref[...] = fk
    fv_ref[...] = fox("fv").astype(BF16)
    fgate_ref[...] = _silu(fox("fgate")).astype(BF16)

    def max_sq_norm(v):
        v32 = v.astype(F32)
        n2 = jnp.dot((v32 * v32).astype(BF16), seg_ref[...], preferred_element_type=F32)
        return jnp.max(n2, axis=0, keepdims=True)

    qn2_ref[0] = max_sq_norm(fq)
    kn2_ref[0] = max_sq_norm(fk)

    gla = proj("gq", "gg")
    gq_ref[...] = gla("gq").astype(BF16)
    gk_ref[...] = gla("gk").astype(BF16)
    gv_ref[...] = gla("gv").astype(BF16)
    gg_ref[...] = _silu(gla("gg")).astype(BF16)


def _memkv_kernel(mem_ref, g_ref, w_ref, mk_ref, mv_ref):
    m = mem_ref[...]
    mn = (m * _rms_scale(m, D_MODEL) * g_ref[...]).astype(BF16)
    kv = jnp.dot(mn, w_ref[...], preferred_element_type=F32)
    mk_ref[...] = kv[:, :MEM_W].astype(BF16)
    mv_ref[...] = kv[:, MEM_W:].astype(BF16)


def _gla_kernel(q_ref, k_ref, v_ref, loga_ref, gate_ref, ng_ref, o_ref, s_ref):
    C = GLA_CHUNK
    n_chunks = q_ref.shape[0] // C

    @pl.when(pl.program_id(0) == 0)
    def _():
        s_ref[...] = jnp.zeros_like(s_ref)

    row = lax.broadcasted_iota(jnp.int32, (C, LANES), 0)
    lane = lax.broadcasted_iota(jnp.int32, (C, LANES), 1)
    causal = (lax.broadcasted_iota(jnp.int32, (C, C), 0)
              >= lax.broadcasted_iota(jnp.int32, (C, C), 1))
    sq_row = lax.broadcasted_iota(jnp.int32, (LANES, LANES), 0)
    sq_lane = lax.broadcasted_iota(jnp.int32, (LANES, LANES), 1)
    eye = sq_row == sq_lane
    scale = GLA_DK ** -0.5
    nt = (((1,), (1,)), ((), ()))
    tn = (((0,), (0,)), ((), ()))

    def chunk(ci, carry):
        r0 = pl.multiple_of(ci * C, C)
        rs = pl.ds(r0, C)
        for p in range(GLA_HEADS // HEAD_PAIR):
            ls = slice(p * LANES, (p + 1) * LANES)
            b = loga_ref[rs, ls]
            shift = 1
            while shift < C:
                b = b + jnp.where(row >= shift, pltpu.roll(b, shift, axis=0), 0.0)
                shift *= 2
            b_last = b[C - 1:C, :]
            q2 = q_ref[rs, ls].astype(F32) * scale
            k2 = k_ref[rs, ls].astype(F32)
            qd = q2 * jnp.exp(b)
            kd = (k2 * jnp.exp(-b)).astype(BF16)
            ke = (k2 * jnp.exp(b_last - b)).astype(BF16)
            dcol = jnp.exp(jnp.sum(jnp.where(eye, jnp.broadcast_to(b_last, (LANES, LANES)), 0.0),
                                   axis=1, keepdims=True))
            for hh in range(HEAD_PAIR):
                h = p * HEAD_PAIR + hh
                own_lane = (lane < GLA_DK_PAD) if hh == 0 else (lane >= GLA_DK_PAD)
                own_row = (sq_row < GLA_DK_PAD) if hh == 0 else (sq_row >= GLA_DK_PAD)
                qh = jnp.where(own_lane, qd, 0.0).astype(BF16)
                attn = lax.dot_general(qh, kd, nt, preferred_element_type=F32)
                attn = jnp.where(causal, attn, 0.0).astype(BF16)
                vs = slice(h * GLA_DV_PAD, (h + 1) * GLA_DV_PAD)
                vh = v_ref[rs, vs]
                s_prev = s_ref[h]
                o = (jnp.dot(attn, vh, preferred_element_type=F32)
                     + jnp.dot(qh, s_prev.astype(BF16), preferred_element_type=F32))
                kv = lax.dot_general(ke, vh, tn, preferred_element_type=F32)
                s_ref[h] = dcol * s_prev + jnp.where(own_row, kv, 0.0)
                on = o * _rms_scale(o, GLA_DV) * ng_ref[...]
                o_ref[rs, vs] = (on * gate_ref[rs, vs].astype(F32)).astype(BF16)
        return carry

    lax.fori_loop(0, n_chunks, chunk, 0)


def _fox_kernel(lim_ref, cend_ref, q_ref, k_ref, v_ref, crow_ref, gate_ref, o_ref, m_ref, acc_ref):
    blk = q_ref.shape[0]
    pair = pl.program_id(0)
    qi = pl.program_id(1)
    nblk = pl.num_programs(1)
    q0 = pl.multiple_of(qi * blk, blk)
    lane = lax.broadcasted_iota(jnp.int32, (1, LANES), 1)
    lo_lanes = lane < FOX_DH
    nt = (((1,), (1,)), ((), ()))
    reps = blk // LANES

    q = q_ref[...]
    zero = jnp.zeros_like(q)
    qh = (jnp.where(lo_lanes, q, zero), jnp.where(lo_lanes, zero, q))
    m_ref[...] = jnp.full_like(m_ref, NEG_BIG)
    acc_ref[...] = jnp.zeros_like(acc_ref)

    def head_row(c8, h):
        sub = lax.broadcasted_iota(jnp.int32, c8.shape, 0)
        return jnp.sum(jnp.where(sub == h, c8, 0.0), axis=0, keepdims=True)

    c_q0 = [head_row(crow_ref[:, pl.ds(q0, LANES)], pair * HEAD_PAIR + hh)[:, 0:1]
            for hh in range(HEAD_PAIR)]

    def step(k0, masked):
        ks = pl.ds(k0, blk)
        kb = k_ref[ks, :]
        vb = v_ref[ks, :]
        one = jnp.ones_like(vb)
        vaug = (jnp.where(lo_lanes, vb, one), jnp.where(lo_lanes, one, vb))
        for hh in range(HEAD_PAIR):
            h = pair * HEAD_PAIR + hh
            bias = c_q0[hh] - head_row(crow_ref[:, ks], h)
            s = lax.dot_general(qh[hh], kb, nt, preferred_element_type=F32) + bias
            if masked:
                qpos = lax.broadcasted_iota(jnp.int32, (blk, blk), 0)
                kpos = lax.broadcasted_iota(jnp.int32, (blk, blk), 1)
                s = jnp.where(kpos <= qpos, s, NEG_BIG)
            m_prev = m_ref[hh]
            m_new = jnp.maximum(m_prev, jnp.max(s, axis=1, keepdims=True))
            p = jnp.exp(s - jnp.tile(m_new, (1, reps)))
            alpha = jnp.exp(m_prev - m_new)
            pv = jnp.dot(p.astype(BF16), vaug[hh], preferred_element_type=F32)
            acc_ref[hh] = alpha * acc_ref[hh] + pv
            m_ref[hh] = m_new

    step(q0, masked=True)
    m_min = [jnp.min(m_ref[hh]) for hh in range(HEAD_PAIR)]

    def live(j):
        jj = jnp.maximum(j, 0)
        prev = jnp.maximum(qi - 1, 0)
        keep = False
        for hh in range(HEAD_PAIR):
            base = (pair * HEAD_PAIR + hh) * nblk
            bound = (lim_ref[base + qi] + cend_ref[base + prev] - cend_ref[base + jj]) - m_min[hh]
            keep = jnp.logical_or(keep, jnp.logical_not(bound <= 0.0))
        return jnp.logical_and(j >= 0, keep)

    def body(j):
        step(pl.multiple_of(j * blk, blk), masked=False)
        return j - 1

    lax.while_loop(live, body, qi - 1)

    outs = []
    for hh in range(HEAD_PAIR):
        acc = acc_ref[hh]
        outs.append(acc / pltpu.roll(acc, FOX_DH, axis=1))
    o = jnp.where(lo_lanes, outs[0], outs[1])
    o_ref[...] = (o * gate_ref[...].astype(F32)).astype(BF16)


def _out_kernel(x_ref, gla_ref, fox_ref, mq_ref, mg_ref, mk_ref, mv_ref,
                wg_ref, wf_ref, wm_ref, fg_ref, o_ref):
    lane = lax.broadcasted_iota(jnp.int32, (1, LANES), 1)
    lo_lanes = lane < MEM_DH
    nt = (((1,), (1,)), ((), ()))
    mem_parts = []
    for p in range(MEM_HEADS // HEAD_PAIR):
        ls = slice(p * LANES, (p + 1) * LANES)
        q = mq_ref[:, ls]
        kb = mk_ref[:, ls]
        vb = mv_ref[:, ls]
        zero = jnp.zeros_like(q)
        one = jnp.ones_like(vb)
        qh = (jnp.where(lo_lanes, q, zero), jnp.where(lo_lanes, zero, q))
        vaug = (jnp.where(lo_lanes, vb, one), jnp.where(lo_lanes, one, vb))
        outs = []
        for hh in range(HEAD_PAIR):
            s = lax.dot_general(qh[hh], kb, nt, preferred_element_type=F32)
            pexp = jnp.exp(s - jnp.max(s, axis=1, keepdims=True))
            pv = jnp.dot(pexp.astype(BF16), vaug[hh], preferred_element_type=F32)
            outs.append(pv / pltpu.roll(pv, MEM_DH, axis=1))
        o = jnp.where(lo_lanes, outs[0], outs[1])
        mem_parts.append((o * mg_ref[:, ls].astype(F32)).astype(BF16))
    mem_mixed = jnp.concatenate(mem_parts, axis=1)
    y = (x_ref[...]
         + jnp.dot(gla_ref[...], wg_ref[...], preferred_element_type=F32)
         + jnp.dot(fox_ref[...], wf_ref[...], preferred_element_type=F32)
         + jnp.dot(mem_mixed, wm_ref[...], preferred_element_type=F32))
    o_ref[...] = y * _rms_scale(y, D_MODEL) * fg_ref[...]


def _pad_heads(w, heads, d, d_pad):
    lead = w.shape[:-1]
    w = w.reshape(lead + (heads, d))
    w = jnp.pad(w, [(0, 0)] * len(lead) + [(0, 0), (0, d_pad - d)])
    return w.reshape(lead + (heads * d_pad,))


def _pad_rows(w, heads, d, d_pad):
    return _pad_heads(w.T, heads, d, d_pad).T


def _layout_weights(w_in, w_alpha_up, b_alpha, b_forget, gla_norm_g, w_out):
    qk, gw = GLA_HEADS * GLA_DK, GLA_HEADS * GLA_DV
    sizes = (qk, qk, gw, GLA_RANK, gw, FOX_W, FOX_W, FOX_W, FOX_HEADS, FOX_W, MEM_W, MEM_W)
    parts = []
    o = 0
    for s in sizes:
        parts.append(w_in[:, o:o + s])
        o += s
    g_q, g_k, g_v, g_lr, g_gate, f_q, f_k, f_v, f_fg, f_gate, m_q, m_gate = parts
    small = jnp.zeros((D_MODEL, SMALL_W), F32)
    small = small.at[:, FG_LANE0:FG_LANE0 + FOX_HEADS].set(f_fg)
    small = small.at[:, LR_LANE0:LR_LANE0 + GLA_RANK].set(g_lr)
    w_all = jnp.concatenate([
        _pad_heads(g_q, GLA_HEADS, GLA_DK, GLA_DK_PAD),
        _pad_heads(g_k, GLA_HEADS, GLA_DK, GLA_DK_PAD),
        _pad_heads(g_v, GLA_HEADS, GLA_DV, GLA_DV_PAD),
        _pad_heads(g_gate, GLA_HEADS, GLA_DV, GLA_DV_PAD),
        f_q, f_k, f_v, f_gate, m_q, m_gate, small], axis=1).astype(BF16)
    wa = jnp.zeros((SMALL_W, GLA_QK_W), F32)
    wa = wa.at[LR_LANE0:LR_LANE0 + GLA_RANK, :].set(
        _pad_heads(w_alpha_up, GLA_HEADS, GLA_DK, GLA_DK_PAD)).astype(BF16)
    ba = _pad_heads(b_alpha[None, :], GLA_HEADS, GLA_DK, GLA_DK_PAD)
    bf = jnp.zeros((1, SMALL_W), F32).at[0, FG_LANE0:FG_LANE0 + FOX_HEADS].set(b_forget)
    ng = jnp.pad(gla_norm_g, (0, GLA_DV_PAD - GLA_DV))[None, :]
    wo_g = _pad_rows(w_out[:gw], GLA_HEADS, GLA_DV, GLA_DV_PAD).astype(BF16)
    wo_f = w_out[gw:gw + FOX_W].astype(BF16)
    wo_m = w_out[gw + FOX_W:].astype(BF16)
    return w_all, wa, ba, bf, ng, wo_g, wo_f, wo_m


def _fox_limits(qn2, kn2, crow, blk):
    nblk = crow.shape[1] // blk
    nproj = qn2.shape[0]
    assert nblk % nproj == 0
    qn = jnp.sqrt(qn2[:, 0, :FOX_HEADS])
    kmax = jnp.sqrt(jnp.max(kn2[:, 0, :FOX_HEADS], axis=0))
    lim = FOX_SKIP_NATS + NORM_SLACK * jnp.repeat(qn, nblk // nproj, axis=0) * kmax[None, :]
    cend = crow[:FOX_HEADS, blk - 1::blk]
    return lim.T.reshape(-1), cend.reshape(-1)


def _params(*sem):
    return pltpu.CompilerParams(dimension_semantics=sem, vmem_limit_bytes=VMEM_LIMIT)


def _layer(x, mem, norm_g, w_in, w_alpha_up, b_alpha, b_forget, gla_norm_g,
           mem_norm_g, w_mem_kv, w_out, out_g):
    T = x.shape[0]
    M = mem.shape[0]
    w_all, wa, ba, bf, ng, wo_g, wo_f, wo_m = _layout_weights(
        w_in, w_alpha_up, b_alpha, b_forget, gla_norm_g, w_out)

    def rows(width, n=PROJ_ROWS):
        return pl.BlockSpec((n, width), lambda i: (i, 0))

    def whole(shape):
        return pl.BlockSpec(shape, lambda i: (0,) * len(shape))

    bshape = lambda w: jax.ShapeDtypeStruct((T, w), BF16)
    nproj = T // PROJ_ROWS
    stat_spec = pl.BlockSpec((1, 1, LANES), lambda i: (i, 0, 0))
    stat_shape = jax.ShapeDtypeStruct((nproj, 1, LANES), F32)
    seg = (jnp.arange(FOX_W)[:, None] // FOX_DH == jnp.arange(LANES)[None, :]).astype(BF16)
    (gq, gk, gv, gg, fq, fk, fv, fgate, mq, mg, loga, crow, qn2, kn2) = pl.pallas_call(
        _proj_kernel,
        grid=(nproj,),
        in_specs=[rows(D_MODEL), whole((1, D_MODEL)), whole((D_MODEL, IN_COLS_PAD)),
                  whole((SMALL_W, GLA_QK_W)), whole((1, GLA_QK_W)), whole((1, SMALL_W)),
                  whole((FOX_W, LANES))],
        out_specs=[rows(GLA_QK_W), rows(GLA_QK_W), rows(GLA_V_W), rows(GLA_V_W),
                   rows(FOX_W), rows(FOX_W), rows(FOX_W), rows(FOX_W),
                   rows(MEM_W), rows(MEM_W), rows(GLA_QK_W),
                   pl.BlockSpec((SUBLANES, PROJ_ROWS), lambda i: (0, i)),
                   stat_spec, stat_spec],
        out_shape=[bshape(GLA_QK_W), bshape(GLA_QK_W), bshape(GLA_V_W), bshape(GLA_V_W),
                   bshape(FOX_W), bshape(FOX_W), bshape(FOX_W), bshape(FOX_W),
                   bshape(MEM_W), bshape(MEM_W),
                   jax.ShapeDtypeStruct((T, GLA_QK_W), F32),
                   jax.ShapeDtypeStruct((SUBLANES, T), F32),
                   stat_shape, stat_shape],
        scratch_shapes=[pltpu.VMEM((SUBLANES, LANES), F32)],
        compiler_params=_params("arbitrary"),
        name="proj",
    )(x, norm_g[None, :], w_all, wa, ba, bf, seg)

    mk, mv = pl.pallas_call(
        _memkv_kernel,
        out_shape=[jax.ShapeDtypeStruct((M, MEM_W), BF16)] * 2,
        compiler_params=pltpu.CompilerParams(vmem_limit_bytes=VMEM_LIMIT),
        name="memkv",
    )(mem, mem_norm_g[None, :], w_mem_kv.astype(BF16))

    gla = pl.pallas_call(
        _gla_kernel,
        grid=(T // GLA_ROWS,),
        in_specs=[rows(GLA_QK_W, GLA_ROWS), rows(GLA_QK_W, GLA_ROWS), rows(GLA_V_W, GLA_ROWS),
                  rows(GLA_QK_W, GLA_ROWS), rows(GLA_V_W, GLA_ROWS), whole((1, GLA_DV_PAD))],
        out_specs=rows(GLA_V_W, GLA_ROWS),
        out_shape=bshape(GLA_V_W),
        scratch_shapes=[pltpu.VMEM((GLA_HEADS, LANES, LANES), F32)],
        compiler_params=_params("arbitrary"),
        name="gla",
    )(gq, gk, gv, loga, gg, ng)

    lim, cend = _fox_limits(qn2, kn2, crow, FOX_BLOCK)
    pair_rows = pl.BlockSpec((FOX_BLOCK, LANES), lambda p, i, *_: (i, p))
    pair_all = pl.BlockSpec((T, LANES), lambda p, i, *_: (0, p))
    fox = pl.pallas_call(
        _fox_kernel,
        grid_spec=pltpu.PrefetchScalarGridSpec(
            num_scalar_prefetch=2,
            grid=(FOX_HEADS // HEAD_PAIR, T // FOX_BLOCK),
            in_specs=[pair_rows, pair_all, pair_all,
                      pl.BlockSpec((SUBLANES, T), lambda p, i, *_: (0, 0)), pair_rows],
            out_specs=pair_rows,
            scratch_shapes=[pltpu.VMEM((HEAD_PAIR, FOX_BLOCK, LANES), F32),
                            pltpu.VMEM((HEAD_PAIR, FOX_BLOCK, LANES), F32)]),
        out_shape=bshape(FOX_W),
        compiler_params=_params("arbitrary", "arbitrary"),
        name="fox",
    )(lim, cend, fq, fk, fv, crow, fgate)

    out = pl.pallas_call(
        _out_kernel,
        grid=(T // OUT_ROWS,),
        in_specs=[rows(D_MODEL, OUT_ROWS), rows(GLA_V_W, OUT_ROWS), rows(FOX_W, OUT_ROWS),
                  rows(MEM_W, OUT_ROWS), rows(MEM_W, OUT_ROWS),
                  whole((M, MEM_W)), whole((M, MEM_W)),
                  whole((GLA_V_W, D_MODEL)), whole((FOX_W, D_MODEL)), whole((MEM_W, D_MODEL)),
                  whole((1, D_MODEL))],
        out_specs=rows(D_MODEL, OUT_ROWS),
        out_shape=jax.ShapeDtypeStruct((T, D_MODEL), F32),
        compiler_params=_params("arbitrary"),
        name="out",
    )(x, gla, fox, mq, mg, mk, mv, wo_g, wo_f, wo_m, out_g[None, :])
    return out


def kernel(x, mem, norm_g, w_in, w_alpha_up, b_alpha, b_forget, gla_norm_g, mem_norm_g,
           w_mem_kv, w_out, final_norm_g):
    assert x.shape[0] == 1 and mem.shape[0] == 1 and norm_g.shape[0] == 1
    assert x.shape[1] % max(PROJ_ROWS, GLA_ROWS, FOX_BLOCK, OUT_ROWS) == 0
    out = _layer(x[0], mem[0], norm_g[0], w_in[0], w_alpha_up[0], b_alpha[0], b_forget[0],
                 gla_norm_g[0], mem_norm_g[0], w_mem_kv[0], w_out[0], final_norm_g)
    return out[None]
```

```python
import functools

import jax
import jax.numpy as jnp
from jax import lax
from jax.experimental import pallas as pl
from jax.experimental.pallas import tpu as pltpu

F32 = jnp.float32
BF16 = jnp.bfloat16

EPS = 1e-6
LANES = 128
SUBLANES = 8

D_MODEL = 1024
GLA_HEADS, GLA_DK, GLA_DV, GLA_RANK = 4, 48, 96, 16
GLA_DK_PAD = 64
GLA_DV_PAD = LANES
GLA_GATE_NORM = 16.0
GLA_CHUNK = 64
FOX_HEADS, FOX_DH = 6, 64
MEM_HEADS, MEM_DH = 4, 64
HEAD_PAIR = 2
GLA_QK_W = GLA_HEADS * GLA_DK_PAD
GLA_V_W = GLA_HEADS * GLA_DV_PAD
FOX_W = FOX_HEADS * FOX_DH
MEM_W = MEM_HEADS * MEM_DH
SMALL_W = LANES
FG_LANE0 = 0
LR_LANE0 = SUBLANES

_GROUPS = (("gq", GLA_QK_W), ("gk", GLA_QK_W), ("gv", GLA_V_W), ("gg", GLA_V_W),
           ("fq", FOX_W), ("fk", FOX_W), ("fv", FOX_W), ("fgate", FOX_W),
           ("mq", MEM_W), ("mg", MEM_W), ("small", SMALL_W))
_OFF = {}
_o = 0
for _n, _w in _GROUPS:
    _OFF[_n] = (_o, _o + _w)
    _o += _w
IN_COLS_PAD = _o

PROJ_ROWS = 512
GLA_ROWS = 512
FOX_BLOCK = 512
OUT_ROWS = 512
VMEM_LIMIT = 56 * 1024 * 1024

NEG_BIG = -1e30
FOX_SKIP_NATS = 105.0
NORM_SLACK = 1.02


def _log_sigmoid(z):
    return jnp.minimum(z, 0.0) - jnp.log(1.0 + jnp.exp(-jnp.abs(z)))


def _silu(z):
    return z / (1.0 + jnp.exp(-z))


def _rms_scale(v, width):
    return lax.rsqrt(jnp.sum(v * v, axis=-1, keepdims=True) * (1.0 / width) + EPS)


def _proj_kernel(x_ref, g_ref, w_ref, wa_ref, ba_ref, bf_ref, seg_ref,
                 gq_ref, gk_ref, gv_ref, gg_ref, fq_ref, fk_ref, fv_ref, fgate_ref,
                 mq_ref, mg_ref, loga_ref, crow_ref, qn2_ref, kn2_ref, carry_ref):
    rows = x_ref.shape[0]

    @pl.when(pl.program_id(0) == 0)
    def _():
        carry_ref[...] = jnp.zeros_like(carry_ref)

    x = x_ref[...]
    xn = (x * _rms_scale(x, D_MODEL) * g_ref[...]).astype(BF16)

    def proj(first, last):
        lo, hi = _OFF[first][0], _OFF[last][1]
        y = jnp.dot(xn, w_ref[:, lo:hi], preferred_element_type=F32)
        return lambda name: y[:, _OFF[name][0] - lo:_OFF[name][1] - lo]

    tail = proj("mq", "small")
    small = tail("small")
    logf = _log_sigmoid(small + bf_ref[...])
    c = logf.T[0:SUBLANES, :]
    lane = lax.broadcasted_iota(jnp.int32, c.shape, 1)
    shift = 1
    while shift < rows:
        c = c + jnp.where(lane >= shift, pltpu.roll(c, shift, axis=1), 0.0)
        shift *= 2
    c = c + carry_ref[:, 0:1]
    crow_ref[...] = c
    carry_ref[...] = jnp.broadcast_to(c[:, rows - 1:rows], carry_ref.shape)

    z = jnp.dot(small.astype(BF16), wa_ref[...], preferred_element_type=F32) + ba_ref[...]
    loga_ref[...] = _log_sigmoid(z) * (1.0 / GLA_GATE_NORM)
    mq_ref[...] = (tail("mq") * MEM_DH ** -0.5).astype(BF16)
    mg_ref[...] = _silu(tail("mg")).astype(BF16)

    fox = proj("fq", "fgate")
    fq = (fox("fq") * FOX_DH ** -0.5).astype(BF16)
    fk = fox("fk").astype(BF16)
    fq_ref[...] = fq
    fk_ref[...] = fox("fk").T.astype(BF16)
    fv_ref[...] = fox("fv").astype(BF16)
    fgate_ref[...] = _silu(fox("fgate")).astype(BF16)

    def max_sq_norm(v):
        v32 = v.astype(F32)
        n2 = jnp.dot((v32 * v32).astype(BF16), seg_ref[...], preferred_element_type=F32)
        return jnp.max(n2, axis=0, keepdims=True)

    qn2_ref[0] = max_sq_norm(fq)
    kn2_ref[0] = max_sq_norm(fk)

    gla = proj("gq", "gg")
    gq_ref[...] = gla("gq").astype(BF16)
    gk_ref[...] = gla("gk").astype(BF16)
    gv_ref[...] = gla("gv").astype(BF16)
    gg_ref[...] = _silu(gla("gg")).astype(BF16)


def _memkv_kernel(mem_ref, g_ref, w_ref, mk_ref, mv_ref):
    m = mem_ref[...]
    mn = (m * _rms_scale(m, D_MODEL) * g_ref[...]).astype(BF16)
    kv = jnp.dot(mn, w_ref[...], preferred_element_type=F32)
    mk_ref[...] = kv[:, :MEM_W].astype(BF16)
    mv_ref[...] = kv[:, MEM_W:].astype(BF16)


def _gla_kernel(q_ref, k_ref, v_ref, loga_ref, gate_ref, ng_ref, o_ref, s_ref):
    C = GLA_CHUNK
    n_chunks = q_ref.shape[0] // C

    @pl.when(pl.program_id(0) == 0)
    def _():
        s_ref[...] = jnp.zeros_like(s_ref)

    row = lax.broadcasted_iota(jnp.int32, (C, LANES), 0)
    lane = lax.broadcasted_iota(jnp.int32, (C, LANES), 1)
    causal = (lax.broadcasted_iota(jnp.int32, (C, C), 0)
              >= lax.broadcasted_iota(jnp.int32, (C, C), 1))
    sq_row = lax.broadcasted_iota(jnp.int32, (LANES, LANES), 0)
    sq_lane = lax.broadcasted_iota(jnp.int32, (LANES, LANES), 1)
    eye = sq_row == sq_lane
    scale = GLA_DK ** -0.5
    nt = (((1,), (1,)), ((), ()))
    tn = (((0,), (0,)), ((), ()))

    def chunk(ci, carry):
        r0 = pl.multiple_of(ci * C, C)
        rs = pl.ds(r0, C)
        for p in range(GLA_HEADS // HEAD_PAIR):
            ls = slice(p * LANES, (p + 1) * LANES)
            b = loga_ref[rs, ls]
            shift = 1
            while shift < C:
                b = b + jnp.where(row >= shift, pltpu.roll(b, shift, axis=0), 0.0)
                shift *= 2
            b_last = b[C - 1:C, :]
            q2 = q_ref[rs, ls].astype(F32) * scale
            k2 = k_ref[rs, ls].astype(F32)
            qd = q2 * jnp.exp(b)
            kd = (k2 * jnp.exp(-b)).astype(BF16)
            ke = (k2 * jnp.exp(b_last - b)).astype(BF16)
            dcol = jnp.exp(jnp.sum(jnp.where(eye, jnp.broadcast_to(b_last, (LANES, LANES)), 0.0),
                                   axis=1, keepdims=True))
            for hh in range(HEAD_PAIR):
                h = p * HEAD_PAIR + hh
                own_lane = (lane < GLA_DK_PAD) if hh == 0 else (lane >= GLA_DK_PAD)
                own_row = (sq_row < GLA_DK_PAD) if hh == 0 else (sq_row >= GLA_DK_PAD)
                qh = jnp.where(own_lane, qd, 0.0).astype(BF16)
                attn = lax.dot_general(qh, kd, nt, preferred_element_type=F32)
                attn = jnp.where(causal, attn, 0.0).astype(BF16)
                vs = slice(h * GLA_DV_PAD, (h + 1) * GLA_DV_PAD)
                vh = v_ref[rs, vs]
                s_prev = s_ref[h]
                o = (jnp.dot(attn, vh, preferred_element_type=F32)
                     + jnp.dot(qh, s_prev.astype(BF16), preferred_element_type=F32))
                kv = lax.dot_general(ke, vh, tn, preferred_element_type=F32)
                s_ref[h] = dcol * s_prev + jnp.where(own_row, kv, 0.0)
                on = o * _rms_scale(o, GLA_DV) * ng_ref[...]
                o_ref[rs, vs] = (on * gate_ref[rs, vs].astype(F32)).astype(BF16)
        return carry

    lax.fori_loop(0, n_chunks, chunk, 0)


def _fox_kernel(lim_ref, cend_ref, q_ref, k_ref, v_ref, crow_ref, gate_ref, o_ref, m_ref, acc_ref):
    blk = q_ref.shape[0]
    pair = pl.program_id(0)
    qi = pl.program_id(1)
    nblk = pl.num_programs(1)
    q0 = pl.multiple_of(qi * blk, blk)
    lane = lax.broadcasted_iota(jnp.int32, (1, LANES), 1)
    lo_lanes = lane < FOX_DH
    nt = (((1,), (1,)), ((), ()))
    reps = blk // LANES

    q = q_ref[...]
    zero = jnp.zeros_like(q)
    q_stack = jnp.concatenate([jnp.where(lo_lanes, q, zero), jnp.where(lo_lanes, zero, q)], axis=0)
    m_ref[...] = jnp.full_like(m_ref, NEG_BIG)
    acc_ref[...] = jnp.zeros_like(acc_ref)

    def head_row(c8, h):
        sub = lax.broadcasted_iota(jnp.int32, c8.shape, 0)
        return jnp.sum(jnp.where(sub == h, c8, 0.0), axis=0, keepdims=True)

    c_q0 = [head_row(crow_ref[:, pl.ds(q0, LANES)], pair * HEAD_PAIR + hh)[:, 0:1]
            for hh in range(HEAD_PAIR)]

    def step(k0, masked):
        ks = pl.ds(k0, blk)
        kb = k_ref[:, ks]
        vb = v_ref[ks, :]
        one = jnp.ones_like(vb)
        vaug = (jnp.where(lo_lanes, vb, one), jnp.where(lo_lanes, one, vb))
        s_all = jnp.dot(q_stack, kb, preferred_element_type=F32)
        for hh in range(HEAD_PAIR):
            h = pair * HEAD_PAIR + hh
            bias = c_q0[hh] - head_row(crow_ref[:, ks], h)
            s = s_all[hh * blk:(hh + 1) * blk] + bias
            if masked:
                qpos = lax.broadcasted_iota(jnp.int32, (blk, blk), 0)
                kpos = lax.broadcasted_iota(jnp.int32, (blk, blk), 1)
                s = jnp.where(kpos <= qpos, s, NEG_BIG)
            m_prev = m_ref[hh]
            m_new = jnp.maximum(m_prev, jnp.max(s, axis=1, keepdims=True))
            p = jnp.exp(s - jnp.tile(m_new, (1, reps)))
            alpha = jnp.exp(m_prev - m_new)
            pv = jnp.dot(p.astype(BF16), vaug[hh], preferred_element_type=F32)
            acc_ref[hh] = alpha * acc_ref[hh] + pv
            m_ref[hh] = m_new

    @pl.when(qi == 0)
    def _():
        step(q0, masked=True)

    @pl.when(qi > 0)
    def _():
        step(q0, masked=True)
        step(q0 - blk, masked=False)

    m_min = [jnp.min(m_ref[hh]) for hh in range(HEAD_PAIR)]

    def live(j):
        jj = jnp.maximum(j, 0)
        prev = jnp.maximum(qi - 1, 0)
        keep = False
        for hh in range(HEAD_PAIR):
            base = (pair * HEAD_PAIR + hh) * nblk
            bound = (lim_ref[base + qi] + cend_ref[base + prev] - cend_ref[base + jj]) - m_min[hh]
            keep = jnp.logical_or(keep, jnp.logical_not(bound <= 0.0))
        return jnp.logical_and(j >= 0, keep)

    def body(j):
        step(pl.multiple_of(j * blk, blk), masked=False)
        step(pl.multiple_of((j - 1) * blk, blk), masked=False)
        return j - 2

    j_rest = lax.while_loop(lambda j: live(j - 1), body, qi - 2)

    @pl.when(live(j_rest))
    def _():
        step(pl.multiple_of(jnp.maximum(j_rest, 0) * blk, blk), masked=False)

    outs = []
    for hh in range(HEAD_PAIR):
        acc = acc_ref[hh]
        outs.append(acc / pltpu.roll(acc, FOX_DH, axis=1))
    o = jnp.where(lo_lanes, outs[0], outs[1])
    o_ref[...] = (o * gate_ref[...].astype(F32)).astype(BF16)


def _out_kernel(x_ref, gla_ref, fox_ref, mq_ref, mg_ref, mk_ref, mv_ref,
                wg_ref, wf_ref, wm_ref, fg_ref, o_ref):
    lane = lax.broadcasted_iota(jnp.int32, (1, LANES), 1)
    lo_lanes = lane < MEM_DH
    nt = (((1,), (1,)), ((), ()))
    mem_parts = []
    for p in range(MEM_HEADS // HEAD_PAIR):
        ls = slice(p * LANES, (p + 1) * LANES)
        q = mq_ref[:, ls]
        kb = mk_ref[:, ls]
        vb = mv_ref[:, ls]
        zero = jnp.zeros_like(q)
        one = jnp.ones_like(vb)
        qh = (jnp.where(lo_lanes, q, zero), jnp.where(lo_lanes, zero, q))
        vaug = (jnp.where(lo_lanes, vb, one), jnp.where(lo_lanes, one, vb))
        outs = []
        for hh in range(HEAD_PAIR):
            s = lax.dot_general(qh[hh], kb, nt, preferred_element_type=F32)
            pexp = jnp.exp(s - jnp.max(s, axis=1, keepdims=True))
            pv = jnp.dot(pexp.astype(BF16), vaug[hh], preferred_element_type=F32)
            outs.append(pv / pltpu.roll(pv, MEM_DH, axis=1))
        o = jnp.where(lo_lanes, outs[0], outs[1])
        mem_parts.append((o * mg_ref[:, ls].astype(F32)).astype(BF16))
    mem_mixed = jnp.concatenate(mem_parts, axis=1)
    y = (x_ref[...]
         + jnp.dot(gla_ref[...], wg_ref[...], preferred_element_type=F32)
         + jnp.dot(fox_ref[...], wf_ref[...], preferred_element_type=F32)
         + jnp.dot(mem_mixed, wm_ref[...], preferred_element_type=F32))
    o_ref[...] = y * _rms_scale(y, D_MODEL) * fg_ref[...]


def _pad_heads(w, heads, d, d_pad):
    lead = w.shape[:-1]
    w = w.reshape(lead + (heads, d))
    w = jnp.pad(w, [(0, 0)] * len(lead) + [(0, 0), (0, d_pad - d)])
    return w.reshape(lead + (heads * d_pad,))


def _pad_rows(w, heads, d, d_pad):
    return _pad_heads(w.T, heads, d, d_pad).T


def _layout_weights(w_in, w_alpha_up, b_alpha, b_forget, gla_norm_g, w_out):
    qk, gw = GLA_HEADS * GLA_DK, GLA_HEADS * GLA_DV
    sizes = (qk, qk, gw, GLA_RANK, gw, FOX_W, FOX_W, FOX_W, FOX_HEADS, FOX_W, MEM_W, MEM_W)
    parts = []
    o = 0
    for s in sizes:
        parts.append(w_in[:, o:o + s])
        o += s
    g_q, g_k, g_v, g_lr, g_gate, f_q, f_k, f_v, f_fg, f_gate, m_q, m_gate = parts
    small = jnp.zeros((D_MODEL, SMALL_W), F32)
    small = small.at[:, FG_LANE0:FG_LANE0 + FOX_HEADS].set(f_fg)
    small = small.at[:, LR_LANE0:LR_LANE0 + GLA_RANK].set(g_lr)
    w_all = jnp.concatenate([
        _pad_heads(g_q, GLA_HEADS, GLA_DK, GLA_DK_PAD),
        _pad_heads(g_k, GLA_HEADS, GLA_DK, GLA_DK_PAD),
        _pad_heads(g_v, GLA_HEADS, GLA_DV, GLA_DV_PAD),
        _pad_heads(g_gate, GLA_HEADS, GLA_DV, GLA_DV_PAD),
        f_q, f_k, f_v, f_gate, m_q, m_gate, small], axis=1).astype(BF16)
    wa = jnp.zeros((SMALL_W, GLA_QK_W), F32)
    wa = wa.at[LR_LANE0:LR_LANE0 + GLA_RANK, :].set(
        _pad_heads(w_alpha_up, GLA_HEADS, GLA_DK, GLA_DK_PAD)).astype(BF16)
    ba = _pad_heads(b_alpha[None, :], GLA_HEADS, GLA_DK, GLA_DK_PAD)
    bf = jnp.zeros((1, SMALL_W), F32).at[0, FG_LANE0:FG_LANE0 + FOX_HEADS].set(b_forget)
    ng = jnp.pad(gla_norm_g, (0, GLA_DV_PAD - GLA_DV))[None, :]
    wo_g = _pad_rows(w_out[:gw], GLA_HEADS, GLA_DV, GLA_DV_PAD).astype(BF16)
    wo_f = w_out[gw:gw + FOX_W].astype(BF16)
    wo_m = w_out[gw + FOX_W:].astype(BF16)
    return w_all, wa, ba, bf, ng, wo_g, wo_f, wo_m


def _fox_limits(qn2, kn2, crow, blk):
    nblk = crow.shape[1] // blk
    nproj = qn2.shape[0]
    assert nblk % nproj == 0
    qn = jnp.sqrt(qn2[:, 0, :FOX_HEADS])
    kmax = jnp.sqrt(jnp.max(kn2[:, 0, :FOX_HEADS], axis=0))
    lim = FOX_SKIP_NATS + NORM_SLACK * jnp.repeat(qn, nblk // nproj, axis=0) * kmax[None, :]
    cend = crow[:FOX_HEADS, blk - 1::blk]
    return lim.T.reshape(-1), cend.reshape(-1)


def _params(*sem):
    return pltpu.CompilerParams(dimension_semantics=sem, vmem_limit_bytes=VMEM_LIMIT)


def _layer(x, mem, norm_g, w_in, w_alpha_up, b_alpha, b_forget, gla_norm_g,
           mem_norm_g, w_mem_kv, w_out, out_g):
    T = x.shape[0]
    M = mem.shape[0]
    w_all, wa, ba, bf, ng, wo_g, wo_f, wo_m = _layout_weights(
        w_in, w_alpha_up, b_alpha, b_forget, gla_norm_g, w_out)

    def rows(width, n=PROJ_ROWS):
        return pl.BlockSpec((n, width), lambda i: (i, 0))

    def whole(shape):
        return pl.BlockSpec(shape, lambda i: (0,) * len(shape))

    bshape = lambda w: jax.ShapeDtypeStruct((T, w), BF16)
    nproj = T // PROJ_ROWS
    stat_spec = pl.BlockSpec((1, 1, LANES), lambda i: (i, 0, 0))
    stat_shape = jax.ShapeDtypeStruct((nproj, 1, LANES), F32)
    seg = (jnp.arange(FOX_W)[:, None] // FOX_DH == jnp.arange(LANES)[None, :]).astype(BF16)
    (gq, gk, gv, gg, fq, fk, fv, fgate, mq, mg, loga, crow, qn2, kn2) = pl.pallas_call(
        _proj_kernel,
        grid=(nproj,),
        in_specs=[rows(D_MODEL), whole((1, D_MODEL)), whole((D_MODEL, IN_COLS_PAD)),
                  whole((SMALL_W, GLA_QK_W)), whole((1, GLA_QK_W)), whole((1, SMALL_W)),
                  whole((FOX_W, LANES))],
        out_specs=[rows(GLA_QK_W), rows(GLA_QK_W), rows(GLA_V_W), rows(GLA_V_W),
                   rows(FOX_W), pl.BlockSpec((FOX_W, PROJ_ROWS), lambda i: (0, i)),
                   rows(FOX_W), rows(FOX_W),
                   rows(MEM_W), rows(MEM_W), rows(GLA_QK_W),
                   pl.BlockSpec((SUBLANES, PROJ_ROWS), lambda i: (0, i)),
                   stat_spec, stat_spec],
        out_shape=[bshape(GLA_QK_W), bshape(GLA_QK_W), bshape(GLA_V_W), bshape(GLA_V_W),
                   bshape(FOX_W), jax.ShapeDtypeStruct((FOX_W, T), BF16),
                   bshape(FOX_W), bshape(FOX_W),
                   bshape(MEM_W), bshape(MEM_W),
                   jax.ShapeDtypeStruct((T, GLA_QK_W), F32),
                   jax.ShapeDtypeStruct((SUBLANES, T), F32),
                   stat_shape, stat_shape],
        scratch_shapes=[pltpu.VMEM((SUBLANES, LANES), F32)],
        compiler_params=_params("arbitrary"),
        name="proj",
    )(x, norm_g[None, :], w_all, wa, ba, bf, seg)

    mk, mv = pl.pallas_call(
        _memkv_kernel,
        out_shape=[jax.ShapeDtypeStruct((M, MEM_W), BF16)] * 2,
        compiler_params=pltpu.CompilerParams(vmem_limit_bytes=VMEM_LIMIT),
        name="memkv",
    )(mem, mem_norm_g[None, :], w_mem_kv.astype(BF16))

    gla = pl.pallas_call(
        _gla_kernel,
        grid=(T // GLA_ROWS,),
        in_specs=[rows(GLA_QK_W, GLA_ROWS), rows(GLA_QK_W, GLA_ROWS), rows(GLA_V_W, GLA_ROWS),
                  rows(GLA_QK_W, GLA_ROWS), rows(GLA_V_W, GLA_ROWS), whole((1, GLA_DV_PAD))],
        out_specs=rows(GLA_V_W, GLA_ROWS),
        out_shape=bshape(GLA_V_W),
        scratch_shapes=[pltpu.VMEM((GLA_HEADS, LANES, LANES), F32)],
        compiler_params=_params("arbitrary"),
        name="gla",
    )(gq, gk, gv, loga, gg, ng)

    lim, cend = _fox_limits(qn2, kn2, crow, FOX_BLOCK)
    pair_rows = pl.BlockSpec((FOX_BLOCK, LANES), lambda p, i, *_: (i, p))
    pair_all = pl.BlockSpec((T, LANES), lambda p, i, *_: (0, p))
    fox = pl.pallas_call(
        _fox_kernel,
        grid_spec=pltpu.PrefetchScalarGridSpec(
            num_scalar_prefetch=2,
            grid=(FOX_HEADS // HEAD_PAIR, T // FOX_BLOCK),
            in_specs=[pair_rows, pl.BlockSpec((LANES, T), lambda p, i, *_: (p, 0)), pair_all,
                      pl.BlockSpec((SUBLANES, T), lambda p, i, *_: (0, 0)), pair_rows],
            out_specs=pair_rows,
            scratch_shapes=[pltpu.VMEM((HEAD_PAIR, FOX_BLOCK, LANES), F32),
                            pltpu.VMEM((HEAD_PAIR, FOX_BLOCK, LANES), F32)]),
        out_shape=bshape(FOX_W),
        compiler_params=_params("arbitrary", "arbitrary"),
        name="fox",
    )(lim, cend, fq, fk, fv, crow, fgate)

    out = pl.pallas_call(
        _out_kernel,
        grid=(T // OUT_ROWS,),
        in_specs=[rows(D_MODEL, OUT_ROWS), rows(GLA_V_W, OUT_ROWS), rows(FOX_W, OUT_ROWS),
                  rows(MEM_W, OUT_ROWS), rows(MEM_W, OUT_ROWS),
                  whole((M, MEM_W)), whole((M, MEM_W)),
                  whole((GLA_V_W, D_MODEL)), whole((FOX_W, D_MODEL)), whole((MEM_W, D_MODEL)),
                  whole((1, D_MODEL))],
        out_specs=rows(D_MODEL, OUT_ROWS),
        out_shape=jax.ShapeDtypeStruct((T, D_MODEL), F32),
        compiler_params=_params("arbitrary"),
        name="out",
    )(x, gla, fox, mq, mg, mk, mv, wo_g, wo_f, wo_m, out_g[None, :])
    return out


def kernel(x, mem, norm_g, w_in, w_alpha_up, b_alpha, b_forget, gla_norm_g, mem_norm_g,
           w_mem_kv, w_out, final_norm_g):
    assert x.shape[0] == 1 and mem.shape[0] == 1 and norm_g.shape[0] == 1
    assert x.shape[1] % max(PROJ_ROWS, GLA_ROWS, FOX_BLOCK, OUT_ROWS) == 0
    out = _layer(x[0], mem[0], norm_g[0], w_in[0], w_alpha_up[0], b_alpha[0], b_forget[0],
                 gla_norm_g[0], mem_norm_g[0], w_mem_kv[0], w_out[0], final_norm_g)
    return out[None]
```

```python
import functools

import jax
import jax.numpy as jnp
from jax import lax
from jax.experimental import pallas as pl
from jax.experimental.pallas import tpu as pltpu

F32 = jnp.float32
BF16 = jnp.bfloat16

EPS = 1e-6
LANES = 128
SUBLANES = 8

D_MODEL = 1024
GLA_HEADS, GLA_DK, GLA_DV, GLA_RANK = 4, 48, 96, 16
GLA_DK_PAD = 64
GLA_DV_PAD = LANES
GLA_GATE_NORM = 16.0
GLA_CHUNK = 64
FOX_HEADS, FOX_DH = 6, 64
MEM_HEADS, MEM_DH = 4, 64
HEAD_PAIR = 2
GLA_QK_W = GLA_HEADS * GLA_DK_PAD
GLA_V_W = GLA_HEADS * GLA_DV_PAD
FOX_W = FOX_HEADS * FOX_DH
MEM_W = MEM_HEADS * MEM_DH
SMALL_W = LANES
FG_LANE0 = 0
LR_LANE0 = SUBLANES

_GROUPS = (("gq", GLA_QK_W), ("gk", GLA_QK_W), ("gv", GLA_V_W), ("gg", GLA_V_W),
           ("fq", FOX_W), ("fk", FOX_W), ("fv", FOX_W), ("fgate", FOX_W),
           ("mq", MEM_W), ("mg", MEM_W), ("small", SMALL_W))
_OFF = {}
_o = 0
for _n, _w in _GROUPS:
    _OFF[_n] = (_o, _o + _w)
    _o += _w
IN_COLS_PAD = _o

PROJ_ROWS = 512
GLA_ROWS = 512
GLA_UNROLL = 8
FOX_BLOCK = 512
OUT_ROWS = 512
VMEM_LIMIT = 56 * 1024 * 1024

NEG_BIG = -1e30
FOX_SKIP_NATS = 105.0
NORM_SLACK = 1.02


def _log_sigmoid(z):
    return jnp.minimum(z, 0.0) - jnp.log(1.0 + jnp.exp(-jnp.abs(z)))


def _silu(z):
    return z / (1.0 + jnp.exp(-z))


def _rms_scale(v, width):
    return lax.rsqrt(jnp.sum(v * v, axis=-1, keepdims=True) * (1.0 / width) + EPS)


def _proj_kernel(x_ref, g_ref, w_ref, wa_ref, ba_ref, bf_ref, seg_ref,
                 gq_ref, gk_ref, gv_ref, gg_ref, fq_ref, fk_ref, fv_ref, fgate_ref,
                 mq_ref, mg_ref, loga_ref, crow_ref, qn2_ref, kn2_ref, carry_ref):
    rows = x_ref.shape[0]

    @pl.when(pl.program_id(0) == 0)
    def _():
        carry_ref[...] = jnp.zeros_like(carry_ref)

    x = x_ref[...]
    xn = (x * _rms_scale(x, D_MODEL) * g_ref[...]).astype(BF16)

    def proj(first, last):
        lo, hi = _OFF[first][0], _OFF[last][1]
        y = jnp.dot(xn, w_ref[:, lo:hi], preferred_element_type=F32)
        return lambda name: y[:, _OFF[name][0] - lo:_OFF[name][1] - lo]

    tail = proj("mq", "small")
    small = tail("small")
    logf = _log_sigmoid(small + bf_ref[...])
    c = logf.T[0:SUBLANES, :]
    lane = lax.broadcasted_iota(jnp.int32, c.shape, 1)
    shift = 1
    while shift < rows:
        c = c + jnp.where(lane >= shift, pltpu.roll(c, shift, axis=1), 0.0)
        shift *= 2
    c = c + carry_ref[:, 0:1]
    crow_ref[...] = c
    carry_ref[...] = jnp.broadcast_to(c[:, rows - 1:rows], carry_ref.shape)

    z = jnp.dot(small.astype(BF16), wa_ref[...], preferred_element_type=F32) + ba_ref[...]
    loga_ref[...] = _log_sigmoid(z) * (1.0 / GLA_GATE_NORM)
    mq_ref[...] = (tail("mq") * MEM_DH ** -0.5).astype(BF16)
    mg_ref[...] = _silu(tail("mg")).astype(BF16)

    fox = proj("fq", "fgate")
    fq = (fox("fq") * FOX_DH ** -0.5).astype(BF16)
    fk = fox("fk").astype(BF16)
    fq_ref[...] = fq
    fk_ref[...] = fox("fk").T.astype(BF16)
    fv_ref[...] = fox("fv").astype(BF16)
    fgate_ref[...] = _silu(fox("fgate")).astype(BF16)

    def max_sq_norm(v):
        v32 = v.astype(F32)
        n2 = jnp.dot((v32 * v32).astype(BF16), seg_ref[...], preferred_element_type=F32)
        return jnp.max(n2, axis=0, keepdims=True)

    qn2_ref[0] = max_sq_norm(fq)
    kn2_ref[0] = max_sq_norm(fk)

    gla = proj("gq", "gg")
    gq_ref[...] = gla("gq").astype(BF16)
    gk_ref[...] = gla("gk").astype(BF16)
    gv_ref[...] = gla("gv").astype(BF16)
    gg_ref[...] = _silu(gla("gg")).astype(BF16)


def _memkv_kernel(mem_ref, g_ref, w_ref, mk_ref, mv_ref):
    m = mem_ref[...]
    mn = (m * _rms_scale(m, D_MODEL) * g_ref[...]).astype(BF16)
    kv = jnp.dot(mn, w_ref[...], preferred_element_type=F32)
    mk_ref[...] = kv[:, :MEM_W].astype(BF16)
    mv_ref[...] = kv[:, MEM_W:].astype(BF16)


def _gla_kernel(q_ref, k_ref, v_ref, loga_ref, gate_ref, ng_ref, o_ref,
                s_ref, lhs_ref, kv_ref, dec_ref, sprev_ref):
    C = GLA_CHUNK
    W = HEAD_PAIR * GLA_DV_PAD
    n_groups = q_ref.shape[0] // (C * GLA_UNROLL)

    @pl.when(pl.program_id(0) == 0)
    def _():
        s_ref[...] = jnp.zeros_like(s_ref)

    row = lax.broadcasted_iota(jnp.int32, (C, LANES), 0)
    lane = lax.broadcasted_iota(jnp.int32, (C, LANES), 1)
    lo_k = lane < GLA_DK_PAD
    causal = row >= jnp.where(lo_k, lane, lane - GLA_DK_PAD)
    lo_v = lax.broadcasted_iota(jnp.int32, (C, W), 1) < GLA_DV_PAD
    st_row = lax.broadcasted_iota(jnp.int32, (LANES, W), 0)
    st_lane = lax.broadcasted_iota(jnp.int32, (LANES, W), 1)
    own = (st_row < GLA_DK_PAD) == (st_lane < GLA_DV_PAD)
    eye = (lax.broadcasted_iota(jnp.int32, (LANES, LANES), 0)
           == lax.broadcasted_iota(jnp.int32, (LANES, LANES), 1))
    scale = GLA_DK ** -0.5
    nt = (((1,), (1,)), ((), ()))
    tn = (((0,), (0,)), ((), ()))
    ng = jnp.concatenate([ng_ref[...]] * HEAD_PAIR, axis=1)

    pairs = range(GLA_HEADS // HEAD_PAIR)

    def local(ci):
        rs = pl.ds(pl.multiple_of(ci * C, C), C)
        for p in pairs:
            ls = slice(p * LANES, (p + 1) * LANES)
            vs = slice(p * W, (p + 1) * W)
            b = loga_ref[rs, ls]
            shift = 1
            while shift < C:
                b = b + jnp.where(row >= shift, pltpu.roll(b, shift, axis=0), 0.0)
                shift *= 2
            b_last = b[C - 1:C, :]
            k2 = k_ref[rs, ls].astype(F32)
            qd = (q_ref[rs, ls].astype(F32) * scale * jnp.exp(b)).astype(BF16)
            kd = (k2 * jnp.exp(-b)).astype(BF16)
            ke = (k2 * jnp.exp(b_last - b)).astype(BF16)
            zk = jnp.zeros_like(kd)
            kd_blk = jnp.concatenate([jnp.where(lo_k, kd, zk), jnp.where(lo_k, zk, kd)], axis=0)
            attn = lax.dot_general(qd, kd_blk, nt, preferred_element_type=F32)
            lhs_ref[rs, vs] = jnp.concatenate([jnp.where(causal, attn, 0.0).astype(BF16), qd], axis=1)
            kv = lax.dot_general(ke, v_ref[rs, vs], tn, preferred_element_type=F32)
            kv_ref[p, ci] = jnp.where(own, kv, 0.0)
            dcol = jnp.exp(jnp.sum(jnp.where(eye, jnp.broadcast_to(b_last, (LANES, LANES)), 0.0),
                                   axis=1, keepdims=True))
            dec_ref[p, ci] = jnp.broadcast_to(dcol, (LANES, LANES))

    def scan(ci, carry):
        for p in pairs:
            s_prev = s_ref[p]
            sprev_ref[p, ci] = s_prev.astype(BF16)
            s_ref[p] = jnp.tile(dec_ref[p, ci], (1, HEAD_PAIR)) * s_prev + kv_ref[p, ci]
        return carry

    def output(ci):
        rs = pl.ds(pl.multiple_of(ci * C, C), C)
        for p in pairs:
            vs = slice(p * W, (p + 1) * W)
            v2 = v_ref[rs, vs]
            zv = jnp.zeros_like(v2)
            v_blk = jnp.concatenate([jnp.where(lo_v, v2, zv), jnp.where(lo_v, zv, v2)], axis=0)
            o = jnp.dot(lhs_ref[rs, vs], jnp.concatenate([v_blk, sprev_ref[p, ci]], axis=0),
                        preferred_element_type=F32)
            o2 = o * o
            ms = jnp.where(lo_v, jnp.sum(o2[:, :GLA_DV_PAD], axis=1, keepdims=True),
                           jnp.sum(o2[:, GLA_DV_PAD:], axis=1, keepdims=True))
            on = o * lax.rsqrt(ms * (1.0 / GLA_DV) + EPS) * ng
            o_ref[rs, vs] = (on * gate_ref[rs, vs].astype(F32)).astype(BF16)

    def unrolled(fn):
        def group(gi, carry):
            for u in range(GLA_UNROLL):
                fn(gi * GLA_UNROLL + u)
            return carry
        return group

    lax.fori_loop(0, n_groups, unrolled(local), 0)
    lax.fori_loop(0, n_groups * GLA_UNROLL, scan, 0)
    lax.fori_loop(0, n_groups, unrolled(output), 0)


def _fox_kernel(lim_ref, cend_ref, q_ref, k_ref, v_ref, crow_ref, gate_ref, o_ref, m_ref, acc_ref):
    blk = q_ref.shape[0]
    pair = pl.program_id(0)
    qi = pl.program_id(1)
    nblk = pl.num_programs(1)
    q0 = pl.multiple_of(qi * blk, blk)
    lane = lax.broadcasted_iota(jnp.int32, (1, LANES), 1)
    lo_lanes = lane < FOX_DH
    nt = (((1,), (1,)), ((), ()))
    reps = blk // LANES

    q = q_ref[...]
    zero = jnp.zeros_like(q)
    q_stack = jnp.concatenate([jnp.where(lo_lanes, q, zero), jnp.where(lo_lanes, zero, q)], axis=0)
    m_ref[...] = jnp.full_like(m_ref, NEG_BIG)
    acc_ref[...] = jnp.zeros_like(acc_ref)

    def head_row(c8, h):
        sub = lax.broadcasted_iota(jnp.int32, c8.shape, 0)
        return jnp.sum(jnp.where(sub == h, c8, 0.0), axis=0, keepdims=True)

    c_q0 = [head_row(crow_ref[:, pl.ds(q0, LANES)], pair * HEAD_PAIR + hh)[:, 0:1]
            for hh in range(HEAD_PAIR)]

    def step(k0, masked):
        ks = pl.ds(k0, blk)
        kb = k_ref[:, ks]
        vb = v_ref[ks, :]
        one = jnp.ones_like(vb)
        vaug = (jnp.where(lo_lanes, vb, one), jnp.where(lo_lanes, one, vb))
        s_all = jnp.dot(q_stack, kb, preferred_element_type=F32)
        for hh in range(HEAD_PAIR):
            h = pair * HEAD_PAIR + hh
            bias = c_q0[hh] - head_row(crow_ref[:, ks], h)
            s = s_all[hh * blk:(hh + 1) * blk] + bias
            if masked:
                qpos = lax.broadcasted_iota(jnp.int32, (blk, blk), 0)
                kpos = lax.broadcasted_iota(jnp.int32, (blk, blk), 1)
                s = jnp.where(kpos <= qpos, s, NEG_BIG)
            m_prev = m_ref[hh]
            m_new = jnp.maximum(m_prev, jnp.max(s, axis=1, keepdims=True))
            p = jnp.exp(s - jnp.tile(m_new, (1, reps)))
            alpha = jnp.exp(m_prev - m_new)
            pv = jnp.dot(p.astype(BF16), vaug[hh], preferred_element_type=F32)
            acc_ref[hh] = alpha * acc_ref[hh] + pv
            m_ref[hh] = m_new

    @pl.when(qi == 0)
    def _():
        step(q0, masked=True)

    @pl.when(qi > 0)
    def _():
        step(q0, masked=True)
        step(q0 - blk, masked=False)

    m_min = [jnp.min(m_ref[hh]) for hh in range(HEAD_PAIR)]

    def live(j):
        jj = jnp.maximum(j, 0)
        prev = jnp.maximum(qi - 1, 0)
        keep = False
        for hh in range(HEAD_PAIR):
            base = (pair * HEAD_PAIR + hh) * nblk
            bound = (lim_ref[base + qi] + cend_ref[base + prev] - cend_ref[base + jj]) - m_min[hh]
            keep = jnp.logical_or(keep, jnp.logical_not(bound <= 0.0))
        return jnp.logical_and(j >= 0, keep)

    def body(j):
        step(pl.multiple_of(j * blk, blk), masked=False)
        step(pl.multiple_of((j - 1) * blk, blk), masked=False)
        return j - 2

    j_rest = lax.while_loop(lambda j: live(j - 1), body, qi - 2)

    @pl.when(live(j_rest))
    def _():
        step(pl.multiple_of(jnp.maximum(j_rest, 0) * blk, blk), masked=False)

    outs = []
    for hh in range(HEAD_PAIR):
        acc = acc_ref[hh]
        outs.append(acc / pltpu.roll(acc, FOX_DH, axis=1))
    o = jnp.where(lo_lanes, outs[0], outs[1])
    o_ref[...] = (o * gate_ref[...].astype(F32)).astype(BF16)


def _out_kernel(x_ref, gla_ref, fox_ref, mq_ref, mg_ref, mk_ref, mv_ref,
                wg_ref, wf_ref, wm_ref, fg_ref, o_ref):
    lane = lax.broadcasted_iota(jnp.int32, (1, LANES), 1)
    lo_lanes = lane < MEM_DH
    nt = (((1,), (1,)), ((), ()))
    mem_parts = []
    for p in range(MEM_HEADS // HEAD_PAIR):
        ls = slice(p * LANES, (p + 1) * LANES)
        q = mq_ref[:, ls]
        kb = mk_ref[:, ls]
        vb = mv_ref[:, ls]
        zero = jnp.zeros_like(q)
        one = jnp.ones_like(vb)
        qh = (jnp.where(lo_lanes, q, zero), jnp.where(lo_lanes, zero, q))
        vaug = (jnp.where(lo_lanes, vb, one), jnp.where(lo_lanes, one, vb))
        outs = []
        for hh in range(HEAD_PAIR):
            s = lax.dot_general(qh[hh], kb, nt, preferred_element_type=F32)
            pexp = jnp.exp(s - jnp.max(s, axis=1, keepdims=True))
            pv = jnp.dot(pexp.astype(BF16), vaug[hh], preferred_element_type=F32)
            outs.append(pv / pltpu.roll(pv, MEM_DH, axis=1))
        o = jnp.where(lo_lanes, outs[0], outs[1])
        mem_parts.append((o * mg_ref[:, ls].astype(F32)).astype(BF16))
    mem_mixed = jnp.concatenate(mem_parts, axis=1)
    y = (x_ref[...]
         + jnp.dot(gla_ref[...], wg_ref[...], preferred_element_type=F32)
         + jnp.dot(fox_ref[...], wf_ref[...], preferred_element_type=F32)
         + jnp.dot(mem_mixed, wm_ref[...], preferred_element_type=F32))
    o_ref[...] = y * _rms_scale(y, D_MODEL) * fg_ref[...]


def _pad_heads(w, heads, d, d_pad):
    lead = w.shape[:-1]
    w = w.reshape(lead + (heads, d))
    w = jnp.pad(w, [(0, 0)] * len(lead) + [(0, 0), (0, d_pad - d)])
    return w.reshape(lead + (heads * d_pad,))


def _pad_rows(w, heads, d, d_pad):
    return _pad_heads(w.T, heads, d, d_pad).T


def _layout_weights(w_in, w_alpha_up, b_alpha, b_forget, gla_norm_g, w_out):
    qk, gw = GLA_HEADS * GLA_DK, GLA_HEADS * GLA_DV
    sizes = (qk, qk, gw, GLA_RANK, gw, FOX_W, FOX_W, FOX_W, FOX_HEADS, FOX_W, MEM_W, MEM_W)
    parts = []
    o = 0
    for s in sizes:
        parts.append(w_in[:, o:o + s])
        o += s
    g_q, g_k, g_v, g_lr, g_gate, f_q, f_k, f_v, f_fg, f_gate, m_q, m_gate = parts
    small = jnp.zeros((D_MODEL, SMALL_W), F32)
    small = small.at[:, FG_LANE0:FG_LANE0 + FOX_HEADS].set(f_fg)
    small = small.at[:, LR_LANE0:LR_LANE0 + GLA_RANK].set(g_lr)
    w_all = jnp.concatenate([
        _pad_heads(g_q, GLA_HEADS, GLA_DK, GLA_DK_PAD),
        _pad_heads(g_k, GLA_HEADS, GLA_DK, GLA_DK_PAD),
        _pad_heads(g_v, GLA_HEADS, GLA_DV, GLA_DV_PAD),
        _pad_heads(g_gate, GLA_HEADS, GLA_DV, GLA_DV_PAD),
        f_q, f_k, f_v, f_gate, m_q, m_gate, small], axis=1).astype(BF16)
    wa = jnp.zeros((SMALL_W, GLA_QK_W), F32)
    wa = wa.at[LR_LANE0:LR_LANE0 + GLA_RANK, :].set(
        _pad_heads(w_alpha_up, GLA_HEADS, GLA_DK, GLA_DK_PAD)).astype(BF16)
    ba = _pad_heads(b_alpha[None, :], GLA_HEADS, GLA_DK, GLA_DK_PAD)
    bf = jnp.zeros((1, SMALL_W), F32).at[0, FG_LANE0:FG_LANE0 + FOX_HEADS].set(b_forget)
    ng = jnp.pad(gla_norm_g, (0, GLA_DV_PAD - GLA_DV))[None, :]
    wo_g = _pad_rows(w_out[:gw], GLA_HEADS, GLA_DV, GLA_DV_PAD).astype(BF16)
    wo_f = w_out[gw:gw + FOX_W].astype(BF16)
    wo_m = w_out[gw + FOX_W:].astype(BF16)
    return w_all, wa, ba, bf, ng, wo_g, wo_f, wo_m


def _fox_limits(qn2, kn2, crow, blk):
    nblk = crow.shape[1] // blk
    nproj = qn2.shape[0]
    assert nblk % nproj == 0
    qn = jnp.sqrt(qn2[:, 0, :FOX_HEADS])
    kmax = jnp.sqrt(jnp.max(kn2[:, 0, :FOX_HEADS], axis=0))
    lim = FOX_SKIP_NATS + NORM_SLACK * jnp.repeat(qn, nblk // nproj, axis=0) * kmax[None, :]
    cend = crow[:FOX_HEADS, blk - 1::blk]
    return lim.T.reshape(-1), cend.reshape(-1)


def _params(*sem):
    return pltpu.CompilerParams(dimension_semantics=sem, vmem_limit_bytes=VMEM_LIMIT)


def _layer(x, mem, norm_g, w_in, w_alpha_up, b_alpha, b_forget, gla_norm_g,
           mem_norm_g, w_mem_kv, w_out, out_g):
    T = x.shape[0]
    M = mem.shape[0]
    w_all, wa, ba, bf, ng, wo_g, wo_f, wo_m = _layout_weights(
        w_in, w_alpha_up, b_alpha, b_forget, gla_norm_g, w_out)

    def rows(width, n=PROJ_ROWS):
        return pl.BlockSpec((n, width), lambda i: (i, 0))

    def whole(shape):
        return pl.BlockSpec(shape, lambda i: (0,) * len(shape))

    bshape = lambda w: jax.ShapeDtypeStruct((T, w), BF16)
    nproj = T // PROJ_ROWS
    stat_spec = pl.BlockSpec((1, 1, LANES), lambda i: (i, 0, 0))
    stat_shape = jax.ShapeDtypeStruct((nproj, 1, LANES), F32)
    seg = (jnp.arange(FOX_W)[:, None] // FOX_DH == jnp.arange(LANES)[None, :]).astype(BF16)
    (gq, gk, gv, gg, fq, fk, fv, fgate, mq, mg, loga, crow, qn2, kn2) = pl.pallas_call(
        _proj_kernel,
        grid=(nproj,),
        in_specs=[rows(D_MODEL), whole((1, D_MODEL)), whole((D_MODEL, IN_COLS_PAD)),
                  whole((SMALL_W, GLA_QK_W)), whole((1, GLA_QK_W)), whole((1, SMALL_W)),
                  whole((FOX_W, LANES))],
        out_specs=[rows(GLA_QK_W), rows(GLA_QK_W), rows(GLA_V_W), rows(GLA_V_W),
                   rows(FOX_W), pl.BlockSpec((FOX_W, PROJ_ROWS), lambda i: (0, i)),
                   rows(FOX_W), rows(FOX_W),
                   rows(MEM_W), rows(MEM_W), rows(GLA_QK_W),
                   pl.BlockSpec((SUBLANES, PROJ_ROWS), lambda i: (0, i)),
                   stat_spec, stat_spec],
        out_shape=[bshape(GLA_QK_W), bshape(GLA_QK_W), bshape(GLA_V_W), bshape(GLA_V_W),
                   bshape(FOX_W), jax.ShapeDtypeStruct((FOX_W, T), BF16),
                   bshape(FOX_W), bshape(FOX_W),
                   bshape(MEM_W), bshape(MEM_W),
                   jax.ShapeDtypeStruct((T, GLA_QK_W), F32),
                   jax.ShapeDtypeStruct((SUBLANES, T), F32),
                   stat_shape, stat_shape],
        scratch_shapes=[pltpu.VMEM((SUBLANES, LANES), F32)],
        compiler_params=_params("arbitrary"),
        name="proj",
    )(x, norm_g[None, :], w_all, wa, ba, bf, seg)

    mk, mv = pl.pallas_call(
        _memkv_kernel,
        out_shape=[jax.ShapeDtypeStruct((M, MEM_W), BF16)] * 2,
        compiler_params=pltpu.CompilerParams(vmem_limit_bytes=VMEM_LIMIT),
        name="memkv",
    )(mem, mem_norm_g[None, :], w_mem_kv.astype(BF16))

    gla_pairs, gla_chunks = GLA_HEADS // HEAD_PAIR, GLA_ROWS // GLA_CHUNK
    pair_w = HEAD_PAIR * GLA_DV_PAD
    gla = pl.pallas_call(
        _gla_kernel,
        grid=(T // GLA_ROWS,),
        in_specs=[rows(GLA_QK_W, GLA_ROWS), rows(GLA_QK_W, GLA_ROWS), rows(GLA_V_W, GLA_ROWS),
                  rows(GLA_QK_W, GLA_ROWS), rows(GLA_V_W, GLA_ROWS), whole((1, GLA_DV_PAD))],
        out_specs=rows(GLA_V_W, GLA_ROWS),
        out_shape=bshape(GLA_V_W),
        scratch_shapes=[
            pltpu.VMEM((gla_pairs, LANES, pair_w), F32),
            pltpu.VMEM((GLA_ROWS, GLA_V_W), BF16),
            pltpu.VMEM((gla_pairs, gla_chunks, LANES, pair_w), F32),
            pltpu.VMEM((gla_pairs, gla_chunks, LANES, LANES), F32),
            pltpu.VMEM((gla_pairs, gla_chunks, LANES, pair_w), BF16)],
        compiler_params=_params("arbitrary"),
        name="gla",
    )(gq, gk, gv, loga, gg, ng)

    lim, cend = _fox_limits(qn2, kn2, crow, FOX_BLOCK)
    pair_rows = pl.BlockSpec((FOX_BLOCK, LANES), lambda p, i, *_: (i, p))
    pair_all = pl.BlockSpec((T, LANES), lambda p, i, *_: (0, p))
    fox = pl.pallas_call(
        _fox_kernel,
        grid_spec=pltpu.PrefetchScalarGridSpec(
            num_scalar_prefetch=2,
            grid=(FOX_HEADS // HEAD_PAIR, T // FOX_BLOCK),
            in_specs=[pair_rows, pl.BlockSpec((LANES, T), lambda p, i, *_: (p, 0)), pair_all,
                      pl.BlockSpec((SUBLANES, T), lambda p, i, *_: (0, 0)), pair_rows],
            out_specs=pair_rows,
            scratch_shapes=[pltpu.VMEM((HEAD_PAIR, FOX_BLOCK, LANES), F32),
                            pltpu.VMEM((HEAD_PAIR, FOX_BLOCK, LANES), F32)]),
        out_shape=bshape(FOX_W),
        compiler_params=_params("arbitrary", "arbitrary"),
        name="fox",
    )(lim, cend, fq, fk, fv, crow, fgate)

    out = pl.pallas_call(
        _out_kernel,
        grid=(T // OUT_ROWS,),
        in_specs=[rows(D_MODEL, OUT_ROWS), rows(GLA_V_W, OUT_ROWS), rows(FOX_W, OUT_ROWS),
                  rows(MEM_W, OUT_ROWS), rows(MEM_W, OUT_ROWS),
                  whole((M, MEM_W)), whole((M, MEM_W)),
                  whole((GLA_V_W, D_MODEL)), whole((FOX_W, D_MODEL)), whole((MEM_W, D_MODEL)),
                  whole((1, D_MODEL))],
        out_specs=rows(D_MODEL, OUT_ROWS),
        out_shape=jax.ShapeDtypeStruct((T, D_MODEL), F32),
        compiler_params=_params("arbitrary"),
        name="out",
    )(x, gla, fox, mq, mg, mk, mv, wo_g, wo_f, wo_m, out_g[None, :])
    return out


def kernel(x, mem, norm_g, w_in, w_alpha_up, b_alpha, b_forget, gla_norm_g, mem_norm_g,
           w_mem_kv, w_out, final_norm_g):
    assert x.shape[0] == 1 and mem.shape[0] == 1 and norm_g.shape[0] == 1
    assert x.shape[1] % max(PROJ_ROWS, GLA_ROWS, FOX_BLOCK, OUT_ROWS) == 0
    out = _layer(x[0], mem[0], norm_g[0], w_in[0], w_alpha_up[0], b_alpha[0], b_forget[0],
                 gla_norm_g[0], mem_norm_g[0], w_mem_kv[0], w_out[0], final_norm_g)
    return out[None]
```

```python
import functools

import jax
import jax.numpy as jnp
from jax import lax
from jax.experimental import pallas as pl
from jax.experimental.pallas import tpu as pltpu

F32 = jnp.float32
BF16 = jnp.bfloat16

EPS = 1e-6
LANES = 128
SUBLANES = 8

D_MODEL = 1024
GLA_HEADS, GLA_DK, GLA_DV, GLA_RANK = 4, 48, 96, 16
GLA_DK_PAD = 64
GLA_DV_PAD = LANES
GLA_GATE_NORM = 16.0
GLA_CHUNK = 64
FOX_HEADS, FOX_DH = 6, 64
MEM_HEADS, MEM_DH = 4, 64
HEAD_PAIR = 2
GLA_QK_W = GLA_HEADS * GLA_DK_PAD
GLA_V_W = GLA_HEADS * GLA_DV_PAD
FOX_W = FOX_HEADS * FOX_DH
MEM_W = MEM_HEADS * MEM_DH
SMALL_W = LANES
FG_LANE0 = 0
LR_LANE0 = SUBLANES

_GROUPS = (("gq", GLA_QK_W), ("gk", GLA_QK_W), ("gv", GLA_V_W), ("gg", GLA_V_W),
           ("fq", FOX_W), ("fk", FOX_W), ("fv", FOX_W), ("fgate", FOX_W),
           ("mq", MEM_W), ("mg", MEM_W), ("small", SMALL_W))
_OFF = {}
_o = 0
for _n, _w in _GROUPS:
    _OFF[_n] = (_o, _o + _w)
    _o += _w
IN_COLS_PAD = _o

PROJ_ROWS = 512
GLA_ROWS = 512
GLA_UNROLL = 8
FOX_BLOCK = 512
OUT_ROWS = 512
VMEM_LIMIT = 56 * 1024 * 1024

NEG_BIG = -1e30
FOX_SKIP_NATS = 105.0
NORM_SLACK = 1.02


def _log_sigmoid(z):
    return jnp.minimum(z, 0.0) - jnp.log(1.0 + jnp.exp(-jnp.abs(z)))


def _silu(z):
    return z / (1.0 + jnp.exp(-z))


def _rms_scale(v, width):
    return lax.rsqrt(jnp.sum(v * v, axis=-1, keepdims=True) * (1.0 / width) + EPS)


def _w_in_segments():
    qk, gw = GLA_HEADS * GLA_DK, GLA_HEADS * GLA_DV
    src = {}
    o = 0
    for name, width in (("gq", qk), ("gk", qk), ("gv", gw), ("lr", GLA_RANK), ("gg", gw),
                        ("fq", FOX_W), ("fk", FOX_W), ("fv", FOX_W), ("fg", FOX_HEADS),
                        ("fgate", FOX_W), ("mq", MEM_W), ("mg", MEM_W)):
        src[name] = o
        o += width
    segs = []
    for name, d, d_pad in (("gq", GLA_DK, GLA_DK_PAD), ("gk", GLA_DK, GLA_DK_PAD),
                           ("gv", GLA_DV, GLA_DV_PAD), ("gg", GLA_DV, GLA_DV_PAD)):
        segs += [(src[name] + h * d, _OFF[name][0] + h * d_pad, d) for h in range(GLA_HEADS)]
    segs += [(src[name], _OFF[name][0], _OFF[name][1] - _OFF[name][0])
             for name in ("fq", "fk", "fv", "fgate", "mq", "mg")]
    segs += [(src["fg"], _OFF["small"][0] + FG_LANE0, FOX_HEADS),
             (src["lr"], _OFF["small"][0] + LR_LANE0, GLA_RANK)]
    return tuple(segs)


def _proj_kernel(x_ref, g_ref, w_in_ref, wa_ref, ba_ref, bf_ref, seg_ref,
                 gq_ref, gk_ref, gv_ref, gg_ref, fq_ref, fk_ref, fv_ref, fgate_ref,
                 mq_ref, mg_ref, loga_ref, crow_ref, qn2_ref, kn2_ref, carry_ref, w_ref):
    rows = x_ref.shape[0]

    @pl.when(pl.program_id(0) == 0)
    def _():
        carry_ref[...] = jnp.zeros_like(carry_ref)
        w_ref[...] = jnp.zeros_like(w_ref)
        for s0, d0, width in _w_in_segments():
            w_ref[:, d0:d0 + width] = w_in_ref[:, s0:s0 + width].astype(BF16)

    x = x_ref[...]
    xn = (x * _rms_scale(x, D_MODEL) * g_ref[...]).astype(BF16)

    def proj(first, last):
        lo, hi = _OFF[first][0], _OFF[last][1]
        y = jnp.dot(xn, w_ref[:, lo:hi], preferred_element_type=F32)
        return lambda name: y[:, _OFF[name][0] - lo:_OFF[name][1] - lo]

    tail = proj("mq", "small")
    small = tail("small")
    logf = _log_sigmoid(small + bf_ref[...])
    c = logf.T[0:SUBLANES, :]
    lane = lax.broadcasted_iota(jnp.int32, c.shape, 1)
    shift = 1
    while shift < rows:
        c = c + jnp.where(lane >= shift, pltpu.roll(c, shift, axis=1), 0.0)
        shift *= 2
    c = c + carry_ref[:, 0:1]
    crow_ref[...] = c
    carry_ref[...] = jnp.broadcast_to(c[:, rows - 1:rows], carry_ref.shape)

    z = jnp.dot(small.astype(BF16), wa_ref[...], preferred_element_type=F32) + ba_ref[...]
    loga_ref[...] = _log_sigmoid(z) * (1.0 / GLA_GATE_NORM)
    mq_ref[...] = (tail("mq") * MEM_DH ** -0.5).astype(BF16)
    mg_ref[...] = _silu(tail("mg")).astype(BF16)

    fox = proj("fq", "fgate")
    fq = (fox("fq") * FOX_DH ** -0.5).astype(BF16)
    fk = fox("fk").astype(BF16)
    fq_ref[...] = fq
    fk_ref[...] = fox("fk").T.astype(BF16)
    fv_ref[...] = fox("fv").astype(BF16)
    fgate_ref[...] = _silu(fox("fgate")).astype(BF16)

    def max_sq_norm(v):
        v32 = v.astype(F32)
        n2 = jnp.dot((v32 * v32).astype(BF16), seg_ref[...], preferred_element_type=F32)
        return jnp.max(n2, axis=0, keepdims=True)

    qn2_ref[0] = max_sq_norm(fq)
    kn2_ref[0] = max_sq_norm(fk)

    gla = proj("gq", "gg")
    gq_ref[...] = gla("gq").astype(BF16)
    gk_ref[...] = gla("gk").astype(BF16)
    gv_ref[...] = gla("gv").astype(BF16)
    gg_ref[...] = _silu(gla("gg")).astype(BF16)


def _memkv_kernel(mem_ref, g_ref, w_ref, mk_ref, mv_ref):
    m = mem_ref[...]
    mn = (m * _rms_scale(m, D_MODEL) * g_ref[...]).astype(BF16)
    kv = jnp.dot(mn, w_ref[...], preferred_element_type=F32)
    mk_ref[...] = kv[:, :MEM_W].astype(BF16)
    mv_ref[...] = kv[:, MEM_W:].astype(BF16)


def _gla_kernel(q_ref, k_ref, v_ref, loga_ref, gate_ref, ng_ref, o_ref,
                s_ref, lhs_ref, kv_ref, dec_ref, sprev_ref):
    C = GLA_CHUNK
    W = HEAD_PAIR * GLA_DV_PAD
    n_groups = q_ref.shape[0] // (C * GLA_UNROLL)

    @pl.when(pl.program_id(0) == 0)
    def _():
        s_ref[...] = jnp.zeros_like(s_ref)

    row = lax.broadcasted_iota(jnp.int32, (C, LANES), 0)
    lane = lax.broadcasted_iota(jnp.int32, (C, LANES), 1)
    lo_k = lane < GLA_DK_PAD
    causal = row >= jnp.where(lo_k, lane, lane - GLA_DK_PAD)
    lo_v = lax.broadcasted_iota(jnp.int32, (C, W), 1) < GLA_DV_PAD
    st_row = lax.broadcasted_iota(jnp.int32, (LANES, W), 0)
    st_lane = lax.broadcasted_iota(jnp.int32, (LANES, W), 1)
    own = (st_row < GLA_DK_PAD) == (st_lane < GLA_DV_PAD)
    eye = (lax.broadcasted_iota(jnp.int32, (LANES, LANES), 0)
           == lax.broadcasted_iota(jnp.int32, (LANES, LANES), 1))
    scale = GLA_DK ** -0.5
    nt = (((1,), (1,)), ((), ()))
    tn = (((0,), (0,)), ((), ()))
    ng = jnp.concatenate([ng_ref[...]] * HEAD_PAIR, axis=1)

    pairs = range(GLA_HEADS // HEAD_PAIR)

    def local(ci):
        rs = pl.ds(pl.multiple_of(ci * C, C), C)
        for p in pairs:
            ls = slice(p * LANES, (p + 1) * LANES)
            vs = slice(p * W, (p + 1) * W)
            b = loga_ref[rs, ls]
            shift = 1
            while shift < C:
                b = b + jnp.where(row >= shift, pltpu.roll(b, shift, axis=0), 0.0)
                shift *= 2
            b_last = b[C - 1:C, :]
            k2 = k_ref[rs, ls].astype(F32)
            qd = (q_ref[rs, ls].astype(F32) * scale * jnp.exp(b)).astype(BF16)
            kd = (k2 * jnp.exp(-b)).astype(BF16)
            ke = (k2 * jnp.exp(b_last - b)).astype(BF16)
            zk = jnp.zeros_like(kd)
            kd_blk = jnp.concatenate([jnp.where(lo_k, kd, zk), jnp.where(lo_k, zk, kd)], axis=0)
            attn = lax.dot_general(qd, kd_blk, nt, preferred_element_type=F32)
            lhs_ref[rs, vs] = jnp.concatenate([jnp.where(causal, attn, 0.0).astype(BF16), qd], axis=1)
            kv = lax.dot_general(ke, v_ref[rs, vs], tn, preferred_element_type=F32)
            kv_ref[p, ci] = jnp.where(own, kv, 0.0)
            dcol = jnp.exp(jnp.sum(jnp.where(eye, jnp.broadcast_to(b_last, (LANES, LANES)), 0.0),
                                   axis=1, keepdims=True))
            dec_ref[p, ci] = jnp.broadcast_to(dcol, (LANES, LANES))

    def scan(ci, carry):
        for p in pairs:
            s_prev = s_ref[p]
            sprev_ref[p, ci] = s_prev.astype(BF16)
            s_ref[p] = jnp.tile(dec_ref[p, ci], (1, HEAD_PAIR)) * s_prev + kv_ref[p, ci]
        return carry

    def output(ci):
        rs = pl.ds(pl.multiple_of(ci * C, C), C)
        for p in pairs:
            vs = slice(p * W, (p + 1) * W)
            v2 = v_ref[rs, vs]
            zv = jnp.zeros_like(v2)
            v_blk = jnp.concatenate([jnp.where(lo_v, v2, zv), jnp.where(lo_v, zv, v2)], axis=0)
            o = jnp.dot(lhs_ref[rs, vs], jnp.concatenate([v_blk, sprev_ref[p, ci]], axis=0),
                        preferred_element_type=F32)
            o2 = o * o
            ms = jnp.where(lo_v, jnp.sum(o2[:, :GLA_DV_PAD], axis=1, keepdims=True),
                           jnp.sum(o2[:, GLA_DV_PAD:], axis=1, keepdims=True))
            on = o * lax.rsqrt(ms * (1.0 / GLA_DV) + EPS) * ng
            o_ref[rs, vs] = (on * gate_ref[rs, vs].astype(F32)).astype(BF16)

    def unrolled(fn):
        def group(gi, carry):
            for u in range(GLA_UNROLL):
                fn(gi * GLA_UNROLL + u)
            return carry
        return group

    lax.fori_loop(0, n_groups, unrolled(local), 0)
    lax.fori_loop(0, n_groups * GLA_UNROLL, scan, 0)
    lax.fori_loop(0, n_groups, unrolled(output), 0)


def _fox_kernel(lim_ref, cend_ref, q_ref, k_ref, v_ref, crow_ref, gate_ref, o_ref, m_ref, acc_ref):
    blk = q_ref.shape[0]
    pair = pl.program_id(0)
    qi = pl.program_id(1)
    nblk = pl.num_programs(1)
    q0 = pl.multiple_of(qi * blk, blk)
    lane = lax.broadcasted_iota(jnp.int32, (1, LANES), 1)
    lo_lanes = lane < FOX_DH
    nt = (((1,), (1,)), ((), ()))
    reps = blk // LANES

    q = q_ref[...]
    zero = jnp.zeros_like(q)
    q_stack = jnp.concatenate([jnp.where(lo_lanes, q, zero), jnp.where(lo_lanes, zero, q)], axis=0)
    m_ref[...] = jnp.full_like(m_ref, NEG_BIG)
    acc_ref[...] = jnp.zeros_like(acc_ref)

    def head_row(c8, h):
        sub = lax.broadcasted_iota(jnp.int32, c8.shape, 0)
        return jnp.sum(jnp.where(sub == h, c8, 0.0), axis=0, keepdims=True)

    c_q0 = [head_row(crow_ref[:, pl.ds(q0, LANES)], pair * HEAD_PAIR + hh)[:, 0:1]
            for hh in range(HEAD_PAIR)]

    def step(k0, masked):
        ks = pl.ds(k0, blk)
        kb = k_ref[:, ks]
        vb = v_ref[ks, :]
        one = jnp.ones_like(vb)
        vaug = (jnp.where(lo_lanes, vb, one), jnp.where(lo_lanes, one, vb))
        s_all = jnp.dot(q_stack, kb, preferred_element_type=F32)
        for hh in range(HEAD_PAIR):
            h = pair * HEAD_PAIR + hh
            bias = c_q0[hh] - head_row(crow_ref[:, ks], h)
            s = s_all[hh * blk:(hh + 1) * blk] + bias
            if masked:
                qpos = lax.broadcasted_iota(jnp.int32, (blk, blk), 0)
                kpos = lax.broadcasted_iota(jnp.int32, (blk, blk), 1)
                s = jnp.where(kpos <= qpos, s, NEG_BIG)
            m_prev = m_ref[hh]
            m_new = jnp.maximum(m_prev, jnp.max(s, axis=1, keepdims=True))
            p = jnp.exp(s - jnp.tile(m_new, (1, reps)))
            alpha = jnp.exp(m_prev - m_new)
            pv = jnp.dot(p.astype(BF16), vaug[hh], preferred_element_type=F32)
            acc_ref[hh] = alpha * acc_ref[hh] + pv
            m_ref[hh] = m_new

    @pl.when(qi == 0)
    def _():
        step(q0, masked=True)

    @pl.when(qi > 0)
    def _():
        step(q0, masked=True)
        step(q0 - blk, masked=False)

    m_min = [jnp.min(m_ref[hh]) for hh in range(HEAD_PAIR)]

    def live(j):
        jj = jnp.maximum(j, 0)
        prev = jnp.maximum(qi - 1, 0)
        keep = False
        for hh in range(HEAD_PAIR):
            base = (pair * HEAD_PAIR + hh) * nblk
            bound = (lim_ref[base + qi] + cend_ref[base + prev] - cend_ref[base + jj]) - m_min[hh]
            keep = jnp.logical_or(keep, jnp.logical_not(bound <= 0.0))
        return jnp.logical_and(j >= 0, keep)

    def body(j):
        step(pl.multiple_of(j * blk, blk), masked=False)
        step(pl.multiple_of((j - 1) * blk, blk), masked=False)
        return j - 2

    j_rest = lax.while_loop(lambda j: live(j - 1), body, qi - 2)

    @pl.when(live(j_rest))
    def _():
        step(pl.multiple_of(jnp.maximum(j_rest, 0) * blk, blk), masked=False)

    outs = []
    for hh in range(HEAD_PAIR):
        acc = acc_ref[hh]
        outs.append(acc / pltpu.roll(acc, FOX_DH, axis=1))
    o = jnp.where(lo_lanes, outs[0], outs[1])
    o_ref[...] = (o * gate_ref[...].astype(F32)).astype(BF16)


def _out_kernel(x_ref, gla_ref, fox_ref, mq_ref, mg_ref, mk_ref, mv_ref,
                wg_ref, wf_ref, wm_ref, fg_ref, o_ref):
    lane = lax.broadcasted_iota(jnp.int32, (1, LANES), 1)
    lo_lanes = lane < MEM_DH
    nt = (((1,), (1,)), ((), ()))
    mem_parts = []
    for p in range(MEM_HEADS // HEAD_PAIR):
        ls = slice(p * LANES, (p + 1) * LANES)
        q = mq_ref[:, ls]
        kb = mk_ref[:, ls]
        vb = mv_ref[:, ls]
        zero = jnp.zeros_like(q)
        one = jnp.ones_like(vb)
        qh = (jnp.where(lo_lanes, q, zero), jnp.where(lo_lanes, zero, q))
        vaug = (jnp.where(lo_lanes, vb, one), jnp.where(lo_lanes, one, vb))
        outs = []
        for hh in range(HEAD_PAIR):
            s = lax.dot_general(qh[hh], kb, nt, preferred_element_type=F32)
            pexp = jnp.exp(s - jnp.max(s, axis=1, keepdims=True))
            pv = jnp.dot(pexp.astype(BF16), vaug[hh], preferred_element_type=F32)
            outs.append(pv / pltpu.roll(pv, MEM_DH, axis=1))
        o = jnp.where(lo_lanes, outs[0], outs[1])
        mem_parts.append((o * mg_ref[:, ls].astype(F32)).astype(BF16))
    mem_mixed = jnp.concatenate(mem_parts, axis=1)
    y = (x_ref[...]
         + jnp.dot(gla_ref[...], wg_ref[...], preferred_element_type=F32)
         + jnp.dot(fox_ref[...], wf_ref[...], preferred_element_type=F32)
         + jnp.dot(mem_mixed, wm_ref[...], preferred_element_type=F32))
    o_ref[...] = y * _rms_scale(y, D_MODEL) * fg_ref[...]


def _pad_heads(w, heads, d, d_pad):
    lead = w.shape[:-1]
    w = w.reshape(lead + (heads, d))
    w = jnp.pad(w, [(0, 0)] * len(lead) + [(0, 0), (0, d_pad - d)])
    return w.reshape(lead + (heads * d_pad,))


def _pad_rows(w, heads, d, d_pad):
    return _pad_heads(w.T, heads, d, d_pad).T


def _layout_weights(w_alpha_up, b_alpha, b_forget, gla_norm_g, w_out):
    gw = GLA_HEADS * GLA_DV
    wa = jnp.zeros((SMALL_W, GLA_QK_W), F32)
    wa = wa.at[LR_LANE0:LR_LANE0 + GLA_RANK, :].set(
        _pad_heads(w_alpha_up, GLA_HEADS, GLA_DK, GLA_DK_PAD)).astype(BF16)
    ba = _pad_heads(b_alpha[None, :], GLA_HEADS, GLA_DK, GLA_DK_PAD)
    bf = jnp.zeros((1, SMALL_W), F32).at[0, FG_LANE0:FG_LANE0 + FOX_HEADS].set(b_forget)
    ng = jnp.pad(gla_norm_g, (0, GLA_DV_PAD - GLA_DV))[None, :]
    wo_g = _pad_rows(w_out[:gw], GLA_HEADS, GLA_DV, GLA_DV_PAD).astype(BF16)
    wo_f = w_out[gw:gw + FOX_W].astype(BF16)
    wo_m = w_out[gw + FOX_W:].astype(BF16)
    return wa, ba, bf, ng, wo_g, wo_f, wo_m


def _fox_limits(qn2, kn2, crow, blk):
    nblk = crow.shape[1] // blk
    nproj = qn2.shape[0]
    assert nblk % nproj == 0
    qn = jnp.sqrt(qn2[:, 0, :FOX_HEADS])
    kmax = jnp.sqrt(jnp.max(kn2[:, 0, :FOX_HEADS], axis=0))
    lim = FOX_SKIP_NATS + NORM_SLACK * jnp.repeat(qn, nblk // nproj, axis=0) * kmax[None, :]
    cend = crow[:FOX_HEADS, blk - 1::blk]
    return lim.T.reshape(-1), cend.reshape(-1)


def _params(*sem):
    return pltpu.CompilerParams(dimension_semantics=sem, vmem_limit_bytes=VMEM_LIMIT)


def _layer(x, mem, norm_g, w_in, w_alpha_up, b_alpha, b_forget, gla_norm_g,
           mem_norm_g, w_mem_kv, w_out, out_g):
    T = x.shape[0]
    M = mem.shape[0]
    wa, ba, bf, ng, wo_g, wo_f, wo_m = _layout_weights(
        w_alpha_up, b_alpha, b_forget, gla_norm_g, w_out)

    def rows(width, n=PROJ_ROWS):
        return pl.BlockSpec((n, width), lambda i: (i, 0))

    def whole(shape):
        return pl.BlockSpec(shape, lambda i: (0,) * len(shape))

    bshape = lambda w: jax.ShapeDtypeStruct((T, w), BF16)
    nproj = T // PROJ_ROWS
    stat_spec = pl.BlockSpec((1, 1, LANES), lambda i: (i, 0, 0))
    stat_shape = jax.ShapeDtypeStruct((nproj, 1, LANES), F32)
    seg = (jnp.arange(FOX_W)[:, None] // FOX_DH == jnp.arange(LANES)[None, :]).astype(BF16)
    (gq, gk, gv, gg, fq, fk, fv, fgate, mq, mg, loga, crow, qn2, kn2) = pl.pallas_call(
        _proj_kernel,
        grid=(nproj,),
        in_specs=[rows(D_MODEL), whole((1, D_MODEL)),
                  pl.BlockSpec(w_in.shape, lambda i: (0, 0), pipeline_mode=pl.Buffered(1)),
                  whole((SMALL_W, GLA_QK_W)), whole((1, GLA_QK_W)), whole((1, SMALL_W)),
                  whole((FOX_W, LANES))],
        out_specs=[rows(GLA_QK_W), rows(GLA_QK_W), rows(GLA_V_W), rows(GLA_V_W),
                   rows(FOX_W), pl.BlockSpec((FOX_W, PROJ_ROWS), lambda i: (0, i)),
                   rows(FOX_W), rows(FOX_W),
                   rows(MEM_W), rows(MEM_W), rows(GLA_QK_W),
                   pl.BlockSpec((SUBLANES, PROJ_ROWS), lambda i: (0, i)),
                   stat_spec, stat_spec],
        out_shape=[bshape(GLA_QK_W), bshape(GLA_QK_W), bshape(GLA_V_W), bshape(GLA_V_W),
                   bshape(FOX_W), jax.ShapeDtypeStruct((FOX_W, T), BF16),
                   bshape(FOX_W), bshape(FOX_W),
                   bshape(MEM_W), bshape(MEM_W),
                   jax.ShapeDtypeStruct((T, GLA_QK_W), F32),
                   jax.ShapeDtypeStruct((SUBLANES, T), F32),
                   stat_shape, stat_shape],
        scratch_shapes=[pltpu.VMEM((SUBLANES, LANES), F32),
                        pltpu.VMEM((D_MODEL, IN_COLS_PAD), BF16)],
        compiler_params=_params("arbitrary"),
        name="proj",
    )(x, norm_g[None, :], w_in, wa, ba, bf, seg)

    mk, mv = pl.pallas_call(
        _memkv_kernel,
        out_shape=[jax.ShapeDtypeStruct((M, MEM_W), BF16)] * 2,
        compiler_params=pltpu.CompilerParams(vmem_limit_bytes=VMEM_LIMIT),
        name="memkv",
    )(mem, mem_norm_g[None, :], w_mem_kv.astype(BF16))

    gla_pairs, gla_chunks = GLA_HEADS // HEAD_PAIR, GLA_ROWS // GLA_CHUNK
    pair_w = HEAD_PAIR * GLA_DV_PAD
    gla = pl.pallas_call(
        _gla_kernel,
        grid=(T // GLA_ROWS,),
        in_specs=[rows(GLA_QK_W, GLA_ROWS), rows(GLA_QK_W, GLA_ROWS), rows(GLA_V_W, GLA_ROWS),
                  rows(GLA_QK_W, GLA_ROWS), rows(GLA_V_W, GLA_ROWS), whole((1, GLA_DV_PAD))],
        out_specs=rows(GLA_V_W, GLA_ROWS),
        out_shape=bshape(GLA_V_W),
        scratch_shapes=[
            pltpu.VMEM((gla_pairs, LANES, pair_w), F32),
            pltpu.VMEM((GLA_ROWS, GLA_V_W), BF16),
            pltpu.VMEM((gla_pairs, gla_chunks, LANES, pair_w), F32),
            pltpu.VMEM((gla_pairs, gla_chunks, LANES, LANES), F32),
            pltpu.VMEM((gla_pairs, gla_chunks, LANES, pair_w), BF16)],
        compiler_params=_params("arbitrary"),
        name="gla",
    )(gq, gk, gv, loga, gg, ng)

    lim, cend = _fox_limits(qn2, kn2, crow, FOX_BLOCK)
    pair_rows = pl.BlockSpec((FOX_BLOCK, LANES), lambda p, i, *_: (i, p))
    pair_all = pl.BlockSpec((T, LANES), lambda p, i, *_: (0, p))
    fox = pl.pallas_call(
        _fox_kernel,
        grid_spec=pltpu.PrefetchScalarGridSpec(
            num_scalar_prefetch=2,
            grid=(FOX_HEADS // HEAD_PAIR, T // FOX_BLOCK),
            in_specs=[pair_rows, pl.BlockSpec((LANES, T), lambda p, i, *_: (p, 0)), pair_all,
                      pl.BlockSpec((SUBLANES, T), lambda p, i, *_: (0, 0)), pair_rows],
            out_specs=pair_rows,
            scratch_shapes=[pltpu.VMEM((HEAD_PAIR, FOX_BLOCK, LANES), F32),
                            pltpu.VMEM((HEAD_PAIR, FOX_BLOCK, LANES), F32)]),
        out_shape=bshape(FOX_W),
        compiler_params=_params("arbitrary", "arbitrary"),
        name="fox",
    )(lim, cend, fq, fk, fv, crow, fgate)

    out = pl.pallas_call(
        _out_kernel,
        grid=(T // OUT_ROWS,),
        in_specs=[rows(D_MODEL, OUT_ROWS), rows(GLA_V_W, OUT_ROWS), rows(FOX_W, OUT_ROWS),
                  rows(MEM_W, OUT_ROWS), rows(MEM_W, OUT_ROWS),
                  whole((M, MEM_W)), whole((M, MEM_W)),
                  whole((GLA_V_W, D_MODEL)), whole((FOX_W, D_MODEL)), whole((MEM_W, D_MODEL)),
                  whole((1, D_MODEL))],
        out_specs=rows(D_MODEL, OUT_ROWS),
        out_shape=jax.ShapeDtypeStruct((T, D_MODEL), F32),
        compiler_params=_params("arbitrary"),
        name="out",
    )(x, gla, fox, mq, mg, mk, mv, wo_g, wo_f, wo_m, out_g[None, :])
    return out


def kernel(x, mem, norm_g, w_in, w_alpha_up, b_alpha, b_forget, gla_norm_g, mem_norm_g,
           w_mem_kv, w_out, final_norm_g):
    assert x.shape[0] == 1 and mem.shape[0] == 1 and norm_g.shape[0] == 1
    assert x.shape[1] % max(PROJ_ROWS, GLA_ROWS, FOX_BLOCK, OUT_ROWS) == 0
    out = _layer(x[0], mem[0], norm_g[0], w_in[0], w_alpha_up[0], b_alpha[0], b_forget[0],
                 gla_norm_g[0], mem_norm_g[0], w_mem_kv[0], w_out[0], final_norm_g)
    return out[None]
```

```python
import functools

import jax
import jax.numpy as jnp
from jax import lax
from jax.experimental import pallas as pl
from jax.experimental.pallas import tpu as pltpu

F32 = jnp.float32
BF16 = jnp.bfloat16

EPS = 1e-6
LANES = 128
SUBLANES = 8

D_MODEL = 1024
GLA_HEADS, GLA_DK, GLA_DV, GLA_RANK = 4, 48, 96, 16
GLA_DK_PAD = 64
GLA_DV_PAD = LANES
GLA_GATE_NORM = 16.0
GLA_CHUNK = 64
FOX_HEADS, FOX_DH = 6, 64
MEM_HEADS, MEM_DH = 4, 64
HEAD_PAIR = 2
GLA_QK_W = GLA_HEADS * GLA_DK_PAD
GLA_V_W = GLA_HEADS * GLA_DV_PAD
FOX_W = FOX_HEADS * FOX_DH
MEM_W = MEM_HEADS * MEM_DH
SMALL_W = LANES
FG_LANE0 = 0
LR_LANE0 = SUBLANES

_GROUPS = (("gq", GLA_QK_W), ("gk", GLA_QK_W), ("gv", GLA_V_W), ("gg", GLA_V_W),
           ("fq", FOX_W), ("fk", FOX_W), ("fv", FOX_W), ("fgate", FOX_W),
           ("mq", MEM_W), ("mg", MEM_W), ("small", SMALL_W))
_OFF = {}
_o = 0
for _n, _w in _GROUPS:
    _OFF[_n] = (_o, _o + _w)
    _o += _w
IN_COLS_PAD = _o

PROJ_ROWS = 512
GLA_ROWS = 512
GLA_UNROLL = 8
FOX_BLOCK = 512
OUT_ROWS = 512
VMEM_LIMIT = 56 * 1024 * 1024

NEG_BIG = -1e30
FOX_SKIP_NATS = 105.0
NORM_SLACK = 1.02


def _log_sigmoid(z):
    return jnp.minimum(z, 0.0) - jnp.log(1.0 + jnp.exp(-jnp.abs(z)))


def _silu(z):
    return z / (1.0 + jnp.exp(-z))


def _rms_scale(v, width):
    return lax.rsqrt(jnp.sum(v * v, axis=-1, keepdims=True) * (1.0 / width) + EPS)


def _w_in_segments():
    qk, gw = GLA_HEADS * GLA_DK, GLA_HEADS * GLA_DV
    src = {}
    o = 0
    for name, width in (("gq", qk), ("gk", qk), ("gv", gw), ("lr", GLA_RANK), ("gg", gw),
                        ("fq", FOX_W), ("fk", FOX_W), ("fv", FOX_W), ("fg", FOX_HEADS),
                        ("fgate", FOX_W), ("mq", MEM_W), ("mg", MEM_W)):
        src[name] = o
        o += width
    segs = []
    for name, d, d_pad in (("gq", GLA_DK, GLA_DK_PAD), ("gk", GLA_DK, GLA_DK_PAD),
                           ("gv", GLA_DV, GLA_DV_PAD), ("gg", GLA_DV, GLA_DV_PAD)):
        segs += [(src[name] + h * d, _OFF[name][0] + h * d_pad, d) for h in range(GLA_HEADS)]
    segs += [(src[name], _OFF[name][0], _OFF[name][1] - _OFF[name][0])
             for name in ("fq", "fk", "fv", "fgate", "mq", "mg")]
    segs += [(src["fg"], _OFF["small"][0] + FG_LANE0, FOX_HEADS),
             (src["lr"], _OFF["small"][0] + LR_LANE0, GLA_RANK)]
    return tuple(segs)


def _proj_kernel(x_ref, g_ref, w_in_ref, wa_ref, ba_ref, bf_ref, seg_ref,
                 gq_ref, gk_ref, gv_ref, gg_ref, fq_ref, fk_ref, fv_ref, fgate_ref,
                 mq_ref, mg_ref, loga_ref, crow_ref, qn2_ref, kn2_ref, carry_ref, wt_ref):
    rows = x_ref.shape[0]
    k_chunks = D_MODEL // LANES

    @pl.when(pl.program_id(0) == 0)
    def _():
        carry_ref[...] = jnp.zeros_like(carry_ref)
        wt_ref[...] = jnp.zeros_like(wt_ref)
        for s0, d0, width in _w_in_segments():
            for c in range(k_chunks):
                wt_ref[d0:d0 + width, c * LANES:(c + 1) * LANES] = (
                    w_in_ref[pl.ds(s0 * k_chunks + c, width, stride=k_chunks), :].astype(BF16))

    x = x_ref[...]
    xn = (x * _rms_scale(x, D_MODEL) * g_ref[...]).astype(BF16)
    nt = (((1,), (1,)), ((), ()))

    def proj(first, last):
        lo, hi = _OFF[first][0], _OFF[last][1]
        y = lax.dot_general(xn, wt_ref[lo:hi, :], nt, preferred_element_type=F32)
        return lambda name: y[:, _OFF[name][0] - lo:_OFF[name][1] - lo]

    tail = proj("mq", "small")
    small = tail("small")
    logf = _log_sigmoid(small + bf_ref[...])
    c = logf.T[0:SUBLANES, :]
    lane = lax.broadcasted_iota(jnp.int32, c.shape, 1)
    shift = 1
    while shift < rows:
        c = c + jnp.where(lane >= shift, pltpu.roll(c, shift, axis=1), 0.0)
        shift *= 2
    c = c + carry_ref[:, 0:1]
    crow_ref[...] = c
    carry_ref[...] = jnp.broadcast_to(c[:, rows - 1:rows], carry_ref.shape)

    z = jnp.dot(small.astype(BF16), wa_ref[...], preferred_element_type=F32) + ba_ref[...]
    loga_ref[...] = _log_sigmoid(z) * (1.0 / GLA_GATE_NORM)
    mq_ref[...] = (tail("mq") * MEM_DH ** -0.5).astype(BF16)
    mg_ref[...] = _silu(tail("mg")).astype(BF16)

    fox = proj("fq", "fgate")
    fq = (fox("fq") * FOX_DH ** -0.5).astype(BF16)
    fk = fox("fk").astype(BF16)
    fq_ref[...] = fq
    fk_ref[...] = fox("fk").T.astype(BF16)
    fv_ref[...] = fox("fv").astype(BF16)
    fgate_ref[...] = _silu(fox("fgate")).astype(BF16)

    def max_sq_norm(v):
        v32 = v.astype(F32)
        n2 = jnp.dot((v32 * v32).astype(BF16), seg_ref[...], preferred_element_type=F32)
        return jnp.max(n2, axis=0, keepdims=True)

    qn2_ref[0] = max_sq_norm(fq)
    kn2_ref[0] = max_sq_norm(fk)

    gla = proj("gq", "gg")
    gq_ref[...] = gla("gq").astype(BF16)
    gk_ref[...] = gla("gk").astype(BF16)
    gv_ref[...] = gla("gv").astype(BF16)
    gg_ref[...] = _silu(gla("gg")).astype(BF16)


def _memkv_kernel(mem_ref, g_ref, w_ref, mk_ref, mv_ref):
    m = mem_ref[...]
    mn = (m * _rms_scale(m, D_MODEL) * g_ref[...]).astype(BF16)
    kv = jnp.dot(mn, w_ref[...], preferred_element_type=F32)
    mk_ref[...] = kv[:, :MEM_W].astype(BF16)
    mv_ref[...] = kv[:, MEM_W:].astype(BF16)


def _gla_kernel(q_ref, k_ref, v_ref, loga_ref, gate_ref, ng_ref, o_ref,
                s_ref, lhs_ref, kv_ref, dec_ref, sprev_ref):
    C = GLA_CHUNK
    W = HEAD_PAIR * GLA_DV_PAD
    n_groups = q_ref.shape[0] // (C * GLA_UNROLL)

    @pl.when(pl.program_id(0) == 0)
    def _():
        s_ref[...] = jnp.zeros_like(s_ref)

    row = lax.broadcasted_iota(jnp.int32, (C, LANES), 0)
    lane = lax.broadcasted_iota(jnp.int32, (C, LANES), 1)
    lo_k = lane < GLA_DK_PAD
    causal = row >= jnp.where(lo_k, lane, lane - GLA_DK_PAD)
    lo_v = lax.broadcasted_iota(jnp.int32, (C, W), 1) < GLA_DV_PAD
    st_row = lax.broadcasted_iota(jnp.int32, (LANES, W), 0)
    st_lane = lax.broadcasted_iota(jnp.int32, (LANES, W), 1)
    own = (st_row < GLA_DK_PAD) == (st_lane < GLA_DV_PAD)
    eye = (lax.broadcasted_iota(jnp.int32, (LANES, LANES), 0)
           == lax.broadcasted_iota(jnp.int32, (LANES, LANES), 1))
    scale = GLA_DK ** -0.5
    nt = (((1,), (1,)), ((), ()))
    tn = (((0,), (0,)), ((), ()))
    ng = jnp.concatenate([ng_ref[...]] * HEAD_PAIR, axis=1)

    pairs = range(GLA_HEADS // HEAD_PAIR)

    def local(ci):
        rs = pl.ds(pl.multiple_of(ci * C, C), C)
        for p in pairs:
            ls = slice(p * LANES, (p + 1) * LANES)
            vs = slice(p * W, (p + 1) * W)
            b = loga_ref[rs, ls]
            shift = 1
            while shift < C:
                b = b + jnp.where(row >= shift, pltpu.roll(b, shift, axis=0), 0.0)
                shift *= 2
            b_last = b[C - 1:C, :]
            k2 = k_ref[rs, ls].astype(F32)
            qd = (q_ref[rs, ls].astype(F32) * scale * jnp.exp(b)).astype(BF16)
            kd = (k2 * jnp.exp(-b)).astype(BF16)
            ke = (k2 * jnp.exp(b_last - b)).astype(BF16)
            zk = jnp.zeros_like(kd)
            kd_blk = jnp.concatenate([jnp.where(lo_k, kd, zk), jnp.where(lo_k, zk, kd)], axis=0)
            attn = lax.dot_general(qd, kd_blk, nt, preferred_element_type=F32)
            lhs_ref[rs, vs] = jnp.concatenate([jnp.where(causal, attn, 0.0).astype(BF16), qd], axis=1)
            kv = lax.dot_general(ke, v_ref[rs, vs], tn, preferred_element_type=F32)
            kv_ref[p, ci] = jnp.where(own, kv, 0.0)
            dcol = jnp.exp(jnp.sum(jnp.where(eye, jnp.broadcast_to(b_last, (LANES, LANES)), 0.0),
                                   axis=1, keepdims=True))
            dec_ref[p, ci] = jnp.broadcast_to(dcol, (LANES, LANES))

    def scan(ci, carry):
        for p in pairs:
            s_prev = s_ref[p]
            sprev_ref[p, ci] = s_prev.astype(BF16)
            s_ref[p] = jnp.tile(dec_ref[p, ci], (1, HEAD_PAIR)) * s_prev + kv_ref[p, ci]
        return carry

    def output(ci):
        rs = pl.ds(pl.multiple_of(ci * C, C), C)
        for p in pairs:
            vs = slice(p * W, (p + 1) * W)
            v2 = v_ref[rs, vs]
            zv = jnp.zeros_like(v2)
            v_blk = jnp.concatenate([jnp.where(lo_v, v2, zv), jnp.where(lo_v, zv, v2)], axis=0)
            o = jnp.dot(lhs_ref[rs, vs], jnp.concatenate([v_blk, sprev_ref[p, ci]], axis=0),
                        preferred_element_type=F32)
            o2 = o * o
            ms = jnp.where(lo_v, jnp.sum(o2[:, :GLA_DV_PAD], axis=1, keepdims=True),
                           jnp.sum(o2[:, GLA_DV_PAD:], axis=1, keepdims=True))
            on = o * lax.rsqrt(ms * (1.0 / GLA_DV) + EPS) * ng
            o_ref[rs, vs] = (on * gate_ref[rs, vs].astype(F32)).astype(BF16)

    def unrolled(fn):
        def group(gi, carry):
            for u in range(GLA_UNROLL):
                fn(gi * GLA_UNROLL + u)
            return carry
        return group

    lax.fori_loop(0, n_groups, unrolled(local), 0)
    lax.fori_loop(0, n_groups * GLA_UNROLL, scan, 0)
    lax.fori_loop(0, n_groups, unrolled(output), 0)


def _fox_kernel(lim_ref, cend_ref, q_ref, k_ref, v_ref, crow_ref, gate_ref, o_ref, m_ref, acc_ref):
    blk = q_ref.shape[0]
    pair = pl.program_id(0)
    qi = pl.program_id(1)
    nblk = pl.num_programs(1)
    q0 = pl.multiple_of(qi * blk, blk)
    lane = lax.broadcasted_iota(jnp.int32, (1, LANES), 1)
    lo_lanes = lane < FOX_DH
    nt = (((1,), (1,)), ((), ()))
    reps = blk // LANES

    q = q_ref[...]
    zero = jnp.zeros_like(q)
    q_stack = jnp.concatenate([jnp.where(lo_lanes, q, zero), jnp.where(lo_lanes, zero, q)], axis=0)
    m_ref[...] = jnp.full_like(m_ref, NEG_BIG)
    acc_ref[...] = jnp.zeros_like(acc_ref)

    def head_row(c8, h):
        sub = lax.broadcasted_iota(jnp.int32, c8.shape, 0)
        return jnp.sum(jnp.where(sub == h, c8, 0.0), axis=0, keepdims=True)

    c_q0 = [head_row(crow_ref[:, pl.ds(q0, LANES)], pair * HEAD_PAIR + hh)[:, 0:1]
            for hh in range(HEAD_PAIR)]

    def step(k0, masked):
        ks = pl.ds(k0, blk)
        kb = k_ref[:, ks]
        vb = v_ref[ks, :]
        one = jnp.ones_like(vb)
        vaug = (jnp.where(lo_lanes, vb, one), jnp.where(lo_lanes, one, vb))
        s_all = jnp.dot(q_stack, kb, preferred_element_type=F32)
        for hh in range(HEAD_PAIR):
            h = pair * HEAD_PAIR + hh
            bias = c_q0[hh] - head_row(crow_ref[:, ks], h)
            s = s_all[hh * blk:(hh + 1) * blk] + bias
            if masked:
                qpos = lax.broadcasted_iota(jnp.int32, (blk, blk), 0)
                kpos = lax.broadcasted_iota(jnp.int32, (blk, blk), 1)
                s = jnp.where(kpos <= qpos, s, NEG_BIG)
            m_prev = m_ref[hh]
            m_new = jnp.maximum(m_prev, jnp.max(s, axis=1, keepdims=True))
            p = jnp.exp(s - jnp.tile(m_new, (1, reps)))
            alpha = jnp.exp(m_prev - m_new)
            pv = jnp.dot(p.astype(BF16), vaug[hh], preferred_element_type=F32)
            acc_ref[hh] = alpha * acc_ref[hh] + pv
            m_ref[hh] = m_new

    @pl.when(qi == 0)
    def _():
        step(q0, masked=True)

    @pl.when(qi > 0)
    def _():
        step(q0, masked=True)
        step(q0 - blk, masked=False)

    m_min = [jnp.min(m_ref[hh]) for hh in range(HEAD_PAIR)]

    def live(j):
        jj = jnp.maximum(j, 0)
        prev = jnp.maximum(qi - 1, 0)
        keep = False
        for hh in range(HEAD_PAIR):
            base = (pair * HEAD_PAIR + hh) * nblk
            bound = (lim_ref[base + qi] + cend_ref[base + prev] - cend_ref[base + jj]) - m_min[hh]
            keep = jnp.logical_or(keep, jnp.logical_not(bound <= 0.0))
        return jnp.logical_and(j >= 0, keep)

    def body(j):
        step(pl.multiple_of(j * blk, blk), masked=False)
        step(pl.multiple_of((j - 1) * blk, blk), masked=False)
        return j - 2

    j_rest = lax.while_loop(lambda j: live(j - 1), body, qi - 2)

    @pl.when(live(j_rest))
    def _():
        step(pl.multiple_of(jnp.maximum(j_rest, 0) * blk, blk), masked=False)

    outs = []
    for hh in range(HEAD_PAIR):
        acc = acc_ref[hh]
        outs.append(acc / pltpu.roll(acc, FOX_DH, axis=1))
    o = jnp.where(lo_lanes, outs[0], outs[1])
    o_ref[...] = (o * gate_ref[...].astype(F32)).astype(BF16)


def _out_kernel(x_ref, gla_ref, fox_ref, mq_ref, mg_ref, mk_ref, mv_ref,
                wg_ref, wf_ref, wm_ref, fg_ref, o_ref):
    lane = lax.broadcasted_iota(jnp.int32, (1, LANES), 1)
    lo_lanes = lane < MEM_DH
    nt = (((1,), (1,)), ((), ()))
    mem_parts = []
    for p in range(MEM_HEADS // HEAD_PAIR):
        ls = slice(p * LANES, (p + 1) * LANES)
        q = mq_ref[:, ls]
        kb = mk_ref[:, ls]
        vb = mv_ref[:, ls]
        zero = jnp.zeros_like(q)
        one = jnp.ones_like(vb)
        qh = (jnp.where(lo_lanes, q, zero), jnp.where(lo_lanes, zero, q))
        vaug = (jnp.where(lo_lanes, vb, one), jnp.where(lo_lanes, one, vb))
        outs = []
        for hh in range(HEAD_PAIR):
            s = lax.dot_general(qh[hh], kb, nt, preferred_element_type=F32)
            pexp = jnp.exp(s - jnp.max(s, axis=1, keepdims=True))
            pv = jnp.dot(pexp.astype(BF16), vaug[hh], preferred_element_type=F32)
            outs.append(pv / pltpu.roll(pv, MEM_DH, axis=1))
        o = jnp.where(lo_lanes, outs[0], outs[1])
        mem_parts.append((o * mg_ref[:, ls].astype(F32)).astype(BF16))
    mem_mixed = jnp.concatenate(mem_parts, axis=1)
    y = (x_ref[...]
         + jnp.dot(gla_ref[...], wg_ref[...], preferred_element_type=F32)
         + jnp.dot(fox_ref[...], wf_ref[...], preferred_element_type=F32)
         + jnp.dot(mem_mixed, wm_ref[...], preferred_element_type=F32))
    o_ref[...] = y * _rms_scale(y, D_MODEL) * fg_ref[...]


def _pad_heads(w, heads, d, d_pad):
    lead = w.shape[:-1]
    w = w.reshape(lead + (heads, d))
    w = jnp.pad(w, [(0, 0)] * len(lead) + [(0, 0), (0, d_pad - d)])
    return w.reshape(lead + (heads * d_pad,))


def _pad_rows(w, heads, d, d_pad):
    return _pad_heads(w.T, heads, d, d_pad).T


def _layout_weights(w_alpha_up, b_alpha, b_forget, gla_norm_g, w_out):
    gw = GLA_HEADS * GLA_DV
    wa = jnp.zeros((SMALL_W, GLA_QK_W), F32)
    wa = wa.at[LR_LANE0:LR_LANE0 + GLA_RANK, :].set(
        _pad_heads(w_alpha_up, GLA_HEADS, GLA_DK, GLA_DK_PAD)).astype(BF16)
    ba = _pad_heads(b_alpha[None, :], GLA_HEADS, GLA_DK, GLA_DK_PAD)
    bf = jnp.zeros((1, SMALL_W), F32).at[0, FG_LANE0:FG_LANE0 + FOX_HEADS].set(b_forget)
    ng = jnp.pad(gla_norm_g, (0, GLA_DV_PAD - GLA_DV))[None, :]
    wo_g = _pad_rows(w_out[:gw], GLA_HEADS, GLA_DV, GLA_DV_PAD).astype(BF16)
    wo_f = w_out[gw:gw + FOX_W].astype(BF16)
    wo_m = w_out[gw + FOX_W:].astype(BF16)
    return wa, ba, bf, ng, wo_g, wo_f, wo_m


def _fox_limits(qn2, kn2, crow, blk):
    nblk = crow.shape[1] // blk
    nproj = qn2.shape[0]
    assert nblk % nproj == 0
    qn = jnp.sqrt(qn2[:, 0, :FOX_HEADS])
    kmax = jnp.sqrt(jnp.max(kn2[:, 0, :FOX_HEADS], axis=0))
    lim = FOX_SKIP_NATS + NORM_SLACK * jnp.repeat(qn, nblk // nproj, axis=0) * kmax[None, :]
    cend = crow[:FOX_HEADS, blk - 1::blk]
    return lim.T.reshape(-1), cend.reshape(-1)


def _params(*sem):
    return pltpu.CompilerParams(dimension_semantics=sem, vmem_limit_bytes=VMEM_LIMIT)


def _layer(x, mem, norm_g, w_in, w_alpha_up, b_alpha, b_forget, gla_norm_g,
           mem_norm_g, w_mem_kv, w_out, out_g):
    w_in_t = jnp.transpose(w_in[None], (0, 2, 1)).reshape(-1, LANES)
    T = x.shape[0]
    M = mem.shape[0]
    wa, ba, bf, ng, wo_g, wo_f, wo_m = _layout_weights(
        w_alpha_up, b_alpha, b_forget, gla_norm_g, w_out)

    def rows(width, n=PROJ_ROWS):
        return pl.BlockSpec((n, width), lambda i: (i, 0))

    def whole(shape):
        return pl.BlockSpec(shape, lambda i: (0,) * len(shape))

    bshape = lambda w: jax.ShapeDtypeStruct((T, w), BF16)
    nproj = T // PROJ_ROWS
    stat_spec = pl.BlockSpec((1, 1, LANES), lambda i: (i, 0, 0))
    stat_shape = jax.ShapeDtypeStruct((nproj, 1, LANES), F32)
    seg = (jnp.arange(FOX_W)[:, None] // FOX_DH == jnp.arange(LANES)[None, :]).astype(BF16)
    (gq, gk, gv, gg, fq, fk, fv, fgate, mq, mg, loga, crow, qn2, kn2) = pl.pallas_call(
        _proj_kernel,
        grid=(nproj,),
        in_specs=[rows(D_MODEL), whole((1, D_MODEL)),
                  pl.BlockSpec(w_in_t.shape, lambda i: (0, 0), pipeline_mode=pl.Buffered(1)),
                  whole((SMALL_W, GLA_QK_W)), whole((1, GLA_QK_W)), whole((1, SMALL_W)),
                  whole((FOX_W, LANES))],
        out_specs=[rows(GLA_QK_W), rows(GLA_QK_W), rows(GLA_V_W), rows(GLA_V_W),
                   rows(FOX_W), pl.BlockSpec((FOX_W, PROJ_ROWS), lambda i: (0, i)),
                   rows(FOX_W), rows(FOX_W),
                   rows(MEM_W), rows(MEM_W), rows(GLA_QK_W),
                   pl.BlockSpec((SUBLANES, PROJ_ROWS), lambda i: (0, i)),
                   stat_spec, stat_spec],
        out_shape=[bshape(GLA_QK_W), bshape(GLA_QK_W), bshape(GLA_V_W), bshape(GLA_V_W),
                   bshape(FOX_W), jax.ShapeDtypeStruct((FOX_W, T), BF16),
                   bshape(FOX_W), bshape(FOX_W),
                   bshape(MEM_W), bshape(MEM_W),
                   jax.ShapeDtypeStruct((T, GLA_QK_W), F32),
                   jax.ShapeDtypeStruct((SUBLANES, T), F32),
                   stat_shape, stat_shape],
        scratch_shapes=[pltpu.VMEM((SUBLANES, LANES), F32),
                        pltpu.VMEM((IN_COLS_PAD, D_MODEL), BF16)],
        compiler_params=_params("arbitrary"),
        name="proj",
    )(x, norm_g[None, :], w_in_t, wa, ba, bf, seg)

    mk, mv = pl.pallas_call(
        _memkv_kernel,
        out_shape=[jax.ShapeDtypeStruct((M, MEM_W), BF16)] * 2,
        compiler_params=pltpu.CompilerParams(vmem_limit_bytes=VMEM_LIMIT),
        name="memkv",
    )(mem, mem_norm_g[None, :], w_mem_kv.astype(BF16))

    gla_pairs, gla_chunks = GLA_HEADS // HEAD_PAIR, GLA_ROWS // GLA_CHUNK
    pair_w = HEAD_PAIR * GLA_DV_PAD
    gla = pl.pallas_call(
        _gla_kernel,
        grid=(T // GLA_ROWS,),
        in_specs=[rows(GLA_QK_W, GLA_ROWS), rows(GLA_QK_W, GLA_ROWS), rows(GLA_V_W, GLA_ROWS),
                  rows(GLA_QK_W, GLA_ROWS), rows(GLA_V_W, GLA_ROWS), whole((1, GLA_DV_PAD))],
        out_specs=rows(GLA_V_W, GLA_ROWS),
        out_shape=bshape(GLA_V_W),
        scratch_shapes=[
            pltpu.VMEM((gla_pairs, LANES, pair_w), F32),
            pltpu.VMEM((GLA_ROWS, GLA_V_W), BF16),
            pltpu.VMEM((gla_pairs, gla_chunks, LANES, pair_w), F32),
            pltpu.VMEM((gla_pairs, gla_chunks, LANES, LANES), F32),
            pltpu.VMEM((gla_pairs, gla_chunks, LANES, pair_w), BF16)],
        compiler_params=_params("arbitrary"),
        name="gla",
    )(gq, gk, gv, loga, gg, ng)

    lim, cend = _fox_limits(qn2, kn2, crow, FOX_BLOCK)
    pair_rows = pl.BlockSpec((FOX_BLOCK, LANES), lambda p, i, *_: (i, p))
    pair_all = pl.BlockSpec((T, LANES), lambda p, i, *_: (0, p))
    fox = pl.pallas_call(
        _fox_kernel,
        grid_spec=pltpu.PrefetchScalarGridSpec(
            num_scalar_prefetch=2,
            grid=(FOX_HEADS // HEAD_PAIR, T // FOX_BLOCK),
            in_specs=[pair_rows, pl.BlockSpec((LANES, T), lambda p, i, *_: (p, 0)), pair_all,
                      pl.BlockSpec((SUBLANES, T), lambda p, i, *_: (0, 0)), pair_rows],
            out_specs=pair_rows,
            scratch_shapes=[pltpu.VMEM((HEAD_PAIR, FOX_BLOCK, LANES), F32),
                            pltpu.VMEM((HEAD_PAIR, FOX_BLOCK, LANES), F32)]),
        out_shape=bshape(FOX_W),
        compiler_params=_params("arbitrary", "arbitrary"),
        name="fox",
    )(lim, cend, fq, fk, fv, crow, fgate)

    out = pl.pallas_call(
        _out_kernel,
        grid=(T // OUT_ROWS,),
        in_specs=[rows(D_MODEL, OUT_ROWS), rows(GLA_V_W, OUT_ROWS), rows(FOX_W, OUT_ROWS),
                  rows(MEM_W, OUT_ROWS), rows(MEM_W, OUT_ROWS),
                  whole((M, MEM_W)), whole((M, MEM_W)),
                  whole((GLA_V_W, D_MODEL)), whole((FOX_W, D_MODEL)), whole((MEM_W, D_MODEL)),
                  whole((1, D_MODEL))],
        out_specs=rows(D_MODEL, OUT_ROWS),
        out_shape=jax.ShapeDtypeStruct((T, D_MODEL), F32),
        compiler_params=_params("arbitrary"),
        name="out",
    )(x, gla, fox, mq, mg, mk, mv, wo_g, wo_f, wo_m, out_g[None, :])
    return out


def kernel(x, mem, norm_g, w_in, w_alpha_up, b_alpha, b_forget, gla_norm_g, mem_norm_g,
           w_mem_kv, w_out, final_norm_g):
    assert x.shape[0] == 1 and mem.shape[0] == 1 and norm_g.shape[0] == 1
    assert x.shape[1] % max(PROJ_ROWS, GLA_ROWS, FOX_BLOCK, OUT_ROWS) == 0
    out = _layer(x[0], mem[0], norm_g[0], w_in[0], w_alpha_up[0], b_alpha[0], b_forget[0],
                 gla_norm_g[0], mem_norm_g[0], w_mem_kv[0], w_out[0], final_norm_g)
    return out[None]
```

```python
import functools

import jax
import jax.numpy as jnp
from jax import lax
from jax.experimental import pallas as pl
from jax.experimental.pallas import tpu as pltpu

F32 = jnp.float32
BF16 = jnp.bfloat16

EPS = 1e-6
LANES = 128
SUBLANES = 8

D_MODEL = 1024
GLA_HEADS, GLA_DK, GLA_DV, GLA_RANK = 4, 48, 96, 16
GLA_DK_PAD = 64
GLA_DV_PAD = LANES
GLA_GATE_NORM = 16.0
GLA_CHUNK = 64
FOX_HEADS, FOX_DH = 6, 64
MEM_HEADS, MEM_DH = 4, 64
HEAD_PAIR = 2
GLA_QK_W = GLA_HEADS * GLA_DK_PAD
GLA_V_W = GLA_HEADS * GLA_DV_PAD
FOX_W = FOX_HEADS * FOX_DH
MEM_W = MEM_HEADS * MEM_DH
SMALL_W = LANES
FG_LANE0 = 0
LR_LANE0 = SUBLANES

_GROUPS = (("gq", GLA_QK_W), ("gk", GLA_QK_W), ("gv", GLA_V_W), ("gg", GLA_V_W),
           ("fq", FOX_W), ("fk", FOX_W), ("fv", FOX_W), ("fgate", FOX_W),
           ("mq", MEM_W), ("mg", MEM_W), ("small", SMALL_W))
_OFF = {}
_o = 0
for _n, _w in _GROUPS:
    _OFF[_n] = (_o, _o + _w)
    _o += _w
IN_COLS_PAD = _o

PROJ_ROWS = 512
GLA_ROWS = 512
GLA_UNROLL = 8
FOX_BLOCK = 512
OUT_ROWS = 512
VMEM_LIMIT = 56 * 1024 * 1024

NEG_BIG = -1e30
FOX_SKIP_NATS = 105.0
NORM_SLACK = 1.02


def _log_sigmoid(z):
    return jnp.minimum(z, 0.0) - jnp.log(1.0 + jnp.exp(-jnp.abs(z)))


def _silu(z):
    return z / (1.0 + jnp.exp(-z))


def _rms_scale(v, width):
    return lax.rsqrt(jnp.sum(v * v, axis=-1, keepdims=True) * (1.0 / width) + EPS)


def _w_in_segments():
    qk, gw = GLA_HEADS * GLA_DK, GLA_HEADS * GLA_DV
    src = {}
    o = 0
    for name, width in (("gq", qk), ("gk", qk), ("gv", gw), ("lr", GLA_RANK), ("gg", gw),
                        ("fq", FOX_W), ("fk", FOX_W), ("fv", FOX_W), ("fg", FOX_HEADS),
                        ("fgate", FOX_W), ("mq", MEM_W), ("mg", MEM_W)):
        src[name] = o
        o += width
    segs = []
    for name, d, d_pad in (("gq", GLA_DK, GLA_DK_PAD), ("gk", GLA_DK, GLA_DK_PAD),
                           ("gv", GLA_DV, GLA_DV_PAD), ("gg", GLA_DV, GLA_DV_PAD)):
        segs += [(src[name] + h * d, _OFF[name][0] + h * d_pad, d) for h in range(GLA_HEADS)]
    segs += [(src[name], _OFF[name][0], _OFF[name][1] - _OFF[name][0])
             for name in ("fq", "fk", "fv", "fgate", "mq", "mg")]
    segs += [(src["fg"], _OFF["small"][0] + FG_LANE0, FOX_HEADS),
             (src["lr"], _OFF["small"][0] + LR_LANE0, GLA_RANK)]
    return tuple(segs)


def _proj_kernel(x_ref, g_ref, w_in_ref, wa_ref, ba_ref, bf_ref, seg_ref, ng_ref,
                 gla_ref, fq_ref, fk_ref, fv_ref, fgate_ref,
                 mq_ref, mg_ref, crow_ref, qn2_ref, kn2_ref,
                 carry_ref, wt_ref, gq_ref, gk_ref, gv_ref, gg_ref, loga_ref,
                 s_ref, lhs_ref, kv_ref, dec_ref, sprev_ref):
    rows = x_ref.shape[0]
    k_chunks = D_MODEL // LANES

    @pl.when(pl.program_id(0) == 0)
    def _():
        carry_ref[...] = jnp.zeros_like(carry_ref)
        s_ref[...] = jnp.zeros_like(s_ref)
        wt_ref[...] = jnp.zeros_like(wt_ref)
        for s0, d0, width in _w_in_segments():
            for c in range(k_chunks):
                wt_ref[d0:d0 + width, c * LANES:(c + 1) * LANES] = (
                    w_in_ref[pl.ds(s0 * k_chunks + c, width, stride=k_chunks), :].astype(BF16))

    x = x_ref[...]
    xn = (x * _rms_scale(x, D_MODEL) * g_ref[...]).astype(BF16)
    nt = (((1,), (1,)), ((), ()))

    def proj(first, last):
        lo, hi = _OFF[first][0], _OFF[last][1]
        y = lax.dot_general(xn, wt_ref[lo:hi, :], nt, preferred_element_type=F32)
        return lambda name: y[:, _OFF[name][0] - lo:_OFF[name][1] - lo]

    tail = proj("mq", "small")
    small = tail("small")
    logf = _log_sigmoid(small + bf_ref[...])
    c = logf.T[0:SUBLANES, :]
    lane = lax.broadcasted_iota(jnp.int32, c.shape, 1)
    shift = 1
    while shift < rows:
        c = c + jnp.where(lane >= shift, pltpu.roll(c, shift, axis=1), 0.0)
        shift *= 2
    c = c + carry_ref[:, 0:1]
    crow_ref[...] = c
    carry_ref[...] = jnp.broadcast_to(c[:, rows - 1:rows], carry_ref.shape)

    z = jnp.dot(small.astype(BF16), wa_ref[...], preferred_element_type=F32) + ba_ref[...]
    loga_ref[...] = _log_sigmoid(z) * (1.0 / GLA_GATE_NORM)
    gla = proj("gq", "gg")
    gq_ref[...] = gla("gq").astype(BF16)
    gk_ref[...] = gla("gk").astype(BF16)
    gv_ref[...] = gla("gv").astype(BF16)
    gg_ref[...] = _silu(gla("gg")).astype(BF16)
    _gla_block(gq_ref, gk_ref, gv_ref, loga_ref, gg_ref, ng_ref, gla_ref,
               s_ref, lhs_ref, kv_ref, dec_ref, sprev_ref)

    mq_ref[...] = (tail("mq") * MEM_DH ** -0.5).astype(BF16)
    mg_ref[...] = _silu(tail("mg")).astype(BF16)

    fox = proj("fq", "fgate")
    fq = (fox("fq") * FOX_DH ** -0.5).astype(BF16)
    fk = fox("fk").astype(BF16)
    fq_ref[...] = fq
    fk_ref[...] = fox("fk").T.astype(BF16)
    fv_ref[...] = fox("fv").astype(BF16)
    fgate_ref[...] = _silu(fox("fgate")).astype(BF16)

    def max_sq_norm(v):
        v32 = v.astype(F32)
        n2 = jnp.dot((v32 * v32).astype(BF16), seg_ref[...], preferred_element_type=F32)
        return jnp.max(n2, axis=0, keepdims=True)

    qn2_ref[0] = max_sq_norm(fq)
    kn2_ref[0] = max_sq_norm(fk)


def _memkv_kernel(mem_ref, g_ref, w_ref, mk_ref, mv_ref):
    m = mem_ref[...]
    mn = (m * _rms_scale(m, D_MODEL) * g_ref[...]).astype(BF16)
    kv = jnp.dot(mn, w_ref[...], preferred_element_type=F32)
    mk_ref[...] = kv[:, :MEM_W].astype(BF16)
    mv_ref[...] = kv[:, MEM_W:].astype(BF16)


def _gla_block(q_ref, k_ref, v_ref, loga_ref, gate_ref, ng_ref, o_ref,
               s_ref, lhs_ref, kv_ref, dec_ref, sprev_ref):
    C = GLA_CHUNK
    W = HEAD_PAIR * GLA_DV_PAD
    n_chunks = q_ref.shape[0] // C

    row = lax.broadcasted_iota(jnp.int32, (C, LANES), 0)
    lane = lax.broadcasted_iota(jnp.int32, (C, LANES), 1)
    lo_k = lane < GLA_DK_PAD
    causal = row >= jnp.where(lo_k, lane, lane - GLA_DK_PAD)
    lo_v = lax.broadcasted_iota(jnp.int32, (C, W), 1) < GLA_DV_PAD
    st_row = lax.broadcasted_iota(jnp.int32, (LANES, W), 0)
    st_lane = lax.broadcasted_iota(jnp.int32, (LANES, W), 1)
    own = (st_row < GLA_DK_PAD) == (st_lane < GLA_DV_PAD)
    eye = (lax.broadcasted_iota(jnp.int32, (LANES, LANES), 0)
           == lax.broadcasted_iota(jnp.int32, (LANES, LANES), 1))
    scale = GLA_DK ** -0.5
    nt = (((1,), (1,)), ((), ()))
    tn = (((0,), (0,)), ((), ()))
    ng = jnp.concatenate([ng_ref[...]] * HEAD_PAIR, axis=1)

    pairs = range(GLA_HEADS // HEAD_PAIR)

    def local(ci):
        rs = slice(ci * C, (ci + 1) * C)
        for p in pairs:
            ls = slice(p * LANES, (p + 1) * LANES)
            vs = slice(p * W, (p + 1) * W)
            b = loga_ref[rs, ls]
            shift = 1
            while shift < C:
                b = b + jnp.where(row >= shift, pltpu.roll(b, shift, axis=0), 0.0)
                shift *= 2
            b_last = b[C - 1:C, :]
            k2 = k_ref[rs, ls].astype(F32)
            qd = (q_ref[rs, ls].astype(F32) * scale * jnp.exp(b)).astype(BF16)
            kd = (k2 * jnp.exp(-b)).astype(BF16)
            ke = (k2 * jnp.exp(b_last - b)).astype(BF16)
            zk = jnp.zeros_like(kd)
            kd_blk = jnp.concatenate([jnp.where(lo_k, kd, zk), jnp.where(lo_k, zk, kd)], axis=0)
            attn = lax.dot_general(qd, kd_blk, nt, preferred_element_type=F32)
            lhs_ref[rs, vs] = jnp.concatenate([jnp.where(causal, attn, 0.0).astype(BF16), qd], axis=1)
            kv = lax.dot_general(ke, v_ref[rs, vs], tn, preferred_element_type=F32)
            kv_ref[p, ci] = jnp.where(own, kv, 0.0)
            dcol = jnp.exp(jnp.sum(jnp.where(eye, jnp.broadcast_to(b_last, (LANES, LANES)), 0.0),
                                   axis=1, keepdims=True))
            dec_ref[p, ci] = jnp.broadcast_to(dcol, (LANES, LANES))

    def scan(ci):
        for p in pairs:
            s_prev = s_ref[p]
            sprev_ref[p, ci] = s_prev.astype(BF16)
            s_ref[p] = jnp.tile(dec_ref[p, ci], (1, HEAD_PAIR)) * s_prev + kv_ref[p, ci]

    def output(ci):
        rs = slice(ci * C, (ci + 1) * C)
        for p in pairs:
            vs = slice(p * W, (p + 1) * W)
            v2 = v_ref[rs, vs]
            zv = jnp.zeros_like(v2)
            v_blk = jnp.concatenate([jnp.where(lo_v, v2, zv), jnp.where(lo_v, zv, v2)], axis=0)
            o = jnp.dot(lhs_ref[rs, vs], jnp.concatenate([v_blk, sprev_ref[p, ci]], axis=0),
                        preferred_element_type=F32)
            o2 = o * o
            ms = jnp.where(lo_v, jnp.sum(o2[:, :GLA_DV_PAD], axis=1, keepdims=True),
                           jnp.sum(o2[:, GLA_DV_PAD:], axis=1, keepdims=True))
            on = o * lax.rsqrt(ms * (1.0 / GLA_DV) + EPS) * ng
            o_ref[rs, vs] = (on * gate_ref[rs, vs].astype(F32)).astype(BF16)

    for phase in (local, scan, output):
        for ci in range(n_chunks):
            phase(ci)


def _fox_kernel(lim_ref, cend_ref, q_ref, k_ref, v_ref, crow_ref, gate_ref, o_ref, m_ref, acc_ref):
    blk = q_ref.shape[0]
    pair = pl.program_id(0)
    qi = pl.program_id(1)
    nblk = pl.num_programs(1)
    q0 = pl.multiple_of(qi * blk, blk)
    lane = lax.broadcasted_iota(jnp.int32, (1, LANES), 1)
    lo_lanes = lane < FOX_DH
    nt = (((1,), (1,)), ((), ()))
    reps = blk // LANES

    q = q_ref[...]
    zero = jnp.zeros_like(q)
    q_stack = jnp.concatenate([jnp.where(lo_lanes, q, zero), jnp.where(lo_lanes, zero, q)], axis=0)
    m_ref[...] = jnp.full_like(m_ref, NEG_BIG)
    acc_ref[...] = jnp.zeros_like(acc_ref)

    def head_row(c8, h):
        sub = lax.broadcasted_iota(jnp.int32, c8.shape, 0)
        return jnp.sum(jnp.where(sub == h, c8, 0.0), axis=0, keepdims=True)

    c_q0 = [head_row(crow_ref[:, pl.ds(q0, LANES)], pair * HEAD_PAIR + hh)[:, 0:1]
            for hh in range(HEAD_PAIR)]

    def step(k0, masked):
        ks = pl.ds(k0, blk)
        kb = k_ref[:, ks]
        vb = v_ref[ks, :]
        one = jnp.ones_like(vb)
        vaug = (jnp.where(lo_lanes, vb, one), jnp.where(lo_lanes, one, vb))
        s_all = jnp.dot(q_stack, kb, preferred_element_type=F32)
        for hh in range(HEAD_PAIR):
            h = pair * HEAD_PAIR + hh
            bias = c_q0[hh] - head_row(crow_ref[:, ks], h)
            s = s_all[hh * blk:(hh + 1) * blk] + bias
            if masked:
                qpos = lax.broadcasted_iota(jnp.int32, (blk, blk), 0)
                kpos = lax.broadcasted_iota(jnp.int32, (blk, blk), 1)
                s = jnp.where(kpos <= qpos, s, NEG_BIG)
            m_prev = m_ref[hh]
            m_new = jnp.maximum(m_prev, jnp.max(s, axis=1, keepdims=True))
            p = jnp.exp(s - jnp.tile(m_new, (1, reps)))
            alpha = jnp.exp(m_prev - m_new)
            pv = jnp.dot(p.astype(BF16), vaug[hh], preferred_element_type=F32)
            acc_ref[hh] = alpha * acc_ref[hh] + pv
            m_ref[hh] = m_new

    @pl.when(qi == 0)
    def _():
        step(q0, masked=True)

    @pl.when(qi > 0)
    def _():
        step(q0, masked=True)
        step(q0 - blk, masked=False)

    m_min = [jnp.min(m_ref[hh]) for hh in range(HEAD_PAIR)]

    def live(j):
        jj = jnp.maximum(j, 0)
        prev = jnp.maximum(qi - 1, 0)
        keep = False
        for hh in range(HEAD_PAIR):
            base = (pair * HEAD_PAIR + hh) * nblk
            bound = (lim_ref[base + qi] + cend_ref[base + prev] - cend_ref[base + jj]) - m_min[hh]
            keep = jnp.logical_or(keep, jnp.logical_not(bound <= 0.0))
        return jnp.logical_and(j >= 0, keep)

    def body(j):
        step(pl.multiple_of(j * blk, blk), masked=False)
        step(pl.multiple_of((j - 1) * blk, blk), masked=False)
        return j - 2

    j_rest = lax.while_loop(lambda j: live(j - 1), body, qi - 2)

    @pl.when(live(j_rest))
    def _():
        step(pl.multiple_of(jnp.maximum(j_rest, 0) * blk, blk), masked=False)

    outs = []
    for hh in range(HEAD_PAIR):
        acc = acc_ref[hh]
        outs.append(acc / pltpu.roll(acc, FOX_DH, axis=1))
    o = jnp.where(lo_lanes, outs[0], outs[1])
    o_ref[...] = (o * gate_ref[...].astype(F32)).astype(BF16)


def _out_kernel(x_ref, gla_ref, fox_ref, mq_ref, mg_ref, mk_ref, mv_ref,
                wg_ref, wf_ref, wm_ref, fg_ref, o_ref):
    lane = lax.broadcasted_iota(jnp.int32, (1, LANES), 1)
    lo_lanes = lane < MEM_DH
    nt = (((1,), (1,)), ((), ()))
    mem_parts = []
    for p in range(MEM_HEADS // HEAD_PAIR):
        ls = slice(p * LANES, (p + 1) * LANES)
        q = mq_ref[:, ls]
        kb = mk_ref[:, ls]
        vb = mv_ref[:, ls]
        zero = jnp.zeros_like(q)
        one = jnp.ones_like(vb)
        qh = (jnp.where(lo_lanes, q, zero), jnp.where(lo_lanes, zero, q))
        vaug = (jnp.where(lo_lanes, vb, one), jnp.where(lo_lanes, one, vb))
        outs = []
        for hh in range(HEAD_PAIR):
            s = lax.dot_general(qh[hh], kb, nt, preferred_element_type=F32)
            pexp = jnp.exp(s - jnp.max(s, axis=1, keepdims=True))
            pv = jnp.dot(pexp.astype(BF16), vaug[hh], preferred_element_type=F32)
            outs.append(pv / pltpu.roll(pv, MEM_DH, axis=1))
        o = jnp.where(lo_lanes, outs[0], outs[1])
        mem_parts.append((o * mg_ref[:, ls].astype(F32)).astype(BF16))
    mem_mixed = jnp.concatenate(mem_parts, axis=1)
    y = (x_ref[...]
         + jnp.dot(gla_ref[...], wg_ref[...], preferred_element_type=F32)
         + jnp.dot(fox_ref[...], wf_ref[...], preferred_element_type=F32)
         + jnp.dot(mem_mixed, wm_ref[...], preferred_element_type=F32))
    o_ref[...] = y * _rms_scale(y, D_MODEL) * fg_ref[...]


def _pad_heads(w, heads, d, d_pad):
    lead = w.shape[:-1]
    w = w.reshape(lead + (heads, d))
    w = jnp.pad(w, [(0, 0)] * len(lead) + [(0, 0), (0, d_pad - d)])
    return w.reshape(lead + (heads * d_pad,))


def _pad_rows(w, heads, d, d_pad):
    return _pad_heads(w.T, heads, d, d_pad).T


def _layout_weights(w_alpha_up, b_alpha, b_forget, gla_norm_g, w_out):
    gw = GLA_HEADS * GLA_DV
    wa = jnp.zeros((SMALL_W, GLA_QK_W), F32)
    wa = wa.at[LR_LANE0:LR_LANE0 + GLA_RANK, :].set(
        _pad_heads(w_alpha_up, GLA_HEADS, GLA_DK, GLA_DK_PAD)).astype(BF16)
    ba = _pad_heads(b_alpha[None, :], GLA_HEADS, GLA_DK, GLA_DK_PAD)
    bf = jnp.zeros((1, SMALL_W), F32).at[0, FG_LANE0:FG_LANE0 + FOX_HEADS].set(b_forget)
    ng = jnp.pad(gla_norm_g, (0, GLA_DV_PAD - GLA_DV))[None, :]
    wo_g = _pad_rows(w_out[:gw], GLA_HEADS, GLA_DV, GLA_DV_PAD).astype(BF16)
    wo_f = w_out[gw:gw + FOX_W].astype(BF16)
    wo_m = w_out[gw + FOX_W:].astype(BF16)
    return wa, ba, bf, ng, wo_g, wo_f, wo_m


def _fox_limits(qn2, kn2, crow, blk):
    nblk = crow.shape[1] // blk
    nproj = qn2.shape[0]
    assert nblk % nproj == 0
    qn = jnp.sqrt(qn2[:, 0, :FOX_HEADS])
    kmax = jnp.sqrt(jnp.max(kn2[:, 0, :FOX_HEADS], axis=0))
    lim = FOX_SKIP_NATS + NORM_SLACK * jnp.repeat(qn, nblk // nproj, axis=0) * kmax[None, :]
    cend = crow[:FOX_HEADS, blk - 1::blk]
    return lim.T.reshape(-1), cend.reshape(-1)


def _params(*sem):
    return pltpu.CompilerParams(dimension_semantics=sem, vmem_limit_bytes=VMEM_LIMIT)


def _layer(x, mem, norm_g, w_in, w_alpha_up, b_alpha, b_forget, gla_norm_g,
           mem_norm_g, w_mem_kv, w_out, out_g):
    w_in_t = jnp.transpose(w_in[None], (0, 2, 1)).reshape(-1, LANES)
    T = x.shape[0]
    M = mem.shape[0]
    wa, ba, bf, ng, wo_g, wo_f, wo_m = _layout_weights(
        w_alpha_up, b_alpha, b_forget, gla_norm_g, w_out)

    def rows(width, n=PROJ_ROWS):
        return pl.BlockSpec((n, width), lambda i: (i, 0))

    def whole(shape):
        return pl.BlockSpec(shape, lambda i: (0,) * len(shape))

    bshape = lambda w: jax.ShapeDtypeStruct((T, w), BF16)
    nproj = T // PROJ_ROWS
    stat_spec = pl.BlockSpec((1, 1, LANES), lambda i: (i, 0, 0))
    stat_shape = jax.ShapeDtypeStruct((nproj, 1, LANES), F32)
    seg = (jnp.arange(FOX_W)[:, None] // FOX_DH == jnp.arange(LANES)[None, :]).astype(BF16)
    gla_pairs, gla_chunks = GLA_HEADS // HEAD_PAIR, PROJ_ROWS // GLA_CHUNK
    pair_w = HEAD_PAIR * GLA_DV_PAD
    (gla, fq, fk, fv, fgate, mq, mg, crow, qn2, kn2) = pl.pallas_call(
        _proj_kernel,
        grid=(nproj,),
        in_specs=[rows(D_MODEL), whole((1, D_MODEL)),
                  pl.BlockSpec(w_in_t.shape, lambda i: (0, 0), pipeline_mode=pl.Buffered(1)),
                  whole((SMALL_W, GLA_QK_W)), whole((1, GLA_QK_W)), whole((1, SMALL_W)),
                  whole((FOX_W, LANES)), whole((1, GLA_DV_PAD))],
        out_specs=[rows(GLA_V_W),
                   rows(FOX_W), pl.BlockSpec((FOX_W, PROJ_ROWS), lambda i: (0, i)),
                   rows(FOX_W), rows(FOX_W),
                   rows(MEM_W), rows(MEM_W),
                   pl.BlockSpec((SUBLANES, PROJ_ROWS), lambda i: (0, i)),
                   stat_spec, stat_spec],
        out_shape=[bshape(GLA_V_W),
                   bshape(FOX_W), jax.ShapeDtypeStruct((FOX_W, T), BF16),
                   bshape(FOX_W), bshape(FOX_W),
                   bshape(MEM_W), bshape(MEM_W),
                   jax.ShapeDtypeStruct((SUBLANES, T), F32),
                   stat_shape, stat_shape],
        scratch_shapes=[
            pltpu.VMEM((SUBLANES, LANES), F32),
            pltpu.VMEM((IN_COLS_PAD, D_MODEL), BF16),
            pltpu.VMEM((PROJ_ROWS, GLA_QK_W), BF16),
            pltpu.VMEM((PROJ_ROWS, GLA_QK_W), BF16),
            pltpu.VMEM((PROJ_ROWS, GLA_V_W), BF16),
            pltpu.VMEM((PROJ_ROWS, GLA_V_W), BF16),
            pltpu.VMEM((PROJ_ROWS, GLA_QK_W), F32),
            pltpu.VMEM((gla_pairs, LANES, pair_w), F32),
            pltpu.VMEM((PROJ_ROWS, GLA_V_W), BF16),
            pltpu.VMEM((gla_pairs, gla_chunks, LANES, pair_w), F32),
            pltpu.VMEM((gla_pairs, gla_chunks, LANES, LANES), F32),
            pltpu.VMEM((gla_pairs, gla_chunks, LANES, pair_w), BF16)],
        compiler_params=_params("arbitrary"),
        name="proj",
    )(x, norm_g[None, :], w_in_t, wa, ba, bf, seg, ng)

    mk, mv = pl.pallas_call(
        _memkv_kernel,
        out_shape=[jax.ShapeDtypeStruct((M, MEM_W), BF16)] * 2,
        compiler_params=pltpu.CompilerParams(vmem_limit_bytes=VMEM_LIMIT),
        name="memkv",
    )(mem, mem_norm_g[None, :], w_mem_kv.astype(BF16))

    lim, cend = _fox_limits(qn2, kn2, crow, FOX_BLOCK)
    pair_rows = pl.BlockSpec((FOX_BLOCK, LANES), lambda p, i, *_: (i, p))
    pair_all = pl.BlockSpec((T, LANES), lambda p, i, *_: (0, p))
    fox = pl.pallas_call(
        _fox_kernel,
        grid_spec=pltpu.PrefetchScalarGridSpec(
            num_scalar_prefetch=2,
            grid=(FOX_HEADS // HEAD_PAIR, T // FOX_BLOCK),
            in_specs=[pair_rows, pl.BlockSpec((LANES, T), lambda p, i, *_: (p, 0)), pair_all,
                      pl.BlockSpec((SUBLANES, T), lambda p, i, *_: (0, 0)), pair_rows],
            out_specs=pair_rows,
            scratch_shapes=[pltpu.VMEM((HEAD_PAIR, FOX_BLOCK, LANES), F32),
                            pltpu.VMEM((HEAD_PAIR, FOX_BLOCK, LANES), F32)]),
        out_shape=bshape(FOX_W),
        compiler_params=_params("arbitrary", "arbitrary"),
        name="fox",
    )(lim, cend, fq, fk, fv, crow, fgate)

    out = pl.pallas_call(
        _out_kernel,
        grid=(T // OUT_ROWS,),
        in_specs=[rows(D_MODEL, OUT_ROWS), rows(GLA_V_W, OUT_ROWS), rows(FOX_W, OUT_ROWS),
                  rows(MEM_W, OUT_ROWS), rows(MEM_W, OUT_ROWS),
                  whole((M, MEM_W)), whole((M, MEM_W)),
                  whole((GLA_V_W, D_MODEL)), whole((FOX_W, D_MODEL)), whole((MEM_W, D_MODEL)),
                  whole((1, D_MODEL))],
        out_specs=rows(D_MODEL, OUT_ROWS),
        out_shape=jax.ShapeDtypeStruct((T, D_MODEL), F32),
        compiler_params=_params("arbitrary"),
        name="out",
    )(x, gla, fox, mq, mg, mk, mv, wo_g, wo_f, wo_m, out_g[None, :])
    return out


def kernel(x, mem, norm_g, w_in, w_alpha_up, b_alpha, b_forget, gla_norm_g, mem_norm_g,
           w_mem_kv, w_out, final_norm_g):
    assert x.shape[0] == 1 and mem.shape[0] == 1 and norm_g.shape[0] == 1
    assert x.shape[1] % max(PROJ_ROWS, GLA_ROWS, FOX_BLOCK, OUT_ROWS) == 0
    out = _layer(x[0], mem[0], norm_g[0], w_in[0], w_alpha_up[0], b_alpha[0], b_forget[0],
                 gla_norm_g[0], mem_norm_g[0], w_mem_kv[0], w_out[0], final_norm_g)
    return out[None]
```

```python
import functools

import jax
import jax.numpy as jnp
from jax import lax
from jax.experimental import pallas as pl
from jax.experimental.pallas import tpu as pltpu

F32 = jnp.float32
BF16 = jnp.bfloat16

EPS = 1e-6
LANES = 128
SUBLANES = 8

D_MODEL = 1024
GLA_HEADS, GLA_DK, GLA_DV, GLA_RANK = 4, 48, 96, 16
GLA_DK_PAD = 64
GLA_DV_PAD = LANES
GLA_GATE_NORM = 16.0
GLA_CHUNK = 64
FOX_HEADS, FOX_DH = 6, 64
MEM_HEADS, MEM_DH = 4, 64
HEAD_PAIR = 2
GLA_QK_W = GLA_HEADS * GLA_DK_PAD
GLA_V_W = GLA_HEADS * GLA_DV_PAD
FOX_W = FOX_HEADS * FOX_DH
MEM_W = MEM_HEADS * MEM_DH
SMALL_W = LANES
FG_LANE0 = 0
LR_LANE0 = SUBLANES

_GROUPS = (("gq", GLA_QK_W), ("gk", GLA_QK_W), ("gv", GLA_V_W), ("gg", GLA_V_W),
           ("fq", FOX_W), ("fk", FOX_W), ("fv", FOX_W), ("fgate", FOX_W),
           ("mq", MEM_W), ("mg", MEM_W), ("small", SMALL_W))
_OFF = {}
_o = 0
for _n, _w in _GROUPS:
    _OFF[_n] = (_o, _o + _w)
    _o += _w
IN_COLS_PAD = _o

PROJ_ROWS = 512
FOX_BLOCK = 512
OUT_ROWS = 1024
VMEM_LIMIT = 56 * 1024 * 1024

NEG_BIG = -1e30
FOX_SKIP_NATS = 105.0
NORM_SLACK = 1.02


def _log_sigmoid(z):
    return jnp.minimum(z, 0.0) - jnp.log(1.0 + jnp.exp(-jnp.abs(z)))


def _silu(z):
    return z / (1.0 + jnp.exp(-z))


def _rms_scale(v, width):
    return lax.rsqrt(jnp.sum(v * v, axis=-1, keepdims=True) * (1.0 / width) + EPS)


def _w_in_segments():
    qk, gw = GLA_HEADS * GLA_DK, GLA_HEADS * GLA_DV
    src = {}
    o = 0
    for name, width in (("gq", qk), ("gk", qk), ("gv", gw), ("lr", GLA_RANK), ("gg", gw),
                        ("fq", FOX_W), ("fk", FOX_W), ("fv", FOX_W), ("fg", FOX_HEADS),
                        ("fgate", FOX_W), ("mq", MEM_W), ("mg", MEM_W)):
        src[name] = o
        o += width
    segs = []
    for name, d, d_pad in (("gq", GLA_DK, GLA_DK_PAD), ("gk", GLA_DK, GLA_DK_PAD),
                           ("gv", GLA_DV, GLA_DV_PAD), ("gg", GLA_DV, GLA_DV_PAD)):
        segs += [(src[name] + h * d, _OFF[name][0] + h * d_pad, d) for h in range(GLA_HEADS)]
    segs += [(src[name], _OFF[name][0], _OFF[name][1] - _OFF[name][0])
             for name in ("fq", "fk", "fv", "fgate", "mq", "mg")]
    segs += [(src["fg"], _OFF["small"][0] + FG_LANE0, FOX_HEADS),
             (src["lr"], _OFF["small"][0] + LR_LANE0, GLA_RANK)]
    return tuple(segs)


def _proj_kernel(x_ref, g_ref, w_in_ref, wa_ref, ba_ref, bf_ref, seg_ref, ng_ref,
                 gla_ref, fq_ref, fk_ref, fv_ref, fgate_ref,
                 mq_ref, mg_ref, crow_ref, qn2_ref, kn2_ref,
                 carry_ref, wt_ref, gq_ref, gk_ref, gv_ref, gg_ref, loga_ref,
                 s_ref, lhs_ref, kv_ref, dec_ref, sprev_ref):
    rows = x_ref.shape[0]
    k_chunks = D_MODEL // LANES

    @pl.when(pl.program_id(0) == 0)
    def _():
        carry_ref[...] = jnp.zeros_like(carry_ref)
        s_ref[...] = jnp.zeros_like(s_ref)
        wt_ref[...] = jnp.zeros_like(wt_ref)
        for s0, d0, width in _w_in_segments():
            for c in range(k_chunks):
                wt_ref[d0:d0 + width, c * LANES:(c + 1) * LANES] = (
                    w_in_ref[pl.ds(s0 * k_chunks + c, width, stride=k_chunks), :].astype(BF16))

    x = x_ref[...]
    xn = (x * _rms_scale(x, D_MODEL) * g_ref[...]).astype(BF16)
    nt = (((1,), (1,)), ((), ()))

    def proj(first, last):
        lo, hi = _OFF[first][0], _OFF[last][1]
        y = lax.dot_general(xn, wt_ref[lo:hi, :], nt, preferred_element_type=F32)
        return lambda name: y[:, _OFF[name][0] - lo:_OFF[name][1] - lo]

    tail = proj("mq", "small")
    small = tail("small")
    logf = _log_sigmoid(small + bf_ref[...])
    c = logf.T[0:SUBLANES, :]
    lane = lax.broadcasted_iota(jnp.int32, c.shape, 1)
    shift = 1
    while shift < rows:
        c = c + jnp.where(lane >= shift, pltpu.roll(c, shift, axis=1), 0.0)
        shift *= 2
    c = c + carry_ref[:, 0:1]
    crow_ref[...] = c
    carry_ref[...] = jnp.broadcast_to(c[:, rows - 1:rows], carry_ref.shape)

    z = jnp.dot(small.astype(BF16), wa_ref[...], preferred_element_type=F32) + ba_ref[...]
    loga_ref[...] = _log_sigmoid(z) * (1.0 / GLA_GATE_NORM)
    gla = proj("gq", "gg")
    gq_ref[...] = gla("gq").astype(BF16)
    gk_ref[...] = gla("gk").astype(BF16)
    gv_ref[...] = gla("gv").astype(BF16)
    gg_ref[...] = _silu(gla("gg")).astype(BF16)
    _gla_block(gq_ref, gk_ref, gv_ref, loga_ref, gg_ref, ng_ref, gla_ref,
               s_ref, lhs_ref, kv_ref, dec_ref, sprev_ref)

    mq_ref[...] = (tail("mq") * MEM_DH ** -0.5).astype(BF16)
    mg_ref[...] = _silu(tail("mg")).astype(BF16)

    fox = proj("fq", "fgate")
    fq = (fox("fq") * FOX_DH ** -0.5).astype(BF16)
    fk = fox("fk").astype(BF16)
    fq_ref[...] = fq
    fk_ref[...] = fox("fk").T.astype(BF16)
    fv_ref[...] = fox("fv").astype(BF16)
    fgate_ref[...] = _silu(fox("fgate")).astype(BF16)

    def max_sq_norm(v):
        v32 = v.astype(F32)
        n2 = jnp.dot((v32 * v32).astype(BF16), seg_ref[...], preferred_element_type=F32)
        return jnp.max(n2, axis=0, keepdims=True)

    qn2_ref[0] = max_sq_norm(fq)
    kn2_ref[0] = max_sq_norm(fk)


def _memkv_kernel(mem_ref, g_ref, w_ref, mk_ref, mv_ref):
    m = mem_ref[...]
    mn = (m * _rms_scale(m, D_MODEL) * g_ref[...]).astype(BF16)
    kv = jnp.dot(mn, w_ref[...].astype(BF16), preferred_element_type=F32)
    mk_ref[...] = kv[:, :MEM_W].astype(BF16)
    mv_ref[...] = kv[:, MEM_W:].astype(BF16)


def _gla_block(q_ref, k_ref, v_ref, loga_ref, gate_ref, ng_ref, o_ref,
               s_ref, lhs_ref, kv_ref, dec_ref, sprev_ref):
    C = GLA_CHUNK
    W = HEAD_PAIR * GLA_DV_PAD
    n_chunks = q_ref.shape[0] // C

    row = lax.broadcasted_iota(jnp.int32, (C, LANES), 0)
    lane = lax.broadcasted_iota(jnp.int32, (C, LANES), 1)
    lo_k = lane < GLA_DK_PAD
    causal = row >= jnp.where(lo_k, lane, lane - GLA_DK_PAD)
    lo_v = lax.broadcasted_iota(jnp.int32, (C, W), 1) < GLA_DV_PAD
    st_row = lax.broadcasted_iota(jnp.int32, (LANES, W), 0)
    st_lane = lax.broadcasted_iota(jnp.int32, (LANES, W), 1)
    own = (st_row < GLA_DK_PAD) == (st_lane < GLA_DV_PAD)
    eye = (lax.broadcasted_iota(jnp.int32, (LANES, LANES), 0)
           == lax.broadcasted_iota(jnp.int32, (LANES, LANES), 1))
    scale = GLA_DK ** -0.5
    nt = (((1,), (1,)), ((), ()))
    tn = (((0,), (0,)), ((), ()))
    ng = jnp.concatenate([ng_ref[...]] * HEAD_PAIR, axis=1)

    pairs = range(GLA_HEADS // HEAD_PAIR)

    def local(ci):
        rs = slice(ci * C, (ci + 1) * C)
        for p in pairs:
            ls = slice(p * LANES, (p + 1) * LANES)
            vs = slice(p * W, (p + 1) * W)
            b = loga_ref[rs, ls]
            shift = 1
            while shift < C:
                b = b + jnp.where(row >= shift, pltpu.roll(b, shift, axis=0), 0.0)
                shift *= 2
            b_last = b[C - 1:C, :]
            k2 = k_ref[rs, ls].astype(F32)
            qd = (q_ref[rs, ls].astype(F32) * scale * jnp.exp(b)).astype(BF16)
            kd = (k2 * jnp.exp(-b)).astype(BF16)
            ke = (k2 * jnp.exp(b_last - b)).astype(BF16)
            zk = jnp.zeros_like(kd)
            kd_blk = jnp.concatenate([jnp.where(lo_k, kd, zk), jnp.where(lo_k, zk, kd)], axis=0)
            attn = lax.dot_general(qd, kd_blk, nt, preferred_element_type=F32)
            lhs_ref[rs, vs] = jnp.concatenate([jnp.where(causal, attn, 0.0).astype(BF16), qd], axis=1)
            kv = lax.dot_general(ke, v_ref[rs, vs], tn, preferred_element_type=F32)
            kv_ref[p, ci] = jnp.where(own, kv, 0.0)
            dcol = jnp.exp(jnp.sum(jnp.where(eye, jnp.broadcast_to(b_last, (LANES, LANES)), 0.0),
                                   axis=1, keepdims=True))
            dec_ref[p, ci] = jnp.broadcast_to(dcol, (LANES, LANES))

    def scan(ci):
        for p in pairs:
            s_prev = s_ref[p]
            sprev_ref[p, ci] = s_prev.astype(BF16)
            s_ref[p] = jnp.tile(dec_ref[p, ci], (1, HEAD_PAIR)) * s_prev + kv_ref[p, ci]

    def output(ci):
        rs = slice(ci * C, (ci + 1) * C)
        for p in pairs:
            vs = slice(p * W, (p + 1) * W)
            v2 = v_ref[rs, vs]
            zv = jnp.zeros_like(v2)
            v_blk = jnp.concatenate([jnp.where(lo_v, v2, zv), jnp.where(lo_v, zv, v2)], axis=0)
            o = jnp.dot(lhs_ref[rs, vs], jnp.concatenate([v_blk, sprev_ref[p, ci]], axis=0),
                        preferred_element_type=F32)
            o2 = o * o
            ms = jnp.where(lo_v, jnp.sum(o2[:, :GLA_DV_PAD], axis=1, keepdims=True),
                           jnp.sum(o2[:, GLA_DV_PAD:], axis=1, keepdims=True))
            on = o * lax.rsqrt(ms * (1.0 / GLA_DV) + EPS) * ng
            o_ref[rs, vs] = (on * gate_ref[rs, vs].astype(F32)).astype(BF16)

    for phase in (local, scan, output):
        for ci in range(n_chunks):
            phase(ci)


def _fox_kernel(lim_ref, cend_ref, q_ref, k_ref, v_ref, crow_ref, gate_ref, o_ref, m_ref, acc_ref):
    blk = q_ref.shape[0]
    pair = pl.program_id(0)
    qi = pl.program_id(1)
    nblk = pl.num_programs(1)
    q0 = pl.multiple_of(qi * blk, blk)
    lane = lax.broadcasted_iota(jnp.int32, (1, LANES), 1)
    lo_lanes = lane < FOX_DH
    nt = (((1,), (1,)), ((), ()))
    reps = blk // LANES

    q = q_ref[...]
    zero = jnp.zeros_like(q)
    q_stack = jnp.concatenate([jnp.where(lo_lanes, q, zero), jnp.where(lo_lanes, zero, q)], axis=0)
    m_ref[...] = jnp.full_like(m_ref, NEG_BIG)
    acc_ref[...] = jnp.zeros_like(acc_ref)

    def head_row(c8, h):
        sub = lax.broadcasted_iota(jnp.int32, c8.shape, 0)
        return jnp.sum(jnp.where(sub == h, c8, 0.0), axis=0, keepdims=True)

    c_q0 = [head_row(crow_ref[:, pl.ds(q0, LANES)], pair * HEAD_PAIR + hh)[:, 0:1]
            for hh in range(HEAD_PAIR)]

    def step(k0, masked):
        ks = pl.ds(k0, blk)
        kb = k_ref[:, ks]
        vb = v_ref[ks, :]
        one = jnp.ones_like(vb)
        vaug = (jnp.where(lo_lanes, vb, one), jnp.where(lo_lanes, one, vb))
        s_all = jnp.dot(q_stack, kb, preferred_element_type=F32)
        for hh in range(HEAD_PAIR):
            h = pair * HEAD_PAIR + hh
            bias = c_q0[hh] - head_row(crow_ref[:, ks], h)
            s = s_all[hh * blk:(hh + 1) * blk] + bias
            if masked:
                qpos = lax.broadcasted_iota(jnp.int32, (blk, blk), 0)
                kpos = lax.broadcasted_iota(jnp.int32, (blk, blk), 1)
                s = jnp.where(kpos <= qpos, s, NEG_BIG)
            m_prev = m_ref[hh]
            m_new = jnp.maximum(m_prev, jnp.max(s, axis=1, keepdims=True))
            p = jnp.exp(s - jnp.tile(m_new, (1, reps)))
            alpha = jnp.exp(m_prev - m_new)
            pv = jnp.dot(p.astype(BF16), vaug[hh], preferred_element_type=F32)
            acc_ref[hh] = alpha * acc_ref[hh] + pv
            m_ref[hh] = m_new

    @pl.when(qi == 0)
    def _():
        step(q0, masked=True)

    @pl.when(qi > 0)
    def _():
        step(q0, masked=True)
        step(q0 - blk, masked=False)

    m_min = [jnp.min(m_ref[hh]) for hh in range(HEAD_PAIR)]

    def live(j):
        jj = jnp.maximum(j, 0)
        prev = jnp.maximum(qi - 1, 0)
        keep = False
        for hh in range(HEAD_PAIR):
            base = (pair * HEAD_PAIR + hh) * nblk
            bound = (lim_ref[base + qi] + cend_ref[base + prev] - cend_ref[base + jj]) - m_min[hh]
            keep = jnp.logical_or(keep, jnp.logical_not(bound <= 0.0))
        return jnp.logical_and(j >= 0, keep)

    def body(j):
        step(pl.multiple_of(j * blk, blk), masked=False)
        step(pl.multiple_of((j - 1) * blk, blk), masked=False)
        return j - 2

    j_rest = lax.while_loop(lambda j: live(j - 1), body, qi - 2)

    @pl.when(live(j_rest))
    def _():
        step(pl.multiple_of(jnp.maximum(j_rest, 0) * blk, blk), masked=False)

    outs = []
    for hh in range(HEAD_PAIR):
        acc = acc_ref[hh]
        outs.append(acc / pltpu.roll(acc, FOX_DH, axis=1))
    o = jnp.where(lo_lanes, outs[0], outs[1])
    o_ref[...] = (o * gate_ref[...].astype(F32)).astype(BF16)


def _out_kernel(x_ref, gla_ref, fox_ref, mq_ref, mg_ref, mk_ref, mv_ref,
                w_out_ref, fg_ref, o_ref, wo_ref):
    @pl.when(pl.program_id(0) == 0)
    def _():
        wo_ref[...] = jnp.zeros_like(wo_ref)
        for h in range(GLA_HEADS):
            wo_ref[h * GLA_DV_PAD:h * GLA_DV_PAD + GLA_DV, :] = (
                w_out_ref[h * GLA_DV:(h + 1) * GLA_DV, :].astype(BF16))
        gw = GLA_HEADS * GLA_DV
        wo_ref[GLA_V_W:, :] = w_out_ref[gw:, :].astype(BF16)

    lane = lax.broadcasted_iota(jnp.int32, (1, LANES), 1)
    lo_lanes = lane < MEM_DH
    nt = (((1,), (1,)), ((), ()))
    mem_parts = []
    for p in range(MEM_HEADS // HEAD_PAIR):
        ls = slice(p * LANES, (p + 1) * LANES)
        q = mq_ref[:, ls]
        kb = mk_ref[:, ls]
        vb = mv_ref[:, ls]
        zero = jnp.zeros_like(q)
        one = jnp.ones_like(vb)
        qh = (jnp.where(lo_lanes, q, zero), jnp.where(lo_lanes, zero, q))
        vaug = (jnp.where(lo_lanes, vb, one), jnp.where(lo_lanes, one, vb))
        outs = []
        for hh in range(HEAD_PAIR):
            s = lax.dot_general(qh[hh], kb, nt, preferred_element_type=F32)
            pexp = jnp.exp(s - jnp.max(s, axis=1, keepdims=True))
            pv = jnp.dot(pexp.astype(BF16), vaug[hh], preferred_element_type=F32)
            outs.append(pv / pltpu.roll(pv, MEM_DH, axis=1))
        o = jnp.where(lo_lanes, outs[0], outs[1])
        mem_parts.append((o * mg_ref[:, ls].astype(F32)).astype(BF16))
    mixed = jnp.concatenate([gla_ref[...], fox_ref[...]] + mem_parts, axis=1)
    y = x_ref[...] + jnp.dot(mixed, wo_ref[...], preferred_element_type=F32)
    o_ref[...] = y * _rms_scale(y, D_MODEL) * fg_ref[...]


def _pad_heads(w, heads, d, d_pad):
    lead = w.shape[:-1]
    w = w.reshape(lead + (heads, d))
    w = jnp.pad(w, [(0, 0)] * len(lead) + [(0, 0), (0, d_pad - d)])
    return w.reshape(lead + (heads * d_pad,))


def _layout_weights(w_alpha_up, b_alpha, b_forget, gla_norm_g):
    wa = jnp.zeros((SMALL_W, GLA_QK_W), F32)
    wa = wa.at[LR_LANE0:LR_LANE0 + GLA_RANK, :].set(
        _pad_heads(w_alpha_up, GLA_HEADS, GLA_DK, GLA_DK_PAD)).astype(BF16)
    ba = _pad_heads(b_alpha[None, :], GLA_HEADS, GLA_DK, GLA_DK_PAD)
    bf = jnp.zeros((1, SMALL_W), F32).at[0, FG_LANE0:FG_LANE0 + FOX_HEADS].set(b_forget)
    ng = jnp.pad(gla_norm_g, (0, GLA_DV_PAD - GLA_DV))[None, :]
    return wa, ba, bf, ng


def _fox_limits(qn2, kn2, crow, blk):
    nblk = crow.shape[1] // blk
    nproj = qn2.shape[0]
    assert nblk % nproj == 0
    qn = jnp.sqrt(qn2[:, 0, :FOX_HEADS])
    kmax = jnp.sqrt(jnp.max(kn2[:, 0, :FOX_HEADS], axis=0))
    lim = FOX_SKIP_NATS + NORM_SLACK * jnp.repeat(qn, nblk // nproj, axis=0) * kmax[None, :]
    cend = crow[:FOX_HEADS, blk - 1::blk]
    return lim.T.reshape(-1), cend.reshape(-1)


def _params(*sem):
    return pltpu.CompilerParams(dimension_semantics=sem, vmem_limit_bytes=VMEM_LIMIT)


def _layer(x, mem, norm_g, w_in, w_alpha_up, b_alpha, b_forget, gla_norm_g,
           mem_norm_g, w_mem_kv, w_out, out_g):
    w_in_t = jnp.transpose(w_in[None], (0, 2, 1)).reshape(-1, LANES)
    T = x.shape[0]
    M = mem.shape[0]
    wa, ba, bf, ng = _layout_weights(w_alpha_up, b_alpha, b_forget, gla_norm_g)

    def rows(width, n=PROJ_ROWS):
        return pl.BlockSpec((n, width), lambda i: (i, 0))

    def whole(shape):
        return pl.BlockSpec(shape, lambda i: (0,) * len(shape))

    bshape = lambda w: jax.ShapeDtypeStruct((T, w), BF16)
    nproj = T // PROJ_ROWS
    stat_spec = pl.BlockSpec((1, 1, LANES), lambda i: (i, 0, 0))
    stat_shape = jax.ShapeDtypeStruct((nproj, 1, LANES), F32)
    seg = (jnp.arange(FOX_W)[:, None] // FOX_DH == jnp.arange(LANES)[None, :]).astype(BF16)
    gla_pairs, gla_chunks = GLA_HEADS // HEAD_PAIR, PROJ_ROWS // GLA_CHUNK
    pair_w = HEAD_PAIR * GLA_DV_PAD
    (gla, fq, fk, fv, fgate, mq, mg, crow, qn2, kn2) = pl.pallas_call(
        _proj_kernel,
        grid=(nproj,),
        in_specs=[rows(D_MODEL), whole((1, D_MODEL)),
                  pl.BlockSpec(w_in_t.shape, lambda i: (0, 0), pipeline_mode=pl.Buffered(1)),
                  whole((SMALL_W, GLA_QK_W)), whole((1, GLA_QK_W)), whole((1, SMALL_W)),
                  whole((FOX_W, LANES)), whole((1, GLA_DV_PAD))],
        out_specs=[rows(GLA_V_W),
                   rows(FOX_W), pl.BlockSpec((FOX_W, PROJ_ROWS), lambda i: (0, i)),
                   rows(FOX_W), rows(FOX_W),
                   rows(MEM_W), rows(MEM_W),
                   pl.BlockSpec((SUBLANES, PROJ_ROWS), lambda i: (0, i)),
                   stat_spec, stat_spec],
        out_shape=[bshape(GLA_V_W),
                   bshape(FOX_W), jax.ShapeDtypeStruct((FOX_W, T), BF16),
                   bshape(FOX_W), bshape(FOX_W),
                   bshape(MEM_W), bshape(MEM_W),
                   jax.ShapeDtypeStruct((SUBLANES, T), F32),
                   stat_shape, stat_shape],
        scratch_shapes=[
            pltpu.VMEM((SUBLANES, LANES), F32),
            pltpu.VMEM((IN_COLS_PAD, D_MODEL), BF16),
            pltpu.VMEM((PROJ_ROWS, GLA_QK_W), BF16),
            pltpu.VMEM((PROJ_ROWS, GLA_QK_W), BF16),
            pltpu.VMEM((PROJ_ROWS, GLA_V_W), BF16),
            pltpu.VMEM((PROJ_ROWS, GLA_V_W), BF16),
            pltpu.VMEM((PROJ_ROWS, GLA_QK_W), F32),
            pltpu.VMEM((gla_pairs, LANES, pair_w), F32),
            pltpu.VMEM((PROJ_ROWS, GLA_V_W), BF16),
            pltpu.VMEM((gla_pairs, gla_chunks, LANES, pair_w), F32),
            pltpu.VMEM((gla_pairs, gla_chunks, LANES, LANES), F32),
            pltpu.VMEM((gla_pairs, gla_chunks, LANES, pair_w), BF16)],
        compiler_params=_params("arbitrary"),
        name="proj",
    )(x, norm_g[None, :], w_in_t, wa, ba, bf, seg, ng)

    mk, mv = pl.pallas_call(
        _memkv_kernel,
        out_shape=[jax.ShapeDtypeStruct((M, MEM_W), BF16)] * 2,
        compiler_params=pltpu.CompilerParams(vmem_limit_bytes=VMEM_LIMIT),
        name="memkv",
    )(mem, mem_norm_g[None, :], w_mem_kv)

    lim, cend = _fox_limits(qn2, kn2, crow, FOX_BLOCK)
    pair_rows = pl.BlockSpec((FOX_BLOCK, LANES), lambda p, i, *_: (i, p))
    pair_all = pl.BlockSpec((T, LANES), lambda p, i, *_: (0, p))
    fox = pl.pallas_call(
        _fox_kernel,
        grid_spec=pltpu.PrefetchScalarGridSpec(
            num_scalar_prefetch=2,
            grid=(FOX_HEADS // HEAD_PAIR, T // FOX_BLOCK),
            in_specs=[pair_rows, pl.BlockSpec((LANES, T), lambda p, i, *_: (p, 0)), pair_all,
                      pl.BlockSpec((SUBLANES, T), lambda p, i, *_: (0, 0)), pair_rows],
            out_specs=pair_rows,
            scratch_shapes=[pltpu.VMEM((HEAD_PAIR, FOX_BLOCK, LANES), F32),
                            pltpu.VMEM((HEAD_PAIR, FOX_BLOCK, LANES), F32)]),
        out_shape=bshape(FOX_W),
        compiler_params=_params("arbitrary", "arbitrary"),
        name="fox",
    )(lim, cend, fq, fk, fv, crow, fgate)

    out = pl.pallas_call(
        _out_kernel,
        grid=(T // OUT_ROWS,),
        in_specs=[rows(D_MODEL, OUT_ROWS), rows(GLA_V_W, OUT_ROWS), rows(FOX_W, OUT_ROWS),
                  rows(MEM_W, OUT_ROWS), rows(MEM_W, OUT_ROWS),
                  whole((M, MEM_W)), whole((M, MEM_W)),
                  pl.BlockSpec(w_out.shape, lambda i: (0, 0), pipeline_mode=pl.Buffered(1)),
                  whole((1, D_MODEL))],
        out_specs=rows(D_MODEL, OUT_ROWS),
        out_shape=jax.ShapeDtypeStruct((T, D_MODEL), F32),
        scratch_shapes=[pltpu.VMEM((GLA_V_W + FOX_W + MEM_W, D_MODEL), BF16)],
        compiler_params=_params("arbitrary"),
        name="out",
    )(x, gla, fox, mq, mg, mk, mv, w_out, out_g[None, :])
    return out


def kernel(x, mem, norm_g, w_in, w_alpha_up, b_alpha, b_forget, gla_norm_g, mem_norm_g,
           w_mem_kv, w_out, final_norm_g):
    assert x.shape[0] == 1 and mem.shape[0] == 1 and norm_g.shape[0] == 1
    assert x.shape[1] % max(PROJ_ROWS, FOX_BLOCK, OUT_ROWS) == 0
    out = _layer(x[0], mem[0], norm_g[0], w_in[0], w_alpha_up[0], b_alpha[0], b_forget[0],
                 gla_norm_g[0], mem_norm_g[0], w_mem_kv[0], w_out[0], final_norm_g)
    return out[None]
```

```python
import functools

import jax
import jax.numpy as jnp
from jax import lax
from jax.experimental import pallas as pl
from jax.experimental.pallas import tpu as pltpu

F32 = jnp.float32
BF16 = jnp.bfloat16

EPS = 1e-6
LANES = 128
SUBLANES = 8

D_MODEL = 1024
GLA_HEADS, GLA_DK, GLA_DV, GLA_RANK = 4, 48, 96, 16
GLA_DK_PAD = 64
GLA_DV_PAD = LANES
GLA_GATE_NORM = 16.0
GLA_CHUNK = 64
FOX_HEADS, FOX_DH = 6, 64
MEM_HEADS, MEM_DH = 4, 64
HEAD_PAIR = 2
GLA_QK_W = GLA_HEADS * GLA_DK_PAD
GLA_V_W = GLA_HEADS * GLA_DV_PAD
GLA_OUT_W = GLA_HEADS * GLA_DV
FOX_W = FOX_HEADS * FOX_DH
MEM_W = MEM_HEADS * MEM_DH
SMALL_W = LANES
FG_LANE0 = 0
LR_LANE0 = SUBLANES

_GROUPS = (("gq", GLA_QK_W), ("gk", GLA_QK_W), ("gv", GLA_V_W), ("gg", GLA_V_W),
           ("fq", FOX_W), ("fk", FOX_W), ("fv", FOX_W), ("fgate", FOX_W),
           ("mq", MEM_W), ("mg", MEM_W), ("small", SMALL_W))
_OFF = {}
_o = 0
for _n, _w in _GROUPS:
    _OFF[_n] = (_o, _o + _w)
    _o += _w
IN_COLS_PAD = _o

PROJ_ROWS = 512
FOX_BLOCK = 512
OUT_ROWS = 1024
VMEM_LIMIT = 56 * 1024 * 1024

NEG_BIG = -1e30
FOX_SKIP_NATS = 105.0
NORM_SLACK = 1.02


def _log_sigmoid(z):
    return jnp.minimum(z, 0.0) - jnp.log(1.0 + jnp.exp(-jnp.abs(z)))


def _silu(z):
    return z / (1.0 + jnp.exp(-z))


def _rms_scale(v, width):
    return lax.rsqrt(jnp.sum(v * v, axis=-1, keepdims=True) * (1.0 / width) + EPS)


def _w_in_segments():
    qk, gw = GLA_HEADS * GLA_DK, GLA_HEADS * GLA_DV
    src = {}
    o = 0
    for name, width in (("gq", qk), ("gk", qk), ("gv", gw), ("lr", GLA_RANK), ("gg", gw),
                        ("fq", FOX_W), ("fk", FOX_W), ("fv", FOX_W), ("fg", FOX_HEADS),
                        ("fgate", FOX_W), ("mq", MEM_W), ("mg", MEM_W)):
        src[name] = o
        o += width
    segs = []
    for name, d, d_pad in (("gq", GLA_DK, GLA_DK_PAD), ("gk", GLA_DK, GLA_DK_PAD),
                           ("gv", GLA_DV, GLA_DV_PAD), ("gg", GLA_DV, GLA_DV_PAD)):
        segs += [(src[name] + h * d, _OFF[name][0] + h * d_pad, d) for h in range(GLA_HEADS)]
    segs += [(src[name], _OFF[name][0], _OFF[name][1] - _OFF[name][0])
             for name in ("fq", "fk", "fv", "fgate", "mq", "mg")]
    segs += [(src["fg"], _OFF["small"][0] + FG_LANE0, FOX_HEADS),
             (src["lr"], _OFF["small"][0] + LR_LANE0, GLA_RANK)]
    return tuple(segs)


def _proj_kernel(x_ref, g_ref, w_in_ref, wa_ref, ba_ref, bf_ref, seg_ref, ng_ref,
                 gla_ref, fq_ref, fk_ref, fv_ref, fgate_ref,
                 mq_ref, mg_ref, crow_ref, qn2_ref, kn2_ref,
                 carry_ref, wt_ref, gq_ref, gk_ref, gv_ref, gg_ref, loga_ref,
                 s_ref, lhs_ref, kv_ref, dec_ref, sprev_ref):
    rows = x_ref.shape[0]
    k_chunks = D_MODEL // LANES

    @pl.when(pl.program_id(0) == 0)
    def _():
        carry_ref[...] = jnp.zeros_like(carry_ref)
        s_ref[...] = jnp.zeros_like(s_ref)
        wt_ref[...] = jnp.zeros_like(wt_ref)
        for s0, d0, width in _w_in_segments():
            for c in range(k_chunks):
                wt_ref[d0:d0 + width, c * LANES:(c + 1) * LANES] = (
                    w_in_ref[pl.ds(s0 * k_chunks + c, width, stride=k_chunks), :].astype(BF16))

    x = x_ref[...]
    xn = (x * _rms_scale(x, D_MODEL) * g_ref[...]).astype(BF16)
    nt = (((1,), (1,)), ((), ()))

    def proj(first, last):
        lo, hi = _OFF[first][0], _OFF[last][1]
        y = lax.dot_general(xn, wt_ref[lo:hi, :], nt, preferred_element_type=F32)
        return lambda name: y[:, _OFF[name][0] - lo:_OFF[name][1] - lo]

    tail = proj("mq", "small")
    small = tail("small")
    logf = _log_sigmoid(small + bf_ref[...])
    c = logf.T[0:SUBLANES, :]
    lane = lax.broadcasted_iota(jnp.int32, c.shape, 1)
    shift = 1
    while shift < rows:
        c = c + jnp.where(lane >= shift, pltpu.roll(c, shift, axis=1), 0.0)
        shift *= 2
    c = c + carry_ref[:, 0:1]
    crow_ref[...] = c
    carry_ref[...] = jnp.broadcast_to(c[:, rows - 1:rows], carry_ref.shape)

    z = jnp.dot(small.astype(BF16), wa_ref[...], preferred_element_type=F32) + ba_ref[...]
    loga_ref[...] = _log_sigmoid(z) * (1.0 / GLA_GATE_NORM)
    gla = proj("gq", "gg")
    gq_ref[...] = gla("gq").astype(BF16)
    gk_ref[...] = gla("gk").astype(BF16)
    gv_ref[...] = gla("gv").astype(BF16)
    gg_ref[...] = _silu(gla("gg")).astype(BF16)
    _gla_block(gq_ref, gk_ref, gv_ref, loga_ref, gg_ref, ng_ref, gla_ref,
               s_ref, lhs_ref, kv_ref, dec_ref, sprev_ref)

    mq_ref[...] = (tail("mq") * MEM_DH ** -0.5).astype(BF16)
    mg_ref[...] = _silu(tail("mg")).astype(BF16)

    fox = proj("fq", "fgate")
    fq = (fox("fq") * FOX_DH ** -0.5).astype(BF16)
    fk = fox("fk").astype(BF16)
    fq_ref[...] = fq
    fk_ref[...] = fox("fk").T.astype(BF16)
    fv_ref[...] = fox("fv").astype(BF16)
    fgate_ref[...] = _silu(fox("fgate")).astype(BF16)

    def max_sq_norm(v):
        v32 = v.astype(F32)
        n2 = jnp.dot((v32 * v32).astype(BF16), seg_ref[...], preferred_element_type=F32)
        return jnp.max(n2, axis=0, keepdims=True)

    qn2_ref[0] = max_sq_norm(fq)
    kn2_ref[0] = max_sq_norm(fk)


def _memkv_kernel(mem_ref, g_ref, w_ref, mk_ref, mv_ref):
    m = mem_ref[...]
    mn = (m * _rms_scale(m, D_MODEL) * g_ref[...]).astype(BF16)
    kv = jnp.dot(mn, w_ref[...].astype(BF16), preferred_element_type=F32)
    mk_ref[...] = kv[:, :MEM_W].astype(BF16)
    mv_ref[...] = kv[:, MEM_W:].astype(BF16)


def _gla_block(q_ref, k_ref, v_ref, loga_ref, gate_ref, ng_ref, o_ref,
               s_ref, lhs_ref, kv_ref, dec_ref, sprev_ref):
    C = GLA_CHUNK
    W = HEAD_PAIR * GLA_DV_PAD
    n_chunks = q_ref.shape[0] // C

    row = lax.broadcasted_iota(jnp.int32, (C, LANES), 0)
    lane = lax.broadcasted_iota(jnp.int32, (C, LANES), 1)
    lo_k = lane < GLA_DK_PAD
    causal = row >= jnp.where(lo_k, lane, lane - GLA_DK_PAD)
    lo_v = lax.broadcasted_iota(jnp.int32, (C, W), 1) < GLA_DV_PAD
    st_row = lax.broadcasted_iota(jnp.int32, (LANES, W), 0)
    st_lane = lax.broadcasted_iota(jnp.int32, (LANES, W), 1)
    own = (st_row < GLA_DK_PAD) == (st_lane < GLA_DV_PAD)
    eye = (lax.broadcasted_iota(jnp.int32, (LANES, LANES), 0)
           == lax.broadcasted_iota(jnp.int32, (LANES, LANES), 1))
    scale = GLA_DK ** -0.5
    nt = (((1,), (1,)), ((), ()))
    tn = (((0,), (0,)), ((), ()))
    ng = jnp.concatenate([ng_ref[...]] * HEAD_PAIR, axis=1)

    pairs = range(GLA_HEADS // HEAD_PAIR)

    def local(ci):
        rs = slice(ci * C, (ci + 1) * C)
        for p in pairs:
            ls = slice(p * LANES, (p + 1) * LANES)
            vs = slice(p * W, (p + 1) * W)
            b = loga_ref[rs, ls]
            shift = 1
            while shift < C:
                b = b + jnp.where(row >= shift, pltpu.roll(b, shift, axis=0), 0.0)
                shift *= 2
            b_last = b[C - 1:C, :]
            k2 = k_ref[rs, ls].astype(F32)
            qd = (q_ref[rs, ls].astype(F32) * scale * jnp.exp(b)).astype(BF16)
            kd = (k2 * jnp.exp(-b)).astype(BF16)
            ke = (k2 * jnp.exp(b_last - b)).astype(BF16)
            zk = jnp.zeros_like(kd)
            kd_blk = jnp.concatenate([jnp.where(lo_k, kd, zk), jnp.where(lo_k, zk, kd)], axis=0)
            attn = lax.dot_general(qd, kd_blk, nt, preferred_element_type=F32)
            lhs_ref[rs, vs] = jnp.concatenate([jnp.where(causal, attn, 0.0).astype(BF16), qd], axis=1)
            kv = lax.dot_general(ke, v_ref[rs, vs], tn, preferred_element_type=F32)
            kv_ref[p, ci] = jnp.where(own, kv, 0.0)
            dcol = jnp.exp(jnp.sum(jnp.where(eye, jnp.broadcast_to(b_last, (LANES, LANES)), 0.0),
                                   axis=1, keepdims=True))
            dec_ref[p, ci] = jnp.broadcast_to(dcol, (LANES, LANES))

    def scan(ci):
        for p in pairs:
            s_prev = s_ref[p]
            sprev_ref[p, ci] = s_prev.astype(BF16)
            s_ref[p] = jnp.tile(dec_ref[p, ci], (1, HEAD_PAIR)) * s_prev + kv_ref[p, ci]

    def output(ci):
        rs = slice(ci * C, (ci + 1) * C)
        for p in pairs:
            vs = slice(p * W, (p + 1) * W)
            v2 = v_ref[rs, vs]
            zv = jnp.zeros_like(v2)
            v_blk = jnp.concatenate([jnp.where(lo_v, v2, zv), jnp.where(lo_v, zv, v2)], axis=0)
            o = jnp.dot(lhs_ref[rs, vs], jnp.concatenate([v_blk, sprev_ref[p, ci]], axis=0),
                        preferred_element_type=F32)
            o2 = o * o
            ms = jnp.where(lo_v, jnp.sum(o2[:, :GLA_DV_PAD], axis=1, keepdims=True),
                           jnp.sum(o2[:, GLA_DV_PAD:], axis=1, keepdims=True))
            on = o * lax.rsqrt(ms * (1.0 / GLA_DV) + EPS) * ng
            og = (on * gate_ref[rs, vs].astype(F32)).astype(BF16)
            for hh in range(HEAD_PAIR):
                c0 = (p * HEAD_PAIR + hh) * GLA_DV
                o_ref[rs, c0:c0 + GLA_DV] = og[:, hh * GLA_DV_PAD:hh * GLA_DV_PAD + GLA_DV]

    for phase in (local, scan, output):
        for ci in range(n_chunks):
            phase(ci)


def _fox_kernel(lim_ref, cend_ref, q_ref, k_ref, v_ref, crow_ref, gate_ref, o_ref, m_ref, acc_ref):
    blk = q_ref.shape[0]
    pair = pl.program_id(0)
    qi = pl.program_id(1)
    nblk = pl.num_programs(1)
    q0 = pl.multiple_of(qi * blk, blk)
    lane = lax.broadcasted_iota(jnp.int32, (1, LANES), 1)
    lo_lanes = lane < FOX_DH
    reps = blk // LANES

    q = q_ref[...]
    zero = jnp.zeros_like(q)
    q_stack = jnp.concatenate([jnp.where(lo_lanes, q, zero), jnp.where(lo_lanes, zero, q)], axis=0)
    m_ref[...] = jnp.full_like(m_ref, NEG_BIG)
    acc_ref[...] = jnp.zeros_like(acc_ref)

    def head_row(c8, h):
        sub = lax.broadcasted_iota(jnp.int32, c8.shape, 0)
        return jnp.sum(jnp.where(sub == h, c8, 0.0), axis=0, keepdims=True)

    c_q0 = [head_row(crow_ref[:, pl.ds(q0, LANES)], pair * HEAD_PAIR + hh)[:, 0:1]
            for hh in range(HEAD_PAIR)]

    def step(k0, masked):
        ks = pl.ds(k0, blk)
        kb = k_ref[:, ks]
        vb = v_ref[ks, :]
        one = jnp.ones_like(vb)
        vaug = (jnp.where(lo_lanes, vb, one), jnp.where(lo_lanes, one, vb))
        s_all = jnp.dot(q_stack, kb, preferred_element_type=F32)
        for hh in range(HEAD_PAIR):
            h = pair * HEAD_PAIR + hh
            bias = c_q0[hh] - head_row(crow_ref[:, ks], h)
            s = s_all[hh * blk:(hh + 1) * blk] + bias
            if masked:
                qpos = lax.broadcasted_iota(jnp.int32, (blk, blk), 0)
                kpos = lax.broadcasted_iota(jnp.int32, (blk, blk), 1)
                s = jnp.where(kpos <= qpos, s, NEG_BIG)
            m_prev = m_ref[hh]
            m_new = jnp.maximum(m_prev, jnp.max(s, axis=1, keepdims=True))
            p = jnp.exp(s - jnp.tile(m_new, (1, reps)))
            alpha = jnp.exp(m_prev - m_new)
            pv = jnp.dot(p.astype(BF16), vaug[hh], preferred_element_type=F32)
            acc_ref[hh] = alpha * acc_ref[hh] + pv
            m_ref[hh] = m_new

    @pl.when(qi == 0)
    def _():
        step(q0, masked=True)

    @pl.when(qi > 0)
    def _():
        step(q0, masked=True)
        step(q0 - blk, masked=False)

    m_min = [jnp.min(m_ref[hh]) for hh in range(HEAD_PAIR)]

    def live(j):
        jj = jnp.maximum(j, 0)
        prev = jnp.maximum(qi - 1, 0)
        keep = False
        for hh in range(HEAD_PAIR):
            base = (pair * HEAD_PAIR + hh) * nblk
            bound = (lim_ref[base + qi] + cend_ref[base + prev] - cend_ref[base + jj]) - m_min[hh]
            keep = jnp.logical_or(keep, jnp.logical_not(bound <= 0.0))
        return jnp.logical_and(j >= 0, keep)

    def body(j):
        step(pl.multiple_of(j * blk, blk), masked=False)
        step(pl.multiple_of((j - 1) * blk, blk), masked=False)
        return j - 2

    j_rest = lax.while_loop(lambda j: live(j - 1), body, qi - 2)

    @pl.when(live(j_rest))
    def _():
        step(pl.multiple_of(jnp.maximum(j_rest, 0) * blk, blk), masked=False)

    outs = []
    for hh in range(HEAD_PAIR):
        acc = acc_ref[hh]
        outs.append(acc / pltpu.roll(acc, FOX_DH, axis=1))
    o = jnp.where(lo_lanes, outs[0], outs[1])
    o_ref[...] = (o * gate_ref[...].astype(F32)).astype(BF16)


def _out_kernel(x_ref, gla_ref, fox_ref, mq_ref, mg_ref, mk_ref, mv_ref,
                w_out_ref, fg_ref, o_ref, wo_ref):
    @pl.when(pl.program_id(0) == 0)
    def _():
        wo_ref[...] = w_out_ref[...].astype(BF16)

    lane = lax.broadcasted_iota(jnp.int32, (1, LANES), 1)
    lo_lanes = lane < MEM_DH
    nt = (((1,), (1,)), ((), ()))
    mem_parts = []
    for p in range(MEM_HEADS // HEAD_PAIR):
        ls = slice(p * LANES, (p + 1) * LANES)
        q = mq_ref[:, ls]
        kb = mk_ref[:, ls]
        vb = mv_ref[:, ls]
        zero = jnp.zeros_like(q)
        one = jnp.ones_like(vb)
        qh = (jnp.where(lo_lanes, q, zero), jnp.where(lo_lanes, zero, q))
        vaug = (jnp.where(lo_lanes, vb, one), jnp.where(lo_lanes, one, vb))
        outs = []
        for hh in range(HEAD_PAIR):
            s = lax.dot_general(qh[hh], kb, nt, preferred_element_type=F32)
            pexp = jnp.exp(s - jnp.max(s, axis=1, keepdims=True))
            pv = jnp.dot(pexp.astype(BF16), vaug[hh], preferred_element_type=F32)
            outs.append(pv / pltpu.roll(pv, MEM_DH, axis=1))
        o = jnp.where(lo_lanes, outs[0], outs[1])
        mem_parts.append((o * mg_ref[:, ls].astype(F32)).astype(BF16))
    mixed = jnp.concatenate([gla_ref[...], fox_ref[...]] + mem_parts, axis=1)
    y = x_ref[...] + jnp.dot(mixed, wo_ref[...], preferred_element_type=F32)
    o_ref[...] = y * _rms_scale(y, D_MODEL) * fg_ref[...]


def _pad_heads(w, heads, d, d_pad):
    lead = w.shape[:-1]
    w = w.reshape(lead + (heads, d))
    w = jnp.pad(w, [(0, 0)] * len(lead) + [(0, 0), (0, d_pad - d)])
    return w.reshape(lead + (heads * d_pad,))


def _layout_weights(w_alpha_up, b_alpha, b_forget, gla_norm_g):
    wa = jnp.zeros((SMALL_W, GLA_QK_W), F32)
    wa = wa.at[LR_LANE0:LR_LANE0 + GLA_RANK, :].set(
        _pad_heads(w_alpha_up, GLA_HEADS, GLA_DK, GLA_DK_PAD)).astype(BF16)
    ba = _pad_heads(b_alpha[None, :], GLA_HEADS, GLA_DK, GLA_DK_PAD)
    bf = jnp.zeros((1, SMALL_W), F32).at[0, FG_LANE0:FG_LANE0 + FOX_HEADS].set(b_forget)
    ng = jnp.pad(gla_norm_g, (0, GLA_DV_PAD - GLA_DV))[None, :]
    return wa, ba, bf, ng


def _fox_limits(qn2, kn2, crow, blk):
    nblk = crow.shape[1] // blk
    nproj = qn2.shape[0]
    assert nblk % nproj == 0
    qn = jnp.sqrt(qn2[:, 0, :FOX_HEADS])
    kmax = jnp.sqrt(jnp.max(kn2[:, 0, :FOX_HEADS], axis=0))
    lim = FOX_SKIP_NATS + NORM_SLACK * jnp.repeat(qn, nblk // nproj, axis=0) * kmax[None, :]
    cend = crow[:FOX_HEADS, blk - 1::blk]
    return lim.T.reshape(-1), cend.reshape(-1)


def _params(*sem):
    return pltpu.CompilerParams(dimension_semantics=sem, vmem_limit_bytes=VMEM_LIMIT)


def _layer(x, mem, norm_g, w_in, w_alpha_up, b_alpha, b_forget, gla_norm_g,
           mem_norm_g, w_mem_kv, w_out, out_g):
    w_in_t = jnp.transpose(w_in[None], (0, 2, 1)).reshape(-1, LANES)
    T = x.shape[0]
    M = mem.shape[0]
    wa, ba, bf, ng = _layout_weights(w_alpha_up, b_alpha, b_forget, gla_norm_g)

    def rows(width, n=PROJ_ROWS):
        return pl.BlockSpec((n, width), lambda i: (i, 0))

    def whole(shape):
        return pl.BlockSpec(shape, lambda i: (0,) * len(shape))

    bshape = lambda w: jax.ShapeDtypeStruct((T, w), BF16)
    nproj = T // PROJ_ROWS
    stat_spec = pl.BlockSpec((1, 1, LANES), lambda i: (i, 0, 0))
    stat_shape = jax.ShapeDtypeStruct((nproj, 1, LANES), F32)
    seg = (jnp.arange(FOX_W)[:, None] // FOX_DH == jnp.arange(LANES)[None, :]).astype(BF16)
    gla_pairs, gla_chunks = GLA_HEADS // HEAD_PAIR, PROJ_ROWS // GLA_CHUNK
    pair_w = HEAD_PAIR * GLA_DV_PAD
    (gla, fq, fk, fv, fgate, mq, mg, crow, qn2, kn2) = pl.pallas_call(
        _proj_kernel,
        grid=(nproj,),
        in_specs=[rows(D_MODEL), whole((1, D_MODEL)),
                  pl.BlockSpec(w_in_t.shape, lambda i: (0, 0), pipeline_mode=pl.Buffered(1)),
                  whole((SMALL_W, GLA_QK_W)), whole((1, GLA_QK_W)), whole((1, SMALL_W)),
                  whole((FOX_W, LANES)), whole((1, GLA_DV_PAD))],
        out_specs=[rows(GLA_OUT_W),
                   rows(FOX_W), pl.BlockSpec((FOX_W, PROJ_ROWS), lambda i: (0, i)),
                   rows(FOX_W), rows(FOX_W),
                   rows(MEM_W), rows(MEM_W),
                   pl.BlockSpec((SUBLANES, PROJ_ROWS), lambda i: (0, i)),
                   stat_spec, stat_spec],
        out_shape=[bshape(GLA_OUT_W),
                   bshape(FOX_W), jax.ShapeDtypeStruct((FOX_W, T), BF16),
                   bshape(FOX_W), bshape(FOX_W),
                   bshape(MEM_W), bshape(MEM_W),
                   jax.ShapeDtypeStruct((SUBLANES, T), F32),
                   stat_shape, stat_shape],
        scratch_shapes=[
            pltpu.VMEM((SUBLANES, LANES), F32),
            pltpu.VMEM((IN_COLS_PAD, D_MODEL), BF16),
            pltpu.VMEM((PROJ_ROWS, GLA_QK_W), BF16),
            pltpu.VMEM((PROJ_ROWS, GLA_QK_W), BF16),
            pltpu.VMEM((PROJ_ROWS, GLA_V_W), BF16),
            pltpu.VMEM((PROJ_ROWS, GLA_V_W), BF16),
            pltpu.VMEM((PROJ_ROWS, GLA_QK_W), F32),
            pltpu.VMEM((gla_pairs, LANES, pair_w), F32),
            pltpu.VMEM((PROJ_ROWS, GLA_V_W), BF16),
            pltpu.VMEM((gla_pairs, gla_chunks, LANES, pair_w), F32),
            pltpu.VMEM((gla_pairs, gla_chunks, LANES, LANES), F32),
            pltpu.VMEM((gla_pairs, gla_chunks, LANES, pair_w), BF16)],
        compiler_params=_params("arbitrary"),
        name="proj",
    )(x, norm_g[None, :], w_in_t, wa, ba, bf, seg, ng)

    mk, mv = pl.pallas_call(
        _memkv_kernel,
        out_shape=[jax.ShapeDtypeStruct((M, MEM_W), BF16)] * 2,
        compiler_params=pltpu.CompilerParams(vmem_limit_bytes=VMEM_LIMIT),
        name="memkv",
    )(mem, mem_norm_g[None, :], w_mem_kv)

    lim, cend = _fox_limits(qn2, kn2, crow, FOX_BLOCK)
    pair_rows = pl.BlockSpec((FOX_BLOCK, LANES), lambda p, i, *_: (i, p))
    pair_all = pl.BlockSpec((T, LANES), lambda p, i, *_: (0, p))
    fox = pl.pallas_call(
        _fox_kernel,
        grid_spec=pltpu.PrefetchScalarGridSpec(
            num_scalar_prefetch=2,
            grid=(FOX_HEADS // HEAD_PAIR, T // FOX_BLOCK),
            in_specs=[pair_rows, pl.BlockSpec((LANES, T), lambda p, i, *_: (p, 0)), pair_all,
                      pl.BlockSpec((SUBLANES, T), lambda p, i, *_: (0, 0)), pair_rows],
            out_specs=pair_rows,
            scratch_shapes=[pltpu.VMEM((HEAD_PAIR, FOX_BLOCK, LANES), F32),
                            pltpu.VMEM((HEAD_PAIR, FOX_BLOCK, LANES), F32)]),
        out_shape=bshape(FOX_W),
        compiler_params=_params("arbitrary", "arbitrary"),
        name="fox",
    )(lim, cend, fq, fk, fv, crow, fgate)

    out = pl.pallas_call(
        _out_kernel,
        grid=(T // OUT_ROWS,),
        in_specs=[rows(D_MODEL, OUT_ROWS), rows(GLA_OUT_W, OUT_ROWS), rows(FOX_W, OUT_ROWS),
                  rows(MEM_W, OUT_ROWS), rows(MEM_W, OUT_ROWS),
                  whole((M, MEM_W)), whole((M, MEM_W)),
                  pl.BlockSpec(w_out.shape, lambda i: (0, 0), pipeline_mode=pl.Buffered(1)),
                  whole((1, D_MODEL))],
        out_specs=rows(D_MODEL, OUT_ROWS),
        out_shape=jax.ShapeDtypeStruct((T, D_MODEL), F32),
        scratch_shapes=[pltpu.VMEM((D_MODEL, D_MODEL), BF16)],
        compiler_params=_params("arbitrary"),
        name="out",
    )(x, gla, fox, mq, mg, mk, mv, w_out, out_g[None, :])
    return out


def kernel(x, mem, norm_g, w_in, w_alpha_up, b_alpha, b_forget, gla_norm_g, mem_norm_g,
           w_mem_kv, w_out, final_norm_g):
    assert x.shape[0] == 1 and mem.shape[0] == 1 and norm_g.shape[0] == 1
    assert x.shape[1] % max(PROJ_ROWS, FOX_BLOCK, OUT_ROWS) == 0
    out = _layer(x[0], mem[0], norm_g[0], w_in[0], w_alpha_up[0], b_alpha[0], b_forget[0],
                 gla_norm_g[0], mem_norm_g[0], w_mem_kv[0], w_out[0], final_norm_g)
    return out[None]
```

```python
import functools

import jax
import jax.numpy as jnp
from jax import lax
from jax.experimental import pallas as pl
from jax.experimental.pallas import tpu as pltpu

F32 = jnp.float32
BF16 = jnp.bfloat16

EPS = 1e-6
LANES = 128
SUBLANES = 8

D_MODEL = 1024
GLA_HEADS, GLA_DK, GLA_DV, GLA_RANK = 4, 48, 96, 16
GLA_DK_PAD = 64
GLA_DV_PAD = LANES
GLA_GATE_NORM = 16.0
GLA_CHUNK = 64
FOX_HEADS, FOX_DH = 6, 64
MEM_HEADS, MEM_DH = 4, 64
HEAD_PAIR = 2
GLA_QK_W = GLA_HEADS * GLA_DK_PAD
GLA_V_W = GLA_HEADS * GLA_DV_PAD
GLA_OUT_W = GLA_HEADS * GLA_DV
FOX_W = FOX_HEADS * FOX_DH
MEM_W = MEM_HEADS * MEM_DH
SMALL_W = LANES
FG_LANE0 = 0
LR_LANE0 = SUBLANES

_GROUPS = (("gq", GLA_QK_W), ("gk", GLA_QK_W), ("gv", GLA_V_W), ("gg", GLA_V_W),
           ("fq", FOX_W), ("fk", FOX_W), ("fv", FOX_W), ("fgate", FOX_W),
           ("mq", MEM_W), ("mg", MEM_W), ("small", SMALL_W))
_OFF = {}
_o = 0
for _n, _w in _GROUPS:
    _OFF[_n] = (_o, _o + _w)
    _o += _w
IN_COLS_PAD = _o

PROJ_ROWS = 512
FOX_BLOCK = 512
OUT_ROWS = 1024
VMEM_LIMIT = 56 * 1024 * 1024

NEG_BIG = -1e30
FOX_SKIP_NATS = 105.0
NORM_SLACK = 1.02


def _log_sigmoid(z):
    return jnp.minimum(z, 0.0) - jnp.log(1.0 + jnp.exp(-jnp.abs(z)))


def _silu(z):
    return z / (1.0 + jnp.exp(-z))


def _rms_scale(v, width):
    return lax.rsqrt(jnp.sum(v * v, axis=-1, keepdims=True) * (1.0 / width) + EPS)


def _w_in_segments():
    qk, gw = GLA_HEADS * GLA_DK, GLA_HEADS * GLA_DV
    src = {}
    o = 0
    for name, width in (("gq", qk), ("gk", qk), ("gv", gw), ("lr", GLA_RANK), ("gg", gw),
                        ("fq", FOX_W), ("fk", FOX_W), ("fv", FOX_W), ("fg", FOX_HEADS),
                        ("fgate", FOX_W), ("mq", MEM_W), ("mg", MEM_W)):
        src[name] = o
        o += width
    segs = []
    for name, d, d_pad in (("gq", GLA_DK, GLA_DK_PAD), ("gk", GLA_DK, GLA_DK_PAD),
                           ("gv", GLA_DV, GLA_DV_PAD), ("gg", GLA_DV, GLA_DV_PAD)):
        segs += [(src[name] + h * d, _OFF[name][0] + h * d_pad, d) for h in range(GLA_HEADS)]
    segs += [(src[name], _OFF[name][0], _OFF[name][1] - _OFF[name][0])
             for name in ("fq", "fk", "fv", "fgate", "mq", "mg")]
    segs += [(src["fg"], _OFF["small"][0] + FG_LANE0, FOX_HEADS),
             (src["lr"], _OFF["small"][0] + LR_LANE0, GLA_RANK)]
    return tuple(segs)


def _proj_kernel(x_ref, g_ref, w_in_ref, wa_ref, ba_ref, bf_ref, seg_ref, ng_ref,
                 gla_ref, fq_ref, fk_ref, fv_ref, fgate_ref,
                 mq_ref, mg_ref, crow_ref, qn2_ref, kn2_ref,
                 carry_ref, wt_ref, gq_ref, gk_ref, gv_ref, gg_ref, loga_ref,
                 s_ref, lhs_ref, kv_ref, dec_ref, sprev_ref):
    rows = x_ref.shape[0]
    k_chunks = D_MODEL // LANES

    @pl.when(pl.program_id(0) == 0)
    def _():
        carry_ref[...] = jnp.zeros_like(carry_ref)
        s_ref[...] = jnp.zeros_like(s_ref)
        wt_ref[...] = jnp.zeros_like(wt_ref)
        for s0, d0, width in _w_in_segments():
            for c in range(k_chunks):
                wt_ref[d0:d0 + width, c * LANES:(c + 1) * LANES] = (
                    w_in_ref[pl.ds(s0 * k_chunks + c, width, stride=k_chunks), :].astype(BF16))

    x = x_ref[...]
    xn = (x * _rms_scale(x, D_MODEL) * g_ref[...]).astype(BF16)
    nt = (((1,), (1,)), ((), ()))

    def proj(first, last):
        lo, hi = _OFF[first][0], _OFF[last][1]
        y = lax.dot_general(xn, wt_ref[lo:hi, :], nt, preferred_element_type=F32)
        return lambda name: y[:, _OFF[name][0] - lo:_OFF[name][1] - lo]

    tail = proj("mq", "small")
    small = tail("small")
    logf = _log_sigmoid(small + bf_ref[...])
    c = logf.T[0:SUBLANES, :]
    lane = lax.broadcasted_iota(jnp.int32, c.shape, 1)
    shift = 1
    while shift < rows:
        c = c + jnp.where(lane >= shift, pltpu.roll(c, shift, axis=1), 0.0)
        shift *= 2
    c = c + carry_ref[:, 0:1]
    crow_ref[...] = c
    carry_ref[...] = jnp.broadcast_to(c[:, rows - 1:rows], carry_ref.shape)

    z = jnp.dot(small.astype(BF16), wa_ref[...], preferred_element_type=F32) + ba_ref[...]
    loga_ref[...] = _log_sigmoid(z) * (1.0 / GLA_GATE_NORM)
    gla = proj("gq", "gg")
    gq_ref[...] = gla("gq").astype(BF16)
    gk_ref[...] = gla("gk").astype(BF16)
    gv_ref[...] = gla("gv").astype(BF16)
    gg_ref[...] = _silu(gla("gg")).astype(BF16)
    _gla_block(gq_ref, gk_ref, gv_ref, loga_ref, gg_ref, ng_ref, gla_ref,
               s_ref, lhs_ref, kv_ref, dec_ref, sprev_ref)

    mq_ref[...] = (tail("mq") * MEM_DH ** -0.5).astype(BF16)
    mg_ref[...] = _silu(tail("mg")).astype(BF16)

    fox = proj("fq", "fgate")
    fq = (fox("fq") * FOX_DH ** -0.5).astype(BF16)
    fk = fox("fk").astype(BF16)
    fq_ref[...] = fq
    fk_ref[...] = fox("fk").T.astype(BF16)
    fv_ref[...] = fox("fv").astype(BF16)
    fgate_ref[...] = _silu(fox("fgate")).astype(BF16)

    def max_sq_norm(v):
        v32 = v.astype(F32)
        n2 = jnp.dot((v32 * v32).astype(BF16), seg_ref[...], preferred_element_type=F32)
        return jnp.max(n2, axis=0, keepdims=True)

    qn2_ref[0] = max_sq_norm(fq)
    kn2_ref[0] = max_sq_norm(fk)


def _memkv_kernel(mem_ref, g_ref, w_ref, mk_ref, mv_ref):
    m = mem_ref[...]
    mn = (m * _rms_scale(m, D_MODEL) * g_ref[...]).astype(BF16)
    kv = jnp.dot(mn, w_ref[...].astype(BF16), preferred_element_type=F32)
    mk_ref[...] = kv[:, :MEM_W].astype(BF16)
    mv_ref[...] = kv[:, MEM_W:].astype(BF16)


def _gla_block(q_ref, k_ref, v_ref, loga_ref, gate_ref, ng_ref, o_ref,
               s_ref, lhs_ref, kv_ref, dec_ref, sprev_ref):
    C = GLA_CHUNK
    W = HEAD_PAIR * GLA_DV_PAD
    n_chunks = q_ref.shape[0] // C

    row = lax.broadcasted_iota(jnp.int32, (C, LANES), 0)
    lane = lax.broadcasted_iota(jnp.int32, (C, LANES), 1)
    lo_k = lane < GLA_DK_PAD
    causal = row >= jnp.where(lo_k, lane, lane - GLA_DK_PAD)
    lo_v = lax.broadcasted_iota(jnp.int32, (C, W), 1) < GLA_DV_PAD
    st_row = lax.broadcasted_iota(jnp.int32, (LANES, W), 0)
    st_lane = lax.broadcasted_iota(jnp.int32, (LANES, W), 1)
    own = (st_row < GLA_DK_PAD) == (st_lane < GLA_DV_PAD)
    eye = (lax.broadcasted_iota(jnp.int32, (LANES, LANES), 0)
           == lax.broadcasted_iota(jnp.int32, (LANES, LANES), 1))
    scale = GLA_DK ** -0.5
    nt = (((1,), (1,)), ((), ()))
    tn = (((0,), (0,)), ((), ()))
    ng = jnp.concatenate([ng_ref[...]] * HEAD_PAIR, axis=1)

    pairs = range(GLA_HEADS // HEAD_PAIR)

    def local(ci):
        rs = slice(ci * C, (ci + 1) * C)
        for p in pairs:
            ls = slice(p * LANES, (p + 1) * LANES)
            vs = slice(p * W, (p + 1) * W)
            b = loga_ref[rs, ls]
            shift = 1
            while shift < C:
                b = b + jnp.where(row >= shift, pltpu.roll(b, shift, axis=0), 0.0)
                shift *= 2
            b_last = b[C - 1:C, :]
            k2 = k_ref[rs, ls].astype(F32)
            qd = (q_ref[rs, ls].astype(F32) * scale * jnp.exp(b)).astype(BF16)
            kd = (k2 * jnp.exp(-b)).astype(BF16)
            ke = (k2 * jnp.exp(b_last - b)).astype(BF16)
            zk = jnp.zeros_like(kd)
            kd_blk = jnp.concatenate([jnp.where(lo_k, kd, zk), jnp.where(lo_k, zk, kd)], axis=0)
            attn = lax.dot_general(qd, kd_blk, nt, preferred_element_type=F32)
            lhs_ref[rs, vs] = jnp.concatenate([jnp.where(causal, attn, 0.0).astype(BF16), qd], axis=1)
            kv = lax.dot_general(ke, v_ref[rs, vs], tn, preferred_element_type=F32)
            kv_ref[p, ci] = jnp.where(own, kv, 0.0)
            dcol = jnp.exp(jnp.sum(jnp.where(eye, jnp.broadcast_to(b_last, (LANES, LANES)), 0.0),
                                   axis=1, keepdims=True))
            dec_ref[p, ci] = jnp.broadcast_to(dcol, (LANES, LANES))

    def scan(ci):
        for p in pairs:
            s_prev = s_ref[p]
            sprev_ref[p, ci] = s_prev.astype(BF16)
            s_ref[p] = jnp.tile(dec_ref[p, ci], (1, HEAD_PAIR)) * s_prev + kv_ref[p, ci]

    def output(ci):
        rs = slice(ci * C, (ci + 1) * C)
        for p in pairs:
            vs = slice(p * W, (p + 1) * W)
            v2 = v_ref[rs, vs]
            zv = jnp.zeros_like(v2)
            v_blk = jnp.concatenate([jnp.where(lo_v, v2, zv), jnp.where(lo_v, zv, v2)], axis=0)
            o = jnp.dot(lhs_ref[rs, vs], jnp.concatenate([v_blk, sprev_ref[p, ci]], axis=0),
                        preferred_element_type=F32)
            o2 = o * o
            ms = jnp.where(lo_v, jnp.sum(o2[:, :GLA_DV_PAD], axis=1, keepdims=True),
                           jnp.sum(o2[:, GLA_DV_PAD:], axis=1, keepdims=True))
            on = o * lax.rsqrt(ms * (1.0 / GLA_DV) + EPS) * ng
            og = (on * gate_ref[rs, vs].astype(F32)).astype(BF16)
            for hh in range(HEAD_PAIR):
                c0 = (p * HEAD_PAIR + hh) * GLA_DV
                o_ref[rs, c0:c0 + GLA_DV] = og[:, hh * GLA_DV_PAD:hh * GLA_DV_PAD + GLA_DV]

    for phase in (local, scan, output):
        for ci in range(n_chunks):
            phase(ci)


def _fox_kernel(lim_ref, cend_ref, q_ref, k_ref, v_ref, crow_ref, gate_ref, o_ref, m_ref, acc_ref):
    blk = q_ref.shape[0]
    pair = pl.program_id(0)
    qi = pl.program_id(1)
    nblk = pl.num_programs(1)
    q0 = pl.multiple_of(qi * blk, blk)
    lane = lax.broadcasted_iota(jnp.int32, (1, LANES), 1)
    lo_lanes = lane < FOX_DH
    reps = blk // LANES

    q = q_ref[...]
    zero = jnp.zeros_like(q)
    q_stack = jnp.concatenate([jnp.where(lo_lanes, q, zero), jnp.where(lo_lanes, zero, q)], axis=0)
    m_ref[...] = jnp.full_like(m_ref, NEG_BIG)
    acc_ref[...] = jnp.zeros_like(acc_ref)

    def head_row(c8, h):
        sub = lax.broadcasted_iota(jnp.int32, c8.shape, 0)
        return jnp.sum(jnp.where(sub == h, c8, 0.0), axis=0, keepdims=True)

    c_q0 = [head_row(crow_ref[:, pl.ds(q0, LANES)], pair * HEAD_PAIR + hh)[:, 0:1]
            for hh in range(HEAD_PAIR)]

    def step(k0, masked, penalty=0.0):
        ks = pl.ds(k0, blk)
        kb = k_ref[:, ks]
        vb = v_ref[ks, :]
        one = jnp.ones_like(vb)
        vaug = (jnp.where(lo_lanes, vb, one), jnp.where(lo_lanes, one, vb))
        s_all = jnp.dot(q_stack, kb, preferred_element_type=F32)
        for hh in range(HEAD_PAIR):
            h = pair * HEAD_PAIR + hh
            bias = (c_q0[hh] + penalty) - head_row(crow_ref[:, ks], h)
            s = s_all[hh * blk:(hh + 1) * blk] + bias
            if masked:
                qpos = lax.broadcasted_iota(jnp.int32, (blk, blk), 0)
                kpos = lax.broadcasted_iota(jnp.int32, (blk, blk), 1)
                s = jnp.where(kpos <= qpos, s, NEG_BIG)
            m_prev = m_ref[hh]
            m_new = jnp.maximum(m_prev, jnp.max(s, axis=1, keepdims=True))
            p = jnp.exp(s - jnp.tile(m_new, (1, reps)))
            alpha = jnp.exp(m_prev - m_new)
            pv = jnp.dot(p.astype(BF16), vaug[hh], preferred_element_type=F32)
            acc_ref[hh] = alpha * acc_ref[hh] + pv
            m_ref[hh] = m_new

    step(q0, masked=True)
    m_min = [jnp.min(m_ref[hh]) for hh in range(HEAD_PAIR)]
    step(pl.multiple_of(jnp.maximum(qi - 1, 0) * blk, blk), masked=False,
         penalty=jnp.where(qi > 0, 0.0, NEG_BIG))

    def live(j):
        jj = jnp.maximum(j, 0)
        prev = jnp.maximum(qi - 1, 0)
        keep = False
        for hh in range(HEAD_PAIR):
            base = (pair * HEAD_PAIR + hh) * nblk
            bound = (lim_ref[base + qi] + cend_ref[base + prev] - cend_ref[base + jj]) - m_min[hh]
            keep = jnp.logical_or(keep, jnp.logical_not(bound <= 0.0))
        return jnp.logical_and(j >= 0, keep)

    def body(j):
        step(pl.multiple_of(j * blk, blk), masked=False)
        step(pl.multiple_of((j - 1) * blk, blk), masked=False)
        return j - 2

    j_rest = lax.while_loop(lambda j: live(j - 1), body, qi - 2)

    @pl.when(live(j_rest))
    def _():
        step(pl.multiple_of(jnp.maximum(j_rest, 0) * blk, blk), masked=False)

    outs = []
    for hh in range(HEAD_PAIR):
        acc = acc_ref[hh]
        outs.append(acc / pltpu.roll(acc, FOX_DH, axis=1))
    o = jnp.where(lo_lanes, outs[0], outs[1])
    o_ref[...] = (o * gate_ref[...].astype(F32)).astype(BF16)


def _out_kernel(x_ref, gla_ref, fox_ref, mq_ref, mg_ref, mk_ref, mv_ref,
                w_out_ref, fg_ref, o_ref, wo_ref):
    @pl.when(pl.program_id(0) == 0)
    def _():
        wo_ref[...] = w_out_ref[...].astype(BF16)

    lane = lax.broadcasted_iota(jnp.int32, (1, LANES), 1)
    lo_lanes = lane < MEM_DH
    nt = (((1,), (1,)), ((), ()))
    mem_parts = []
    for p in range(MEM_HEADS // HEAD_PAIR):
        ls = slice(p * LANES, (p + 1) * LANES)
        q = mq_ref[:, ls]
        kb = mk_ref[:, ls]
        vb = mv_ref[:, ls]
        zero = jnp.zeros_like(q)
        one = jnp.ones_like(vb)
        qh = (jnp.where(lo_lanes, q, zero), jnp.where(lo_lanes, zero, q))
        vaug = (jnp.where(lo_lanes, vb, one), jnp.where(lo_lanes, one, vb))
        outs = []
        for hh in range(HEAD_PAIR):
            s = lax.dot_general(qh[hh], kb, nt, preferred_element_type=F32)
            pexp = jnp.exp(s - jnp.max(s, axis=1, keepdims=True))
            pv = jnp.dot(pexp.astype(BF16), vaug[hh], preferred_element_type=F32)
            outs.append(pv / pltpu.roll(pv, MEM_DH, axis=1))
        o = jnp.where(lo_lanes, outs[0], outs[1])
        mem_parts.append((o * mg_ref[:, ls].astype(F32)).astype(BF16))
    mixed = jnp.concatenate([gla_ref[...], fox_ref[...]] + mem_parts, axis=1)
    y = x_ref[...] + jnp.dot(mixed, wo_ref[...], preferred_element_type=F32)
    o_ref[...] = y * _rms_scale(y, D_MODEL) * fg_ref[...]


def _pad_heads(w, heads, d, d_pad):
    lead = w.shape[:-1]
    w = w.reshape(lead + (heads, d))
    w = jnp.pad(w, [(0, 0)] * len(lead) + [(0, 0), (0, d_pad - d)])
    return w.reshape(lead + (heads * d_pad,))


def _layout_weights(w_alpha_up, b_alpha, b_forget, gla_norm_g):
    wa = jnp.zeros((SMALL_W, GLA_QK_W), F32)
    wa = wa.at[LR_LANE0:LR_LANE0 + GLA_RANK, :].set(
        _pad_heads(w_alpha_up, GLA_HEADS, GLA_DK, GLA_DK_PAD)).astype(BF16)
    ba = _pad_heads(b_alpha[None, :], GLA_HEADS, GLA_DK, GLA_DK_PAD)
    bf = jnp.zeros((1, SMALL_W), F32).at[0, FG_LANE0:FG_LANE0 + FOX_HEADS].set(b_forget)
    ng = jnp.pad(gla_norm_g, (0, GLA_DV_PAD - GLA_DV))[None, :]
    return wa, ba, bf, ng


def _fox_limits(qn2, kn2, crow, blk):
    nblk = crow.shape[1] // blk
    nproj = qn2.shape[0]
    assert nblk % nproj == 0
    qn = jnp.sqrt(qn2[:, 0, :FOX_HEADS])
    kmax = jnp.sqrt(jnp.max(kn2[:, 0, :FOX_HEADS], axis=0))
    lim = FOX_SKIP_NATS + NORM_SLACK * jnp.repeat(qn, nblk // nproj, axis=0) * kmax[None, :]
    cend = crow[:FOX_HEADS, blk - 1::blk]
    return lim.T.reshape(-1), cend.reshape(-1)


def _params(*sem):
    return pltpu.CompilerParams(dimension_semantics=sem, vmem_limit_bytes=VMEM_LIMIT)


def _layer(x, mem, norm_g, w_in, w_alpha_up, b_alpha, b_forget, gla_norm_g,
           mem_norm_g, w_mem_kv, w_out, out_g):
    w_in_t = jnp.transpose(w_in[None], (0, 2, 1)).reshape(-1, LANES)
    T = x.shape[0]
    M = mem.shape[0]
    wa, ba, bf, ng = _layout_weights(w_alpha_up, b_alpha, b_forget, gla_norm_g)

    def rows(width, n=PROJ_ROWS):
        return pl.BlockSpec((n, width), lambda i: (i, 0))

    def whole(shape):
        return pl.BlockSpec(shape, lambda i: (0,) * len(shape))

    bshape = lambda w: jax.ShapeDtypeStruct((T, w), BF16)
    nproj = T // PROJ_ROWS
    stat_spec = pl.BlockSpec((1, 1, LANES), lambda i: (i, 0, 0))
    stat_shape = jax.ShapeDtypeStruct((nproj, 1, LANES), F32)
    seg = (jnp.arange(FOX_W)[:, None] // FOX_DH == jnp.arange(LANES)[None, :]).astype(BF16)
    gla_pairs, gla_chunks = GLA_HEADS // HEAD_PAIR, PROJ_ROWS // GLA_CHUNK
    pair_w = HEAD_PAIR * GLA_DV_PAD
    (gla, fq, fk, fv, fgate, mq, mg, crow, qn2, kn2) = pl.pallas_call(
        _proj_kernel,
        grid=(nproj,),
        in_specs=[rows(D_MODEL), whole((1, D_MODEL)),
                  pl.BlockSpec(w_in_t.shape, lambda i: (0, 0), pipeline_mode=pl.Buffered(1)),
                  whole((SMALL_W, GLA_QK_W)), whole((1, GLA_QK_W)), whole((1, SMALL_W)),
                  whole((FOX_W, LANES)), whole((1, GLA_DV_PAD))],
        out_specs=[rows(GLA_OUT_W),
                   rows(FOX_W), pl.BlockSpec((FOX_W, PROJ_ROWS), lambda i: (0, i)),
                   rows(FOX_W), rows(FOX_W),
                   rows(MEM_W), rows(MEM_W),
                   pl.BlockSpec((SUBLANES, PROJ_ROWS), lambda i: (0, i)),
                   stat_spec, stat_spec],
        out_shape=[bshape(GLA_OUT_W),
                   bshape(FOX_W), jax.ShapeDtypeStruct((FOX_W, T), BF16),
                   bshape(FOX_W), bshape(FOX_W),
                   bshape(MEM_W), bshape(MEM_W),
                   jax.ShapeDtypeStruct((SUBLANES, T), F32),
                   stat_shape, stat_shape],
        scratch_shapes=[
            pltpu.VMEM((SUBLANES, LANES), F32),
            pltpu.VMEM((IN_COLS_PAD, D_MODEL), BF16),
            pltpu.VMEM((PROJ_ROWS, GLA_QK_W), BF16),
            pltpu.VMEM((PROJ_ROWS, GLA_QK_W), BF16),
            pltpu.VMEM((PROJ_ROWS, GLA_V_W), BF16),
            pltpu.VMEM((PROJ_ROWS, GLA_V_W), BF16),
            pltpu.VMEM((PROJ_ROWS, GLA_QK_W), F32),
            pltpu.VMEM((gla_pairs, LANES, pair_w), F32),
            pltpu.VMEM((PROJ_ROWS, GLA_V_W), BF16),
            pltpu.VMEM((gla_pairs, gla_chunks, LANES, pair_w), F32),
            pltpu.VMEM((gla_pairs, gla_chunks, LANES, LANES), F32),
            pltpu.VMEM((gla_pairs, gla_chunks, LANES, pair_w), BF16)],
        compiler_params=_params("arbitrary"),
        name="proj",
    )(x, norm_g[None, :], w_in_t, wa, ba, bf, seg, ng)

    mk, mv = pl.pallas_call(
        _memkv_kernel,
        out_shape=[jax.ShapeDtypeStruct((M, MEM_W), BF16)] * 2,
        compiler_params=pltpu.CompilerParams(vmem_limit_bytes=VMEM_LIMIT),
        name="memkv",
    )(mem, mem_norm_g[None, :], w_mem_kv)

    lim, cend = _fox_limits(qn2, kn2, crow, FOX_BLOCK)
    pair_rows = pl.BlockSpec((FOX_BLOCK, LANES), lambda p, i, *_: (i, p))
    pair_all = pl.BlockSpec((T, LANES), lambda p, i, *_: (0, p))
    fox = pl.pallas_call(
        _fox_kernel,
        grid_spec=pltpu.PrefetchScalarGridSpec(
            num_scalar_prefetch=2,
            grid=(FOX_HEADS // HEAD_PAIR, T // FOX_BLOCK),
            in_specs=[pair_rows, pl.BlockSpec((LANES, T), lambda p, i, *_: (p, 0)), pair_all,
                      pl.BlockSpec((SUBLANES, T), lambda p, i, *_: (0, 0)), pair_rows],
            out_specs=pair_rows,
            scratch_shapes=[pltpu.VMEM((HEAD_PAIR, FOX_BLOCK, LANES), F32),
                            pltpu.VMEM((HEAD_PAIR, FOX_BLOCK, LANES), F32)]),
        out_shape=bshape(FOX_W),
        compiler_params=_params("arbitrary", "arbitrary"),
        name="fox",
    )(lim, cend, fq, fk, fv, crow, fgate)

    out = pl.pallas_call(
        _out_kernel,
        grid=(T // OUT_ROWS,),
        in_specs=[rows(D_MODEL, OUT_ROWS), rows(GLA_OUT_W, OUT_ROWS), rows(FOX_W, OUT_ROWS),
                  rows(MEM_W, OUT_ROWS), rows(MEM_W, OUT_ROWS),
                  whole((M, MEM_W)), whole((M, MEM_W)),
                  pl.BlockSpec(w_out.shape, lambda i: (0, 0), pipeline_mode=pl.Buffered(1)),
                  whole((1, D_MODEL))],
        out_specs=rows(D_MODEL, OUT_ROWS),
        out_shape=jax.ShapeDtypeStruct((T, D_MODEL), F32),
        scratch_shapes=[pltpu.VMEM((D_MODEL, D_MODEL), BF16)],
        compiler_params=_params("arbitrary"),
        name="out",
    )(x, gla, fox, mq, mg, mk, mv, w_out, out_g[None, :])
    return out


def kernel(x, mem, norm_g, w_in, w_alpha_up, b_alpha, b_forget, gla_norm_g, mem_norm_g,
           w_mem_kv, w_out, final_norm_g):
    assert x.shape[0] == 1 and mem.shape[0] == 1 and norm_g.shape[0] == 1
    assert x.shape[1] % max(PROJ_ROWS, FOX_BLOCK, OUT_ROWS) == 0
    out = _layer(x[0], mem[0], norm_g[0], w_in[0], w_alpha_up[0], b_alpha[0], b_forget[0],
                 gla_norm_g[0], mem_norm_g[0], w_mem_kv[0], w_out[0], final_norm_g)
    return out[None]
```

```python
import functools

import jax
import jax.numpy as jnp
from jax import lax
from jax.experimental import pallas as pl
from jax.experimental.pallas import tpu as pltpu

F32 = jnp.float32
BF16 = jnp.bfloat16

EPS = 1e-6
LANES = 128
SUBLANES = 8

D_MODEL = 1024
GLA_HEADS, GLA_DK, GLA_DV, GLA_RANK = 4, 48, 96, 16
GLA_DK_PAD = 64
GLA_DV_PAD = LANES
GLA_GATE_NORM = 16.0
GLA_CHUNK = 64
FOX_HEADS, FOX_DH = 6, 64
MEM_HEADS, MEM_DH = 4, 64
HEAD_PAIR = 2
GLA_QK_W = GLA_HEADS * GLA_DK_PAD
GLA_V_W = GLA_HEADS * GLA_DV_PAD
GLA_OUT_W = GLA_HEADS * GLA_DV
FOX_W = FOX_HEADS * FOX_DH
MEM_W = MEM_HEADS * MEM_DH
SMALL_W = LANES
FG_LANE0 = 0
LR_LANE0 = SUBLANES

_GROUPS = (("gq", GLA_QK_W), ("gk", GLA_QK_W), ("gv", GLA_V_W), ("gg", GLA_V_W),
           ("fq", FOX_W), ("fk", FOX_W), ("fv", FOX_W), ("fgate", FOX_W),
           ("mq", MEM_W), ("mg", MEM_W), ("small", SMALL_W))
_OFF = {}
_o = 0
for _n, _w in _GROUPS:
    _OFF[_n] = (_o, _o + _w)
    _o += _w
IN_COLS_PAD = _o

PROJ_ROWS = 512
FOX_BLOCK = 1024
FOX_KEYS = 512
OUT_ROWS = 1024
VMEM_LIMIT = 56 * 1024 * 1024

NEG_BIG = -1e30
FOX_SKIP_NATS = 105.0
NORM_SLACK = 1.02


def _log_sigmoid(z):
    return jnp.minimum(z, 0.0) - jnp.log(1.0 + jnp.exp(-jnp.abs(z)))


def _silu(z):
    return z / (1.0 + jnp.exp(-z))


def _rms_scale(v, width):
    return lax.rsqrt(jnp.sum(v * v, axis=-1, keepdims=True) * (1.0 / width) + EPS)


def _w_in_segments():
    qk, gw = GLA_HEADS * GLA_DK, GLA_HEADS * GLA_DV
    src = {}
    o = 0
    for name, width in (("gq", qk), ("gk", qk), ("gv", gw), ("lr", GLA_RANK), ("gg", gw),
                        ("fq", FOX_W), ("fk", FOX_W), ("fv", FOX_W), ("fg", FOX_HEADS),
                        ("fgate", FOX_W), ("mq", MEM_W), ("mg", MEM_W)):
        src[name] = o
        o += width
    segs = []
    for name, d, d_pad in (("gq", GLA_DK, GLA_DK_PAD), ("gk", GLA_DK, GLA_DK_PAD),
                           ("gv", GLA_DV, GLA_DV_PAD), ("gg", GLA_DV, GLA_DV_PAD)):
        segs += [(src[name] + h * d, _OFF[name][0] + h * d_pad, d) for h in range(GLA_HEADS)]
    segs += [(src[name], _OFF[name][0], _OFF[name][1] - _OFF[name][0])
             for name in ("fq", "fk", "fv", "fgate", "mq", "mg")]
    segs += [(src["fg"], _OFF["small"][0] + FG_LANE0, FOX_HEADS),
             (src["lr"], _OFF["small"][0] + LR_LANE0, GLA_RANK)]
    return tuple(segs)


def _proj_kernel(x_ref, g_ref, w_in_ref, wa_ref, ba_ref, bf_ref, seg_ref, ng_ref,
                 gla_ref, fq_ref, fk_ref, fv_ref, fgate_ref,
                 mq_ref, mg_ref, crow_ref, qn2_ref, kn2_ref,
                 carry_ref, wt_ref, gq_ref, gk_ref, gv_ref, gg_ref, loga_ref,
                 s_ref, lhs_ref, kv_ref, dec_ref, sprev_ref):
    rows = x_ref.shape[0]
    k_chunks = D_MODEL // LANES

    @pl.when(pl.program_id(0) == 0)
    def _():
        carry_ref[...] = jnp.zeros_like(carry_ref)
        s_ref[...] = jnp.zeros_like(s_ref)
        wt_ref[...] = jnp.zeros_like(wt_ref)
        for s0, d0, width in _w_in_segments():
            for c in range(k_chunks):
                wt_ref[d0:d0 + width, c * LANES:(c + 1) * LANES] = (
                    w_in_ref[pl.ds(s0 * k_chunks + c, width, stride=k_chunks), :].astype(BF16))

    x = x_ref[...]
    xn = (x * _rms_scale(x, D_MODEL) * g_ref[...]).astype(BF16)
    nt = (((1,), (1,)), ((), ()))

    def proj(first, last):
        lo, hi = _OFF[first][0], _OFF[last][1]
        y = lax.dot_general(xn, wt_ref[lo:hi, :], nt, preferred_element_type=F32)
        return lambda name: y[:, _OFF[name][0] - lo:_OFF[name][1] - lo]

    tail = proj("mq", "small")
    gla = proj("gq", "gg")
    small = tail("small")
    logf = _log_sigmoid(small + bf_ref[...])
    c = logf.T[0:SUBLANES, :]
    lane = lax.broadcasted_iota(jnp.int32, c.shape, 1)
    shift = 1
    while shift < rows:
        c = c + jnp.where(lane >= shift, pltpu.roll(c, shift, axis=1), 0.0)
        shift *= 2
    c = c + carry_ref[:, 0:1]
    crow_ref[...] = c
    carry_ref[...] = jnp.broadcast_to(c[:, rows - 1:rows], carry_ref.shape)

    z = jnp.dot(small.astype(BF16), wa_ref[...], preferred_element_type=F32) + ba_ref[...]
    loga_ref[...] = _log_sigmoid(z) * (1.0 / GLA_GATE_NORM)
    gq_ref[...] = gla("gq").astype(BF16)
    gk_ref[...] = gla("gk").astype(BF16)
    gv_ref[...] = gla("gv").astype(BF16)
    gg_ref[...] = _silu(gla("gg")).astype(BF16)
    gla_local, gla_scan, gla_output = _gla_block(
        gq_ref, gk_ref, gv_ref, loga_ref, gg_ref, ng_ref, gla_ref,
        s_ref, lhs_ref, kv_ref, dec_ref, sprev_ref)

    def max_sq_norm(v):
        v32 = v.astype(F32)
        n2 = jnp.dot((v32 * v32).astype(BF16), seg_ref[...], preferred_element_type=F32)
        return jnp.max(n2, axis=0, keepdims=True)

    gla_local()
    fox_qk = proj("fq", "fk")
    mq_ref[...] = (tail("mq") * MEM_DH ** -0.5).astype(BF16)
    mg_ref[...] = _silu(tail("mg")).astype(BF16)
    gla_scan()
    fq = (fox_qk("fq") * FOX_DH ** -0.5).astype(BF16)
    fk = fox_qk("fk").astype(BF16)
    fq_ref[...] = fq
    fk_ref[...] = fox_qk("fk").T.astype(BF16)
    gla_output()
    fox_vg = proj("fv", "fgate")
    fv_ref[...] = fox_vg("fv").astype(BF16)
    fgate_ref[...] = _silu(fox_vg("fgate")).astype(BF16)
    qn2_ref[0] = max_sq_norm(fq)
    kn2_ref[0] = max_sq_norm(fk)


def _memkv_kernel(mem_ref, g_ref, w_ref, mk_ref, mv_ref):
    m = mem_ref[...]
    mn = (m * _rms_scale(m, D_MODEL) * g_ref[...]).astype(BF16)
    kv = jnp.dot(mn, w_ref[...].astype(BF16), preferred_element_type=F32)
    mk_ref[...] = kv[:, :MEM_W].astype(BF16)
    mv_ref[...] = kv[:, MEM_W:].astype(BF16)


def _gla_block(q_ref, k_ref, v_ref, loga_ref, gate_ref, ng_ref, o_ref,
               s_ref, lhs_ref, kv_ref, dec_ref, sprev_ref):
    C = GLA_CHUNK
    W = HEAD_PAIR * GLA_DV_PAD
    n_chunks = q_ref.shape[0] // C

    row = lax.broadcasted_iota(jnp.int32, (C, LANES), 0)
    lane = lax.broadcasted_iota(jnp.int32, (C, LANES), 1)
    lo_k = lane < GLA_DK_PAD
    causal = row >= jnp.where(lo_k, lane, lane - GLA_DK_PAD)
    lo_v = lax.broadcasted_iota(jnp.int32, (C, W), 1) < GLA_DV_PAD
    st_row = lax.broadcasted_iota(jnp.int32, (LANES, W), 0)
    st_lane = lax.broadcasted_iota(jnp.int32, (LANES, W), 1)
    own = (st_row < GLA_DK_PAD) == (st_lane < GLA_DV_PAD)
    eye = (lax.broadcasted_iota(jnp.int32, (LANES, LANES), 0)
           == lax.broadcasted_iota(jnp.int32, (LANES, LANES), 1))
    scale = GLA_DK ** -0.5
    nt = (((1,), (1,)), ((), ()))
    tn = (((0,), (0,)), ((), ()))
    ng = jnp.concatenate([ng_ref[...]] * HEAD_PAIR, axis=1)

    pairs = range(GLA_HEADS // HEAD_PAIR)

    def local(ci):
        rs = slice(ci * C, (ci + 1) * C)
        for p in pairs:
            ls = slice(p * LANES, (p + 1) * LANES)
            vs = slice(p * W, (p + 1) * W)
            b = loga_ref[rs, ls]
            shift = 1
            while shift < C:
                b = b + jnp.where(row >= shift, pltpu.roll(b, shift, axis=0), 0.0)
                shift *= 2
            b_last = b[C - 1:C, :]
            k2 = k_ref[rs, ls].astype(F32)
            qd = (q_ref[rs, ls].astype(F32) * scale * jnp.exp(b)).astype(BF16)
            kd = (k2 * jnp.exp(-b)).astype(BF16)
            ke = (k2 * jnp.exp(b_last - b)).astype(BF16)
            zk = jnp.zeros_like(kd)
            kd_blk = jnp.concatenate([jnp.where(lo_k, kd, zk), jnp.where(lo_k, zk, kd)], axis=0)
            attn = lax.dot_general(qd, kd_blk, nt, preferred_element_type=F32)
            lhs_ref[rs, vs] = jnp.concatenate([jnp.where(causal, attn, 0.0).astype(BF16), qd], axis=1)
            kv = lax.dot_general(ke, v_ref[rs, vs], tn, preferred_element_type=F32)
            kv_ref[p, ci] = jnp.where(own, kv, 0.0)
            dcol = jnp.exp(jnp.sum(jnp.where(eye, jnp.broadcast_to(b_last, (LANES, LANES)), 0.0),
                                   axis=1, keepdims=True))
            dec_ref[p, ci] = jnp.broadcast_to(dcol, (LANES, LANES))

    def scan(ci):
        for p in pairs:
            s_prev = s_ref[p]
            sprev_ref[p, ci] = s_prev.astype(BF16)
            s_ref[p] = jnp.tile(dec_ref[p, ci], (1, HEAD_PAIR)) * s_prev + kv_ref[p, ci]

    def output(ci):
        rs = slice(ci * C, (ci + 1) * C)
        for p in pairs:
            vs = slice(p * W, (p + 1) * W)
            v2 = v_ref[rs, vs]
            zv = jnp.zeros_like(v2)
            v_blk = jnp.concatenate([jnp.where(lo_v, v2, zv), jnp.where(lo_v, zv, v2)], axis=0)
            o = jnp.dot(lhs_ref[rs, vs], jnp.concatenate([v_blk, sprev_ref[p, ci]], axis=0),
                        preferred_element_type=F32)
            o2 = o * o
            ms = jnp.where(lo_v, jnp.sum(o2[:, :GLA_DV_PAD], axis=1, keepdims=True),
                           jnp.sum(o2[:, GLA_DV_PAD:], axis=1, keepdims=True))
            on = o * lax.rsqrt(ms * (1.0 / GLA_DV) + EPS) * ng
            og = (on * gate_ref[rs, vs].astype(F32)).astype(BF16)
            for hh in range(HEAD_PAIR):
                c0 = (p * HEAD_PAIR + hh) * GLA_DV
                o_ref[rs, c0:c0 + GLA_DV] = og[:, hh * GLA_DV_PAD:hh * GLA_DV_PAD + GLA_DV]

    def all_chunks(phase):
        return lambda: [phase(ci) for ci in range(n_chunks)]

    return all_chunks(local), all_chunks(scan), all_chunks(output)


def _fox_kernel(lim_ref, cend_ref, q_ref, k_ref, v_ref, crow_ref, gate_ref, o_ref, m_ref, acc_ref):
    blk = FOX_KEYS
    streams = range(q_ref.shape[0] // blk)
    pair = pl.program_id(0)
    qi = pl.program_id(1)
    nblk = pl.num_programs(1) * len(streams)
    lane = lax.broadcasted_iota(jnp.int32, (1, LANES), 1)
    lo_lanes = lane < FOX_DH
    reps = blk // LANES
    diag = [qi * len(streams) + s for s in streams]

    q = q_ref[...]
    zero = jnp.zeros_like(q)
    q_lo, q_hi = jnp.where(lo_lanes, q, zero), jnp.where(lo_lanes, zero, q)
    q_stack = [jnp.concatenate([q_lo[s * blk:(s + 1) * blk], q_hi[s * blk:(s + 1) * blk]], axis=0)
               for s in streams]
    m_ref[...] = jnp.full_like(m_ref, NEG_BIG)
    acc_ref[...] = jnp.zeros_like(acc_ref)

    def head_row(c8, h):
        sub = lax.broadcasted_iota(jnp.int32, c8.shape, 0)
        return jnp.sum(jnp.where(sub == h, c8, 0.0), axis=0, keepdims=True)

    c_q0 = [[head_row(crow_ref[:, pl.ds(pl.multiple_of(diag[s] * blk, blk), LANES)],
                      pair * HEAD_PAIR + hh)[:, 0:1] for hh in range(HEAD_PAIR)]
            for s in streams]

    def step(s, j, masked):
        penalty = jnp.where(j >= 0, 0.0, NEG_BIG)
        ks = pl.ds(pl.multiple_of(jnp.maximum(j, 0) * blk, blk), blk)
        rows = slice(s * blk, (s + 1) * blk)
        kb = k_ref[:, ks]
        vb = v_ref[ks, :]
        one = jnp.ones_like(vb)
        vaug = (jnp.where(lo_lanes, vb, one), jnp.where(lo_lanes, one, vb))
        s_all = jnp.dot(q_stack[s], kb, preferred_element_type=F32)
        for hh in range(HEAD_PAIR):
            h = pair * HEAD_PAIR + hh
            bias = (c_q0[s][hh] + penalty) - head_row(crow_ref[:, ks], h)
            sc = s_all[hh * blk:(hh + 1) * blk] + bias
            if masked:
                qpos = lax.broadcasted_iota(jnp.int32, (blk, blk), 0)
                kpos = lax.broadcasted_iota(jnp.int32, (blk, blk), 1)
                sc = jnp.where(kpos <= qpos, sc, NEG_BIG)
            m_prev = m_ref[hh, rows]
            m_new = jnp.maximum(m_prev, jnp.max(sc, axis=1, keepdims=True))
            p = jnp.exp(sc - jnp.tile(m_new, (1, reps)))
            alpha = jnp.exp(m_prev - m_new)
            pv = jnp.dot(p.astype(BF16), vaug[hh], preferred_element_type=F32)
            acc_ref[hh, rows] = alpha * acc_ref[hh, rows] + pv
            m_ref[hh, rows] = m_new

    for s in streams:
        step(s, diag[s], masked=True)
    m_min = [[jnp.min(m_ref[hh, s * blk:(s + 1) * blk]) for hh in range(HEAD_PAIR)]
             for s in streams]
    for s in streams:
        step(s, diag[s] - 1, masked=False)

    def live(t):
        keep = False
        for s in streams:
            j = diag[s] - 1 - t
            jj = jnp.maximum(j, 0)
            prev = jnp.maximum(diag[s] - 1, 0)
            for hh in range(HEAD_PAIR):
                base = (pair * HEAD_PAIR + hh) * nblk
                bound = (lim_ref[base + diag[s]] + cend_ref[base + prev]
                         - cend_ref[base + jj]) - m_min[s][hh]
                keep = jnp.logical_or(keep, jnp.logical_and(j >= 0, jnp.logical_not(bound <= 0.0)))
        return keep

    def body(t):
        for s in streams:
            step(s, diag[s] - 1 - t, masked=False)
        return t + 1

    lax.while_loop(live, body, 1)

    outs = []
    for hh in range(HEAD_PAIR):
        acc = acc_ref[hh]
        outs.append(acc / pltpu.roll(acc, FOX_DH, axis=1))
    o = jnp.where(lo_lanes, outs[0], outs[1])
    o_ref[...] = (o * gate_ref[...].astype(F32)).astype(BF16)


def _out_kernel(x_ref, gla_ref, fox_ref, mq_ref, mg_ref, mk_ref, mv_ref,
                w_out_ref, fg_ref, o_ref, wo_ref):
    @pl.when(pl.program_id(0) == 0)
    def _():
        wo_ref[...] = w_out_ref[...].astype(BF16)

    lane = lax.broadcasted_iota(jnp.int32, (1, LANES), 1)
    lo_lanes = lane < MEM_DH
    nt = (((1,), (1,)), ((), ()))
    mem_parts = []
    for p in range(MEM_HEADS // HEAD_PAIR):
        ls = slice(p * LANES, (p + 1) * LANES)
        q = mq_ref[:, ls]
        kb = mk_ref[:, ls]
        vb = mv_ref[:, ls]
        zero = jnp.zeros_like(q)
        one = jnp.ones_like(vb)
        qh = (jnp.where(lo_lanes, q, zero), jnp.where(lo_lanes, zero, q))
        vaug = (jnp.where(lo_lanes, vb, one), jnp.where(lo_lanes, one, vb))
        outs = []
        for hh in range(HEAD_PAIR):
            s = lax.dot_general(qh[hh], kb, nt, preferred_element_type=F32)
            pexp = jnp.exp(s - jnp.max(s, axis=1, keepdims=True))
            pv = jnp.dot(pexp.astype(BF16), vaug[hh], preferred_element_type=F32)
            outs.append(pv / pltpu.roll(pv, MEM_DH, axis=1))
        o = jnp.where(lo_lanes, outs[0], outs[1])
        mem_parts.append((o * mg_ref[:, ls].astype(F32)).astype(BF16))
    mixed = jnp.concatenate([gla_ref[...], fox_ref[...]] + mem_parts, axis=1)
    y = x_ref[...] + jnp.dot(mixed, wo_ref[...], preferred_element_type=F32)
    o_ref[...] = y * _rms_scale(y, D_MODEL) * fg_ref[...]


def _pad_heads(w, heads, d, d_pad):
    lead = w.shape[:-1]
    w = w.reshape(lead + (heads, d))
    w = jnp.pad(w, [(0, 0)] * len(lead) + [(0, 0), (0, d_pad - d)])
    return w.reshape(lead + (heads * d_pad,))


def _layout_weights(w_alpha_up, b_alpha, b_forget, gla_norm_g):
    wa = jnp.zeros((SMALL_W, GLA_QK_W), F32)
    wa = wa.at[LR_LANE0:LR_LANE0 + GLA_RANK, :].set(
        _pad_heads(w_alpha_up, GLA_HEADS, GLA_DK, GLA_DK_PAD)).astype(BF16)
    ba = _pad_heads(b_alpha[None, :], GLA_HEADS, GLA_DK, GLA_DK_PAD)
    bf = jnp.zeros((1, SMALL_W), F32).at[0, FG_LANE0:FG_LANE0 + FOX_HEADS].set(b_forget)
    ng = jnp.pad(gla_norm_g, (0, GLA_DV_PAD - GLA_DV))[None, :]
    return wa, ba, bf, ng


def _fox_limits(qn2, kn2, crow, blk):
    nblk = crow.shape[1] // blk
    nproj = qn2.shape[0]
    assert nblk % nproj == 0
    qn = jnp.sqrt(qn2[:, 0, :FOX_HEADS])
    kmax = jnp.sqrt(jnp.max(kn2[:, 0, :FOX_HEADS], axis=0))
    lim = FOX_SKIP_NATS + NORM_SLACK * jnp.repeat(qn, nblk // nproj, axis=0) * kmax[None, :]
    cend = crow[:FOX_HEADS, blk - 1::blk]
    return lim.T.reshape(-1), cend.reshape(-1)


def _params(*sem):
    return pltpu.CompilerParams(dimension_semantics=sem, vmem_limit_bytes=VMEM_LIMIT)


def _layer(x, mem, norm_g, w_in, w_alpha_up, b_alpha, b_forget, gla_norm_g,
           mem_norm_g, w_mem_kv, w_out, out_g):
    w_in_t = jnp.transpose(w_in[None], (0, 2, 1)).reshape(-1, LANES)
    T = x.shape[0]
    M = mem.shape[0]
    wa, ba, bf, ng = _layout_weights(w_alpha_up, b_alpha, b_forget, gla_norm_g)

    def rows(width, n=PROJ_ROWS):
        return pl.BlockSpec((n, width), lambda i: (i, 0))

    def whole(shape):
        return pl.BlockSpec(shape, lambda i: (0,) * len(shape))

    bshape = lambda w: jax.ShapeDtypeStruct((T, w), BF16)
    nproj = T // PROJ_ROWS
    stat_spec = pl.BlockSpec((1, 1, LANES), lambda i: (i, 0, 0))
    stat_shape = jax.ShapeDtypeStruct((nproj, 1, LANES), F32)
    seg = (jnp.arange(FOX_W)[:, None] // FOX_DH == jnp.arange(LANES)[None, :]).astype(BF16)
    gla_pairs, gla_chunks = GLA_HEADS // HEAD_PAIR, PROJ_ROWS // GLA_CHUNK
    pair_w = HEAD_PAIR * GLA_DV_PAD
    (gla, fq, fk, fv, fgate, mq, mg, crow, qn2, kn2) = pl.pallas_call(
        _proj_kernel,
        grid=(nproj,),
        in_specs=[rows(D_MODEL), whole((1, D_MODEL)),
                  pl.BlockSpec(w_in_t.shape, lambda i: (0, 0), pipeline_mode=pl.Buffered(1)),
                  whole((SMALL_W, GLA_QK_W)), whole((1, GLA_QK_W)), whole((1, SMALL_W)),
                  whole((FOX_W, LANES)), whole((1, GLA_DV_PAD))],
        out_specs=[rows(GLA_OUT_W),
                   rows(FOX_W), pl.BlockSpec((FOX_W, PROJ_ROWS), lambda i: (0, i)),
                   rows(FOX_W), rows(FOX_W),
                   rows(MEM_W), rows(MEM_W),
                   pl.BlockSpec((SUBLANES, PROJ_ROWS), lambda i: (0, i)),
                   stat_spec, stat_spec],
        out_shape=[bshape(GLA_OUT_W),
                   bshape(FOX_W), jax.ShapeDtypeStruct((FOX_W, T), BF16),
                   bshape(FOX_W), bshape(FOX_W),
                   bshape(MEM_W), bshape(MEM_W),
                   jax.ShapeDtypeStruct((SUBLANES, T), F32),
                   stat_shape, stat_shape],
        scratch_shapes=[
            pltpu.VMEM((SUBLANES, LANES), F32),
            pltpu.VMEM((IN_COLS_PAD, D_MODEL), BF16),
            pltpu.VMEM((PROJ_ROWS, GLA_QK_W), BF16),
            pltpu.VMEM((PROJ_ROWS, GLA_QK_W), BF16),
            pltpu.VMEM((PROJ_ROWS, GLA_V_W), BF16),
            pltpu.VMEM((PROJ_ROWS, GLA_V_W), BF16),
            pltpu.VMEM((PROJ_ROWS, GLA_QK_W), F32),
            pltpu.VMEM((gla_pairs, LANES, pair_w), F32),
            pltpu.VMEM((PROJ_ROWS, GLA_V_W), BF16),
            pltpu.VMEM((gla_pairs, gla_chunks, LANES, pair_w), F32),
            pltpu.VMEM((gla_pairs, gla_chunks, LANES, LANES), F32),
            pltpu.VMEM((gla_pairs, gla_chunks, LANES, pair_w), BF16)],
        compiler_params=_params("arbitrary"),
        name="proj",
    )(x, norm_g[None, :], w_in_t, wa, ba, bf, seg, ng)

    mk, mv = pl.pallas_call(
        _memkv_kernel,
        out_shape=[jax.ShapeDtypeStruct((M, MEM_W), BF16)] * 2,
        compiler_params=pltpu.CompilerParams(vmem_limit_bytes=VMEM_LIMIT),
        name="memkv",
    )(mem, mem_norm_g[None, :], w_mem_kv)

    lim, cend = _fox_limits(qn2, kn2, crow, FOX_KEYS)
    pair_rows = pl.BlockSpec((FOX_BLOCK, LANES), lambda p, i, *_: (i, p))
    pair_all = pl.BlockSpec((T, LANES), lambda p, i, *_: (0, p))
    fox = pl.pallas_call(
        _fox_kernel,
        grid_spec=pltpu.PrefetchScalarGridSpec(
            num_scalar_prefetch=2,
            grid=(FOX_HEADS // HEAD_PAIR, T // FOX_BLOCK),
            in_specs=[pair_rows, pl.BlockSpec((LANES, T), lambda p, i, *_: (p, 0)), pair_all,
                      pl.BlockSpec((SUBLANES, T), lambda p, i, *_: (0, 0)), pair_rows],
            out_specs=pair_rows,
            scratch_shapes=[pltpu.VMEM((HEAD_PAIR, FOX_BLOCK, LANES), F32),
                            pltpu.VMEM((HEAD_PAIR, FOX_BLOCK, LANES), F32)]),
        out_shape=bshape(FOX_W),
        compiler_params=_params("arbitrary", "arbitrary"),
        name="fox",
    )(lim, cend, fq, fk, fv, crow, fgate)

    out = pl.pallas_call(
        _out_kernel,
        grid=(T // OUT_ROWS,),
        in_specs=[rows(D_MODEL, OUT_ROWS), rows(GLA_OUT_W, OUT_ROWS), rows(FOX_W, OUT_ROWS),
                  rows(MEM_W, OUT_ROWS), rows(MEM_W, OUT_ROWS),
                  whole((M, MEM_W)), whole((M, MEM_W)),
                  pl.BlockSpec(w_out.shape, lambda i: (0, 0), pipeline_mode=pl.Buffered(1)),
                  whole((1, D_MODEL))],
        out_specs=rows(D_MODEL, OUT_ROWS),
        out_shape=jax.ShapeDtypeStruct((T, D_MODEL), F32),
        scratch_shapes=[pltpu.VMEM((D_MODEL, D_MODEL), BF16)],
        compiler_params=_params("arbitrary"),
        name="out",
    )(x, gla, fox, mq, mg, mk, mv, w_out, out_g[None, :])
    return out


def kernel(x, mem, norm_g, w_in, w_alpha_up, b_alpha, b_forget, gla_norm_g, mem_norm_g,
           w_mem_kv, w_out, final_norm_g):
    assert x.shape[0] == 1 and mem.shape[0] == 1 and norm_g.shape[0] == 1
    assert x.shape[1] % max(PROJ_ROWS, FOX_BLOCK, OUT_ROWS) == 0
    out = _layer(x[0], mem[0], norm_g[0], w_in[0], w_alpha_up[0], b_alpha[0], b_forget[0],
                 gla_norm_g[0], mem_norm_g[0], w_mem_kv[0], w_out[0], final_norm_g)
    return out[None]
```

```python
import functools

import jax
import jax.numpy as jnp
from jax import lax
from jax.experimental import pallas as pl
from jax.experimental.pallas import tpu as pltpu

F32 = jnp.float32
BF16 = jnp.bfloat16

EPS = 1e-6
LANES = 128
SUBLANES = 8

D_MODEL = 1024
GLA_HEADS, GLA_DK, GLA_DV, GLA_RANK = 4, 48, 96, 16
GLA_DK_PAD = 64
GLA_DV_PAD = LANES
GLA_GATE_NORM = 16.0
GLA_CHUNK = 64
FOX_HEADS, FOX_DH = 6, 64
MEM_HEADS, MEM_DH = 4, 64
HEAD_PAIR = 2
GLA_QK_W = GLA_HEADS * GLA_DK_PAD
GLA_V_W = GLA_HEADS * GLA_DV_PAD
GLA_OUT_W = GLA_HEADS * GLA_DV
FOX_W = FOX_HEADS * FOX_DH
MEM_W = MEM_HEADS * MEM_DH
SMALL_W = LANES
FG_LANE0 = 0
LR_LANE0 = SUBLANES

_GROUPS = (("gq", GLA_QK_W), ("gk", GLA_QK_W), ("gv", GLA_V_W), ("gg", GLA_V_W),
           ("fq", FOX_W), ("fk", FOX_W), ("fv", FOX_W), ("fgate", FOX_W),
           ("mq", MEM_W), ("mg", MEM_W), ("small", SMALL_W))
_OFF = {}
_o = 0
for _n, _w in _GROUPS:
    _OFF[_n] = (_o, _o + _w)
    _o += _w
IN_COLS_PAD = _o

PROJ_ROWS = 512
FOX_BLOCK = 2048
FOX_KEYS = 512
OUT_ROWS = 1024
VMEM_LIMIT = 56 * 1024 * 1024

NEG_BIG = -1e30
FOX_SKIP_NATS = 105.0
NORM_SLACK = 1.02


def _log_sigmoid(z):
    return jnp.minimum(z, 0.0) - jnp.log(1.0 + jnp.exp(-jnp.abs(z)))


def _silu(z):
    return z / (1.0 + jnp.exp(-z))


def _rms_scale(v, width):
    return lax.rsqrt(jnp.sum(v * v, axis=-1, keepdims=True) * (1.0 / width) + EPS)


def _w_in_segments():
    qk, gw = GLA_HEADS * GLA_DK, GLA_HEADS * GLA_DV
    src = {}
    o = 0
    for name, width in (("gq", qk), ("gk", qk), ("gv", gw), ("lr", GLA_RANK), ("gg", gw),
                        ("fq", FOX_W), ("fk", FOX_W), ("fv", FOX_W), ("fg", FOX_HEADS),
                        ("fgate", FOX_W), ("mq", MEM_W), ("mg", MEM_W)):
        src[name] = o
        o += width
    segs = []
    for name, d, d_pad in (("gq", GLA_DK, GLA_DK_PAD), ("gk", GLA_DK, GLA_DK_PAD),
                           ("gv", GLA_DV, GLA_DV_PAD), ("gg", GLA_DV, GLA_DV_PAD)):
        segs += [(src[name] + h * d, _OFF[name][0] + h * d_pad, d) for h in range(GLA_HEADS)]
    segs += [(src[name], _OFF[name][0], _OFF[name][1] - _OFF[name][0])
             for name in ("fq", "fk", "fv", "fgate", "mq", "mg")]
    segs += [(src["fg"], _OFF["small"][0] + FG_LANE0, FOX_HEADS),
             (src["lr"], _OFF["small"][0] + LR_LANE0, GLA_RANK)]
    return tuple(segs)


def _proj_kernel(x_ref, g_ref, w_in_ref, wa_ref, ba_ref, bf_ref, seg_ref, ng_ref,
                 gla_ref, fq_ref, fk_ref, fv_ref, fgate_ref,
                 mq_ref, mg_ref, crow_ref, qn2_ref, kn2_ref,
                 carry_ref, wt_ref, gq_ref, gk_ref, gv_ref, gg_ref, loga_ref,
                 s_ref, lhs_ref, kv_ref, dec_ref, sprev_ref):
    rows = x_ref.shape[0]
    k_chunks = D_MODEL // LANES

    @pl.when(pl.program_id(0) == 0)
    def _():
        carry_ref[...] = jnp.zeros_like(carry_ref)
        s_ref[...] = jnp.zeros_like(s_ref)
        wt_ref[...] = jnp.zeros_like(wt_ref)
        for s0, d0, width in _w_in_segments():
            for c in range(k_chunks):
                wt_ref[d0:d0 + width, c * LANES:(c + 1) * LANES] = (
                    w_in_ref[pl.ds(s0 * k_chunks + c, width, stride=k_chunks), :].astype(BF16))

    x = x_ref[...]
    xn = (x * _rms_scale(x, D_MODEL) * g_ref[...]).astype(BF16)
    nt = (((1,), (1,)), ((), ()))

    def proj(first, last):
        lo, hi = _OFF[first][0], _OFF[last][1]
        y = lax.dot_general(xn, wt_ref[lo:hi, :], nt, preferred_element_type=F32)
        return lambda name: y[:, _OFF[name][0] - lo:_OFF[name][1] - lo]

    tail = proj("mq", "small")
    gla = proj("gq", "gg")
    small = tail("small")
    logf = _log_sigmoid(small + bf_ref[...])
    c = logf.T[0:SUBLANES, :]
    lane = lax.broadcasted_iota(jnp.int32, c.shape, 1)
    shift = 1
    while shift < rows:
        c = c + jnp.where(lane >= shift, pltpu.roll(c, shift, axis=1), 0.0)
        shift *= 2
    c = c + carry_ref[:, 0:1]
    crow_ref[...] = c
    carry_ref[...] = jnp.broadcast_to(c[:, rows - 1:rows], carry_ref.shape)

    z = jnp.dot(small.astype(BF16), wa_ref[...], preferred_element_type=F32) + ba_ref[...]
    loga_ref[...] = _log_sigmoid(z) * (1.0 / GLA_GATE_NORM)
    gq_ref[...] = gla("gq").astype(BF16)
    gk_ref[...] = gla("gk").astype(BF16)
    gv_ref[...] = gla("gv").astype(BF16)
    gg_ref[...] = _silu(gla("gg")).astype(BF16)
    gla_local, gla_scan, gla_output = _gla_block(
        gq_ref, gk_ref, gv_ref, loga_ref, gg_ref, ng_ref, gla_ref,
        s_ref, lhs_ref, kv_ref, dec_ref, sprev_ref)

    def max_sq_norm(v):
        v32 = v.astype(F32)
        n2 = jnp.dot((v32 * v32).astype(BF16), seg_ref[...], preferred_element_type=F32)
        return jnp.max(n2, axis=0, keepdims=True)

    gla_local()
    fox_qk = proj("fq", "fk")
    mq_ref[...] = (tail("mq") * MEM_DH ** -0.5).astype(BF16)
    mg_ref[...] = _silu(tail("mg")).astype(BF16)
    gla_scan()
    fq = (fox_qk("fq") * FOX_DH ** -0.5).astype(BF16)
    fk = fox_qk("fk").astype(BF16)
    fq_ref[...] = fq
    fk_ref[...] = fox_qk("fk").T.astype(BF16)
    gla_output()
    fox_vg = proj("fv", "fgate")
    fv_ref[...] = fox_vg("fv").astype(BF16)
    fgate_ref[...] = _silu(fox_vg("fgate")).astype(BF16)
    qn2_ref[0] = max_sq_norm(fq)
    kn2_ref[0] = max_sq_norm(fk)


def _memkv_kernel(mem_ref, g_ref, w_ref, mk_ref, mv_ref):
    m = mem_ref[...]
    mn = (m * _rms_scale(m, D_MODEL) * g_ref[...]).astype(BF16)
    kv = jnp.dot(mn, w_ref[...].astype(BF16), preferred_element_type=F32)
    mk_ref[...] = kv[:, :MEM_W].astype(BF16)
    mv_ref[...] = kv[:, MEM_W:].astype(BF16)


def _gla_block(q_ref, k_ref, v_ref, loga_ref, gate_ref, ng_ref, o_ref,
               s_ref, lhs_ref, kv_ref, dec_ref, sprev_ref):
    C = GLA_CHUNK
    W = HEAD_PAIR * GLA_DV_PAD
    n_chunks = q_ref.shape[0] // C

    row = lax.broadcasted_iota(jnp.int32, (C, LANES), 0)
    lane = lax.broadcasted_iota(jnp.int32, (C, LANES), 1)
    lo_k = lane < GLA_DK_PAD
    causal = row >= jnp.where(lo_k, lane, lane - GLA_DK_PAD)
    lo_v = lax.broadcasted_iota(jnp.int32, (C, W), 1) < GLA_DV_PAD
    st_row = lax.broadcasted_iota(jnp.int32, (LANES, W), 0)
    st_lane = lax.broadcasted_iota(jnp.int32, (LANES, W), 1)
    own = (st_row < GLA_DK_PAD) == (st_lane < GLA_DV_PAD)
    eye = (lax.broadcasted_iota(jnp.int32, (LANES, LANES), 0)
           == lax.broadcasted_iota(jnp.int32, (LANES, LANES), 1))
    scale = GLA_DK ** -0.5
    nt = (((1,), (1,)), ((), ()))
    tn = (((0,), (0,)), ((), ()))
    ng = jnp.concatenate([ng_ref[...]] * HEAD_PAIR, axis=1)

    pairs = range(GLA_HEADS // HEAD_PAIR)

    def local(ci):
        rs = slice(ci * C, (ci + 1) * C)
        for p in pairs:
            ls = slice(p * LANES, (p + 1) * LANES)
            vs = slice(p * W, (p + 1) * W)
            b = loga_ref[rs, ls]
            shift = 1
            while shift < C:
                b = b + jnp.where(row >= shift, pltpu.roll(b, shift, axis=0), 0.0)
                shift *= 2
            b_last = b[C - 1:C, :]
            k2 = k_ref[rs, ls].astype(F32)
            qd = (q_ref[rs, ls].astype(F32) * scale * jnp.exp(b)).astype(BF16)
            kd = (k2 * jnp.exp(-b)).astype(BF16)
            ke = (k2 * jnp.exp(b_last - b)).astype(BF16)
            zk = jnp.zeros_like(kd)
            kd_blk = jnp.concatenate([jnp.where(lo_k, kd, zk), jnp.where(lo_k, zk, kd)], axis=0)
            attn = lax.dot_general(qd, kd_blk, nt, preferred_element_type=F32)
            lhs_ref[rs, vs] = jnp.concatenate([jnp.where(causal, attn, 0.0).astype(BF16), qd], axis=1)
            kv = lax.dot_general(ke, v_ref[rs, vs], tn, preferred_element_type=F32)
            kv_ref[p, ci] = jnp.where(own, kv, 0.0)
            dcol = jnp.exp(jnp.sum(jnp.where(eye, jnp.broadcast_to(b_last, (LANES, LANES)), 0.0),
                                   axis=1, keepdims=True))
            dec_ref[p, ci] = jnp.broadcast_to(dcol, (LANES, LANES))

    def scan(ci):
        for p in pairs:
            s_prev = s_ref[p]
            sprev_ref[p, ci] = s_prev.astype(BF16)
            s_ref[p] = jnp.tile(dec_ref[p, ci], (1, HEAD_PAIR)) * s_prev + kv_ref[p, ci]

    def output(ci):
        rs = slice(ci * C, (ci + 1) * C)
        for p in pairs:
            vs = slice(p * W, (p + 1) * W)
            v2 = v_ref[rs, vs]
            zv = jnp.zeros_like(v2)
            v_blk = jnp.concatenate([jnp.where(lo_v, v2, zv), jnp.where(lo_v, zv, v2)], axis=0)
            o = jnp.dot(lhs_ref[rs, vs], jnp.concatenate([v_blk, sprev_ref[p, ci]], axis=0),
                        preferred_element_type=F32)
            o2 = o * o
            ms = jnp.where(lo_v, jnp.sum(o2[:, :GLA_DV_PAD], axis=1, keepdims=True),
                           jnp.sum(o2[:, GLA_DV_PAD:], axis=1, keepdims=True))
            on = o * lax.rsqrt(ms * (1.0 / GLA_DV) + EPS) * ng
            og = (on * gate_ref[rs, vs].astype(F32)).astype(BF16)
            for hh in range(HEAD_PAIR):
                c0 = (p * HEAD_PAIR + hh) * GLA_DV
                o_ref[rs, c0:c0 + GLA_DV] = og[:, hh * GLA_DV_PAD:hh * GLA_DV_PAD + GLA_DV]

    def all_chunks(phase):
        return lambda: [phase(ci) for ci in range(n_chunks)]

    return all_chunks(local), all_chunks(scan), all_chunks(output)


def _fox_kernel(lim_ref, cend_ref, q_ref, k_ref, v_ref, crow_ref, gate_ref, o_ref, m_ref, acc_ref):
    blk = FOX_KEYS
    streams = range(q_ref.shape[0] // blk)
    pair = pl.program_id(0)
    qi = pl.program_id(1)
    nblk = pl.num_programs(1) * len(streams)
    lane = lax.broadcasted_iota(jnp.int32, (1, LANES), 1)
    lo_lanes = lane < FOX_DH
    reps = blk // LANES
    diag = [qi * len(streams) + s for s in streams]

    q = q_ref[...]
    zero = jnp.zeros_like(q)
    q_lo, q_hi = jnp.where(lo_lanes, q, zero), jnp.where(lo_lanes, zero, q)
    q_stack = [jnp.concatenate([q_lo[s * blk:(s + 1) * blk], q_hi[s * blk:(s + 1) * blk]], axis=0)
               for s in streams]
    m_ref[...] = jnp.full_like(m_ref, NEG_BIG)
    acc_ref[...] = jnp.zeros_like(acc_ref)

    def head_row(c8, h):
        sub = lax.broadcasted_iota(jnp.int32, c8.shape, 0)
        return jnp.sum(jnp.where(sub == h, c8, 0.0), axis=0, keepdims=True)

    c_q0 = [[head_row(crow_ref[:, pl.ds(pl.multiple_of(diag[s] * blk, blk), LANES)],
                      pair * HEAD_PAIR + hh)[:, 0:1] for hh in range(HEAD_PAIR)]
            for s in streams]

    def step(s, j, masked):
        penalty = jnp.where(j >= 0, 0.0, NEG_BIG)
        ks = pl.ds(pl.multiple_of(jnp.maximum(j, 0) * blk, blk), blk)
        rows = slice(s * blk, (s + 1) * blk)
        kb = k_ref[:, ks]
        vb = v_ref[ks, :]
        one = jnp.ones_like(vb)
        vaug = (jnp.where(lo_lanes, vb, one), jnp.where(lo_lanes, one, vb))
        s_all = jnp.dot(q_stack[s], kb, preferred_element_type=F32)
        for hh in range(HEAD_PAIR):
            h = pair * HEAD_PAIR + hh
            bias = (c_q0[s][hh] + penalty) - head_row(crow_ref[:, ks], h)
            sc = s_all[hh * blk:(hh + 1) * blk] + bias
            if masked:
                qpos = lax.broadcasted_iota(jnp.int32, (blk, blk), 0)
                kpos = lax.broadcasted_iota(jnp.int32, (blk, blk), 1)
                sc = jnp.where(kpos <= qpos, sc, NEG_BIG)
            m_prev = m_ref[hh, rows]
            m_new = jnp.maximum(m_prev, jnp.max(sc, axis=1, keepdims=True))
            p = jnp.exp(sc - jnp.tile(m_new, (1, reps)))
            alpha = jnp.exp(m_prev - m_new)
            pv = jnp.dot(p.astype(BF16), vaug[hh], preferred_element_type=F32)
            acc_ref[hh, rows] = alpha * acc_ref[hh, rows] + pv
            m_ref[hh, rows] = m_new

    for s in streams:
        step(s, diag[s], masked=True)
    m_min = [[jnp.min(m_ref[hh, s * blk:(s + 1) * blk]) for hh in range(HEAD_PAIR)]
             for s in streams]
    for s in streams:
        step(s, diag[s] - 1, masked=False)

    def live(t):
        keep = False
        for s in streams:
            j = diag[s] - 1 - t
            jj = jnp.maximum(j, 0)
            prev = jnp.maximum(diag[s] - 1, 0)
            for hh in range(HEAD_PAIR):
                base = (pair * HEAD_PAIR + hh) * nblk
                bound = (lim_ref[base + diag[s]] + cend_ref[base + prev]
                         - cend_ref[base + jj]) - m_min[s][hh]
                keep = jnp.logical_or(keep, jnp.logical_and(j >= 0, jnp.logical_not(bound <= 0.0)))
        return keep

    def body(t):
        for s in streams:
            step(s, diag[s] - 1 - t, masked=False)
        return t + 1

    lax.while_loop(live, body, 1)

    outs = []
    for hh in range(HEAD_PAIR):
        acc = acc_ref[hh]
        outs.append(acc / pltpu.roll(acc, FOX_DH, axis=1))
    o = jnp.where(lo_lanes, outs[0], outs[1])
    o_ref[...] = (o * gate_ref[...].astype(F32)).astype(BF16)


def _out_kernel(x_ref, gla_ref, fox_ref, mq_ref, mg_ref, mk_ref, mv_ref,
                w_out_ref, fg_ref, o_ref, wo_ref):
    @pl.when(pl.program_id(0) == 0)
    def _():
        wo_ref[...] = w_out_ref[...].astype(BF16)

    lane = lax.broadcasted_iota(jnp.int32, (1, LANES), 1)
    lo_lanes = lane < MEM_DH
    nt = (((1,), (1,)), ((), ()))
    mem_parts = []
    for p in range(MEM_HEADS // HEAD_PAIR):
        ls = slice(p * LANES, (p + 1) * LANES)
        q = mq_ref[:, ls]
        kb = mk_ref[:, ls]
        vb = mv_ref[:, ls]
        zero = jnp.zeros_like(q)
        one = jnp.ones_like(vb)
        qh = (jnp.where(lo_lanes, q, zero), jnp.where(lo_lanes, zero, q))
        vaug = (jnp.where(lo_lanes, vb, one), jnp.where(lo_lanes, one, vb))
        outs = []
        for hh in range(HEAD_PAIR):
            s = lax.dot_general(qh[hh], kb, nt, preferred_element_type=F32)
            pexp = jnp.exp(s - jnp.max(s, axis=1, keepdims=True))
            pv = jnp.dot(pexp.astype(BF16), vaug[hh], preferred_element_type=F32)
            outs.append(pv / pltpu.roll(pv, MEM_DH, axis=1))
        o = jnp.where(lo_lanes, outs[0], outs[1])
        mem_parts.append((o * mg_ref[:, ls].astype(F32)).astype(BF16))
    mixed = jnp.concatenate([gla_ref[...], fox_ref[...]] + mem_parts, axis=1)
    y = x_ref[...] + jnp.dot(mixed, wo_ref[...], preferred_element_type=F32)
    o_ref[...] = y * _rms_scale(y, D_MODEL) * fg_ref[...]


def _pad_heads(w, heads, d, d_pad):
    lead = w.shape[:-1]
    w = w.reshape(lead + (heads, d))
    w = jnp.pad(w, [(0, 0)] * len(lead) + [(0, 0), (0, d_pad - d)])
    return w.reshape(lead + (heads * d_pad,))


def _layout_weights(w_alpha_up, b_alpha, b_forget, gla_norm_g):
    wa = jnp.zeros((SMALL_W, GLA_QK_W), F32)
    wa = wa.at[LR_LANE0:LR_LANE0 + GLA_RANK, :].set(
        _pad_heads(w_alpha_up, GLA_HEADS, GLA_DK, GLA_DK_PAD)).astype(BF16)
    ba = _pad_heads(b_alpha[None, :], GLA_HEADS, GLA_DK, GLA_DK_PAD)
    bf = jnp.zeros((1, SMALL_W), F32).at[0, FG_LANE0:FG_LANE0 + FOX_HEADS].set(b_forget)
    ng = jnp.pad(gla_norm_g, (0, GLA_DV_PAD - GLA_DV))[None, :]
    return wa, ba, bf, ng


def _fox_limits(qn2, kn2, crow, blk):
    nblk = crow.shape[1] // blk
    nproj = qn2.shape[0]
    assert nblk % nproj == 0
    qn = jnp.sqrt(qn2[:, 0, :FOX_HEADS])
    kmax = jnp.sqrt(jnp.max(kn2[:, 0, :FOX_HEADS], axis=0))
    lim = FOX_SKIP_NATS + NORM_SLACK * jnp.repeat(qn, nblk // nproj, axis=0) * kmax[None, :]
    cend = crow[:FOX_HEADS, blk - 1::blk]
    return lim.T.reshape(-1), cend.reshape(-1)


def _params(*sem):
    return pltpu.CompilerParams(dimension_semantics=sem, vmem_limit_bytes=VMEM_LIMIT)


def _layer(x, mem, norm_g, w_in, w_alpha_up, b_alpha, b_forget, gla_norm_g,
           mem_norm_g, w_mem_kv, w_out, out_g):
    w_in_t = jnp.transpose(w_in[None], (0, 2, 1)).reshape(-1, LANES)
    T = x.shape[0]
    M = mem.shape[0]
    wa, ba, bf, ng = _layout_weights(w_alpha_up, b_alpha, b_forget, gla_norm_g)

    def rows(width, n=PROJ_ROWS):
        return pl.BlockSpec((n, width), lambda i: (i, 0))

    def whole(shape):
        return pl.BlockSpec(shape, lambda i: (0,) * len(shape))

    bshape = lambda w: jax.ShapeDtypeStruct((T, w), BF16)
    nproj = T // PROJ_ROWS
    stat_spec = pl.BlockSpec((1, 1, LANES), lambda i: (i, 0, 0))
    stat_shape = jax.ShapeDtypeStruct((nproj, 1, LANES), F32)
    seg = (jnp.arange(FOX_W)[:, None] // FOX_DH == jnp.arange(LANES)[None, :]).astype(BF16)
    gla_pairs, gla_chunks = GLA_HEADS // HEAD_PAIR, PROJ_ROWS // GLA_CHUNK
    pair_w = HEAD_PAIR * GLA_DV_PAD
    (gla, fq, fk, fv, fgate, mq, mg, crow, qn2, kn2) = pl.pallas_call(
        _proj_kernel,
        grid=(nproj,),
        in_specs=[rows(D_MODEL), whole((1, D_MODEL)),
                  pl.BlockSpec(w_in_t.shape, lambda i: (0, 0), pipeline_mode=pl.Buffered(1)),
                  whole((SMALL_W, GLA_QK_W)), whole((1, GLA_QK_W)), whole((1, SMALL_W)),
                  whole((FOX_W, LANES)), whole((1, GLA_DV_PAD))],
        out_specs=[rows(GLA_OUT_W),
                   rows(FOX_W), pl.BlockSpec((FOX_W, PROJ_ROWS), lambda i: (0, i)),
                   rows(FOX_W), rows(FOX_W),
                   rows(MEM_W), rows(MEM_W),
                   pl.BlockSpec((SUBLANES, PROJ_ROWS), lambda i: (0, i)),
                   stat_spec, stat_spec],
        out_shape=[bshape(GLA_OUT_W),
                   bshape(FOX_W), jax.ShapeDtypeStruct((FOX_W, T), BF16),
                   bshape(FOX_W), bshape(FOX_W),
                   bshape(MEM_W), bshape(MEM_W),
                   jax.ShapeDtypeStruct((SUBLANES, T), F32),
                   stat_shape, stat_shape],
        scratch_shapes=[
            pltpu.VMEM((SUBLANES, LANES), F32),
            pltpu.VMEM((IN_COLS_PAD, D_MODEL), BF16),
            pltpu.VMEM((PROJ_ROWS, GLA_QK_W), BF16),
            pltpu.VMEM((PROJ_ROWS, GLA_QK_W), BF16),
            pltpu.VMEM((PROJ_ROWS, GLA_V_W), BF16),
            pltpu.VMEM((PROJ_ROWS, GLA_V_W), BF16),
            pltpu.VMEM((PROJ_ROWS, GLA_QK_W), F32),
            pltpu.VMEM((gla_pairs, LANES, pair_w), F32),
            pltpu.VMEM((PROJ_ROWS, GLA_V_W), BF16),
            pltpu.VMEM((gla_pairs, gla_chunks, LANES, pair_w), F32),
            pltpu.VMEM((gla_pairs, gla_chunks, LANES, LANES), F32),
            pltpu.VMEM((gla_pairs, gla_chunks, LANES, pair_w), BF16)],
        compiler_params=_params("arbitrary"),
        name="proj",
    )(x, norm_g[None, :], w_in_t, wa, ba, bf, seg, ng)

    mk, mv = pl.pallas_call(
        _memkv_kernel,
        out_shape=[jax.ShapeDtypeStruct((M, MEM_W), BF16)] * 2,
        compiler_params=pltpu.CompilerParams(vmem_limit_bytes=VMEM_LIMIT),
        name="memkv",
    )(mem, mem_norm_g[None, :], w_mem_kv)

    lim, cend = _fox_limits(qn2, kn2, crow, FOX_KEYS)
    pair_rows = pl.BlockSpec((FOX_BLOCK, LANES), lambda p, i, *_: (i, p))
    pair_all = pl.BlockSpec((T, LANES), lambda p, i, *_: (0, p))
    fox = pl.pallas_call(
        _fox_kernel,
        grid_spec=pltpu.PrefetchScalarGridSpec(
            num_scalar_prefetch=2,
            grid=(FOX_HEADS // HEAD_PAIR, T // FOX_BLOCK),
            in_specs=[pair_rows, pl.BlockSpec((LANES, T), lambda p, i, *_: (p, 0)), pair_all,
                      pl.BlockSpec((SUBLANES, T), lambda p, i, *_: (0, 0)), pair_rows],
            out_specs=pair_rows,
            scratch_shapes=[pltpu.VMEM((HEAD_PAIR, FOX_BLOCK, LANES), F32),
                            pltpu.VMEM((HEAD_PAIR, FOX_BLOCK, LANES), F32)]),
        out_shape=bshape(FOX_W),
        compiler_params=_params("arbitrary", "arbitrary"),
        name="fox",
    )(lim, cend, fq, fk, fv, crow, fgate)

    out = pl.pallas_call(
        _out_kernel,
        grid=(T // OUT_ROWS,),
        in_specs=[rows(D_MODEL, OUT_ROWS), rows(GLA_OUT_W, OUT_ROWS), rows(FOX_W, OUT_ROWS),
                  rows(MEM_W, OUT_ROWS), rows(MEM_W, OUT_ROWS),
                  whole((M, MEM_W)), whole((M, MEM_W)),
                  pl.BlockSpec(w_out.shape, lambda i: (0, 0), pipeline_mode=pl.Buffered(1)),
                  whole((1, D_MODEL))],
        out_specs=rows(D_MODEL, OUT_ROWS),
        out_shape=jax.ShapeDtypeStruct((T, D_MODEL), F32),
        scratch_shapes=[pltpu.VMEM((D_MODEL, D_MODEL), BF16)],
        compiler_params=_params("arbitrary"),
        name="out",
    )(x, gla, fox, mq, mg, mk, mv, w_out, out_g[None, :])
    return out


def kernel(x, mem, norm_g, w_in, w_alpha_up, b_alpha, b_forget, gla_norm_g, mem_norm_g,
           w_mem_kv, w_out, final_norm_g):
    assert x.shape[0] == 1 and mem.shape[0] == 1 and norm_g.shape[0] == 1
    assert x.shape[1] % max(PROJ_ROWS, FOX_BLOCK, OUT_ROWS) == 0
    out = _layer(x[0], mem[0], norm_g[0], w_in[0], w_alpha_up[0], b_alpha[0], b_forget[0],
                 gla_norm_g[0], mem_norm_g[0], w_mem_kv[0], w_out[0], final_norm_g)
    return out[None]
```

```python
import functools

import jax
import jax.numpy as jnp
from jax import lax
from jax.experimental import pallas as pl
from jax.experimental.pallas import tpu as pltpu

F32 = jnp.float32
BF16 = jnp.bfloat16

EPS = 1e-6
LANES = 128
SUBLANES = 8

D_MODEL = 1024
GLA_HEADS, GLA_DK, GLA_DV, GLA_RANK = 4, 48, 96, 16
GLA_DK_PAD = 64
GLA_DV_PAD = LANES
GLA_GATE_NORM = 16.0
GLA_CHUNK = 64
FOX_HEADS, FOX_DH = 6, 64
MEM_HEADS, MEM_DH = 4, 64
HEAD_PAIR = 2
GLA_QK_W = GLA_HEADS * GLA_DK_PAD
GLA_V_W = GLA_HEADS * GLA_DV_PAD
GLA_OUT_W = GLA_HEADS * GLA_DV
FOX_W = FOX_HEADS * FOX_DH
MEM_W = MEM_HEADS * MEM_DH
SMALL_W = LANES
FG_LANE0 = 0
LR_LANE0 = SUBLANES

_GROUPS = (("gq", GLA_QK_W), ("gk", GLA_QK_W), ("gv", GLA_V_W), ("gg", GLA_V_W),
           ("fq", FOX_W), ("fk", FOX_W), ("fv", FOX_W), ("fgate", FOX_W),
           ("mq", MEM_W), ("mg", MEM_W), ("small", SMALL_W))
_OFF = {}
_o = 0
for _n, _w in _GROUPS:
    _OFF[_n] = (_o, _o + _w)
    _o += _w
IN_COLS_PAD = _o

PROJ_ROWS = 512
FOX_BLOCK = 2048
FOX_KEYS = 512
OUT_ROWS = 1024
VMEM_LIMIT = 56 * 1024 * 1024

NEG_BIG = -1e30
FOX_SKIP_NATS = 105.0
NORM_SLACK = 1.02


def _log_sigmoid(z):
    return jnp.minimum(z, 0.0) - jnp.log(1.0 + jnp.exp(-jnp.abs(z)))


def _silu(z):
    return z / (1.0 + jnp.exp(-z))


def _rms_scale(v, width):
    return lax.rsqrt(jnp.sum(v * v, axis=-1, keepdims=True) * (1.0 / width) + EPS)


def _w_in_segments():
    qk, gw = GLA_HEADS * GLA_DK, GLA_HEADS * GLA_DV
    src = {}
    o = 0
    for name, width in (("gq", qk), ("gk", qk), ("gv", gw), ("lr", GLA_RANK), ("gg", gw),
                        ("fq", FOX_W), ("fk", FOX_W), ("fv", FOX_W), ("fg", FOX_HEADS),
                        ("fgate", FOX_W), ("mq", MEM_W), ("mg", MEM_W)):
        src[name] = o
        o += width
    segs = []
    for name, d, d_pad in (("gq", GLA_DK, GLA_DK_PAD), ("gk", GLA_DK, GLA_DK_PAD),
                           ("gv", GLA_DV, GLA_DV_PAD), ("gg", GLA_DV, GLA_DV_PAD)):
        segs += [(src[name] + h * d, _OFF[name][0] + h * d_pad, d) for h in range(GLA_HEADS)]
    segs += [(src[name], _OFF[name][0], _OFF[name][1] - _OFF[name][0])
             for name in ("fq", "fk", "fv", "fgate", "mq", "mg")]
    segs += [(src["fg"], _OFF["small"][0] + FG_LANE0, FOX_HEADS),
             (src["lr"], _OFF["small"][0] + LR_LANE0, GLA_RANK)]
    return tuple(segs)


def _proj_kernel(x_ref, g_ref, w_in_ref, w_alpha_ref, b_alpha_ref, b_forget_ref, gla_g_ref,
                 gla_ref, fq_ref, fk_ref, fv_ref, fgate_ref,
                 mq_ref, mg_ref, crow_ref, qn2_ref, kn2_ref,
                 carry_ref, wt_ref, wa_ref, ba_ref, bf_ref, ng_ref, seg_ref,
                 gq_ref, gk_ref, gv_ref, gg_ref, loga_ref,
                 s_ref, lhs_ref, kv_ref, dec_ref, sprev_ref):
    rows = x_ref.shape[0]
    k_chunks = D_MODEL // LANES

    @pl.when(pl.program_id(0) == 0)
    def _():
        carry_ref[...] = jnp.zeros_like(carry_ref)
        s_ref[...] = jnp.zeros_like(s_ref)
        wa_ref[...] = jnp.zeros_like(wa_ref)
        ba_ref[...] = jnp.zeros_like(ba_ref)
        bf_ref[...] = jnp.zeros_like(bf_ref)
        ng_ref[...] = jnp.zeros_like(ng_ref)
        for h in range(GLA_HEADS):
            src = slice(h * GLA_DK, (h + 1) * GLA_DK)
            dst = slice(h * GLA_DK_PAD, h * GLA_DK_PAD + GLA_DK)
            wa_ref[LR_LANE0:LR_LANE0 + GLA_RANK, dst] = w_alpha_ref[:, src]
            ba_ref[:, dst] = b_alpha_ref[:, src]
        bf_ref[:, FG_LANE0:FG_LANE0 + FOX_HEADS] = b_forget_ref[...]
        ng_ref[:, 0:GLA_DV] = gla_g_ref[...]
        seg_ref[...] = (lax.broadcasted_iota(jnp.int32, seg_ref.shape, 0) // FOX_DH
                        == lax.broadcasted_iota(jnp.int32, seg_ref.shape, 1)).astype(BF16)
        wt_ref[...] = jnp.zeros_like(wt_ref)
        for s0, d0, width in _w_in_segments():
            for c in range(k_chunks):
                wt_ref[d0:d0 + width, c * LANES:(c + 1) * LANES] = (
                    w_in_ref[pl.ds(s0 * k_chunks + c, width, stride=k_chunks), :].astype(BF16))

    x = x_ref[...]
    xn = (x * _rms_scale(x, D_MODEL) * g_ref[...]).astype(BF16)
    nt = (((1,), (1,)), ((), ()))

    def proj(first, last):
        lo, hi = _OFF[first][0], _OFF[last][1]
        y = lax.dot_general(xn, wt_ref[lo:hi, :], nt, preferred_element_type=F32)
        return lambda name: y[:, _OFF[name][0] - lo:_OFF[name][1] - lo]

    tail = proj("mq", "small")
    gla = proj("gq", "gg")
    small = tail("small")
    logf = _log_sigmoid(small + bf_ref[...])
    c = logf.T[0:SUBLANES, :]
    lane = lax.broadcasted_iota(jnp.int32, c.shape, 1)
    shift = 1
    while shift < rows:
        c = c + jnp.where(lane >= shift, pltpu.roll(c, shift, axis=1), 0.0)
        shift *= 2
    c = c + carry_ref[:, 0:1]
    crow_ref[...] = c
    carry_ref[...] = jnp.broadcast_to(c[:, rows - 1:rows], carry_ref.shape)

    z = (jnp.dot(small.astype(BF16), wa_ref[...].astype(BF16), preferred_element_type=F32)
         + ba_ref[...])
    loga_ref[...] = _log_sigmoid(z) * (1.0 / GLA_GATE_NORM)
    gq_ref[...] = gla("gq").astype(BF16)
    gk_ref[...] = gla("gk").astype(BF16)
    gv_ref[...] = gla("gv").astype(BF16)
    gg_ref[...] = _silu(gla("gg")).astype(BF16)
    gla_local, gla_scan, gla_output = _gla_block(
        gq_ref, gk_ref, gv_ref, loga_ref, gg_ref, ng_ref, gla_ref,
        s_ref, lhs_ref, kv_ref, dec_ref, sprev_ref)

    def max_sq_norm(v):
        v32 = v.astype(F32)
        n2 = jnp.dot((v32 * v32).astype(BF16), seg_ref[...], preferred_element_type=F32)
        return jnp.max(n2, axis=0, keepdims=True)

    gla_local()
    fox_qk = proj("fq", "fk")
    mq_ref[...] = (tail("mq") * MEM_DH ** -0.5).astype(BF16)
    mg_ref[...] = _silu(tail("mg")).astype(BF16)
    gla_scan()
    fq = (fox_qk("fq") * FOX_DH ** -0.5).astype(BF16)
    fk = fox_qk("fk").astype(BF16)
    fq_ref[...] = fq
    fk_ref[...] = fox_qk("fk").T.astype(BF16)
    gla_output()
    fox_vg = proj("fv", "fgate")
    fv_ref[...] = fox_vg("fv").astype(BF16)
    fgate_ref[...] = _silu(fox_vg("fgate")).astype(BF16)
    qn2_ref[0] = max_sq_norm(fq)
    kn2_ref[0] = max_sq_norm(fk)


def _memkv_kernel(mem_ref, g_ref, w_ref, mk_ref, mv_ref):
    m = mem_ref[...]
    mn = (m * _rms_scale(m, D_MODEL) * g_ref[...]).astype(BF16)
    kv = jnp.dot(mn, w_ref[...].astype(BF16), preferred_element_type=F32)
    mk_ref[...] = kv[:, :MEM_W].astype(BF16)
    mv_ref[...] = kv[:, MEM_W:].astype(BF16)


def _gla_block(q_ref, k_ref, v_ref, loga_ref, gate_ref, ng_ref, o_ref,
               s_ref, lhs_ref, kv_ref, dec_ref, sprev_ref):
    C = GLA_CHUNK
    W = HEAD_PAIR * GLA_DV_PAD
    n_chunks = q_ref.shape[0] // C

    row = lax.broadcasted_iota(jnp.int32, (C, LANES), 0)
    lane = lax.broadcasted_iota(jnp.int32, (C, LANES), 1)
    lo_k = lane < GLA_DK_PAD
    causal = row >= jnp.where(lo_k, lane, lane - GLA_DK_PAD)
    lo_v = lax.broadcasted_iota(jnp.int32, (C, W), 1) < GLA_DV_PAD
    st_row = lax.broadcasted_iota(jnp.int32, (LANES, W), 0)
    st_lane = lax.broadcasted_iota(jnp.int32, (LANES, W), 1)
    own = (st_row < GLA_DK_PAD) == (st_lane < GLA_DV_PAD)
    eye = (lax.broadcasted_iota(jnp.int32, (LANES, LANES), 0)
           == lax.broadcasted_iota(jnp.int32, (LANES, LANES), 1))
    scale = GLA_DK ** -0.5
    nt = (((1,), (1,)), ((), ()))
    tn = (((0,), (0,)), ((), ()))
    ng = jnp.concatenate([ng_ref[...]] * HEAD_PAIR, axis=1)

    pairs = range(GLA_HEADS // HEAD_PAIR)

    def local(ci):
        rs = slice(ci * C, (ci + 1) * C)
        for p in pairs:
            ls = slice(p * LANES, (p + 1) * LANES)
            vs = slice(p * W, (p + 1) * W)
            b = loga_ref[rs, ls]
            shift = 1
            while shift < C:
                b = b + jnp.where(row >= shift, pltpu.roll(b, shift, axis=0), 0.0)
                shift *= 2
            b_last = b[C - 1:C, :]
            k2 = k_ref[rs, ls].astype(F32)
            qd = (q_ref[rs, ls].astype(F32) * scale * jnp.exp(b)).astype(BF16)
            kd = (k2 * jnp.exp(-b)).astype(BF16)
            ke = (k2 * jnp.exp(b_last - b)).astype(BF16)
            zk = jnp.zeros_like(kd)
            kd_blk = jnp.concatenate([jnp.where(lo_k, kd, zk), jnp.where(lo_k, zk, kd)], axis=0)
            attn = lax.dot_general(qd, kd_blk, nt, preferred_element_type=F32)
            lhs_ref[rs, vs] = jnp.concatenate([jnp.where(causal, attn, 0.0).astype(BF16), qd], axis=1)
            kv = lax.dot_general(ke, v_ref[rs, vs], tn, preferred_element_type=F32)
            kv_ref[p, ci] = jnp.where(own, kv, 0.0)
            dcol = jnp.exp(jnp.sum(jnp.where(eye, jnp.broadcast_to(b_last, (LANES, LANES)), 0.0),
                                   axis=1, keepdims=True))
            dec_ref[p, ci] = jnp.broadcast_to(dcol, (LANES, LANES))

    def scan(ci):
        for p in pairs:
            s_prev = s_ref[p]
            sprev_ref[p, ci] = s_prev.astype(BF16)
            s_ref[p] = jnp.tile(dec_ref[p, ci], (1, HEAD_PAIR)) * s_prev + kv_ref[p, ci]

    def output(ci):
        rs = slice(ci * C, (ci + 1) * C)
        for p in pairs:
            vs = slice(p * W, (p + 1) * W)
            v2 = v_ref[rs, vs]
            zv = jnp.zeros_like(v2)
            v_blk = jnp.concatenate([jnp.where(lo_v, v2, zv), jnp.where(lo_v, zv, v2)], axis=0)
            o = jnp.dot(lhs_ref[rs, vs], jnp.concatenate([v_blk, sprev_ref[p, ci]], axis=0),
                        preferred_element_type=F32)
            o2 = o * o
            ms = jnp.where(lo_v, jnp.sum(o2[:, :GLA_DV_PAD], axis=1, keepdims=True),
                           jnp.sum(o2[:, GLA_DV_PAD:], axis=1, keepdims=True))
            on = o * lax.rsqrt(ms * (1.0 / GLA_DV) + EPS) * ng
            og = (on * gate_ref[rs, vs].astype(F32)).astype(BF16)
            for hh in range(HEAD_PAIR):
                c0 = (p * HEAD_PAIR + hh) * GLA_DV
                o_ref[rs, c0:c0 + GLA_DV] = og[:, hh * GLA_DV_PAD:hh * GLA_DV_PAD + GLA_DV]

    def all_chunks(phase):
        return lambda: [phase(ci) for ci in range(n_chunks)]

    return all_chunks(local), all_chunks(scan), all_chunks(output)


def _fox_kernel(qn2_ref, kn2_ref, cend_ref, q_ref, k_ref, v_ref, crow_ref, gate_ref, o_ref,
                m_ref, acc_ref):
    blk = FOX_KEYS
    streams = range(q_ref.shape[0] // blk)
    pair = pl.program_id(0)
    qi = pl.program_id(1)
    nblk = pl.num_programs(1) * len(streams)
    lane = lax.broadcasted_iota(jnp.int32, (1, LANES), 1)
    lo_lanes = lane < FOX_DH
    reps = blk // LANES
    diag = [qi * len(streams) + s for s in streams]

    q = q_ref[...]
    zero = jnp.zeros_like(q)
    q_lo, q_hi = jnp.where(lo_lanes, q, zero), jnp.where(lo_lanes, zero, q)
    q_stack = [jnp.concatenate([q_lo[s * blk:(s + 1) * blk], q_hi[s * blk:(s + 1) * blk]], axis=0)
               for s in streams]
    m_ref[...] = jnp.full_like(m_ref, NEG_BIG)
    acc_ref[...] = jnp.zeros_like(acc_ref)

    def head_row(c8, h):
        sub = lax.broadcasted_iota(jnp.int32, c8.shape, 0)
        return jnp.sum(jnp.where(sub == h, c8, 0.0), axis=0, keepdims=True)

    c_q0 = [[head_row(crow_ref[:, pl.ds(pl.multiple_of(diag[s] * blk, blk), LANES)],
                      pair * HEAD_PAIR + hh)[:, 0:1] for hh in range(HEAD_PAIR)]
            for s in streams]

    def step(s, j, masked):
        penalty = jnp.where(j >= 0, 0.0, NEG_BIG)
        ks = pl.ds(pl.multiple_of(jnp.maximum(j, 0) * blk, blk), blk)
        rows = slice(s * blk, (s + 1) * blk)
        kb = k_ref[:, ks]
        vb = v_ref[ks, :]
        one = jnp.ones_like(vb)
        vaug = (jnp.where(lo_lanes, vb, one), jnp.where(lo_lanes, one, vb))
        s_all = jnp.dot(q_stack[s], kb, preferred_element_type=F32)
        for hh in range(HEAD_PAIR):
            h = pair * HEAD_PAIR + hh
            bias = (c_q0[s][hh] + penalty) - head_row(crow_ref[:, ks], h)
            sc = s_all[hh * blk:(hh + 1) * blk] + bias
            if masked:
                qpos = lax.broadcasted_iota(jnp.int32, (blk, blk), 0)
                kpos = lax.broadcasted_iota(jnp.int32, (blk, blk), 1)
                sc = jnp.where(kpos <= qpos, sc, NEG_BIG)
            m_prev = m_ref[hh, rows]
            m_new = jnp.maximum(m_prev, jnp.max(sc, axis=1, keepdims=True))
            p = jnp.exp(sc - jnp.tile(m_new, (1, reps)))
            alpha = jnp.exp(m_prev - m_new)
            pv = jnp.dot(p.astype(BF16), vaug[hh], preferred_element_type=F32)
            acc_ref[hh, rows] = alpha * acc_ref[hh, rows] + pv
            m_ref[hh, rows] = m_new

    for s in streams:
        step(s, diag[s], masked=True)
    m_min = [[jnp.min(m_ref[hh, s * blk:(s + 1) * blk]) for hh in range(HEAD_PAIR)]
             for s in streams]
    for s in streams:
        step(s, diag[s] - 1, masked=False)

    n_stat = kn2_ref.shape[0] // LANES
    k_max2 = [lax.fori_loop(0, n_stat,
                            lambda b, m, h=pair * HEAD_PAIR + hh: jnp.maximum(m, kn2_ref[b * LANES + h]),
                            jnp.float32(0.0))
              for hh in range(HEAD_PAIR)]

    def live(t):
        keep = False
        for s in streams:
            j = diag[s] - 1 - t
            jj = jnp.maximum(j, 0)
            prev = jnp.maximum(diag[s] - 1, 0)
            stat = (diag[s] * blk // PROJ_ROWS) * LANES
            for hh in range(HEAD_PAIR):
                h = pair * HEAD_PAIR + hh
                gap = (FOX_SKIP_NATS + cend_ref[h * nblk + prev] - cend_ref[h * nblk + jj]) - m_min[s][hh]
                norm2 = (NORM_SLACK * NORM_SLACK) * qn2_ref[stat + h] * k_max2[hh]
                dead = jnp.logical_and(gap <= 0.0, norm2 <= gap * gap)
                keep = jnp.logical_or(keep, jnp.logical_and(j >= 0, jnp.logical_not(dead)))
        return keep

    def body(t):
        for s in streams:
            step(s, diag[s] - 1 - t, masked=False)
        return t + 1

    lax.while_loop(live, body, 1)

    outs = []
    for hh in range(HEAD_PAIR):
        acc = acc_ref[hh]
        outs.append(acc / pltpu.roll(acc, FOX_DH, axis=1))
    o = jnp.where(lo_lanes, outs[0], outs[1])
    o_ref[...] = (o * gate_ref[...].astype(F32)).astype(BF16)


def _out_kernel(x_ref, gla_ref, fox_ref, mq_ref, mg_ref, mk_ref, mv_ref,
                w_out_ref, fg_ref, o_ref, wo_ref):
    @pl.when(pl.program_id(0) == 0)
    def _():
        wo_ref[...] = w_out_ref[...].astype(BF16)

    lane = lax.broadcasted_iota(jnp.int32, (1, LANES), 1)
    lo_lanes = lane < MEM_DH
    nt = (((1,), (1,)), ((), ()))
    mem_parts = []
    for p in range(MEM_HEADS // HEAD_PAIR):
        ls = slice(p * LANES, (p + 1) * LANES)
        q = mq_ref[:, ls]
        kb = mk_ref[:, ls]
        vb = mv_ref[:, ls]
        zero = jnp.zeros_like(q)
        one = jnp.ones_like(vb)
        qh = (jnp.where(lo_lanes, q, zero), jnp.where(lo_lanes, zero, q))
        vaug = (jnp.where(lo_lanes, vb, one), jnp.where(lo_lanes, one, vb))
        outs = []
        for hh in range(HEAD_PAIR):
            s = lax.dot_general(qh[hh], kb, nt, preferred_element_type=F32)
            pexp = jnp.exp(s - jnp.max(s, axis=1, keepdims=True))
            pv = jnp.dot(pexp.astype(BF16), vaug[hh], preferred_element_type=F32)
            outs.append(pv / pltpu.roll(pv, MEM_DH, axis=1))
        o = jnp.where(lo_lanes, outs[0], outs[1])
        mem_parts.append((o * mg_ref[:, ls].astype(F32)).astype(BF16))
    mixed = jnp.concatenate([gla_ref[...], fox_ref[...]] + mem_parts, axis=1)
    y = x_ref[...] + jnp.dot(mixed, wo_ref[...], preferred_element_type=F32)
    o_ref[...] = y * _rms_scale(y, D_MODEL) * fg_ref[...]


def _params(*sem):
    return pltpu.CompilerParams(dimension_semantics=sem, vmem_limit_bytes=VMEM_LIMIT)


def _layer(x, mem, norm_g, w_in, w_alpha_up, b_alpha, b_forget, gla_norm_g,
           mem_norm_g, w_mem_kv, w_out, out_g):
    w_in_t = jnp.transpose(w_in[None], (0, 2, 1)).reshape(-1, LANES)
    T = x.shape[0]
    M = mem.shape[0]

    def rows(width, n=PROJ_ROWS):
        return pl.BlockSpec((n, width), lambda i: (i, 0))

    def whole(shape):
        return pl.BlockSpec(shape, lambda i: (0,) * len(shape))

    bshape = lambda w: jax.ShapeDtypeStruct((T, w), BF16)
    nproj = T // PROJ_ROWS
    stat_spec = pl.BlockSpec((1, 1, LANES), lambda i: (i, 0, 0))
    stat_shape = jax.ShapeDtypeStruct((nproj, 1, LANES), F32)
    gla_pairs, gla_chunks = GLA_HEADS // HEAD_PAIR, PROJ_ROWS // GLA_CHUNK
    pair_w = HEAD_PAIR * GLA_DV_PAD
    (gla, fq, fk, fv, fgate, mq, mg, crow, qn2, kn2) = pl.pallas_call(
        _proj_kernel,
        grid=(nproj,),
        in_specs=[rows(D_MODEL), whole((1, D_MODEL)),
                  pl.BlockSpec(w_in_t.shape, lambda i: (0, 0), pipeline_mode=pl.Buffered(1)),
                  whole(w_alpha_up.shape), whole((1, GLA_HEADS * GLA_DK)),
                  whole((1, FOX_HEADS)), whole((1, GLA_DV))],
        out_specs=[rows(GLA_OUT_W),
                   rows(FOX_W), pl.BlockSpec((FOX_W, PROJ_ROWS), lambda i: (0, i)),
                   rows(FOX_W), rows(FOX_W),
                   rows(MEM_W), rows(MEM_W),
                   pl.BlockSpec((SUBLANES, PROJ_ROWS), lambda i: (0, i)),
                   stat_spec, stat_spec],
        out_shape=[bshape(GLA_OUT_W),
                   bshape(FOX_W), jax.ShapeDtypeStruct((FOX_W, T), BF16),
                   bshape(FOX_W), bshape(FOX_W),
                   bshape(MEM_W), bshape(MEM_W),
                   jax.ShapeDtypeStruct((SUBLANES, T), F32),
                   stat_shape, stat_shape],
        scratch_shapes=[
            pltpu.VMEM((SUBLANES, LANES), F32),
            pltpu.VMEM((IN_COLS_PAD, D_MODEL), BF16),
            pltpu.VMEM((SMALL_W, GLA_QK_W), F32),
            pltpu.VMEM((1, GLA_QK_W), F32),
            pltpu.VMEM((1, SMALL_W), F32),
            pltpu.VMEM((1, GLA_DV_PAD), F32),
            pltpu.VMEM((FOX_W, LANES), BF16),
            pltpu.VMEM((PROJ_ROWS, GLA_QK_W), BF16),
            pltpu.VMEM((PROJ_ROWS, GLA_QK_W), BF16),
            pltpu.VMEM((PROJ_ROWS, GLA_V_W), BF16),
            pltpu.VMEM((PROJ_ROWS, GLA_V_W), BF16),
            pltpu.VMEM((PROJ_ROWS, GLA_QK_W), F32),
            pltpu.VMEM((gla_pairs, LANES, pair_w), F32),
            pltpu.VMEM((PROJ_ROWS, GLA_V_W), BF16),
            pltpu.VMEM((gla_pairs, gla_chunks, LANES, pair_w), F32),
            pltpu.VMEM((gla_pairs, gla_chunks, LANES, LANES), F32),
            pltpu.VMEM((gla_pairs, gla_chunks, LANES, pair_w), BF16)],
        compiler_params=_params("arbitrary"),
        name="proj",
    )(x, norm_g[None, :], w_in_t, w_alpha_up, b_alpha[None, :], b_forget[None, :],
      gla_norm_g[None, :])

    mk, mv = pl.pallas_call(
        _memkv_kernel,
        out_shape=[jax.ShapeDtypeStruct((M, MEM_W), BF16)] * 2,
        compiler_params=pltpu.CompilerParams(vmem_limit_bytes=VMEM_LIMIT),
        name="memkv",
    )(mem, mem_norm_g[None, :], w_mem_kv)

    cend = crow[:FOX_HEADS, FOX_KEYS - 1::FOX_KEYS].reshape(-1)
    pair_rows = pl.BlockSpec((FOX_BLOCK, LANES), lambda p, i, *_: (i, p))
    pair_all = pl.BlockSpec((T, LANES), lambda p, i, *_: (0, p))
    fox = pl.pallas_call(
        _fox_kernel,
        grid_spec=pltpu.PrefetchScalarGridSpec(
            num_scalar_prefetch=3,
            grid=(FOX_HEADS // HEAD_PAIR, T // FOX_BLOCK),
            in_specs=[pair_rows, pl.BlockSpec((LANES, T), lambda p, i, *_: (p, 0)), pair_all,
                      pl.BlockSpec((SUBLANES, T), lambda p, i, *_: (0, 0)), pair_rows],
            out_specs=pair_rows,
            scratch_shapes=[pltpu.VMEM((HEAD_PAIR, FOX_BLOCK, LANES), F32),
                            pltpu.VMEM((HEAD_PAIR, FOX_BLOCK, LANES), F32)]),
        out_shape=bshape(FOX_W),
        compiler_params=_params("arbitrary", "arbitrary"),
        name="fox",
    )(qn2.reshape(-1), kn2.reshape(-1), cend, fq, fk, fv, crow, fgate)

    out = pl.pallas_call(
        _out_kernel,
        grid=(T // OUT_ROWS,),
        in_specs=[rows(D_MODEL, OUT_ROWS), rows(GLA_OUT_W, OUT_ROWS), rows(FOX_W, OUT_ROWS),
                  rows(MEM_W, OUT_ROWS), rows(MEM_W, OUT_ROWS),
                  whole((M, MEM_W)), whole((M, MEM_W)),
                  pl.BlockSpec(w_out.shape, lambda i: (0, 0), pipeline_mode=pl.Buffered(1)),
                  whole((1, D_MODEL))],
        out_specs=rows(D_MODEL, OUT_ROWS),
        out_shape=jax.ShapeDtypeStruct((T, D_MODEL), F32),
        scratch_shapes=[pltpu.VMEM((D_MODEL, D_MODEL), BF16)],
        compiler_params=_params("arbitrary"),
        name="out",
    )(x, gla, fox, mq, mg, mk, mv, w_out, out_g[None, :])
    return out


def kernel(x, mem, norm_g, w_in, w_alpha_up, b_alpha, b_forget, gla_norm_g, mem_norm_g,
           w_mem_kv, w_out, final_norm_g):
    assert x.shape[0] == 1 and mem.shape[0] == 1 and norm_g.shape[0] == 1
    assert x.shape[1] % max(PROJ_ROWS, FOX_BLOCK, OUT_ROWS) == 0
    out = _layer(x[0], mem[0], norm_g[0], w_in[0], w_alpha_up[0], b_alpha[0], b_forget[0],
                 gla_norm_g[0], mem_norm_g[0], w_mem_kv[0], w_out[0], final_norm_g)
    return out[None]
```

```python
import functools

import jax
import jax.numpy as jnp
from jax import lax
from jax.experimental import pallas as pl
from jax.experimental.pallas import tpu as pltpu

F32 = jnp.float32
BF16 = jnp.bfloat16

EPS = 1e-6
LANES = 128
SUBLANES = 8

D_MODEL = 1024
GLA_HEADS, GLA_DK, GLA_DV, GLA_RANK = 4, 48, 96, 16
GLA_DK_PAD = 64
GLA_DV_PAD = LANES
GLA_GATE_NORM = 16.0
GLA_CHUNK = 64
FOX_HEADS, FOX_DH = 6, 64
MEM_HEADS, MEM_DH = 4, 64
HEAD_PAIR = 2
GLA_QK_W = GLA_HEADS * GLA_DK_PAD
GLA_V_W = GLA_HEADS * GLA_DV_PAD
GLA_OUT_W = GLA_HEADS * GLA_DV
FOX_W = FOX_HEADS * FOX_DH
MEM_W = MEM_HEADS * MEM_DH
SMALL_W = LANES
FG_LANE0 = 0
LR_LANE0 = SUBLANES

_GROUPS = (("gq", GLA_QK_W), ("gk", GLA_QK_W), ("gv", GLA_V_W), ("gg", GLA_V_W),
           ("fq", FOX_W), ("fk", FOX_W), ("fv", FOX_W), ("fgate", FOX_W),
           ("mq", MEM_W), ("mg", MEM_W), ("small", SMALL_W))
_OFF = {}
_o = 0
for _n, _w in _GROUPS:
    _OFF[_n] = (_o, _o + _w)
    _o += _w
IN_COLS_PAD = _o

PROJ_ROWS = 512
FOX_BLOCK = 2048
FOX_KEYS = 512
OUT_ROWS = 1024
VMEM_LIMIT = 56 * 1024 * 1024

NEG_BIG = -1e30
FOX_SKIP_NATS = 105.0
NORM_SLACK = 1.02


def _log_sigmoid(z):
    return jnp.minimum(z, 0.0) - jnp.log(1.0 + jnp.exp(-jnp.abs(z)))


def _silu(z):
    return z / (1.0 + jnp.exp(-z))


def _rms_scale(v, width):
    return lax.rsqrt(jnp.sum(v * v, axis=-1, keepdims=True) * (1.0 / width) + EPS)


def _w_in_segments():
    qk, gw = GLA_HEADS * GLA_DK, GLA_HEADS * GLA_DV
    src = {}
    o = 0
    for name, width in (("gq", qk), ("gk", qk), ("gv", gw), ("lr", GLA_RANK), ("gg", gw),
                        ("fq", FOX_W), ("fk", FOX_W), ("fv", FOX_W), ("fg", FOX_HEADS),
                        ("fgate", FOX_W), ("mq", MEM_W), ("mg", MEM_W)):
        src[name] = o
        o += width
    segs = []
    for name, d, d_pad in (("gq", GLA_DK, GLA_DK_PAD), ("gk", GLA_DK, GLA_DK_PAD),
                           ("gv", GLA_DV, GLA_DV_PAD), ("gg", GLA_DV, GLA_DV_PAD)):
        segs += [(src[name] + h * d, _OFF[name][0] + h * d_pad, d) for h in range(GLA_HEADS)]
    segs += [(src[name], _OFF[name][0], _OFF[name][1] - _OFF[name][0])
             for name in ("fq", "fk", "fv", "fgate", "mq", "mg")]
    segs += [(src["fg"], _OFF["small"][0] + FG_LANE0, FOX_HEADS),
             (src["lr"], _OFF["small"][0] + LR_LANE0, GLA_RANK)]
    return tuple(segs)


def _proj_kernel(x_ref, g_ref, w_in_ref, w_alpha_ref, b_alpha_ref, b_forget_ref, gla_g_ref,
                 gla_ref, fq_ref, fk_ref, fv_ref, fgate_ref,
                 mq_ref, mg_ref, crow_ref, qn2_ref, kn2_ref,
                 carry_ref, wt_ref, wa_ref, ba_ref, bf_ref, ng_ref, seg_ref,
                 gq_ref, gk_ref, gv_ref, gg_ref, loga_ref,
                 s_ref, lhs_ref, kv_ref, dec_ref, sprev_ref):
    rows = x_ref.shape[0]
    k_chunks = D_MODEL // LANES

    @pl.when(pl.program_id(0) == 0)
    def _():
        carry_ref[...] = jnp.zeros_like(carry_ref)
        s_ref[...] = jnp.zeros_like(s_ref)
        kn2_ref[...] = jnp.zeros_like(kn2_ref)
        wa_ref[...] = jnp.zeros_like(wa_ref)
        ba_ref[...] = jnp.zeros_like(ba_ref)
        bf_ref[...] = jnp.zeros_like(bf_ref)
        ng_ref[...] = jnp.zeros_like(ng_ref)
        for h in range(GLA_HEADS):
            src = slice(h * GLA_DK, (h + 1) * GLA_DK)
            dst = slice(h * GLA_DK_PAD, h * GLA_DK_PAD + GLA_DK)
            wa_ref[LR_LANE0:LR_LANE0 + GLA_RANK, dst] = w_alpha_ref[:, src]
            ba_ref[:, dst] = b_alpha_ref[:, src]
        bf_ref[:, FG_LANE0:FG_LANE0 + FOX_HEADS] = b_forget_ref[...]
        ng_ref[:, 0:GLA_DV] = gla_g_ref[...]
        seg_ref[...] = (lax.broadcasted_iota(jnp.int32, seg_ref.shape, 0) // FOX_DH
                        == lax.broadcasted_iota(jnp.int32, seg_ref.shape, 1)).astype(BF16)
        wt_ref[...] = jnp.zeros_like(wt_ref)
        for s0, d0, width in _w_in_segments():
            for c in range(k_chunks):
                wt_ref[d0:d0 + width, c * LANES:(c + 1) * LANES] = (
                    w_in_ref[pl.ds(s0 * k_chunks + c, width, stride=k_chunks), :].astype(BF16))

    x = x_ref[...]
    xn = (x * _rms_scale(x, D_MODEL) * g_ref[...]).astype(BF16)
    nt = (((1,), (1,)), ((), ()))

    def proj(first, last):
        lo, hi = _OFF[first][0], _OFF[last][1]
        y = lax.dot_general(xn, wt_ref[lo:hi, :], nt, preferred_element_type=F32)
        return lambda name: y[:, _OFF[name][0] - lo:_OFF[name][1] - lo]

    tail = proj("mq", "small")
    gla = proj("gq", "gg")
    small = tail("small")
    logf = _log_sigmoid(small + bf_ref[...])
    c = logf.T[0:SUBLANES, :]
    lane = lax.broadcasted_iota(jnp.int32, c.shape, 1)
    shift = 1
    while shift < rows:
        c = c + jnp.where(lane >= shift, pltpu.roll(c, shift, axis=1), 0.0)
        shift *= 2
    c = c + carry_ref[:, 0:1]
    crow_ref[...] = c
    carry_ref[...] = jnp.broadcast_to(c[:, rows - 1:rows], carry_ref.shape)

    z = (jnp.dot(small.astype(BF16), wa_ref[...].astype(BF16), preferred_element_type=F32)
         + ba_ref[...])
    loga_ref[...] = _log_sigmoid(z) * (1.0 / GLA_GATE_NORM)
    gq_ref[...] = gla("gq").astype(BF16)
    gk_ref[...] = gla("gk").astype(BF16)
    gv_ref[...] = gla("gv").astype(BF16)
    gg_ref[...] = _silu(gla("gg")).astype(BF16)
    gla_local, gla_scan, gla_output = _gla_block(
        gq_ref, gk_ref, gv_ref, loga_ref, gg_ref, ng_ref, gla_ref,
        s_ref, lhs_ref, kv_ref, dec_ref, sprev_ref)

    def max_sq_norm(v):
        v32 = v.astype(F32)
        n2 = jnp.dot((v32 * v32).astype(BF16), seg_ref[...], preferred_element_type=F32)
        return jnp.max(n2, axis=0, keepdims=True)

    gla_local()
    fox_qk = proj("fq", "fk")
    mq_ref[...] = (tail("mq") * MEM_DH ** -0.5).astype(BF16)
    mg_ref[...] = _silu(tail("mg")).astype(BF16)
    gla_scan()
    fq = (fox_qk("fq") * FOX_DH ** -0.5).astype(BF16)
    fk = fox_qk("fk").astype(BF16)
    fq_ref[...] = fq
    fk_ref[...] = fox_qk("fk").T.astype(BF16)
    gla_output()
    fox_vg = proj("fv", "fgate")
    fv_ref[...] = fox_vg("fv").astype(BF16)
    fgate_ref[...] = _silu(fox_vg("fgate")).astype(BF16)
    qn2_ref[0] = max_sq_norm(fq)
    kn2_ref[0] = jnp.maximum(kn2_ref[0], max_sq_norm(fk))


def _memkv_kernel(mem_ref, g_ref, w_ref, mk_ref, mv_ref):
    m = mem_ref[...]
    mn = (m * _rms_scale(m, D_MODEL) * g_ref[...]).astype(BF16)
    kv = jnp.dot(mn, w_ref[...].astype(BF16), preferred_element_type=F32)
    mk_ref[...] = kv[:, :MEM_W].astype(BF16)
    mv_ref[...] = kv[:, MEM_W:].astype(BF16)


def _gla_block(q_ref, k_ref, v_ref, loga_ref, gate_ref, ng_ref, o_ref,
               s_ref, lhs_ref, kv_ref, dec_ref, sprev_ref):
    C = GLA_CHUNK
    W = HEAD_PAIR * GLA_DV_PAD
    n_chunks = q_ref.shape[0] // C

    row = lax.broadcasted_iota(jnp.int32, (C, LANES), 0)
    lane = lax.broadcasted_iota(jnp.int32, (C, LANES), 1)
    lo_k = lane < GLA_DK_PAD
    causal = row >= jnp.where(lo_k, lane, lane - GLA_DK_PAD)
    lo_v = lax.broadcasted_iota(jnp.int32, (C, W), 1) < GLA_DV_PAD
    st_row = lax.broadcasted_iota(jnp.int32, (LANES, W), 0)
    st_lane = lax.broadcasted_iota(jnp.int32, (LANES, W), 1)
    own = (st_row < GLA_DK_PAD) == (st_lane < GLA_DV_PAD)
    eye = (lax.broadcasted_iota(jnp.int32, (LANES, LANES), 0)
           == lax.broadcasted_iota(jnp.int32, (LANES, LANES), 1))
    scale = GLA_DK ** -0.5
    nt = (((1,), (1,)), ((), ()))
    tn = (((0,), (0,)), ((), ()))
    ng = jnp.concatenate([ng_ref[...]] * HEAD_PAIR, axis=1)

    pairs = range(GLA_HEADS // HEAD_PAIR)

    def local(ci):
        rs = slice(ci * C, (ci + 1) * C)
        for p in pairs:
            ls = slice(p * LANES, (p + 1) * LANES)
            vs = slice(p * W, (p + 1) * W)
            b = loga_ref[rs, ls]
            shift = 1
            while shift < C:
                b = b + jnp.where(row >= shift, pltpu.roll(b, shift, axis=0), 0.0)
                shift *= 2
            b_last = b[C - 1:C, :]
            k2 = k_ref[rs, ls].astype(F32)
            qd = (q_ref[rs, ls].astype(F32) * scale * jnp.exp(b)).astype(BF16)
            kd = (k2 * jnp.exp(-b)).astype(BF16)
            ke = (k2 * jnp.exp(b_last - b)).astype(BF16)
            zk = jnp.zeros_like(kd)
            kd_blk = jnp.concatenate([jnp.where(lo_k, kd, zk), jnp.where(lo_k, zk, kd)], axis=0)
            attn = lax.dot_general(qd, kd_blk, nt, preferred_element_type=F32)
            lhs_ref[rs, vs] = jnp.concatenate([jnp.where(causal, attn, 0.0).astype(BF16), qd], axis=1)
            kv = lax.dot_general(ke, v_ref[rs, vs], tn, preferred_element_type=F32)
            kv_ref[p, ci] = jnp.where(own, kv, 0.0)
            dcol = jnp.exp(jnp.sum(jnp.where(eye, jnp.broadcast_to(b_last, (LANES, LANES)), 0.0),
                                   axis=1, keepdims=True))
            dec_ref[p, ci] = jnp.broadcast_to(dcol, (LANES, LANES))

    def scan(ci):
        for p in pairs:
            s_prev = s_ref[p]
            sprev_ref[p, ci] = s_prev.astype(BF16)
            s_ref[p] = jnp.tile(dec_ref[p, ci], (1, HEAD_PAIR)) * s_prev + kv_ref[p, ci]

    def output(ci):
        rs = slice(ci * C, (ci + 1) * C)
        for p in pairs:
            vs = slice(p * W, (p + 1) * W)
            v2 = v_ref[rs, vs]
            zv = jnp.zeros_like(v2)
            v_blk = jnp.concatenate([jnp.where(lo_v, v2, zv), jnp.where(lo_v, zv, v2)], axis=0)
            o = jnp.dot(lhs_ref[rs, vs], jnp.concatenate([v_blk, sprev_ref[p, ci]], axis=0),
                        preferred_element_type=F32)
            o2 = o * o
            ms = jnp.where(lo_v, jnp.sum(o2[:, :GLA_DV_PAD], axis=1, keepdims=True),
                           jnp.sum(o2[:, GLA_DV_PAD:], axis=1, keepdims=True))
            on = o * lax.rsqrt(ms * (1.0 / GLA_DV) + EPS) * ng
            og = (on * gate_ref[rs, vs].astype(F32)).astype(BF16)
            for hh in range(HEAD_PAIR):
                c0 = (p * HEAD_PAIR + hh) * GLA_DV
                o_ref[rs, c0:c0 + GLA_DV] = og[:, hh * GLA_DV_PAD:hh * GLA_DV_PAD + GLA_DV]

    def all_chunks(phase):
        return lambda: [phase(ci) for ci in range(n_chunks)]

    return all_chunks(local), all_chunks(scan), all_chunks(output)


def _fox_kernel(qn2_ref, kn2_ref, cend_ref, q_ref, k_ref, v_ref, crow_ref, gate_ref, o_ref,
                m_ref, acc_ref):
    blk = FOX_KEYS
    streams = range(q_ref.shape[0] // blk)
    pair = pl.program_id(0)
    qi = pl.program_id(1)
    nblk = pl.num_programs(1) * len(streams)
    lane = lax.broadcasted_iota(jnp.int32, (1, LANES), 1)
    lo_lanes = lane < FOX_DH
    reps = blk // LANES
    diag = [qi * len(streams) + s for s in streams]

    q = q_ref[...]
    zero = jnp.zeros_like(q)
    q_lo, q_hi = jnp.where(lo_lanes, q, zero), jnp.where(lo_lanes, zero, q)
    q_stack = [jnp.concatenate([q_lo[s * blk:(s + 1) * blk], q_hi[s * blk:(s + 1) * blk]], axis=0)
               for s in streams]
    m_ref[...] = jnp.full_like(m_ref, NEG_BIG)
    acc_ref[...] = jnp.zeros_like(acc_ref)

    def head_row(c8, h):
        sub = lax.broadcasted_iota(jnp.int32, c8.shape, 0)
        return jnp.sum(jnp.where(sub == h, c8, 0.0), axis=0, keepdims=True)

    c_q0 = [[head_row(crow_ref[:, pl.ds(pl.multiple_of(diag[s] * blk, blk), LANES)],
                      pair * HEAD_PAIR + hh)[:, 0:1] for hh in range(HEAD_PAIR)]
            for s in streams]

    def step(s, j, masked):
        penalty = jnp.where(j >= 0, 0.0, NEG_BIG)
        ks = pl.ds(pl.multiple_of(jnp.maximum(j, 0) * blk, blk), blk)
        rows = slice(s * blk, (s + 1) * blk)
        kb = k_ref[:, ks]
        vb = v_ref[ks, :]
        one = jnp.ones_like(vb)
        vaug = (jnp.where(lo_lanes, vb, one), jnp.where(lo_lanes, one, vb))
        s_all = jnp.dot(q_stack[s], kb, preferred_element_type=F32)
        for hh in range(HEAD_PAIR):
            h = pair * HEAD_PAIR + hh
            bias = (c_q0[s][hh] + penalty) - head_row(crow_ref[:, ks], h)
            sc = s_all[hh * blk:(hh + 1) * blk] + bias
            if masked:
                qpos = lax.broadcasted_iota(jnp.int32, (blk, blk), 0)
                kpos = lax.broadcasted_iota(jnp.int32, (blk, blk), 1)
                sc = jnp.where(kpos <= qpos, sc, NEG_BIG)
            m_prev = m_ref[hh, rows]
            m_new = jnp.maximum(m_prev, jnp.max(sc, axis=1, keepdims=True))
            p = jnp.exp(sc - jnp.tile(m_new, (1, reps)))
            alpha = jnp.exp(m_prev - m_new)
            pv = jnp.dot(p.astype(BF16), vaug[hh], preferred_element_type=F32)
            acc_ref[hh, rows] = alpha * acc_ref[hh, rows] + pv
            m_ref[hh, rows] = m_new

    for s in streams:
        step(s, diag[s], masked=True)
    m_min = [[jnp.min(m_ref[hh, s * blk:(s + 1) * blk]) for hh in range(HEAD_PAIR)]
             for s in streams]
    for s in streams:
        step(s, diag[s] - 1, masked=False)

    k_max2 = [kn2_ref[pair * HEAD_PAIR + hh] for hh in range(HEAD_PAIR)]

    def live(t):
        keep = False
        for s in streams:
            j = diag[s] - 1 - t
            jj = jnp.maximum(j, 0)
            prev = jnp.maximum(diag[s] - 1, 0)
            stat = (diag[s] * blk // PROJ_ROWS) * LANES
            for hh in range(HEAD_PAIR):
                h = pair * HEAD_PAIR + hh
                gap = (FOX_SKIP_NATS + cend_ref[h * nblk + prev] - cend_ref[h * nblk + jj]) - m_min[s][hh]
                norm2 = (NORM_SLACK * NORM_SLACK) * qn2_ref[stat + h] * k_max2[hh]
                dead = jnp.logical_and(gap <= 0.0, norm2 <= gap * gap)
                keep = jnp.logical_or(keep, jnp.logical_and(j >= 0, jnp.logical_not(dead)))
        return keep

    def body(t):
        for s in streams:
            step(s, diag[s] - 1 - t, masked=False)
        return t + 1

    lax.while_loop(live, body, 1)

    outs = []
    for hh in range(HEAD_PAIR):
        acc = acc_ref[hh]
        outs.append(acc / pltpu.roll(acc, FOX_DH, axis=1))
    o = jnp.where(lo_lanes, outs[0], outs[1])
    o_ref[...] = (o * gate_ref[...].astype(F32)).astype(BF16)


def _out_kernel(x_ref, gla_ref, fox_ref, mq_ref, mg_ref, mk_ref, mv_ref,
                w_out_ref, fg_ref, o_ref, wo_ref):
    @pl.when(pl.program_id(0) == 0)
    def _():
        wo_ref[...] = w_out_ref[...].astype(BF16)

    lane = lax.broadcasted_iota(jnp.int32, (1, LANES), 1)
    lo_lanes = lane < MEM_DH
    nt = (((1,), (1,)), ((), ()))
    mem_parts = []
    for p in range(MEM_HEADS // HEAD_PAIR):
        ls = slice(p * LANES, (p + 1) * LANES)
        q = mq_ref[:, ls]
        kb = mk_ref[:, ls]
        vb = mv_ref[:, ls]
        zero = jnp.zeros_like(q)
        one = jnp.ones_like(vb)
        qh = (jnp.where(lo_lanes, q, zero), jnp.where(lo_lanes, zero, q))
        vaug = (jnp.where(lo_lanes, vb, one), jnp.where(lo_lanes, one, vb))
        outs = []
        for hh in range(HEAD_PAIR):
            s = lax.dot_general(qh[hh], kb, nt, preferred_element_type=F32)
            pexp = jnp.exp(s - jnp.max(s, axis=1, keepdims=True))
            pv = jnp.dot(pexp.astype(BF16), vaug[hh], preferred_element_type=F32)
            outs.append(pv / pltpu.roll(pv, MEM_DH, axis=1))
        o = jnp.where(lo_lanes, outs[0], outs[1])
        mem_parts.append((o * mg_ref[:, ls].astype(F32)).astype(BF16))
    mixed = jnp.concatenate([gla_ref[...], fox_ref[...]] + mem_parts, axis=1)
    y = x_ref[...] + jnp.dot(mixed, wo_ref[...], preferred_element_type=F32)
    o_ref[...] = y * _rms_scale(y, D_MODEL) * fg_ref[...]


def _params(*sem):
    return pltpu.CompilerParams(dimension_semantics=sem, vmem_limit_bytes=VMEM_LIMIT)


def _layer(x, mem, norm_g, w_in, w_alpha_up, b_alpha, b_forget, gla_norm_g,
           mem_norm_g, w_mem_kv, w_out, out_g):
    w_in_t = jnp.transpose(w_in[None], (0, 2, 1)).reshape(-1, LANES)
    T = x.shape[0]
    M = mem.shape[0]

    def rows(width, n=PROJ_ROWS):
        return pl.BlockSpec((n, width), lambda i: (i, 0))

    def whole(shape):
        return pl.BlockSpec(shape, lambda i: (0,) * len(shape))

    bshape = lambda w: jax.ShapeDtypeStruct((T, w), BF16)
    nproj = T // PROJ_ROWS
    stat_spec = pl.BlockSpec((1, 1, LANES), lambda i: (i, 0, 0))
    stat_shape = jax.ShapeDtypeStruct((nproj, 1, LANES), F32)
    gla_pairs, gla_chunks = GLA_HEADS // HEAD_PAIR, PROJ_ROWS // GLA_CHUNK
    pair_w = HEAD_PAIR * GLA_DV_PAD
    (gla, fq, fk, fv, fgate, mq, mg, crow, qn2, kn2) = pl.pallas_call(
        _proj_kernel,
        grid=(nproj,),
        in_specs=[rows(D_MODEL), whole((1, D_MODEL)),
                  pl.BlockSpec(w_in_t.shape, lambda i: (0, 0), pipeline_mode=pl.Buffered(1)),
                  whole(w_alpha_up.shape), whole((1, GLA_HEADS * GLA_DK)),
                  whole((1, FOX_HEADS)), whole((1, GLA_DV))],
        out_specs=[rows(GLA_OUT_W),
                   rows(FOX_W), pl.BlockSpec((FOX_W, PROJ_ROWS), lambda i: (0, i)),
                   rows(FOX_W), rows(FOX_W),
                   rows(MEM_W), rows(MEM_W),
                   pl.BlockSpec((SUBLANES, PROJ_ROWS), lambda i: (0, i)),
                   stat_spec, pl.BlockSpec((1, 1, LANES), lambda i: (0, 0, 0))],
        out_shape=[bshape(GLA_OUT_W),
                   bshape(FOX_W), jax.ShapeDtypeStruct((FOX_W, T), BF16),
                   bshape(FOX_W), bshape(FOX_W),
                   bshape(MEM_W), bshape(MEM_W),
                   jax.ShapeDtypeStruct((SUBLANES, T), F32),
                   stat_shape, jax.ShapeDtypeStruct((1, 1, LANES), F32)],
        scratch_shapes=[
            pltpu.VMEM((SUBLANES, LANES), F32),
            pltpu.VMEM((IN_COLS_PAD, D_MODEL), BF16),
            pltpu.VMEM((SMALL_W, GLA_QK_W), F32),
            pltpu.VMEM((1, GLA_QK_W), F32),
            pltpu.VMEM((1, SMALL_W), F32),
            pltpu.VMEM((1, GLA_DV_PAD), F32),
            pltpu.VMEM((FOX_W, LANES), BF16),
            pltpu.VMEM((PROJ_ROWS, GLA_QK_W), BF16),
            pltpu.VMEM((PROJ_ROWS, GLA_QK_W), BF16),
            pltpu.VMEM((PROJ_ROWS, GLA_V_W), BF16),
            pltpu.VMEM((PROJ_ROWS, GLA_V_W), BF16),
            pltpu.VMEM((PROJ_ROWS, GLA_QK_W), F32),
            pltpu.VMEM((gla_pairs, LANES, pair_w), F32),
            pltpu.VMEM((PROJ_ROWS, GLA_V_W), BF16),
            pltpu.VMEM((gla_pairs, gla_chunks, LANES, pair_w), F32),
            pltpu.VMEM((gla_pairs, gla_chunks, LANES, LANES), F32),
            pltpu.VMEM((gla_pairs, gla_chunks, LANES, pair_w), BF16)],
        compiler_params=_params("arbitrary"),
        name="proj",
    )(x, norm_g[None, :], w_in_t, w_alpha_up, b_alpha[None, :], b_forget[None, :],
      gla_norm_g[None, :])

    mk, mv = pl.pallas_call(
        _memkv_kernel,
        out_shape=[jax.ShapeDtypeStruct((M, MEM_W), BF16)] * 2,
        compiler_params=pltpu.CompilerParams(vmem_limit_bytes=VMEM_LIMIT),
        name="memkv",
    )(mem, mem_norm_g[None, :], w_mem_kv)

    cend = crow[:FOX_HEADS, FOX_KEYS - 1::FOX_KEYS].reshape(-1)
    pair_rows = pl.BlockSpec((FOX_BLOCK, LANES), lambda p, i, *_: (i, p))
    pair_all = pl.BlockSpec((T, LANES), lambda p, i, *_: (0, p))
    fox = pl.pallas_call(
        _fox_kernel,
        grid_spec=pltpu.PrefetchScalarGridSpec(
            num_scalar_prefetch=3,
            grid=(FOX_HEADS // HEAD_PAIR, T // FOX_BLOCK),
            in_specs=[pair_rows, pl.BlockSpec((LANES, T), lambda p, i, *_: (p, 0)), pair_all,
                      pl.BlockSpec((SUBLANES, T), lambda p, i, *_: (0, 0)), pair_rows],
            out_specs=pair_rows,
            scratch_shapes=[pltpu.VMEM((HEAD_PAIR, FOX_BLOCK, LANES), F32),
                            pltpu.VMEM((HEAD_PAIR, FOX_BLOCK, LANES), F32)]),
        out_shape=bshape(FOX_W),
        compiler_params=_params("arbitrary", "arbitrary"),
        name="fox",
    )(qn2.reshape(-1), kn2.reshape(-1), cend, fq, fk, fv, crow, fgate)

    out = pl.pallas_call(
        _out_kernel,
        grid=(T // OUT_ROWS,),
        in_specs=[rows(D_MODEL, OUT_ROWS), rows(GLA_OUT_W, OUT_ROWS), rows(FOX_W, OUT_ROWS),
                  rows(MEM_W, OUT_ROWS), rows(MEM_W, OUT_ROWS),
                  whole((M, MEM_W)), whole((M, MEM_W)),
                  pl.BlockSpec(w_out.shape, lambda i: (0, 0), pipeline_mode=pl.Buffered(1)),
                  whole((1, D_MODEL))],
        out_specs=rows(D_MODEL, OUT_ROWS),
        out_shape=jax.ShapeDtypeStruct((T, D_MODEL), F32),
        scratch_shapes=[pltpu.VMEM((D_MODEL, D_MODEL), BF16)],
        compiler_params=_params("arbitrary"),
        name="out",
    )(x, gla, fox, mq, mg, mk, mv, w_out, out_g[None, :])
    return out


def kernel(x, mem, norm_g, w_in, w_alpha_up, b_alpha, b_forget, gla_norm_g, mem_norm_g,
           w_mem_kv, w_out, final_norm_g):
    assert x.shape[0] == 1 and mem.shape[0] == 1 and norm_g.shape[0] == 1
    assert x.shape[1] % max(PROJ_ROWS, FOX_BLOCK, OUT_ROWS) == 0
    out = _layer(x[0], mem[0], norm_g[0], w_in[0], w_alpha_up[0], b_alpha[0], b_forget[0],
                 gla_norm_g[0], mem_norm_g[0], w_mem_kv[0], w_out[0], final_norm_g)
    return out[None]
```

```python
import functools

import jax
import jax.numpy as jnp
from jax import lax
from jax.experimental import pallas as pl
from jax.experimental.pallas import tpu as pltpu

F32 = jnp.float32
BF16 = jnp.bfloat16

EPS = 1e-6
LANES = 128
SUBLANES = 8

D_MODEL = 1024
GLA_HEADS, GLA_DK, GLA_DV, GLA_RANK = 4, 48, 96, 16
GLA_DK_PAD = 64
GLA_DV_PAD = LANES
GLA_GATE_NORM = 16.0
GLA_CHUNK = 64
FOX_HEADS, FOX_DH = 6, 64
MEM_HEADS, MEM_DH = 4, 64
HEAD_PAIR = 2
GLA_QK_W = GLA_HEADS * GLA_DK_PAD
GLA_V_W = GLA_HEADS * GLA_DV_PAD
GLA_OUT_W = GLA_HEADS * GLA_DV
FOX_W = FOX_HEADS * FOX_DH
MEM_W = MEM_HEADS * MEM_DH
SMALL_W = LANES
FG_LANE0 = 0
LR_LANE0 = SUBLANES

_GROUPS = (("gq", GLA_QK_W), ("gk", GLA_QK_W), ("gv", GLA_V_W), ("gg", GLA_V_W),
           ("fq", FOX_W), ("fk", FOX_W), ("fv", FOX_W), ("fgate", FOX_W),
           ("mq", MEM_W), ("mg", MEM_W), ("small", SMALL_W))
_OFF = {}
_o = 0
for _n, _w in _GROUPS:
    _OFF[_n] = (_o, _o + _w)
    _o += _w
IN_COLS_PAD = _o

PROJ_ROWS = 1024
FOX_BLOCK = 2048
FOX_KEYS = 512
OUT_ROWS = 1024
VMEM_LIMIT = 56 * 1024 * 1024

NEG_BIG = -1e30
FOX_SKIP_NATS = 105.0
NORM_SLACK = 1.02


def _log_sigmoid(z):
    return jnp.minimum(z, 0.0) - jnp.log(1.0 + jnp.exp(-jnp.abs(z)))


def _silu(z):
    return z / (1.0 + jnp.exp(-z))


def _rms_scale(v, width):
    return lax.rsqrt(jnp.sum(v * v, axis=-1, keepdims=True) * (1.0 / width) + EPS)


def _w_in_segments():
    qk, gw = GLA_HEADS * GLA_DK, GLA_HEADS * GLA_DV
    src = {}
    o = 0
    for name, width in (("gq", qk), ("gk", qk), ("gv", gw), ("lr", GLA_RANK), ("gg", gw),
                        ("fq", FOX_W), ("fk", FOX_W), ("fv", FOX_W), ("fg", FOX_HEADS),
                        ("fgate", FOX_W), ("mq", MEM_W), ("mg", MEM_W)):
        src[name] = o
        o += width
    segs = []
    for name, d, d_pad in (("gq", GLA_DK, GLA_DK_PAD), ("gk", GLA_DK, GLA_DK_PAD),
                           ("gv", GLA_DV, GLA_DV_PAD), ("gg", GLA_DV, GLA_DV_PAD)):
        segs += [(src[name] + h * d, _OFF[name][0] + h * d_pad, d) for h in range(GLA_HEADS)]
    segs += [(src[name], _OFF[name][0], _OFF[name][1] - _OFF[name][0])
             for name in ("fq", "fk", "fv", "fgate", "mq", "mg")]
    segs += [(src["fg"], _OFF["small"][0] + FG_LANE0, FOX_HEADS),
             (src["lr"], _OFF["small"][0] + LR_LANE0, GLA_RANK)]
    return tuple(segs)


def _proj_kernel(x_ref, g_ref, w_in_ref, w_alpha_ref, b_alpha_ref, b_forget_ref, gla_g_ref,
                 gla_ref, fq_ref, fk_ref, fv_ref, fgate_ref,
                 mq_ref, mg_ref, crow_ref, qn2_ref, kn2_ref,
                 carry_ref, wt_ref, wa_ref, ba_ref, bf_ref, ng_ref, seg_ref,
                 gq_ref, gk_ref, gv_ref, gg_ref, loga_ref,
                 s_ref, lhs_ref, kv_ref, dec_ref, sprev_ref):
    rows = x_ref.shape[0]
    k_chunks = D_MODEL // LANES

    @pl.when(pl.program_id(0) == 0)
    def _():
        carry_ref[...] = jnp.zeros_like(carry_ref)
        s_ref[...] = jnp.zeros_like(s_ref)
        kn2_ref[...] = jnp.zeros_like(kn2_ref)
        wa_ref[...] = jnp.zeros_like(wa_ref)
        ba_ref[...] = jnp.zeros_like(ba_ref)
        bf_ref[...] = jnp.zeros_like(bf_ref)
        ng_ref[...] = jnp.zeros_like(ng_ref)
        for h in range(GLA_HEADS):
            src = slice(h * GLA_DK, (h + 1) * GLA_DK)
            dst = slice(h * GLA_DK_PAD, h * GLA_DK_PAD + GLA_DK)
            wa_ref[LR_LANE0:LR_LANE0 + GLA_RANK, dst] = w_alpha_ref[:, src]
            ba_ref[:, dst] = b_alpha_ref[:, src]
        bf_ref[:, FG_LANE0:FG_LANE0 + FOX_HEADS] = b_forget_ref[...]
        ng_ref[:, 0:GLA_DV] = gla_g_ref[...]
        seg_ref[...] = (lax.broadcasted_iota(jnp.int32, seg_ref.shape, 0) // FOX_DH
                        == lax.broadcasted_iota(jnp.int32, seg_ref.shape, 1)).astype(BF16)
        wt_ref[...] = jnp.zeros_like(wt_ref)
        for s0, d0, width in _w_in_segments():
            for c in range(k_chunks):
                wt_ref[d0:d0 + width, c * LANES:(c + 1) * LANES] = (
                    w_in_ref[pl.ds(s0 * k_chunks + c, width, stride=k_chunks), :].astype(BF16))

    x = x_ref[...]
    xn = (x * _rms_scale(x, D_MODEL) * g_ref[...]).astype(BF16)
    nt = (((1,), (1,)), ((), ()))

    def proj(first, last):
        lo, hi = _OFF[first][0], _OFF[last][1]
        y = lax.dot_general(xn, wt_ref[lo:hi, :], nt, preferred_element_type=F32)
        return lambda name: y[:, _OFF[name][0] - lo:_OFF[name][1] - lo]

    tail = proj("mq", "small")
    gla = proj("gq", "gg")
    small = tail("small")
    logf = _log_sigmoid(small + bf_ref[...])
    c = logf.T[0:SUBLANES, :]
    lane = lax.broadcasted_iota(jnp.int32, c.shape, 1)
    shift = 1
    while shift < rows:
        c = c + jnp.where(lane >= shift, pltpu.roll(c, shift, axis=1), 0.0)
        shift *= 2
    c = c + carry_ref[:, 0:1]
    crow_ref[...] = c
    carry_ref[...] = jnp.broadcast_to(c[:, rows - 1:rows], carry_ref.shape)

    z = (jnp.dot(small.astype(BF16), wa_ref[...].astype(BF16), preferred_element_type=F32)
         + ba_ref[...])
    loga_ref[...] = _log_sigmoid(z) * (1.0 / GLA_GATE_NORM)
    gq_ref[...] = gla("gq").astype(BF16)
    gk_ref[...] = gla("gk").astype(BF16)
    gv_ref[...] = gla("gv").astype(BF16)
    gg_ref[...] = _silu(gla("gg")).astype(BF16)
    gla_local, gla_scan, gla_output = _gla_block(
        gq_ref, gk_ref, gv_ref, loga_ref, gg_ref, ng_ref, gla_ref,
        s_ref, lhs_ref, kv_ref, dec_ref, sprev_ref)

    def max_sq_norm(v):
        v32 = v.astype(F32)
        n2 = jnp.dot((v32 * v32).astype(BF16), seg_ref[...], preferred_element_type=F32)
        return jnp.max(n2, axis=0, keepdims=True)

    gla_local()
    fox_qk = proj("fq", "fk")
    mq_ref[...] = (tail("mq") * MEM_DH ** -0.5).astype(BF16)
    mg_ref[...] = _silu(tail("mg")).astype(BF16)
    gla_scan()
    fq = (fox_qk("fq") * FOX_DH ** -0.5).astype(BF16)
    fk = fox_qk("fk").astype(BF16)
    fq_ref[...] = fq
    fk_ref[...] = fox_qk("fk").T.astype(BF16)
    gla_output()
    fox_vg = proj("fv", "fgate")
    fv_ref[...] = fox_vg("fv").astype(BF16)
    fgate_ref[...] = _silu(fox_vg("fgate")).astype(BF16)
    qn2_ref[0] = max_sq_norm(fq)
    kn2_ref[0] = jnp.maximum(kn2_ref[0], max_sq_norm(fk))


def _memkv_kernel(mem_ref, g_ref, w_ref, mk_ref, mv_ref):
    m = mem_ref[...]
    mn = (m * _rms_scale(m, D_MODEL) * g_ref[...]).astype(BF16)
    kv = jnp.dot(mn, w_ref[...].astype(BF16), preferred_element_type=F32)
    mk_ref[...] = kv[:, :MEM_W].astype(BF16)
    mv_ref[...] = kv[:, MEM_W:].astype(BF16)


def _gla_block(q_ref, k_ref, v_ref, loga_ref, gate_ref, ng_ref, o_ref,
               s_ref, lhs_ref, kv_ref, dec_ref, sprev_ref):
    C = GLA_CHUNK
    W = HEAD_PAIR * GLA_DV_PAD
    n_chunks = q_ref.shape[0] // C

    row = lax.broadcasted_iota(jnp.int32, (C, LANES), 0)
    lane = lax.broadcasted_iota(jnp.int32, (C, LANES), 1)
    lo_k = lane < GLA_DK_PAD
    causal = row >= jnp.where(lo_k, lane, lane - GLA_DK_PAD)
    lo_v = lax.broadcasted_iota(jnp.int32, (C, W), 1) < GLA_DV_PAD
    st_row = lax.broadcasted_iota(jnp.int32, (LANES, W), 0)
    st_lane = lax.broadcasted_iota(jnp.int32, (LANES, W), 1)
    own = (st_row < GLA_DK_PAD) == (st_lane < GLA_DV_PAD)
    eye = (lax.broadcasted_iota(jnp.int32, (LANES, LANES), 0)
           == lax.broadcasted_iota(jnp.int32, (LANES, LANES), 1))
    scale = GLA_DK ** -0.5
    nt = (((1,), (1,)), ((), ()))
    tn = (((0,), (0,)), ((), ()))
    ng = jnp.concatenate([ng_ref[...]] * HEAD_PAIR, axis=1)

    pairs = range(GLA_HEADS // HEAD_PAIR)

    def local(ci):
        rs = slice(ci * C, (ci + 1) * C)
        for p in pairs:
            ls = slice(p * LANES, (p + 1) * LANES)
            vs = slice(p * W, (p + 1) * W)
            b = loga_ref[rs, ls]
            shift = 1
            while shift < C:
                b = b + jnp.where(row >= shift, pltpu.roll(b, shift, axis=0), 0.0)
                shift *= 2
            b_last = b[C - 1:C, :]
            k2 = k_ref[rs, ls].astype(F32)
            qd = (q_ref[rs, ls].astype(F32) * scale * jnp.exp(b)).astype(BF16)
            kd = (k2 * jnp.exp(-b)).astype(BF16)
            ke = (k2 * jnp.exp(b_last - b)).astype(BF16)
            zk = jnp.zeros_like(kd)
            kd_blk = jnp.concatenate([jnp.where(lo_k, kd, zk), jnp.where(lo_k, zk, kd)], axis=0)
            attn = lax.dot_general(qd, kd_blk, nt, preferred_element_type=F32)
            lhs_ref[rs, vs] = jnp.concatenate([jnp.where(causal, attn, 0.0).astype(BF16), qd], axis=1)
            kv = lax.dot_general(ke, v_ref[rs, vs], tn, preferred_element_type=F32)
            kv_ref[p, ci] = jnp.where(own, kv, 0.0)
            dcol = jnp.exp(jnp.sum(jnp.where(eye, jnp.broadcast_to(b_last, (LANES, LANES)), 0.0),
                                   axis=1, keepdims=True))
            dec_ref[p, ci] = jnp.broadcast_to(dcol, (LANES, LANES))

    def scan(ci):
        for p in pairs:
            s_prev = s_ref[p]
            sprev_ref[p, ci] = s_prev.astype(BF16)
            s_ref[p] = jnp.tile(dec_ref[p, ci], (1, HEAD_PAIR)) * s_prev + kv_ref[p, ci]

    def output(ci):
        rs = slice(ci * C, (ci + 1) * C)
        for p in pairs:
            vs = slice(p * W, (p + 1) * W)
            v2 = v_ref[rs, vs]
            zv = jnp.zeros_like(v2)
            v_blk = jnp.concatenate([jnp.where(lo_v, v2, zv), jnp.where(lo_v, zv, v2)], axis=0)
            o = jnp.dot(lhs_ref[rs, vs], jnp.concatenate([v_blk, sprev_ref[p, ci]], axis=0),
                        preferred_element_type=F32)
            o2 = o * o
            ms = jnp.where(lo_v, jnp.sum(o2[:, :GLA_DV_PAD], axis=1, keepdims=True),
                           jnp.sum(o2[:, GLA_DV_PAD:], axis=1, keepdims=True))
            on = o * lax.rsqrt(ms * (1.0 / GLA_DV) + EPS) * ng
            og = (on * gate_ref[rs, vs].astype(F32)).astype(BF16)
            for hh in range(HEAD_PAIR):
                c0 = (p * HEAD_PAIR + hh) * GLA_DV
                o_ref[rs, c0:c0 + GLA_DV] = og[:, hh * GLA_DV_PAD:hh * GLA_DV_PAD + GLA_DV]

    def all_chunks(phase):
        return lambda: [phase(ci) for ci in range(n_chunks)]

    return all_chunks(local), all_chunks(scan), all_chunks(output)


def _fox_kernel(qn2_ref, kn2_ref, cend_ref, q_ref, k_ref, v_ref, crow_ref, gate_ref, o_ref,
                m_ref, acc_ref):
    blk = FOX_KEYS
    streams = range(q_ref.shape[0] // blk)
    pair = pl.program_id(0)
    qi = pl.program_id(1)
    nblk = pl.num_programs(1) * len(streams)
    lane = lax.broadcasted_iota(jnp.int32, (1, LANES), 1)
    lo_lanes = lane < FOX_DH
    reps = blk // LANES
    diag = [qi * len(streams) + s for s in streams]

    q = q_ref[...]
    zero = jnp.zeros_like(q)
    q_lo, q_hi = jnp.where(lo_lanes, q, zero), jnp.where(lo_lanes, zero, q)
    q_stack = [jnp.concatenate([q_lo[s * blk:(s + 1) * blk], q_hi[s * blk:(s + 1) * blk]], axis=0)
               for s in streams]
    m_ref[...] = jnp.full_like(m_ref, NEG_BIG)
    acc_ref[...] = jnp.zeros_like(acc_ref)

    def head_row(c8, h):
        sub = lax.broadcasted_iota(jnp.int32, c8.shape, 0)
        return jnp.sum(jnp.where(sub == h, c8, 0.0), axis=0, keepdims=True)

    c_q0 = [[head_row(crow_ref[:, pl.ds(pl.multiple_of(diag[s] * blk, blk), LANES)],
                      pair * HEAD_PAIR + hh)[:, 0:1] for hh in range(HEAD_PAIR)]
            for s in streams]

    def step(s, j, masked):
        penalty = jnp.where(j >= 0, 0.0, NEG_BIG)
        ks = pl.ds(pl.multiple_of(jnp.maximum(j, 0) * blk, blk), blk)
        rows = slice(s * blk, (s + 1) * blk)
        kb = k_ref[:, ks]
        vb = v_ref[ks, :]
        one = jnp.ones_like(vb)
        vaug = (jnp.where(lo_lanes, vb, one), jnp.where(lo_lanes, one, vb))
        s_all = jnp.dot(q_stack[s], kb, preferred_element_type=F32)
        for hh in range(HEAD_PAIR):
            h = pair * HEAD_PAIR + hh
            bias = (c_q0[s][hh] + penalty) - head_row(crow_ref[:, ks], h)
            sc = s_all[hh * blk:(hh + 1) * blk] + bias
            if masked:
                qpos = lax.broadcasted_iota(jnp.int32, (blk, blk), 0)
                kpos = lax.broadcasted_iota(jnp.int32, (blk, blk), 1)
                sc = jnp.where(kpos <= qpos, sc, NEG_BIG)
            m_prev = m_ref[hh, rows]
            m_new = jnp.maximum(m_prev, jnp.max(sc, axis=1, keepdims=True))
            p = jnp.exp(sc - jnp.tile(m_new, (1, reps)))
            alpha = jnp.exp(m_prev - m_new)
            pv = jnp.dot(p.astype(BF16), vaug[hh], preferred_element_type=F32)
            acc_ref[hh, rows] = alpha * acc_ref[hh, rows] + pv
            m_ref[hh, rows] = m_new

    for s in streams:
        step(s, diag[s], masked=True)
    m_min = [[jnp.min(m_ref[hh, s * blk:(s + 1) * blk]) for hh in range(HEAD_PAIR)]
             for s in streams]
    for s in streams:
        step(s, diag[s] - 1, masked=False)

    k_max2 = [kn2_ref[pair * HEAD_PAIR + hh] for hh in range(HEAD_PAIR)]

    def live(t):
        keep = False
        for s in streams:
            j = diag[s] - 1 - t
            jj = jnp.maximum(j, 0)
            prev = jnp.maximum(diag[s] - 1, 0)
            stat = (diag[s] * blk // PROJ_ROWS) * LANES
            for hh in range(HEAD_PAIR):
                h = pair * HEAD_PAIR + hh
                gap = (FOX_SKIP_NATS + cend_ref[h * nblk + prev] - cend_ref[h * nblk + jj]) - m_min[s][hh]
                norm2 = (NORM_SLACK * NORM_SLACK) * qn2_ref[stat + h] * k_max2[hh]
                dead = jnp.logical_and(gap <= 0.0, norm2 <= gap * gap)
                keep = jnp.logical_or(keep, jnp.logical_and(j >= 0, jnp.logical_not(dead)))
        return keep

    def body(t):
        for s in streams:
            step(s, diag[s] - 1 - t, masked=False)
        return t + 1

    lax.while_loop(live, body, 1)

    outs = []
    for hh in range(HEAD_PAIR):
        acc = acc_ref[hh]
        outs.append(acc / pltpu.roll(acc, FOX_DH, axis=1))
    o = jnp.where(lo_lanes, outs[0], outs[1])
    o_ref[...] = (o * gate_ref[...].astype(F32)).astype(BF16)


def _out_kernel(x_ref, gla_ref, fox_ref, mq_ref, mg_ref, mk_ref, mv_ref,
                w_out_ref, fg_ref, o_ref, wo_ref):
    @pl.when(pl.program_id(0) == 0)
    def _():
        wo_ref[...] = w_out_ref[...].astype(BF16)

    lane = lax.broadcasted_iota(jnp.int32, (1, LANES), 1)
    lo_lanes = lane < MEM_DH
    nt = (((1,), (1,)), ((), ()))
    mem_parts = []
    for p in range(MEM_HEADS // HEAD_PAIR):
        ls = slice(p * LANES, (p + 1) * LANES)
        q = mq_ref[:, ls]
        kb = mk_ref[:, ls]
        vb = mv_ref[:, ls]
        zero = jnp.zeros_like(q)
        one = jnp.ones_like(vb)
        qh = (jnp.where(lo_lanes, q, zero), jnp.where(lo_lanes, zero, q))
        vaug = (jnp.where(lo_lanes, vb, one), jnp.where(lo_lanes, one, vb))
        outs = []
        for hh in range(HEAD_PAIR):
            s = lax.dot_general(qh[hh], kb, nt, preferred_element_type=F32)
            pexp = jnp.exp(s - jnp.max(s, axis=1, keepdims=True))
            pv = jnp.dot(pexp.astype(BF16), vaug[hh], preferred_element_type=F32)
            outs.append(pv / pltpu.roll(pv, MEM_DH, axis=1))
        o = jnp.where(lo_lanes, outs[0], outs[1])
        mem_parts.append((o * mg_ref[:, ls].astype(F32)).astype(BF16))
    mixed = jnp.concatenate([gla_ref[...], fox_ref[...]] + mem_parts, axis=1)
    y = x_ref[...] + jnp.dot(mixed, wo_ref[...], preferred_element_type=F32)
    o_ref[...] = y * _rms_scale(y, D_MODEL) * fg_ref[...]


def _params(*sem):
    return pltpu.CompilerParams(dimension_semantics=sem, vmem_limit_bytes=VMEM_LIMIT)


def _layer(x, mem, norm_g, w_in, w_alpha_up, b_alpha, b_forget, gla_norm_g,
           mem_norm_g, w_mem_kv, w_out, out_g):
    w_in_t = jnp.transpose(w_in[None], (0, 2, 1)).reshape(-1, LANES)
    T = x.shape[0]
    M = mem.shape[0]

    def rows(width, n=PROJ_ROWS):
        return pl.BlockSpec((n, width), lambda i: (i, 0))

    def whole(shape):
        return pl.BlockSpec(shape, lambda i: (0,) * len(shape))

    bshape = lambda w: jax.ShapeDtypeStruct((T, w), BF16)
    nproj = T // PROJ_ROWS
    stat_spec = pl.BlockSpec((1, 1, LANES), lambda i: (i, 0, 0))
    stat_shape = jax.ShapeDtypeStruct((nproj, 1, LANES), F32)
    gla_pairs, gla_chunks = GLA_HEADS // HEAD_PAIR, PROJ_ROWS // GLA_CHUNK
    pair_w = HEAD_PAIR * GLA_DV_PAD
    (gla, fq, fk, fv, fgate, mq, mg, crow, qn2, kn2) = pl.pallas_call(
        _proj_kernel,
        grid=(nproj,),
        in_specs=[rows(D_MODEL), whole((1, D_MODEL)),
                  pl.BlockSpec(w_in_t.shape, lambda i: (0, 0), pipeline_mode=pl.Buffered(1)),
                  whole(w_alpha_up.shape), whole((1, GLA_HEADS * GLA_DK)),
                  whole((1, FOX_HEADS)), whole((1, GLA_DV))],
        out_specs=[rows(GLA_OUT_W),
                   rows(FOX_W), pl.BlockSpec((FOX_W, PROJ_ROWS), lambda i: (0, i)),
                   rows(FOX_W), rows(FOX_W),
                   rows(MEM_W), rows(MEM_W),
                   pl.BlockSpec((SUBLANES, PROJ_ROWS), lambda i: (0, i)),
                   stat_spec, pl.BlockSpec((1, 1, LANES), lambda i: (0, 0, 0))],
        out_shape=[bshape(GLA_OUT_W),
                   bshape(FOX_W), jax.ShapeDtypeStruct((FOX_W, T), BF16),
                   bshape(FOX_W), bshape(FOX_W),
                   bshape(MEM_W), bshape(MEM_W),
                   jax.ShapeDtypeStruct((SUBLANES, T), F32),
                   stat_shape, jax.ShapeDtypeStruct((1, 1, LANES), F32)],
        scratch_shapes=[
            pltpu.VMEM((SUBLANES, LANES), F32),
            pltpu.VMEM((IN_COLS_PAD, D_MODEL), BF16),
            pltpu.VMEM((SMALL_W, GLA_QK_W), F32),
            pltpu.VMEM((1, GLA_QK_W), F32),
            pltpu.VMEM((1, SMALL_W), F32),
            pltpu.VMEM((1, GLA_DV_PAD), F32),
            pltpu.VMEM((FOX_W, LANES), BF16),
            pltpu.VMEM((PROJ_ROWS, GLA_QK_W), BF16),
            pltpu.VMEM((PROJ_ROWS, GLA_QK_W), BF16),
            pltpu.VMEM((PROJ_ROWS, GLA_V_W), BF16),
            pltpu.VMEM((PROJ_ROWS, GLA_V_W), BF16),
            pltpu.VMEM((PROJ_ROWS, GLA_QK_W), F32),
            pltpu.VMEM((gla_pairs, LANES, pair_w), F32),
            pltpu.VMEM((PROJ_ROWS, GLA_V_W), BF16),
            pltpu.VMEM((gla_pairs, gla_chunks, LANES, pair_w), F32),
            pltpu.VMEM((gla_pairs, gla_chunks, LANES, LANES), F32),
            pltpu.VMEM((gla_pairs, gla_chunks, LANES, pair_w), BF16)],
        compiler_params=_params("arbitrary"),
        name="proj",
    )(x, norm_g[None, :], w_in_t, w_alpha_up, b_alpha[None, :], b_forget[None, :],
      gla_norm_g[None, :])

    mk, mv = pl.pallas_call(
        _memkv_kernel,
        out_shape=[jax.ShapeDtypeStruct((M, MEM_W), BF16)] * 2,
        compiler_params=pltpu.CompilerParams(vmem_limit_bytes=VMEM_LIMIT),
        name="memkv",
    )(mem, mem_norm_g[None, :], w_mem_kv)

    cend = crow[:FOX_HEADS, FOX_KEYS - 1::FOX_KEYS].reshape(-1)
    pair_rows = pl.BlockSpec((FOX_BLOCK, LANES), lambda p, i, *_: (i, p))
    pair_all = pl.BlockSpec((T, LANES), lambda p, i, *_: (0, p))
    fox = pl.pallas_call(
        _fox_kernel,
        grid_spec=pltpu.PrefetchScalarGridSpec(
            num_scalar_prefetch=3,
            grid=(FOX_HEADS // HEAD_PAIR, T // FOX_BLOCK),
            in_specs=[pair_rows, pl.BlockSpec((LANES, T), lambda p, i, *_: (p, 0)), pair_all,
                      pl.BlockSpec((SUBLANES, T), lambda p, i, *_: (0, 0)), pair_rows],
            out_specs=pair_rows,
            scratch_shapes=[pltpu.VMEM((HEAD_PAIR, FOX_BLOCK, LANES), F32),
                            pltpu.VMEM((HEAD_PAIR, FOX_BLOCK, LANES), F32)]),
        out_shape=bshape(FOX_W),
        compiler_params=_params("arbitrary", "arbitrary"),
        name="fox",
    )(qn2.reshape(-1), kn2.reshape(-1), cend, fq, fk, fv, crow, fgate)

    out = pl.pallas_call(
        _out_kernel,
        grid=(T // OUT_ROWS,),
        in_specs=[rows(D_MODEL, OUT_ROWS), rows(GLA_OUT_W, OUT_ROWS), rows(FOX_W, OUT_ROWS),
                  rows(MEM_W, OUT_ROWS), rows(MEM_W, OUT_ROWS),
                  whole((M, MEM_W)), whole((M, MEM_W)),
                  pl.BlockSpec(w_out.shape, lambda i: (0, 0), pipeline_mode=pl.Buffered(1)),
                  whole((1, D_MODEL))],
        out_specs=rows(D_MODEL, OUT_ROWS),
        out_shape=jax.ShapeDtypeStruct((T, D_MODEL), F32),
        scratch_shapes=[pltpu.VMEM((D_MODEL, D_MODEL), BF16)],
        compiler_params=_params("arbitrary"),
        name="out",
    )(x, gla, fox, mq, mg, mk, mv, w_out, out_g[None, :])
    return out


def kernel(x, mem, norm_g, w_in, w_alpha_up, b_alpha, b_forget, gla_norm_g, mem_norm_g,
           w_mem_kv, w_out, final_norm_g):
    assert x.shape[0] == 1 and mem.shape[0] == 1 and norm_g.shape[0] == 1
    assert x.shape[1] % max(PROJ_ROWS, FOX_BLOCK, OUT_ROWS) == 0
    out = _layer(x[0], mem[0], norm_g[0], w_in[0], w_alpha_up[0], b_alpha[0], b_forget[0],
                 gla_norm_g[0], mem_norm_g[0], w_mem_kv[0], w_out[0], final_norm_g)
    return out[None]
```

```python
import functools

import jax
import jax.numpy as jnp
from jax import lax
from jax.experimental import pallas as pl
from jax.experimental.pallas import tpu as pltpu

F32 = jnp.float32
BF16 = jnp.bfloat16

EPS = 1e-6
LANES = 128
SUBLANES = 8

D_MODEL = 1024
GLA_HEADS, GLA_DK, GLA_DV, GLA_RANK = 4, 48, 96, 16
GLA_DK_PAD = 64
GLA_DV_PAD = LANES
GLA_GATE_NORM = 16.0
GLA_CHUNK = 64
FOX_HEADS, FOX_DH = 6, 64
MEM_HEADS, MEM_DH = 4, 64
HEAD_PAIR = 2
GLA_QK_W = GLA_HEADS * GLA_DK_PAD
GLA_V_W = GLA_HEADS * GLA_DV_PAD
GLA_OUT_W = GLA_HEADS * GLA_DV
FOX_W = FOX_HEADS * FOX_DH
MEM_W = MEM_HEADS * MEM_DH
SMALL_W = LANES
FG_LANE0 = 0
LR_LANE0 = SUBLANES

_GROUPS = (("gq", GLA_QK_W), ("gk", GLA_QK_W), ("gv", GLA_V_W), ("gg", GLA_V_W),
           ("fq", FOX_W), ("fk", FOX_W), ("fv", FOX_W), ("fgate", FOX_W),
           ("mq", MEM_W), ("mg", MEM_W), ("small", SMALL_W))
_OFF = {}
_o = 0
for _n, _w in _GROUPS:
    _OFF[_n] = (_o, _o + _w)
    _o += _w
IN_COLS_PAD = _o

PROJ_ROWS = 1024
FOX_BLOCK = 2048
FOX_KEYS = 512
OUT_ROWS = 1024
VMEM_LIMIT = 56 * 1024 * 1024

NEG_BIG = -1e30
FOX_SKIP_NATS = 105.0
NORM_SLACK = 1.02


def _log_sigmoid(z):
    return jnp.minimum(z, 0.0) - jnp.log(1.0 + jnp.exp(-jnp.abs(z)))


def _silu(z):
    return z / (1.0 + jnp.exp(-z))


def _rms_scale(v, width):
    return lax.rsqrt(jnp.sum(v * v, axis=-1, keepdims=True) * (1.0 / width) + EPS)


def _w_in_segments():
    qk, gw = GLA_HEADS * GLA_DK, GLA_HEADS * GLA_DV
    src = {}
    o = 0
    for name, width in (("gq", qk), ("gk", qk), ("gv", gw), ("lr", GLA_RANK), ("gg", gw),
                        ("fq", FOX_W), ("fk", FOX_W), ("fv", FOX_W), ("fg", FOX_HEADS),
                        ("fgate", FOX_W), ("mq", MEM_W), ("mg", MEM_W)):
        src[name] = o
        o += width
    segs = []
    for name, d, d_pad in (("gq", GLA_DK, GLA_DK_PAD), ("gk", GLA_DK, GLA_DK_PAD),
                           ("gv", GLA_DV, GLA_DV_PAD), ("gg", GLA_DV, GLA_DV_PAD)):
        segs += [(src[name] + h * d, _OFF[name][0] + h * d_pad, d) for h in range(GLA_HEADS)]
    segs += [(src[name], _OFF[name][0], _OFF[name][1] - _OFF[name][0])
             for name in ("fq", "fk", "fv", "fgate", "mq", "mg")]
    segs += [(src["fg"], _OFF["small"][0] + FG_LANE0, FOX_HEADS),
             (src["lr"], _OFF["small"][0] + LR_LANE0, GLA_RANK)]
    return tuple(segs)


def _proj_kernel(x_ref, g_ref, w_in_ref, w_alpha_ref, b_alpha_ref, b_forget_ref, gla_g_ref,
                 gla_ref, fq_ref, fk_ref, fv_ref, fgate_ref,
                 mq_ref, mg_ref, crow_ref, qn2_ref, kn2_ref,
                 carry_ref, wt_ref, wa_ref, ba_ref, bf_ref, ng_ref, seg_ref,
                 gq_ref, gk_ref, gv_ref, gg_ref, loga_ref,
                 s_ref, lhs_ref, kv_ref, dec_ref, sprev_ref):
    rows = x_ref.shape[0]
    k_chunks = D_MODEL // LANES

    @pl.when(pl.program_id(0) == 0)
    def _():
        carry_ref[...] = jnp.zeros_like(carry_ref)
        s_ref[...] = jnp.zeros_like(s_ref)
        kn2_ref[...] = jnp.zeros_like(kn2_ref)
        wa_ref[...] = jnp.zeros_like(wa_ref)
        ba_ref[...] = jnp.zeros_like(ba_ref)
        bf_ref[...] = jnp.zeros_like(bf_ref)
        ng_ref[...] = jnp.zeros_like(ng_ref)
        for h in range(GLA_HEADS):
            src = slice(h * GLA_DK, (h + 1) * GLA_DK)
            dst = slice(h * GLA_DK_PAD, h * GLA_DK_PAD + GLA_DK)
            wa_ref[LR_LANE0:LR_LANE0 + GLA_RANK, dst] = w_alpha_ref[:, src]
            ba_ref[:, dst] = b_alpha_ref[:, src]
        bf_ref[:, FG_LANE0:FG_LANE0 + FOX_HEADS] = b_forget_ref[...]
        ng_ref[:, 0:GLA_DV] = gla_g_ref[...]
        seg_ref[...] = (lax.broadcasted_iota(jnp.int32, seg_ref.shape, 0) // FOX_DH
                        == lax.broadcasted_iota(jnp.int32, seg_ref.shape, 1)).astype(BF16)
        wt_ref[...] = jnp.zeros_like(wt_ref)
        for s0, d0, width in _w_in_segments():
            for c in range(k_chunks):
                wt_ref[d0:d0 + width, c * LANES:(c + 1) * LANES] = (
                    w_in_ref[pl.ds(s0 * k_chunks + c, width, stride=k_chunks), :].astype(BF16))

    x = x_ref[...]
    xn = (x * _rms_scale(x, D_MODEL) * g_ref[...]).astype(BF16)
    nt = (((1,), (1,)), ((), ()))

    def proj(first, last):
        lo, hi = _OFF[first][0], _OFF[last][1]
        y = lax.dot_general(xn, wt_ref[lo:hi, :], nt, preferred_element_type=F32)
        return lambda name: y[:, _OFF[name][0] - lo:_OFF[name][1] - lo]

    tail = proj("mq", "small")
    gla = proj("gq", "gg")
    small = tail("small")
    logf = _log_sigmoid(small + bf_ref[...])
    c = logf.T[0:SUBLANES, :]
    lane = lax.broadcasted_iota(jnp.int32, c.shape, 1)
    shift = 1
    while shift < rows:
        c = c + jnp.where(lane >= shift, pltpu.roll(c, shift, axis=1), 0.0)
        shift *= 2
    c = c + carry_ref[:, 0:1]
    crow_ref[...] = c
    carry_ref[...] = jnp.broadcast_to(c[:, rows - 1:rows], carry_ref.shape)

    z = (jnp.dot(small.astype(BF16), wa_ref[...].astype(BF16), preferred_element_type=F32)
         + ba_ref[...])
    loga_ref[...] = _log_sigmoid(z) * (1.0 / GLA_GATE_NORM)
    gq_ref[...] = gla("gq").astype(BF16)
    gk_ref[...] = gla("gk").astype(BF16)
    gv_ref[...] = gla("gv").astype(BF16)
    gg_ref[...] = _silu(gla("gg")).astype(BF16)
    gla_local, gla_scan, gla_output = _gla_block(
        gq_ref, gk_ref, gv_ref, loga_ref, gg_ref, ng_ref, gla_ref,
        s_ref, lhs_ref, kv_ref, dec_ref, sprev_ref)

    def max_sq_norm(v):
        v32 = v.astype(F32)
        n2 = jnp.dot((v32 * v32).astype(BF16), seg_ref[...], preferred_element_type=F32)
        return jnp.max(n2, axis=0, keepdims=True)

    gla_local()
    fox_qk = proj("fq", "fk")
    mq_ref[...] = (tail("mq") * MEM_DH ** -0.5).astype(BF16)
    mg_ref[...] = _silu(tail("mg")).astype(BF16)
    gla_scan()
    fq = (fox_qk("fq") * FOX_DH ** -0.5).astype(BF16)
    fk = fox_qk("fk").astype(BF16)
    fq_ref[...] = fq
    fk_ref[...] = fox_qk("fk").T.astype(BF16)
    gla_output()
    fox_vg = proj("fv", "fgate")
    fv_ref[...] = fox_vg("fv").astype(BF16)
    fgate_ref[...] = _silu(fox_vg("fgate")).astype(BF16)
    qn2_ref[0] = max_sq_norm(fq)
    kn2_ref[0] = jnp.maximum(kn2_ref[0], max_sq_norm(fk))


def _memkv_kernel(mem_ref, g_ref, w_ref, mk_ref, mv_ref):
    m = mem_ref[...]
    mn = (m * _rms_scale(m, D_MODEL) * g_ref[...]).astype(BF16)
    kv = jnp.dot(mn, w_ref[...].astype(BF16), preferred_element_type=F32)
    mk_ref[...] = kv[:, :MEM_W].astype(BF16)
    mv_ref[...] = kv[:, MEM_W:].astype(BF16)


def _gla_block(q_ref, k_ref, v_ref, loga_ref, gate_ref, ng_ref, o_ref,
               s_ref, lhs_ref, kv_ref, dec_ref, sprev_ref):
    C = GLA_CHUNK
    W = HEAD_PAIR * GLA_DV_PAD
    n_chunks = q_ref.shape[0] // C

    row = lax.broadcasted_iota(jnp.int32, (C, LANES), 0)
    lane = lax.broadcasted_iota(jnp.int32, (C, LANES), 1)
    lo_k = lane < GLA_DK_PAD
    causal = row >= jnp.where(lo_k, lane, lane - GLA_DK_PAD)
    lo_v = lax.broadcasted_iota(jnp.int32, (C, W), 1) < GLA_DV_PAD
    st_row = lax.broadcasted_iota(jnp.int32, (LANES, W), 0)
    st_lane = lax.broadcasted_iota(jnp.int32, (LANES, W), 1)
    own = (st_row < GLA_DK_PAD) == (st_lane < GLA_DV_PAD)
    eye = (lax.broadcasted_iota(jnp.int32, (LANES, LANES), 0)
           == lax.broadcasted_iota(jnp.int32, (LANES, LANES), 1))
    scale = GLA_DK ** -0.5
    nt = (((1,), (1,)), ((), ()))
    tn = (((0,), (0,)), ((), ()))
    ng = jnp.concatenate([ng_ref[...]] * HEAD_PAIR, axis=1)

    pairs = range(GLA_HEADS // HEAD_PAIR)

    def local(ci):
        rs = slice(ci * C, (ci + 1) * C)
        for p in pairs:
            ls = slice(p * LANES, (p + 1) * LANES)
            vs = slice(p * W, (p + 1) * W)
            b = loga_ref[rs, ls]
            shift = 1
            while shift < C:
                b = b + jnp.where(row >= shift, pltpu.roll(b, shift, axis=0), 0.0)
                shift *= 2
            b_last = b[C - 1:C, :]
            k2 = k_ref[rs, ls].astype(F32)
            qd = (q_ref[rs, ls].astype(F32) * scale * jnp.exp(b)).astype(BF16)
            kd = (k2 * jnp.exp(-b)).astype(BF16)
            ke = (k2 * jnp.exp(b_last - b)).astype(BF16)
            zk = jnp.zeros_like(kd)
            kd_blk = jnp.concatenate([jnp.where(lo_k, kd, zk), jnp.where(lo_k, zk, kd)], axis=0)
            attn = lax.dot_general(qd, kd_blk, nt, preferred_element_type=F32)
            lhs_ref[rs, vs] = jnp.concatenate([jnp.where(causal, attn, 0.0).astype(BF16), qd], axis=1)
            kv = lax.dot_general(ke, v_ref[rs, vs], tn, preferred_element_type=F32)
            kv_ref[p, ci] = jnp.where(own, kv, 0.0)
            dcol = jnp.exp(jnp.sum(jnp.where(eye, jnp.broadcast_to(b_last, (LANES, LANES)), 0.0),
                                   axis=1, keepdims=True))
            dec_ref[p, ci] = jnp.broadcast_to(dcol, (LANES, LANES))

    def scan(ci):
        for p in pairs:
            s_prev = s_ref[p]
            sprev_ref[p, ci] = s_prev.astype(BF16)
            s_ref[p] = jnp.tile(dec_ref[p, ci], (1, HEAD_PAIR)) * s_prev + kv_ref[p, ci]

    def output(ci):
        rs = slice(ci * C, (ci + 1) * C)
        for p in pairs:
            vs = slice(p * W, (p + 1) * W)
            v2 = v_ref[rs, vs]
            zv = jnp.zeros_like(v2)
            v_blk = jnp.concatenate([jnp.where(lo_v, v2, zv), jnp.where(lo_v, zv, v2)], axis=0)
            o = jnp.dot(lhs_ref[rs, vs], jnp.concatenate([v_blk, sprev_ref[p, ci]], axis=0),
                        preferred_element_type=F32)
            o2 = o * o
            ms = jnp.where(lo_v, jnp.sum(o2[:, :GLA_DV_PAD], axis=1, keepdims=True),
                           jnp.sum(o2[:, GLA_DV_PAD:], axis=1, keepdims=True))
            on = o * lax.rsqrt(ms * (1.0 / GLA_DV) + EPS) * ng
            og = (on * gate_ref[rs, vs].astype(F32)).astype(BF16)
            for hh in range(HEAD_PAIR):
                c0 = (p * HEAD_PAIR + hh) * GLA_DV
                o_ref[rs, c0:c0 + GLA_DV] = og[:, hh * GLA_DV_PAD:hh * GLA_DV_PAD + GLA_DV]

    def all_chunks(phase):
        return lambda: [phase(ci) for ci in range(n_chunks)]

    return all_chunks(local), all_chunks(scan), all_chunks(output)


def _fox_kernel(qn2_ref, kn2_ref, cend_ref, q_ref, k_ref, v_ref, crow_ref, gate_ref, o_ref,
                m_ref, acc_ref):
    blk = FOX_KEYS
    streams = range(q_ref.shape[0] // blk)
    pair = pl.program_id(0)
    qi = pl.program_id(1)
    nblk = pl.num_programs(1) * len(streams)
    lane = lax.broadcasted_iota(jnp.int32, (1, LANES), 1)
    lo_lanes = lane < FOX_DH
    reps = blk // LANES
    diag = [qi * len(streams) + s for s in streams]

    q = q_ref[...]
    zero = jnp.zeros_like(q)
    q_lo, q_hi = jnp.where(lo_lanes, q, zero), jnp.where(lo_lanes, zero, q)
    q_stack = [jnp.concatenate([q_lo[s * blk:(s + 1) * blk], q_hi[s * blk:(s + 1) * blk]], axis=0)
               for s in streams]
    m_ref[...] = jnp.full_like(m_ref, NEG_BIG)
    acc_ref[...] = jnp.zeros_like(acc_ref)

    def head_row(c8, h):
        sub = lax.broadcasted_iota(jnp.int32, c8.shape, 0)
        return jnp.sum(jnp.where(sub == h, c8, 0.0), axis=0, keepdims=True)

    c_q0 = [[head_row(crow_ref[:, pl.ds(pl.multiple_of(diag[s] * blk, blk), LANES)],
                      pair * HEAD_PAIR + hh)[:, 0:1] for hh in range(HEAD_PAIR)]
            for s in streams]

    def scores(s, j):
        ks = pl.ds(pl.multiple_of(jnp.maximum(j, 0) * blk, blk), blk)
        kb = k_ref[:, ks]
        return jnp.dot(q_stack[s], kb, preferred_element_type=F32)

    def finish(s, j, s_all, masked):
        penalty = jnp.where(j >= 0, 0.0, NEG_BIG)
        ks = pl.ds(pl.multiple_of(jnp.maximum(j, 0) * blk, blk), blk)
        rows = slice(s * blk, (s + 1) * blk)
        vb = v_ref[ks, :]
        one = jnp.ones_like(vb)
        vaug = (jnp.where(lo_lanes, vb, one), jnp.where(lo_lanes, one, vb))
        for hh in range(HEAD_PAIR):
            h = pair * HEAD_PAIR + hh
            bias = (c_q0[s][hh] + penalty) - head_row(crow_ref[:, ks], h)
            sc = s_all[hh * blk:(hh + 1) * blk] + bias
            if masked:
                qpos = lax.broadcasted_iota(jnp.int32, (blk, blk), 0)
                kpos = lax.broadcasted_iota(jnp.int32, (blk, blk), 1)
                sc = jnp.where(kpos <= qpos, sc, NEG_BIG)
            m_prev = m_ref[hh, rows]
            m_new = jnp.maximum(m_prev, jnp.max(sc, axis=1, keepdims=True))
            p = jnp.exp(sc - jnp.tile(m_new, (1, reps)))
            alpha = jnp.exp(m_prev - m_new)
            pv = jnp.dot(p.astype(BF16), vaug[hh], preferred_element_type=F32)
            acc_ref[hh, rows] = alpha * acc_ref[hh, rows] + pv
            m_ref[hh, rows] = m_new

    def sweep(back, masked=False):
        js = [diag[s] - back for s in streams]
        s_alls = [scores(s, js[s]) for s in streams]
        for s in streams:
            finish(s, js[s], s_alls[s], masked)

    sweep(0, masked=True)
    m_min = [[jnp.min(m_ref[hh, s * blk:(s + 1) * blk]) for hh in range(HEAD_PAIR)]
             for s in streams]
    sweep(1)

    k_max2 = [kn2_ref[pair * HEAD_PAIR + hh] for hh in range(HEAD_PAIR)]

    def live(t):
        keep = False
        for s in streams:
            j = diag[s] - 1 - t
            jj = jnp.maximum(j, 0)
            prev = jnp.maximum(diag[s] - 1, 0)
            stat = (diag[s] * blk // PROJ_ROWS) * LANES
            for hh in range(HEAD_PAIR):
                h = pair * HEAD_PAIR + hh
                gap = (FOX_SKIP_NATS + cend_ref[h * nblk + prev] - cend_ref[h * nblk + jj]) - m_min[s][hh]
                norm2 = (NORM_SLACK * NORM_SLACK) * qn2_ref[stat + h] * k_max2[hh]
                dead = jnp.logical_and(gap <= 0.0, norm2 <= gap * gap)
                keep = jnp.logical_or(keep, jnp.logical_and(j >= 0, jnp.logical_not(dead)))
        return keep

    def body(t):
        sweep(t + 1)
        return t + 1

    lax.while_loop(live, body, 1)

    outs = []
    for hh in range(HEAD_PAIR):
        acc = acc_ref[hh]
        outs.append(acc / pltpu.roll(acc, FOX_DH, axis=1))
    o = jnp.where(lo_lanes, outs[0], outs[1])
    o_ref[...] = (o * gate_ref[...].astype(F32)).astype(BF16)


def _out_kernel(x_ref, gla_ref, fox_ref, mq_ref, mg_ref, mk_ref, mv_ref,
                w_out_ref, fg_ref, o_ref, wo_ref):
    @pl.when(pl.program_id(0) == 0)
    def _():
        wo_ref[...] = w_out_ref[...].astype(BF16)

    lane = lax.broadcasted_iota(jnp.int32, (1, LANES), 1)
    lo_lanes = lane < MEM_DH
    nt = (((1,), (1,)), ((), ()))
    mem_parts = []
    for p in range(MEM_HEADS // HEAD_PAIR):
        ls = slice(p * LANES, (p + 1) * LANES)
        q = mq_ref[:, ls]
        kb = mk_ref[:, ls]
        vb = mv_ref[:, ls]
        zero = jnp.zeros_like(q)
        one = jnp.ones_like(vb)
        qh = (jnp.where(lo_lanes, q, zero), jnp.where(lo_lanes, zero, q))
        vaug = (jnp.where(lo_lanes, vb, one), jnp.where(lo_lanes, one, vb))
        outs = []
        for hh in range(HEAD_PAIR):
            s = lax.dot_general(qh[hh], kb, nt, preferred_element_type=F32)
            pexp = jnp.exp(s - jnp.max(s, axis=1, keepdims=True))
            pv = jnp.dot(pexp.astype(BF16), vaug[hh], preferred_element_type=F32)
            outs.append(pv / pltpu.roll(pv, MEM_DH, axis=1))
        o = jnp.where(lo_lanes, outs[0], outs[1])
        mem_parts.append((o * mg_ref[:, ls].astype(F32)).astype(BF16))
    mixed = jnp.concatenate([gla_ref[...], fox_ref[...]] + mem_parts, axis=1)
    y = x_ref[...] + jnp.dot(mixed, wo_ref[...], preferred_element_type=F32)
    o_ref[...] = y * _rms_scale(y, D_MODEL) * fg_ref[...]


def _params(*sem):
    return pltpu.CompilerParams(dimension_semantics=sem, vmem_limit_bytes=VMEM_LIMIT)


def _layer(x, mem, norm_g, w_in, w_alpha_up, b_alpha, b_forget, gla_norm_g,
           mem_norm_g, w_mem_kv, w_out, out_g):
    w_in_t = jnp.transpose(w_in[None], (0, 2, 1)).reshape(-1, LANES)
    T = x.shape[0]
    M = mem.shape[0]

    def rows(width, n=PROJ_ROWS):
        return pl.BlockSpec((n, width), lambda i: (i, 0))

    def whole(shape):
        return pl.BlockSpec(shape, lambda i: (0,) * len(shape))

    bshape = lambda w: jax.ShapeDtypeStruct((T, w), BF16)
    nproj = T // PROJ_ROWS
    stat_spec = pl.BlockSpec((1, 1, LANES), lambda i: (i, 0, 0))
    stat_shape = jax.ShapeDtypeStruct((nproj, 1, LANES), F32)
    gla_pairs, gla_chunks = GLA_HEADS // HEAD_PAIR, PROJ_ROWS // GLA_CHUNK
    pair_w = HEAD_PAIR * GLA_DV_PAD
    (gla, fq, fk, fv, fgate, mq, mg, crow, qn2, kn2) = pl.pallas_call(
        _proj_kernel,
        grid=(nproj,),
        in_specs=[rows(D_MODEL), whole((1, D_MODEL)),
                  pl.BlockSpec(w_in_t.shape, lambda i: (0, 0), pipeline_mode=pl.Buffered(1)),
                  whole(w_alpha_up.shape), whole((1, GLA_HEADS * GLA_DK)),
                  whole((1, FOX_HEADS)), whole((1, GLA_DV))],
        out_specs=[rows(GLA_OUT_W),
                   rows(FOX_W), pl.BlockSpec((FOX_W, PROJ_ROWS), lambda i: (0, i)),
                   rows(FOX_W), rows(FOX_W),
                   rows(MEM_W), rows(MEM_W),
                   pl.BlockSpec((SUBLANES, PROJ_ROWS), lambda i: (0, i)),
                   stat_spec, pl.BlockSpec((1, 1, LANES), lambda i: (0, 0, 0))],
        out_shape=[bshape(GLA_OUT_W),
                   bshape(FOX_W), jax.ShapeDtypeStruct((FOX_W, T), BF16),
                   bshape(FOX_W), bshape(FOX_W),
                   bshape(MEM_W), bshape(MEM_W),
                   jax.ShapeDtypeStruct((SUBLANES, T), F32),
                   stat_shape, jax.ShapeDtypeStruct((1, 1, LANES), F32)],
        scratch_shapes=[
            pltpu.VMEM((SUBLANES, LANES), F32),
            pltpu.VMEM((IN_COLS_PAD, D_MODEL), BF16),
            pltpu.VMEM((SMALL_W, GLA_QK_W), F32),
            pltpu.VMEM((1, GLA_QK_W), F32),
            pltpu.VMEM((1, SMALL_W), F32),
            pltpu.VMEM((1, GLA_DV_PAD), F32),
            pltpu.VMEM((FOX_W, LANES), BF16),
            pltpu.VMEM((PROJ_ROWS, GLA_QK_W), BF16),
            pltpu.VMEM((PROJ_ROWS, GLA_QK_W), BF16),
            pltpu.VMEM((PROJ_ROWS, GLA_V_W), BF16),
            pltpu.VMEM((PROJ_ROWS, GLA_V_W), BF16),
            pltpu.VMEM((PROJ_ROWS, GLA_QK_W), F32),
            pltpu.VMEM((gla_pairs, LANES, pair_w), F32),
            pltpu.VMEM((PROJ_ROWS, GLA_V_W), BF16),
            pltpu.VMEM((gla_pairs, gla_chunks, LANES, pair_w), F32),
            pltpu.VMEM((gla_pairs, gla_chunks, LANES, LANES), F32),
            pltpu.VMEM((gla_pairs, gla_chunks, LANES, pair_w), BF16)],
        compiler_params=_params("arbitrary"),
        name="proj",
    )(x, norm_g[None, :], w_in_t, w_alpha_up, b_alpha[None, :], b_forget[None, :],
      gla_norm_g[None, :])

    mk, mv = pl.pallas_call(
        _memkv_kernel,
        out_shape=[jax.ShapeDtypeStruct((M, MEM_W), BF16)] * 2,
        compiler_params=pltpu.CompilerParams(vmem_limit_bytes=VMEM_LIMIT),
        name="memkv",
    )(mem, mem_norm_g[None, :], w_mem_kv)

    cend = crow[:FOX_HEADS, FOX_KEYS - 1::FOX_KEYS].reshape(-1)
    pair_rows = pl.BlockSpec((FOX_BLOCK, LANES), lambda p, i, *_: (i, p))
    pair_all = pl.BlockSpec((T, LANES), lambda p, i, *_: (0, p))
    fox = pl.pallas_call(
        _fox_kernel,
        grid_spec=pltpu.PrefetchScalarGridSpec(
            num_scalar_prefetch=3,
            grid=(FOX_HEADS // HEAD_PAIR, T // FOX_BLOCK),
            in_specs=[pair_rows, pl.BlockSpec((LANES, T), lambda p, i, *_: (p, 0)), pair_all,
                      pl.BlockSpec((SUBLANES, T), lambda p, i, *_: (0, 0)), pair_rows],
            out_specs=pair_rows,
            scratch_shapes=[pltpu.VMEM((HEAD_PAIR, FOX_BLOCK, LANES), F32),
                            pltpu.VMEM((HEAD_PAIR, FOX_BLOCK, LANES), F32)]),
        out_shape=bshape(FOX_W),
        compiler_params=_params("arbitrary", "arbitrary"),
        name="fox",
    )(qn2.reshape(-1), kn2.reshape(-1), cend, fq, fk, fv, crow, fgate)

    out = pl.pallas_call(
        _out_kernel,
        grid=(T // OUT_ROWS,),
        in_specs=[rows(D_MODEL, OUT_ROWS), rows(GLA_OUT_W, OUT_ROWS), rows(FOX_W, OUT_ROWS),
                  rows(MEM_W, OUT_ROWS), rows(MEM_W, OUT_ROWS),
                  whole((M, MEM_W)), whole((M, MEM_W)),
                  pl.BlockSpec(w_out.shape, lambda i: (0, 0), pipeline_mode=pl.Buffered(1)),
                  whole((1, D_MODEL))],
        out_specs=rows(D_MODEL, OUT_ROWS),
        out_shape=jax.ShapeDtypeStruct((T, D_MODEL), F32),
        scratch_shapes=[pltpu.VMEM((D_MODEL, D_MODEL), BF16)],
        compiler_params=_params("arbitrary"),
        name="out",
    )(x, gla, fox, mq, mg, mk, mv, w_out, out_g[None, :])
    return out


def kernel(x, mem, norm_g, w_in, w_alpha_up, b_alpha, b_forget, gla_norm_g, mem_norm_g,
           w_mem_kv, w_out, final_norm_g):
    assert x.shape[0] == 1 and mem.shape[0] == 1 and norm_g.shape[0] == 1
    assert x.shape[1] % max(PROJ_ROWS, FOX_BLOCK, OUT_ROWS) == 0
    out = _layer(x[0], mem[0], norm_g[0], w_in[0], w_alpha_up[0], b_alpha[0], b_forget[0],
                 gla_norm_g[0], mem_norm_g[0], w_mem_kv[0], w_out[0], final_norm_g)
    return out[None]
```

```python
import functools

import jax
import jax.numpy as jnp
from jax import lax
from jax.experimental import pallas as pl
from jax.experimental.pallas import tpu as pltpu

F32 = jnp.float32
BF16 = jnp.bfloat16

EPS = 1e-6
LANES = 128
SUBLANES = 8

D_MODEL = 1024
GLA_HEADS, GLA_DK, GLA_DV, GLA_RANK = 4, 48, 96, 16
GLA_DK_PAD = 64
GLA_DV_PAD = LANES
GLA_GATE_NORM = 16.0
GLA_CHUNK = 64
FOX_HEADS, FOX_DH = 6, 64
MEM_HEADS, MEM_DH = 4, 64
HEAD_PAIR = 2
GLA_QK_W = GLA_HEADS * GLA_DK_PAD
GLA_V_W = GLA_HEADS * GLA_DV_PAD
GLA_OUT_W = GLA_HEADS * GLA_DV
FOX_W = FOX_HEADS * FOX_DH
MEM_W = MEM_HEADS * MEM_DH
SMALL_W = LANES
FG_LANE0 = 0
LR_LANE0 = SUBLANES

_GROUPS = (("gq", GLA_QK_W), ("gk", GLA_QK_W), ("gv", GLA_V_W), ("gg", GLA_V_W),
           ("fq", FOX_W), ("fk", FOX_W), ("fv", FOX_W), ("fgate", FOX_W),
           ("mq", MEM_W), ("mg", MEM_W), ("small", SMALL_W))
_OFF = {}
_o = 0
for _n, _w in _GROUPS:
    _OFF[_n] = (_o, _o + _w)
    _o += _w
IN_COLS_PAD = _o

PROJ_ROWS = 1024
FOX_BLOCK = 2048
FOX_KEYS = 512
OUT_ROWS = 1024
VMEM_LIMIT = 56 * 1024 * 1024

NEG_BIG = -1e30
FOX_SKIP_NATS = 105.0
NORM_SLACK = 1.02
FOX_FIXED_NORM2 = 900.0
CX_HI, CX_MID, CX_LO, CX_ONE = 0, 8, 16, 24


def _log_sigmoid(z):
    return jnp.minimum(z, 0.0) - jnp.log(1.0 + jnp.exp(-jnp.abs(z)))


def _silu(z):
    return z / (1.0 + jnp.exp(-z))


def _rms_scale(v, width):
    return lax.rsqrt(jnp.sum(v * v, axis=-1, keepdims=True) * (1.0 / width) + EPS)


def _w_in_segments():
    qk, gw = GLA_HEADS * GLA_DK, GLA_HEADS * GLA_DV
    src = {}
    o = 0
    for name, width in (("gq", qk), ("gk", qk), ("gv", gw), ("lr", GLA_RANK), ("gg", gw),
                        ("fq", FOX_W), ("fk", FOX_W), ("fv", FOX_W), ("fg", FOX_HEADS),
                        ("fgate", FOX_W), ("mq", MEM_W), ("mg", MEM_W)):
        src[name] = o
        o += width
    segs = []
    for name, d, d_pad in (("gq", GLA_DK, GLA_DK_PAD), ("gk", GLA_DK, GLA_DK_PAD),
                           ("gv", GLA_DV, GLA_DV_PAD), ("gg", GLA_DV, GLA_DV_PAD)):
        segs += [(src[name] + h * d, _OFF[name][0] + h * d_pad, d) for h in range(GLA_HEADS)]
    segs += [(src[name], _OFF[name][0], _OFF[name][1] - _OFF[name][0])
             for name in ("fq", "fk", "fv", "fgate", "mq", "mg")]
    segs += [(src["fg"], _OFF["small"][0] + FG_LANE0, FOX_HEADS),
             (src["lr"], _OFF["small"][0] + LR_LANE0, GLA_RANK)]
    return tuple(segs)


def _proj_kernel(x_ref, g_ref, w_in_ref, w_alpha_ref, b_alpha_ref, b_forget_ref, gla_g_ref,
                 gla_ref, fq_ref, fk_ref, fv_ref, fgate_ref,
                 mq_ref, mg_ref, crow_ref, cx_ref, qn2_ref, kn2_ref,
                 carry_ref, wt_ref, wa_ref, ba_ref, bf_ref, ng_ref, seg_ref,
                 gq_ref, gk_ref, gv_ref, gg_ref, loga_ref,
                 s_ref, lhs_ref, kv_ref, dec_ref, sprev_ref):
    rows = x_ref.shape[0]
    k_chunks = D_MODEL // LANES

    @pl.when(pl.program_id(0) == 0)
    def _():
        carry_ref[...] = jnp.zeros_like(carry_ref)
        s_ref[...] = jnp.zeros_like(s_ref)
        kn2_ref[...] = jnp.zeros_like(kn2_ref)
        wa_ref[...] = jnp.zeros_like(wa_ref)
        ba_ref[...] = jnp.zeros_like(ba_ref)
        bf_ref[...] = jnp.zeros_like(bf_ref)
        ng_ref[...] = jnp.zeros_like(ng_ref)
        for h in range(GLA_HEADS):
            src = slice(h * GLA_DK, (h + 1) * GLA_DK)
            dst = slice(h * GLA_DK_PAD, h * GLA_DK_PAD + GLA_DK)
            wa_ref[LR_LANE0:LR_LANE0 + GLA_RANK, dst] = w_alpha_ref[:, src]
            ba_ref[:, dst] = b_alpha_ref[:, src]
        bf_ref[:, FG_LANE0:FG_LANE0 + FOX_HEADS] = b_forget_ref[...]
        ng_ref[:, 0:GLA_DV] = gla_g_ref[...]
        seg_ref[...] = (lax.broadcasted_iota(jnp.int32, seg_ref.shape, 0) // FOX_DH
                        == lax.broadcasted_iota(jnp.int32, seg_ref.shape, 1)).astype(BF16)
        wt_ref[...] = jnp.zeros_like(wt_ref)
        for s0, d0, width in _w_in_segments():
            for c in range(k_chunks):
                wt_ref[d0:d0 + width, c * LANES:(c + 1) * LANES] = (
                    w_in_ref[pl.ds(s0 * k_chunks + c, width, stride=k_chunks), :].astype(BF16))

    x = x_ref[...]
    xn = (x * _rms_scale(x, D_MODEL) * g_ref[...]).astype(BF16)
    nt = (((1,), (1,)), ((), ()))

    def proj(first, last):
        lo, hi = _OFF[first][0], _OFF[last][1]
        y = lax.dot_general(xn, wt_ref[lo:hi, :], nt, preferred_element_type=F32)
        return lambda name: y[:, _OFF[name][0] - lo:_OFF[name][1] - lo]

    tail = proj("mq", "small")
    gla = proj("gq", "gg")
    small = tail("small")
    logf = _log_sigmoid(small + bf_ref[...])
    c = logf.T[0:SUBLANES, :]
    lane = lax.broadcasted_iota(jnp.int32, c.shape, 1)
    shift = 1
    while shift < rows:
        c = c + jnp.where(lane >= shift, pltpu.roll(c, shift, axis=1), 0.0)
        shift *= 2
    c = c + carry_ref[:, 0:1]
    crow_ref[...] = c
    carry_ref[...] = jnp.broadcast_to(c[:, rows - 1:rows], carry_ref.shape)
    neg = -c
    hi = neg.astype(BF16).astype(F32)
    mid = (neg - hi).astype(BF16).astype(F32)
    low = neg - hi - mid
    cx_ref[...] = jnp.concatenate(
        [hi, mid, low, jnp.ones_like(c), jnp.zeros((LANES - 4 * SUBLANES, rows), F32)],
        axis=0).astype(BF16)

    z = (jnp.dot(small.astype(BF16), wa_ref[...].astype(BF16), preferred_element_type=F32)
         + ba_ref[...])
    loga_ref[...] = _log_sigmoid(z) * (1.0 / GLA_GATE_NORM)
    gq_ref[...] = gla("gq").astype(BF16)
    gk_ref[...] = gla("gk").astype(BF16)
    gv_ref[...] = gla("gv").astype(BF16)
    gg_ref[...] = _silu(gla("gg")).astype(BF16)
    gla_local, gla_scan, gla_output = _gla_block(
        gq_ref, gk_ref, gv_ref, loga_ref, gg_ref, ng_ref, gla_ref,
        s_ref, lhs_ref, kv_ref, dec_ref, sprev_ref)

    def max_sq_norm(v):
        v32 = v.astype(F32)
        n2 = jnp.dot((v32 * v32).astype(BF16), seg_ref[...], preferred_element_type=F32)
        return jnp.max(n2, axis=0, keepdims=True)

    gla_local()
    fox_qk = proj("fq", "fk")
    mq_ref[...] = (tail("mq") * MEM_DH ** -0.5).astype(BF16)
    mg_ref[...] = _silu(tail("mg")).astype(BF16)
    gla_scan()
    fq = (fox_qk("fq") * FOX_DH ** -0.5).astype(BF16)
    fk = fox_qk("fk").astype(BF16)
    fq_ref[...] = fq
    fk_ref[...] = fox_qk("fk").T.astype(BF16)
    gla_output()
    fox_vg = proj("fv", "fgate")
    fv_ref[...] = fox_vg("fv").astype(BF16)
    fgate_ref[...] = _silu(fox_vg("fgate")).astype(BF16)
    qn2_ref[0] = max_sq_norm(fq)
    kn2_ref[0] = jnp.maximum(kn2_ref[0], max_sq_norm(fk))


def _memkv_kernel(mem_ref, g_ref, w_ref, mk_ref, mv_ref):
    m = mem_ref[...]
    mn = (m * _rms_scale(m, D_MODEL) * g_ref[...]).astype(BF16)
    kv = jnp.dot(mn, w_ref[...].astype(BF16), preferred_element_type=F32)
    mk_ref[...] = kv[:, :MEM_W].astype(BF16)
    mv_ref[...] = kv[:, MEM_W:].astype(BF16)


def _gla_block(q_ref, k_ref, v_ref, loga_ref, gate_ref, ng_ref, o_ref,
               s_ref, lhs_ref, kv_ref, dec_ref, sprev_ref):
    C = GLA_CHUNK
    W = HEAD_PAIR * GLA_DV_PAD
    n_chunks = q_ref.shape[0] // C

    row = lax.broadcasted_iota(jnp.int32, (C, LANES), 0)
    lane = lax.broadcasted_iota(jnp.int32, (C, LANES), 1)
    lo_k = lane < GLA_DK_PAD
    causal = row >= jnp.where(lo_k, lane, lane - GLA_DK_PAD)
    lo_v = lax.broadcasted_iota(jnp.int32, (C, W), 1) < GLA_DV_PAD
    st_row = lax.broadcasted_iota(jnp.int32, (LANES, W), 0)
    st_lane = lax.broadcasted_iota(jnp.int32, (LANES, W), 1)
    own = (st_row < GLA_DK_PAD) == (st_lane < GLA_DV_PAD)
    eye = (lax.broadcasted_iota(jnp.int32, (LANES, LANES), 0)
           == lax.broadcasted_iota(jnp.int32, (LANES, LANES), 1))
    scale = GLA_DK ** -0.5
    nt = (((1,), (1,)), ((), ()))
    tn = (((0,), (0,)), ((), ()))
    ng = jnp.concatenate([ng_ref[...]] * HEAD_PAIR, axis=1)

    pairs = range(GLA_HEADS // HEAD_PAIR)

    def local(ci):
        rs = slice(ci * C, (ci + 1) * C)
        for p in pairs:
            ls = slice(p * LANES, (p + 1) * LANES)
            vs = slice(p * W, (p + 1) * W)
            b = loga_ref[rs, ls]
            shift = 1
            while shift < C:
                b = b + jnp.where(row >= shift, pltpu.roll(b, shift, axis=0), 0.0)
                shift *= 2
            b_last = b[C - 1:C, :]
            k2 = k_ref[rs, ls].astype(F32)
            qd = (q_ref[rs, ls].astype(F32) * scale * jnp.exp(b)).astype(BF16)
            kd = (k2 * jnp.exp(-b)).astype(BF16)
            ke = (k2 * jnp.exp(b_last - b)).astype(BF16)
            zk = jnp.zeros_like(kd)
            kd_blk = jnp.concatenate([jnp.where(lo_k, kd, zk), jnp.where(lo_k, zk, kd)], axis=0)
            attn = lax.dot_general(qd, kd_blk, nt, preferred_element_type=F32)
            lhs_ref[rs, vs] = jnp.concatenate([jnp.where(causal, attn, 0.0).astype(BF16), qd], axis=1)
            kv = lax.dot_general(ke, v_ref[rs, vs], tn, preferred_element_type=F32)
            kv_ref[p, ci] = jnp.where(own, kv, 0.0)
            dcol = jnp.exp(jnp.sum(jnp.where(eye, jnp.broadcast_to(b_last, (LANES, LANES)), 0.0),
                                   axis=1, keepdims=True))
            dec_ref[p, ci] = jnp.broadcast_to(dcol, (LANES, LANES))

    def scan(ci):
        for p in pairs:
            s_prev = s_ref[p]
            sprev_ref[p, ci] = s_prev.astype(BF16)
            s_ref[p] = jnp.tile(dec_ref[p, ci], (1, HEAD_PAIR)) * s_prev + kv_ref[p, ci]

    def output(ci):
        rs = slice(ci * C, (ci + 1) * C)
        for p in pairs:
            vs = slice(p * W, (p + 1) * W)
            v2 = v_ref[rs, vs]
            zv = jnp.zeros_like(v2)
            v_blk = jnp.concatenate([jnp.where(lo_v, v2, zv), jnp.where(lo_v, zv, v2)], axis=0)
            o = jnp.dot(lhs_ref[rs, vs], jnp.concatenate([v_blk, sprev_ref[p, ci]], axis=0),
                        preferred_element_type=F32)
            o2 = o * o
            ms = jnp.where(lo_v, jnp.sum(o2[:, :GLA_DV_PAD], axis=1, keepdims=True),
                           jnp.sum(o2[:, GLA_DV_PAD:], axis=1, keepdims=True))
            on = o * lax.rsqrt(ms * (1.0 / GLA_DV) + EPS) * ng
            og = (on * gate_ref[rs, vs].astype(F32)).astype(BF16)
            for hh in range(HEAD_PAIR):
                c0 = (p * HEAD_PAIR + hh) * GLA_DV
                o_ref[rs, c0:c0 + GLA_DV] = og[:, hh * GLA_DV_PAD:hh * GLA_DV_PAD + GLA_DV]

    def all_chunks(phase):
        return lambda: [phase(ci) for ci in range(n_chunks)]

    return all_chunks(local), all_chunks(scan), all_chunks(output)


def _fox_kernel(qn2_ref, kn2_ref, cend_ref, q_ref, k_ref, cx_ref, v_ref, gate_ref, o_ref,
                m_ref, acc_ref):
    blk = FOX_KEYS
    streams = range(q_ref.shape[0] // blk)
    pair = pl.program_id(0)
    qi = pl.program_id(1)
    nblk = pl.num_programs(1) * len(streams)
    lane = lax.broadcasted_iota(jnp.int32, (1, LANES), 1)
    lo_lanes = lane < FOX_DH
    reps = blk // LANES
    diag = [qi * len(streams) + s for s in streams]
    heads = [pair * HEAD_PAIR + hh for hh in range(HEAD_PAIR)]

    q = q_ref[...]
    zero = jnp.zeros_like(q)
    q_lo, q_hi = jnp.where(lo_lanes, q, zero), jnp.where(lo_lanes, zero, q)
    q_stack = [jnp.concatenate([q_lo[s * blk:(s + 1) * blk], q_hi[s * blk:(s + 1) * blk]], axis=0)
               for s in streams]
    xlane = lax.broadcasted_iota(jnp.int32, (HEAD_PAIR * blk, LANES), 1)
    xhead = jnp.where(lax.broadcasted_iota(jnp.int32, (HEAD_PAIR * blk, LANES), 0) < blk,
                      heads[0], heads[1])
    pick_c = jnp.where((xlane == CX_HI + xhead) | (xlane == CX_MID + xhead)
                       | (xlane == CX_LO + xhead), 1.0, 0.0)

    def q_aug(s, shift=None, void=None):
        extra = pick_c
        if shift is not None:
            hi = shift.astype(BF16).astype(F32)
            mid = (shift - hi).astype(BF16).astype(F32)
            low = shift - hi - mid
            extra = jnp.where(xlane == CX_ONE, -hi, jnp.where(
                xlane == CX_ONE + 1, -mid, jnp.where(xlane == CX_ONE + 2, -low, extra)))
        if void is not None:
            extra = jnp.where(jnp.logical_and(xlane == CX_ONE + 3, void), NEG_BIG, extra)
        return jnp.concatenate([q_stack[s], extra.astype(BF16)], axis=1)

    m_ref[...] = jnp.full_like(m_ref, NEG_BIG)
    acc_ref[...] = jnp.zeros_like(acc_ref)

    def block(j):
        ks = pl.ds(pl.multiple_of(jnp.maximum(j, 0) * blk, blk), blk)
        k_aug = jnp.concatenate([k_ref[:, ks], cx_ref[:, ks]], axis=0)
        vb = v_ref[ks, :]
        one = jnp.ones_like(vb)
        return k_aug, (jnp.where(lo_lanes, vb, one), jnp.where(lo_lanes, one, vb))

    def step_online(s, j, qa, masked):
        rows = slice(s * blk, (s + 1) * blk)
        k_aug, vaug = block(j)
        s_all = jnp.dot(qa, k_aug, preferred_element_type=F32)
        for hh in range(HEAD_PAIR):
            sc = s_all[hh * blk:(hh + 1) * blk]
            if masked:
                qpos = lax.broadcasted_iota(jnp.int32, (blk, blk), 0)
                kpos = lax.broadcasted_iota(jnp.int32, (blk, blk), 1)
                sc = jnp.where(kpos <= qpos, sc, NEG_BIG)
            m_prev = m_ref[hh, rows]
            m_new = jnp.maximum(m_prev, jnp.max(sc, axis=1, keepdims=True))
            p = jnp.exp(sc - jnp.tile(m_new, (1, reps)))
            alpha = jnp.exp(m_prev - m_new)
            pv = jnp.dot(p.astype(BF16), vaug[hh], preferred_element_type=F32)
            acc_ref[hh, rows] = alpha * acc_ref[hh, rows] + pv
            m_ref[hh, rows] = m_new

    def step_fixed(s, j, qa):
        rows = slice(s * blk, (s + 1) * blk)
        k_aug, vaug = block(j)
        p_all = jnp.exp(jnp.dot(qa, k_aug, preferred_element_type=F32)).astype(BF16)
        for hh in range(HEAD_PAIR):
            acc_ref[hh, rows] += jnp.dot(p_all[hh * blk:(hh + 1) * blk], vaug[hh],
                                         preferred_element_type=F32)

    for s in streams:
        step_online(s, diag[s], q_aug(s), masked=True)
    m_min = [[jnp.min(m_ref[hh, s * blk:(s + 1) * blk]) for hh in range(HEAD_PAIR)]
             for s in streams]
    k_max2 = [kn2_ref[h] for h in heads]

    def norm2(s, hh):
        stat = (diag[s] * blk // PROJ_ROWS) * LANES
        return (NORM_SLACK * NORM_SLACK) * qn2_ref[stat + heads[hh]] * k_max2[hh]

    def live(t):
        keep = False
        for s in streams:
            j = diag[s] - 1 - t
            jj = jnp.maximum(j, 0)
            for hh in range(HEAD_PAIR):
                gap = (FOX_SKIP_NATS - cend_ref[heads[hh] * nblk + jj]) - m_min[s][hh]
                dead = jnp.logical_and(gap <= 0.0, norm2(s, hh) <= gap * gap)
                keep = jnp.logical_or(keep, jnp.logical_and(j >= 0, jnp.logical_not(dead)))
        return keep

    def tail(stepper):
        def sweep(back):
            for s in streams:
                stepper(s, diag[s] - back)

        def body(t):
            sweep(t + 1)
            return t + 1

        sweep(1)
        lax.while_loop(live, body, 1)

    fixed_ok = True
    for s in streams:
        for hh in range(HEAD_PAIR):
            fixed_ok = jnp.logical_and(fixed_ok, norm2(s, hh) <= FOX_FIXED_NORM2)

    @pl.when(fixed_ok)
    def _():
        q_fix, q_void = [], []
        for s in streams:
            shift = jnp.concatenate([m_ref[hh, s * blk:(s + 1) * blk] for hh in range(HEAD_PAIR)],
                                    axis=0)
            q_fix.append(q_aug(s, shift))
            q_void.append(q_aug(s, shift, void=True))
        tail(lambda s, j: step_fixed(s, j, jnp.where(j >= 0, q_fix[s], q_void[s])))

    @pl.when(jnp.logical_not(fixed_ok))
    def _():
        tail(lambda s, j: step_online(s, j, q_aug(s, void=j < 0), masked=False))

    outs = []
    for hh in range(HEAD_PAIR):
        acc = acc_ref[hh]
        outs.append(acc / pltpu.roll(acc, FOX_DH, axis=1))
    o = jnp.where(lo_lanes, outs[0], outs[1])
    o_ref[...] = (o * gate_ref[...].astype(F32)).astype(BF16)


def _out_kernel(x_ref, gla_ref, fox_ref, mq_ref, mg_ref, mk_ref, mv_ref,
                w_out_ref, fg_ref, o_ref, wo_ref):
    @pl.when(pl.program_id(0) == 0)
    def _():
        wo_ref[...] = w_out_ref[...].astype(BF16)

    lane = lax.broadcasted_iota(jnp.int32, (1, LANES), 1)
    lo_lanes = lane < MEM_DH
    nt = (((1,), (1,)), ((), ()))
    mem_parts = []
    for p in range(MEM_HEADS // HEAD_PAIR):
        ls = slice(p * LANES, (p + 1) * LANES)
        q = mq_ref[:, ls]
        kb = mk_ref[:, ls]
        vb = mv_ref[:, ls]
        zero = jnp.zeros_like(q)
        one = jnp.ones_like(vb)
        qh = (jnp.where(lo_lanes, q, zero), jnp.where(lo_lanes, zero, q))
        vaug = (jnp.where(lo_lanes, vb, one), jnp.where(lo_lanes, one, vb))
        outs = []
        for hh in range(HEAD_PAIR):
            s = lax.dot_general(qh[hh], kb, nt, preferred_element_type=F32)
            pexp = jnp.exp(s - jnp.max(s, axis=1, keepdims=True))
            pv = jnp.dot(pexp.astype(BF16), vaug[hh], preferred_element_type=F32)
            outs.append(pv / pltpu.roll(pv, MEM_DH, axis=1))
        o = jnp.where(lo_lanes, outs[0], outs[1])
        mem_parts.append((o * mg_ref[:, ls].astype(F32)).astype(BF16))
    mixed = jnp.concatenate([gla_ref[...], fox_ref[...]] + mem_parts, axis=1)
    y = x_ref[...] + jnp.dot(mixed, wo_ref[...], preferred_element_type=F32)
    o_ref[...] = y * _rms_scale(y, D_MODEL) * fg_ref[...]


def _params(*sem):
    return pltpu.CompilerParams(dimension_semantics=sem, vmem_limit_bytes=VMEM_LIMIT)


def _layer(x, mem, norm_g, w_in, w_alpha_up, b_alpha, b_forget, gla_norm_g,
           mem_norm_g, w_mem_kv, w_out, out_g):
    w_in_t = jnp.transpose(w_in[None], (0, 2, 1)).reshape(-1, LANES)
    T = x.shape[0]
    M = mem.shape[0]

    def rows(width, n=PROJ_ROWS):
        return pl.BlockSpec((n, width), lambda i: (i, 0))

    def whole(shape):
        return pl.BlockSpec(shape, lambda i: (0,) * len(shape))

    bshape = lambda w: jax.ShapeDtypeStruct((T, w), BF16)
    nproj = T // PROJ_ROWS
    stat_spec = pl.BlockSpec((1, 1, LANES), lambda i: (i, 0, 0))
    stat_shape = jax.ShapeDtypeStruct((nproj, 1, LANES), F32)
    gla_pairs, gla_chunks = GLA_HEADS // HEAD_PAIR, PROJ_ROWS // GLA_CHUNK
    pair_w = HEAD_PAIR * GLA_DV_PAD
    (gla, fq, fk, fv, fgate, mq, mg, crow, cx, qn2, kn2) = pl.pallas_call(
        _proj_kernel,
        grid=(nproj,),
        in_specs=[rows(D_MODEL), whole((1, D_MODEL)),
                  pl.BlockSpec(w_in_t.shape, lambda i: (0, 0), pipeline_mode=pl.Buffered(1)),
                  whole(w_alpha_up.shape), whole((1, GLA_HEADS * GLA_DK)),
                  whole((1, FOX_HEADS)), whole((1, GLA_DV))],
        out_specs=[rows(GLA_OUT_W),
                   rows(FOX_W), pl.BlockSpec((FOX_W, PROJ_ROWS), lambda i: (0, i)),
                   rows(FOX_W), rows(FOX_W),
                   rows(MEM_W), rows(MEM_W),
                   pl.BlockSpec((SUBLANES, PROJ_ROWS), lambda i: (0, i)),
                   pl.BlockSpec((LANES, PROJ_ROWS), lambda i: (0, i)),
                   stat_spec, pl.BlockSpec((1, 1, LANES), lambda i: (0, 0, 0))],
        out_shape=[bshape(GLA_OUT_W),
                   bshape(FOX_W), jax.ShapeDtypeStruct((FOX_W, T), BF16),
                   bshape(FOX_W), bshape(FOX_W),
                   bshape(MEM_W), bshape(MEM_W),
                   jax.ShapeDtypeStruct((SUBLANES, T), F32),
                   jax.ShapeDtypeStruct((LANES, T), BF16),
                   stat_shape, jax.ShapeDtypeStruct((1, 1, LANES), F32)],
        scratch_shapes=[
            pltpu.VMEM((SUBLANES, LANES), F32),
            pltpu.VMEM((IN_COLS_PAD, D_MODEL), BF16),
            pltpu.VMEM((SMALL_W, GLA_QK_W), F32),
            pltpu.VMEM((1, GLA_QK_W), F32),
            pltpu.VMEM((1, SMALL_W), F32),
            pltpu.VMEM((1, GLA_DV_PAD), F32),
            pltpu.VMEM((FOX_W, LANES), BF16),
            pltpu.VMEM((PROJ_ROWS, GLA_QK_W), BF16),
            pltpu.VMEM((PROJ_ROWS, GLA_QK_W), BF16),
            pltpu.VMEM((PROJ_ROWS, GLA_V_W), BF16),
            pltpu.VMEM((PROJ_ROWS, GLA_V_W), BF16),
            pltpu.VMEM((PROJ_ROWS, GLA_QK_W), F32),
            pltpu.VMEM((gla_pairs, LANES, pair_w), F32),
            pltpu.VMEM((PROJ_ROWS, GLA_V_W), BF16),
            pltpu.VMEM((gla_pairs, gla_chunks, LANES, pair_w), F32),
            pltpu.VMEM((gla_pairs, gla_chunks, LANES, LANES), F32),
            pltpu.VMEM((gla_pairs, gla_chunks, LANES, pair_w), BF16)],
        compiler_params=_params("arbitrary"),
        name="proj",
    )(x, norm_g[None, :], w_in_t, w_alpha_up, b_alpha[None, :], b_forget[None, :],
      gla_norm_g[None, :])

    mk, mv = pl.pallas_call(
        _memkv_kernel,
        out_shape=[jax.ShapeDtypeStruct((M, MEM_W), BF16)] * 2,
        compiler_params=pltpu.CompilerParams(vmem_limit_bytes=VMEM_LIMIT),
        name="memkv",
    )(mem, mem_norm_g[None, :], w_mem_kv)

    cend = crow[:FOX_HEADS, FOX_KEYS - 1::FOX_KEYS].reshape(-1)
    pair_rows = pl.BlockSpec((FOX_BLOCK, LANES), lambda p, i, *_: (i, p))
    pair_all = pl.BlockSpec((T, LANES), lambda p, i, *_: (0, p))
    fox = pl.pallas_call(
        _fox_kernel,
        grid_spec=pltpu.PrefetchScalarGridSpec(
            num_scalar_prefetch=3,
            grid=(FOX_HEADS // HEAD_PAIR, T // FOX_BLOCK),
            in_specs=[pair_rows, pl.BlockSpec((LANES, T), lambda p, i, *_: (p, 0)),
                      pl.BlockSpec((LANES, T), lambda p, i, *_: (0, 0)), pair_all, pair_rows],
            out_specs=pair_rows,
            scratch_shapes=[pltpu.VMEM((HEAD_PAIR, FOX_BLOCK, LANES), F32),
                            pltpu.VMEM((HEAD_PAIR, FOX_BLOCK, LANES), F32)]),
        out_shape=bshape(FOX_W),
        compiler_params=_params("arbitrary", "arbitrary"),
        name="fox",
    )(qn2.reshape(-1), kn2.reshape(-1), cend, fq, fk, cx, fv, fgate)

    out = pl.pallas_call(
        _out_kernel,
        grid=(T // OUT_ROWS,),
        in_specs=[rows(D_MODEL, OUT_ROWS), rows(GLA_OUT_W, OUT_ROWS), rows(FOX_W, OUT_ROWS),
                  rows(MEM_W, OUT_ROWS), rows(MEM_W, OUT_ROWS),
                  whole((M, MEM_W)), whole((M, MEM_W)),
                  pl.BlockSpec(w_out.shape, lambda i: (0, 0), pipeline_mode=pl.Buffered(1)),
                  whole((1, D_MODEL))],
        out_specs=rows(D_MODEL, OUT_ROWS),
        out_shape=jax.ShapeDtypeStruct((T, D_MODEL), F32),
        scratch_shapes=[pltpu.VMEM((D_MODEL, D_MODEL), BF16)],
        compiler_params=_params("arbitrary"),
        name="out",
    )(x, gla, fox, mq, mg, mk, mv, w_out, out_g[None, :])
    return out


def kernel(x, mem, norm_g, w_in, w_alpha_up, b_alpha, b_forget, gla_norm_g, mem_norm_g,
           w_mem_kv, w_out, final_norm_g):
    assert x.shape[0] == 1 and mem.shape[0] == 1 and norm_g.shape[0] == 1
    assert x.shape[1] % max(PROJ_ROWS, FOX_BLOCK, OUT_ROWS) == 0
    out = _layer(x[0], mem[0], norm_g[0], w_in[0], w_alpha_up[0], b_alpha[0], b_forget[0],
                 gla_norm_g[0], mem_norm_g[0], w_mem_kv[0], w_out[0], final_norm_g)
    return out[None]
```

```python
import functools

import jax
import jax.numpy as jnp
from jax import lax
from jax.experimental import pallas as pl
from jax.experimental.pallas import tpu as pltpu

F32 = jnp.float32
BF16 = jnp.bfloat16

EPS = 1e-6
LANES = 128
SUBLANES = 8

D_MODEL = 1024
GLA_HEADS, GLA_DK, GLA_DV, GLA_RANK = 4, 48, 96, 16
GLA_DK_PAD = 64
GLA_DV_PAD = LANES
GLA_GATE_NORM = 16.0
GLA_CHUNK = 64
FOX_HEADS, FOX_DH = 6, 64
MEM_HEADS, MEM_DH = 4, 64
HEAD_PAIR = 2
GLA_QK_W = GLA_HEADS * GLA_DK_PAD
GLA_V_W = GLA_HEADS * GLA_DV_PAD
GLA_OUT_W = GLA_HEADS * GLA_DV
FOX_W = FOX_HEADS * FOX_DH
MEM_W = MEM_HEADS * MEM_DH
SMALL_W = LANES
FG_LANE0 = 0
LR_LANE0 = SUBLANES

_GROUPS = (("gq", GLA_QK_W), ("gk", GLA_QK_W), ("gv", GLA_V_W), ("gg", GLA_V_W),
           ("fq", FOX_W), ("fk", FOX_W), ("fv", FOX_W), ("fgate", FOX_W),
           ("mq", MEM_W), ("mg", MEM_W), ("small", SMALL_W))
_OFF = {}
_o = 0
for _n, _w in _GROUPS:
    _OFF[_n] = (_o, _o + _w)
    _o += _w
IN_COLS_PAD = _o

PROJ_ROWS = 1024
FOX_BLOCK = 2048
FOX_KEYS = 512
OUT_ROWS = 1024
VMEM_LIMIT = 56 * 1024 * 1024

NEG_BIG = -1e30
FOX_SKIP_NATS = 105.0
NORM_SLACK = 1.02
FOX_DIRECT_NORM2 = 3600.0
CX_HI, CX_MID, CX_LO, CX_ONE = 0, 8, 16, 24


def _log_sigmoid(z):
    return jnp.minimum(z, 0.0) - jnp.log(1.0 + jnp.exp(-jnp.abs(z)))


def _silu(z):
    return z / (1.0 + jnp.exp(-z))


def _rms_scale(v, width):
    return lax.rsqrt(jnp.sum(v * v, axis=-1, keepdims=True) * (1.0 / width) + EPS)


def _w_in_segments():
    qk, gw = GLA_HEADS * GLA_DK, GLA_HEADS * GLA_DV
    src = {}
    o = 0
    for name, width in (("gq", qk), ("gk", qk), ("gv", gw), ("lr", GLA_RANK), ("gg", gw),
                        ("fq", FOX_W), ("fk", FOX_W), ("fv", FOX_W), ("fg", FOX_HEADS),
                        ("fgate", FOX_W), ("mq", MEM_W), ("mg", MEM_W)):
        src[name] = o
        o += width
    segs = []
    for name, d, d_pad in (("gq", GLA_DK, GLA_DK_PAD), ("gk", GLA_DK, GLA_DK_PAD),
                           ("gv", GLA_DV, GLA_DV_PAD), ("gg", GLA_DV, GLA_DV_PAD)):
        segs += [(src[name] + h * d, _OFF[name][0] + h * d_pad, d) for h in range(GLA_HEADS)]
    segs += [(src[name], _OFF[name][0], _OFF[name][1] - _OFF[name][0])
             for name in ("fq", "fk", "fv", "fgate", "mq", "mg")]
    segs += [(src["fg"], _OFF["small"][0] + FG_LANE0, FOX_HEADS),
             (src["lr"], _OFF["small"][0] + LR_LANE0, GLA_RANK)]
    return tuple(segs)


def _proj_kernel(x_ref, g_ref, w_in_ref, w_alpha_ref, b_alpha_ref, b_forget_ref, gla_g_ref,
                 gla_ref, fq_ref, fk_ref, fv_ref, fgate_ref,
                 mq_ref, mg_ref, crow_ref, cx_ref, cq_ref, qn2_ref, kn2_ref,
                 carry_ref, wt_ref, wa_ref, ba_ref, bf_ref, ng_ref, seg_ref,
                 gq_ref, gk_ref, gv_ref, gg_ref, loga_ref,
                 s_ref, lhs_ref, kv_ref, dec_ref, sprev_ref):
    rows = x_ref.shape[0]
    k_chunks = D_MODEL // LANES

    @pl.when(pl.program_id(0) == 0)
    def _():
        carry_ref[...] = jnp.zeros_like(carry_ref)
        s_ref[...] = jnp.zeros_like(s_ref)
        kn2_ref[...] = jnp.zeros_like(kn2_ref)
        wa_ref[...] = jnp.zeros_like(wa_ref)
        ba_ref[...] = jnp.zeros_like(ba_ref)
        bf_ref[...] = jnp.zeros_like(bf_ref)
        ng_ref[...] = jnp.zeros_like(ng_ref)
        for h in range(GLA_HEADS):
            src = slice(h * GLA_DK, (h + 1) * GLA_DK)
            dst = slice(h * GLA_DK_PAD, h * GLA_DK_PAD + GLA_DK)
            wa_ref[LR_LANE0:LR_LANE0 + GLA_RANK, dst] = w_alpha_ref[:, src]
            ba_ref[:, dst] = b_alpha_ref[:, src]
        bf_ref[:, FG_LANE0:FG_LANE0 + FOX_HEADS] = b_forget_ref[...]
        ng_ref[:, 0:GLA_DV] = gla_g_ref[...]
        seg_ref[...] = (lax.broadcasted_iota(jnp.int32, seg_ref.shape, 0) // FOX_DH
                        == lax.broadcasted_iota(jnp.int32, seg_ref.shape, 1)).astype(BF16)
        wt_ref[...] = jnp.zeros_like(wt_ref)
        for s0, d0, width in _w_in_segments():
            for c in range(k_chunks):
                wt_ref[d0:d0 + width, c * LANES:(c + 1) * LANES] = (
                    w_in_ref[pl.ds(s0 * k_chunks + c, width, stride=k_chunks), :].astype(BF16))

    x = x_ref[...]
    xn = (x * _rms_scale(x, D_MODEL) * g_ref[...]).astype(BF16)
    nt = (((1,), (1,)), ((), ()))

    def proj(first, last):
        lo, hi = _OFF[first][0], _OFF[last][1]
        y = lax.dot_general(xn, wt_ref[lo:hi, :], nt, preferred_element_type=F32)
        return lambda name: y[:, _OFF[name][0] - lo:_OFF[name][1] - lo]

    tail = proj("mq", "small")
    gla = proj("gq", "gg")
    small = tail("small")
    logf = _log_sigmoid(small + bf_ref[...])
    c = logf.T[0:SUBLANES, :]
    lane = lax.broadcasted_iota(jnp.int32, c.shape, 1)
    shift = 1
    while shift < rows:
        c = c + jnp.where(lane >= shift, pltpu.roll(c, shift, axis=1), 0.0)
        shift *= 2
    c = c + carry_ref[:, 0:1]
    crow_ref[...] = c
    carry_ref[...] = jnp.broadcast_to(c[:, rows - 1:rows], carry_ref.shape)
    neg = -c
    hi = neg.astype(BF16).astype(F32)
    mid = (neg - hi).astype(BF16).astype(F32)
    low = neg - hi - mid
    parts = jnp.concatenate(
        [hi, mid, low, jnp.ones_like(c), jnp.zeros((LANES - 4 * SUBLANES, rows), F32)], axis=0)
    cx_ref[...] = parts.astype(BF16)
    cq_ref[...] = parts.T.astype(BF16)

    z = (jnp.dot(small.astype(BF16), wa_ref[...].astype(BF16), preferred_element_type=F32)
         + ba_ref[...])
    loga_ref[...] = _log_sigmoid(z) * (1.0 / GLA_GATE_NORM)
    gq_ref[...] = gla("gq").astype(BF16)
    gk_ref[...] = gla("gk").astype(BF16)
    gv_ref[...] = gla("gv").astype(BF16)
    gg_ref[...] = _silu(gla("gg")).astype(BF16)
    gla_local, gla_scan, gla_output = _gla_block(
        gq_ref, gk_ref, gv_ref, loga_ref, gg_ref, ng_ref, gla_ref,
        s_ref, lhs_ref, kv_ref, dec_ref, sprev_ref)

    def max_sq_norm(v):
        v32 = v.astype(F32)
        n2 = jnp.dot((v32 * v32).astype(BF16), seg_ref[...], preferred_element_type=F32)
        return jnp.max(n2, axis=0, keepdims=True)

    gla_local()
    fox_qk = proj("fq", "fk")
    mq_ref[...] = (tail("mq") * MEM_DH ** -0.5).astype(BF16)
    mg_ref[...] = _silu(tail("mg")).astype(BF16)
    gla_scan()
    fq = (fox_qk("fq") * FOX_DH ** -0.5).astype(BF16)
    fk = fox_qk("fk").astype(BF16)
    fq_ref[...] = fq
    fk_ref[...] = fox_qk("fk").T.astype(BF16)
    gla_output()
    fox_vg = proj("fv", "fgate")
    fv_ref[...] = fox_vg("fv").astype(BF16)
    fgate_ref[...] = _silu(fox_vg("fgate")).astype(BF16)
    qn2_ref[0] = max_sq_norm(fq)
    kn2_ref[0] = jnp.maximum(kn2_ref[0], max_sq_norm(fk))


def _memkv_kernel(mem_ref, g_ref, w_ref, mk_ref, mv_ref):
    m = mem_ref[...]
    mn = (m * _rms_scale(m, D_MODEL) * g_ref[...]).astype(BF16)
    kv = jnp.dot(mn, w_ref[...].astype(BF16), preferred_element_type=F32)
    mk_ref[...] = kv[:, :MEM_W].astype(BF16)
    mv_ref[...] = kv[:, MEM_W:].astype(BF16)


def _gla_block(q_ref, k_ref, v_ref, loga_ref, gate_ref, ng_ref, o_ref,
               s_ref, lhs_ref, kv_ref, dec_ref, sprev_ref):
    C = GLA_CHUNK
    W = HEAD_PAIR * GLA_DV_PAD
    n_chunks = q_ref.shape[0] // C

    row = lax.broadcasted_iota(jnp.int32, (C, LANES), 0)
    lane = lax.broadcasted_iota(jnp.int32, (C, LANES), 1)
    lo_k = lane < GLA_DK_PAD
    causal = row >= jnp.where(lo_k, lane, lane - GLA_DK_PAD)
    lo_v = lax.broadcasted_iota(jnp.int32, (C, W), 1) < GLA_DV_PAD
    st_row = lax.broadcasted_iota(jnp.int32, (LANES, W), 0)
    st_lane = lax.broadcasted_iota(jnp.int32, (LANES, W), 1)
    own = (st_row < GLA_DK_PAD) == (st_lane < GLA_DV_PAD)
    eye = (lax.broadcasted_iota(jnp.int32, (LANES, LANES), 0)
           == lax.broadcasted_iota(jnp.int32, (LANES, LANES), 1))
    scale = GLA_DK ** -0.5
    nt = (((1,), (1,)), ((), ()))
    tn = (((0,), (0,)), ((), ()))
    ng = jnp.concatenate([ng_ref[...]] * HEAD_PAIR, axis=1)

    pairs = range(GLA_HEADS // HEAD_PAIR)

    def local(ci):
        rs = slice(ci * C, (ci + 1) * C)
        for p in pairs:
            ls = slice(p * LANES, (p + 1) * LANES)
            vs = slice(p * W, (p + 1) * W)
            b = loga_ref[rs, ls]
            shift = 1
            while shift < C:
                b = b + jnp.where(row >= shift, pltpu.roll(b, shift, axis=0), 0.0)
                shift *= 2
            b_last = b[C - 1:C, :]
            k2 = k_ref[rs, ls].astype(F32)
            qd = (q_ref[rs, ls].astype(F32) * scale * jnp.exp(b)).astype(BF16)
            kd = (k2 * jnp.exp(-b)).astype(BF16)
            ke = (k2 * jnp.exp(b_last - b)).astype(BF16)
            zk = jnp.zeros_like(kd)
            kd_blk = jnp.concatenate([jnp.where(lo_k, kd, zk), jnp.where(lo_k, zk, kd)], axis=0)
            attn = lax.dot_general(qd, kd_blk, nt, preferred_element_type=F32)
            lhs_ref[rs, vs] = jnp.concatenate([jnp.where(causal, attn, 0.0).astype(BF16), qd], axis=1)
            kv = lax.dot_general(ke, v_ref[rs, vs], tn, preferred_element_type=F32)
            kv_ref[p, ci] = jnp.where(own, kv, 0.0)
            dcol = jnp.exp(jnp.sum(jnp.where(eye, jnp.broadcast_to(b_last, (LANES, LANES)), 0.0),
                                   axis=1, keepdims=True))
            dec_ref[p, ci] = jnp.broadcast_to(dcol, (LANES, LANES))

    def scan(ci):
        for p in pairs:
            s_prev = s_ref[p]
            sprev_ref[p, ci] = s_prev.astype(BF16)
            s_ref[p] = jnp.tile(dec_ref[p, ci], (1, HEAD_PAIR)) * s_prev + kv_ref[p, ci]

    def output(ci):
        rs = slice(ci * C, (ci + 1) * C)
        for p in pairs:
            vs = slice(p * W, (p + 1) * W)
            v2 = v_ref[rs, vs]
            zv = jnp.zeros_like(v2)
            v_blk = jnp.concatenate([jnp.where(lo_v, v2, zv), jnp.where(lo_v, zv, v2)], axis=0)
            o = jnp.dot(lhs_ref[rs, vs], jnp.concatenate([v_blk, sprev_ref[p, ci]], axis=0),
                        preferred_element_type=F32)
            o2 = o * o
            ms = jnp.where(lo_v, jnp.sum(o2[:, :GLA_DV_PAD], axis=1, keepdims=True),
                           jnp.sum(o2[:, GLA_DV_PAD:], axis=1, keepdims=True))
            on = o * lax.rsqrt(ms * (1.0 / GLA_DV) + EPS) * ng
            og = (on * gate_ref[rs, vs].astype(F32)).astype(BF16)
            for hh in range(HEAD_PAIR):
                c0 = (p * HEAD_PAIR + hh) * GLA_DV
                o_ref[rs, c0:c0 + GLA_DV] = og[:, hh * GLA_DV_PAD:hh * GLA_DV_PAD + GLA_DV]

    def all_chunks(phase):
        return lambda: [phase(ci) for ci in range(n_chunks)]

    return all_chunks(local), all_chunks(scan), all_chunks(output)


def _fox_kernel(qn2_ref, kn2_ref, cend_ref, q_ref, cq_ref, k_ref, cx_ref, v_ref, gate_ref, o_ref,
                m_ref, acc_ref):
    blk = FOX_KEYS
    streams = range(q_ref.shape[0] // blk)
    pair = pl.program_id(0)
    qi = pl.program_id(1)
    nblk = pl.num_programs(1) * len(streams)
    lane = lax.broadcasted_iota(jnp.int32, (1, LANES), 1)
    lo_lanes = lane < FOX_DH
    reps = blk // LANES
    diag = [qi * len(streams) + s for s in streams]
    heads = [pair * HEAD_PAIR + hh for hh in range(HEAD_PAIR)]

    q = q_ref[...]
    zero = jnp.zeros_like(q)
    q_lo, q_hi = jnp.where(lo_lanes, q, zero), jnp.where(lo_lanes, zero, q)
    q_stack = [jnp.concatenate([q_lo[s * blk:(s + 1) * blk], q_hi[s * blk:(s + 1) * blk]], axis=0)
               for s in streams]
    g_row = lax.broadcasted_iota(jnp.int32, (LANES, LANES), 0)
    g_col = lax.broadcasted_iota(jnp.int32, (LANES, LANES), 1)
    xlane = lax.broadcasted_iota(jnp.int32, (HEAD_PAIR * blk, LANES), 1)
    cq = cq_ref[...]

    def lane_map(h, shift):
        g = jnp.where((g_row == CX_ONE) & ((g_col == CX_HI + h) | (g_col == CX_MID + h)
                                           | (g_col == CX_LO + h)), 1.0, 0.0)
        if shift:
            for part, base in enumerate((CX_HI, CX_MID, CX_LO)):
                g = jnp.where((g_row == base + h) & (g_col == CX_ONE + part), -1.0, g)
        return g.astype(BF16)

    def q_aug(s, shift, void=None):
        cs = cq[s * blk:(s + 1) * blk]
        extra = jnp.concatenate([jnp.dot(cs, lane_map(h, shift), preferred_element_type=F32)
                                 for h in heads], axis=0)
        if void is not None:
            extra = jnp.where(jnp.logical_and(xlane == CX_ONE + 3, void), NEG_BIG, extra)
        return jnp.concatenate([q_stack[s], extra.astype(BF16)], axis=1)

    acc_ref[...] = jnp.zeros_like(acc_ref)
    qpos = lax.broadcasted_iota(jnp.int32, (blk, blk), 0)
    kpos = lax.broadcasted_iota(jnp.int32, (blk, blk), 1)

    def block(j):
        ks = pl.ds(pl.multiple_of(jnp.maximum(j, 0) * blk, blk), blk)
        k_aug = jnp.concatenate([k_ref[:, ks], cx_ref[:, ks]], axis=0)
        vb = v_ref[ks, :]
        one = jnp.ones_like(vb)
        return k_aug, (jnp.where(lo_lanes, vb, one), jnp.where(lo_lanes, one, vb))

    def step_online(s, j, qa, masked):
        rows = slice(s * blk, (s + 1) * blk)
        k_aug, vaug = block(j)
        s_all = jnp.dot(qa, k_aug, preferred_element_type=F32)
        for hh in range(HEAD_PAIR):
            sc = s_all[hh * blk:(hh + 1) * blk]
            if masked:
                sc = jnp.where(kpos <= qpos, sc, NEG_BIG)
            m_prev = m_ref[hh, rows]
            m_new = jnp.maximum(m_prev, jnp.max(sc, axis=1, keepdims=True))
            p = jnp.exp(sc - jnp.tile(m_new, (1, reps)))
            alpha = jnp.exp(m_prev - m_new)
            pv = jnp.dot(p.astype(BF16), vaug[hh], preferred_element_type=F32)
            acc_ref[hh, rows] = alpha * acc_ref[hh, rows] + pv
            m_ref[hh, rows] = m_new

    def step_direct(s, j, qa, masked):
        rows = slice(s * blk, (s + 1) * blk)
        k_aug, vaug = block(j)
        s_all = jnp.dot(qa, k_aug, preferred_element_type=F32)
        for hh in range(HEAD_PAIR):
            sc = s_all[hh * blk:(hh + 1) * blk]
            if masked:
                sc = jnp.where(kpos <= qpos, sc, NEG_BIG)
            acc_ref[hh, rows] += jnp.dot(jnp.exp(sc).astype(BF16), vaug[hh],
                                         preferred_element_type=F32)

    k_max2 = [kn2_ref[h] for h in heads]

    def norm2(s, hh):
        stat = (diag[s] * blk // PROJ_ROWS) * LANES
        return (NORM_SLACK * NORM_SLACK) * qn2_ref[stat + heads[hh]] * k_max2[hh]

    def live(t):
        keep = False
        for s in streams:
            j = diag[s] - 1 - t
            jj = jnp.maximum(j, 0)
            prev = jnp.maximum(diag[s] - 1, 0)
            for hh in range(HEAD_PAIR):
                base = heads[hh] * nblk
                gap = FOX_SKIP_NATS + cend_ref[base + prev] - cend_ref[base + jj]
                dead = jnp.logical_and(gap <= 0.0, norm2(s, hh) <= gap * gap)
                keep = jnp.logical_or(keep, jnp.logical_and(j >= 0, jnp.logical_not(dead)))
        return keep

    def sweeps(stepper):
        def sweep(back, masked=False):
            for s in streams:
                stepper(s, diag[s] - back, masked)

        def body(t):
            sweep(t + 1)
            return t + 1

        sweep(0, masked=True)
        sweep(1)
        lax.while_loop(live, body, 1)

    direct_ok = True
    for s in streams:
        for hh in range(HEAD_PAIR):
            direct_ok = jnp.logical_and(direct_ok, norm2(s, hh) <= FOX_DIRECT_NORM2)

    @pl.when(direct_ok)
    def _():
        q_dir = [q_aug(s, shift=True) for s in streams]
        q_void = [q_aug(s, shift=True, void=True) for s in streams]
        sweeps(lambda s, j, masked: step_direct(
            s, j, q_dir[s] if masked else jnp.where(j >= 0, q_dir[s], q_void[s]), masked))

    @pl.when(jnp.logical_not(direct_ok))
    def _():
        m_ref[...] = jnp.full_like(m_ref, NEG_BIG)
        sweeps(lambda s, j, masked: step_online(s, j, q_aug(s, shift=False, void=j < 0), masked))

    outs = []
    for hh in range(HEAD_PAIR):
        acc = acc_ref[hh]
        outs.append(acc / pltpu.roll(acc, FOX_DH, axis=1))
    o = jnp.where(lo_lanes, outs[0], outs[1])
    o_ref[...] = (o * gate_ref[...].astype(F32)).astype(BF16)


def _out_kernel(x_ref, gla_ref, fox_ref, mq_ref, mg_ref, mk_ref, mv_ref,
                w_out_ref, fg_ref, o_ref, wo_ref):
    @pl.when(pl.program_id(0) == 0)
    def _():
        wo_ref[...] = w_out_ref[...].astype(BF16)

    lane = lax.broadcasted_iota(jnp.int32, (1, LANES), 1)
    lo_lanes = lane < MEM_DH
    nt = (((1,), (1,)), ((), ()))
    mem_parts = []
    for p in range(MEM_HEADS // HEAD_PAIR):
        ls = slice(p * LANES, (p + 1) * LANES)
        q = mq_ref[:, ls]
        kb = mk_ref[:, ls]
        vb = mv_ref[:, ls]
        zero = jnp.zeros_like(q)
        one = jnp.ones_like(vb)
        qh = (jnp.where(lo_lanes, q, zero), jnp.where(lo_lanes, zero, q))
        vaug = (jnp.where(lo_lanes, vb, one), jnp.where(lo_lanes, one, vb))
        outs = []
        for hh in range(HEAD_PAIR):
            s = lax.dot_general(qh[hh], kb, nt, preferred_element_type=F32)
            pexp = jnp.exp(s - jnp.max(s, axis=1, keepdims=True))
            pv = jnp.dot(pexp.astype(BF16), vaug[hh], preferred_element_type=F32)
            outs.append(pv / pltpu.roll(pv, MEM_DH, axis=1))
        o = jnp.where(lo_lanes, outs[0], outs[1])
        mem_parts.append((o * mg_ref[:, ls].astype(F32)).astype(BF16))
    mixed = jnp.concatenate([gla_ref[...], fox_ref[...]] + mem_parts, axis=1)
    y = x_ref[...] + jnp.dot(mixed, wo_ref[...], preferred_element_type=F32)
    o_ref[...] = y * _rms_scale(y, D_MODEL) * fg_ref[...]


def _params(*sem):
    return pltpu.CompilerParams(dimension_semantics=sem, vmem_limit_bytes=VMEM_LIMIT)


def _layer(x, mem, norm_g, w_in, w_alpha_up, b_alpha, b_forget, gla_norm_g,
           mem_norm_g, w_mem_kv, w_out, out_g):
    w_in_t = jnp.transpose(w_in[None], (0, 2, 1)).reshape(-1, LANES)
    T = x.shape[0]
    M = mem.shape[0]

    def rows(width, n=PROJ_ROWS):
        return pl.BlockSpec((n, width), lambda i: (i, 0))

    def whole(shape):
        return pl.BlockSpec(shape, lambda i: (0,) * len(shape))

    bshape = lambda w: jax.ShapeDtypeStruct((T, w), BF16)
    nproj = T // PROJ_ROWS
    stat_spec = pl.BlockSpec((1, 1, LANES), lambda i: (i, 0, 0))
    stat_shape = jax.ShapeDtypeStruct((nproj, 1, LANES), F32)
    gla_pairs, gla_chunks = GLA_HEADS // HEAD_PAIR, PROJ_ROWS // GLA_CHUNK
    pair_w = HEAD_PAIR * GLA_DV_PAD
    (gla, fq, fk, fv, fgate, mq, mg, crow, cx, cq, qn2, kn2) = pl.pallas_call(
        _proj_kernel,
        grid=(nproj,),
        in_specs=[rows(D_MODEL), whole((1, D_MODEL)),
                  pl.BlockSpec(w_in_t.shape, lambda i: (0, 0), pipeline_mode=pl.Buffered(1)),
                  whole(w_alpha_up.shape), whole((1, GLA_HEADS * GLA_DK)),
                  whole((1, FOX_HEADS)), whole((1, GLA_DV))],
        out_specs=[rows(GLA_OUT_W),
                   rows(FOX_W), pl.BlockSpec((FOX_W, PROJ_ROWS), lambda i: (0, i)),
                   rows(FOX_W), rows(FOX_W),
                   rows(MEM_W), rows(MEM_W),
                   pl.BlockSpec((SUBLANES, PROJ_ROWS), lambda i: (0, i)),
                   pl.BlockSpec((LANES, PROJ_ROWS), lambda i: (0, i)), rows(LANES),
                   stat_spec, pl.BlockSpec((1, 1, LANES), lambda i: (0, 0, 0))],
        out_shape=[bshape(GLA_OUT_W),
                   bshape(FOX_W), jax.ShapeDtypeStruct((FOX_W, T), BF16),
                   bshape(FOX_W), bshape(FOX_W),
                   bshape(MEM_W), bshape(MEM_W),
                   jax.ShapeDtypeStruct((SUBLANES, T), F32),
                   jax.ShapeDtypeStruct((LANES, T), BF16), bshape(LANES),
                   stat_shape, jax.ShapeDtypeStruct((1, 1, LANES), F32)],
        scratch_shapes=[
            pltpu.VMEM((SUBLANES, LANES), F32),
            pltpu.VMEM((IN_COLS_PAD, D_MODEL), BF16),
            pltpu.VMEM((SMALL_W, GLA_QK_W), F32),
            pltpu.VMEM((1, GLA_QK_W), F32),
            pltpu.VMEM((1, SMALL_W), F32),
            pltpu.VMEM((1, GLA_DV_PAD), F32),
            pltpu.VMEM((FOX_W, LANES), BF16),
            pltpu.VMEM((PROJ_ROWS, GLA_QK_W), BF16),
            pltpu.VMEM((PROJ_ROWS, GLA_QK_W), BF16),
            pltpu.VMEM((PROJ_ROWS, GLA_V_W), BF16),
            pltpu.VMEM((PROJ_ROWS, GLA_V_W), BF16),
            pltpu.VMEM((PROJ_ROWS, GLA_QK_W), F32),
            pltpu.VMEM((gla_pairs, LANES, pair_w), F32),
            pltpu.VMEM((PROJ_ROWS, GLA_V_W), BF16),
            pltpu.VMEM((gla_pairs, gla_chunks, LANES, pair_w), F32),
            pltpu.VMEM((gla_pairs, gla_chunks, LANES, LANES), F32),
            pltpu.VMEM((gla_pairs, gla_chunks, LANES, pair_w), BF16)],
        compiler_params=_params("arbitrary"),
        name="proj",
    )(x, norm_g[None, :], w_in_t, w_alpha_up, b_alpha[None, :], b_forget[None, :],
      gla_norm_g[None, :])

    mk, mv = pl.pallas_call(
        _memkv_kernel,
        out_shape=[jax.ShapeDtypeStruct((M, MEM_W), BF16)] * 2,
        compiler_params=pltpu.CompilerParams(vmem_limit_bytes=VMEM_LIMIT),
        name="memkv",
    )(mem, mem_norm_g[None, :], w_mem_kv)

    cend = crow[:FOX_HEADS, FOX_KEYS - 1::FOX_KEYS].reshape(-1)
    pair_rows = pl.BlockSpec((FOX_BLOCK, LANES), lambda p, i, *_: (i, p))
    pair_all = pl.BlockSpec((T, LANES), lambda p, i, *_: (0, p))
    fox = pl.pallas_call(
        _fox_kernel,
        grid_spec=pltpu.PrefetchScalarGridSpec(
            num_scalar_prefetch=3,
            grid=(FOX_HEADS // HEAD_PAIR, T // FOX_BLOCK),
            in_specs=[pair_rows, pl.BlockSpec((FOX_BLOCK, LANES), lambda p, i, *_: (i, 0)),
                      pl.BlockSpec((LANES, T), lambda p, i, *_: (p, 0)),
                      pl.BlockSpec((LANES, T), lambda p, i, *_: (0, 0)), pair_all, pair_rows],
            out_specs=pair_rows,
            scratch_shapes=[pltpu.VMEM((HEAD_PAIR, FOX_BLOCK, LANES), F32),
                            pltpu.VMEM((HEAD_PAIR, FOX_BLOCK, LANES), F32)]),
        out_shape=bshape(FOX_W),
        compiler_params=_params("arbitrary", "arbitrary"),
        name="fox",
    )(qn2.reshape(-1), kn2.reshape(-1), cend, fq, cq, fk, cx, fv, fgate)

    out = pl.pallas_call(
        _out_kernel,
        grid=(T // OUT_ROWS,),
        in_specs=[rows(D_MODEL, OUT_ROWS), rows(GLA_OUT_W, OUT_ROWS), rows(FOX_W, OUT_ROWS),
                  rows(MEM_W, OUT_ROWS), rows(MEM_W, OUT_ROWS),
                  whole((M, MEM_W)), whole((M, MEM_W)),
                  pl.BlockSpec(w_out.shape, lambda i: (0, 0), pipeline_mode=pl.Buffered(1)),
                  whole((1, D_MODEL))],
        out_specs=rows(D_MODEL, OUT_ROWS),
        out_shape=jax.ShapeDtypeStruct((T, D_MODEL), F32),
        scratch_shapes=[pltpu.VMEM((D_MODEL, D_MODEL), BF16)],
        compiler_params=_params("arbitrary"),
        name="out",
    )(x, gla, fox, mq, mg, mk, mv, w_out, out_g[None, :])
    return out


def kernel(x, mem, norm_g, w_in, w_alpha_up, b_alpha, b_forget, gla_norm_g, mem_norm_g,
           w_mem_kv, w_out, final_norm_g):
    assert x.shape[0] == 1 and mem.shape[0] == 1 and norm_g.shape[0] == 1
    assert x.shape[1] % max(PROJ_ROWS, FOX_BLOCK, OUT_ROWS) == 0
    out = _layer(x[0], mem[0], norm_g[0], w_in[0], w_alpha_up[0], b_alpha[0], b_forget[0],
                 gla_norm_g[0], mem_norm_g[0], w_mem_kv[0], w_out[0], final_norm_g)
    return out[None]
```

```python
import functools

import jax
import jax.numpy as jnp
from jax import lax
from jax.experimental import pallas as pl
from jax.experimental.pallas import tpu as pltpu

F32 = jnp.float32
BF16 = jnp.bfloat16

EPS = 1e-6
LANES = 128
SUBLANES = 8

D_MODEL = 1024
GLA_HEADS, GLA_DK, GLA_DV, GLA_RANK = 4, 48, 96, 16
GLA_DK_PAD = 64
GLA_DV_PAD = LANES
GLA_GATE_NORM = 16.0
GLA_CHUNK = 64
FOX_HEADS, FOX_DH = 6, 64
MEM_HEADS, MEM_DH = 4, 64
HEAD_PAIR = 2
GLA_QK_W = GLA_HEADS * GLA_DK_PAD
GLA_V_W = GLA_HEADS * GLA_DV_PAD
GLA_OUT_W = GLA_HEADS * GLA_DV
FOX_W = FOX_HEADS * FOX_DH
MEM_W = MEM_HEADS * MEM_DH
SMALL_W = LANES
FG_LANE0 = 0
LR_LANE0 = SUBLANES

_GROUPS = (("gq", GLA_QK_W), ("gk", GLA_QK_W), ("gv", GLA_V_W), ("gg", GLA_V_W),
           ("fq", FOX_W), ("fk", FOX_W), ("fv", FOX_W), ("fgate", FOX_W),
           ("mq", MEM_W), ("mg", MEM_W), ("small", SMALL_W))
_OFF = {}
_o = 0
for _n, _w in _GROUPS:
    _OFF[_n] = (_o, _o + _w)
    _o += _w
IN_COLS_PAD = _o

PROJ_ROWS = 1024
FOX_BLOCK = 2048
FOX_KEYS = 256
OUT_ROWS = 1024
VMEM_LIMIT = 56 * 1024 * 1024

NEG_BIG = -1e30
FOX_SKIP_NATS = 105.0
NORM_SLACK = 1.02
FOX_DIRECT_NORM2 = 3600.0
CX_HI, CX_MID, CX_LO, CX_ONE = 0, 8, 16, 24


def _log_sigmoid(z):
    return jnp.minimum(z, 0.0) - jnp.log(1.0 + jnp.exp(-jnp.abs(z)))


def _silu(z):
    return z / (1.0 + jnp.exp(-z))


def _rms_scale(v, width):
    return lax.rsqrt(jnp.sum(v * v, axis=-1, keepdims=True) * (1.0 / width) + EPS)


def _w_in_segments():
    qk, gw = GLA_HEADS * GLA_DK, GLA_HEADS * GLA_DV
    src = {}
    o = 0
    for name, width in (("gq", qk), ("gk", qk), ("gv", gw), ("lr", GLA_RANK), ("gg", gw),
                        ("fq", FOX_W), ("fk", FOX_W), ("fv", FOX_W), ("fg", FOX_HEADS),
                        ("fgate", FOX_W), ("mq", MEM_W), ("mg", MEM_W)):
        src[name] = o
        o += width
    segs = []
    for name, d, d_pad in (("gq", GLA_DK, GLA_DK_PAD), ("gk", GLA_DK, GLA_DK_PAD),
                           ("gv", GLA_DV, GLA_DV_PAD), ("gg", GLA_DV, GLA_DV_PAD)):
        segs += [(src[name] + h * d, _OFF[name][0] + h * d_pad, d) for h in range(GLA_HEADS)]
    segs += [(src[name], _OFF[name][0], _OFF[name][1] - _OFF[name][0])
             for name in ("fq", "fk", "fv", "fgate", "mq", "mg")]
    segs += [(src["fg"], _OFF["small"][0] + FG_LANE0, FOX_HEADS),
             (src["lr"], _OFF["small"][0] + LR_LANE0, GLA_RANK)]
    return tuple(segs)


def _proj_kernel(x_ref, g_ref, w_in_ref, w_alpha_ref, b_alpha_ref, b_forget_ref, gla_g_ref,
                 gla_ref, fq_ref, fk_ref, fv_ref, fgate_ref,
                 mq_ref, mg_ref, crow_ref, cx_ref, cq_ref, qn2_ref, kn2_ref,
                 carry_ref, wt_ref, wa_ref, ba_ref, bf_ref, ng_ref, seg_ref,
                 gq_ref, gk_ref, gv_ref, gg_ref, loga_ref,
                 s_ref, lhs_ref, kv_ref, dec_ref, sprev_ref):
    rows = x_ref.shape[0]
    k_chunks = D_MODEL // LANES

    @pl.when(pl.program_id(0) == 0)
    def _():
        carry_ref[...] = jnp.zeros_like(carry_ref)
        s_ref[...] = jnp.zeros_like(s_ref)
        kn2_ref[...] = jnp.zeros_like(kn2_ref)
        wa_ref[...] = jnp.zeros_like(wa_ref)
        ba_ref[...] = jnp.zeros_like(ba_ref)
        bf_ref[...] = jnp.zeros_like(bf_ref)
        ng_ref[...] = jnp.zeros_like(ng_ref)
        for h in range(GLA_HEADS):
            src = slice(h * GLA_DK, (h + 1) * GLA_DK)
            dst = slice(h * GLA_DK_PAD, h * GLA_DK_PAD + GLA_DK)
            wa_ref[LR_LANE0:LR_LANE0 + GLA_RANK, dst] = w_alpha_ref[:, src]
            ba_ref[:, dst] = b_alpha_ref[:, src]
        bf_ref[:, FG_LANE0:FG_LANE0 + FOX_HEADS] = b_forget_ref[...]
        ng_ref[:, 0:GLA_DV] = gla_g_ref[...]
        seg_ref[...] = (lax.broadcasted_iota(jnp.int32, seg_ref.shape, 0) // FOX_DH
                        == lax.broadcasted_iota(jnp.int32, seg_ref.shape, 1)).astype(BF16)
        wt_ref[...] = jnp.zeros_like(wt_ref)
        for s0, d0, width in _w_in_segments():
            for c in range(k_chunks):
                wt_ref[d0:d0 + width, c * LANES:(c + 1) * LANES] = (
                    w_in_ref[pl.ds(s0 * k_chunks + c, width, stride=k_chunks), :].astype(BF16))

    x = x_ref[...]
    xn = (x * _rms_scale(x, D_MODEL) * g_ref[...]).astype(BF16)
    nt = (((1,), (1,)), ((), ()))

    def proj(first, last):
        lo, hi = _OFF[first][0], _OFF[last][1]
        y = lax.dot_general(xn, wt_ref[lo:hi, :], nt, preferred_element_type=F32)
        return lambda name: y[:, _OFF[name][0] - lo:_OFF[name][1] - lo]

    tail = proj("mq", "small")
    gla = proj("gq", "gg")
    small = tail("small")
    logf = _log_sigmoid(small + bf_ref[...])
    c = logf.T[0:SUBLANES, :]
    lane = lax.broadcasted_iota(jnp.int32, c.shape, 1)
    shift = 1
    while shift < rows:
        c = c + jnp.where(lane >= shift, pltpu.roll(c, shift, axis=1), 0.0)
        shift *= 2
    c = c + carry_ref[:, 0:1]
    crow_ref[...] = c
    carry_ref[...] = jnp.broadcast_to(c[:, rows - 1:rows], carry_ref.shape)
    neg = -c
    hi = neg.astype(BF16).astype(F32)
    mid = (neg - hi).astype(BF16).astype(F32)
    low = neg - hi - mid
    parts = jnp.concatenate(
        [hi, mid, low, jnp.ones_like(c), jnp.zeros((LANES - 4 * SUBLANES, rows), F32)], axis=0)
    cx_ref[...] = parts.astype(BF16)
    cq_ref[...] = parts.T.astype(BF16)

    z = (jnp.dot(small.astype(BF16), wa_ref[...].astype(BF16), preferred_element_type=F32)
         + ba_ref[...])
    loga_ref[...] = _log_sigmoid(z) * (1.0 / GLA_GATE_NORM)
    gq_ref[...] = gla("gq").astype(BF16)
    gk_ref[...] = gla("gk").astype(BF16)
    gv_ref[...] = gla("gv").astype(BF16)
    gg_ref[...] = _silu(gla("gg")).astype(BF16)
    gla_local, gla_scan, gla_output = _gla_block(
        gq_ref, gk_ref, gv_ref, loga_ref, gg_ref, ng_ref, gla_ref,
        s_ref, lhs_ref, kv_ref, dec_ref, sprev_ref)

    def max_sq_norm(v):
        v32 = v.astype(F32)
        n2 = jnp.dot((v32 * v32).astype(BF16), seg_ref[...], preferred_element_type=F32)
        return jnp.max(n2, axis=0, keepdims=True)

    gla_local()
    fox_qk = proj("fq", "fk")
    mq_ref[...] = (tail("mq") * MEM_DH ** -0.5).astype(BF16)
    mg_ref[...] = _silu(tail("mg")).astype(BF16)
    gla_scan()
    fq = (fox_qk("fq") * FOX_DH ** -0.5).astype(BF16)
    fk = fox_qk("fk").astype(BF16)
    fq_ref[...] = fq
    fk_ref[...] = fox_qk("fk").T.astype(BF16)
    gla_output()
    fox_vg = proj("fv", "fgate")
    fv_ref[...] = fox_vg("fv").astype(BF16)
    fgate_ref[...] = _silu(fox_vg("fgate")).astype(BF16)
    qn2_ref[0] = max_sq_norm(fq)
    kn2_ref[0] = jnp.maximum(kn2_ref[0], max_sq_norm(fk))


def _memkv_kernel(mem_ref, g_ref, w_ref, mk_ref, mv_ref):
    m = mem_ref[...]
    mn = (m * _rms_scale(m, D_MODEL) * g_ref[...]).astype(BF16)
    kv = jnp.dot(mn, w_ref[...].astype(BF16), preferred_element_type=F32)
    mk_ref[...] = kv[:, :MEM_W].astype(BF16)
    mv_ref[...] = kv[:, MEM_W:].astype(BF16)


def _gla_block(q_ref, k_ref, v_ref, loga_ref, gate_ref, ng_ref, o_ref,
               s_ref, lhs_ref, kv_ref, dec_ref, sprev_ref):
    C = GLA_CHUNK
    W = HEAD_PAIR * GLA_DV_PAD
    n_chunks = q_ref.shape[0] // C

    row = lax.broadcasted_iota(jnp.int32, (C, LANES), 0)
    lane = lax.broadcasted_iota(jnp.int32, (C, LANES), 1)
    lo_k = lane < GLA_DK_PAD
    causal = row >= jnp.where(lo_k, lane, lane - GLA_DK_PAD)
    lo_v = lax.broadcasted_iota(jnp.int32, (C, W), 1) < GLA_DV_PAD
    st_row = lax.broadcasted_iota(jnp.int32, (LANES, W), 0)
    st_lane = lax.broadcasted_iota(jnp.int32, (LANES, W), 1)
    own = (st_row < GLA_DK_PAD) == (st_lane < GLA_DV_PAD)
    eye = (lax.broadcasted_iota(jnp.int32, (LANES, LANES), 0)
           == lax.broadcasted_iota(jnp.int32, (LANES, LANES), 1))
    scale = GLA_DK ** -0.5
    nt = (((1,), (1,)), ((), ()))
    tn = (((0,), (0,)), ((), ()))
    ng = jnp.concatenate([ng_ref[...]] * HEAD_PAIR, axis=1)

    pairs = range(GLA_HEADS // HEAD_PAIR)

    def local(ci):
        rs = slice(ci * C, (ci + 1) * C)
        for p in pairs:
            ls = slice(p * LANES, (p + 1) * LANES)
            vs = slice(p * W, (p + 1) * W)
            b = loga_ref[rs, ls]
            shift = 1
            while shift < C:
                b = b + jnp.where(row >= shift, pltpu.roll(b, shift, axis=0), 0.0)
                shift *= 2
            b_last = b[C - 1:C, :]
            k2 = k_ref[rs, ls].astype(F32)
            qd = (q_ref[rs, ls].astype(F32) * scale * jnp.exp(b)).astype(BF16)
            kd = (k2 * jnp.exp(-b)).astype(BF16)
            ke = (k2 * jnp.exp(b_last - b)).astype(BF16)
            zk = jnp.zeros_like(kd)
            kd_blk = jnp.concatenate([jnp.where(lo_k, kd, zk), jnp.where(lo_k, zk, kd)], axis=0)
            attn = lax.dot_general(qd, kd_blk, nt, preferred_element_type=F32)
            lhs_ref[rs, vs] = jnp.concatenate([jnp.where(causal, attn, 0.0).astype(BF16), qd], axis=1)
            kv = lax.dot_general(ke, v_ref[rs, vs], tn, preferred_element_type=F32)
            kv_ref[p, ci] = jnp.where(own, kv, 0.0)
            dcol = jnp.exp(jnp.sum(jnp.where(eye, jnp.broadcast_to(b_last, (LANES, LANES)), 0.0),
                                   axis=1, keepdims=True))
            dec_ref[p, ci] = jnp.broadcast_to(dcol, (LANES, LANES))

    def scan(ci):
        for p in pairs:
            s_prev = s_ref[p]
            sprev_ref[p, ci] = s_prev.astype(BF16)
            s_ref[p] = jnp.tile(dec_ref[p, ci], (1, HEAD_PAIR)) * s_prev + kv_ref[p, ci]

    def output(ci):
        rs = slice(ci * C, (ci + 1) * C)
        for p in pairs:
            vs = slice(p * W, (p + 1) * W)
            v2 = v_ref[rs, vs]
            zv = jnp.zeros_like(v2)
            v_blk = jnp.concatenate([jnp.where(lo_v, v2, zv), jnp.where(lo_v, zv, v2)], axis=0)
            o = jnp.dot(lhs_ref[rs, vs], jnp.concatenate([v_blk, sprev_ref[p, ci]], axis=0),
                        preferred_element_type=F32)
            o2 = o * o
            ms = jnp.where(lo_v, jnp.sum(o2[:, :GLA_DV_PAD], axis=1, keepdims=True),
                           jnp.sum(o2[:, GLA_DV_PAD:], axis=1, keepdims=True))
            on = o * lax.rsqrt(ms * (1.0 / GLA_DV) + EPS) * ng
            og = (on * gate_ref[rs, vs].astype(F32)).astype(BF16)
            for hh in range(HEAD_PAIR):
                c0 = (p * HEAD_PAIR + hh) * GLA_DV
                o_ref[rs, c0:c0 + GLA_DV] = og[:, hh * GLA_DV_PAD:hh * GLA_DV_PAD + GLA_DV]

    def all_chunks(phase):
        return lambda: [phase(ci) for ci in range(n_chunks)]

    return all_chunks(local), all_chunks(scan), all_chunks(output)


def _fox_kernel(qn2_ref, kn2_ref, cend_ref, q_ref, cq_ref, k_ref, cx_ref, v_ref, gate_ref, o_ref,
                m_ref, acc_ref):
    blk = FOX_KEYS
    streams = range(q_ref.shape[0] // blk)
    pair = pl.program_id(0)
    qi = pl.program_id(1)
    nblk = pl.num_programs(1) * len(streams)
    lane = lax.broadcasted_iota(jnp.int32, (1, LANES), 1)
    lo_lanes = lane < FOX_DH
    reps = blk // LANES
    diag = [qi * len(streams) + s for s in streams]
    heads = [pair * HEAD_PAIR + hh for hh in range(HEAD_PAIR)]

    q = q_ref[...]
    zero = jnp.zeros_like(q)
    q_lo, q_hi = jnp.where(lo_lanes, q, zero), jnp.where(lo_lanes, zero, q)
    q_stack = [jnp.concatenate([q_lo[s * blk:(s + 1) * blk], q_hi[s * blk:(s + 1) * blk]], axis=0)
               for s in streams]
    g_row = lax.broadcasted_iota(jnp.int32, (LANES, LANES), 0)
    g_col = lax.broadcasted_iota(jnp.int32, (LANES, LANES), 1)
    xlane = lax.broadcasted_iota(jnp.int32, (HEAD_PAIR * blk, LANES), 1)
    cq = cq_ref[...]

    def lane_map(h, shift):
        g = jnp.where((g_row == CX_ONE) & ((g_col == CX_HI + h) | (g_col == CX_MID + h)
                                           | (g_col == CX_LO + h)), 1.0, 0.0)
        if shift:
            for part, base in enumerate((CX_HI, CX_MID, CX_LO)):
                g = jnp.where((g_row == base + h) & (g_col == CX_ONE + part), -1.0, g)
        return g.astype(BF16)

    def q_aug(s, shift, void=None):
        cs = cq[s * blk:(s + 1) * blk]
        extra = jnp.concatenate([jnp.dot(cs, lane_map(h, shift), preferred_element_type=F32)
                                 for h in heads], axis=0)
        if void is not None:
            extra = jnp.where(jnp.logical_and(xlane == CX_ONE + 3, void), NEG_BIG, extra)
        return jnp.concatenate([q_stack[s], extra.astype(BF16)], axis=1)

    acc_ref[...] = jnp.zeros_like(acc_ref)
    qpos = lax.broadcasted_iota(jnp.int32, (blk, blk), 0)
    kpos = lax.broadcasted_iota(jnp.int32, (blk, blk), 1)

    def block(j):
        ks = pl.ds(pl.multiple_of(jnp.maximum(j, 0) * blk, blk), blk)
        k_aug = jnp.concatenate([k_ref[:, ks], cx_ref[:, ks]], axis=0)
        vb = v_ref[ks, :]
        one = jnp.ones_like(vb)
        return k_aug, (jnp.where(lo_lanes, vb, one), jnp.where(lo_lanes, one, vb))

    def step_online(s, j, qa, masked):
        rows = slice(s * blk, (s + 1) * blk)
        k_aug, vaug = block(j)
        s_all = jnp.dot(qa, k_aug, preferred_element_type=F32)
        for hh in range(HEAD_PAIR):
            sc = s_all[hh * blk:(hh + 1) * blk]
            if masked:
                sc = jnp.where(kpos <= qpos, sc, NEG_BIG)
            m_prev = m_ref[hh, rows]
            m_new = jnp.maximum(m_prev, jnp.max(sc, axis=1, keepdims=True))
            p = jnp.exp(sc - jnp.tile(m_new, (1, reps)))
            alpha = jnp.exp(m_prev - m_new)
            pv = jnp.dot(p.astype(BF16), vaug[hh], preferred_element_type=F32)
            acc_ref[hh, rows] = alpha * acc_ref[hh, rows] + pv
            m_ref[hh, rows] = m_new

    def step_direct(s, j, qa, masked):
        rows = slice(s * blk, (s + 1) * blk)
        k_aug, vaug = block(j)
        s_all = jnp.dot(qa, k_aug, preferred_element_type=F32)
        for hh in range(HEAD_PAIR):
            sc = s_all[hh * blk:(hh + 1) * blk]
            if masked:
                sc = jnp.where(kpos <= qpos, sc, NEG_BIG)
            acc_ref[hh, rows] += jnp.dot(jnp.exp(sc).astype(BF16), vaug[hh],
                                         preferred_element_type=F32)

    k_max2 = [kn2_ref[h] for h in heads]

    def norm2(s, hh):
        stat = (diag[s] * blk // PROJ_ROWS) * LANES
        return (NORM_SLACK * NORM_SLACK) * qn2_ref[stat + heads[hh]] * k_max2[hh]

    def live(t):
        keep = False
        for s in streams:
            j = diag[s] - 1 - t
            jj = jnp.maximum(j, 0)
            prev = jnp.maximum(diag[s] - 1, 0)
            for hh in range(HEAD_PAIR):
                base = heads[hh] * nblk
                gap = FOX_SKIP_NATS + cend_ref[base + prev] - cend_ref[base + jj]
                dead = jnp.logical_and(gap <= 0.0, norm2(s, hh) <= gap * gap)
                keep = jnp.logical_or(keep, jnp.logical_and(j >= 0, jnp.logical_not(dead)))
        return keep

    def sweeps(stepper):
        def sweep(back, masked=False):
            for s in streams:
                stepper(s, diag[s] - back, masked)

        def body(t):
            sweep(t + 1)
            return t + 1

        sweep(0, masked=True)
        sweep(1)
        lax.while_loop(live, body, 1)

    direct_ok = True
    for s in streams:
        for hh in range(HEAD_PAIR):
            direct_ok = jnp.logical_and(direct_ok, norm2(s, hh) <= FOX_DIRECT_NORM2)

    @pl.when(direct_ok)
    def _():
        q_dir = [q_aug(s, shift=True) for s in streams]
        q_void = [q_aug(s, shift=True, void=True) for s in streams]
        sweeps(lambda s, j, masked: step_direct(
            s, j, q_dir[s] if masked else jnp.where(j >= 0, q_dir[s], q_void[s]), masked))

    @pl.when(jnp.logical_not(direct_ok))
    def _():
        m_ref[...] = jnp.full_like(m_ref, NEG_BIG)
        sweeps(lambda s, j, masked: step_online(s, j, q_aug(s, shift=False, void=j < 0), masked))

    outs = []
    for hh in range(HEAD_PAIR):
        acc = acc_ref[hh]
        outs.append(acc / pltpu.roll(acc, FOX_DH, axis=1))
    o = jnp.where(lo_lanes, outs[0], outs[1])
    o_ref[...] = (o * gate_ref[...].astype(F32)).astype(BF16)


def _out_kernel(x_ref, gla_ref, fox_ref, mq_ref, mg_ref, mk_ref, mv_ref,
                w_out_ref, fg_ref, o_ref, wo_ref):
    @pl.when(pl.program_id(0) == 0)
    def _():
        wo_ref[...] = w_out_ref[...].astype(BF16)

    lane = lax.broadcasted_iota(jnp.int32, (1, LANES), 1)
    lo_lanes = lane < MEM_DH
    nt = (((1,), (1,)), ((), ()))
    mem_parts = []
    for p in range(MEM_HEADS // HEAD_PAIR):
        ls = slice(p * LANES, (p + 1) * LANES)
        q = mq_ref[:, ls]
        kb = mk_ref[:, ls]
        vb = mv_ref[:, ls]
        zero = jnp.zeros_like(q)
        one = jnp.ones_like(vb)
        qh = (jnp.where(lo_lanes, q, zero), jnp.where(lo_lanes, zero, q))
        vaug = (jnp.where(lo_lanes, vb, one), jnp.where(lo_lanes, one, vb))
        outs = []
        for hh in range(HEAD_PAIR):
            s = lax.dot_general(qh[hh], kb, nt, preferred_element_type=F32)
            pexp = jnp.exp(s - jnp.max(s, axis=1, keepdims=True))
            pv = jnp.dot(pexp.astype(BF16), vaug[hh], preferred_element_type=F32)
            outs.append(pv / pltpu.roll(pv, MEM_DH, axis=1))
        o = jnp.where(lo_lanes, outs[0], outs[1])
        mem_parts.append((o * mg_ref[:, ls].astype(F32)).astype(BF16))
    mixed = jnp.concatenate([gla_ref[...], fox_ref[...]] + mem_parts, axis=1)
    y = x_ref[...] + jnp.dot(mixed, wo_ref[...], preferred_element_type=F32)
    o_ref[...] = y * _rms_scale(y, D_MODEL) * fg_ref[...]


def _params(*sem):
    return pltpu.CompilerParams(dimension_semantics=sem, vmem_limit_bytes=VMEM_LIMIT)


def _layer(x, mem, norm_g, w_in, w_alpha_up, b_alpha, b_forget, gla_norm_g,
           mem_norm_g, w_mem_kv, w_out, out_g):
    w_in_t = jnp.transpose(w_in[None], (0, 2, 1)).reshape(-1, LANES)
    T = x.shape[0]
    M = mem.shape[0]

    def rows(width, n=PROJ_ROWS):
        return pl.BlockSpec((n, width), lambda i: (i, 0))

    def whole(shape):
        return pl.BlockSpec(shape, lambda i: (0,) * len(shape))

    bshape = lambda w: jax.ShapeDtypeStruct((T, w), BF16)
    nproj = T // PROJ_ROWS
    stat_spec = pl.BlockSpec((1, 1, LANES), lambda i: (i, 0, 0))
    stat_shape = jax.ShapeDtypeStruct((nproj, 1, LANES), F32)
    gla_pairs, gla_chunks = GLA_HEADS // HEAD_PAIR, PROJ_ROWS // GLA_CHUNK
    pair_w = HEAD_PAIR * GLA_DV_PAD
    (gla, fq, fk, fv, fgate, mq, mg, crow, cx, cq, qn2, kn2) = pl.pallas_call(
        _proj_kernel,
        grid=(nproj,),
        in_specs=[rows(D_MODEL), whole((1, D_MODEL)),
                  pl.BlockSpec(w_in_t.shape, lambda i: (0, 0), pipeline_mode=pl.Buffered(1)),
                  whole(w_alpha_up.shape), whole((1, GLA_HEADS * GLA_DK)),
                  whole((1, FOX_HEADS)), whole((1, GLA_DV))],
        out_specs=[rows(GLA_OUT_W),
                   rows(FOX_W), pl.BlockSpec((FOX_W, PROJ_ROWS), lambda i: (0, i)),
                   rows(FOX_W), rows(FOX_W),
                   rows(MEM_W), rows(MEM_W),
                   pl.BlockSpec((SUBLANES, PROJ_ROWS), lambda i: (0, i)),
                   pl.BlockSpec((LANES, PROJ_ROWS), lambda i: (0, i)), rows(LANES),
                   stat_spec, pl.BlockSpec((1, 1, LANES), lambda i: (0, 0, 0))],
        out_shape=[bshape(GLA_OUT_W),
                   bshape(FOX_W), jax.ShapeDtypeStruct((FOX_W, T), BF16),
                   bshape(FOX_W), bshape(FOX_W),
                   bshape(MEM_W), bshape(MEM_W),
                   jax.ShapeDtypeStruct((SUBLANES, T), F32),
                   jax.ShapeDtypeStruct((LANES, T), BF16), bshape(LANES),
                   stat_shape, jax.ShapeDtypeStruct((1, 1, LANES), F32)],
        scratch_shapes=[
            pltpu.VMEM((SUBLANES, LANES), F32),
            pltpu.VMEM((IN_COLS_PAD, D_MODEL), BF16),
            pltpu.VMEM((SMALL_W, GLA_QK_W), F32),
            pltpu.VMEM((1, GLA_QK_W), F32),
            pltpu.VMEM((1, SMALL_W), F32),
            pltpu.VMEM((1, GLA_DV_PAD), F32),
            pltpu.VMEM((FOX_W, LANES), BF16),
            pltpu.VMEM((PROJ_ROWS, GLA_QK_W), BF16),
            pltpu.VMEM((PROJ_ROWS, GLA_QK_W), BF16),
            pltpu.VMEM((PROJ_ROWS, GLA_V_W), BF16),
            pltpu.VMEM((PROJ_ROWS, GLA_V_W), BF16),
            pltpu.VMEM((PROJ_ROWS, GLA_QK_W), F32),
            pltpu.VMEM((gla_pairs, LANES, pair_w), F32),
            pltpu.VMEM((PROJ_ROWS, GLA_V_W), BF16),
            pltpu.VMEM((gla_pairs, gla_chunks, LANES, pair_w), F32),
            pltpu.VMEM((gla_pairs, gla_chunks, LANES, LANES), F32),
            pltpu.VMEM((gla_pairs, gla_chunks, LANES, pair_w), BF16)],
        compiler_params=_params("arbitrary"),
        name="proj",
    )(x, norm_g[None, :], w_in_t, w_alpha_up, b_alpha[None, :], b_forget[None, :],
      gla_norm_g[None, :])

    mk, mv = pl.pallas_call(
        _memkv_kernel,
        out_shape=[jax.ShapeDtypeStruct((M, MEM_W), BF16)] * 2,
        compiler_params=pltpu.CompilerParams(vmem_limit_bytes=VMEM_LIMIT),
        name="memkv",
    )(mem, mem_norm_g[None, :], w_mem_kv)

    cend = crow[:FOX_HEADS, FOX_KEYS - 1::FOX_KEYS].reshape(-1)
    pair_rows = pl.BlockSpec((FOX_BLOCK, LANES), lambda p, i, *_: (i, p))
    pair_all = pl.BlockSpec((T, LANES), lambda p, i, *_: (0, p))
    fox = pl.pallas_call(
        _fox_kernel,
        grid_spec=pltpu.PrefetchScalarGridSpec(
            num_scalar_prefetch=3,
            grid=(FOX_HEADS // HEAD_PAIR, T // FOX_BLOCK),
            in_specs=[pair_rows, pl.BlockSpec((FOX_BLOCK, LANES), lambda p, i, *_: (i, 0)),
                      pl.BlockSpec((LANES, T), lambda p, i, *_: (p, 0)),
                      pl.BlockSpec((LANES, T), lambda p, i, *_: (0, 0)), pair_all, pair_rows],
            out_specs=pair_rows,
            scratch_shapes=[pltpu.VMEM((HEAD_PAIR, FOX_BLOCK, LANES), F32),
                            pltpu.VMEM((HEAD_PAIR, FOX_BLOCK, LANES), F32)]),
        out_shape=bshape(FOX_W),
        compiler_params=_params("arbitrary", "arbitrary"),
        name="fox",
    )(qn2.reshape(-1), kn2.reshape(-1), cend, fq, cq, fk, cx, fv, fgate)

    out = pl.pallas_call(
        _out_kernel,
        grid=(T // OUT_ROWS,),
        in_specs=[rows(D_MODEL, OUT_ROWS), rows(GLA_OUT_W, OUT_ROWS), rows(FOX_W, OUT_ROWS),
                  rows(MEM_W, OUT_ROWS), rows(MEM_W, OUT_ROWS),
                  whole((M, MEM_W)), whole((M, MEM_W)),
                  pl.BlockSpec(w_out.shape, lambda i: (0, 0), pipeline_mode=pl.Buffered(1)),
                  whole((1, D_MODEL))],
        out_specs=rows(D_MODEL, OUT_ROWS),
        out_shape=jax.ShapeDtypeStruct((T, D_MODEL), F32),
        scratch_shapes=[pltpu.VMEM((D_MODEL, D_MODEL), BF16)],
        compiler_params=_params("arbitrary"),
        name="out",
    )(x, gla, fox, mq, mg, mk, mv, w_out, out_g[None, :])
    return out


def kernel(x, mem, norm_g, w_in, w_alpha_up, b_alpha, b_forget, gla_norm_g, mem_norm_g,
           w_mem_kv, w_out, final_norm_g):
    assert x.shape[0] == 1 and mem.shape[0] == 1 and norm_g.shape[0] == 1
    assert x.shape[1] % max(PROJ_ROWS, FOX_BLOCK, OUT_ROWS) == 0
    out = _layer(x[0], mem[0], norm_g[0], w_in[0], w_alpha_up[0], b_alpha[0], b_forget[0],
                 gla_norm_g[0], mem_norm_g[0], w_mem_kv[0], w_out[0], final_norm_g)
    return out[None]
```

```python
import functools

import jax
import jax.numpy as jnp
from jax import lax
from jax.experimental import pallas as pl
from jax.experimental.pallas import tpu as pltpu

F32 = jnp.float32
BF16 = jnp.bfloat16

EPS = 1e-6
LANES = 128
SUBLANES = 8

D_MODEL = 1024
GLA_HEADS, GLA_DK, GLA_DV, GLA_RANK = 4, 48, 96, 16
GLA_DK_PAD = 64
GLA_DV_PAD = LANES
GLA_GATE_NORM = 16.0
GLA_CHUNK = 64
FOX_HEADS, FOX_DH = 6, 64
MEM_HEADS, MEM_DH = 4, 64
HEAD_PAIR = 2
GLA_QK_W = GLA_HEADS * GLA_DK_PAD
GLA_V_W = GLA_HEADS * GLA_DV_PAD
GLA_OUT_W = GLA_HEADS * GLA_DV
FOX_W = FOX_HEADS * FOX_DH
MEM_W = MEM_HEADS * MEM_DH
SMALL_W = LANES
FG_LANE0 = 0
LR_LANE0 = SUBLANES

_GROUPS = (("gq", GLA_QK_W), ("gk", GLA_QK_W), ("gv", GLA_V_W), ("gg", GLA_V_W),
           ("fq", FOX_W), ("fk", FOX_W), ("fv", FOX_W), ("fgate", FOX_W),
           ("mq", MEM_W), ("mg", MEM_W), ("small", SMALL_W))
_OFF = {}
_o = 0
for _n, _w in _GROUPS:
    _OFF[_n] = (_o, _o + _w)
    _o += _w
IN_COLS_PAD = _o

PROJ_ROWS = 1024
FOX_BLOCK = 2048
FOX_KEYS = 256
OUT_ROWS = 1024
VMEM_LIMIT = 56 * 1024 * 1024

NEG_BIG = -1e30
FOX_SKIP_NATS = 105.0
NORM_SLACK = 1.02
FOX_DIRECT_NORM2 = 3600.0
CX_HI, CX_MID, CX_LO, CX_ONE = 0, 8, 16, 24


def _log_sigmoid(z):
    return jnp.minimum(z, 0.0) - jnp.log(1.0 + jnp.exp(-jnp.abs(z)))


def _silu(z):
    return z / (1.0 + jnp.exp(-z))


def _rms_scale(v, width):
    return lax.rsqrt(jnp.sum(v * v, axis=-1, keepdims=True) * (1.0 / width) + EPS)


def _w_in_segments():
    qk, gw = GLA_HEADS * GLA_DK, GLA_HEADS * GLA_DV
    src = {}
    o = 0
    for name, width in (("gq", qk), ("gk", qk), ("gv", gw), ("lr", GLA_RANK), ("gg", gw),
                        ("fq", FOX_W), ("fk", FOX_W), ("fv", FOX_W), ("fg", FOX_HEADS),
                        ("fgate", FOX_W), ("mq", MEM_W), ("mg", MEM_W)):
        src[name] = o
        o += width
    segs = []
    for name, d, d_pad in (("gq", GLA_DK, GLA_DK_PAD), ("gk", GLA_DK, GLA_DK_PAD),
                           ("gv", GLA_DV, GLA_DV_PAD), ("gg", GLA_DV, GLA_DV_PAD)):
        segs += [(src[name] + h * d, _OFF[name][0] + h * d_pad, d) for h in range(GLA_HEADS)]
    segs += [(src[name], _OFF[name][0], _OFF[name][1] - _OFF[name][0])
             for name in ("fq", "fk", "fv", "fgate", "mq", "mg")]
    segs += [(src["fg"], _OFF["small"][0] + FG_LANE0, FOX_HEADS),
             (src["lr"], _OFF["small"][0] + LR_LANE0, GLA_RANK)]
    return tuple(segs)


def _proj_kernel(x_ref, g_ref, w_in_ref, w_alpha_ref, b_alpha_ref, b_forget_ref, gla_g_ref,
                 gla_ref, fq_ref, fk_ref, fv_ref, fgate_ref,
                 mq_ref, mg_ref, crow_ref, cx_ref, cq_ref, qn2_ref, kn2_ref,
                 carry_ref, wt_ref, wa_ref, ba_ref, bf_ref, ng_ref, seg_ref,
                 gq_ref, gk_ref, gv_ref, gg_ref, loga_ref,
                 s_ref, lhs_ref, kv_ref, dec_ref, sprev_ref):
    rows = x_ref.shape[0]
    k_chunks = D_MODEL // LANES

    @pl.when(pl.program_id(0) == 0)
    def _():
        carry_ref[...] = jnp.zeros_like(carry_ref)
        s_ref[...] = jnp.zeros_like(s_ref)
        kn2_ref[...] = jnp.zeros_like(kn2_ref)
        wa_ref[...] = jnp.zeros_like(wa_ref)
        ba_ref[...] = jnp.zeros_like(ba_ref)
        bf_ref[...] = jnp.zeros_like(bf_ref)
        ng_ref[...] = jnp.zeros_like(ng_ref)
        for h in range(GLA_HEADS):
            src = slice(h * GLA_DK, (h + 1) * GLA_DK)
            dst = slice(h * GLA_DK_PAD, h * GLA_DK_PAD + GLA_DK)
            wa_ref[LR_LANE0:LR_LANE0 + GLA_RANK, dst] = w_alpha_ref[:, src]
            ba_ref[:, dst] = b_alpha_ref[:, src]
        bf_ref[:, FG_LANE0:FG_LANE0 + FOX_HEADS] = b_forget_ref[...]
        ng_ref[:, 0:GLA_DV] = gla_g_ref[...]
        seg_ref[...] = (lax.broadcasted_iota(jnp.int32, seg_ref.shape, 0) // FOX_DH
                        == lax.broadcasted_iota(jnp.int32, seg_ref.shape, 1)).astype(BF16)
        wt_ref[...] = jnp.zeros_like(wt_ref)
        for s0, d0, width in _w_in_segments():
            for c in range(k_chunks):
                wt_ref[d0:d0 + width, c * LANES:(c + 1) * LANES] = (
                    w_in_ref[pl.ds(s0 * k_chunks + c, width, stride=k_chunks), :].astype(BF16))

    x = x_ref[...]
    xn = (x * _rms_scale(x, D_MODEL) * g_ref[...]).astype(BF16)
    nt = (((1,), (1,)), ((), ()))

    def proj(first, last):
        lo, hi = _OFF[first][0], _OFF[last][1]
        y = lax.dot_general(xn, wt_ref[lo:hi, :], nt, preferred_element_type=F32)
        return lambda name: y[:, _OFF[name][0] - lo:_OFF[name][1] - lo]

    tail = proj("mq", "small")
    gla = proj("gq", "gg")
    small = tail("small")
    logf = _log_sigmoid(small + bf_ref[...])
    c = logf.T[0:SUBLANES, :]
    lane = lax.broadcasted_iota(jnp.int32, c.shape, 1)
    shift = 1
    while shift < rows:
        c = c + jnp.where(lane >= shift, pltpu.roll(c, shift, axis=1), 0.0)
        shift *= 2
    c = c + carry_ref[:, 0:1]
    crow_ref[...] = c
    carry_ref[...] = jnp.broadcast_to(c[:, rows - 1:rows], carry_ref.shape)
    neg = -c
    hi = neg.astype(BF16).astype(F32)
    mid = (neg - hi).astype(BF16).astype(F32)
    low = neg - hi - mid
    parts = jnp.concatenate(
        [hi, mid, low, jnp.ones_like(c), jnp.zeros((LANES - 4 * SUBLANES, rows), F32)], axis=0)
    cx_ref[...] = parts.astype(BF16)
    cq_ref[...] = parts.T.astype(BF16)

    z = (jnp.dot(small.astype(BF16), wa_ref[...].astype(BF16), preferred_element_type=F32)
         + ba_ref[...])
    loga_ref[...] = _log_sigmoid(z) * (1.0 / GLA_GATE_NORM)
    gq_ref[...] = gla("gq").astype(BF16)
    gk_ref[...] = gla("gk").astype(BF16)
    gv_ref[...] = gla("gv").astype(BF16)
    gg_ref[...] = _silu(gla("gg")).astype(BF16)
    gla_local, gla_scan, gla_output = _gla_block(
        gq_ref, gk_ref, gv_ref, loga_ref, gg_ref, ng_ref, gla_ref,
        s_ref, lhs_ref, kv_ref, dec_ref, sprev_ref)

    def max_sq_norm(v):
        v32 = v.astype(F32)
        n2 = jnp.dot((v32 * v32).astype(BF16), seg_ref[...], preferred_element_type=F32)
        return jnp.max(n2, axis=0, keepdims=True)

    gla_local()
    fox_qk = proj("fq", "fk")
    mq_ref[...] = (tail("mq") * MEM_DH ** -0.5).astype(BF16)
    mg_ref[...] = _silu(tail("mg")).astype(BF16)
    gla_scan()
    fq = (fox_qk("fq") * FOX_DH ** -0.5).astype(BF16)
    fk = fox_qk("fk").astype(BF16)
    fq_ref[...] = fq
    fk_ref[...] = fox_qk("fk").T.astype(BF16)
    gla_output()
    fox_vg = proj("fv", "fgate")
    fv_ref[...] = fox_vg("fv").astype(BF16)
    fgate_ref[...] = _silu(fox_vg("fgate")).astype(BF16)
    qn2_ref[0] = max_sq_norm(fq)
    kn2_ref[0] = jnp.maximum(kn2_ref[0], max_sq_norm(fk))


def _memkv_kernel(mem_ref, g_ref, w_ref, mk_ref, mv_ref):
    m = mem_ref[...]
    mn = (m * _rms_scale(m, D_MODEL) * g_ref[...]).astype(BF16)
    kv = jnp.dot(mn, w_ref[...].astype(BF16), preferred_element_type=F32)
    mk_ref[...] = kv[:, :MEM_W].astype(BF16)
    mv_ref[...] = kv[:, MEM_W:].astype(BF16)


def _gla_block(q_ref, k_ref, v_ref, loga_ref, gate_ref, ng_ref, o_ref,
               s_ref, lhs_ref, kv_ref, dec_ref, sprev_ref):
    C = GLA_CHUNK
    W = HEAD_PAIR * GLA_DV_PAD
    n_chunks = q_ref.shape[0] // C

    row = lax.broadcasted_iota(jnp.int32, (C, LANES), 0)
    lane = lax.broadcasted_iota(jnp.int32, (C, LANES), 1)
    lo_k = lane < GLA_DK_PAD
    causal = row >= jnp.where(lo_k, lane, lane - GLA_DK_PAD)
    lo_v = lax.broadcasted_iota(jnp.int32, (C, W), 1) < GLA_DV_PAD
    st_row = lax.broadcasted_iota(jnp.int32, (LANES, W), 0)
    st_lane = lax.broadcasted_iota(jnp.int32, (LANES, W), 1)
    own = (st_row < GLA_DK_PAD) == (st_lane < GLA_DV_PAD)
    eye = (lax.broadcasted_iota(jnp.int32, (LANES, LANES), 0)
           == lax.broadcasted_iota(jnp.int32, (LANES, LANES), 1))
    scale = GLA_DK ** -0.5
    nt = (((1,), (1,)), ((), ()))
    tn = (((0,), (0,)), ((), ()))
    ng = jnp.concatenate([ng_ref[...]] * HEAD_PAIR, axis=1)

    pairs = range(GLA_HEADS // HEAD_PAIR)

    def local(ci):
        rs = slice(ci * C, (ci + 1) * C)
        for p in pairs:
            ls = slice(p * LANES, (p + 1) * LANES)
            vs = slice(p * W, (p + 1) * W)
            b = loga_ref[rs, ls]
            shift = 1
            while shift < C:
                b = b + jnp.where(row >= shift, pltpu.roll(b, shift, axis=0), 0.0)
                shift *= 2
            b_last = b[C - 1:C, :]
            k2 = k_ref[rs, ls].astype(F32)
            qd = (q_ref[rs, ls].astype(F32) * scale * jnp.exp(b)).astype(BF16)
            kd = (k2 * jnp.exp(-b)).astype(BF16)
            ke = (k2 * jnp.exp(b_last - b)).astype(BF16)
            zk = jnp.zeros_like(kd)
            kd_blk = jnp.concatenate([jnp.where(lo_k, kd, zk), jnp.where(lo_k, zk, kd)], axis=0)
            attn = lax.dot_general(qd, kd_blk, nt, preferred_element_type=F32)
            lhs_ref[rs, vs] = jnp.concatenate([jnp.where(causal, attn, 0.0).astype(BF16), qd], axis=1)
            kv = lax.dot_general(ke, v_ref[rs, vs], tn, preferred_element_type=F32)
            kv_ref[p, ci] = jnp.where(own, kv, 0.0)
            dcol = jnp.exp(jnp.sum(jnp.where(eye, jnp.broadcast_to(b_last, (LANES, LANES)), 0.0),
                                   axis=1, keepdims=True))
            dec_ref[p, ci] = jnp.broadcast_to(dcol, (LANES, LANES))

    def scan(ci):
        for p in pairs:
            s_prev = s_ref[p]
            sprev_ref[p, ci] = s_prev.astype(BF16)
            s_ref[p] = jnp.tile(dec_ref[p, ci], (1, HEAD_PAIR)) * s_prev + kv_ref[p, ci]

    def output(ci):
        rs = slice(ci * C, (ci + 1) * C)
        for p in pairs:
            vs = slice(p * W, (p + 1) * W)
            v2 = v_ref[rs, vs]
            zv = jnp.zeros_like(v2)
            v_blk = jnp.concatenate([jnp.where(lo_v, v2, zv), jnp.where(lo_v, zv, v2)], axis=0)
            o = jnp.dot(lhs_ref[rs, vs], jnp.concatenate([v_blk, sprev_ref[p, ci]], axis=0),
                        preferred_element_type=F32)
            o2 = o * o
            ms = jnp.where(lo_v, jnp.sum(o2[:, :GLA_DV_PAD], axis=1, keepdims=True),
                           jnp.sum(o2[:, GLA_DV_PAD:], axis=1, keepdims=True))
            on = o * lax.rsqrt(ms * (1.0 / GLA_DV) + EPS) * ng
            og = (on * gate_ref[rs, vs].astype(F32)).astype(BF16)
            for hh in range(HEAD_PAIR):
                c0 = (p * HEAD_PAIR + hh) * GLA_DV
                o_ref[rs, c0:c0 + GLA_DV] = og[:, hh * GLA_DV_PAD:hh * GLA_DV_PAD + GLA_DV]

    def all_chunks(phase):
        return lambda: [phase(ci) for ci in range(n_chunks)]

    return all_chunks(local), all_chunks(scan), all_chunks(output)


def _fox_kernel(qn2_ref, kn2_ref, cend_ref, q_ref, cq_ref, k_ref, cx_ref, v_ref, gate_ref, o_ref,
                m_ref, acc_ref):
    blk = FOX_KEYS
    streams = range(q_ref.shape[0] // blk)
    pair = pl.program_id(0)
    qi = pl.program_id(1)
    nblk = pl.num_programs(1) * len(streams)
    lane = lax.broadcasted_iota(jnp.int32, (1, LANES), 1)
    lo_lanes = lane < FOX_DH
    reps = blk // LANES
    diag = [qi * len(streams) + s for s in streams]
    heads = [pair * HEAD_PAIR + hh for hh in range(HEAD_PAIR)]

    q = q_ref[...]
    zero = jnp.zeros_like(q)
    q_lo, q_hi = jnp.where(lo_lanes, q, zero), jnp.where(lo_lanes, zero, q)
    q_stack = [jnp.concatenate([q_lo[s * blk:(s + 1) * blk], q_hi[s * blk:(s + 1) * blk]], axis=0)
               for s in streams]
    g_row = lax.broadcasted_iota(jnp.int32, (LANES, LANES), 0)
    g_col = lax.broadcasted_iota(jnp.int32, (LANES, LANES), 1)
    xlane = lax.broadcasted_iota(jnp.int32, (HEAD_PAIR * blk, LANES), 1)
    cq = cq_ref[...]

    def lane_map(h, shift):
        g = jnp.where((g_row == CX_ONE) & ((g_col == CX_HI + h) | (g_col == CX_MID + h)
                                           | (g_col == CX_LO + h)), 1.0, 0.0)
        if shift:
            for part, base in enumerate((CX_HI, CX_MID, CX_LO)):
                g = jnp.where((g_row == base + h) & (g_col == CX_ONE + part), -1.0, g)
        return g.astype(BF16)

    def q_aug(s, shift, void=None):
        cs = cq[s * blk:(s + 1) * blk]
        extra = jnp.concatenate([jnp.dot(cs, lane_map(h, shift), preferred_element_type=F32)
                                 for h in heads], axis=0)
        if void is not None:
            extra = jnp.where(jnp.logical_and(xlane == CX_ONE + 3, void), NEG_BIG, extra)
        return jnp.concatenate([q_stack[s], extra.astype(BF16)], axis=1)

    acc_ref[...] = jnp.zeros_like(acc_ref)
    qpos = lax.broadcasted_iota(jnp.int32, (blk, blk), 0)
    kpos = lax.broadcasted_iota(jnp.int32, (blk, blk), 1)

    def block(j):
        ks = pl.ds(pl.multiple_of(jnp.maximum(j, 0) * blk, blk), blk)
        k_aug = jnp.concatenate([k_ref[:, ks], cx_ref[:, ks]], axis=0)
        vb = v_ref[ks, :]
        one = jnp.ones_like(vb)
        return k_aug, (jnp.where(lo_lanes, vb, one), jnp.where(lo_lanes, one, vb))

    def step_online(s, j, qa, masked):
        rows = slice(s * blk, (s + 1) * blk)
        k_aug, vaug = block(j)
        s_all = jnp.dot(qa, k_aug, preferred_element_type=F32)
        for hh in range(HEAD_PAIR):
            sc = s_all[hh * blk:(hh + 1) * blk]
            if masked:
                sc = jnp.where(kpos <= qpos, sc, NEG_BIG)
            m_prev = m_ref[hh, rows]
            m_new = jnp.maximum(m_prev, jnp.max(sc, axis=1, keepdims=True))
            p = jnp.exp(sc - jnp.tile(m_new, (1, reps)))
            alpha = jnp.exp(m_prev - m_new)
            pv = jnp.dot(p.astype(BF16), vaug[hh], preferred_element_type=F32)
            acc_ref[hh, rows] = alpha * acc_ref[hh, rows] + pv
            m_ref[hh, rows] = m_new

    def step_direct(s, j, qa, masked, only=None):
        rows = slice(s * blk, (s + 1) * blk)
        k_aug, vaug = block(j)
        hsel = range(HEAD_PAIR) if only is None else (only,)
        lhs = qa if only is None else qa[only * blk:(only + 1) * blk]
        s_all = jnp.dot(lhs, k_aug, preferred_element_type=F32)
        for n, hh in enumerate(hsel):
            sc = s_all[n * blk:(n + 1) * blk]
            if masked:
                sc = jnp.where(kpos <= qpos, sc, NEG_BIG)
            acc_ref[hh, rows] += jnp.dot(jnp.exp(sc).astype(BF16), vaug[hh],
                                         preferred_element_type=F32)

    k_max2 = [kn2_ref[h] for h in heads]

    def norm2(s, hh):
        stat = (diag[s] * blk // PROJ_ROWS) * LANES
        return (NORM_SLACK * NORM_SLACK) * qn2_ref[stat + heads[hh]] * k_max2[hh]

    def live_head(t, hh):
        keep = False
        base = heads[hh] * nblk
        for s in streams:
            j = diag[s] - 1 - t
            jj = jnp.maximum(j, 0)
            prev = jnp.maximum(diag[s] - 1, 0)
            gap = FOX_SKIP_NATS + cend_ref[base + prev] - cend_ref[base + jj]
            dead = jnp.logical_and(gap <= 0.0, norm2(s, hh) <= gap * gap)
            keep = jnp.logical_or(keep, jnp.logical_and(j >= 0, jnp.logical_not(dead)))
        return keep

    def sweeps(stepper, per_head):
        def sweep(back, masked=False, only=None):
            for s in streams:
                stepper(s, diag[s] - back, masked, only)

        def loop(t0, cond, only=None):
            def body(t):
                sweep(t + 1, only=only)
                return t + 1
            return lax.while_loop(cond, body, t0)

        sweep(0, masked=True)
        sweep(1)
        if per_head:
            t_both = loop(1, lambda t: jnp.logical_and(live_head(t, 0), live_head(t, 1)))
            for hh in range(HEAD_PAIR):
                loop(t_both, lambda t, hh=hh: live_head(t, hh), only=hh)
        else:
            loop(1, lambda t: jnp.logical_or(live_head(t, 0), live_head(t, 1)))

    direct_ok = True
    for s in streams:
        for hh in range(HEAD_PAIR):
            direct_ok = jnp.logical_and(direct_ok, norm2(s, hh) <= FOX_DIRECT_NORM2)

    @pl.when(direct_ok)
    def _():
        q_dir = [q_aug(s, shift=True) for s in streams]
        q_void = [q_aug(s, shift=True, void=True) for s in streams]
        sweeps(lambda s, j, masked, only: step_direct(
            s, j, q_dir[s] if masked else jnp.where(j >= 0, q_dir[s], q_void[s]), masked, only),
            per_head=True)

    @pl.when(jnp.logical_not(direct_ok))
    def _():
        m_ref[...] = jnp.full_like(m_ref, NEG_BIG)
        sweeps(lambda s, j, masked, only: step_online(
            s, j, q_aug(s, shift=False, void=j < 0), masked), per_head=False)

    outs = []
    for hh in range(HEAD_PAIR):
        acc = acc_ref[hh]
        outs.append(acc / pltpu.roll(acc, FOX_DH, axis=1))
    o = jnp.where(lo_lanes, outs[0], outs[1])
    o_ref[...] = (o * gate_ref[...].astype(F32)).astype(BF16)


def _out_kernel(x_ref, gla_ref, fox_ref, mq_ref, mg_ref, mk_ref, mv_ref,
                w_out_ref, fg_ref, o_ref, wo_ref):
    @pl.when(pl.program_id(0) == 0)
    def _():
        wo_ref[...] = w_out_ref[...].astype(BF16)

    lane = lax.broadcasted_iota(jnp.int32, (1, LANES), 1)
    lo_lanes = lane < MEM_DH
    nt = (((1,), (1,)), ((), ()))
    mem_parts = []
    for p in range(MEM_HEADS // HEAD_PAIR):
        ls = slice(p * LANES, (p + 1) * LANES)
        q = mq_ref[:, ls]
        kb = mk_ref[:, ls]
        vb = mv_ref[:, ls]
        zero = jnp.zeros_like(q)
        one = jnp.ones_like(vb)
        qh = (jnp.where(lo_lanes, q, zero), jnp.where(lo_lanes, zero, q))
        vaug = (jnp.where(lo_lanes, vb, one), jnp.where(lo_lanes, one, vb))
        outs = []
        for hh in range(HEAD_PAIR):
            s = lax.dot_general(qh[hh], kb, nt, preferred_element_type=F32)
            pexp = jnp.exp(s - jnp.max(s, axis=1, keepdims=True))
            pv = jnp.dot(pexp.astype(BF16), vaug[hh], preferred_element_type=F32)
            outs.append(pv / pltpu.roll(pv, MEM_DH, axis=1))
        o = jnp.where(lo_lanes, outs[0], outs[1])
        mem_parts.append((o * mg_ref[:, ls].astype(F32)).astype(BF16))
    mixed = jnp.concatenate([gla_ref[...], fox_ref[...]] + mem_parts, axis=1)
    y = x_ref[...] + jnp.dot(mixed, wo_ref[...], preferred_element_type=F32)
    o_ref[...] = y * _rms_scale(y, D_MODEL) * fg_ref[...]


def _params(*sem):
    return pltpu.CompilerParams(dimension_semantics=sem, vmem_limit_bytes=VMEM_LIMIT)


def _layer(x, mem, norm_g, w_in, w_alpha_up, b_alpha, b_forget, gla_norm_g,
           mem_norm_g, w_mem_kv, w_out, out_g):
    w_in_t = jnp.transpose(w_in[None], (0, 2, 1)).reshape(-1, LANES)
    T = x.shape[0]
    M = mem.shape[0]

    def rows(width, n=PROJ_ROWS):
        return pl.BlockSpec((n, width), lambda i: (i, 0))

    def whole(shape):
        return pl.BlockSpec(shape, lambda i: (0,) * len(shape))

    bshape = lambda w: jax.ShapeDtypeStruct((T, w), BF16)
    nproj = T // PROJ_ROWS
    stat_spec = pl.BlockSpec((1, 1, LANES), lambda i: (i, 0, 0))
    stat_shape = jax.ShapeDtypeStruct((nproj, 1, LANES), F32)
    gla_pairs, gla_chunks = GLA_HEADS // HEAD_PAIR, PROJ_ROWS // GLA_CHUNK
    pair_w = HEAD_PAIR * GLA_DV_PAD
    (gla, fq, fk, fv, fgate, mq, mg, crow, cx, cq, qn2, kn2) = pl.pallas_call(
        _proj_kernel,
        grid=(nproj,),
        in_specs=[rows(D_MODEL), whole((1, D_MODEL)),
                  pl.BlockSpec(w_in_t.shape, lambda i: (0, 0), pipeline_mode=pl.Buffered(1)),
                  whole(w_alpha_up.shape), whole((1, GLA_HEADS * GLA_DK)),
                  whole((1, FOX_HEADS)), whole((1, GLA_DV))],
        out_specs=[rows(GLA_OUT_W),
                   rows(FOX_W), pl.BlockSpec((FOX_W, PROJ_ROWS), lambda i: (0, i)),
                   rows(FOX_W), rows(FOX_W),
                   rows(MEM_W), rows(MEM_W),
                   pl.BlockSpec((SUBLANES, PROJ_ROWS), lambda i: (0, i)),
                   pl.BlockSpec((LANES, PROJ_ROWS), lambda i: (0, i)), rows(LANES),
                   stat_spec, pl.BlockSpec((1, 1, LANES), lambda i: (0, 0, 0))],
        out_shape=[bshape(GLA_OUT_W),
                   bshape(FOX_W), jax.ShapeDtypeStruct((FOX_W, T), BF16),
                   bshape(FOX_W), bshape(FOX_W),
                   bshape(MEM_W), bshape(MEM_W),
                   jax.ShapeDtypeStruct((SUBLANES, T), F32),
                   jax.ShapeDtypeStruct((LANES, T), BF16), bshape(LANES),
                   stat_shape, jax.ShapeDtypeStruct((1, 1, LANES), F32)],
        scratch_shapes=[
            pltpu.VMEM((SUBLANES, LANES), F32),
            pltpu.VMEM((IN_COLS_PAD, D_MODEL), BF16),
            pltpu.VMEM((SMALL_W, GLA_QK_W), F32),
            pltpu.VMEM((1, GLA_QK_W), F32),
            pltpu.VMEM((1, SMALL_W), F32),
            pltpu.VMEM((1, GLA_DV_PAD), F32),
            pltpu.VMEM((FOX_W, LANES), BF16),
            pltpu.VMEM((PROJ_ROWS, GLA_QK_W), BF16),
            pltpu.VMEM((PROJ_ROWS, GLA_QK_W), BF16),
            pltpu.VMEM((PROJ_ROWS, GLA_V_W), BF16),
            pltpu.VMEM((PROJ_ROWS, GLA_V_W), BF16),
            pltpu.VMEM((PROJ_ROWS, GLA_QK_W), F32),
            pltpu.VMEM((gla_pairs, LANES, pair_w), F32),
            pltpu.VMEM((PROJ_ROWS, GLA_V_W), BF16),
            pltpu.VMEM((gla_pairs, gla_chunks, LANES, pair_w), F32),
            pltpu.VMEM((gla_pairs, gla_chunks, LANES, LANES), F32),
            pltpu.VMEM((gla_pairs, gla_chunks, LANES, pair_w), BF16)],
        compiler_params=_params("arbitrary"),
        name="proj",
    )(x, norm_g[None, :], w_in_t, w_alpha_up, b_alpha[None, :], b_forget[None, :],
      gla_norm_g[None, :])

    mk, mv = pl.pallas_call(
        _memkv_kernel,
        out_shape=[jax.ShapeDtypeStruct((M, MEM_W), BF16)] * 2,
        compiler_params=pltpu.CompilerParams(vmem_limit_bytes=VMEM_LIMIT),
        name="memkv",
    )(mem, mem_norm_g[None, :], w_mem_kv)

    cend = crow[:FOX_HEADS, FOX_KEYS - 1::FOX_KEYS].reshape(-1)
    pair_rows = pl.BlockSpec((FOX_BLOCK, LANES), lambda p, i, *_: (i, p))
    pair_all = pl.BlockSpec((T, LANES), lambda p, i, *_: (0, p))
    fox = pl.pallas_call(
        _fox_kernel,
        grid_spec=pltpu.PrefetchScalarGridSpec(
            num_scalar_prefetch=3,
            grid=(FOX_HEADS // HEAD_PAIR, T // FOX_BLOCK),
            in_specs=[pair_rows, pl.BlockSpec((FOX_BLOCK, LANES), lambda p, i, *_: (i, 0)),
                      pl.BlockSpec((LANES, T), lambda p, i, *_: (p, 0)),
                      pl.BlockSpec((LANES, T), lambda p, i, *_: (0, 0)), pair_all, pair_rows],
            out_specs=pair_rows,
            scratch_shapes=[pltpu.VMEM((HEAD_PAIR, FOX_BLOCK, LANES), F32),
                            pltpu.VMEM((HEAD_PAIR, FOX_BLOCK, LANES), F32)]),
        out_shape=bshape(FOX_W),
        compiler_params=_params("arbitrary", "arbitrary"),
        name="fox",
    )(qn2.reshape(-1), kn2.reshape(-1), cend, fq, cq, fk, cx, fv, fgate)

    out = pl.pallas_call(
        _out_kernel,
        grid=(T // OUT_ROWS,),
        in_specs=[rows(D_MODEL, OUT_ROWS), rows(GLA_OUT_W, OUT_ROWS), rows(FOX_W, OUT_ROWS),
                  rows(MEM_W, OUT_ROWS), rows(MEM_W, OUT_ROWS),
                  whole((M, MEM_W)), whole((M, MEM_W)),
                  pl.BlockSpec(w_out.shape, lambda i: (0, 0), pipeline_mode=pl.Buffered(1)),
                  whole((1, D_MODEL))],
        out_specs=rows(D_MODEL, OUT_ROWS),
        out_shape=jax.ShapeDtypeStruct((T, D_MODEL), F32),
        scratch_shapes=[pltpu.VMEM((D_MODEL, D_MODEL), BF16)],
        compiler_params=_params("arbitrary"),
        name="out",
    )(x, gla, fox, mq, mg, mk, mv, w_out, out_g[None, :])
    return out


def kernel(x, mem, norm_g, w_in, w_alpha_up, b_alpha, b_forget, gla_norm_g, mem_norm_g,
           w_mem_kv, w_out, final_norm_g):
    assert x.shape[0] == 1 and mem.shape[0] == 1 and norm_g.shape[0] == 1
    assert x.shape[1] % max(PROJ_ROWS, FOX_BLOCK, OUT_ROWS) == 0
    out = _layer(x[0], mem[0], norm_g[0], w_in[0], w_alpha_up[0], b_alpha[0], b_forget[0],
                 gla_norm_g[0], mem_norm_g[0], w_mem_kv[0], w_out[0], final_norm_g)
    return out[None]
```

```python
import functools

import jax
import jax.numpy as jnp
from jax import lax
from jax.experimental import pallas as pl
from jax.experimental.pallas import tpu as pltpu

F32 = jnp.float32
BF16 = jnp.bfloat16

EPS = 1e-6
LANES = 128
SUBLANES = 8

D_MODEL = 1024
GLA_HEADS, GLA_DK, GLA_DV, GLA_RANK = 4, 48, 96, 16
GLA_DK_PAD = 64
GLA_DV_PAD = LANES
GLA_GATE_NORM = 16.0
GLA_CHUNK = 64
FOX_HEADS, FOX_DH = 6, 64
MEM_HEADS, MEM_DH = 4, 64
HEAD_PAIR = 2
GLA_QK_W = GLA_HEADS * GLA_DK_PAD
GLA_V_W = GLA_HEADS * GLA_DV_PAD
GLA_OUT_W = GLA_HEADS * GLA_DV
FOX_W = FOX_HEADS * FOX_DH
MEM_W = MEM_HEADS * MEM_DH
SMALL_W = LANES
FG_LANE0 = 0
LR_LANE0 = SUBLANES

_GROUPS = (("gq", GLA_QK_W), ("gk", GLA_QK_W), ("gv", GLA_OUT_W), ("gg", GLA_OUT_W),
           ("fq", FOX_W), ("fk", FOX_W), ("fv", FOX_W), ("fgate", FOX_W),
           ("mq", MEM_W), ("mg", MEM_W), ("small", SMALL_W))
_OFF = {}
_o = 0
for _n, _w in _GROUPS:
    _OFF[_n] = (_o, _o + _w)
    _o += _w
IN_COLS_PAD = _o

PROJ_ROWS = 1024
FOX_BLOCK = 2048
FOX_KEYS = 256
OUT_ROWS = 1024
VMEM_LIMIT = 56 * 1024 * 1024

NEG_BIG = -1e30
FOX_SKIP_NATS = 105.0
NORM_SLACK = 1.02
FOX_DIRECT_NORM2 = 3600.0
CX_HI, CX_MID, CX_LO, CX_ONE = 0, 8, 16, 24


def _log_sigmoid(z):
    return jnp.minimum(z, 0.0) - jnp.log(1.0 + jnp.exp(-jnp.abs(z)))


def _silu(z):
    return z / (1.0 + jnp.exp(-z))


def _rms_scale(v, width):
    return lax.rsqrt(jnp.sum(v * v, axis=-1, keepdims=True) * (1.0 / width) + EPS)


def _w_in_segments():
    qk, gw = GLA_HEADS * GLA_DK, GLA_HEADS * GLA_DV
    src = {}
    o = 0
    for name, width in (("gq", qk), ("gk", qk), ("gv", gw), ("lr", GLA_RANK), ("gg", gw),
                        ("fq", FOX_W), ("fk", FOX_W), ("fv", FOX_W), ("fg", FOX_HEADS),
                        ("fgate", FOX_W), ("mq", MEM_W), ("mg", MEM_W)):
        src[name] = o
        o += width
    segs = []
    for name, d, d_pad in (("gq", GLA_DK, GLA_DK_PAD), ("gk", GLA_DK, GLA_DK_PAD)):
        segs += [(src[name] + h * d, _OFF[name][0] + h * d_pad, d) for h in range(GLA_HEADS)]
    segs += [(src[name], _OFF[name][0], _OFF[name][1] - _OFF[name][0])
             for name in ("gv", "gg", "fq", "fk", "fv", "fgate", "mq", "mg")]
    segs += [(src["fg"], _OFF["small"][0] + FG_LANE0, FOX_HEADS),
             (src["lr"], _OFF["small"][0] + LR_LANE0, GLA_RANK)]
    return tuple(segs)


def _proj_kernel(x_ref, g_ref, w_in_ref, w_alpha_ref, b_alpha_ref, b_forget_ref, gla_g_ref,
                 gla_ref, fq_ref, fk_ref, fv_ref, fgate_ref,
                 mq_ref, mg_ref, crow_ref, cx_ref, cq_ref, qn2_ref, kn2_ref,
                 carry_ref, wt_ref, wa_ref, ba_ref, bf_ref, ng_ref, seg_ref,
                 gq_ref, gk_ref, gv_ref, gg_ref, loga_ref,
                 s_ref, lhs_ref, kv_ref, dec_ref, sprev_ref):
    rows = x_ref.shape[0]
    k_chunks = D_MODEL // LANES

    @pl.when(pl.program_id(0) == 0)
    def _():
        carry_ref[...] = jnp.zeros_like(carry_ref)
        s_ref[...] = jnp.zeros_like(s_ref)
        kn2_ref[...] = jnp.zeros_like(kn2_ref)
        gv_ref[...] = jnp.zeros_like(gv_ref)
        gg_ref[...] = jnp.zeros_like(gg_ref)
        wa_ref[...] = jnp.zeros_like(wa_ref)
        ba_ref[...] = jnp.zeros_like(ba_ref)
        bf_ref[...] = jnp.zeros_like(bf_ref)
        ng_ref[...] = jnp.zeros_like(ng_ref)
        for h in range(GLA_HEADS):
            src = slice(h * GLA_DK, (h + 1) * GLA_DK)
            dst = slice(h * GLA_DK_PAD, h * GLA_DK_PAD + GLA_DK)
            wa_ref[LR_LANE0:LR_LANE0 + GLA_RANK, dst] = w_alpha_ref[:, src]
            ba_ref[:, dst] = b_alpha_ref[:, src]
        bf_ref[:, FG_LANE0:FG_LANE0 + FOX_HEADS] = b_forget_ref[...]
        ng_ref[:, 0:GLA_DV] = gla_g_ref[...]
        seg_ref[...] = (lax.broadcasted_iota(jnp.int32, seg_ref.shape, 0) // FOX_DH
                        == lax.broadcasted_iota(jnp.int32, seg_ref.shape, 1)).astype(BF16)
        wt_ref[...] = jnp.zeros_like(wt_ref)
        for s0, d0, width in _w_in_segments():
            for c in range(k_chunks):
                wt_ref[d0:d0 + width, c * LANES:(c + 1) * LANES] = (
                    w_in_ref[pl.ds(s0 * k_chunks + c, width, stride=k_chunks), :].astype(BF16))

    x = x_ref[...]
    xn = (x * _rms_scale(x, D_MODEL) * g_ref[...]).astype(BF16)
    nt = (((1,), (1,)), ((), ()))

    def proj(first, last):
        lo, hi = _OFF[first][0], _OFF[last][1]
        y = lax.dot_general(xn, wt_ref[lo:hi, :], nt, preferred_element_type=F32)
        return lambda name: y[:, _OFF[name][0] - lo:_OFF[name][1] - lo]

    tail = proj("mq", "small")
    gla = proj("gq", "gg")
    small = tail("small")
    logf = _log_sigmoid(small + bf_ref[...])
    c = logf.T[0:SUBLANES, :]
    lane = lax.broadcasted_iota(jnp.int32, c.shape, 1)
    shift = 1
    while shift < rows:
        c = c + jnp.where(lane >= shift, pltpu.roll(c, shift, axis=1), 0.0)
        shift *= 2
    c = c + carry_ref[:, 0:1]
    crow_ref[...] = c
    carry_ref[...] = jnp.broadcast_to(c[:, rows - 1:rows], carry_ref.shape)
    neg = -c
    hi = neg.astype(BF16).astype(F32)
    mid = (neg - hi).astype(BF16).astype(F32)
    low = neg - hi - mid
    parts = jnp.concatenate(
        [hi, mid, low, jnp.ones_like(c), jnp.zeros((LANES - 4 * SUBLANES, rows), F32)], axis=0)
    cx_ref[...] = parts.astype(BF16)
    cq_ref[...] = parts.T.astype(BF16)

    z = (jnp.dot(small.astype(BF16), wa_ref[...].astype(BF16), preferred_element_type=F32)
         + ba_ref[...])
    loga_ref[...] = _log_sigmoid(z) * (1.0 / GLA_GATE_NORM)
    gq_ref[...] = gla("gq").astype(BF16)
    gk_ref[...] = gla("gk").astype(BF16)
    gv, gg = gla("gv").astype(BF16), _silu(gla("gg")).astype(BF16)
    for h in range(GLA_HEADS):
        src = slice(h * GLA_DV, (h + 1) * GLA_DV)
        dst = slice(h * GLA_DV_PAD, h * GLA_DV_PAD + GLA_DV)
        gv_ref[:, dst] = gv[:, src]
        gg_ref[:, dst] = gg[:, src]
    gla_local, gla_scan, gla_output = _gla_block(
        gq_ref, gk_ref, gv_ref, loga_ref, gg_ref, ng_ref, gla_ref,
        s_ref, lhs_ref, kv_ref, dec_ref, sprev_ref)

    def max_sq_norm(v):
        v32 = v.astype(F32)
        n2 = jnp.dot((v32 * v32).astype(BF16), seg_ref[...], preferred_element_type=F32)
        return jnp.max(n2, axis=0, keepdims=True)

    gla_local()
    fox_qk = proj("fq", "fk")
    mq_ref[...] = (tail("mq") * MEM_DH ** -0.5).astype(BF16)
    mg_ref[...] = _silu(tail("mg")).astype(BF16)
    gla_scan()
    fq = (fox_qk("fq") * FOX_DH ** -0.5).astype(BF16)
    fk = fox_qk("fk").astype(BF16)
    fq_ref[...] = fq
    fk_ref[...] = fox_qk("fk").T.astype(BF16)
    gla_output()
    fox_vg = proj("fv", "fgate")
    fv_ref[...] = fox_vg("fv").astype(BF16)
    fgate_ref[...] = _silu(fox_vg("fgate")).astype(BF16)
    qn2_ref[0] = max_sq_norm(fq)
    kn2_ref[0] = jnp.maximum(kn2_ref[0], max_sq_norm(fk))


def _memkv_kernel(mem_ref, g_ref, w_ref, mk_ref, mv_ref):
    m = mem_ref[...]
    mn = (m * _rms_scale(m, D_MODEL) * g_ref[...]).astype(BF16)
    kv = jnp.dot(mn, w_ref[...].astype(BF16), preferred_element_type=F32)
    mk_ref[...] = kv[:, :MEM_W].astype(BF16)
    mv_ref[...] = kv[:, MEM_W:].astype(BF16)


def _gla_block(q_ref, k_ref, v_ref, loga_ref, gate_ref, ng_ref, o_ref,
               s_ref, lhs_ref, kv_ref, dec_ref, sprev_ref):
    C = GLA_CHUNK
    W = HEAD_PAIR * GLA_DV_PAD
    n_chunks = q_ref.shape[0] // C

    row = lax.broadcasted_iota(jnp.int32, (C, LANES), 0)
    lane = lax.broadcasted_iota(jnp.int32, (C, LANES), 1)
    lo_k = lane < GLA_DK_PAD
    causal = row >= jnp.where(lo_k, lane, lane - GLA_DK_PAD)
    lo_v = lax.broadcasted_iota(jnp.int32, (C, W), 1) < GLA_DV_PAD
    st_row = lax.broadcasted_iota(jnp.int32, (LANES, W), 0)
    st_lane = lax.broadcasted_iota(jnp.int32, (LANES, W), 1)
    own = (st_row < GLA_DK_PAD) == (st_lane < GLA_DV_PAD)
    eye = (lax.broadcasted_iota(jnp.int32, (LANES, LANES), 0)
           == lax.broadcasted_iota(jnp.int32, (LANES, LANES), 1))
    scale = GLA_DK ** -0.5
    nt = (((1,), (1,)), ((), ()))
    tn = (((0,), (0,)), ((), ()))
    ng = jnp.concatenate([ng_ref[...]] * HEAD_PAIR, axis=1)

    pairs = range(GLA_HEADS // HEAD_PAIR)

    def local(ci):
        rs = slice(ci * C, (ci + 1) * C)
        for p in pairs:
            ls = slice(p * LANES, (p + 1) * LANES)
            vs = slice(p * W, (p + 1) * W)
            b = loga_ref[rs, ls]
            shift = 1
            while shift < C:
                b = b + jnp.where(row >= shift, pltpu.roll(b, shift, axis=0), 0.0)
                shift *= 2
            b_last = b[C - 1:C, :]
            k2 = k_ref[rs, ls].astype(F32)
            qd = (q_ref[rs, ls].astype(F32) * scale * jnp.exp(b)).astype(BF16)
            kd = (k2 * jnp.exp(-b)).astype(BF16)
            ke = (k2 * jnp.exp(b_last - b)).astype(BF16)
            zk = jnp.zeros_like(kd)
            kd_blk = jnp.concatenate([jnp.where(lo_k, kd, zk), jnp.where(lo_k, zk, kd)], axis=0)
            attn = lax.dot_general(qd, kd_blk, nt, preferred_element_type=F32)
            lhs_ref[rs, vs] = jnp.concatenate([jnp.where(causal, attn, 0.0).astype(BF16), qd], axis=1)
            kv = lax.dot_general(ke, v_ref[rs, vs], tn, preferred_element_type=F32)
            kv_ref[p, ci] = jnp.where(own, kv, 0.0)
            dcol = jnp.exp(jnp.sum(jnp.where(eye, jnp.broadcast_to(b_last, (LANES, LANES)), 0.0),
                                   axis=1, keepdims=True))
            dec_ref[p, ci] = jnp.broadcast_to(dcol, (LANES, LANES))

    def scan(ci):
        for p in pairs:
            s_prev = s_ref[p]
            sprev_ref[p, ci] = s_prev.astype(BF16)
            s_ref[p] = jnp.tile(dec_ref[p, ci], (1, HEAD_PAIR)) * s_prev + kv_ref[p, ci]

    def output(ci):
        rs = slice(ci * C, (ci + 1) * C)
        for p in pairs:
            vs = slice(p * W, (p + 1) * W)
            v2 = v_ref[rs, vs]
            zv = jnp.zeros_like(v2)
            v_blk = jnp.concatenate([jnp.where(lo_v, v2, zv), jnp.where(lo_v, zv, v2)], axis=0)
            o = jnp.dot(lhs_ref[rs, vs], jnp.concatenate([v_blk, sprev_ref[p, ci]], axis=0),
                        preferred_element_type=F32)
            o2 = o * o
            ms = jnp.where(lo_v, jnp.sum(o2[:, :GLA_DV_PAD], axis=1, keepdims=True),
                           jnp.sum(o2[:, GLA_DV_PAD:], axis=1, keepdims=True))
            on = o * lax.rsqrt(ms * (1.0 / GLA_DV) + EPS) * ng
            og = (on * gate_ref[rs, vs].astype(F32)).astype(BF16)
            for hh in range(HEAD_PAIR):
                c0 = (p * HEAD_PAIR + hh) * GLA_DV
                o_ref[rs, c0:c0 + GLA_DV] = og[:, hh * GLA_DV_PAD:hh * GLA_DV_PAD + GLA_DV]

    def all_chunks(phase):
        return lambda: [phase(ci) for ci in range(n_chunks)]

    return all_chunks(local), all_chunks(scan), all_chunks(output)


def _fox_kernel(qn2_ref, kn2_ref, cend_ref, q_ref, cq_ref, k_ref, cx_ref, v_ref, gate_ref, o_ref,
                m_ref, acc_ref):
    blk = FOX_KEYS
    streams = range(q_ref.shape[0] // blk)
    pair = pl.program_id(0)
    qi = pl.program_id(1)
    nblk = pl.num_programs(1) * len(streams)
    lane = lax.broadcasted_iota(jnp.int32, (1, LANES), 1)
    lo_lanes = lane < FOX_DH
    reps = blk // LANES
    diag = [qi * len(streams) + s for s in streams]
    heads = [pair * HEAD_PAIR + hh for hh in range(HEAD_PAIR)]

    q = q_ref[...]
    zero = jnp.zeros_like(q)
    q_lo, q_hi = jnp.where(lo_lanes, q, zero), jnp.where(lo_lanes, zero, q)
    q_stack = [jnp.concatenate([q_lo[s * blk:(s + 1) * blk], q_hi[s * blk:(s + 1) * blk]], axis=0)
               for s in streams]
    g_row = lax.broadcasted_iota(jnp.int32, (LANES, LANES), 0)
    g_col = lax.broadcasted_iota(jnp.int32, (LANES, LANES), 1)
    xlane = lax.broadcasted_iota(jnp.int32, (HEAD_PAIR * blk, LANES), 1)
    cq = cq_ref[...]

    def lane_map(h, shift):
        g = jnp.where((g_row == CX_ONE) & ((g_col == CX_HI + h) | (g_col == CX_MID + h)
                                           | (g_col == CX_LO + h)), 1.0, 0.0)
        if shift:
            for part, base in enumerate((CX_HI, CX_MID, CX_LO)):
                g = jnp.where((g_row == base + h) & (g_col == CX_ONE + part), -1.0, g)
        return g.astype(BF16)

    def q_aug(s, shift, void=None):
        cs = cq[s * blk:(s + 1) * blk]
        extra = jnp.concatenate([jnp.dot(cs, lane_map(h, shift), preferred_element_type=F32)
                                 for h in heads], axis=0)
        if void is not None:
            extra = jnp.where(jnp.logical_and(xlane == CX_ONE + 3, void), NEG_BIG, extra)
        return jnp.concatenate([q_stack[s], extra.astype(BF16)], axis=1)

    acc_ref[...] = jnp.zeros_like(acc_ref)
    qpos = lax.broadcasted_iota(jnp.int32, (blk, blk), 0)
    kpos = lax.broadcasted_iota(jnp.int32, (blk, blk), 1)

    def block(j):
        ks = pl.ds(pl.multiple_of(jnp.maximum(j, 0) * blk, blk), blk)
        k_aug = jnp.concatenate([k_ref[:, ks], cx_ref[:, ks]], axis=0)
        vb = v_ref[ks, :]
        one = jnp.ones_like(vb)
        return k_aug, (jnp.where(lo_lanes, vb, one), jnp.where(lo_lanes, one, vb))

    def step_online(s, j, qa, masked):
        rows = slice(s * blk, (s + 1) * blk)
        k_aug, vaug = block(j)
        s_all = jnp.dot(qa, k_aug, preferred_element_type=F32)
        for hh in range(HEAD_PAIR):
            sc = s_all[hh * blk:(hh + 1) * blk]
            if masked:
                sc = jnp.where(kpos <= qpos, sc, NEG_BIG)
            m_prev = m_ref[hh, rows]
            m_new = jnp.maximum(m_prev, jnp.max(sc, axis=1, keepdims=True))
            p = jnp.exp(sc - jnp.tile(m_new, (1, reps)))
            alpha = jnp.exp(m_prev - m_new)
            pv = jnp.dot(p.astype(BF16), vaug[hh], preferred_element_type=F32)
            acc_ref[hh, rows] = alpha * acc_ref[hh, rows] + pv
            m_ref[hh, rows] = m_new

    def step_direct(s, j, qa, masked, only=None):
        rows = slice(s * blk, (s + 1) * blk)
        k_aug, vaug = block(j)
        hsel = range(HEAD_PAIR) if only is None else (only,)
        lhs = qa if only is None else qa[only * blk:(only + 1) * blk]
        s_all = jnp.dot(lhs, k_aug, preferred_element_type=F32)
        for n, hh in enumerate(hsel):
            sc = s_all[n * blk:(n + 1) * blk]
            if masked:
                sc = jnp.where(kpos <= qpos, sc, NEG_BIG)
            acc_ref[hh, rows] += jnp.dot(jnp.exp(sc).astype(BF16), vaug[hh],
                                         preferred_element_type=F32)

    k_max2 = [kn2_ref[h] for h in heads]

    def norm2(s, hh):
        stat = (diag[s] * blk // PROJ_ROWS) * LANES
        return (NORM_SLACK * NORM_SLACK) * qn2_ref[stat + heads[hh]] * k_max2[hh]

    def live_head(t, hh):
        keep = False
        base = heads[hh] * nblk
        for s in streams:
            j = diag[s] - 1 - t
            jj = jnp.maximum(j, 0)
            prev = jnp.maximum(diag[s] - 1, 0)
            gap = FOX_SKIP_NATS + cend_ref[base + prev] - cend_ref[base + jj]
            dead = jnp.logical_and(gap <= 0.0, norm2(s, hh) <= gap * gap)
            keep = jnp.logical_or(keep, jnp.logical_and(j >= 0, jnp.logical_not(dead)))
        return keep

    def sweeps(stepper, per_head):
        def sweep(back, masked=False, only=None):
            for s in streams:
                stepper(s, diag[s] - back, masked, only)

        def loop(t0, cond, only=None):
            def body(t):
                sweep(t + 1, only=only)
                return t + 1
            return lax.while_loop(cond, body, t0)

        sweep(0, masked=True)
        sweep(1)
        if per_head:
            t_both = loop(1, lambda t: jnp.logical_and(live_head(t, 0), live_head(t, 1)))
            for hh in range(HEAD_PAIR):
                loop(t_both, lambda t, hh=hh: live_head(t, hh), only=hh)
        else:
            loop(1, lambda t: jnp.logical_or(live_head(t, 0), live_head(t, 1)))

    direct_ok = True
    for s in streams:
        for hh in range(HEAD_PAIR):
            direct_ok = jnp.logical_and(direct_ok, norm2(s, hh) <= FOX_DIRECT_NORM2)

    @pl.when(direct_ok)
    def _():
        q_dir = [q_aug(s, shift=True) for s in streams]
        q_void = [q_aug(s, shift=True, void=True) for s in streams]
        sweeps(lambda s, j, masked, only: step_direct(
            s, j, q_dir[s] if masked else jnp.where(j >= 0, q_dir[s], q_void[s]), masked, only),
            per_head=True)

    @pl.when(jnp.logical_not(direct_ok))
    def _():
        m_ref[...] = jnp.full_like(m_ref, NEG_BIG)
        sweeps(lambda s, j, masked, only: step_online(
            s, j, q_aug(s, shift=False, void=j < 0), masked), per_head=False)

    outs = []
    for hh in range(HEAD_PAIR):
        acc = acc_ref[hh]
        outs.append(acc / pltpu.roll(acc, FOX_DH, axis=1))
    o = jnp.where(lo_lanes, outs[0], outs[1])
    o_ref[...] = (o * gate_ref[...].astype(F32)).astype(BF16)


def _out_kernel(x_ref, gla_ref, fox_ref, mq_ref, mg_ref, mk_ref, mv_ref,
                w_out_ref, fg_ref, o_ref, wo_ref):
    @pl.when(pl.program_id(0) == 0)
    def _():
        wo_ref[...] = w_out_ref[...].astype(BF16)

    lane = lax.broadcasted_iota(jnp.int32, (1, LANES), 1)
    lo_lanes = lane < MEM_DH
    nt = (((1,), (1,)), ((), ()))
    mem_parts = []
    for p in range(MEM_HEADS // HEAD_PAIR):
        ls = slice(p * LANES, (p + 1) * LANES)
        q = mq_ref[:, ls]
        kb = mk_ref[:, ls]
        vb = mv_ref[:, ls]
        zero = jnp.zeros_like(q)
        one = jnp.ones_like(vb)
        qh = (jnp.where(lo_lanes, q, zero), jnp.where(lo_lanes, zero, q))
        vaug = (jnp.where(lo_lanes, vb, one), jnp.where(lo_lanes, one, vb))
        outs = []
        for hh in range(HEAD_PAIR):
            s = lax.dot_general(qh[hh], kb, nt, preferred_element_type=F32)
            pexp = jnp.exp(s - jnp.max(s, axis=1, keepdims=True))
            pv = jnp.dot(pexp.astype(BF16), vaug[hh], preferred_element_type=F32)
            outs.append(pv / pltpu.roll(pv, MEM_DH, axis=1))
        o = jnp.where(lo_lanes, outs[0], outs[1])
        mem_parts.append((o * mg_ref[:, ls].astype(F32)).astype(BF16))
    mixed = jnp.concatenate([gla_ref[...], fox_ref[...]] + mem_parts, axis=1)
    y = x_ref[...] + jnp.dot(mixed, wo_ref[...], preferred_element_type=F32)
    o_ref[...] = y * _rms_scale(y, D_MODEL) * fg_ref[...]


def _params(*sem):
    return pltpu.CompilerParams(dimension_semantics=sem, vmem_limit_bytes=VMEM_LIMIT)


def _layer(x, mem, norm_g, w_in, w_alpha_up, b_alpha, b_forget, gla_norm_g,
           mem_norm_g, w_mem_kv, w_out, out_g):
    w_in_t = jnp.transpose(w_in[None], (0, 2, 1)).reshape(-1, LANES)
    T = x.shape[0]
    M = mem.shape[0]

    def rows(width, n=PROJ_ROWS):
        return pl.BlockSpec((n, width), lambda i: (i, 0))

    def whole(shape):
        return pl.BlockSpec(shape, lambda i: (0,) * len(shape))

    bshape = lambda w: jax.ShapeDtypeStruct((T, w), BF16)
    nproj = T // PROJ_ROWS
    stat_spec = pl.BlockSpec((1, 1, LANES), lambda i: (i, 0, 0))
    stat_shape = jax.ShapeDtypeStruct((nproj, 1, LANES), F32)
    gla_pairs, gla_chunks = GLA_HEADS // HEAD_PAIR, PROJ_ROWS // GLA_CHUNK
    pair_w = HEAD_PAIR * GLA_DV_PAD
    (gla, fq, fk, fv, fgate, mq, mg, crow, cx, cq, qn2, kn2) = pl.pallas_call(
        _proj_kernel,
        grid=(nproj,),
        in_specs=[rows(D_MODEL), whole((1, D_MODEL)),
                  pl.BlockSpec(w_in_t.shape, lambda i: (0, 0), pipeline_mode=pl.Buffered(1)),
                  whole(w_alpha_up.shape), whole((1, GLA_HEADS * GLA_DK)),
                  whole((1, FOX_HEADS)), whole((1, GLA_DV))],
        out_specs=[rows(GLA_OUT_W),
                   rows(FOX_W), pl.BlockSpec((FOX_W, PROJ_ROWS), lambda i: (0, i)),
                   rows(FOX_W), rows(FOX_W),
                   rows(MEM_W), rows(MEM_W),
                   pl.BlockSpec((SUBLANES, PROJ_ROWS), lambda i: (0, i)),
                   pl.BlockSpec((LANES, PROJ_ROWS), lambda i: (0, i)), rows(LANES),
                   stat_spec, pl.BlockSpec((1, 1, LANES), lambda i: (0, 0, 0))],
        out_shape=[bshape(GLA_OUT_W),
                   bshape(FOX_W), jax.ShapeDtypeStruct((FOX_W, T), BF16),
                   bshape(FOX_W), bshape(FOX_W),
                   bshape(MEM_W), bshape(MEM_W),
                   jax.ShapeDtypeStruct((SUBLANES, T), F32),
                   jax.ShapeDtypeStruct((LANES, T), BF16), bshape(LANES),
                   stat_shape, jax.ShapeDtypeStruct((1, 1, LANES), F32)],
        scratch_shapes=[
            pltpu.VMEM((SUBLANES, LANES), F32),
            pltpu.VMEM((IN_COLS_PAD, D_MODEL), BF16),
            pltpu.VMEM((SMALL_W, GLA_QK_W), F32),
            pltpu.VMEM((1, GLA_QK_W), F32),
            pltpu.VMEM((1, SMALL_W), F32),
            pltpu.VMEM((1, GLA_DV_PAD), F32),
            pltpu.VMEM((FOX_W, LANES), BF16),
            pltpu.VMEM((PROJ_ROWS, GLA_QK_W), BF16),
            pltpu.VMEM((PROJ_ROWS, GLA_QK_W), BF16),
            pltpu.VMEM((PROJ_ROWS, GLA_V_W), BF16),
            pltpu.VMEM((PROJ_ROWS, GLA_V_W), BF16),
            pltpu.VMEM((PROJ_ROWS, GLA_QK_W), F32),
            pltpu.VMEM((gla_pairs, LANES, pair_w), F32),
            pltpu.VMEM((PROJ_ROWS, GLA_V_W), BF16),
            pltpu.VMEM((gla_pairs, gla_chunks, LANES, pair_w), F32),
            pltpu.VMEM((gla_pairs, gla_chunks, LANES, LANES), F32),
            pltpu.VMEM((gla_pairs, gla_chunks, LANES, pair_w), BF16)],
        compiler_params=_params("arbitrary"),
        name="proj",
    )(x, norm_g[None, :], w_in_t, w_alpha_up, b_alpha[None, :], b_forget[None, :],
      gla_norm_g[None, :])

    mk, mv = pl.pallas_call(
        _memkv_kernel,
        out_shape=[jax.ShapeDtypeStruct((M, MEM_W), BF16)] * 2,
        compiler_params=pltpu.CompilerParams(vmem_limit_bytes=VMEM_LIMIT),
        name="memkv",
    )(mem, mem_norm_g[None, :], w_mem_kv)

    cend = crow[:FOX_HEADS, FOX_KEYS - 1::FOX_KEYS].reshape(-1)
    pair_rows = pl.BlockSpec((FOX_BLOCK, LANES), lambda p, i, *_: (i, p))
    pair_all = pl.BlockSpec((T, LANES), lambda p, i, *_: (0, p))
    fox = pl.pallas_call(
        _fox_kernel,
        grid_spec=pltpu.PrefetchScalarGridSpec(
            num_scalar_prefetch=3,
            grid=(FOX_HEADS // HEAD_PAIR, T // FOX_BLOCK),
            in_specs=[pair_rows, pl.BlockSpec((FOX_BLOCK, LANES), lambda p, i, *_: (i, 0)),
                      pl.BlockSpec((LANES, T), lambda p, i, *_: (p, 0)),
                      pl.BlockSpec((LANES, T), lambda p, i, *_: (0, 0)), pair_all, pair_rows],
            out_specs=pair_rows,
            scratch_shapes=[pltpu.VMEM((HEAD_PAIR, FOX_BLOCK, LANES), F32),
                            pltpu.VMEM((HEAD_PAIR, FOX_BLOCK, LANES), F32)]),
        out_shape=bshape(FOX_W),
        compiler_params=_params("arbitrary", "arbitrary"),
        name="fox",
    )(qn2.reshape(-1), kn2.reshape(-1), cend, fq, cq, fk, cx, fv, fgate)

    out = pl.pallas_call(
        _out_kernel,
        grid=(T // OUT_ROWS,),
        in_specs=[rows(D_MODEL, OUT_ROWS), rows(GLA_OUT_W, OUT_ROWS), rows(FOX_W, OUT_ROWS),
                  rows(MEM_W, OUT_ROWS), rows(MEM_W, OUT_ROWS),
                  whole((M, MEM_W)), whole((M, MEM_W)),
                  pl.BlockSpec(w_out.shape, lambda i: (0, 0), pipeline_mode=pl.Buffered(1)),
                  whole((1, D_MODEL))],
        out_specs=rows(D_MODEL, OUT_ROWS),
        out_shape=jax.ShapeDtypeStruct((T, D_MODEL), F32),
        scratch_shapes=[pltpu.VMEM((D_MODEL, D_MODEL), BF16)],
        compiler_params=_params("arbitrary"),
        name="out",
    )(x, gla, fox, mq, mg, mk, mv, w_out, out_g[None, :])
    return out


def kernel(x, mem, norm_g, w_in, w_alpha_up, b_alpha, b_forget, gla_norm_g, mem_norm_g,
           w_mem_kv, w_out, final_norm_g):
    assert x.shape[0] == 1 and mem.shape[0] == 1 and norm_g.shape[0] == 1
    assert x.shape[1] % max(PROJ_ROWS, FOX_BLOCK, OUT_ROWS) == 0
    out = _layer(x[0], mem[0], norm_g[0], w_in[0], w_alpha_up[0], b_alpha[0], b_forget[0],
                 gla_norm_g[0], mem_norm_g[0], w_mem_kv[0], w_out[0], final_norm_g)
    return out[None]
```

```python
import functools

import jax
import jax.numpy as jnp
from jax import lax
from jax.experimental import pallas as pl
from jax.experimental.pallas import tpu as pltpu

F32 = jnp.float32
BF16 = jnp.bfloat16

EPS = 1e-6
LANES = 128
SUBLANES = 8

D_MODEL = 1024
GLA_HEADS, GLA_DK, GLA_DV, GLA_RANK = 4, 48, 96, 16
GLA_DK_PAD = 64
GLA_DV_PAD = LANES
GLA_GATE_NORM = 16.0
GLA_CHUNK = 64
FOX_HEADS, FOX_DH = 6, 64
MEM_HEADS, MEM_DH = 4, 64
HEAD_PAIR = 2
GLA_QK_W = GLA_HEADS * GLA_DK_PAD
GLA_V_W = GLA_HEADS * GLA_DV_PAD
GLA_OUT_W = GLA_HEADS * GLA_DV
FOX_W = FOX_HEADS * FOX_DH
MEM_W = MEM_HEADS * MEM_DH
SMALL_W = LANES
FG_LANE0 = 0
LR_LANE0 = SUBLANES

_GROUPS = (("gq", GLA_QK_W), ("gk", GLA_QK_W), ("gv", GLA_OUT_W), ("gg", GLA_OUT_W),
           ("fq", FOX_W), ("fk", FOX_W), ("fv", FOX_W), ("fgate", FOX_W),
           ("mq", MEM_W), ("mg", MEM_W), ("small", SMALL_W))
_OFF = {}
_o = 0
for _n, _w in _GROUPS:
    _OFF[_n] = (_o, _o + _w)
    _o += _w
IN_COLS_PAD = _o

PROJ_ROWS = 1024
FOX_BLOCK = 2048
FOX_KEYS = 256
OUT_ROWS = 1024
VMEM_LIMIT = 56 * 1024 * 1024

NEG_BIG = -1e30
FOX_SKIP_NATS = 105.0
NORM_SLACK = 1.02
FOX_DIRECT_NORM2 = 3600.0
CX_HI, CX_MID, CX_LO, CX_ONE = 0, 8, 16, 24


def _log_sigmoid(z):
    return jnp.minimum(z, 0.0) - jnp.log(1.0 + jnp.exp(-jnp.abs(z)))


def _silu(z):
    return z / (1.0 + jnp.exp(-z))


def _rms_scale(v, width):
    return lax.rsqrt(jnp.sum(v * v, axis=-1, keepdims=True) * (1.0 / width) + EPS)


def _w_in_segments():
    qk, gw = GLA_HEADS * GLA_DK, GLA_HEADS * GLA_DV
    src = {}
    o = 0
    for name, width in (("gq", qk), ("gk", qk), ("gv", gw), ("lr", GLA_RANK), ("gg", gw),
                        ("fq", FOX_W), ("fk", FOX_W), ("fv", FOX_W), ("fg", FOX_HEADS),
                        ("fgate", FOX_W), ("mq", MEM_W), ("mg", MEM_W)):
        src[name] = o
        o += width
    segs = []
    for name, d, d_pad in (("gq", GLA_DK, GLA_DK_PAD), ("gk", GLA_DK, GLA_DK_PAD)):
        segs += [(src[name] + h * d, _OFF[name][0] + h * d_pad, d) for h in range(GLA_HEADS)]
    segs += [(src[name], _OFF[name][0], _OFF[name][1] - _OFF[name][0])
             for name in ("gv", "gg", "fq", "fk", "fv", "fgate", "mq", "mg")]
    segs += [(src["fg"], _OFF["small"][0] + FG_LANE0, FOX_HEADS),
             (src["lr"], _OFF["small"][0] + LR_LANE0, GLA_RANK)]
    return tuple(segs)


def _proj_kernel(x_ref, g_ref, w_in_ref, w_alpha_ref, b_alpha_ref, b_forget_ref, gla_g_ref,
                 gla_ref, fq_ref, fk_ref, fv_ref, fgate_ref,
                 mq_ref, mg_ref, crow_ref, cx_ref, cq_ref, qn2_ref, kn2_ref,
                 carry_ref, wt_ref, wa_ref, ba_ref, bf_ref, ng_ref, seg_ref,
                 gq_ref, gk_ref, gv_ref, gg_ref, loga_ref,
                 s_ref, lhs_ref, kv_ref, dec_ref, sprev_ref):
    rows = x_ref.shape[0]
    k_chunks = D_MODEL // LANES

    @pl.when(pl.program_id(0) == 0)
    def _():
        carry_ref[...] = jnp.zeros_like(carry_ref)
        s_ref[...] = jnp.zeros_like(s_ref)
        kn2_ref[...] = jnp.zeros_like(kn2_ref)
        gv_ref[...] = jnp.zeros_like(gv_ref)
        gg_ref[...] = jnp.zeros_like(gg_ref)
        wa_ref[...] = jnp.zeros_like(wa_ref)
        ba_ref[...] = jnp.zeros_like(ba_ref)
        bf_ref[...] = jnp.zeros_like(bf_ref)
        ng_ref[...] = jnp.zeros_like(ng_ref)
        for h in range(GLA_HEADS):
            src = slice(h * GLA_DK, (h + 1) * GLA_DK)
            dst = slice(h * GLA_DK_PAD, h * GLA_DK_PAD + GLA_DK)
            wa_ref[LR_LANE0:LR_LANE0 + GLA_RANK, dst] = w_alpha_ref[:, src]
            ba_ref[:, dst] = b_alpha_ref[:, src]
        bf_ref[:, FG_LANE0:FG_LANE0 + FOX_HEADS] = b_forget_ref[...]
        ng_ref[:, 0:GLA_DV] = gla_g_ref[...]
        seg_ref[...] = (lax.broadcasted_iota(jnp.int32, seg_ref.shape, 0) // FOX_DH
                        == lax.broadcasted_iota(jnp.int32, seg_ref.shape, 1)).astype(BF16)
        wt_ref[...] = jnp.zeros_like(wt_ref)
        for s0, d0, width in _w_in_segments():
            for c in range(k_chunks):
                wt_ref[d0:d0 + width, c * LANES:(c + 1) * LANES] = (
                    w_in_ref[pl.ds(s0 * k_chunks + c, width, stride=k_chunks), :].astype(BF16))

    x = x_ref[...]
    xn = (x * _rms_scale(x, D_MODEL) * g_ref[...]).astype(BF16)
    nt = (((1,), (1,)), ((), ()))

    def proj(first, last):
        lo, hi = _OFF[first][0], _OFF[last][1]
        y = lax.dot_general(xn, wt_ref[lo:hi, :], nt, preferred_element_type=F32)
        return lambda name: y[:, _OFF[name][0] - lo:_OFF[name][1] - lo]

    tail = proj("mq", "small")
    gla = proj("gq", "gg")
    small = tail("small")
    logf = _log_sigmoid(small + bf_ref[...])
    c = logf.T[0:SUBLANES, :]
    lane = lax.broadcasted_iota(jnp.int32, c.shape, 1)
    shift = 1
    while shift < rows:
        c = c + jnp.where(lane >= shift, pltpu.roll(c, shift, axis=1), 0.0)
        shift *= 2
    c = c + carry_ref[:, 0:1]
    crow_ref[...] = c
    carry_ref[...] = jnp.broadcast_to(c[:, rows - 1:rows], carry_ref.shape)
    neg = -c
    hi = neg.astype(BF16).astype(F32)
    mid = (neg - hi).astype(BF16).astype(F32)
    low = neg - hi - mid
    parts = jnp.concatenate(
        [hi, mid, low, jnp.ones_like(c), jnp.zeros((LANES - 4 * SUBLANES, rows), F32)], axis=0)
    cx_ref[...] = parts.astype(BF16)
    cq_ref[...] = parts.T.astype(BF16)

    z = (jnp.dot(small.astype(BF16), wa_ref[...].astype(BF16), preferred_element_type=F32)
         + ba_ref[...])
    loga_ref[...] = _log_sigmoid(z) * (1.0 / GLA_GATE_NORM)
    gq_ref[...] = gla("gq").astype(BF16)
    gk_ref[...] = gla("gk").astype(BF16)
    gv, gg = gla("gv").astype(BF16), _silu(gla("gg")).astype(BF16)
    for h in range(GLA_HEADS):
        src = slice(h * GLA_DV, (h + 1) * GLA_DV)
        dst = slice(h * GLA_DV_PAD, h * GLA_DV_PAD + GLA_DV)
        gv_ref[:, dst] = gv[:, src]
        gg_ref[:, dst] = gg[:, src]
    gla_local, gla_scan, gla_output = _gla_block(
        gq_ref, gk_ref, gv_ref, loga_ref, gg_ref, ng_ref, gla_ref,
        s_ref, lhs_ref, kv_ref, dec_ref, sprev_ref)

    def max_sq_norm(v):
        v32 = v.astype(F32)
        n2 = jnp.dot((v32 * v32).astype(BF16), seg_ref[...], preferred_element_type=F32)
        return jnp.max(n2, axis=0, keepdims=True)

    gla_local()
    fox_qk = proj("fq", "fk")
    mq_ref[...] = (tail("mq") * MEM_DH ** -0.5).astype(BF16)
    mg_ref[...] = _silu(tail("mg")).astype(BF16)
    gla_scan()
    fq = (fox_qk("fq") * FOX_DH ** -0.5).astype(BF16)
    fk = fox_qk("fk").astype(BF16)
    fq_ref[...] = fq
    fk_ref[...] = fox_qk("fk").T.astype(BF16)
    gla_output()
    fox_vg = proj("fv", "fgate")
    fv_ref[...] = fox_vg("fv").astype(BF16)
    fgate_ref[...] = _silu(fox_vg("fgate")).astype(BF16)
    qn2_ref[0] = max_sq_norm(fq)
    kn2_ref[0] = jnp.maximum(kn2_ref[0], max_sq_norm(fk))


def _memkv_kernel(mem_ref, g_ref, w_ref, mk_ref, mv_ref):
    m = mem_ref[...]
    mn = (m * _rms_scale(m, D_MODEL) * g_ref[...]).astype(BF16)
    kv = jnp.dot(mn, w_ref[...].astype(BF16), preferred_element_type=F32)
    mk_ref[...] = kv[:, :MEM_W].astype(BF16)
    mv_ref[...] = kv[:, MEM_W:].astype(BF16)


def _gla_block(q_ref, k_ref, v_ref, loga_ref, gate_ref, ng_ref, o_ref,
               s_ref, lhs_ref, kv_ref, dec_ref, sprev_ref):
    C = GLA_CHUNK
    W = HEAD_PAIR * GLA_DV_PAD
    n_chunks = q_ref.shape[0] // C

    row = lax.broadcasted_iota(jnp.int32, (C, LANES), 0)
    lane = lax.broadcasted_iota(jnp.int32, (C, LANES), 1)
    lo_k = lane < GLA_DK_PAD
    causal = row >= jnp.where(lo_k, lane, lane - GLA_DK_PAD)
    lo_v = lax.broadcasted_iota(jnp.int32, (C, W), 1) < GLA_DV_PAD
    st_row = lax.broadcasted_iota(jnp.int32, (LANES, W), 0)
    st_lane = lax.broadcasted_iota(jnp.int32, (LANES, W), 1)
    own = (st_row < GLA_DK_PAD) == (st_lane < GLA_DV_PAD)
    eye = (lax.broadcasted_iota(jnp.int32, (LANES, LANES), 0)
           == lax.broadcasted_iota(jnp.int32, (LANES, LANES), 1))
    scale = GLA_DK ** -0.5
    nt = (((1,), (1,)), ((), ()))
    tn = (((0,), (0,)), ((), ()))
    ng = jnp.concatenate([ng_ref[...]] * HEAD_PAIR, axis=1)

    pairs = range(GLA_HEADS // HEAD_PAIR)

    def local(ci):
        rs = slice(ci * C, (ci + 1) * C)
        for p in pairs:
            ls = slice(p * LANES, (p + 1) * LANES)
            vs = slice(p * W, (p + 1) * W)
            b = loga_ref[rs, ls]
            shift = 1
            while shift < C:
                b = b + jnp.where(row >= shift, pltpu.roll(b, shift, axis=0), 0.0)
                shift *= 2
            b_last = b[C - 1:C, :]
            k2 = k_ref[rs, ls].astype(F32)
            qd = (q_ref[rs, ls].astype(F32) * scale * jnp.exp(b)).astype(BF16)
            kd = (k2 * jnp.exp(-b)).astype(BF16)
            ke = (k2 * jnp.exp(b_last - b)).astype(BF16)
            zk = jnp.zeros_like(kd)
            kd_blk = jnp.concatenate([jnp.where(lo_k, kd, zk), jnp.where(lo_k, zk, kd)], axis=0)
            attn = lax.dot_general(qd, kd_blk, nt, preferred_element_type=F32)
            lhs_ref[rs, vs] = jnp.concatenate([jnp.where(causal, attn, 0.0).astype(BF16), qd], axis=1)
            kv = lax.dot_general(ke, v_ref[rs, vs], tn, preferred_element_type=F32)
            kv_ref[p, ci] = jnp.where(own, kv, 0.0)
            dcol = jnp.exp(jnp.sum(jnp.where(eye, jnp.broadcast_to(b_last, (LANES, LANES)), 0.0),
                                   axis=1, keepdims=True))
            dec_ref[p, ci] = jnp.broadcast_to(dcol, (LANES, LANES))

    def scan(ci):
        for p in pairs:
            s_prev = s_ref[p]
            sprev_ref[p, ci] = s_prev.astype(BF16)
            s_ref[p] = jnp.tile(dec_ref[p, ci], (1, HEAD_PAIR)) * s_prev + kv_ref[p, ci]

    def output(ci):
        rs = slice(ci * C, (ci + 1) * C)
        for p in pairs:
            vs = slice(p * W, (p + 1) * W)
            v2 = v_ref[rs, vs]
            zv = jnp.zeros_like(v2)
            v_blk = jnp.concatenate([jnp.where(lo_v, v2, zv), jnp.where(lo_v, zv, v2)], axis=0)
            o = jnp.dot(lhs_ref[rs, vs], jnp.concatenate([v_blk, sprev_ref[p, ci]], axis=0),
                        preferred_element_type=F32)
            o2 = o * o
            ms = jnp.where(lo_v, jnp.sum(o2[:, :GLA_DV_PAD], axis=1, keepdims=True),
                           jnp.sum(o2[:, GLA_DV_PAD:], axis=1, keepdims=True))
            on = o * lax.rsqrt(ms * (1.0 / GLA_DV) + EPS) * ng
            og = (on * gate_ref[rs, vs].astype(F32)).astype(BF16)
            for hh in range(HEAD_PAIR):
                c0 = (p * HEAD_PAIR + hh) * GLA_DV
                o_ref[rs, c0:c0 + GLA_DV] = og[:, hh * GLA_DV_PAD:hh * GLA_DV_PAD + GLA_DV]

    def all_chunks(phase):
        return lambda: [phase(ci) for ci in range(n_chunks)]

    return all_chunks(local), all_chunks(scan), all_chunks(output)


def _fox_kernel(qn2_ref, kn2_ref, cend_ref, q_ref, cq_ref, k_ref, cx_ref, v_ref, gate_ref, o_ref,
                m_ref, acc_ref):
    blk = FOX_KEYS
    streams = range(q_ref.shape[0] // blk)
    pair = pl.program_id(0)
    qi = pl.program_id(1)
    nblk = pl.num_programs(1) * len(streams)
    lane = lax.broadcasted_iota(jnp.int32, (1, LANES), 1)
    lo_lanes = lane < FOX_DH
    reps = blk // LANES
    diag = [qi * len(streams) + s for s in streams]
    heads = [pair * HEAD_PAIR + hh for hh in range(HEAD_PAIR)]

    q = q_ref[...]
    zero = jnp.zeros_like(q)
    q_lo, q_hi = jnp.where(lo_lanes, q, zero), jnp.where(lo_lanes, zero, q)
    q_stack = [jnp.concatenate([q_lo[s * blk:(s + 1) * blk], q_hi[s * blk:(s + 1) * blk]], axis=0)
               for s in streams]
    g_row = lax.broadcasted_iota(jnp.int32, (LANES, LANES), 0)
    g_col = lax.broadcasted_iota(jnp.int32, (LANES, LANES), 1)
    xlane = lax.broadcasted_iota(jnp.int32, (HEAD_PAIR * blk, LANES), 1)
    cq = cq_ref[...]

    def lane_map(h, shift):
        g = jnp.where((g_row == CX_ONE) & ((g_col == CX_HI + h) | (g_col == CX_MID + h)
                                           | (g_col == CX_LO + h)), 1.0, 0.0)
        if shift:
            for part, base in enumerate((CX_HI, CX_MID, CX_LO)):
                g = jnp.where((g_row == base + h) & (g_col == CX_ONE + part), -1.0, g)
        return g.astype(BF16)

    def q_aug(s, shift, void=None):
        cs = cq[s * blk:(s + 1) * blk]
        extra = jnp.concatenate([jnp.dot(cs, lane_map(h, shift), preferred_element_type=F32)
                                 for h in heads], axis=0)
        if void is not None:
            extra = jnp.where(jnp.logical_and(xlane == CX_ONE + 3, void), NEG_BIG, extra)
        return jnp.concatenate([q_stack[s], extra.astype(BF16)], axis=1)

    acc_ref[...] = jnp.zeros_like(acc_ref)
    qpos = lax.broadcasted_iota(jnp.int32, (blk, blk), 0)
    kpos = lax.broadcasted_iota(jnp.int32, (blk, blk), 1)

    def block(j):
        ks = pl.ds(pl.multiple_of(jnp.maximum(j, 0) * blk, blk), blk)
        k_aug = jnp.concatenate([k_ref[:, ks], cx_ref[:, ks]], axis=0)
        vb = v_ref[ks, :]
        one = jnp.ones_like(vb)
        return k_aug, (jnp.where(lo_lanes, vb, one), jnp.where(lo_lanes, one, vb))

    def step_online(s, j, qa, masked):
        rows = slice(s * blk, (s + 1) * blk)
        k_aug, vaug = block(j)
        s_all = jnp.dot(qa, k_aug, preferred_element_type=F32)
        for hh in range(HEAD_PAIR):
            sc = s_all[hh * blk:(hh + 1) * blk]
            if masked:
                sc = jnp.where(kpos <= qpos, sc, NEG_BIG)
            m_prev = m_ref[hh, rows]
            m_new = jnp.maximum(m_prev, jnp.max(sc, axis=1, keepdims=True))
            p = jnp.exp(sc - jnp.tile(m_new, (1, reps)))
            alpha = jnp.exp(m_prev - m_new)
            pv = jnp.dot(p.astype(BF16), vaug[hh], preferred_element_type=F32)
            acc_ref[hh, rows] = alpha * acc_ref[hh, rows] + pv
            m_ref[hh, rows] = m_new

    def step_direct(s, j, qa, masked, only=None):
        rows = slice(s * blk, (s + 1) * blk)
        k_aug, vaug = block(j)
        hsel = range(HEAD_PAIR) if only is None else (only,)
        lhs = qa if only is None else qa[only * blk:(only + 1) * blk]
        s_all = jnp.dot(lhs, k_aug, preferred_element_type=F32)
        for n, hh in enumerate(hsel):
            sc = s_all[n * blk:(n + 1) * blk]
            if masked:
                sc = jnp.where(kpos <= qpos, sc, NEG_BIG)
            acc_ref[hh, rows] += jnp.dot(jnp.exp(sc).astype(BF16), vaug[hh],
                                         preferred_element_type=F32)

    k_max2 = [kn2_ref[h] for h in heads]

    def norm2(s, hh):
        stat = (diag[s] * blk // PROJ_ROWS) * LANES
        return (NORM_SLACK * NORM_SLACK) * qn2_ref[stat + heads[hh]] * k_max2[hh]

    n2 = [[norm2(s, hh) for hh in range(HEAD_PAIR)] for s in streams]
    gap0 = [[FOX_SKIP_NATS + cend_ref[heads[hh] * nblk + jnp.maximum(diag[s] - 1, 0)]
             for hh in range(HEAD_PAIR)] for s in streams]

    def live_head(t, hh):
        keep = False
        for s in streams:
            j = diag[s] - 1 - t
            gap = gap0[s][hh] - cend_ref[heads[hh] * nblk + jnp.maximum(j, 0)]
            dead = jnp.logical_and(gap <= 0.0, n2[s][hh] <= gap * gap)
            keep = jnp.logical_or(keep, jnp.logical_and(j >= 0, jnp.logical_not(dead)))
        return keep

    def sweeps(stepper, per_head):
        def sweep(back, masked=False, only=None):
            for s in streams:
                stepper(s, diag[s] - back, masked, only)

        def loop(t0, cond, only=None):
            def body(t):
                sweep(t + 1, only=only)
                return t + 1
            return lax.while_loop(cond, body, t0)

        sweep(0, masked=True)
        sweep(1)
        if per_head:
            t_both = loop(1, lambda t: jnp.logical_and(live_head(t, 0), live_head(t, 1)))
            for hh in range(HEAD_PAIR):
                loop(t_both, lambda t, hh=hh: live_head(t, hh), only=hh)
        else:
            loop(1, lambda t: jnp.logical_or(live_head(t, 0), live_head(t, 1)))

    direct_ok = True
    for s in streams:
        for hh in range(HEAD_PAIR):
            direct_ok = jnp.logical_and(direct_ok, n2[s][hh] <= FOX_DIRECT_NORM2)

    @pl.when(direct_ok)
    def _():
        q_dir = [q_aug(s, shift=True) for s in streams]
        q_void = [q_aug(s, shift=True, void=True) for s in streams]
        sweeps(lambda s, j, masked, only: step_direct(
            s, j, q_dir[s] if masked else jnp.where(j >= 0, q_dir[s], q_void[s]), masked, only),
            per_head=True)

    @pl.when(jnp.logical_not(direct_ok))
    def _():
        m_ref[...] = jnp.full_like(m_ref, NEG_BIG)
        sweeps(lambda s, j, masked, only: step_online(
            s, j, q_aug(s, shift=False, void=j < 0), masked), per_head=False)

    outs = []
    for hh in range(HEAD_PAIR):
        acc = acc_ref[hh]
        outs.append(acc / pltpu.roll(acc, FOX_DH, axis=1))
    o = jnp.where(lo_lanes, outs[0], outs[1])
    o_ref[...] = (o * gate_ref[...].astype(F32)).astype(BF16)


def _out_kernel(x_ref, gla_ref, fox_ref, mq_ref, mg_ref, mk_ref, mv_ref,
                w_out_ref, fg_ref, o_ref, wo_ref):
    @pl.when(pl.program_id(0) == 0)
    def _():
        wo_ref[...] = w_out_ref[...].astype(BF16)

    lane = lax.broadcasted_iota(jnp.int32, (1, LANES), 1)
    lo_lanes = lane < MEM_DH
    nt = (((1,), (1,)), ((), ()))
    pairs = range(MEM_HEADS // HEAD_PAIR)
    scores = []
    for p in pairs:
        ls = slice(p * LANES, (p + 1) * LANES)
        q = mq_ref[:, ls]
        zero = jnp.zeros_like(q)
        scores.append([lax.dot_general(qh, mk_ref[:, ls], nt, preferred_element_type=F32)
                       for qh in (jnp.where(lo_lanes, q, zero), jnp.where(lo_lanes, zero, q))])
    done = GLA_OUT_W + FOX_W
    y = x_ref[...] + jnp.dot(jnp.concatenate([gla_ref[...], fox_ref[...]], axis=1),
                             wo_ref[0:done, :], preferred_element_type=F32)
    mem_parts = []
    for p in pairs:
        ls = slice(p * LANES, (p + 1) * LANES)
        vb = mv_ref[:, ls]
        one = jnp.ones_like(vb)
        vaug = (jnp.where(lo_lanes, vb, one), jnp.where(lo_lanes, one, vb))
        outs = []
        for hh in range(HEAD_PAIR):
            s = scores[p][hh]
            pexp = jnp.exp(s - jnp.max(s, axis=1, keepdims=True))
            pv = jnp.dot(pexp.astype(BF16), vaug[hh], preferred_element_type=F32)
            outs.append(pv / pltpu.roll(pv, MEM_DH, axis=1))
        o = jnp.where(lo_lanes, outs[0], outs[1])
        mem_parts.append((o * mg_ref[:, ls].astype(F32)).astype(BF16))
    y = y + jnp.dot(jnp.concatenate(mem_parts, axis=1), wo_ref[done:, :],
                    preferred_element_type=F32)
    o_ref[...] = y * _rms_scale(y, D_MODEL) * fg_ref[...]


def _params(*sem):
    return pltpu.CompilerParams(dimension_semantics=sem, vmem_limit_bytes=VMEM_LIMIT)


def _layer(x, mem, norm_g, w_in, w_alpha_up, b_alpha, b_forget, gla_norm_g,
           mem_norm_g, w_mem_kv, w_out, out_g):
    w_in_t = jnp.transpose(w_in[None], (0, 2, 1)).reshape(-1, LANES)
    T = x.shape[0]
    M = mem.shape[0]

    def rows(width, n=PROJ_ROWS):
        return pl.BlockSpec((n, width), lambda i: (i, 0))

    def whole(shape):
        return pl.BlockSpec(shape, lambda i: (0,) * len(shape))

    bshape = lambda w: jax.ShapeDtypeStruct((T, w), BF16)
    nproj = T // PROJ_ROWS
    stat_spec = pl.BlockSpec((1, 1, LANES), lambda i: (i, 0, 0))
    stat_shape = jax.ShapeDtypeStruct((nproj, 1, LANES), F32)
    gla_pairs, gla_chunks = GLA_HEADS // HEAD_PAIR, PROJ_ROWS // GLA_CHUNK
    pair_w = HEAD_PAIR * GLA_DV_PAD
    (gla, fq, fk, fv, fgate, mq, mg, crow, cx, cq, qn2, kn2) = pl.pallas_call(
        _proj_kernel,
        grid=(nproj,),
        in_specs=[rows(D_MODEL), whole((1, D_MODEL)),
                  pl.BlockSpec(w_in_t.shape, lambda i: (0, 0), pipeline_mode=pl.Buffered(1)),
                  whole(w_alpha_up.shape), whole((1, GLA_HEADS * GLA_DK)),
                  whole((1, FOX_HEADS)), whole((1, GLA_DV))],
        out_specs=[rows(GLA_OUT_W),
                   rows(FOX_W), pl.BlockSpec((FOX_W, PROJ_ROWS), lambda i: (0, i)),
                   rows(FOX_W), rows(FOX_W),
                   rows(MEM_W), rows(MEM_W),
                   pl.BlockSpec((SUBLANES, PROJ_ROWS), lambda i: (0, i)),
                   pl.BlockSpec((LANES, PROJ_ROWS), lambda i: (0, i)), rows(LANES),
                   stat_spec, pl.BlockSpec((1, 1, LANES), lambda i: (0, 0, 0))],
        out_shape=[bshape(GLA_OUT_W),
                   bshape(FOX_W), jax.ShapeDtypeStruct((FOX_W, T), BF16),
                   bshape(FOX_W), bshape(FOX_W),
                   bshape(MEM_W), bshape(MEM_W),
                   jax.ShapeDtypeStruct((SUBLANES, T), F32),
                   jax.ShapeDtypeStruct((LANES, T), BF16), bshape(LANES),
                   stat_shape, jax.ShapeDtypeStruct((1, 1, LANES), F32)],
        scratch_shapes=[
            pltpu.VMEM((SUBLANES, LANES), F32),
            pltpu.VMEM((IN_COLS_PAD, D_MODEL), BF16),
            pltpu.VMEM((SMALL_W, GLA_QK_W), F32),
            pltpu.VMEM((1, GLA_QK_W), F32),
            pltpu.VMEM((1, SMALL_W), F32),
            pltpu.VMEM((1, GLA_DV_PAD), F32),
            pltpu.VMEM((FOX_W, LANES), BF16),
            pltpu.VMEM((PROJ_ROWS, GLA_QK_W), BF16),
            pltpu.VMEM((PROJ_ROWS, GLA_QK_W), BF16),
            pltpu.VMEM((PROJ_ROWS, GLA_V_W), BF16),
            pltpu.VMEM((PROJ_ROWS, GLA_V_W), BF16),
            pltpu.VMEM((PROJ_ROWS, GLA_QK_W), F32),
            pltpu.VMEM((gla_pairs, LANES, pair_w), F32),
            pltpu.VMEM((PROJ_ROWS, GLA_V_W), BF16),
            pltpu.VMEM((gla_pairs, gla_chunks, LANES, pair_w), F32),
            pltpu.VMEM((gla_pairs, gla_chunks, LANES, LANES), F32),
            pltpu.VMEM((gla_pairs, gla_chunks, LANES, pair_w), BF16)],
        compiler_params=_params("arbitrary"),
        name="proj",
    )(x, norm_g[None, :], w_in_t, w_alpha_up, b_alpha[None, :], b_forget[None, :],
      gla_norm_g[None, :])

    mk, mv = pl.pallas_call(
        _memkv_kernel,
        out_shape=[jax.ShapeDtypeStruct((M, MEM_W), BF16)] * 2,
        compiler_params=pltpu.CompilerParams(vmem_limit_bytes=VMEM_LIMIT),
        name="memkv",
    )(mem, mem_norm_g[None, :], w_mem_kv)

    cend = crow[:FOX_HEADS, FOX_KEYS - 1::FOX_KEYS].reshape(-1)
    pair_rows = pl.BlockSpec((FOX_BLOCK, LANES), lambda p, i, *_: (i, p))
    pair_all = pl.BlockSpec((T, LANES), lambda p, i, *_: (0, p))
    fox = pl.pallas_call(
        _fox_kernel,
        grid_spec=pltpu.PrefetchScalarGridSpec(
            num_scalar_prefetch=3,
            grid=(FOX_HEADS // HEAD_PAIR, T // FOX_BLOCK),
            in_specs=[pair_rows, pl.BlockSpec((FOX_BLOCK, LANES), lambda p, i, *_: (i, 0)),
                      pl.BlockSpec((LANES, T), lambda p, i, *_: (p, 0)),
                      pl.BlockSpec((LANES, T), lambda p, i, *_: (0, 0)), pair_all, pair_rows],
            out_specs=pair_rows,
            scratch_shapes=[pltpu.VMEM((HEAD_PAIR, FOX_BLOCK, LANES), F32),
                            pltpu.VMEM((HEAD_PAIR, FOX_BLOCK, LANES), F32)]),
        out_shape=bshape(FOX_W),
        compiler_params=_params("arbitrary", "arbitrary"),
        name="fox",
    )(qn2.reshape(-1), kn2.reshape(-1), cend, fq, cq, fk, cx, fv, fgate)

    out = pl.pallas_call(
        _out_kernel,
        grid=(T // OUT_ROWS,),
        in_specs=[rows(D_MODEL, OUT_ROWS), rows(GLA_OUT_W, OUT_ROWS), rows(FOX_W, OUT_ROWS),
                  rows(MEM_W, OUT_ROWS), rows(MEM_W, OUT_ROWS),
                  whole((M, MEM_W)), whole((M, MEM_W)),
                  pl.BlockSpec(w_out.shape, lambda i: (0, 0), pipeline_mode=pl.Buffered(1)),
                  whole((1, D_MODEL))],
        out_specs=rows(D_MODEL, OUT_ROWS),
        out_shape=jax.ShapeDtypeStruct((T, D_MODEL), F32),
        scratch_shapes=[pltpu.VMEM((D_MODEL, D_MODEL), BF16)],
        compiler_params=_params("arbitrary"),
        name="out",
    )(x, gla, fox, mq, mg, mk, mv, w_out, out_g[None, :])
    return out


def kernel(x, mem, norm_g, w_in, w_alpha_up, b_alpha, b_forget, gla_norm_g, mem_norm_g,
           w_mem_kv, w_out, final_norm_g):
    assert x.shape[0] == 1 and mem.shape[0] == 1 and norm_g.shape[0] == 1
    assert x.shape[1] % max(PROJ_ROWS, FOX_BLOCK, OUT_ROWS) == 0
    out = _layer(x[0], mem[0], norm_g[0], w_in[0], w_alpha_up[0], b_alpha[0], b_forget[0],
                 gla_norm_g[0], mem_norm_g[0], w_mem_kv[0], w_out[0], final_norm_g)
    return out[None]
```

```python
import jax
import jax.numpy as jnp
from jax import lax
from jax.experimental import pallas as pl
from jax.experimental.pallas import tpu as pltpu

F32 = jnp.float32
BF16 = jnp.bfloat16

EPS = 1e-6
LANES = 128
SUBLANES = 8

D_MODEL = 1024
GLA_HEADS, GLA_DK, GLA_DV, GLA_RANK = 4, 48, 96, 16
GLA_DK_PAD = 64
GLA_DV_PAD = LANES
GLA_GATE_NORM = 16.0
GLA_CHUNK = 64
FOX_HEADS, FOX_DH = 6, 64
MEM_HEADS, MEM_DH = 4, 64
HEAD_PAIR = 2
GLA_QK_W = GLA_HEADS * GLA_DK_PAD
GLA_V_W = GLA_HEADS * GLA_DV_PAD
GLA_OUT_W = GLA_HEADS * GLA_DV
FOX_W = FOX_HEADS * FOX_DH
MEM_W = MEM_HEADS * MEM_DH
SMALL_W = LANES
FG_LANE0 = 0
LR_LANE0 = SUBLANES

_GROUPS = (("gq", GLA_QK_W), ("gk", GLA_QK_W), ("gv", GLA_OUT_W), ("gg", GLA_OUT_W),
           ("fq", FOX_W), ("fk", FOX_W), ("fv", FOX_W), ("fgate", FOX_W),
           ("mq", MEM_W), ("mg", MEM_W), ("small", SMALL_W))
_OFF = {}
_o = 0
for _n, _w in _GROUPS:
    _OFF[_n] = (_o, _o + _w)
    _o += _w
IN_COLS_PAD = _o

PROJ_ROWS = 1024
FOX_BLOCK = 2048
FOX_KEYS = 256
OUT_ROWS = 1024
VMEM_LIMIT = 56 * 1024 * 1024

NEG_BIG = -1e30
FOX_SKIP_NATS = 105.0
NORM_SLACK = 1.02
FOX_DIRECT_NORM2 = 3600.0
CX_HI, CX_MID, CX_LO, CX_ONE = 0, 8, 16, 24


def _log_sigmoid(z):
    return jnp.minimum(z, 0.0) - jnp.log(1.0 + jnp.exp(-jnp.abs(z)))


def _silu(z):
    return z / (1.0 + jnp.exp(-z))


def _rms_scale(v, width):
    return lax.rsqrt(jnp.sum(v * v, axis=-1, keepdims=True) * (1.0 / width) + EPS)


def _w_in_segments():
    qk, gw = GLA_HEADS * GLA_DK, GLA_HEADS * GLA_DV
    src = {}
    o = 0
    for name, width in (("gq", qk), ("gk", qk), ("gv", gw), ("lr", GLA_RANK), ("gg", gw),
                        ("fq", FOX_W), ("fk", FOX_W), ("fv", FOX_W), ("fg", FOX_HEADS),
                        ("fgate", FOX_W), ("mq", MEM_W), ("mg", MEM_W)):
        src[name] = o
        o += width
    segs = []
    for name, d, d_pad in (("gq", GLA_DK, GLA_DK_PAD), ("gk", GLA_DK, GLA_DK_PAD)):
        segs += [(src[name] + h * d, _OFF[name][0] + h * d_pad, d) for h in range(GLA_HEADS)]
    segs += [(src[name], _OFF[name][0], _OFF[name][1] - _OFF[name][0])
             for name in ("gv", "gg", "fq", "fk", "fv", "fgate", "mq", "mg")]
    segs += [(src["fg"], _OFF["small"][0] + FG_LANE0, FOX_HEADS),
             (src["lr"], _OFF["small"][0] + LR_LANE0, GLA_RANK)]
    return tuple(segs)


def _proj_kernel(x_ref, g_ref, w_in_ref, w_alpha_ref, b_alpha_ref, b_forget_ref, gla_g_ref,
                 gla_ref, fq_ref, fk_ref, fv_ref, fgate_ref,
                 mq_ref, mg_ref, crow_ref, cx_ref, cq_ref, qn2_ref, kn2_ref,
                 carry_ref, wt_ref, wa_ref, ba_ref, bf_ref, ng_ref, seg_ref,
                 gq_ref, gk_ref, gv_ref, gg_ref, loga_ref,
                 s_ref, lhs_ref, kv_ref, dec_ref, sprev_ref):
    rows = x_ref.shape[0]
    k_chunks = D_MODEL // LANES

    @pl.when(pl.program_id(0) == 0)
    def _():
        carry_ref[...] = jnp.zeros_like(carry_ref)
        s_ref[...] = jnp.zeros_like(s_ref)
        kn2_ref[...] = jnp.zeros_like(kn2_ref)
        gv_ref[...] = jnp.zeros_like(gv_ref)
        gg_ref[...] = jnp.zeros_like(gg_ref)
        wa_ref[...] = jnp.zeros_like(wa_ref)
        ba_ref[...] = jnp.zeros_like(ba_ref)
        bf_ref[...] = jnp.zeros_like(bf_ref)
        ng_ref[...] = jnp.zeros_like(ng_ref)
        for h in range(GLA_HEADS):
            src = slice(h * GLA_DK, (h + 1) * GLA_DK)
            dst = slice(h * GLA_DK_PAD, h * GLA_DK_PAD + GLA_DK)
            wa_ref[LR_LANE0:LR_LANE0 + GLA_RANK, dst] = w_alpha_ref[:, src]
            ba_ref[:, dst] = b_alpha_ref[:, src]
        bf_ref[:, FG_LANE0:FG_LANE0 + FOX_HEADS] = b_forget_ref[...]
        ng_ref[:, 0:GLA_DV] = gla_g_ref[...]
        seg_ref[...] = (lax.broadcasted_iota(jnp.int32, seg_ref.shape, 0) // FOX_DH
                        == lax.broadcasted_iota(jnp.int32, seg_ref.shape, 1)).astype(BF16)
        wt_ref[...] = jnp.zeros_like(wt_ref)
        for s0, d0, width in _w_in_segments():
            for c in range(k_chunks):
                wt_ref[d0:d0 + width, c * LANES:(c + 1) * LANES] = (
                    w_in_ref[pl.ds(s0 * k_chunks + c, width, stride=k_chunks), :].astype(BF16))

    x = x_ref[...]
    xn = (x * _rms_scale(x, D_MODEL) * g_ref[...]).astype(BF16)
    nt = (((1,), (1,)), ((), ()))

    def proj(first, last):
        lo, hi = _OFF[first][0], _OFF[last][1]
        y = lax.dot_general(xn, wt_ref[lo:hi, :], nt, preferred_element_type=F32)
        return lambda name: y[:, _OFF[name][0] - lo:_OFF[name][1] - lo]

    tail = proj("mq", "small")
    gla = proj("gq", "gg")
    small = tail("small")
    logf = _log_sigmoid(small + bf_ref[...])
    c = logf.T[0:SUBLANES, :]
    lane = lax.broadcasted_iota(jnp.int32, c.shape, 1)
    shift = 1
    while shift < rows:
        c = c + jnp.where(lane >= shift, pltpu.roll(c, shift, axis=1), 0.0)
        shift *= 2
    c = c + carry_ref[:, 0:1]
    crow_ref[...] = c
    carry_ref[...] = jnp.broadcast_to(c[:, rows - 1:rows], carry_ref.shape)
    neg = -c
    hi = neg.astype(BF16).astype(F32)
    mid = (neg - hi).astype(BF16).astype(F32)
    low = neg - hi - mid
    parts = jnp.concatenate(
        [hi, mid, low, jnp.ones_like(c), jnp.zeros((LANES - 4 * SUBLANES, rows), F32)], axis=0)
    cx_ref[...] = parts.astype(BF16)
    cq_ref[...] = parts.T.astype(BF16)

    z = (jnp.dot(small.astype(BF16), wa_ref[...].astype(BF16), preferred_element_type=F32)
         + ba_ref[...])
    loga_ref[...] = _log_sigmoid(z) * (1.0 / GLA_GATE_NORM)
    gq_ref[...] = gla("gq").astype(BF16)
    gk_ref[...] = gla("gk").astype(BF16)
    gv, gg = gla("gv").astype(BF16), _silu(gla("gg")).astype(BF16)
    for h in range(GLA_HEADS):
        src = slice(h * GLA_DV, (h + 1) * GLA_DV)
        dst = slice(h * GLA_DV_PAD, h * GLA_DV_PAD + GLA_DV)
        gv_ref[:, dst] = gv[:, src]
        gg_ref[:, dst] = gg[:, src]
    gla_local, gla_scan, gla_output = _gla_block(
        gq_ref, gk_ref, gv_ref, loga_ref, gg_ref, ng_ref, gla_ref,
        s_ref, lhs_ref, kv_ref, dec_ref, sprev_ref)

    def max_sq_norm(v):
        v32 = v.astype(F32)
        n2 = jnp.dot((v32 * v32).astype(BF16), seg_ref[...], preferred_element_type=F32)
        return jnp.max(n2, axis=0, keepdims=True)

    gla_local()
    fox_qk = proj("fq", "fk")
    mq_ref[...] = (tail("mq") * MEM_DH ** -0.5).astype(BF16)
    mg_ref[...] = _silu(tail("mg")).astype(BF16)
    gla_scan()
    fq = (fox_qk("fq") * FOX_DH ** -0.5).astype(BF16)
    fk = fox_qk("fk").astype(BF16)
    fq_ref[...] = fq
    fk_ref[...] = fox_qk("fk").T.astype(BF16)
    gla_output()
    fox_vg = proj("fv", "fgate")
    fv_ref[...] = fox_vg("fv").astype(BF16)
    fgate_ref[...] = _silu(fox_vg("fgate")).astype(BF16)
    qn2_ref[0] = max_sq_norm(fq)
    kn2_ref[0] = jnp.maximum(kn2_ref[0], max_sq_norm(fk))


def _memkv(mem_ref, g_ref, w_ref, mk_ref, mv_ref):
    m = mem_ref[...]
    mn = (m * _rms_scale(m, D_MODEL) * g_ref[...]).astype(BF16)
    kv = jnp.dot(mn, w_ref[...].astype(BF16), preferred_element_type=F32)
    mk_ref[...] = kv[:, :MEM_W].astype(BF16)
    mv_ref[...] = kv[:, MEM_W:].astype(BF16)


def _gla_block(q_ref, k_ref, v_ref, loga_ref, gate_ref, ng_ref, o_ref,
               s_ref, lhs_ref, kv_ref, dec_ref, sprev_ref):
    C = GLA_CHUNK
    W = HEAD_PAIR * GLA_DV_PAD
    n_chunks = q_ref.shape[0] // C

    row = lax.broadcasted_iota(jnp.int32, (C, LANES), 0)
    lane = lax.broadcasted_iota(jnp.int32, (C, LANES), 1)
    lo_k = lane < GLA_DK_PAD
    causal = row >= jnp.where(lo_k, lane, lane - GLA_DK_PAD)
    lo_v = lax.broadcasted_iota(jnp.int32, (C, W), 1) < GLA_DV_PAD
    st_row = lax.broadcasted_iota(jnp.int32, (LANES, W), 0)
    st_lane = lax.broadcasted_iota(jnp.int32, (LANES, W), 1)
    own = (st_row < GLA_DK_PAD) == (st_lane < GLA_DV_PAD)
    eye = (lax.broadcasted_iota(jnp.int32, (LANES, LANES), 0)
           == lax.broadcasted_iota(jnp.int32, (LANES, LANES), 1))
    scale = GLA_DK ** -0.5
    nt = (((1,), (1,)), ((), ()))
    tn = (((0,), (0,)), ((), ()))
    ng = jnp.concatenate([ng_ref[...]] * HEAD_PAIR, axis=1)

    pairs = range(GLA_HEADS // HEAD_PAIR)

    def local(ci):
        rs = slice(ci * C, (ci + 1) * C)
        for p in pairs:
            ls = slice(p * LANES, (p + 1) * LANES)
            vs = slice(p * W, (p + 1) * W)
            b = loga_ref[rs, ls]
            shift = 1
            while shift < C:
                b = b + jnp.where(row >= shift, pltpu.roll(b, shift, axis=0), 0.0)
                shift *= 2
            b_last = b[C - 1:C, :]
            k2 = k_ref[rs, ls].astype(F32)
            qd = (q_ref[rs, ls].astype(F32) * scale * jnp.exp(b)).astype(BF16)
            kd = (k2 * jnp.exp(-b)).astype(BF16)
            ke = (k2 * jnp.exp(b_last - b)).astype(BF16)
            zk = jnp.zeros_like(kd)
            kd_blk = jnp.concatenate([jnp.where(lo_k, kd, zk), jnp.where(lo_k, zk, kd)], axis=0)
            attn = lax.dot_general(qd, kd_blk, nt, preferred_element_type=F32)
            lhs_ref[rs, vs] = jnp.concatenate([jnp.where(causal, attn, 0.0).astype(BF16), qd], axis=1)
            kv = lax.dot_general(ke, v_ref[rs, vs], tn, preferred_element_type=F32)
            kv_ref[p, ci] = jnp.where(own, kv, 0.0)
            dcol = jnp.exp(jnp.sum(jnp.where(eye, jnp.broadcast_to(b_last, (LANES, LANES)), 0.0),
                                   axis=1, keepdims=True))
            dec_ref[p, ci] = jnp.broadcast_to(dcol, (LANES, LANES))

    def scan(ci):
        for p in pairs:
            s_prev = s_ref[p]
            sprev_ref[p, ci] = s_prev.astype(BF16)
            s_ref[p] = jnp.tile(dec_ref[p, ci], (1, HEAD_PAIR)) * s_prev + kv_ref[p, ci]

    def output(ci):
        rs = slice(ci * C, (ci + 1) * C)
        for p in pairs:
            vs = slice(p * W, (p + 1) * W)
            v2 = v_ref[rs, vs]
            zv = jnp.zeros_like(v2)
            v_blk = jnp.concatenate([jnp.where(lo_v, v2, zv), jnp.where(lo_v, zv, v2)], axis=0)
            o = jnp.dot(lhs_ref[rs, vs], jnp.concatenate([v_blk, sprev_ref[p, ci]], axis=0),
                        preferred_element_type=F32)
            o2 = o * o
            ms = jnp.where(lo_v, jnp.sum(o2[:, :GLA_DV_PAD], axis=1, keepdims=True),
                           jnp.sum(o2[:, GLA_DV_PAD:], axis=1, keepdims=True))
            on = o * lax.rsqrt(ms * (1.0 / GLA_DV) + EPS) * ng
            og = (on * gate_ref[rs, vs].astype(F32)).astype(BF16)
            for hh in range(HEAD_PAIR):
                c0 = (p * HEAD_PAIR + hh) * GLA_DV
                o_ref[rs, c0:c0 + GLA_DV] = og[:, hh * GLA_DV_PAD:hh * GLA_DV_PAD + GLA_DV]

    def all_chunks(phase):
        return lambda: [phase(ci) for ci in range(n_chunks)]

    return all_chunks(local), all_chunks(scan), all_chunks(output)


def _fox_kernel(qn2_ref, kn2_ref, cend_ref, q_ref, cq_ref, k_ref, cx_ref, v_ref, gate_ref, o_ref,
                m_ref, acc_ref):
    blk = FOX_KEYS
    streams = range(q_ref.shape[0] // blk)
    pair = pl.program_id(0)
    qi = pl.program_id(1)
    nblk = pl.num_programs(1) * len(streams)
    lane = lax.broadcasted_iota(jnp.int32, (1, LANES), 1)
    lo_lanes = lane < FOX_DH
    reps = blk // LANES
    diag = [qi * len(streams) + s for s in streams]
    heads = [pair * HEAD_PAIR + hh for hh in range(HEAD_PAIR)]

    q = q_ref[...]
    zero = jnp.zeros_like(q)
    q_lo, q_hi = jnp.where(lo_lanes, q, zero), jnp.where(lo_lanes, zero, q)
    q_stack = [jnp.concatenate([q_lo[s * blk:(s + 1) * blk], q_hi[s * blk:(s + 1) * blk]], axis=0)
               for s in streams]
    g_row = lax.broadcasted_iota(jnp.int32, (LANES, LANES), 0)
    g_col = lax.broadcasted_iota(jnp.int32, (LANES, LANES), 1)
    xlane = lax.broadcasted_iota(jnp.int32, (HEAD_PAIR * blk, LANES), 1)
    cq = cq_ref[...]

    def lane_map(h, shift):
        g = jnp.where((g_row == CX_ONE) & ((g_col == CX_HI + h) | (g_col == CX_MID + h)
                                           | (g_col == CX_LO + h)), 1.0, 0.0)
        if shift:
            for part, base in enumerate((CX_HI, CX_MID, CX_LO)):
                g = jnp.where((g_row == base + h) & (g_col == CX_ONE + part), -1.0, g)
        return g.astype(BF16)

    def q_aug(s, shift, void=None):
        cs = cq[s * blk:(s + 1) * blk]
        extra = jnp.concatenate([jnp.dot(cs, lane_map(h, shift), preferred_element_type=F32)
                                 for h in heads], axis=0)
        if void is not None:
            extra = jnp.where(jnp.logical_and(xlane == CX_ONE + 3, void), NEG_BIG, extra)
        return jnp.concatenate([q_stack[s], extra.astype(BF16)], axis=1)

    acc_ref[...] = jnp.zeros_like(acc_ref)
    qpos = lax.broadcasted_iota(jnp.int32, (blk, blk), 0)
    kpos = lax.broadcasted_iota(jnp.int32, (blk, blk), 1)

    def block(j):
        ks = pl.ds(pl.multiple_of(jnp.maximum(j, 0) * blk, blk), blk)
        k_aug = jnp.concatenate([k_ref[:, ks], cx_ref[:, ks]], axis=0)
        vb = v_ref[ks, :]
        one = jnp.ones_like(vb)
        return k_aug, (jnp.where(lo_lanes, vb, one), jnp.where(lo_lanes, one, vb))

    def step_online(s, j, qa, masked):
        rows = slice(s * blk, (s + 1) * blk)
        k_aug, vaug = block(j)
        s_all = jnp.dot(qa, k_aug, preferred_element_type=F32)
        for hh in range(HEAD_PAIR):
            sc = s_all[hh * blk:(hh + 1) * blk]
            if masked:
                sc = jnp.where(kpos <= qpos, sc, NEG_BIG)
            m_prev = m_ref[hh, rows]
            m_new = jnp.maximum(m_prev, jnp.max(sc, axis=1, keepdims=True))
            p = jnp.exp(sc - jnp.tile(m_new, (1, reps)))
            alpha = jnp.exp(m_prev - m_new)
            pv = jnp.dot(p.astype(BF16), vaug[hh], preferred_element_type=F32)
            acc_ref[hh, rows] = alpha * acc_ref[hh, rows] + pv
            m_ref[hh, rows] = m_new

    def step_direct(s, j, qa, masked, only=None):
        rows = slice(s * blk, (s + 1) * blk)
        k_aug, vaug = block(j)
        hsel = range(HEAD_PAIR) if only is None else (only,)
        lhs = qa if only is None else qa[only * blk:(only + 1) * blk]
        s_all = jnp.dot(lhs, k_aug, preferred_element_type=F32)
        for n, hh in enumerate(hsel):
            sc = s_all[n * blk:(n + 1) * blk]
            if masked:
                sc = jnp.where(kpos <= qpos, sc, NEG_BIG)
            acc_ref[hh, rows] += jnp.dot(jnp.exp(sc).astype(BF16), vaug[hh],
                                         preferred_element_type=F32)

    k_max2 = [kn2_ref[h] for h in heads]

    def norm2(s, hh):
        stat = (diag[s] * blk // PROJ_ROWS) * LANES
        return (NORM_SLACK * NORM_SLACK) * qn2_ref[stat + heads[hh]] * k_max2[hh]

    n2 = [[norm2(s, hh) for hh in range(HEAD_PAIR)] for s in streams]
    gap0 = [[FOX_SKIP_NATS + cend_ref[heads[hh] * nblk + jnp.maximum(diag[s] - 1, 0)]
             for hh in range(HEAD_PAIR)] for s in streams]

    def live_head(t, hh):
        keep = False
        for s in streams:
            j = diag[s] - 1 - t
            gap = gap0[s][hh] - cend_ref[heads[hh] * nblk + jnp.maximum(j, 0)]
            dead = jnp.logical_and(gap <= 0.0, n2[s][hh] <= gap * gap)
            keep = jnp.logical_or(keep, jnp.logical_and(j >= 0, jnp.logical_not(dead)))
        return keep

    def sweeps(stepper, per_head):
        def sweep(back, masked=False, only=None):
            for s in streams:
                stepper(s, diag[s] - back, masked, only)

        def loop(t0, cond, only=None):
            def body(t):
                sweep(t + 1, only=only)
                return t + 1
            return lax.while_loop(cond, body, t0)

        sweep(0, masked=True)
        sweep(1)
        if per_head:
            t_both = loop(1, lambda t: jnp.logical_and(live_head(t, 0), live_head(t, 1)))
            for hh in range(HEAD_PAIR):
                loop(t_both, lambda t, hh=hh: live_head(t, hh), only=hh)
        else:
            loop(1, lambda t: jnp.logical_or(live_head(t, 0), live_head(t, 1)))

    direct_ok = True
    for s in streams:
        for hh in range(HEAD_PAIR):
            direct_ok = jnp.logical_and(direct_ok, n2[s][hh] <= FOX_DIRECT_NORM2)

    @pl.when(direct_ok)
    def _():
        q_dir = [q_aug(s, shift=True) for s in streams]
        q_void = [q_aug(s, shift=True, void=True) for s in streams]
        sweeps(lambda s, j, masked, only: step_direct(
            s, j, q_dir[s] if masked else jnp.where(j >= 0, q_dir[s], q_void[s]), masked, only),
            per_head=True)

    @pl.when(jnp.logical_not(direct_ok))
    def _():
        m_ref[...] = jnp.full_like(m_ref, NEG_BIG)
        sweeps(lambda s, j, masked, only: step_online(
            s, j, q_aug(s, shift=False, void=j < 0), masked), per_head=False)

    outs = []
    for hh in range(HEAD_PAIR):
        acc = acc_ref[hh]
        outs.append(acc / pltpu.roll(acc, FOX_DH, axis=1))
    o = jnp.where(lo_lanes, outs[0], outs[1])
    o_ref[...] = (o * gate_ref[...].astype(F32)).astype(BF16)


def _out_kernel(x_ref, gla_ref, fox_ref, mq_ref, mg_ref, mem_ref, mem_g_ref, w_mem_ref,
                w_out_ref, fg_ref, o_ref, wo_ref, mk_ref, mv_ref):
    @pl.when(pl.program_id(0) == 0)
    def _():
        wo_ref[...] = w_out_ref[...].astype(BF16)
        _memkv(mem_ref, mem_g_ref, w_mem_ref, mk_ref, mv_ref)

    lane = lax.broadcasted_iota(jnp.int32, (1, LANES), 1)
    lo_lanes = lane < MEM_DH
    nt = (((1,), (1,)), ((), ()))
    mem_parts = []
    for p in range(MEM_HEADS // HEAD_PAIR):
        ls = slice(p * LANES, (p + 1) * LANES)
        q = mq_ref[:, ls]
        kb = mk_ref[:, ls]
        vb = mv_ref[:, ls]
        zero = jnp.zeros_like(q)
        one = jnp.ones_like(vb)
        qh = (jnp.where(lo_lanes, q, zero), jnp.where(lo_lanes, zero, q))
        vaug = (jnp.where(lo_lanes, vb, one), jnp.where(lo_lanes, one, vb))
        outs = []
        for hh in range(HEAD_PAIR):
            s = lax.dot_general(qh[hh], kb, nt, preferred_element_type=F32)
            pexp = jnp.exp(s - jnp.max(s, axis=1, keepdims=True))
            pv = jnp.dot(pexp.astype(BF16), vaug[hh], preferred_element_type=F32)
            outs.append(pv / pltpu.roll(pv, MEM_DH, axis=1))
        o = jnp.where(lo_lanes, outs[0], outs[1])
        mem_parts.append((o * mg_ref[:, ls].astype(F32)).astype(BF16))
    mixed = jnp.concatenate([gla_ref[...], fox_ref[...]] + mem_parts, axis=1)
    y = x_ref[...] + jnp.dot(mixed, wo_ref[...], preferred_element_type=F32)
    o_ref[...] = y * _rms_scale(y, D_MODEL) * fg_ref[...]


def _params(*sem):
    return pltpu.CompilerParams(dimension_semantics=sem, vmem_limit_bytes=VMEM_LIMIT)


def _layer(x, mem, norm_g, w_in, w_alpha_up, b_alpha, b_forget, gla_norm_g,
           mem_norm_g, w_mem_kv, w_out, out_g):
    w_in_t = jnp.transpose(w_in[None], (0, 2, 1)).reshape(-1, LANES)
    T = x.shape[0]
    M = mem.shape[0]

    def rows(width, n=PROJ_ROWS):
        return pl.BlockSpec((n, width), lambda i: (i, 0))

    def whole(shape):
        return pl.BlockSpec(shape, lambda i: (0,) * len(shape))

    def once(shape):
        return pl.BlockSpec(shape, lambda i: (0,) * len(shape), pipeline_mode=pl.Buffered(1))

    bshape = lambda w: jax.ShapeDtypeStruct((T, w), BF16)
    nproj = T // PROJ_ROWS
    stat_spec = pl.BlockSpec((1, 1, LANES), lambda i: (i, 0, 0))
    stat_shape = jax.ShapeDtypeStruct((nproj, 1, LANES), F32)
    gla_pairs, gla_chunks = GLA_HEADS // HEAD_PAIR, PROJ_ROWS // GLA_CHUNK
    pair_w = HEAD_PAIR * GLA_DV_PAD
    (gla, fq, fk, fv, fgate, mq, mg, crow, cx, cq, qn2, kn2) = pl.pallas_call(
        _proj_kernel,
        grid=(nproj,),
        in_specs=[rows(D_MODEL), whole((1, D_MODEL)),
                  pl.BlockSpec(w_in_t.shape, lambda i: (0, 0), pipeline_mode=pl.Buffered(1)),
                  whole(w_alpha_up.shape), whole((1, GLA_HEADS * GLA_DK)),
                  whole((1, FOX_HEADS)), whole((1, GLA_DV))],
        out_specs=[rows(GLA_OUT_W),
                   rows(FOX_W), pl.BlockSpec((FOX_W, PROJ_ROWS), lambda i: (0, i)),
                   rows(FOX_W), rows(FOX_W),
                   rows(MEM_W), rows(MEM_W),
                   pl.BlockSpec((SUBLANES, PROJ_ROWS), lambda i: (0, i)),
                   pl.BlockSpec((LANES, PROJ_ROWS), lambda i: (0, i)), rows(LANES),
                   stat_spec, pl.BlockSpec((1, 1, LANES), lambda i: (0, 0, 0))],
        out_shape=[bshape(GLA_OUT_W),
                   bshape(FOX_W), jax.ShapeDtypeStruct((FOX_W, T), BF16),
                   bshape(FOX_W), bshape(FOX_W),
                   bshape(MEM_W), bshape(MEM_W),
                   jax.ShapeDtypeStruct((SUBLANES, T), F32),
                   jax.ShapeDtypeStruct((LANES, T), BF16), bshape(LANES),
                   stat_shape, jax.ShapeDtypeStruct((1, 1, LANES), F32)],
        scratch_shapes=[
            pltpu.VMEM((SUBLANES, LANES), F32),
            pltpu.VMEM((IN_COLS_PAD, D_MODEL), BF16),
            pltpu.VMEM((SMALL_W, GLA_QK_W), F32),
            pltpu.VMEM((1, GLA_QK_W), F32),
            pltpu.VMEM((1, SMALL_W), F32),
            pltpu.VMEM((1, GLA_DV_PAD), F32),
            pltpu.VMEM((FOX_W, LANES), BF16),
            pltpu.VMEM((PROJ_ROWS, GLA_QK_W), BF16),
            pltpu.VMEM((PROJ_ROWS, GLA_QK_W), BF16),
            pltpu.VMEM((PROJ_ROWS, GLA_V_W), BF16),
            pltpu.VMEM((PROJ_ROWS, GLA_V_W), BF16),
            pltpu.VMEM((PROJ_ROWS, GLA_QK_W), F32),
            pltpu.VMEM((gla_pairs, LANES, pair_w), F32),
            pltpu.VMEM((PROJ_ROWS, GLA_V_W), BF16),
            pltpu.VMEM((gla_pairs, gla_chunks, LANES, pair_w), F32),
            pltpu.VMEM((gla_pairs, gla_chunks, LANES, LANES), F32),
            pltpu.VMEM((gla_pairs, gla_chunks, LANES, pair_w), BF16)],
        compiler_params=_params("arbitrary"),
        name="proj",
    )(x, norm_g[None, :], w_in_t, w_alpha_up, b_alpha[None, :], b_forget[None, :],
      gla_norm_g[None, :])

    cend = crow[:FOX_HEADS, FOX_KEYS - 1::FOX_KEYS].reshape(-1)
    pair_rows = pl.BlockSpec((FOX_BLOCK, LANES), lambda p, i, *_: (i, p))
    pair_all = pl.BlockSpec((T, LANES), lambda p, i, *_: (0, p))
    fox = pl.pallas_call(
        _fox_kernel,
        grid_spec=pltpu.PrefetchScalarGridSpec(
            num_scalar_prefetch=3,
            grid=(FOX_HEADS // HEAD_PAIR, T // FOX_BLOCK),
            in_specs=[pair_rows, pl.BlockSpec((FOX_BLOCK, LANES), lambda p, i, *_: (i, 0)),
                      pl.BlockSpec((LANES, T), lambda p, i, *_: (p, 0)),
                      pl.BlockSpec((LANES, T), lambda p, i, *_: (0, 0)), pair_all, pair_rows],
            out_specs=pair_rows,
            scratch_shapes=[pltpu.VMEM((HEAD_PAIR, FOX_BLOCK, LANES), F32),
                            pltpu.VMEM((HEAD_PAIR, FOX_BLOCK, LANES), F32)]),
        out_shape=bshape(FOX_W),
        compiler_params=_params("arbitrary", "arbitrary"),
        name="fox",
    )(qn2.reshape(-1), kn2.reshape(-1), cend, fq, cq, fk, cx, fv, fgate)

    out = pl.pallas_call(
        _out_kernel,
        grid=(T // OUT_ROWS,),
        in_specs=[rows(D_MODEL, OUT_ROWS), rows(GLA_OUT_W, OUT_ROWS), rows(FOX_W, OUT_ROWS),
                  rows(MEM_W, OUT_ROWS), rows(MEM_W, OUT_ROWS),
                  once(mem.shape), once((1, D_MODEL)), once(w_mem_kv.shape), once(w_out.shape),
                  whole((1, D_MODEL))],
        out_specs=rows(D_MODEL, OUT_ROWS),
        out_shape=jax.ShapeDtypeStruct((T, D_MODEL), F32),
        scratch_shapes=[pltpu.VMEM((D_MODEL, D_MODEL), BF16),
                        pltpu.VMEM((M, MEM_W), BF16), pltpu.VMEM((M, MEM_W), BF16)],
        compiler_params=_params("arbitrary"),
        name="out",
    )(x, gla, fox, mq, mg, mem, mem_norm_g[None, :], w_mem_kv, w_out, out_g[None, :])
    return out


def kernel(x, mem, norm_g, w_in, w_alpha_up, b_alpha, b_forget, gla_norm_g, mem_norm_g,
           w_mem_kv, w_out, final_norm_g):
    assert x.shape[0] == 1 and mem.shape[0] == 1 and norm_g.shape[0] == 1
    assert x.shape[1] % max(PROJ_ROWS, FOX_BLOCK, OUT_ROWS) == 0
    out = _layer(x[0], mem[0], norm_g[0], w_in[0], w_alpha_up[0], b_alpha[0], b_forget[0],
                 gla_norm_g[0], mem_norm_g[0], w_mem_kv[0], w_out[0], final_norm_g)
    return out[None]
```

```python
import jax
import jax.numpy as jnp
from jax import lax
from jax.experimental import pallas as pl
from jax.experimental.pallas import tpu as pltpu

F32 = jnp.float32
BF16 = jnp.bfloat16

EPS = 1e-6
LANES = 128
SUBLANES = 8

D_MODEL = 1024
GLA_HEADS, GLA_DK, GLA_DV, GLA_RANK = 4, 48, 96, 16
GLA_DK_PAD = 64
GLA_DV_PAD = LANES
GLA_GATE_NORM = 16.0
GLA_CHUNK = 64
FOX_HEADS, FOX_DH = 6, 64
MEM_HEADS, MEM_DH = 4, 64
HEAD_PAIR = 2
GLA_QK_W = GLA_HEADS * GLA_DK_PAD
GLA_V_W = GLA_HEADS * GLA_DV_PAD
GLA_OUT_W = GLA_HEADS * GLA_DV
FOX_W = FOX_HEADS * FOX_DH
MEM_W = MEM_HEADS * MEM_DH
SMALL_W = LANES
FG_LANE0 = 0
LR_LANE0 = SUBLANES

_GROUPS = (("gq", GLA_QK_W), ("gk", GLA_QK_W), ("gv", GLA_OUT_W), ("gg", GLA_OUT_W),
           ("fq", FOX_W), ("fk", FOX_W), ("fv", FOX_W), ("fgate", FOX_W),
           ("mq", MEM_W), ("mg", MEM_W), ("small", SMALL_W))
_OFF = {}
_o = 0
for _n, _w in _GROUPS:
    _OFF[_n] = (_o, _o + _w)
    _o += _w
IN_COLS_PAD = _o

PROJ_ROWS = 1024
FOX_BLOCK = 2048
FOX_KEYS = 256
OUT_ROWS = 1024
VMEM_LIMIT = 56 * 1024 * 1024

NEG_BIG = -1e30
FOX_SKIP_NATS = 105.0
NORM_SLACK = 1.02
FOX_DIRECT_NORM2 = 3600.0
CX_HI, CX_MID, CX_LO, CX_ONE = 0, 8, 16, 24


def _log_sigmoid(z):
    return jnp.minimum(z, 0.0) - jnp.log(1.0 + jnp.exp(-jnp.abs(z)))


def _silu(z):
    return z / (1.0 + jnp.exp(-z))


def _rms_scale(v, width):
    return lax.rsqrt(jnp.sum(v * v, axis=-1, keepdims=True) * (1.0 / width) + EPS)


def _w_in_segments():
    qk, gw = GLA_HEADS * GLA_DK, GLA_HEADS * GLA_DV
    src = {}
    o = 0
    for name, width in (("gq", qk), ("gk", qk), ("gv", gw), ("lr", GLA_RANK), ("gg", gw),
                        ("fq", FOX_W), ("fk", FOX_W), ("fv", FOX_W), ("fg", FOX_HEADS),
                        ("fgate", FOX_W), ("mq", MEM_W), ("mg", MEM_W)):
        src[name] = o
        o += width
    segs = []
    for name, d, d_pad in (("gq", GLA_DK, GLA_DK_PAD), ("gk", GLA_DK, GLA_DK_PAD)):
        segs += [(src[name] + h * d, _OFF[name][0] + h * d_pad, d) for h in range(GLA_HEADS)]
    segs += [(src[name], _OFF[name][0], _OFF[name][1] - _OFF[name][0])
             for name in ("gv", "gg", "fq", "fk", "fv", "fgate", "mq", "mg")]
    segs += [(src["fg"], _OFF["small"][0] + FG_LANE0, FOX_HEADS),
             (src["lr"], _OFF["small"][0] + LR_LANE0, GLA_RANK)]
    return tuple(segs)


def _proj_kernel(x_ref, g_ref, w_in_ref, w_alpha_ref, b_alpha_ref, b_forget_ref, gla_g_ref,
                 gla_ref, fq_ref, fk_ref, fv_ref, fgate_ref,
                 mq_ref, mg_ref, crow_ref, cx_ref, cq_ref, qn2_ref, kn2_ref,
                 carry_ref, wt_ref, wa_ref, ba_ref, bf_ref, ng_ref, seg_ref,
                 gq_ref, gk_ref, gv_ref, gg_ref, loga_ref,
                 s_ref, lhs_ref, kv_ref, dec_ref, sprev_ref):
    rows = x_ref.shape[0]
    k_chunks = D_MODEL // LANES

    @pl.when(pl.program_id(0) == 0)
    def _():
        carry_ref[...] = jnp.zeros_like(carry_ref)
        s_ref[...] = jnp.zeros_like(s_ref)
        kn2_ref[...] = jnp.zeros_like(kn2_ref)
        gv_ref[...] = jnp.zeros_like(gv_ref)
        gg_ref[...] = jnp.zeros_like(gg_ref)
        wa_ref[...] = jnp.zeros_like(wa_ref)
        ba_ref[...] = jnp.zeros_like(ba_ref)
        bf_ref[...] = jnp.zeros_like(bf_ref)
        ng_ref[...] = jnp.zeros_like(ng_ref)
        for h in range(GLA_HEADS):
            src = slice(h * GLA_DK, (h + 1) * GLA_DK)
            dst = slice(h * GLA_DK_PAD, h * GLA_DK_PAD + GLA_DK)
            wa_ref[LR_LANE0:LR_LANE0 + GLA_RANK, dst] = w_alpha_ref[:, src]
            ba_ref[:, dst] = b_alpha_ref[:, src]
        bf_ref[:, FG_LANE0:FG_LANE0 + FOX_HEADS] = b_forget_ref[...]
        ng_ref[:, 0:GLA_DV] = gla_g_ref[...]
        seg_ref[...] = (lax.broadcasted_iota(jnp.int32, seg_ref.shape, 0) // FOX_DH
                        == lax.broadcasted_iota(jnp.int32, seg_ref.shape, 1)).astype(BF16)
        wt_ref[...] = jnp.zeros_like(wt_ref)
        for s0, d0, width in _w_in_segments():
            for c in range(k_chunks):
                wt_ref[d0:d0 + width, c * LANES:(c + 1) * LANES] = (
                    w_in_ref[pl.ds(s0 * k_chunks + c, width, stride=k_chunks), :].astype(BF16))

    x = x_ref[...]
    xn = (x * _rms_scale(x, D_MODEL) * g_ref[...]).astype(BF16)
    nt = (((1,), (1,)), ((), ()))

    def proj(first, last):
        lo, hi = _OFF[first][0], _OFF[last][1]
        y = lax.dot_general(xn, wt_ref[lo:hi, :], nt, preferred_element_type=F32)
        return lambda name: y[:, _OFF[name][0] - lo:_OFF[name][1] - lo]

    tail = proj("mq", "small")
    gla = proj("gq", "gg")
    small = tail("small")
    logf = _log_sigmoid(small + bf_ref[...])
    c = logf.T[0:SUBLANES, :]
    lane = lax.broadcasted_iota(jnp.int32, c.shape, 1)
    shift = 1
    while shift < rows:
        c = c + jnp.where(lane >= shift, pltpu.roll(c, shift, axis=1), 0.0)
        shift *= 2
    c = c + carry_ref[:, 0:1]
    crow_ref[...] = c
    carry_ref[...] = jnp.broadcast_to(c[:, rows - 1:rows], carry_ref.shape)
    neg = -c
    hi = neg.astype(BF16).astype(F32)
    mid = (neg - hi).astype(BF16).astype(F32)
    low = neg - hi - mid
    parts = jnp.concatenate(
        [hi, mid, low, jnp.ones_like(c), jnp.zeros((LANES - 4 * SUBLANES, rows), F32)], axis=0)
    cx_ref[...] = parts.astype(BF16)
    cq_ref[...] = parts.T.astype(BF16)

    z = (jnp.dot(small.astype(BF16), wa_ref[...].astype(BF16), preferred_element_type=F32)
         + ba_ref[...])
    loga_ref[...] = _log_sigmoid(z) * (1.0 / GLA_GATE_NORM)
    gq_ref[...] = gla("gq").astype(BF16)
    gk_ref[...] = gla("gk").astype(BF16)
    gv, gg = gla("gv").astype(BF16), _silu(gla("gg")).astype(BF16)
    for h in range(GLA_HEADS):
        src = slice(h * GLA_DV, (h + 1) * GLA_DV)
        dst = slice(h * GLA_DV_PAD, h * GLA_DV_PAD + GLA_DV)
        gv_ref[:, dst] = gv[:, src]
        gg_ref[:, dst] = gg[:, src]
    gla_local, gla_scan, gla_output = _gla_block(
        gq_ref, gk_ref, gv_ref, loga_ref, gg_ref, ng_ref, gla_ref,
        s_ref, lhs_ref, kv_ref, dec_ref, sprev_ref)

    def max_sq_norm(v):
        v32 = v.astype(F32)
        n2 = jnp.dot((v32 * v32).astype(BF16), seg_ref[...], preferred_element_type=F32)
        return jnp.max(n2, axis=0, keepdims=True)

    gla_local()
    fox_qk = proj("fq", "fk")
    mq_ref[...] = (tail("mq") * MEM_DH ** -0.5).astype(BF16)
    mg_ref[...] = _silu(tail("mg")).astype(BF16)
    gla_scan()
    fq = (fox_qk("fq") * FOX_DH ** -0.5).astype(BF16)
    fk = fox_qk("fk").astype(BF16)
    fq_ref[...] = fq
    fk_ref[...] = fox_qk("fk").T.astype(BF16)
    gla_output()
    fox_vg = proj("fv", "fgate")
    fv_ref[...] = fox_vg("fv").astype(BF16)
    fgate_ref[...] = _silu(fox_vg("fgate")).astype(BF16)
    qn2_ref[0] = max_sq_norm(fq)
    kn2_ref[0] = jnp.maximum(kn2_ref[0], max_sq_norm(fk))


def _memkv(mem_ref, g_ref, w_ref, mk_ref, mv_ref):
    m = mem_ref[...]
    mn = (m * _rms_scale(m, D_MODEL) * g_ref[...]).astype(BF16)
    kv = jnp.dot(mn, w_ref[...].astype(BF16), preferred_element_type=F32)
    mk_ref[...] = kv[:, :MEM_W].astype(BF16)
    mv_ref[...] = kv[:, MEM_W:].astype(BF16)


def _gla_block(q_ref, k_ref, v_ref, loga_ref, gate_ref, ng_ref, o_ref,
               s_ref, lhs_ref, kv_ref, dec_ref, sprev_ref):
    C = GLA_CHUNK
    W = HEAD_PAIR * GLA_DV_PAD
    n_chunks = q_ref.shape[0] // C

    row = lax.broadcasted_iota(jnp.int32, (C, LANES), 0)
    lane = lax.broadcasted_iota(jnp.int32, (C, LANES), 1)
    lo_k = lane < GLA_DK_PAD
    causal = row >= jnp.where(lo_k, lane, lane - GLA_DK_PAD)
    lo_v = lax.broadcasted_iota(jnp.int32, (C, W), 1) < GLA_DV_PAD
    st_row = lax.broadcasted_iota(jnp.int32, (LANES, W), 0)
    st_lane = lax.broadcasted_iota(jnp.int32, (LANES, W), 1)
    own = (st_row < GLA_DK_PAD) == (st_lane < GLA_DV_PAD)
    eye = (lax.broadcasted_iota(jnp.int32, (LANES, LANES), 0)
           == lax.broadcasted_iota(jnp.int32, (LANES, LANES), 1))
    scale = GLA_DK ** -0.5
    nt = (((1,), (1,)), ((), ()))
    tn = (((0,), (0,)), ((), ()))
    ng = jnp.concatenate([ng_ref[...]] * HEAD_PAIR, axis=1)

    pairs = range(GLA_HEADS // HEAD_PAIR)

    def local(ci):
        rs = slice(ci * C, (ci + 1) * C)
        for p in pairs:
            ls = slice(p * LANES, (p + 1) * LANES)
            vs = slice(p * W, (p + 1) * W)
            b = loga_ref[rs, ls]
            shift = 1
            while shift < C:
                b = b + jnp.where(row >= shift, pltpu.roll(b, shift, axis=0), 0.0)
                shift *= 2
            b_last = b[C - 1:C, :]
            k2 = k_ref[rs, ls].astype(F32)
            qd = (q_ref[rs, ls].astype(F32) * scale * jnp.exp(b)).astype(BF16)
            kd = (k2 * jnp.exp(-b)).astype(BF16)
            ke = (k2 * jnp.exp(b_last - b)).astype(BF16)
            zk = jnp.zeros_like(kd)
            kd_blk = jnp.concatenate([jnp.where(lo_k, kd, zk), jnp.where(lo_k, zk, kd)], axis=0)
            attn = lax.dot_general(qd, kd_blk, nt, preferred_element_type=F32)
            lhs_ref[rs, vs] = jnp.concatenate([jnp.where(causal, attn, 0.0).astype(BF16), qd], axis=1)
            kv = lax.dot_general(ke, v_ref[rs, vs], tn, preferred_element_type=F32)
            kv_ref[p, ci] = jnp.where(own, kv, 0.0)
            dcol = jnp.exp(jnp.sum(jnp.where(eye, jnp.broadcast_to(b_last, (LANES, LANES)), 0.0),
                                   axis=1, keepdims=True))
            dec_ref[p, ci] = jnp.broadcast_to(dcol, (LANES, LANES))

    def scan(ci):
        for p in pairs:
            s_prev = s_ref[p]
            sprev_ref[p, ci] = s_prev.astype(BF16)
            s_ref[p] = jnp.tile(dec_ref[p, ci], (1, HEAD_PAIR)) * s_prev + kv_ref[p, ci]

    def output(ci):
        rs = slice(ci * C, (ci + 1) * C)
        for p in pairs:
            vs = slice(p * W, (p + 1) * W)
            v2 = v_ref[rs, vs]
            zv = jnp.zeros_like(v2)
            v_blk = jnp.concatenate([jnp.where(lo_v, v2, zv), jnp.where(lo_v, zv, v2)], axis=0)
            o = jnp.dot(lhs_ref[rs, vs], jnp.concatenate([v_blk, sprev_ref[p, ci]], axis=0),
                        preferred_element_type=F32)
            o2 = o * o
            ms = jnp.where(lo_v, jnp.sum(o2[:, :GLA_DV_PAD], axis=1, keepdims=True),
                           jnp.sum(o2[:, GLA_DV_PAD:], axis=1, keepdims=True))
            on = o * lax.rsqrt(ms * (1.0 / GLA_DV) + EPS) * ng
            og = (on * gate_ref[rs, vs].astype(F32)).astype(BF16)
            for hh in range(HEAD_PAIR):
                c0 = (p * HEAD_PAIR + hh) * GLA_DV
                o_ref[rs, c0:c0 + GLA_DV] = og[:, hh * GLA_DV_PAD:hh * GLA_DV_PAD + GLA_DV]

    def all_chunks(phase):
        return lambda: [phase(ci) for ci in range(n_chunks)]

    return all_chunks(local), all_chunks(scan), all_chunks(output)


def _fox_kernel(qn2_ref, kn2_ref, cend_ref, q_ref, cq_ref, k_ref, cx_ref, v_ref, gate_ref, o_ref,
                m_ref, acc_ref):
    blk = FOX_KEYS
    streams = range(q_ref.shape[0] // blk)
    pair = pl.program_id(0)
    qi = pl.program_id(1)
    nblk = pl.num_programs(1) * len(streams)
    lane = lax.broadcasted_iota(jnp.int32, (1, LANES), 1)
    lo_lanes = lane < FOX_DH
    reps = blk // LANES
    diag = [qi * len(streams) + s for s in streams]
    heads = [pair * HEAD_PAIR + hh for hh in range(HEAD_PAIR)]

    q = q_ref[...]
    zero = jnp.zeros_like(q)
    q_lo, q_hi = jnp.where(lo_lanes, q, zero), jnp.where(lo_lanes, zero, q)
    q_stack = [jnp.concatenate([q_lo[s * blk:(s + 1) * blk], q_hi[s * blk:(s + 1) * blk]], axis=0)
               for s in streams]
    g_row = lax.broadcasted_iota(jnp.int32, (LANES, LANES), 0)
    g_col = lax.broadcasted_iota(jnp.int32, (LANES, LANES), 1)
    xlane = lax.broadcasted_iota(jnp.int32, (HEAD_PAIR * blk, LANES), 1)
    cq = cq_ref[...]

    def lane_map(h, shift):
        g = jnp.where((g_row == CX_ONE) & ((g_col == CX_HI + h) | (g_col == CX_MID + h)
                                           | (g_col == CX_LO + h)), 1.0, 0.0)
        if shift:
            for part, base in enumerate((CX_HI, CX_MID, CX_LO)):
                g = jnp.where((g_row == base + h) & (g_col == CX_ONE + part), -1.0, g)
        return g.astype(BF16)

    def q_aug(s, shift, void=None):
        cs = cq[s * blk:(s + 1) * blk]
        extra = jnp.concatenate([jnp.dot(cs, lane_map(h, shift), preferred_element_type=F32)
                                 for h in heads], axis=0)
        if void is not None:
            extra = jnp.where(jnp.logical_and(xlane == CX_ONE + 3, void), NEG_BIG, extra)
        return jnp.concatenate([q_stack[s], extra.astype(BF16)], axis=1)

    acc_ref[...] = jnp.zeros_like(acc_ref)
    qpos = lax.broadcasted_iota(jnp.int32, (blk, blk), 0)
    kpos = lax.broadcasted_iota(jnp.int32, (blk, blk), 1)

    def block(j):
        ks = pl.ds(pl.multiple_of(jnp.maximum(j, 0) * blk, blk), blk)
        k_aug = jnp.concatenate([k_ref[:, ks], cx_ref[:, ks]], axis=0)
        vb = v_ref[ks, :]
        one = jnp.ones_like(vb)
        return k_aug, (jnp.where(lo_lanes, vb, one), jnp.where(lo_lanes, one, vb))

    def step_online(s, j, qa, masked):
        rows = slice(s * blk, (s + 1) * blk)
        k_aug, vaug = block(j)
        s_all = jnp.dot(qa, k_aug, preferred_element_type=F32)
        for hh in range(HEAD_PAIR):
            sc = s_all[hh * blk:(hh + 1) * blk]
            if masked:
                sc = jnp.where(kpos <= qpos, sc, NEG_BIG)
            m_prev = m_ref[hh, rows]
            m_new = jnp.maximum(m_prev, jnp.max(sc, axis=1, keepdims=True))
            p = jnp.exp(sc - jnp.tile(m_new, (1, reps)))
            alpha = jnp.exp(m_prev - m_new)
            pv = jnp.dot(p.astype(BF16), vaug[hh], preferred_element_type=F32)
            acc_ref[hh, rows] = alpha * acc_ref[hh, rows] + pv
            m_ref[hh, rows] = m_new

    def step_direct(s, j, qa, masked, only=None):
        rows = slice(s * blk, (s + 1) * blk)
        k_aug, vaug = block(j)
        hsel = range(HEAD_PAIR) if only is None else (only,)
        lhs = qa if only is None else qa[only * blk:(only + 1) * blk]
        s_all = jnp.dot(lhs, k_aug, preferred_element_type=F32)
        for n, hh in enumerate(hsel):
            sc = s_all[n * blk:(n + 1) * blk]
            if masked:
                sc = jnp.where(kpos <= qpos, sc, NEG_BIG)
            acc_ref[hh, rows] += jnp.dot(jnp.exp(sc).astype(BF16), vaug[hh],
                                         preferred_element_type=F32)

    k_max2 = [kn2_ref[h] for h in heads]

    def norm2(s, hh):
        stat = (diag[s] * blk // PROJ_ROWS) * LANES
        return (NORM_SLACK * NORM_SLACK) * qn2_ref[stat + heads[hh]] * k_max2[hh]

    n2 = [[norm2(s, hh) for hh in range(HEAD_PAIR)] for s in streams]
    gap0 = [[FOX_SKIP_NATS + cend_ref[heads[hh] * nblk + jnp.maximum(diag[s] - 1, 0)]
             for hh in range(HEAD_PAIR)] for s in streams]

    def live_head(t, hh, direct):
        keep = False
        for s in streams:
            j = diag[s] - 1 - t
            gap = gap0[s][hh] - cend_ref[heads[hh] * nblk + jnp.maximum(j, 0)]
            bound2 = n2[s][hh] if direct else 4.0 * n2[s][hh]
            dead = jnp.logical_and(gap <= 0.0, bound2 <= gap * gap)
            keep = jnp.logical_or(keep, jnp.logical_and(j >= 0, jnp.logical_not(dead)))
        return keep

    def sweeps(stepper, direct):
        def sweep(back, masked=False, only=None):
            for s in streams:
                stepper(s, diag[s] - back, masked, only)

        def loop(t0, cond, only=None):
            def body(t):
                sweep(t + 1, only=only)
                return t + 1
            return lax.while_loop(cond, body, t0)

        def live(t, hh):
            return live_head(t, hh, direct)

        sweep(0, masked=True)
        sweep(1)
        if direct:
            t_both = loop(1, lambda t: jnp.logical_and(live(t, 0), live(t, 1)))
            for hh in range(HEAD_PAIR):
                loop(t_both, lambda t, hh=hh: live(t, hh), only=hh)
        else:
            loop(1, lambda t: jnp.logical_or(live(t, 0), live(t, 1)))

    direct_ok = True
    for s in streams:
        for hh in range(HEAD_PAIR):
            direct_ok = jnp.logical_and(direct_ok, n2[s][hh] <= FOX_DIRECT_NORM2)

    @pl.when(direct_ok)
    def _():
        q_dir = [q_aug(s, shift=True) for s in streams]
        q_void = [q_aug(s, shift=True, void=True) for s in streams]
        sweeps(lambda s, j, masked, only: step_direct(
            s, j, q_dir[s] if masked else jnp.where(j >= 0, q_dir[s], q_void[s]), masked, only),
            direct=True)

    @pl.when(jnp.logical_not(direct_ok))
    def _():
        m_ref[...] = jnp.full_like(m_ref, NEG_BIG)
        sweeps(lambda s, j, masked, only: step_online(
            s, j, q_aug(s, shift=False, void=j < 0), masked), direct=False)

    outs = []
    for hh in range(HEAD_PAIR):
        acc = acc_ref[hh]
        outs.append(acc / pltpu.roll(acc, FOX_DH, axis=1))
    o = jnp.where(lo_lanes, outs[0], outs[1])
    o_ref[...] = (o * gate_ref[...].astype(F32)).astype(BF16)


def _out_kernel(x_ref, gla_ref, fox_ref, mq_ref, mg_ref, mem_ref, mem_g_ref, w_mem_ref,
                w_out_ref, fg_ref, o_ref, wo_ref, mk_ref, mv_ref):
    @pl.when(pl.program_id(0) == 0)
    def _():
        wo_ref[...] = w_out_ref[...].astype(BF16)
        _memkv(mem_ref, mem_g_ref, w_mem_ref, mk_ref, mv_ref)

    lane = lax.broadcasted_iota(jnp.int32, (1, LANES), 1)
    lo_lanes = lane < MEM_DH
    nt = (((1,), (1,)), ((), ()))
    mem_parts = []
    for p in range(MEM_HEADS // HEAD_PAIR):
        ls = slice(p * LANES, (p + 1) * LANES)
        q = mq_ref[:, ls]
        kb = mk_ref[:, ls]
        vb = mv_ref[:, ls]
        zero = jnp.zeros_like(q)
        one = jnp.ones_like(vb)
        qh = (jnp.where(lo_lanes, q, zero), jnp.where(lo_lanes, zero, q))
        vaug = (jnp.where(lo_lanes, vb, one), jnp.where(lo_lanes, one, vb))
        outs = []
        for hh in range(HEAD_PAIR):
            s = lax.dot_general(qh[hh], kb, nt, preferred_element_type=F32)
            pexp = jnp.exp(s - jnp.max(s, axis=1, keepdims=True))
            pv = jnp.dot(pexp.astype(BF16), vaug[hh], preferred_element_type=F32)
            outs.append(pv / pltpu.roll(pv, MEM_DH, axis=1))
        o = jnp.where(lo_lanes, outs[0], outs[1])
        mem_parts.append((o * mg_ref[:, ls].astype(F32)).astype(BF16))
    mixed = jnp.concatenate([gla_ref[...], fox_ref[...]] + mem_parts, axis=1)
    y = x_ref[...] + jnp.dot(mixed, wo_ref[...], preferred_element_type=F32)
    o_ref[...] = y * _rms_scale(y, D_MODEL) * fg_ref[...]


def _params(*sem):
    return pltpu.CompilerParams(dimension_semantics=sem, vmem_limit_bytes=VMEM_LIMIT)


def _layer(x, mem, norm_g, w_in, w_alpha_up, b_alpha, b_forget, gla_norm_g,
           mem_norm_g, w_mem_kv, w_out, out_g):
    w_in_t = jnp.transpose(w_in[None], (0, 2, 1)).reshape(-1, LANES)
    T = x.shape[0]
    M = mem.shape[0]

    def rows(width, n=PROJ_ROWS):
        return pl.BlockSpec((n, width), lambda i: (i, 0))

    def whole(shape):
        return pl.BlockSpec(shape, lambda i: (0,) * len(shape))

    def once(shape):
        return pl.BlockSpec(shape, lambda i: (0,) * len(shape), pipeline_mode=pl.Buffered(1))

    bshape = lambda w: jax.ShapeDtypeStruct((T, w), BF16)
    nproj = T // PROJ_ROWS
    stat_spec = pl.BlockSpec((1, 1, LANES), lambda i: (i, 0, 0))
    stat_shape = jax.ShapeDtypeStruct((nproj, 1, LANES), F32)
    gla_pairs, gla_chunks = GLA_HEADS // HEAD_PAIR, PROJ_ROWS // GLA_CHUNK
    pair_w = HEAD_PAIR * GLA_DV_PAD
    (gla, fq, fk, fv, fgate, mq, mg, crow, cx, cq, qn2, kn2) = pl.pallas_call(
        _proj_kernel,
        grid=(nproj,),
        in_specs=[rows(D_MODEL), whole((1, D_MODEL)),
                  pl.BlockSpec(w_in_t.shape, lambda i: (0, 0), pipeline_mode=pl.Buffered(1)),
                  whole(w_alpha_up.shape), whole((1, GLA_HEADS * GLA_DK)),
                  whole((1, FOX_HEADS)), whole((1, GLA_DV))],
        out_specs=[rows(GLA_OUT_W),
                   rows(FOX_W), pl.BlockSpec((FOX_W, PROJ_ROWS), lambda i: (0, i)),
                   rows(FOX_W), rows(FOX_W),
                   rows(MEM_W), rows(MEM_W),
                   pl.BlockSpec((SUBLANES, PROJ_ROWS), lambda i: (0, i)),
                   pl.BlockSpec((LANES, PROJ_ROWS), lambda i: (0, i)), rows(LANES),
                   stat_spec, pl.BlockSpec((1, 1, LANES), lambda i: (0, 0, 0))],
        out_shape=[bshape(GLA_OUT_W),
                   bshape(FOX_W), jax.ShapeDtypeStruct((FOX_W, T), BF16),
                   bshape(FOX_W), bshape(FOX_W),
                   bshape(MEM_W), bshape(MEM_W),
                   jax.ShapeDtypeStruct((SUBLANES, T), F32),
                   jax.ShapeDtypeStruct((LANES, T), BF16), bshape(LANES),
                   stat_shape, jax.ShapeDtypeStruct((1, 1, LANES), F32)],
        scratch_shapes=[
            pltpu.VMEM((SUBLANES, LANES), F32),
            pltpu.VMEM((IN_COLS_PAD, D_MODEL), BF16),
            pltpu.VMEM((SMALL_W, GLA_QK_W), F32),
            pltpu.VMEM((1, GLA_QK_W), F32),
            pltpu.VMEM((1, SMALL_W), F32),
            pltpu.VMEM((1, GLA_DV_PAD), F32),
            pltpu.VMEM((FOX_W, LANES), BF16),
            pltpu.VMEM((PROJ_ROWS, GLA_QK_W), BF16),
            pltpu.VMEM((PROJ_ROWS, GLA_QK_W), BF16),
            pltpu.VMEM((PROJ_ROWS, GLA_V_W), BF16),
            pltpu.VMEM((PROJ_ROWS, GLA_V_W), BF16),
            pltpu.VMEM((PROJ_ROWS, GLA_QK_W), F32),
            pltpu.VMEM((gla_pairs, LANES, pair_w), F32),
            pltpu.VMEM((PROJ_ROWS, GLA_V_W), BF16),
            pltpu.VMEM((gla_pairs, gla_chunks, LANES, pair_w), F32),
            pltpu.VMEM((gla_pairs, gla_chunks, LANES, LANES), F32),
            pltpu.VMEM((gla_pairs, gla_chunks, LANES, pair_w), BF16)],
        compiler_params=_params("arbitrary"),
        name="proj",
    )(x, norm_g[None, :], w_in_t, w_alpha_up, b_alpha[None, :], b_forget[None, :],
      gla_norm_g[None, :])

    cend = crow[:FOX_HEADS, FOX_KEYS - 1::FOX_KEYS].reshape(-1)
    pair_rows = pl.BlockSpec((FOX_BLOCK, LANES), lambda p, i, *_: (i, p))
    pair_all = pl.BlockSpec((T, LANES), lambda p, i, *_: (0, p))
    fox = pl.pallas_call(
        _fox_kernel,
        grid_spec=pltpu.PrefetchScalarGridSpec(
            num_scalar_prefetch=3,
            grid=(FOX_HEADS // HEAD_PAIR, T // FOX_BLOCK),
            in_specs=[pair_rows, pl.BlockSpec((FOX_BLOCK, LANES), lambda p, i, *_: (i, 0)),
                      pl.BlockSpec((LANES, T), lambda p, i, *_: (p, 0)),
                      pl.BlockSpec((LANES, T), lambda p, i, *_: (0, 0)), pair_all, pair_rows],
            out_specs=pair_rows,
            scratch_shapes=[pltpu.VMEM((HEAD_PAIR, FOX_BLOCK, LANES), F32),
                            pltpu.VMEM((HEAD_PAIR, FOX_BLOCK, LANES), F32)]),
        out_shape=bshape(FOX_W),
        compiler_params=_params("arbitrary", "arbitrary"),
        name="fox",
    )(qn2.reshape(-1), kn2.reshape(-1), cend, fq, cq, fk, cx, fv, fgate)

    out = pl.pallas_call(
        _out_kernel,
        grid=(T // OUT_ROWS,),
        in_specs=[rows(D_MODEL, OUT_ROWS), rows(GLA_OUT_W, OUT_ROWS), rows(FOX_W, OUT_ROWS),
                  rows(MEM_W, OUT_ROWS), rows(MEM_W, OUT_ROWS),
                  once(mem.shape), once((1, D_MODEL)), once(w_mem_kv.shape), once(w_out.shape),
                  whole((1, D_MODEL))],
        out_specs=rows(D_MODEL, OUT_ROWS),
        out_shape=jax.ShapeDtypeStruct((T, D_MODEL), F32),
        scratch_shapes=[pltpu.VMEM((D_MODEL, D_MODEL), BF16),
                        pltpu.VMEM((M, MEM_W), BF16), pltpu.VMEM((M, MEM_W), BF16)],
        compiler_params=_params("arbitrary"),
        name="out",
    )(x, gla, fox, mq, mg, mem, mem_norm_g[None, :], w_mem_kv, w_out, out_g[None, :])
    return out


def kernel(x, mem, norm_g, w_in, w_alpha_up, b_alpha, b_forget, gla_norm_g, mem_norm_g,
           w_mem_kv, w_out, final_norm_g):
    assert x.shape[0] == 1 and mem.shape[0] == 1 and norm_g.shape[0] == 1
    assert x.shape[1] % max(PROJ_ROWS, FOX_BLOCK, OUT_ROWS) == 0
    out = _layer(x[0], mem[0], norm_g[0], w_in[0], w_alpha_up[0], b_alpha[0], b_forget[0],
                 gla_norm_g[0], mem_norm_g[0], w_mem_kv[0], w_out[0], final_norm_g)
    return out[None]
```

```python
import jax
import jax.numpy as jnp
from jax import lax
from jax.experimental import pallas as pl
from jax.experimental.pallas import tpu as pltpu

F32 = jnp.float32
BF16 = jnp.bfloat16

EPS = 1e-6
LANES = 128
SUBLANES = 8

D_MODEL = 1024
GLA_HEADS, GLA_DK, GLA_DV, GLA_RANK = 4, 48, 96, 16
GLA_DK_PAD = 64
GLA_DV_PAD = LANES
GLA_GATE_NORM = 16.0
GLA_CHUNK = 64
FOX_HEADS, FOX_DH = 6, 64
MEM_HEADS, MEM_DH = 4, 64
HEAD_PAIR = 2
GLA_QK_W = GLA_HEADS * GLA_DK_PAD
GLA_V_W = GLA_HEADS * GLA_DV_PAD
GLA_OUT_W = GLA_HEADS * GLA_DV
FOX_W = FOX_HEADS * FOX_DH
MEM_W = MEM_HEADS * MEM_DH
SMALL_W = LANES
FG_LANE0 = 0
LR_LANE0 = SUBLANES

_GROUPS = (("gq", GLA_QK_W), ("gk", GLA_QK_W), ("gv", GLA_OUT_W), ("gg", GLA_OUT_W),
           ("fq", FOX_W), ("fk", FOX_W), ("fv", FOX_W), ("fgate", FOX_W),
           ("mq", MEM_W), ("mg", MEM_W), ("small", SMALL_W))
_OFF = {}
_o = 0
for _n, _w in _GROUPS:
    _OFF[_n] = (_o, _o + _w)
    _o += _w
IN_COLS_PAD = _o

PROJ_ROWS = 1024
FOX_BLOCK = 2048
FOX_KEYS = 256
OUT_ROWS = 1024
VMEM_LIMIT = 56 * 1024 * 1024

NEG_BIG = -1e30
FOX_SKIP_NATS = 105.0
NORM_SLACK = 1.02
FOX_DIRECT_NORM2 = 3600.0
CX_HI, CX_MID, CX_LO, CX_ONE = 0, 8, 16, 24


def _log_sigmoid(z):
    return jnp.minimum(z, 0.0) - jnp.log(1.0 + jnp.exp(-jnp.abs(z)))


def _silu(z):
    return z / (1.0 + jnp.exp(-z))


def _rms_scale(v, width):
    return lax.rsqrt(jnp.sum(v * v, axis=-1, keepdims=True) * (1.0 / width) + EPS)


def _w_in_segments():
    qk, gw = GLA_HEADS * GLA_DK, GLA_HEADS * GLA_DV
    src = {}
    o = 0
    for name, width in (("gq", qk), ("gk", qk), ("gv", gw), ("lr", GLA_RANK), ("gg", gw),
                        ("fq", FOX_W), ("fk", FOX_W), ("fv", FOX_W), ("fg", FOX_HEADS),
                        ("fgate", FOX_W), ("mq", MEM_W), ("mg", MEM_W)):
        src[name] = o
        o += width
    segs = []
    for name, d, d_pad in (("gq", GLA_DK, GLA_DK_PAD), ("gk", GLA_DK, GLA_DK_PAD)):
        segs += [(src[name] + h * d, _OFF[name][0] + h * d_pad, d) for h in range(GLA_HEADS)]
    segs += [(src[name], _OFF[name][0], _OFF[name][1] - _OFF[name][0])
             for name in ("gv", "gg", "fq", "fk", "fv", "fgate", "mq", "mg")]
    segs += [(src["fg"], _OFF["small"][0] + FG_LANE0, FOX_HEADS),
             (src["lr"], _OFF["small"][0] + LR_LANE0, GLA_RANK)]
    return tuple(segs)


def _proj_kernel(x_ref, g_ref, w_in_ref, w_alpha_ref, b_alpha_ref, b_forget_ref, gla_g_ref,
                 gla_ref, fqvg_ref, kx_ref, mqg_ref, crow_ref, cq_ref, qn2_ref, kn2_ref,
                 carry_ref, wt_ref, wa_ref, ba_ref, bf_ref, ng_ref, seg_ref,
                 gq_ref, gk_ref, gv_ref, gg_ref, loga_ref,
                 s_ref, lhs_ref, kv_ref, dec_ref, sprev_ref):
    rows = x_ref.shape[0]
    k_chunks = D_MODEL // LANES

    @pl.when(pl.program_id(0) == 0)
    def _():
        carry_ref[...] = jnp.zeros_like(carry_ref)
        s_ref[...] = jnp.zeros_like(s_ref)
        kn2_ref[...] = jnp.zeros_like(kn2_ref)
        gv_ref[...] = jnp.zeros_like(gv_ref)
        gg_ref[...] = jnp.zeros_like(gg_ref)
        wa_ref[...] = jnp.zeros_like(wa_ref)
        ba_ref[...] = jnp.zeros_like(ba_ref)
        bf_ref[...] = jnp.zeros_like(bf_ref)
        ng_ref[...] = jnp.zeros_like(ng_ref)
        for h in range(GLA_HEADS):
            src = slice(h * GLA_DK, (h + 1) * GLA_DK)
            dst = slice(h * GLA_DK_PAD, h * GLA_DK_PAD + GLA_DK)
            wa_ref[LR_LANE0:LR_LANE0 + GLA_RANK, dst] = w_alpha_ref[:, src]
            ba_ref[:, dst] = b_alpha_ref[:, src]
        bf_ref[:, FG_LANE0:FG_LANE0 + FOX_HEADS] = b_forget_ref[...]
        ng_ref[:, 0:GLA_DV] = gla_g_ref[...]
        seg_ref[...] = (lax.broadcasted_iota(jnp.int32, seg_ref.shape, 0) // FOX_DH
                        == lax.broadcasted_iota(jnp.int32, seg_ref.shape, 1)).astype(BF16)
        wt_ref[...] = jnp.zeros_like(wt_ref)
        for s0, d0, width in _w_in_segments():
            for c in range(k_chunks):
                wt_ref[d0:d0 + width, c * LANES:(c + 1) * LANES] = (
                    w_in_ref[pl.ds(s0 * k_chunks + c, width, stride=k_chunks), :].astype(BF16))

    x = x_ref[...]
    xn = (x * _rms_scale(x, D_MODEL) * g_ref[...]).astype(BF16)
    nt = (((1,), (1,)), ((), ()))

    def proj(first, last):
        lo, hi = _OFF[first][0], _OFF[last][1]
        y = lax.dot_general(xn, wt_ref[lo:hi, :], nt, preferred_element_type=F32)
        return lambda name: y[:, _OFF[name][0] - lo:_OFF[name][1] - lo]

    tail = proj("mq", "small")
    gla = proj("gq", "gg")
    small = tail("small")
    logf = _log_sigmoid(small + bf_ref[...])
    c = logf.T[0:SUBLANES, :]
    lane = lax.broadcasted_iota(jnp.int32, c.shape, 1)
    shift = 1
    while shift < rows:
        c = c + jnp.where(lane >= shift, pltpu.roll(c, shift, axis=1), 0.0)
        shift *= 2
    c = c + carry_ref[:, 0:1]
    crow_ref[...] = c
    carry_ref[...] = jnp.broadcast_to(c[:, rows - 1:rows], carry_ref.shape)
    neg = -c
    hi = neg.astype(BF16).astype(F32)
    mid = (neg - hi).astype(BF16).astype(F32)
    low = neg - hi - mid
    parts = jnp.concatenate(
        [hi, mid, low, jnp.ones_like(c), jnp.zeros((LANES - 4 * SUBLANES, rows), F32)], axis=0)
    kx_ref[FOX_W:, :] = parts.astype(BF16)
    cq_ref[...] = parts.T.astype(BF16)

    z = (jnp.dot(small.astype(BF16), wa_ref[...].astype(BF16), preferred_element_type=F32)
         + ba_ref[...])
    loga_ref[...] = _log_sigmoid(z) * (1.0 / GLA_GATE_NORM)
    gq_ref[...] = gla("gq").astype(BF16)
    gk_ref[...] = gla("gk").astype(BF16)
    gv, gg = gla("gv").astype(BF16), _silu(gla("gg")).astype(BF16)
    for h in range(GLA_HEADS):
        src = slice(h * GLA_DV, (h + 1) * GLA_DV)
        dst = slice(h * GLA_DV_PAD, h * GLA_DV_PAD + GLA_DV)
        gv_ref[:, dst] = gv[:, src]
        gg_ref[:, dst] = gg[:, src]
    gla_local, gla_scan, gla_output = _gla_block(
        gq_ref, gk_ref, gv_ref, loga_ref, gg_ref, ng_ref, gla_ref,
        s_ref, lhs_ref, kv_ref, dec_ref, sprev_ref)

    def max_sq_norm(v):
        v32 = v.astype(F32)
        n2 = jnp.dot((v32 * v32).astype(BF16), seg_ref[...], preferred_element_type=F32)
        return jnp.max(n2, axis=0, keepdims=True)

    gla_local()
    fox_qk = proj("fq", "fk")
    mqg_ref[:, :MEM_W] = (tail("mq") * MEM_DH ** -0.5).astype(BF16)
    mqg_ref[:, MEM_W:] = _silu(tail("mg")).astype(BF16)
    gla_scan()
    fq = (fox_qk("fq") * FOX_DH ** -0.5).astype(BF16)
    fk = fox_qk("fk").astype(BF16)
    fqvg_ref[:, :FOX_W] = fq
    kx_ref[:FOX_W, :] = fox_qk("fk").T.astype(BF16)
    gla_output()
    fox_vg = proj("fv", "fgate")
    fqvg_ref[:, FOX_W:2 * FOX_W] = fox_vg("fv").astype(BF16)
    fqvg_ref[:, 2 * FOX_W:] = _silu(fox_vg("fgate")).astype(BF16)
    qn2_ref[0] = max_sq_norm(fq)
    kn2_ref[0] = jnp.maximum(kn2_ref[0], max_sq_norm(fk))


def _memkv(mem_ref, g_ref, w_ref, mk_ref, mv_ref):
    m = mem_ref[...]
    mn = (m * _rms_scale(m, D_MODEL) * g_ref[...]).astype(BF16)
    kv = jnp.dot(mn, w_ref[...].astype(BF16), preferred_element_type=F32)
    mk_ref[...] = kv[:, :MEM_W].astype(BF16)
    mv_ref[...] = kv[:, MEM_W:].astype(BF16)


def _gla_block(q_ref, k_ref, v_ref, loga_ref, gate_ref, ng_ref, o_ref,
               s_ref, lhs_ref, kv_ref, dec_ref, sprev_ref):
    C = GLA_CHUNK
    W = HEAD_PAIR * GLA_DV_PAD
    n_chunks = q_ref.shape[0] // C

    row = lax.broadcasted_iota(jnp.int32, (C, LANES), 0)
    lane = lax.broadcasted_iota(jnp.int32, (C, LANES), 1)
    lo_k = lane < GLA_DK_PAD
    causal = row >= jnp.where(lo_k, lane, lane - GLA_DK_PAD)
    lo_v = lax.broadcasted_iota(jnp.int32, (C, W), 1) < GLA_DV_PAD
    st_row = lax.broadcasted_iota(jnp.int32, (LANES, W), 0)
    st_lane = lax.broadcasted_iota(jnp.int32, (LANES, W), 1)
    own = (st_row < GLA_DK_PAD) == (st_lane < GLA_DV_PAD)
    eye = (lax.broadcasted_iota(jnp.int32, (LANES, LANES), 0)
           == lax.broadcasted_iota(jnp.int32, (LANES, LANES), 1))
    scale = GLA_DK ** -0.5
    nt = (((1,), (1,)), ((), ()))
    tn = (((0,), (0,)), ((), ()))
    ng = jnp.concatenate([ng_ref[...]] * HEAD_PAIR, axis=1)

    pairs = range(GLA_HEADS // HEAD_PAIR)

    def local(ci):
        rs = slice(ci * C, (ci + 1) * C)
        for p in pairs:
            ls = slice(p * LANES, (p + 1) * LANES)
            vs = slice(p * W, (p + 1) * W)
            b = loga_ref[rs, ls]
            shift = 1
            while shift < C:
                b = b + jnp.where(row >= shift, pltpu.roll(b, shift, axis=0), 0.0)
                shift *= 2
            b_last = b[C - 1:C, :]
            k2 = k_ref[rs, ls].astype(F32)
            qd = (q_ref[rs, ls].astype(F32) * scale * jnp.exp(b)).astype(BF16)
            kd = (k2 * jnp.exp(-b)).astype(BF16)
            ke = (k2 * jnp.exp(b_last - b)).astype(BF16)
            zk = jnp.zeros_like(kd)
            kd_blk = jnp.concatenate([jnp.where(lo_k, kd, zk), jnp.where(lo_k, zk, kd)], axis=0)
            attn = lax.dot_general(qd, kd_blk, nt, preferred_element_type=F32)
            lhs_ref[rs, vs] = jnp.concatenate([jnp.where(causal, attn, 0.0).astype(BF16), qd], axis=1)
            kv = lax.dot_general(ke, v_ref[rs, vs], tn, preferred_element_type=F32)
            kv_ref[p, ci] = jnp.where(own, kv, 0.0)
            dcol = jnp.exp(jnp.sum(jnp.where(eye, jnp.broadcast_to(b_last, (LANES, LANES)), 0.0),
                                   axis=1, keepdims=True))
            dec_ref[p, ci] = jnp.broadcast_to(dcol, (LANES, LANES))

    def scan(ci):
        for p in pairs:
            s_prev = s_ref[p]
            sprev_ref[p, ci] = s_prev.astype(BF16)
            s_ref[p] = jnp.tile(dec_ref[p, ci], (1, HEAD_PAIR)) * s_prev + kv_ref[p, ci]

    def output(ci):
        rs = slice(ci * C, (ci + 1) * C)
        for p in pairs:
            vs = slice(p * W, (p + 1) * W)
            v2 = v_ref[rs, vs]
            zv = jnp.zeros_like(v2)
            v_blk = jnp.concatenate([jnp.where(lo_v, v2, zv), jnp.where(lo_v, zv, v2)], axis=0)
            o = jnp.dot(lhs_ref[rs, vs], jnp.concatenate([v_blk, sprev_ref[p, ci]], axis=0),
                        preferred_element_type=F32)
            o2 = o * o
            ms = jnp.where(lo_v, jnp.sum(o2[:, :GLA_DV_PAD], axis=1, keepdims=True),
                           jnp.sum(o2[:, GLA_DV_PAD:], axis=1, keepdims=True))
            on = o * lax.rsqrt(ms * (1.0 / GLA_DV) + EPS) * ng
            og = (on * gate_ref[rs, vs].astype(F32)).astype(BF16)
            for hh in range(HEAD_PAIR):
                c0 = (p * HEAD_PAIR + hh) * GLA_DV
                o_ref[rs, c0:c0 + GLA_DV] = og[:, hh * GLA_DV_PAD:hh * GLA_DV_PAD + GLA_DV]

    def all_chunks(phase):
        return lambda: [phase(ci) for ci in range(n_chunks)]

    return all_chunks(local), all_chunks(scan), all_chunks(output)


def _fox_kernel(qn2_ref, kn2_ref, cend_ref, q_ref, cq_ref, k_ref, cx_ref, v_ref, gate_ref, o_ref,
                m_ref, acc_ref):
    blk = FOX_KEYS
    streams = range(q_ref.shape[0] // blk)
    pair = pl.program_id(0)
    qi = pl.program_id(1)
    nblk = pl.num_programs(1) * len(streams)
    lane = lax.broadcasted_iota(jnp.int32, (1, LANES), 1)
    lo_lanes = lane < FOX_DH
    reps = blk // LANES
    diag = [qi * len(streams) + s for s in streams]
    heads = [pair * HEAD_PAIR + hh for hh in range(HEAD_PAIR)]

    q = q_ref[...]
    zero = jnp.zeros_like(q)
    q_lo, q_hi = jnp.where(lo_lanes, q, zero), jnp.where(lo_lanes, zero, q)
    q_stack = [jnp.concatenate([q_lo[s * blk:(s + 1) * blk], q_hi[s * blk:(s + 1) * blk]], axis=0)
               for s in streams]
    g_row = lax.broadcasted_iota(jnp.int32, (LANES, LANES), 0)
    g_col = lax.broadcasted_iota(jnp.int32, (LANES, LANES), 1)
    xlane = lax.broadcasted_iota(jnp.int32, (HEAD_PAIR * blk, LANES), 1)
    cq = cq_ref[...]

    def lane_map(h, shift):
        g = jnp.where((g_row == CX_ONE) & ((g_col == CX_HI + h) | (g_col == CX_MID + h)
                                           | (g_col == CX_LO + h)), 1.0, 0.0)
        if shift:
            for part, base in enumerate((CX_HI, CX_MID, CX_LO)):
                g = jnp.where((g_row == base + h) & (g_col == CX_ONE + part), -1.0, g)
        return g.astype(BF16)

    def q_aug(s, shift, void=None):
        cs = cq[s * blk:(s + 1) * blk]
        extra = jnp.concatenate([jnp.dot(cs, lane_map(h, shift), preferred_element_type=F32)
                                 for h in heads], axis=0)
        if void is not None:
            extra = jnp.where(jnp.logical_and(xlane == CX_ONE + 3, void), NEG_BIG, extra)
        return jnp.concatenate([q_stack[s], extra.astype(BF16)], axis=1)

    acc_ref[...] = jnp.zeros_like(acc_ref)
    qpos = lax.broadcasted_iota(jnp.int32, (blk, blk), 0)
    kpos = lax.broadcasted_iota(jnp.int32, (blk, blk), 1)

    def block(j):
        ks = pl.ds(pl.multiple_of(jnp.maximum(j, 0) * blk, blk), blk)
        k_aug = jnp.concatenate([k_ref[:, ks], cx_ref[:, ks]], axis=0)
        vb = v_ref[ks, :]
        one = jnp.ones_like(vb)
        return k_aug, (jnp.where(lo_lanes, vb, one), jnp.where(lo_lanes, one, vb))

    def step_online(s, j, qa, masked):
        rows = slice(s * blk, (s + 1) * blk)
        k_aug, vaug = block(j)
        s_all = jnp.dot(qa, k_aug, preferred_element_type=F32)
        for hh in range(HEAD_PAIR):
            sc = s_all[hh * blk:(hh + 1) * blk]
            if masked:
                sc = jnp.where(kpos <= qpos, sc, NEG_BIG)
            m_prev = m_ref[hh, rows]
            m_new = jnp.maximum(m_prev, jnp.max(sc, axis=1, keepdims=True))
            p = jnp.exp(sc - jnp.tile(m_new, (1, reps)))
            alpha = jnp.exp(m_prev - m_new)
            pv = jnp.dot(p.astype(BF16), vaug[hh], preferred_element_type=F32)
            acc_ref[hh, rows] = alpha * acc_ref[hh, rows] + pv
            m_ref[hh, rows] = m_new

    def step_direct(s, j, qa, masked, only=None):
        rows = slice(s * blk, (s + 1) * blk)
        k_aug, vaug = block(j)
        hsel = range(HEAD_PAIR) if only is None else (only,)
        lhs = qa if only is None else qa[only * blk:(only + 1) * blk]
        s_all = jnp.dot(lhs, k_aug, preferred_element_type=F32)
        for n, hh in enumerate(hsel):
            sc = s_all[n * blk:(n + 1) * blk]
            if masked:
                sc = jnp.where(kpos <= qpos, sc, NEG_BIG)
            acc_ref[hh, rows] += jnp.dot(jnp.exp(sc).astype(BF16), vaug[hh],
                                         preferred_element_type=F32)

    k_max2 = [kn2_ref[h] for h in heads]

    def norm2(s, hh):
        stat = (diag[s] * blk // PROJ_ROWS) * LANES
        return (NORM_SLACK * NORM_SLACK) * qn2_ref[stat + heads[hh]] * k_max2[hh]

    n2 = [[norm2(s, hh) for hh in range(HEAD_PAIR)] for s in streams]
    gap0 = [[FOX_SKIP_NATS + cend_ref[heads[hh] * nblk + jnp.maximum(diag[s] - 1, 0)]
             for hh in range(HEAD_PAIR)] for s in streams]

    def live_head(t, hh, direct):
        keep = False
        for s in streams:
            j = diag[s] - 1 - t
            gap = gap0[s][hh] - cend_ref[heads[hh] * nblk + jnp.maximum(j, 0)]
            bound2 = n2[s][hh] if direct else 4.0 * n2[s][hh]
            dead = jnp.logical_and(gap <= 0.0, bound2 <= gap * gap)
            keep = jnp.logical_or(keep, jnp.logical_and(j >= 0, jnp.logical_not(dead)))
        return keep

    def sweeps(stepper, direct):
        def sweep(back, masked=False, only=None):
            for s in streams:
                stepper(s, diag[s] - back, masked, only)

        def loop(t0, cond, only=None):
            def body(t):
                sweep(t + 1, only=only)
                return t + 1
            return lax.while_loop(cond, body, t0)

        def live(t, hh):
            return live_head(t, hh, direct)

        sweep(0, masked=True)
        sweep(1)
        if direct:
            t_both = loop(1, lambda t: jnp.logical_and(live(t, 0), live(t, 1)))
            for hh in range(HEAD_PAIR):
                loop(t_both, lambda t, hh=hh: live(t, hh), only=hh)
        else:
            loop(1, lambda t: jnp.logical_or(live(t, 0), live(t, 1)))

    direct_ok = True
    for s in streams:
        for hh in range(HEAD_PAIR):
            direct_ok = jnp.logical_and(direct_ok, n2[s][hh] <= FOX_DIRECT_NORM2)

    @pl.when(direct_ok)
    def _():
        q_dir = [q_aug(s, shift=True) for s in streams]
        q_void = [q_aug(s, shift=True, void=True) for s in streams]
        sweeps(lambda s, j, masked, only: step_direct(
            s, j, q_dir[s] if masked else jnp.where(j >= 0, q_dir[s], q_void[s]), masked, only),
            direct=True)

    @pl.when(jnp.logical_not(direct_ok))
    def _():
        m_ref[...] = jnp.full_like(m_ref, NEG_BIG)
        sweeps(lambda s, j, masked, only: step_online(
            s, j, q_aug(s, shift=False, void=j < 0), masked), direct=False)

    outs = []
    for hh in range(HEAD_PAIR):
        acc = acc_ref[hh]
        outs.append(acc / pltpu.roll(acc, FOX_DH, axis=1))
    o = jnp.where(lo_lanes, outs[0], outs[1])
    o_ref[...] = (o * gate_ref[...].astype(F32)).astype(BF16)


def _out_kernel(x_ref, gla_ref, fox_ref, mq_ref, mg_ref, mem_ref, mem_g_ref, w_mem_ref,
                w_out_ref, fg_ref, o_ref, wo_ref, mk_ref, mv_ref):
    @pl.when(pl.program_id(0) == 0)
    def _():
        wo_ref[...] = w_out_ref[...].astype(BF16)
        _memkv(mem_ref, mem_g_ref, w_mem_ref, mk_ref, mv_ref)

    lane = lax.broadcasted_iota(jnp.int32, (1, LANES), 1)
    lo_lanes = lane < MEM_DH
    nt = (((1,), (1,)), ((), ()))
    mem_parts = []
    for p in range(MEM_HEADS // HEAD_PAIR):
        ls = slice(p * LANES, (p + 1) * LANES)
        q = mq_ref[:, ls]
        kb = mk_ref[:, ls]
        vb = mv_ref[:, ls]
        zero = jnp.zeros_like(q)
        one = jnp.ones_like(vb)
        qh = (jnp.where(lo_lanes, q, zero), jnp.where(lo_lanes, zero, q))
        vaug = (jnp.where(lo_lanes, vb, one), jnp.where(lo_lanes, one, vb))
        outs = []
        for hh in range(HEAD_PAIR):
            s = lax.dot_general(qh[hh], kb, nt, preferred_element_type=F32)
            pexp = jnp.exp(s - jnp.max(s, axis=1, keepdims=True))
            pv = jnp.dot(pexp.astype(BF16), vaug[hh], preferred_element_type=F32)
            outs.append(pv / pltpu.roll(pv, MEM_DH, axis=1))
        o = jnp.where(lo_lanes, outs[0], outs[1])
        mem_parts.append((o * mg_ref[:, ls].astype(F32)).astype(BF16))
    mixed = jnp.concatenate([gla_ref[...], fox_ref[...]] + mem_parts, axis=1)
    y = x_ref[...] + jnp.dot(mixed, wo_ref[...], preferred_element_type=F32)
    o_ref[...] = y * _rms_scale(y, D_MODEL) * fg_ref[...]


def _params(*sem):
    return pltpu.CompilerParams(dimension_semantics=sem, vmem_limit_bytes=VMEM_LIMIT)


def _layer(x, mem, norm_g, w_in, w_alpha_up, b_alpha, b_forget, gla_norm_g,
           mem_norm_g, w_mem_kv, w_out, out_g):
    w_in_t = jnp.transpose(w_in[None], (0, 2, 1)).reshape(-1, LANES)
    T = x.shape[0]
    M = mem.shape[0]

    def rows(width, n=PROJ_ROWS):
        return pl.BlockSpec((n, width), lambda i: (i, 0))

    def whole(shape):
        return pl.BlockSpec(shape, lambda i: (0,) * len(shape))

    def once(shape):
        return pl.BlockSpec(shape, lambda i: (0,) * len(shape), pipeline_mode=pl.Buffered(1))

    bshape = lambda w: jax.ShapeDtypeStruct((T, w), BF16)
    nproj = T // PROJ_ROWS
    stat_spec = pl.BlockSpec((1, 1, LANES), lambda i: (i, 0, 0))
    stat_shape = jax.ShapeDtypeStruct((nproj, 1, LANES), F32)
    gla_pairs, gla_chunks = GLA_HEADS // HEAD_PAIR, PROJ_ROWS // GLA_CHUNK
    pair_w = HEAD_PAIR * GLA_DV_PAD
    (gla, fqvg, kx, mqg, crow, cq, qn2, kn2) = pl.pallas_call(
        _proj_kernel,
        grid=(nproj,),
        in_specs=[rows(D_MODEL), whole((1, D_MODEL)),
                  pl.BlockSpec(w_in_t.shape, lambda i: (0, 0), pipeline_mode=pl.Buffered(1)),
                  whole(w_alpha_up.shape), whole((1, GLA_HEADS * GLA_DK)),
                  whole((1, FOX_HEADS)), whole((1, GLA_DV))],
        out_specs=[rows(GLA_OUT_W), rows(3 * FOX_W),
                   pl.BlockSpec((FOX_W + LANES, PROJ_ROWS), lambda i: (0, i)),
                   rows(2 * MEM_W),
                   pl.BlockSpec((SUBLANES, PROJ_ROWS), lambda i: (0, i)), rows(LANES),
                   stat_spec, pl.BlockSpec((1, 1, LANES), lambda i: (0, 0, 0))],
        out_shape=[bshape(GLA_OUT_W), bshape(3 * FOX_W),
                   jax.ShapeDtypeStruct((FOX_W + LANES, T), BF16),
                   bshape(2 * MEM_W),
                   jax.ShapeDtypeStruct((SUBLANES, T), F32), bshape(LANES),
                   stat_shape, jax.ShapeDtypeStruct((1, 1, LANES), F32)],
        scratch_shapes=[
            pltpu.VMEM((SUBLANES, LANES), F32),
            pltpu.VMEM((IN_COLS_PAD, D_MODEL), BF16),
            pltpu.VMEM((SMALL_W, GLA_QK_W), F32),
            pltpu.VMEM((1, GLA_QK_W), F32),
            pltpu.VMEM((1, SMALL_W), F32),
            pltpu.VMEM((1, GLA_DV_PAD), F32),
            pltpu.VMEM((FOX_W, LANES), BF16),
            pltpu.VMEM((PROJ_ROWS, GLA_QK_W), BF16),
            pltpu.VMEM((PROJ_ROWS, GLA_QK_W), BF16),
            pltpu.VMEM((PROJ_ROWS, GLA_V_W), BF16),
            pltpu.VMEM((PROJ_ROWS, GLA_V_W), BF16),
            pltpu.VMEM((PROJ_ROWS, GLA_QK_W), F32),
            pltpu.VMEM((gla_pairs, LANES, pair_w), F32),
            pltpu.VMEM((PROJ_ROWS, GLA_V_W), BF16),
            pltpu.VMEM((gla_pairs, gla_chunks, LANES, pair_w), F32),
            pltpu.VMEM((gla_pairs, gla_chunks, LANES, LANES), F32),
            pltpu.VMEM((gla_pairs, gla_chunks, LANES, pair_w), BF16)],
        compiler_params=_params("arbitrary"),
        name="proj",
    )(x, norm_g[None, :], w_in_t, w_alpha_up, b_alpha[None, :], b_forget[None, :],
      gla_norm_g[None, :])

    cend = crow[:FOX_HEADS, FOX_KEYS - 1::FOX_KEYS].reshape(-1)
    fox_pairs = FOX_HEADS // HEAD_PAIR
    pair_rows = pl.BlockSpec((FOX_BLOCK, LANES), lambda p, i, *_: (i, p))
    fox = pl.pallas_call(
        _fox_kernel,
        grid_spec=pltpu.PrefetchScalarGridSpec(
            num_scalar_prefetch=3,
            grid=(fox_pairs, T // FOX_BLOCK),
            in_specs=[pair_rows,
                      pl.BlockSpec((FOX_BLOCK, LANES), lambda p, i, *_: (i, 0)),
                      pl.BlockSpec((LANES, T), lambda p, i, *_: (p, 0)),
                      pl.BlockSpec((LANES, T), lambda p, i, *_: (fox_pairs, 0)),
                      pl.BlockSpec((T, LANES), lambda p, i, *_: (0, fox_pairs + p)),
                      pl.BlockSpec((FOX_BLOCK, LANES),
                                   lambda p, i, *_: (i, 2 * fox_pairs + p))],
            out_specs=pair_rows,
            scratch_shapes=[pltpu.VMEM((HEAD_PAIR, FOX_BLOCK, LANES), F32),
                            pltpu.VMEM((HEAD_PAIR, FOX_BLOCK, LANES), F32)]),
        out_shape=bshape(FOX_W),
        compiler_params=_params("arbitrary", "arbitrary"),
        name="fox",
    )(qn2.reshape(-1), kn2.reshape(-1), cend, fqvg, cq, kx, kx, fqvg, fqvg)

    out = pl.pallas_call(
        _out_kernel,
        grid=(T // OUT_ROWS,),
        in_specs=[rows(D_MODEL, OUT_ROWS), rows(GLA_OUT_W, OUT_ROWS), rows(FOX_W, OUT_ROWS),
                  pl.BlockSpec((OUT_ROWS, MEM_W), lambda i: (i, 0)),
                  pl.BlockSpec((OUT_ROWS, MEM_W), lambda i: (i, 1)),
                  once(mem.shape), once((1, D_MODEL)), once(w_mem_kv.shape), once(w_out.shape),
                  whole((1, D_MODEL))],
        out_specs=rows(D_MODEL, OUT_ROWS),
        out_shape=jax.ShapeDtypeStruct((T, D_MODEL), F32),
        scratch_shapes=[pltpu.VMEM((D_MODEL, D_MODEL), BF16),
                        pltpu.VMEM((M, MEM_W), BF16), pltpu.VMEM((M, MEM_W), BF16)],
        compiler_params=_params("arbitrary"),
        name="out",
    )(x, gla, fox, mqg, mqg, mem, mem_norm_g[None, :], w_mem_kv, w_out, out_g[None, :])
    return out


def kernel(x, mem, norm_g, w_in, w_alpha_up, b_alpha, b_forget, gla_norm_g, mem_norm_g,
           w_mem_kv, w_out, final_norm_g):
    assert x.shape[0] == 1 and mem.shape[0] == 1 and norm_g.shape[0] == 1
    assert x.shape[1] % max(PROJ_ROWS, FOX_BLOCK, OUT_ROWS) == 0
    out = _layer(x[0], mem[0], norm_g[0], w_in[0], w_alpha_up[0], b_alpha[0], b_forget[0],
                 gla_norm_g[0], mem_norm_g[0], w_mem_kv[0], w_out[0], final_norm_g)
    return out[None]
```

```python
import jax
import jax.numpy as jnp
from jax import lax
from jax.experimental import pallas as pl
from jax.experimental.pallas import tpu as pltpu

F32 = jnp.float32
BF16 = jnp.bfloat16

EPS = 1e-6
LANES = 128
SUBLANES = 8

D_MODEL = 1024
GLA_HEADS, GLA_DK, GLA_DV, GLA_RANK = 4, 48, 96, 16
GLA_DK_PAD = 64
GLA_DV_PAD = LANES
GLA_GATE_NORM = 16.0
GLA_CHUNK = 64
FOX_HEADS, FOX_DH = 6, 64
MEM_HEADS, MEM_DH = 4, 64
HEAD_PAIR = 2
GLA_QK_W = GLA_HEADS * GLA_DK_PAD
GLA_V_W = GLA_HEADS * GLA_DV_PAD
GLA_OUT_W = GLA_HEADS * GLA_DV
FOX_W = FOX_HEADS * FOX_DH
MEM_W = MEM_HEADS * MEM_DH
SMALL_W = LANES
FG_LANE0 = 0
LR_LANE0 = SUBLANES

_GROUPS = (("gq", GLA_QK_W), ("gk", GLA_QK_W), ("gv", GLA_OUT_W), ("gg", GLA_OUT_W),
           ("fq", FOX_W), ("fk", FOX_W), ("fv", FOX_W), ("fgate", FOX_W),
           ("mq", MEM_W), ("mg", MEM_W), ("small", SMALL_W))
_OFF = {}
_o = 0
for _n, _w in _GROUPS:
    _OFF[_n] = (_o, _o + _w)
    _o += _w
IN_COLS_PAD = _o

PROJ_ROWS = 1024
FOX_BLOCK = 4096
FOX_KEYS = 256
OUT_ROWS = 1024
VMEM_LIMIT = 56 * 1024 * 1024

NEG_BIG = -1e30
FOX_SKIP_NATS = 105.0
NORM_SLACK = 1.02
FOX_DIRECT_NORM2 = 3600.0
CX_HI, CX_MID, CX_LO, CX_ONE = 0, 8, 16, 24


def _log_sigmoid(z):
    return jnp.minimum(z, 0.0) - jnp.log(1.0 + jnp.exp(-jnp.abs(z)))


def _silu(z):
    return z / (1.0 + jnp.exp(-z))


def _rms_scale(v, width):
    return lax.rsqrt(jnp.sum(v * v, axis=-1, keepdims=True) * (1.0 / width) + EPS)


def _w_in_segments():
    qk, gw = GLA_HEADS * GLA_DK, GLA_HEADS * GLA_DV
    src = {}
    o = 0
    for name, width in (("gq", qk), ("gk", qk), ("gv", gw), ("lr", GLA_RANK), ("gg", gw),
                        ("fq", FOX_W), ("fk", FOX_W), ("fv", FOX_W), ("fg", FOX_HEADS),
                        ("fgate", FOX_W), ("mq", MEM_W), ("mg", MEM_W)):
        src[name] = o
        o += width
    segs = []
    for name, d, d_pad in (("gq", GLA_DK, GLA_DK_PAD), ("gk", GLA_DK, GLA_DK_PAD)):
        segs += [(src[name] + h * d, _OFF[name][0] + h * d_pad, d) for h in range(GLA_HEADS)]
    segs += [(src[name], _OFF[name][0], _OFF[name][1] - _OFF[name][0])
             for name in ("gv", "gg", "fq", "fk", "fv", "fgate", "mq", "mg")]
    segs += [(src["fg"], _OFF["small"][0] + FG_LANE0, FOX_HEADS),
             (src["lr"], _OFF["small"][0] + LR_LANE0, GLA_RANK)]
    return tuple(segs)


def _proj_kernel(x_ref, g_ref, w_in_ref, w_alpha_ref, b_alpha_ref, b_forget_ref, gla_g_ref,
                 gla_ref, fqvg_ref, kx_ref, mqg_ref, crow_ref, cq_ref, qn2_ref, kn2_ref,
                 carry_ref, wt_ref, wa_ref, ba_ref, bf_ref, ng_ref, seg_ref,
                 gq_ref, gk_ref, gv_ref, gg_ref, loga_ref,
                 s_ref, lhs_ref, kv_ref, dec_ref, sprev_ref):
    rows = x_ref.shape[0]
    k_chunks = D_MODEL // LANES

    @pl.when(pl.program_id(0) == 0)
    def _():
        carry_ref[...] = jnp.zeros_like(carry_ref)
        s_ref[...] = jnp.zeros_like(s_ref)
        kn2_ref[...] = jnp.zeros_like(kn2_ref)
        gv_ref[...] = jnp.zeros_like(gv_ref)
        gg_ref[...] = jnp.zeros_like(gg_ref)
        wa_ref[...] = jnp.zeros_like(wa_ref)
        ba_ref[...] = jnp.zeros_like(ba_ref)
        bf_ref[...] = jnp.zeros_like(bf_ref)
        ng_ref[...] = jnp.zeros_like(ng_ref)
        for h in range(GLA_HEADS):
            src = slice(h * GLA_DK, (h + 1) * GLA_DK)
            dst = slice(h * GLA_DK_PAD, h * GLA_DK_PAD + GLA_DK)
            wa_ref[LR_LANE0:LR_LANE0 + GLA_RANK, dst] = w_alpha_ref[:, src]
            ba_ref[:, dst] = b_alpha_ref[:, src]
        bf_ref[:, FG_LANE0:FG_LANE0 + FOX_HEADS] = b_forget_ref[...]
        ng_ref[:, 0:GLA_DV] = gla_g_ref[...]
        seg_ref[...] = (lax.broadcasted_iota(jnp.int32, seg_ref.shape, 0) // FOX_DH
                        == lax.broadcasted_iota(jnp.int32, seg_ref.shape, 1)).astype(BF16)
        wt_ref[...] = jnp.zeros_like(wt_ref)
        for s0, d0, width in _w_in_segments():
            for c in range(k_chunks):
                wt_ref[d0:d0 + width, c * LANES:(c + 1) * LANES] = (
                    w_in_ref[pl.ds(s0 * k_chunks + c, width, stride=k_chunks), :].astype(BF16))

    x = x_ref[...]
    xn = (x * _rms_scale(x, D_MODEL) * g_ref[...]).astype(BF16)
    nt = (((1,), (1,)), ((), ()))

    def proj(first, last):
        lo, hi = _OFF[first][0], _OFF[last][1]
        y = lax.dot_general(xn, wt_ref[lo:hi, :], nt, preferred_element_type=F32)
        return lambda name: y[:, _OFF[name][0] - lo:_OFF[name][1] - lo]

    tail = proj("mq", "small")
    gla = proj("gq", "gg")
    small = tail("small")
    logf = _log_sigmoid(small + bf_ref[...])
    c = logf.T[0:SUBLANES, :]
    lane = lax.broadcasted_iota(jnp.int32, c.shape, 1)
    shift = 1
    while shift < rows:
        c = c + jnp.where(lane >= shift, pltpu.roll(c, shift, axis=1), 0.0)
        shift *= 2
    c = c + carry_ref[:, 0:1]
    crow_ref[...] = c
    carry_ref[...] = jnp.broadcast_to(c[:, rows - 1:rows], carry_ref.shape)
    neg = -c
    hi = neg.astype(BF16).astype(F32)
    mid = (neg - hi).astype(BF16).astype(F32)
    low = neg - hi - mid
    parts = jnp.concatenate(
        [hi, mid, low, jnp.ones_like(c), jnp.zeros((LANES - 4 * SUBLANES, rows), F32)], axis=0)
    kx_ref[FOX_W:, :] = parts.astype(BF16)
    cq_ref[...] = parts.T.astype(BF16)

    z = (jnp.dot(small.astype(BF16), wa_ref[...].astype(BF16), preferred_element_type=F32)
         + ba_ref[...])
    loga_ref[...] = _log_sigmoid(z) * (1.0 / GLA_GATE_NORM)
    gq_ref[...] = gla("gq").astype(BF16)
    gk_ref[...] = gla("gk").astype(BF16)
    gv, gg = gla("gv").astype(BF16), _silu(gla("gg")).astype(BF16)
    for h in range(GLA_HEADS):
        src = slice(h * GLA_DV, (h + 1) * GLA_DV)
        dst = slice(h * GLA_DV_PAD, h * GLA_DV_PAD + GLA_DV)
        gv_ref[:, dst] = gv[:, src]
        gg_ref[:, dst] = gg[:, src]
    gla_local, gla_scan, gla_output = _gla_block(
        gq_ref, gk_ref, gv_ref, loga_ref, gg_ref, ng_ref, gla_ref,
        s_ref, lhs_ref, kv_ref, dec_ref, sprev_ref)

    def max_sq_norm(v):
        v32 = v.astype(F32)
        n2 = jnp.dot((v32 * v32).astype(BF16), seg_ref[...], preferred_element_type=F32)
        return jnp.max(n2, axis=0, keepdims=True)

    gla_local()
    fox_qk = proj("fq", "fk")
    mqg_ref[:, :MEM_W] = (tail("mq") * MEM_DH ** -0.5).astype(BF16)
    mqg_ref[:, MEM_W:] = _silu(tail("mg")).astype(BF16)
    gla_scan()
    fq = (fox_qk("fq") * FOX_DH ** -0.5).astype(BF16)
    fk = fox_qk("fk").astype(BF16)
    fqvg_ref[:, :FOX_W] = fq
    kx_ref[:FOX_W, :] = fox_qk("fk").T.astype(BF16)
    gla_output()
    fox_vg = proj("fv", "fgate")
    fqvg_ref[:, FOX_W:2 * FOX_W] = fox_vg("fv").astype(BF16)
    fqvg_ref[:, 2 * FOX_W:] = _silu(fox_vg("fgate")).astype(BF16)
    qn2_ref[0] = max_sq_norm(fq)
    kn2_ref[0] = jnp.maximum(kn2_ref[0], max_sq_norm(fk))


def _memkv(mem_ref, g_ref, w_ref, mk_ref, mv_ref):
    m = mem_ref[...]
    mn = (m * _rms_scale(m, D_MODEL) * g_ref[...]).astype(BF16)
    kv = jnp.dot(mn, w_ref[...].astype(BF16), preferred_element_type=F32)
    mk_ref[...] = kv[:, :MEM_W].astype(BF16)
    mv_ref[...] = kv[:, MEM_W:].astype(BF16)


def _gla_block(q_ref, k_ref, v_ref, loga_ref, gate_ref, ng_ref, o_ref,
               s_ref, lhs_ref, kv_ref, dec_ref, sprev_ref):
    C = GLA_CHUNK
    W = HEAD_PAIR * GLA_DV_PAD
    n_chunks = q_ref.shape[0] // C

    row = lax.broadcasted_iota(jnp.int32, (C, LANES), 0)
    lane = lax.broadcasted_iota(jnp.int32, (C, LANES), 1)
    lo_k = lane < GLA_DK_PAD
    causal = row >= jnp.where(lo_k, lane, lane - GLA_DK_PAD)
    lo_v = lax.broadcasted_iota(jnp.int32, (C, W), 1) < GLA_DV_PAD
    st_row = lax.broadcasted_iota(jnp.int32, (LANES, W), 0)
    st_lane = lax.broadcasted_iota(jnp.int32, (LANES, W), 1)
    own = (st_row < GLA_DK_PAD) == (st_lane < GLA_DV_PAD)
    eye = (lax.broadcasted_iota(jnp.int32, (LANES, LANES), 0)
           == lax.broadcasted_iota(jnp.int32, (LANES, LANES), 1))
    scale = GLA_DK ** -0.5
    nt = (((1,), (1,)), ((), ()))
    tn = (((0,), (0,)), ((), ()))
    ng = jnp.concatenate([ng_ref[...]] * HEAD_PAIR, axis=1)

    pairs = range(GLA_HEADS // HEAD_PAIR)

    def local(ci):
        rs = slice(ci * C, (ci + 1) * C)
        for p in pairs:
            ls = slice(p * LANES, (p + 1) * LANES)
            vs = slice(p * W, (p + 1) * W)
            b = loga_ref[rs, ls]
            shift = 1
            while shift < C:
                b = b + jnp.where(row >= shift, pltpu.roll(b, shift, axis=0), 0.0)
                shift *= 2
            b_last = b[C - 1:C, :]
            k2 = k_ref[rs, ls].astype(F32)
            qd = (q_ref[rs, ls].astype(F32) * scale * jnp.exp(b)).astype(BF16)
            kd = (k2 * jnp.exp(-b)).astype(BF16)
            ke = (k2 * jnp.exp(b_last - b)).astype(BF16)
            zk = jnp.zeros_like(kd)
            kd_blk = jnp.concatenate([jnp.where(lo_k, kd, zk), jnp.where(lo_k, zk, kd)], axis=0)
            attn = lax.dot_general(qd, kd_blk, nt, preferred_element_type=F32)
            lhs_ref[rs, vs] = jnp.concatenate([jnp.where(causal, attn, 0.0).astype(BF16), qd], axis=1)
            kv = lax.dot_general(ke, v_ref[rs, vs], tn, preferred_element_type=F32)
            kv_ref[p, ci] = jnp.where(own, kv, 0.0)
            dcol = jnp.exp(jnp.sum(jnp.where(eye, jnp.broadcast_to(b_last, (LANES, LANES)), 0.0),
                                   axis=1, keepdims=True))
            dec_ref[p, ci] = jnp.broadcast_to(dcol, (LANES, LANES))

    def scan(ci):
        for p in pairs:
            s_prev = s_ref[p]
            sprev_ref[p, ci] = s_prev.astype(BF16)
            s_ref[p] = jnp.tile(dec_ref[p, ci], (1, HEAD_PAIR)) * s_prev + kv_ref[p, ci]

    def output(ci):
        rs = slice(ci * C, (ci + 1) * C)
        for p in pairs:
            vs = slice(p * W, (p + 1) * W)
            v2 = v_ref[rs, vs]
            zv = jnp.zeros_like(v2)
            v_blk = jnp.concatenate([jnp.where(lo_v, v2, zv), jnp.where(lo_v, zv, v2)], axis=0)
            o = jnp.dot(lhs_ref[rs, vs], jnp.concatenate([v_blk, sprev_ref[p, ci]], axis=0),
                        preferred_element_type=F32)
            o2 = o * o
            ms = jnp.where(lo_v, jnp.sum(o2[:, :GLA_DV_PAD], axis=1, keepdims=True),
                           jnp.sum(o2[:, GLA_DV_PAD:], axis=1, keepdims=True))
            on = o * lax.rsqrt(ms * (1.0 / GLA_DV) + EPS) * ng
            og = (on * gate_ref[rs, vs].astype(F32)).astype(BF16)
            for hh in range(HEAD_PAIR):
                c0 = (p * HEAD_PAIR + hh) * GLA_DV
                o_ref[rs, c0:c0 + GLA_DV] = og[:, hh * GLA_DV_PAD:hh * GLA_DV_PAD + GLA_DV]

    def all_chunks(phase):
        return lambda: [phase(ci) for ci in range(n_chunks)]

    return all_chunks(local), all_chunks(scan), all_chunks(output)


def _fox_kernel(qn2_ref, kn2_ref, cend_ref, q_ref, cq_ref, k_ref, cx_ref, v_ref, gate_ref, o_ref,
                m_ref, acc_ref):
    blk = FOX_KEYS
    streams = range(q_ref.shape[0] // blk)
    pair = pl.program_id(0)
    qi = pl.program_id(1)
    nblk = pl.num_programs(1) * len(streams)
    lane = lax.broadcasted_iota(jnp.int32, (1, LANES), 1)
    lo_lanes = lane < FOX_DH
    reps = blk // LANES
    diag = [qi * len(streams) + s for s in streams]
    heads = [pair * HEAD_PAIR + hh for hh in range(HEAD_PAIR)]

    q = q_ref[...]
    zero = jnp.zeros_like(q)
    q_lo, q_hi = jnp.where(lo_lanes, q, zero), jnp.where(lo_lanes, zero, q)
    q_stack = [jnp.concatenate([q_lo[s * blk:(s + 1) * blk], q_hi[s * blk:(s + 1) * blk]], axis=0)
               for s in streams]
    g_row = lax.broadcasted_iota(jnp.int32, (LANES, LANES), 0)
    g_col = lax.broadcasted_iota(jnp.int32, (LANES, LANES), 1)
    xlane = lax.broadcasted_iota(jnp.int32, (HEAD_PAIR * blk, LANES), 1)
    cq = cq_ref[...]

    def lane_map(h, shift):
        g = jnp.where((g_row == CX_ONE) & ((g_col == CX_HI + h) | (g_col == CX_MID + h)
                                           | (g_col == CX_LO + h)), 1.0, 0.0)
        if shift:
            for part, base in enumerate((CX_HI, CX_MID, CX_LO)):
                g = jnp.where((g_row == base + h) & (g_col == CX_ONE + part), -1.0, g)
        return g.astype(BF16)

    def q_aug(s, shift, void=None):
        cs = cq[s * blk:(s + 1) * blk]
        extra = jnp.concatenate([jnp.dot(cs, lane_map(h, shift), preferred_element_type=F32)
                                 for h in heads], axis=0)
        if void is not None:
            extra = jnp.where(jnp.logical_and(xlane == CX_ONE + 3, void), NEG_BIG, extra)
        return jnp.concatenate([q_stack[s], extra.astype(BF16)], axis=1)

    acc_ref[...] = jnp.zeros_like(acc_ref)
    qpos = lax.broadcasted_iota(jnp.int32, (blk, blk), 0)
    kpos = lax.broadcasted_iota(jnp.int32, (blk, blk), 1)

    def block(j):
        ks = pl.ds(pl.multiple_of(jnp.maximum(j, 0) * blk, blk), blk)
        k_aug = jnp.concatenate([k_ref[:, ks], cx_ref[:, ks]], axis=0)
        vb = v_ref[ks, :]
        one = jnp.ones_like(vb)
        return k_aug, (jnp.where(lo_lanes, vb, one), jnp.where(lo_lanes, one, vb))

    def step_online(s, j, qa, masked):
        rows = slice(s * blk, (s + 1) * blk)
        k_aug, vaug = block(j)
        s_all = jnp.dot(qa, k_aug, preferred_element_type=F32)
        for hh in range(HEAD_PAIR):
            sc = s_all[hh * blk:(hh + 1) * blk]
            if masked:
                sc = jnp.where(kpos <= qpos, sc, NEG_BIG)
            m_prev = m_ref[hh, rows]
            m_new = jnp.maximum(m_prev, jnp.max(sc, axis=1, keepdims=True))
            p = jnp.exp(sc - jnp.tile(m_new, (1, reps)))
            alpha = jnp.exp(m_prev - m_new)
            pv = jnp.dot(p.astype(BF16), vaug[hh], preferred_element_type=F32)
            acc_ref[hh, rows] = alpha * acc_ref[hh, rows] + pv
            m_ref[hh, rows] = m_new

    def step_direct(s, j, qa, masked, only=None):
        rows = slice(s * blk, (s + 1) * blk)
        k_aug, vaug = block(j)
        hsel = range(HEAD_PAIR) if only is None else (only,)
        lhs = qa if only is None else qa[only * blk:(only + 1) * blk]
        s_all = jnp.dot(lhs, k_aug, preferred_element_type=F32)
        for n, hh in enumerate(hsel):
            sc = s_all[n * blk:(n + 1) * blk]
            if masked:
                sc = jnp.where(kpos <= qpos, sc, NEG_BIG)
            acc_ref[hh, rows] += jnp.dot(jnp.exp(sc).astype(BF16), vaug[hh],
                                         preferred_element_type=F32)

    k_max2 = [kn2_ref[h] for h in heads]

    def norm2(s, hh):
        stat = (diag[s] * blk // PROJ_ROWS) * LANES
        return (NORM_SLACK * NORM_SLACK) * qn2_ref[stat + heads[hh]] * k_max2[hh]

    n2 = [[norm2(s, hh) for hh in range(HEAD_PAIR)] for s in streams]
    gap0 = [[FOX_SKIP_NATS + cend_ref[heads[hh] * nblk + jnp.maximum(diag[s] - 1, 0)]
             for hh in range(HEAD_PAIR)] for s in streams]

    def live_head(t, hh, direct):
        keep = False
        for s in streams:
            j = diag[s] - 1 - t
            gap = gap0[s][hh] - cend_ref[heads[hh] * nblk + jnp.maximum(j, 0)]
            bound2 = n2[s][hh] if direct else 4.0 * n2[s][hh]
            dead = jnp.logical_and(gap <= 0.0, bound2 <= gap * gap)
            keep = jnp.logical_or(keep, jnp.logical_and(j >= 0, jnp.logical_not(dead)))
        return keep

    def sweeps(stepper, direct):
        def sweep(back, masked=False, only=None):
            for s in streams:
                stepper(s, diag[s] - back, masked, only)

        def loop(t0, cond, only=None):
            def body(t):
                sweep(t + 1, only=only)
                return t + 1
            return lax.while_loop(cond, body, t0)

        def live(t, hh):
            return live_head(t, hh, direct)

        sweep(0, masked=True)
        sweep(1)
        if direct:
            t_both = loop(1, lambda t: jnp.logical_and(live(t, 0), live(t, 1)))
            for hh in range(HEAD_PAIR):
                loop(t_both, lambda t, hh=hh: live(t, hh), only=hh)
        else:
            loop(1, lambda t: jnp.logical_or(live(t, 0), live(t, 1)))

    direct_ok = True
    for s in streams:
        for hh in range(HEAD_PAIR):
            direct_ok = jnp.logical_and(direct_ok, n2[s][hh] <= FOX_DIRECT_NORM2)

    @pl.when(direct_ok)
    def _():
        q_dir = [q_aug(s, shift=True) for s in streams]
        q_void = [q_aug(s, shift=True, void=True) for s in streams]
        sweeps(lambda s, j, masked, only: step_direct(
            s, j, q_dir[s] if masked else jnp.where(j >= 0, q_dir[s], q_void[s]), masked, only),
            direct=True)

    @pl.when(jnp.logical_not(direct_ok))
    def _():
        m_ref[...] = jnp.full_like(m_ref, NEG_BIG)
        sweeps(lambda s, j, masked, only: step_online(
            s, j, q_aug(s, shift=False, void=j < 0), masked), direct=False)

    outs = []
    for hh in range(HEAD_PAIR):
        acc = acc_ref[hh]
        outs.append(acc / pltpu.roll(acc, FOX_DH, axis=1))
    o = jnp.where(lo_lanes, outs[0], outs[1])
    o_ref[...] = (o * gate_ref[...].astype(F32)).astype(BF16)


def _out_kernel(x_ref, gla_ref, fox_ref, mq_ref, mg_ref, mem_ref, mem_g_ref, w_mem_ref,
                w_out_ref, fg_ref, o_ref, wo_ref, mk_ref, mv_ref):
    @pl.when(pl.program_id(0) == 0)
    def _():
        wo_ref[...] = w_out_ref[...].astype(BF16)
        _memkv(mem_ref, mem_g_ref, w_mem_ref, mk_ref, mv_ref)

    lane = lax.broadcasted_iota(jnp.int32, (1, LANES), 1)
    lo_lanes = lane < MEM_DH
    nt = (((1,), (1,)), ((), ()))
    mem_parts = []
    for p in range(MEM_HEADS // HEAD_PAIR):
        ls = slice(p * LANES, (p + 1) * LANES)
        q = mq_ref[:, ls]
        kb = mk_ref[:, ls]
        vb = mv_ref[:, ls]
        zero = jnp.zeros_like(q)
        one = jnp.ones_like(vb)
        qh = (jnp.where(lo_lanes, q, zero), jnp.where(lo_lanes, zero, q))
        vaug = (jnp.where(lo_lanes, vb, one), jnp.where(lo_lanes, one, vb))
        outs = []
        for hh in range(HEAD_PAIR):
            s = lax.dot_general(qh[hh], kb, nt, preferred_element_type=F32)
            pexp = jnp.exp(s - jnp.max(s, axis=1, keepdims=True))
            pv = jnp.dot(pexp.astype(BF16), vaug[hh], preferred_element_type=F32)
            outs.append(pv / pltpu.roll(pv, MEM_DH, axis=1))
        o = jnp.where(lo_lanes, outs[0], outs[1])
        mem_parts.append((o * mg_ref[:, ls].astype(F32)).astype(BF16))
    mixed = jnp.concatenate([gla_ref[...], fox_ref[...]] + mem_parts, axis=1)
    y = x_ref[...] + jnp.dot(mixed, wo_ref[...], preferred_element_type=F32)
    o_ref[...] = y * _rms_scale(y, D_MODEL) * fg_ref[...]


def _params(*sem):
    return pltpu.CompilerParams(dimension_semantics=sem, vmem_limit_bytes=VMEM_LIMIT)


def _layer(x, mem, norm_g, w_in, w_alpha_up, b_alpha, b_forget, gla_norm_g,
           mem_norm_g, w_mem_kv, w_out, out_g):
    w_in_t = jnp.transpose(w_in[None], (0, 2, 1)).reshape(-1, LANES)
    T = x.shape[0]
    M = mem.shape[0]

    def rows(width, n=PROJ_ROWS):
        return pl.BlockSpec((n, width), lambda i: (i, 0))

    def whole(shape):
        return pl.BlockSpec(shape, lambda i: (0,) * len(shape))

    def once(shape):
        return pl.BlockSpec(shape, lambda i: (0,) * len(shape), pipeline_mode=pl.Buffered(1))

    bshape = lambda w: jax.ShapeDtypeStruct((T, w), BF16)
    nproj = T // PROJ_ROWS
    stat_spec = pl.BlockSpec((1, 1, LANES), lambda i: (i, 0, 0))
    stat_shape = jax.ShapeDtypeStruct((nproj, 1, LANES), F32)
    gla_pairs, gla_chunks = GLA_HEADS // HEAD_PAIR, PROJ_ROWS // GLA_CHUNK
    pair_w = HEAD_PAIR * GLA_DV_PAD
    (gla, fqvg, kx, mqg, crow, cq, qn2, kn2) = pl.pallas_call(
        _proj_kernel,
        grid=(nproj,),
        in_specs=[rows(D_MODEL), whole((1, D_MODEL)),
                  pl.BlockSpec(w_in_t.shape, lambda i: (0, 0), pipeline_mode=pl.Buffered(1)),
                  whole(w_alpha_up.shape), whole((1, GLA_HEADS * GLA_DK)),
                  whole((1, FOX_HEADS)), whole((1, GLA_DV))],
        out_specs=[rows(GLA_OUT_W), rows(3 * FOX_W),
                   pl.BlockSpec((FOX_W + LANES, PROJ_ROWS), lambda i: (0, i)),
                   rows(2 * MEM_W),
                   pl.BlockSpec((SUBLANES, PROJ_ROWS), lambda i: (0, i)), rows(LANES),
                   stat_spec, pl.BlockSpec((1, 1, LANES), lambda i: (0, 0, 0))],
        out_shape=[bshape(GLA_OUT_W), bshape(3 * FOX_W),
                   jax.ShapeDtypeStruct((FOX_W + LANES, T), BF16),
                   bshape(2 * MEM_W),
                   jax.ShapeDtypeStruct((SUBLANES, T), F32), bshape(LANES),
                   stat_shape, jax.ShapeDtypeStruct((1, 1, LANES), F32)],
        scratch_shapes=[
            pltpu.VMEM((SUBLANES, LANES), F32),
            pltpu.VMEM((IN_COLS_PAD, D_MODEL), BF16),
            pltpu.VMEM((SMALL_W, GLA_QK_W), F32),
            pltpu.VMEM((1, GLA_QK_W), F32),
            pltpu.VMEM((1, SMALL_W), F32),
            pltpu.VMEM((1, GLA_DV_PAD), F32),
            pltpu.VMEM((FOX_W, LANES), BF16),
            pltpu.VMEM((PROJ_ROWS, GLA_QK_W), BF16),
            pltpu.VMEM((PROJ_ROWS, GLA_QK_W), BF16),
            pltpu.VMEM((PROJ_ROWS, GLA_V_W), BF16),
            pltpu.VMEM((PROJ_ROWS, GLA_V_W), BF16),
            pltpu.VMEM((PROJ_ROWS, GLA_QK_W), F32),
            pltpu.VMEM((gla_pairs, LANES, pair_w), F32),
            pltpu.VMEM((PROJ_ROWS, GLA_V_W), BF16),
            pltpu.VMEM((gla_pairs, gla_chunks, LANES, pair_w), F32),
            pltpu.VMEM((gla_pairs, gla_chunks, LANES, LANES), F32),
            pltpu.VMEM((gla_pairs, gla_chunks, LANES, pair_w), BF16)],
        compiler_params=_params("arbitrary"),
        name="proj",
    )(x, norm_g[None, :], w_in_t, w_alpha_up, b_alpha[None, :], b_forget[None, :],
      gla_norm_g[None, :])

    cend = crow[:FOX_HEADS, FOX_KEYS - 1::FOX_KEYS].reshape(-1)
    fox_pairs = FOX_HEADS // HEAD_PAIR
    pair_rows = pl.BlockSpec((FOX_BLOCK, LANES), lambda p, i, *_: (i, p))
    fox = pl.pallas_call(
        _fox_kernel,
        grid_spec=pltpu.PrefetchScalarGridSpec(
            num_scalar_prefetch=3,
            grid=(fox_pairs, T // FOX_BLOCK),
            in_specs=[pair_rows,
                      pl.BlockSpec((FOX_BLOCK, LANES), lambda p, i, *_: (i, 0)),
                      pl.BlockSpec((LANES, T), lambda p, i, *_: (p, 0)),
                      pl.BlockSpec((LANES, T), lambda p, i, *_: (fox_pairs, 0)),
                      pl.BlockSpec((T, LANES), lambda p, i, *_: (0, fox_pairs + p)),
                      pl.BlockSpec((FOX_BLOCK, LANES),
                                   lambda p, i, *_: (i, 2 * fox_pairs + p))],
            out_specs=pair_rows,
            scratch_shapes=[pltpu.VMEM((HEAD_PAIR, FOX_BLOCK, LANES), F32),
                            pltpu.VMEM((HEAD_PAIR, FOX_BLOCK, LANES), F32)]),
        out_shape=bshape(FOX_W),
        compiler_params=_params("arbitrary", "arbitrary"),
        name="fox",
    )(qn2.reshape(-1), kn2.reshape(-1), cend, fqvg, cq, kx, kx, fqvg, fqvg)

    out = pl.pallas_call(
        _out_kernel,
        grid=(T // OUT_ROWS,),
        in_specs=[rows(D_MODEL, OUT_ROWS), rows(GLA_OUT_W, OUT_ROWS), rows(FOX_W, OUT_ROWS),
                  pl.BlockSpec((OUT_ROWS, MEM_W), lambda i: (i, 0)),
                  pl.BlockSpec((OUT_ROWS, MEM_W), lambda i: (i, 1)),
                  once(mem.shape), once((1, D_MODEL)), once(w_mem_kv.shape), once(w_out.shape),
                  whole((1, D_MODEL))],
        out_specs=rows(D_MODEL, OUT_ROWS),
        out_shape=jax.ShapeDtypeStruct((T, D_MODEL), F32),
        scratch_shapes=[pltpu.VMEM((D_MODEL, D_MODEL), BF16),
                        pltpu.VMEM((M, MEM_W), BF16), pltpu.VMEM((M, MEM_W), BF16)],
        compiler_params=_params("arbitrary"),
        name="out",
    )(x, gla, fox, mqg, mqg, mem, mem_norm_g[None, :], w_mem_kv, w_out, out_g[None, :])
    return out


def kernel(x, mem, norm_g, w_in, w_alpha_up, b_alpha, b_forget, gla_norm_g, mem_norm_g,
           w_mem_kv, w_out, final_norm_g):
    assert x.shape[0] == 1 and mem.shape[0] == 1 and norm_g.shape[0] == 1
    assert x.shape[1] % max(PROJ_ROWS, FOX_BLOCK, OUT_ROWS) == 0
    out = _layer(x[0], mem[0], norm_g[0], w_in[0], w_alpha_up[0], b_alpha[0], b_forget[0],
                 gla_norm_g[0], mem_norm_g[0], w_mem_kv[0], w_out[0], final_norm_g)
    return out[None]
```

```python
import jax
import jax.numpy as jnp
from jax import lax
from jax.experimental import pallas as pl
from jax.experimental.pallas import tpu as pltpu

F32 = jnp.float32
BF16 = jnp.bfloat16

EPS = 1e-6
LANES = 128
SUBLANES = 8

D_MODEL = 1024
GLA_HEADS, GLA_DK, GLA_DV, GLA_RANK = 4, 48, 96, 16
GLA_DK_PAD = 64
GLA_DV_PAD = LANES
GLA_GATE_NORM = 16.0
GLA_CHUNK = 64
FOX_HEADS, FOX_DH = 6, 64
MEM_HEADS, MEM_DH = 4, 64
HEAD_PAIR = 2
GLA_QK_W = GLA_HEADS * GLA_DK_PAD
GLA_V_W = GLA_HEADS * GLA_DV_PAD
GLA_OUT_W = GLA_HEADS * GLA_DV
FOX_W = FOX_HEADS * FOX_DH
MEM_W = MEM_HEADS * MEM_DH
SMALL_W = LANES
FG_LANE0 = 0
LR_LANE0 = SUBLANES

_GROUPS = (("gq", GLA_QK_W), ("gk", GLA_QK_W), ("gv", GLA_OUT_W), ("gg", GLA_OUT_W),
           ("fq", FOX_W), ("fk", FOX_W), ("fv", FOX_W), ("fgate", FOX_W),
           ("mq", MEM_W), ("mg", MEM_W), ("small", SMALL_W))
_OFF = {}
_o = 0
for _n, _w in _GROUPS:
    _OFF[_n] = (_o, _o + _w)
    _o += _w
IN_COLS_PAD = _o

PROJ_ROWS = 1024
FOX_BLOCK = 4096
FOX_KEYS = 256
OUT_ROWS = 1024
VMEM_LIMIT = 56 * 1024 * 1024

NEG_BIG = -1e30
FOX_SKIP_NATS = 105.0
NORM_SLACK = 1.02
FOX_DIRECT_NORM2 = 3600.0
CX_HI, CX_MID, CX_LO, CX_ONE = 0, 8, 16, 24


def _log_sigmoid(z):
    return jnp.minimum(z, 0.0) - jnp.log(1.0 + jnp.exp(-jnp.abs(z)))


def _silu(z):
    return z / (1.0 + jnp.exp(-z))


def _rms_scale(v, width):
    return lax.rsqrt(jnp.sum(v * v, axis=-1, keepdims=True) * (1.0 / width) + EPS)


def _w_in_segments():
    qk, gw = GLA_HEADS * GLA_DK, GLA_HEADS * GLA_DV
    src = {}
    o = 0
    for name, width in (("gq", qk), ("gk", qk), ("gv", gw), ("lr", GLA_RANK), ("gg", gw),
                        ("fq", FOX_W), ("fk", FOX_W), ("fv", FOX_W), ("fg", FOX_HEADS),
                        ("fgate", FOX_W), ("mq", MEM_W), ("mg", MEM_W)):
        src[name] = o
        o += width
    segs = []
    for name, d, d_pad in (("gq", GLA_DK, GLA_DK_PAD), ("gk", GLA_DK, GLA_DK_PAD)):
        segs += [(src[name] + h * d, _OFF[name][0] + h * d_pad, d) for h in range(GLA_HEADS)]
    segs += [(src[name], _OFF[name][0], _OFF[name][1] - _OFF[name][0])
             for name in ("gv", "gg", "fq", "fk", "fv", "fgate", "mq", "mg")]
    segs += [(src["fg"], _OFF["small"][0] + FG_LANE0, FOX_HEADS),
             (src["lr"], _OFF["small"][0] + LR_LANE0, GLA_RANK)]
    return tuple(segs)


def _proj_kernel(x_ref, g_ref, w_in_ref, w_alpha_ref, b_alpha_ref, b_forget_ref, gla_g_ref,
                 gla_ref, fqvg_ref, kx_ref, mqg_ref, crow_ref, cq_ref, qn2_ref, kn2_ref,
                 carry_ref, wt_ref, wa_ref, ba_ref, bf_ref, ng_ref, seg_ref,
                 gq_ref, gk_ref, gv_ref, gg_ref, loga_ref,
                 s_ref, lhs_ref, kv_ref, dec_ref, sprev_ref):
    rows = x_ref.shape[0]
    k_chunks = D_MODEL // LANES

    @pl.when(pl.program_id(0) == 0)
    def _():
        carry_ref[...] = jnp.zeros_like(carry_ref)
        s_ref[...] = jnp.zeros_like(s_ref)
        kn2_ref[...] = jnp.zeros_like(kn2_ref)
        gv_ref[...] = jnp.zeros_like(gv_ref)
        gg_ref[...] = jnp.zeros_like(gg_ref)
        wa_ref[...] = jnp.zeros_like(wa_ref)
        ba_ref[...] = jnp.zeros_like(ba_ref)
        bf_ref[...] = jnp.zeros_like(bf_ref)
        ng_ref[...] = jnp.zeros_like(ng_ref)
        for h in range(GLA_HEADS):
            src = slice(h * GLA_DK, (h + 1) * GLA_DK)
            dst = slice(h * GLA_DK_PAD, h * GLA_DK_PAD + GLA_DK)
            wa_ref[LR_LANE0:LR_LANE0 + GLA_RANK, dst] = w_alpha_ref[:, src]
            ba_ref[:, dst] = b_alpha_ref[:, src]
        bf_ref[:, FG_LANE0:FG_LANE0 + FOX_HEADS] = b_forget_ref[...]
        ng_ref[:, 0:GLA_DV] = gla_g_ref[...]
        seg_ref[...] = (lax.broadcasted_iota(jnp.int32, seg_ref.shape, 0) // FOX_DH
                        == lax.broadcasted_iota(jnp.int32, seg_ref.shape, 1)).astype(BF16)
        wt_ref[...] = jnp.zeros_like(wt_ref)
        for s0, d0, width in _w_in_segments():
            for c in range(k_chunks):
                wt_ref[d0:d0 + width, c * LANES:(c + 1) * LANES] = (
                    w_in_ref[pl.ds(s0 * k_chunks + c, width, stride=k_chunks), :].astype(BF16))

    x = x_ref[...]
    xn = (x * _rms_scale(x, D_MODEL) * g_ref[...]).astype(BF16)
    nt = (((1,), (1,)), ((), ()))

    def proj(first, last):
        lo, hi = _OFF[first][0], _OFF[last][1]
        y = lax.dot_general(xn, wt_ref[lo:hi, :], nt, preferred_element_type=F32)
        return lambda name: y[:, _OFF[name][0] - lo:_OFF[name][1] - lo]

    tail = proj("mq", "small")
    gla = proj("gq", "gg")
    small = tail("small")
    logf = _log_sigmoid(small + bf_ref[...])
    c = logf.T[0:SUBLANES, :]
    lane = lax.broadcasted_iota(jnp.int32, c.shape, 1)
    shift = 1
    while shift < rows:
        c = c + jnp.where(lane >= shift, pltpu.roll(c, shift, axis=1), 0.0)
        shift *= 2
    c = c + carry_ref[:, 0:1]
    crow_ref[...] = c
    carry_ref[...] = jnp.broadcast_to(c[:, rows - 1:rows], carry_ref.shape)
    neg = -c
    hi = neg.astype(BF16).astype(F32)
    mid = (neg - hi).astype(BF16).astype(F32)
    low = neg - hi - mid
    parts = jnp.concatenate(
        [hi, mid, low, jnp.ones_like(c), jnp.zeros((LANES - 4 * SUBLANES, rows), F32)], axis=0)
    kx_ref[FOX_W:, :] = parts.astype(BF16)
    cq_ref[...] = parts.T.astype(BF16)

    z = (jnp.dot(small.astype(BF16), wa_ref[...].astype(BF16), preferred_element_type=F32)
         + ba_ref[...])
    loga_ref[...] = _log_sigmoid(z) * (1.0 / GLA_GATE_NORM)
    gq_ref[...] = gla("gq").astype(BF16)
    gk_ref[...] = gla("gk").astype(BF16)
    gv, gg = gla("gv").astype(BF16), _silu(gla("gg")).astype(BF16)
    for h in range(GLA_HEADS):
        src = slice(h * GLA_DV, (h + 1) * GLA_DV)
        dst = slice(h * GLA_DV_PAD, h * GLA_DV_PAD + GLA_DV)
        gv_ref[:, dst] = gv[:, src]
        gg_ref[:, dst] = gg[:, src]
    gla_local, gla_scan, gla_output = _gla_block(
        gq_ref, gk_ref, gv_ref, loga_ref, gg_ref, ng_ref, gla_ref,
        s_ref, lhs_ref, kv_ref, dec_ref, sprev_ref)

    def max_sq_norm(v):
        v32 = v.astype(F32)
        n2 = jnp.dot((v32 * v32).astype(BF16), seg_ref[...], preferred_element_type=F32)
        return jnp.max(n2, axis=0, keepdims=True)

    gla_local()
    fox_qk = proj("fq", "fk")
    mqg_ref[:, :MEM_W] = (tail("mq") * MEM_DH ** -0.5).astype(BF16)
    mqg_ref[:, MEM_W:] = _silu(tail("mg")).astype(BF16)
    gla_scan()
    fq = (fox_qk("fq") * FOX_DH ** -0.5).astype(BF16)
    fk = fox_qk("fk").astype(BF16)
    fqvg_ref[:, :FOX_W] = fq
    kx_ref[:FOX_W, :] = fox_qk("fk").T.astype(BF16)
    gla_output()
    fox_vg = proj("fv", "fgate")
    fqvg_ref[:, FOX_W:2 * FOX_W] = fox_vg("fv").astype(BF16)
    fqvg_ref[:, 2 * FOX_W:] = _silu(fox_vg("fgate")).astype(BF16)
    qn2_ref[0] = max_sq_norm(fq)
    kn2_ref[0] = jnp.maximum(kn2_ref[0], max_sq_norm(fk))


def _memkv(mem_ref, g_ref, w_ref, mk_ref, mv_ref):
    m = mem_ref[...]
    mn = (m * _rms_scale(m, D_MODEL) * g_ref[...]).astype(BF16)
    kv = jnp.dot(mn, w_ref[...].astype(BF16), preferred_element_type=F32)
    mk_ref[...] = kv[:, :MEM_W].astype(BF16)
    mv_ref[...] = kv[:, MEM_W:].astype(BF16)


def _gla_block(q_ref, k_ref, v_ref, loga_ref, gate_ref, ng_ref, o_ref,
               s_ref, lhs_ref, kv_ref, dec_ref, sprev_ref):
    C = GLA_CHUNK
    W = HEAD_PAIR * GLA_DV_PAD
    n_chunks = q_ref.shape[0] // C

    row = lax.broadcasted_iota(jnp.int32, (C, LANES), 0)
    lane = lax.broadcasted_iota(jnp.int32, (C, LANES), 1)
    lo_k = lane < GLA_DK_PAD
    causal = row >= jnp.where(lo_k, lane, lane - GLA_DK_PAD)
    lo_v = lax.broadcasted_iota(jnp.int32, (C, W), 1) < GLA_DV_PAD
    st_row = lax.broadcasted_iota(jnp.int32, (LANES, W), 0)
    st_lane = lax.broadcasted_iota(jnp.int32, (LANES, W), 1)
    own = (st_row < GLA_DK_PAD) == (st_lane < GLA_DV_PAD)
    eye = (lax.broadcasted_iota(jnp.int32, (LANES, LANES), 0)
           == lax.broadcasted_iota(jnp.int32, (LANES, LANES), 1))
    scale = GLA_DK ** -0.5
    nt = (((1,), (1,)), ((), ()))
    tn = (((0,), (0,)), ((), ()))
    ng = jnp.concatenate([ng_ref[...]] * HEAD_PAIR, axis=1)

    pairs = range(GLA_HEADS // HEAD_PAIR)

    def local(ci):
        rs = slice(ci * C, (ci + 1) * C)
        for p in pairs:
            ls = slice(p * LANES, (p + 1) * LANES)
            vs = slice(p * W, (p + 1) * W)
            b = loga_ref[rs, ls]
            shift = 1
            while shift < C:
                b = b + jnp.where(row >= shift, pltpu.roll(b, shift, axis=0), 0.0)
                shift *= 2
            b_last = b[C - 1:C, :]
            k2 = k_ref[rs, ls].astype(F32)
            qd = (q_ref[rs, ls].astype(F32) * scale * jnp.exp(b)).astype(BF16)
            kd = (k2 * jnp.exp(-b)).astype(BF16)
            ke = (k2 * jnp.exp(b_last - b)).astype(BF16)
            zk = jnp.zeros_like(kd)
            kd_blk = jnp.concatenate([jnp.where(lo_k, kd, zk), jnp.where(lo_k, zk, kd)], axis=0)
            attn = lax.dot_general(qd, kd_blk, nt, preferred_element_type=F32)
            lhs_ref[rs, vs] = jnp.concatenate([jnp.where(causal, attn, 0.0).astype(BF16), qd], axis=1)
            kv = lax.dot_general(ke, v_ref[rs, vs], tn, preferred_element_type=F32)
            kv_ref[p, ci] = jnp.where(own, kv, 0.0)
            dcol = jnp.exp(jnp.sum(jnp.where(eye, jnp.broadcast_to(b_last, (LANES, LANES)), 0.0),
                                   axis=1, keepdims=True))
            dec_ref[p, ci] = jnp.broadcast_to(dcol, (LANES, LANES))

    def scan(ci):
        for p in pairs:
            s_prev = s_ref[p]
            sprev_ref[p, ci] = s_prev.astype(BF16)
            s_ref[p] = jnp.tile(dec_ref[p, ci], (1, HEAD_PAIR)) * s_prev + kv_ref[p, ci]

    def output(ci):
        rs = slice(ci * C, (ci + 1) * C)
        for p in pairs:
            vs = slice(p * W, (p + 1) * W)
            v2 = v_ref[rs, vs]
            zv = jnp.zeros_like(v2)
            v_blk = jnp.concatenate([jnp.where(lo_v, v2, zv), jnp.where(lo_v, zv, v2)], axis=0)
            o = jnp.dot(lhs_ref[rs, vs], jnp.concatenate([v_blk, sprev_ref[p, ci]], axis=0),
                        preferred_element_type=F32)
            o2 = o * o
            ms = jnp.where(lo_v, jnp.sum(o2[:, :GLA_DV_PAD], axis=1, keepdims=True),
                           jnp.sum(o2[:, GLA_DV_PAD:], axis=1, keepdims=True))
            on = o * lax.rsqrt(ms * (1.0 / GLA_DV) + EPS) * ng
            og = (on * gate_ref[rs, vs].astype(F32)).astype(BF16)
            for hh in range(HEAD_PAIR):
                c0 = (p * HEAD_PAIR + hh) * GLA_DV
                o_ref[rs, c0:c0 + GLA_DV] = og[:, hh * GLA_DV_PAD:hh * GLA_DV_PAD + GLA_DV]

    def all_chunks(phase):
        return lambda: [phase(ci) for ci in range(n_chunks)]

    return all_chunks(local), all_chunks(scan), all_chunks(output)


def _fox_kernel(qn2_ref, kn2_ref, cend_ref, q_ref, cq_ref, k_ref, cx_ref, v_ref, gate_ref, o_ref,
                m_ref, acc_ref):
    blk = FOX_KEYS
    streams = range(q_ref.shape[0] // blk)
    pair = pl.program_id(0)
    qi = pl.program_id(1)
    nblk = pl.num_programs(1) * len(streams)
    lane = lax.broadcasted_iota(jnp.int32, (1, LANES), 1)
    lo_lanes = lane < FOX_DH
    reps = blk // LANES
    diag = [qi * len(streams) + s for s in streams]
    heads = [pair * HEAD_PAIR + hh for hh in range(HEAD_PAIR)]

    q = q_ref[...]
    zero = jnp.zeros_like(q)
    q_lo, q_hi = jnp.where(lo_lanes, q, zero), jnp.where(lo_lanes, zero, q)
    q_stack = [jnp.concatenate([q_lo[s * blk:(s + 1) * blk], q_hi[s * blk:(s + 1) * blk]], axis=0)
               for s in streams]
    g_row = lax.broadcasted_iota(jnp.int32, (LANES, LANES), 0)
    g_col = lax.broadcasted_iota(jnp.int32, (LANES, LANES), 1)
    xlane = lax.broadcasted_iota(jnp.int32, (HEAD_PAIR * blk, LANES), 1)
    cq = cq_ref[...]

    def lane_map(h, shift):
        g = jnp.where((g_row == CX_ONE) & ((g_col == CX_HI + h) | (g_col == CX_MID + h)
                                           | (g_col == CX_LO + h)), 1.0, 0.0)
        if shift:
            for part, base in enumerate((CX_HI, CX_MID, CX_LO)):
                g = jnp.where((g_row == base + h) & (g_col == CX_ONE + part), -1.0, g)
        return g.astype(BF16)

    def q_aug(s, shift, void=None):
        cs = cq[s * blk:(s + 1) * blk]
        both = jnp.dot(cs, jnp.concatenate([lane_map(h, shift) for h in heads], axis=1),
                       preferred_element_type=F32)
        extra = jnp.concatenate([both[:, :LANES], both[:, LANES:]], axis=0)
        if void is not None:
            extra = jnp.where(jnp.logical_and(xlane == CX_ONE + 3, void), NEG_BIG, extra)
        return jnp.concatenate([q_stack[s], extra.astype(BF16)], axis=1)

    acc_ref[...] = jnp.zeros_like(acc_ref)
    qpos = lax.broadcasted_iota(jnp.int32, (blk, blk), 0)
    kpos = lax.broadcasted_iota(jnp.int32, (blk, blk), 1)

    def block(j):
        ks = pl.ds(pl.multiple_of(jnp.maximum(j, 0) * blk, blk), blk)
        k_aug = jnp.concatenate([k_ref[:, ks], cx_ref[:, ks]], axis=0)
        vb = v_ref[ks, :]
        one = jnp.ones_like(vb)
        return k_aug, (jnp.where(lo_lanes, vb, one), jnp.where(lo_lanes, one, vb))

    def step_online(s, j, qa, masked):
        rows = slice(s * blk, (s + 1) * blk)
        k_aug, vaug = block(j)
        s_all = jnp.dot(qa, k_aug, preferred_element_type=F32)
        for hh in range(HEAD_PAIR):
            sc = s_all[hh * blk:(hh + 1) * blk]
            if masked:
                sc = jnp.where(kpos <= qpos, sc, NEG_BIG)
            m_prev = m_ref[hh, rows]
            m_new = jnp.maximum(m_prev, jnp.max(sc, axis=1, keepdims=True))
            p = jnp.exp(sc - jnp.tile(m_new, (1, reps)))
            alpha = jnp.exp(m_prev - m_new)
            pv = jnp.dot(p.astype(BF16), vaug[hh], preferred_element_type=F32)
            acc_ref[hh, rows] = alpha * acc_ref[hh, rows] + pv
            m_ref[hh, rows] = m_new

    def step_direct(s, j, qa, masked, only=None):
        rows = slice(s * blk, (s + 1) * blk)
        k_aug, vaug = block(j)
        hsel = range(HEAD_PAIR) if only is None else (only,)
        lhs = qa if only is None else qa[only * blk:(only + 1) * blk]
        s_all = jnp.dot(lhs, k_aug, preferred_element_type=F32)
        for n, hh in enumerate(hsel):
            sc = s_all[n * blk:(n + 1) * blk]
            if masked:
                sc = jnp.where(kpos <= qpos, sc, NEG_BIG)
            acc_ref[hh, rows] += jnp.dot(jnp.exp(sc).astype(BF16), vaug[hh],
                                         preferred_element_type=F32)

    k_max2 = [kn2_ref[h] for h in heads]

    def norm2(s, hh):
        stat = (diag[s] * blk // PROJ_ROWS) * LANES
        return (NORM_SLACK * NORM_SLACK) * qn2_ref[stat + heads[hh]] * k_max2[hh]

    n2 = [[norm2(s, hh) for hh in range(HEAD_PAIR)] for s in streams]
    gap0 = [[FOX_SKIP_NATS + cend_ref[heads[hh] * nblk + jnp.maximum(diag[s] - 1, 0)]
             for hh in range(HEAD_PAIR)] for s in streams]

    def live_head(t, hh, direct):
        keep = False
        for s in streams:
            j = diag[s] - 1 - t
            gap = gap0[s][hh] - cend_ref[heads[hh] * nblk + jnp.maximum(j, 0)]
            bound2 = n2[s][hh] if direct else 4.0 * n2[s][hh]
            dead = jnp.logical_and(gap <= 0.0, bound2 <= gap * gap)
            keep = jnp.logical_or(keep, jnp.logical_and(j >= 0, jnp.logical_not(dead)))
        return keep

    def sweeps(stepper, direct):
        def sweep(back, masked=False, only=None):
            for s in streams:
                stepper(s, diag[s] - back, masked, only)

        def loop(t0, cond, only=None):
            def body(t):
                sweep(t + 1, only=only)
                return t + 1
            return lax.while_loop(cond, body, t0)

        def live(t, hh):
            return live_head(t, hh, direct)

        sweep(0, masked=True)
        sweep(1)
        if direct:
            t_both = loop(1, lambda t: jnp.logical_and(live(t, 0), live(t, 1)))
            for hh in range(HEAD_PAIR):
                loop(t_both, lambda t, hh=hh: live(t, hh), only=hh)
        else:
            loop(1, lambda t: jnp.logical_or(live(t, 0), live(t, 1)))

    direct_ok = True
    for s in streams:
        for hh in range(HEAD_PAIR):
            direct_ok = jnp.logical_and(direct_ok, n2[s][hh] <= FOX_DIRECT_NORM2)

    @pl.when(direct_ok)
    def _():
        q_dir = [q_aug(s, shift=True) for s in streams]
        q_void = [q_aug(s, shift=True, void=True) for s in streams]
        sweeps(lambda s, j, masked, only: step_direct(
            s, j, q_dir[s] if masked else jnp.where(j >= 0, q_dir[s], q_void[s]), masked, only),
            direct=True)

    @pl.when(jnp.logical_not(direct_ok))
    def _():
        m_ref[...] = jnp.full_like(m_ref, NEG_BIG)
        sweeps(lambda s, j, masked, only: step_online(
            s, j, q_aug(s, shift=False, void=j < 0), masked), direct=False)

    outs = []
    for hh in range(HEAD_PAIR):
        acc = acc_ref[hh]
        outs.append(acc / pltpu.roll(acc, FOX_DH, axis=1))
    o = jnp.where(lo_lanes, outs[0], outs[1])
    o_ref[...] = (o * gate_ref[...].astype(F32)).astype(BF16)


def _out_kernel(x_ref, gla_ref, fox_ref, mq_ref, mg_ref, mem_ref, mem_g_ref, w_mem_ref,
                w_out_ref, fg_ref, o_ref, wo_ref, mk_ref, mv_ref):
    @pl.when(pl.program_id(0) == 0)
    def _():
        wo_ref[...] = w_out_ref[...].astype(BF16)
        _memkv(mem_ref, mem_g_ref, w_mem_ref, mk_ref, mv_ref)

    lane = lax.broadcasted_iota(jnp.int32, (1, LANES), 1)
    lo_lanes = lane < MEM_DH
    nt = (((1,), (1,)), ((), ()))
    mem_parts = []
    for p in range(MEM_HEADS // HEAD_PAIR):
        ls = slice(p * LANES, (p + 1) * LANES)
        q = mq_ref[:, ls]
        kb = mk_ref[:, ls]
        vb = mv_ref[:, ls]
        zero = jnp.zeros_like(q)
        one = jnp.ones_like(vb)
        qh = (jnp.where(lo_lanes, q, zero), jnp.where(lo_lanes, zero, q))
        vaug = (jnp.where(lo_lanes, vb, one), jnp.where(lo_lanes, one, vb))
        outs = []
        for hh in range(HEAD_PAIR):
            s = lax.dot_general(qh[hh], kb, nt, preferred_element_type=F32)
            pexp = jnp.exp(s - jnp.max(s, axis=1, keepdims=True))
            pv = jnp.dot(pexp.astype(BF16), vaug[hh], preferred_element_type=F32)
            outs.append(pv / pltpu.roll(pv, MEM_DH, axis=1))
        o = jnp.where(lo_lanes, outs[0], outs[1])
        mem_parts.append((o * mg_ref[:, ls].astype(F32)).astype(BF16))
    mixed = jnp.concatenate([gla_ref[...], fox_ref[...]] + mem_parts, axis=1)
    y = x_ref[...] + jnp.dot(mixed, wo_ref[...], preferred_element_type=F32)
    o_ref[...] = y * _rms_scale(y, D_MODEL) * fg_ref[...]


def _params(*sem):
    return pltpu.CompilerParams(dimension_semantics=sem, vmem_limit_bytes=VMEM_LIMIT)


def _layer(x, mem, norm_g, w_in, w_alpha_up, b_alpha, b_forget, gla_norm_g,
           mem_norm_g, w_mem_kv, w_out, out_g):
    w_in_t = jnp.transpose(w_in[None], (0, 2, 1)).reshape(-1, LANES)
    T = x.shape[0]
    M = mem.shape[0]

    def rows(width, n=PROJ_ROWS):
        return pl.BlockSpec((n, width), lambda i: (i, 0))

    def whole(shape):
        return pl.BlockSpec(shape, lambda i: (0,) * len(shape))

    def once(shape):
        return pl.BlockSpec(shape, lambda i: (0,) * len(shape), pipeline_mode=pl.Buffered(1))

    bshape = lambda w: jax.ShapeDtypeStruct((T, w), BF16)
    nproj = T // PROJ_ROWS
    stat_spec = pl.BlockSpec((1, 1, LANES), lambda i: (i, 0, 0))
    stat_shape = jax.ShapeDtypeStruct((nproj, 1, LANES), F32)
    gla_pairs, gla_chunks = GLA_HEADS // HEAD_PAIR, PROJ_ROWS // GLA_CHUNK
    pair_w = HEAD_PAIR * GLA_DV_PAD
    (gla, fqvg, kx, mqg, crow, cq, qn2, kn2) = pl.pallas_call(
        _proj_kernel,
        grid=(nproj,),
        in_specs=[rows(D_MODEL), whole((1, D_MODEL)),
                  pl.BlockSpec(w_in_t.shape, lambda i: (0, 0), pipeline_mode=pl.Buffered(1)),
                  whole(w_alpha_up.shape), whole((1, GLA_HEADS * GLA_DK)),
                  whole((1, FOX_HEADS)), whole((1, GLA_DV))],
        out_specs=[rows(GLA_OUT_W), rows(3 * FOX_W),
                   pl.BlockSpec((FOX_W + LANES, PROJ_ROWS), lambda i: (0, i)),
                   rows(2 * MEM_W),
                   pl.BlockSpec((SUBLANES, PROJ_ROWS), lambda i: (0, i)), rows(LANES),
                   stat_spec, pl.BlockSpec((1, 1, LANES), lambda i: (0, 0, 0))],
        out_shape=[bshape(GLA_OUT_W), bshape(3 * FOX_W),
                   jax.ShapeDtypeStruct((FOX_W + LANES, T), BF16),
                   bshape(2 * MEM_W),
                   jax.ShapeDtypeStruct((SUBLANES, T), F32), bshape(LANES),
                   stat_shape, jax.ShapeDtypeStruct((1, 1, LANES), F32)],
        scratch_shapes=[
            pltpu.VMEM((SUBLANES, LANES), F32),
            pltpu.VMEM((IN_COLS_PAD, D_MODEL), BF16),
            pltpu.VMEM((SMALL_W, GLA_QK_W), F32),
            pltpu.VMEM((1, GLA_QK_W), F32),
            pltpu.VMEM((1, SMALL_W), F32),
            pltpu.VMEM((1, GLA_DV_PAD), F32),
            pltpu.VMEM((FOX_W, LANES), BF16),
            pltpu.VMEM((PROJ_ROWS, GLA_QK_W), BF16),
            pltpu.VMEM((PROJ_ROWS, GLA_QK_W), BF16),
            pltpu.VMEM((PROJ_ROWS, GLA_V_W), BF16),
            pltpu.VMEM((PROJ_ROWS, GLA_V_W), BF16),
            pltpu.VMEM((PROJ_ROWS, GLA_QK_W), F32),
            pltpu.VMEM((gla_pairs, LANES, pair_w), F32),
            pltpu.VMEM((PROJ_ROWS, GLA_V_W), BF16),
            pltpu.VMEM((gla_pairs, gla_chunks, LANES, pair_w), F32),
            pltpu.VMEM((gla_pairs, gla_chunks, LANES, LANES), F32),
            pltpu.VMEM((gla_pairs, gla_chunks, LANES, pair_w), BF16)],
        compiler_params=_params("arbitrary"),
        name="proj",
    )(x, norm_g[None, :], w_in_t, w_alpha_up, b_alpha[None, :], b_forget[None, :],
      gla_norm_g[None, :])

    cend = crow[:FOX_HEADS, FOX_KEYS - 1::FOX_KEYS].reshape(-1)
    fox_pairs = FOX_HEADS // HEAD_PAIR
    pair_rows = pl.BlockSpec((FOX_BLOCK, LANES), lambda p, i, *_: (i, p))
    fox = pl.pallas_call(
        _fox_kernel,
        grid_spec=pltpu.PrefetchScalarGridSpec(
            num_scalar_prefetch=3,
            grid=(fox_pairs, T // FOX_BLOCK),
            in_specs=[pair_rows,
                      pl.BlockSpec((FOX_BLOCK, LANES), lambda p, i, *_: (i, 0)),
                      pl.BlockSpec((LANES, T), lambda p, i, *_: (p, 0)),
                      pl.BlockSpec((LANES, T), lambda p, i, *_: (fox_pairs, 0)),
                      pl.BlockSpec((T, LANES), lambda p, i, *_: (0, fox_pairs + p)),
                      pl.BlockSpec((FOX_BLOCK, LANES),
                                   lambda p, i, *_: (i, 2 * fox_pairs + p))],
            out_specs=pair_rows,
            scratch_shapes=[pltpu.VMEM((HEAD_PAIR, FOX_BLOCK, LANES), F32),
                            pltpu.VMEM((HEAD_PAIR, FOX_BLOCK, LANES), F32)]),
        out_shape=bshape(FOX_W),
        compiler_params=_params("arbitrary", "arbitrary"),
        name="fox",
    )(qn2.reshape(-1), kn2.reshape(-1), cend, fqvg, cq, kx, kx, fqvg, fqvg)

    out = pl.pallas_call(
        _out_kernel,
        grid=(T // OUT_ROWS,),
        in_specs=[rows(D_MODEL, OUT_ROWS), rows(GLA_OUT_W, OUT_ROWS), rows(FOX_W, OUT_ROWS),
                  pl.BlockSpec((OUT_ROWS, MEM_W), lambda i: (i, 0)),
                  pl.BlockSpec((OUT_ROWS, MEM_W), lambda i: (i, 1)),
                  once(mem.shape), once((1, D_MODEL)), once(w_mem_kv.shape), once(w_out.shape),
                  whole((1, D_MODEL))],
        out_specs=rows(D_MODEL, OUT_ROWS),
        out_shape=jax.ShapeDtypeStruct((T, D_MODEL), F32),
        scratch_shapes=[pltpu.VMEM((D_MODEL, D_MODEL), BF16),
                        pltpu.VMEM((M, MEM_W), BF16), pltpu.VMEM((M, MEM_W), BF16)],
        compiler_params=_params("arbitrary"),
        name="out",
    )(x, gla, fox, mqg, mqg, mem, mem_norm_g[None, :], w_mem_kv, w_out, out_g[None, :])
    return out


def kernel(x, mem, norm_g, w_in, w_alpha_up, b_alpha, b_forget, gla_norm_g, mem_norm_g,
           w_mem_kv, w_out, final_norm_g):
    assert x.shape[0] == 1 and mem.shape[0] == 1 and norm_g.shape[0] == 1
    assert x.shape[1] % max(PROJ_ROWS, FOX_BLOCK, OUT_ROWS) == 0
    out = _layer(x[0], mem[0], norm_g[0], w_in[0], w_alpha_up[0], b_alpha[0], b_forget[0],
                 gla_norm_g[0], mem_norm_g[0], w_mem_kv[0], w_out[0], final_norm_g)
    return out[None]
```

```python
import jax
import jax.numpy as jnp
from jax import lax
from jax.experimental import pallas as pl
from jax.experimental.pallas import tpu as pltpu

F32 = jnp.float32
BF16 = jnp.bfloat16

EPS = 1e-6
LANES = 128
SUBLANES = 8

D_MODEL = 1024
GLA_HEADS, GLA_DK, GLA_DV, GLA_RANK = 4, 48, 96, 16
GLA_DK_PAD = 64
GLA_DV_PAD = LANES
GLA_GATE_NORM = 16.0
GLA_CHUNK = 64
FOX_HEADS, FOX_DH = 6, 64
MEM_HEADS, MEM_DH = 4, 64
HEAD_PAIR = 2
GLA_QK_W = GLA_HEADS * GLA_DK_PAD
GLA_V_W = GLA_HEADS * GLA_DV_PAD
GLA_OUT_W = GLA_HEADS * GLA_DV
FOX_W = FOX_HEADS * FOX_DH
MEM_W = MEM_HEADS * MEM_DH
SMALL_W = LANES
FG_LANE0 = 0
LR_LANE0 = SUBLANES

_GROUPS = (("gq", GLA_QK_W), ("gk", GLA_QK_W), ("gv", GLA_OUT_W), ("gg", GLA_OUT_W),
           ("fq", FOX_W), ("fk", FOX_W), ("fv", FOX_W), ("fgate", FOX_W),
           ("mq", MEM_W), ("mg", MEM_W), ("small", SMALL_W))
_OFF = {}
_o = 0
for _n, _w in _GROUPS:
    _OFF[_n] = (_o, _o + _w)
    _o += _w
IN_COLS_PAD = _o

PROJ_ROWS = 1024
FOX_BLOCK = 4096
FOX_KEYS = 256
OUT_ROWS = 1024
VMEM_LIMIT = 56 * 1024 * 1024

NEG_BIG = -1e30
FOX_SKIP_NATS = 105.0
NORM_SLACK = 1.02
FOX_DIRECT_NORM2 = 3600.0
CX_HI, CX_MID, CX_LO, CX_ONE = 0, 8, 16, 24


def _log_sigmoid(z):
    return jnp.minimum(z, 0.0) - jnp.log(1.0 + jnp.exp(-jnp.abs(z)))


def _silu(z):
    return z / (1.0 + jnp.exp(-z))


def _rms_scale(v, width):
    return lax.rsqrt(jnp.sum(v * v, axis=-1, keepdims=True) * (1.0 / width) + EPS)


def _w_in_segments():
    qk, gw = GLA_HEADS * GLA_DK, GLA_HEADS * GLA_DV
    src = {}
    o = 0
    for name, width in (("gq", qk), ("gk", qk), ("gv", gw), ("lr", GLA_RANK), ("gg", gw),
                        ("fq", FOX_W), ("fk", FOX_W), ("fv", FOX_W), ("fg", FOX_HEADS),
                        ("fgate", FOX_W), ("mq", MEM_W), ("mg", MEM_W)):
        src[name] = o
        o += width
    segs = []
    for name, d, d_pad in (("gq", GLA_DK, GLA_DK_PAD), ("gk", GLA_DK, GLA_DK_PAD)):
        segs += [(src[name] + h * d, _OFF[name][0] + h * d_pad, d) for h in range(GLA_HEADS)]
    segs += [(src[name], _OFF[name][0], _OFF[name][1] - _OFF[name][0])
             for name in ("gv", "gg", "fq", "fk", "fv", "fgate", "mq", "mg")]
    segs += [(src["fg"], _OFF["small"][0] + FG_LANE0, FOX_HEADS),
             (src["lr"], _OFF["small"][0] + LR_LANE0, GLA_RANK)]
    return tuple(segs)


def _proj_kernel(x_ref, g_ref, w_in_ref, w_alpha_ref, b_alpha_ref, b_forget_ref, gla_g_ref,
                 gla_ref, fqvg_ref, kx_ref, mqg_ref, crow_ref, cq_ref, qn2_ref, kn2_ref,
                 carry_ref, wt_ref, wa_ref, ba_ref, bf_ref, ng_ref, seg_ref,
                 gq_ref, gk_ref, gv_ref, gg_ref, loga_ref,
                 s_ref, lhs_ref, kv_ref, dec_ref, sprev_ref):
    rows = x_ref.shape[0]
    k_chunks = D_MODEL // LANES

    @pl.when(pl.program_id(0) == 0)
    def _():
        carry_ref[...] = jnp.zeros_like(carry_ref)
        s_ref[...] = jnp.zeros_like(s_ref)
        kn2_ref[...] = jnp.zeros_like(kn2_ref)
        gv_ref[...] = jnp.zeros_like(gv_ref)
        gg_ref[...] = jnp.zeros_like(gg_ref)
        wa_ref[...] = jnp.zeros_like(wa_ref)
        ba_ref[...] = jnp.zeros_like(ba_ref)
        bf_ref[...] = jnp.zeros_like(bf_ref)
        ng_ref[...] = jnp.zeros_like(ng_ref)
        for h in range(GLA_HEADS):
            src = slice(h * GLA_DK, (h + 1) * GLA_DK)
            dst = slice(h * GLA_DK_PAD, h * GLA_DK_PAD + GLA_DK)
            wa_ref[LR_LANE0:LR_LANE0 + GLA_RANK, dst] = w_alpha_ref[:, src]
            ba_ref[:, dst] = b_alpha_ref[:, src]
        bf_ref[:, FG_LANE0:FG_LANE0 + FOX_HEADS] = b_forget_ref[...]
        ng_ref[:, 0:GLA_DV] = gla_g_ref[...]
        seg_ref[...] = (lax.broadcasted_iota(jnp.int32, seg_ref.shape, 0) // FOX_DH
                        == lax.broadcasted_iota(jnp.int32, seg_ref.shape, 1)).astype(BF16)
        wt_ref[...] = jnp.zeros_like(wt_ref)
        for s0, d0, width in _w_in_segments():
            for c in range(k_chunks):
                wt_ref[d0:d0 + width, c * LANES:(c + 1) * LANES] = (
                    w_in_ref[pl.ds(s0 * k_chunks + c, width, stride=k_chunks), :].astype(BF16))

    x = x_ref[...]
    xn = (x * _rms_scale(x, D_MODEL) * g_ref[...]).astype(BF16)
    nt = (((1,), (1,)), ((), ()))

    def proj(first, last):
        lo, hi = _OFF[first][0], _OFF[last][1]
        y = lax.dot_general(xn, wt_ref[lo:hi, :], nt, preferred_element_type=F32)
        return lambda name: y[:, _OFF[name][0] - lo:_OFF[name][1] - lo]

    tail = proj("mq", "small")
    gla = proj("gq", "gg")
    small = tail("small")
    logf = _log_sigmoid(small + bf_ref[...])
    c = logf.T[0:SUBLANES, :]
    lane = lax.broadcasted_iota(jnp.int32, c.shape, 1)
    shift = 1
    while shift < rows:
        c = c + jnp.where(lane >= shift, pltpu.roll(c, shift, axis=1), 0.0)
        shift *= 2
    c = c + carry_ref[:, 0:1]
    crow_ref[...] = c
    carry_ref[...] = jnp.broadcast_to(c[:, rows - 1:rows], carry_ref.shape)
    neg = -c
    hi = neg.astype(BF16).astype(F32)
    mid = (neg - hi).astype(BF16).astype(F32)
    low = neg - hi - mid
    parts = jnp.concatenate(
        [hi, mid, low, jnp.ones_like(c), jnp.zeros((LANES - 4 * SUBLANES, rows), F32)], axis=0)
    kx_ref[FOX_W:, :] = parts.astype(BF16)
    cq_ref[...] = parts.T.astype(BF16)

    z = (jnp.dot(small.astype(BF16), wa_ref[...].astype(BF16), preferred_element_type=F32)
         + ba_ref[...])
    loga_ref[...] = _log_sigmoid(z) * (1.0 / GLA_GATE_NORM)
    gq_ref[...] = gla("gq").astype(BF16)
    gk_ref[...] = gla("gk").astype(BF16)
    gv, gg = gla("gv").astype(BF16), _silu(gla("gg")).astype(BF16)
    for h in range(GLA_HEADS):
        src = slice(h * GLA_DV, (h + 1) * GLA_DV)
        dst = slice(h * GLA_DV_PAD, h * GLA_DV_PAD + GLA_DV)
        gv_ref[:, dst] = gv[:, src]
        gg_ref[:, dst] = gg[:, src]
    gla_local, gla_scan, gla_output = _gla_block(
        gq_ref, gk_ref, gv_ref, loga_ref, gg_ref, ng_ref, gla_ref,
        s_ref, lhs_ref, kv_ref, dec_ref, sprev_ref)

    def max_sq_norm(v):
        v32 = v.astype(F32)
        n2 = jnp.dot((v32 * v32).astype(BF16), seg_ref[...], preferred_element_type=F32)
        return jnp.max(n2, axis=0, keepdims=True)

    gla_local()
    fox_qk = proj("fq", "fk")
    mqg_ref[:, :MEM_W] = (tail("mq") * MEM_DH ** -0.5).astype(BF16)
    mqg_ref[:, MEM_W:] = _silu(tail("mg")).astype(BF16)
    gla_scan()
    fq = (fox_qk("fq") * FOX_DH ** -0.5).astype(BF16)
    fk = fox_qk("fk").astype(BF16)
    fqvg_ref[:, :FOX_W] = fq
    kx_ref[:FOX_W, :] = fox_qk("fk").T.astype(BF16)
    gla_output()
    fox_vg = proj("fv", "fgate")
    fqvg_ref[:, FOX_W:2 * FOX_W] = fox_vg("fv").astype(BF16)
    fqvg_ref[:, 2 * FOX_W:] = _silu(fox_vg("fgate")).astype(BF16)
    qn2_ref[0] = max_sq_norm(fq)
    kn2_ref[0] = jnp.maximum(kn2_ref[0], max_sq_norm(fk))


def _memkv(mem_ref, g_ref, w_ref, mk_ref, mv_ref):
    m = mem_ref[...]
    mn = (m * _rms_scale(m, D_MODEL) * g_ref[...]).astype(BF16)
    kv = jnp.dot(mn, w_ref[...].astype(BF16), preferred_element_type=F32)
    mk_ref[...] = kv[:, :MEM_W].astype(BF16)
    mv_ref[...] = kv[:, MEM_W:].astype(BF16)


def _gla_block(q_ref, k_ref, v_ref, loga_ref, gate_ref, ng_ref, o_ref,
               s_ref, lhs_ref, kv_ref, dec_ref, sprev_ref):
    C = GLA_CHUNK
    W = HEAD_PAIR * GLA_DV_PAD
    n_chunks = q_ref.shape[0] // C

    row = lax.broadcasted_iota(jnp.int32, (C, LANES), 0)
    lane = lax.broadcasted_iota(jnp.int32, (C, LANES), 1)
    lo_k = lane < GLA_DK_PAD
    causal = row >= jnp.where(lo_k, lane, lane - GLA_DK_PAD)
    lo_v = lax.broadcasted_iota(jnp.int32, (C, W), 1) < GLA_DV_PAD
    st_row = lax.broadcasted_iota(jnp.int32, (LANES, W), 0)
    st_lane = lax.broadcasted_iota(jnp.int32, (LANES, W), 1)
    own = (st_row < GLA_DK_PAD) == (st_lane < GLA_DV_PAD)
    eye = (lax.broadcasted_iota(jnp.int32, (LANES, LANES), 0)
           == lax.broadcasted_iota(jnp.int32, (LANES, LANES), 1))
    scale = GLA_DK ** -0.5
    nt = (((1,), (1,)), ((), ()))
    tn = (((0,), (0,)), ((), ()))
    ng = jnp.concatenate([ng_ref[...]] * HEAD_PAIR, axis=1)

    pairs = range(GLA_HEADS // HEAD_PAIR)

    def local(ci):
        rs = slice(ci * C, (ci + 1) * C)
        for p in pairs:
            ls = slice(p * LANES, (p + 1) * LANES)
            vs = slice(p * W, (p + 1) * W)
            b = loga_ref[rs, ls]
            shift = 1
            while shift < C:
                b = b + jnp.where(row >= shift, pltpu.roll(b, shift, axis=0), 0.0)
                shift *= 2
            b_last = b[C - 1:C, :]
            k2 = k_ref[rs, ls].astype(F32)
            qd = (q_ref[rs, ls].astype(F32) * scale * jnp.exp(b)).astype(BF16)
            kd = (k2 * jnp.exp(-b)).astype(BF16)
            ke = (k2 * jnp.exp(b_last - b)).astype(BF16)
            zk = jnp.zeros_like(kd)
            kd_blk = jnp.concatenate([jnp.where(lo_k, kd, zk), jnp.where(lo_k, zk, kd)], axis=0)
            attn = lax.dot_general(qd, kd_blk, nt, preferred_element_type=F32)
            lhs_ref[rs, vs] = jnp.concatenate([jnp.where(causal, attn, 0.0).astype(BF16), qd], axis=1)
            kv = lax.dot_general(ke, v_ref[rs, vs], tn, preferred_element_type=F32)
            kv_ref[p, ci] = jnp.where(own, kv, 0.0)
            dcol = jnp.exp(jnp.sum(jnp.where(eye, jnp.broadcast_to(b_last, (LANES, LANES)), 0.0),
                                   axis=1, keepdims=True))
            dec_ref[p, ci] = jnp.broadcast_to(dcol, (LANES, LANES))

    def scan(ci):
        for p in pairs:
            s_prev = s_ref[p]
            sprev_ref[p, ci] = s_prev.astype(BF16)
            s_ref[p] = jnp.tile(dec_ref[p, ci], (1, HEAD_PAIR)) * s_prev + kv_ref[p, ci]

    def output(ci):
        rs = slice(ci * C, (ci + 1) * C)
        for p in pairs:
            vs = slice(p * W, (p + 1) * W)
            v2 = v_ref[rs, vs]
            zv = jnp.zeros_like(v2)
            v_blk = jnp.concatenate([jnp.where(lo_v, v2, zv), jnp.where(lo_v, zv, v2)], axis=0)
            o = jnp.dot(lhs_ref[rs, vs], jnp.concatenate([v_blk, sprev_ref[p, ci]], axis=0),
                        preferred_element_type=F32)
            o2 = o * o
            ms = jnp.where(lo_v, jnp.sum(o2[:, :GLA_DV_PAD], axis=1, keepdims=True),
                           jnp.sum(o2[:, GLA_DV_PAD:], axis=1, keepdims=True))
            on = o * lax.rsqrt(ms * (1.0 / GLA_DV) + EPS) * ng
            og = (on * gate_ref[rs, vs].astype(F32)).astype(BF16)
            for hh in range(HEAD_PAIR):
                c0 = (p * HEAD_PAIR + hh) * GLA_DV
                o_ref[rs, c0:c0 + GLA_DV] = og[:, hh * GLA_DV_PAD:hh * GLA_DV_PAD + GLA_DV]

    def all_chunks(phase):
        return lambda: [phase(ci) for ci in range(n_chunks)]

    return all_chunks(local), all_chunks(scan), all_chunks(output)


def _fox_kernel(qn2_ref, kn2_ref, cend_ref, q_ref, cq_ref, k_ref, cx_ref, v_ref, gate_ref, o_ref,
                m_ref, acc_ref):
    blk = FOX_KEYS
    streams = range(q_ref.shape[0] // blk)
    pair = pl.program_id(0)
    qi = pl.program_id(1)
    nblk = pl.num_programs(1) * len(streams)
    lane = lax.broadcasted_iota(jnp.int32, (1, LANES), 1)
    lo_lanes = lane < FOX_DH
    reps = blk // LANES
    diag = [qi * len(streams) + s for s in streams]
    heads = [pair * HEAD_PAIR + hh for hh in range(HEAD_PAIR)]

    q = q_ref[...]
    zero = jnp.zeros_like(q)
    q_lo, q_hi = jnp.where(lo_lanes, q, zero), jnp.where(lo_lanes, zero, q)
    q_stack = [jnp.concatenate([q_lo[s * blk:(s + 1) * blk], q_hi[s * blk:(s + 1) * blk]], axis=0)
               for s in streams]
    g_row = lax.broadcasted_iota(jnp.int32, (LANES, LANES), 0)
    g_col = lax.broadcasted_iota(jnp.int32, (LANES, LANES), 1)
    xlane = lax.broadcasted_iota(jnp.int32, (HEAD_PAIR * blk, LANES), 1)
    cq = cq_ref[...]

    def lane_map(h, shift):
        g = jnp.where((g_row == CX_ONE) & ((g_col == CX_HI + h) | (g_col == CX_MID + h)
                                           | (g_col == CX_LO + h)), 1.0, 0.0)
        if shift:
            for part, base in enumerate((CX_HI, CX_MID, CX_LO)):
                g = jnp.where((g_row == base + h) & (g_col == CX_ONE + part), -1.0, g)
        return g.astype(BF16)

    def q_aug(s, shift, void=None):
        cs = cq[s * blk:(s + 1) * blk]
        extra = jnp.concatenate([jnp.dot(cs, lane_map(h, shift), preferred_element_type=F32)
                                 for h in heads], axis=0)
        if void is not None:
            extra = jnp.where(jnp.logical_and(xlane == CX_ONE + 3, void), NEG_BIG, extra)
        return jnp.concatenate([q_stack[s], extra.astype(BF16)], axis=1)

    acc_ref[...] = jnp.zeros_like(acc_ref)
    qpos = lax.broadcasted_iota(jnp.int32, (blk, blk), 0)
    kpos = lax.broadcasted_iota(jnp.int32, (blk, blk), 1)

    def block(j):
        ks = pl.ds(pl.multiple_of(jnp.maximum(j, 0) * blk, blk), blk)
        k_aug = jnp.concatenate([k_ref[:, ks], cx_ref[:, ks]], axis=0)
        vb = v_ref[ks, :]
        one = jnp.ones_like(vb)
        return k_aug, (jnp.where(lo_lanes, vb, one), jnp.where(lo_lanes, one, vb))

    def step_online(s, j, qa, masked):
        rows = slice(s * blk, (s + 1) * blk)
        k_aug, vaug = block(j)
        s_all = jnp.dot(qa, k_aug, preferred_element_type=F32)
        for hh in range(HEAD_PAIR):
            sc = s_all[hh * blk:(hh + 1) * blk]
            if masked:
                sc = jnp.where(kpos <= qpos, sc, NEG_BIG)
            m_prev = m_ref[hh, rows]
            m_new = jnp.maximum(m_prev, jnp.max(sc, axis=1, keepdims=True))
            p = jnp.exp(sc - jnp.tile(m_new, (1, reps)))
            alpha = jnp.exp(m_prev - m_new)
            pv = jnp.dot(p.astype(BF16), vaug[hh], preferred_element_type=F32)
            acc_ref[hh, rows] = alpha * acc_ref[hh, rows] + pv
            m_ref[hh, rows] = m_new

    def step_direct(s, j, qa, masked, only=None):
        rows = slice(s * blk, (s + 1) * blk)
        k_aug, vaug = block(j)
        hsel = range(HEAD_PAIR) if only is None else (only,)
        lhs = qa if only is None else qa[only * blk:(only + 1) * blk]
        s_all = jnp.dot(lhs, k_aug, preferred_element_type=F32)
        for n, hh in enumerate(hsel):
            sc = s_all[n * blk:(n + 1) * blk]
            if masked:
                sc = jnp.where(kpos <= qpos, sc, NEG_BIG)
            acc_ref[hh, rows] += jnp.dot(jnp.exp(sc).astype(BF16), vaug[hh],
                                         preferred_element_type=F32)

    k_max2 = [kn2_ref[h] for h in heads]

    def norm2(s, hh):
        stat = (diag[s] * blk // PROJ_ROWS) * LANES
        return (NORM_SLACK * NORM_SLACK) * qn2_ref[stat + heads[hh]] * k_max2[hh]

    n2 = [[norm2(s, hh) for hh in range(HEAD_PAIR)] for s in streams]
    gap0 = [[FOX_SKIP_NATS + cend_ref[heads[hh] * nblk + jnp.maximum(diag[s] - 1, 0)]
             for hh in range(HEAD_PAIR)] for s in streams]

    def live_head(t, hh, direct):
        keep = False
        for s in streams:
            j = diag[s] - 1 - t
            gap = gap0[s][hh] - cend_ref[heads[hh] * nblk + jnp.maximum(j, 0)]
            bound2 = n2[s][hh] if direct else 4.0 * n2[s][hh]
            dead = jnp.logical_and(gap <= 0.0, bound2 <= gap * gap)
            keep = jnp.logical_or(keep, jnp.logical_and(j >= 0, jnp.logical_not(dead)))
        return keep

    def sweeps(stepper, direct):
        def sweep(back, masked=False, only=None):
            for s in streams:
                stepper(s, diag[s] - back, masked, only)

        def loop(t0, cond, only=None, unroll=1):
            def body(t):
                for u in range(unroll):
                    sweep(t + 1 + u, only=only)
                return t + unroll
            return lax.while_loop(cond, body, t0)

        def live(t, hh):
            return live_head(t, hh, direct)

        def both(t):
            return jnp.logical_and(live(t, 0), live(t, 1))

        sweep(0, masked=True)
        sweep(1)
        if direct:
            t_two = loop(1, lambda t: both(t + 1), unroll=2)
            t_both = loop(t_two, both)
            for hh in range(HEAD_PAIR):
                loop(t_both, lambda t, hh=hh: live(t, hh), only=hh)
        else:
            loop(1, lambda t: jnp.logical_or(live(t, 0), live(t, 1)))

    direct_ok = True
    for s in streams:
        for hh in range(HEAD_PAIR):
            direct_ok = jnp.logical_and(direct_ok, n2[s][hh] <= FOX_DIRECT_NORM2)

    @pl.when(direct_ok)
    def _():
        q_dir = [q_aug(s, shift=True) for s in streams]
        q_void = [q_aug(s, shift=True, void=True) for s in streams]
        sweeps(lambda s, j, masked, only: step_direct(
            s, j, q_dir[s] if masked else jnp.where(j >= 0, q_dir[s], q_void[s]), masked, only),
            direct=True)

    @pl.when(jnp.logical_not(direct_ok))
    def _():
        m_ref[...] = jnp.full_like(m_ref, NEG_BIG)
        sweeps(lambda s, j, masked, only: step_online(
            s, j, q_aug(s, shift=False, void=j < 0), masked), direct=False)

    outs = []
    for hh in range(HEAD_PAIR):
        acc = acc_ref[hh]
        outs.append(acc / pltpu.roll(acc, FOX_DH, axis=1))
    o = jnp.where(lo_lanes, outs[0], outs[1])
    o_ref[...] = (o * gate_ref[...].astype(F32)).astype(BF16)


def _out_kernel(x_ref, gla_ref, fox_ref, mq_ref, mg_ref, mem_ref, mem_g_ref, w_mem_ref,
                w_out_ref, fg_ref, o_ref, wo_ref, mk_ref, mv_ref):
    @pl.when(pl.program_id(0) == 0)
    def _():
        wo_ref[...] = w_out_ref[...].astype(BF16)
        _memkv(mem_ref, mem_g_ref, w_mem_ref, mk_ref, mv_ref)

    lane = lax.broadcasted_iota(jnp.int32, (1, LANES), 1)
    lo_lanes = lane < MEM_DH
    nt = (((1,), (1,)), ((), ()))
    mem_parts = []
    for p in range(MEM_HEADS // HEAD_PAIR):
        ls = slice(p * LANES, (p + 1) * LANES)
        q = mq_ref[:, ls]
        kb = mk_ref[:, ls]
        vb = mv_ref[:, ls]
        zero = jnp.zeros_like(q)
        one = jnp.ones_like(vb)
        qh = (jnp.where(lo_lanes, q, zero), jnp.where(lo_lanes, zero, q))
        vaug = (jnp.where(lo_lanes, vb, one), jnp.where(lo_lanes, one, vb))
        outs = []
        for hh in range(HEAD_PAIR):
            s = lax.dot_general(qh[hh], kb, nt, preferred_element_type=F32)
            pexp = jnp.exp(s - jnp.max(s, axis=1, keepdims=True))
            pv = jnp.dot(pexp.astype(BF16), vaug[hh], preferred_element_type=F32)
            outs.append(pv / pltpu.roll(pv, MEM_DH, axis=1))
        o = jnp.where(lo_lanes, outs[0], outs[1])
        mem_parts.append((o * mg_ref[:, ls].astype(F32)).astype(BF16))
    mixed = jnp.concatenate([gla_ref[...], fox_ref[...]] + mem_parts, axis=1)
    y = x_ref[...] + jnp.dot(mixed, wo_ref[...], preferred_element_type=F32)
    o_ref[...] = y * _rms_scale(y, D_MODEL) * fg_ref[...]


def _params(*sem):
    return pltpu.CompilerParams(dimension_semantics=sem, vmem_limit_bytes=VMEM_LIMIT)


def _layer(x, mem, norm_g, w_in, w_alpha_up, b_alpha, b_forget, gla_norm_g,
           mem_norm_g, w_mem_kv, w_out, out_g):
    w_in_t = jnp.transpose(w_in[None], (0, 2, 1)).reshape(-1, LANES)
    T = x.shape[0]
    M = mem.shape[0]

    def rows(width, n=PROJ_ROWS):
        return pl.BlockSpec((n, width), lambda i: (i, 0))

    def whole(shape):
        return pl.BlockSpec(shape, lambda i: (0,) * len(shape))

    def once(shape):
        return pl.BlockSpec(shape, lambda i: (0,) * len(shape), pipeline_mode=pl.Buffered(1))

    bshape = lambda w: jax.ShapeDtypeStruct((T, w), BF16)
    nproj = T // PROJ_ROWS
    stat_spec = pl.BlockSpec((1, 1, LANES), lambda i: (i, 0, 0))
    stat_shape = jax.ShapeDtypeStruct((nproj, 1, LANES), F32)
    gla_pairs, gla_chunks = GLA_HEADS // HEAD_PAIR, PROJ_ROWS // GLA_CHUNK
    pair_w = HEAD_PAIR * GLA_DV_PAD
    (gla, fqvg, kx, mqg, crow, cq, qn2, kn2) = pl.pallas_call(
        _proj_kernel,
        grid=(nproj,),
        in_specs=[rows(D_MODEL), whole((1, D_MODEL)),
                  pl.BlockSpec(w_in_t.shape, lambda i: (0, 0), pipeline_mode=pl.Buffered(1)),
                  whole(w_alpha_up.shape), whole((1, GLA_HEADS * GLA_DK)),
                  whole((1, FOX_HEADS)), whole((1, GLA_DV))],
        out_specs=[rows(GLA_OUT_W), rows(3 * FOX_W),
                   pl.BlockSpec((FOX_W + LANES, PROJ_ROWS), lambda i: (0, i)),
                   rows(2 * MEM_W),
                   pl.BlockSpec((SUBLANES, PROJ_ROWS), lambda i: (0, i)), rows(LANES),
                   stat_spec, pl.BlockSpec((1, 1, LANES), lambda i: (0, 0, 0))],
        out_shape=[bshape(GLA_OUT_W), bshape(3 * FOX_W),
                   jax.ShapeDtypeStruct((FOX_W + LANES, T), BF16),
                   bshape(2 * MEM_W),
                   jax.ShapeDtypeStruct((SUBLANES, T), F32), bshape(LANES),
                   stat_shape, jax.ShapeDtypeStruct((1, 1, LANES), F32)],
        scratch_shapes=[
            pltpu.VMEM((SUBLANES, LANES), F32),
            pltpu.VMEM((IN_COLS_PAD, D_MODEL), BF16),
            pltpu.VMEM((SMALL_W, GLA_QK_W), F32),
            pltpu.VMEM((1, GLA_QK_W), F32),
            pltpu.VMEM((1, SMALL_W), F32),
            pltpu.VMEM((1, GLA_DV_PAD), F32),
            pltpu.VMEM((FOX_W, LANES), BF16),
            pltpu.VMEM((PROJ_ROWS, GLA_QK_W), BF16),
            pltpu.VMEM((PROJ_ROWS, GLA_QK_W), BF16),
            pltpu.VMEM((PROJ_ROWS, GLA_V_W), BF16),
            pltpu.VMEM((PROJ_ROWS, GLA_V_W), BF16),
            pltpu.VMEM((PROJ_ROWS, GLA_QK_W), F32),
            pltpu.VMEM((gla_pairs, LANES, pair_w), F32),
            pltpu.VMEM((PROJ_ROWS, GLA_V_W), BF16),
            pltpu.VMEM((gla_pairs, gla_chunks, LANES, pair_w), F32),
            pltpu.VMEM((gla_pairs, gla_chunks, LANES, LANES), F32),
            pltpu.VMEM((gla_pairs, gla_chunks, LANES, pair_w), BF16)],
        compiler_params=_params("arbitrary"),
        name="proj",
    )(x, norm_g[None, :], w_in_t, w_alpha_up, b_alpha[None, :], b_forget[None, :],
      gla_norm_g[None, :])

    cend = crow[:FOX_HEADS, FOX_KEYS - 1::FOX_KEYS].reshape(-1)
    fox_pairs = FOX_HEADS // HEAD_PAIR
    pair_rows = pl.BlockSpec((FOX_BLOCK, LANES), lambda p, i, *_: (i, p))
    fox = pl.pallas_call(
        _fox_kernel,
        grid_spec=pltpu.PrefetchScalarGridSpec(
            num_scalar_prefetch=3,
            grid=(fox_pairs, T // FOX_BLOCK),
            in_specs=[pair_rows,
                      pl.BlockSpec((FOX_BLOCK, LANES), lambda p, i, *_: (i, 0)),
                      pl.BlockSpec((LANES, T), lambda p, i, *_: (p, 0)),
                      pl.BlockSpec((LANES, T), lambda p, i, *_: (fox_pairs, 0)),
                      pl.BlockSpec((T, LANES), lambda p, i, *_: (0, fox_pairs + p)),
                      pl.BlockSpec((FOX_BLOCK, LANES),
                                   lambda p, i, *_: (i, 2 * fox_pairs + p))],
            out_specs=pair_rows,
            scratch_shapes=[pltpu.VMEM((HEAD_PAIR, FOX_BLOCK, LANES), F32),
                            pltpu.VMEM((HEAD_PAIR, FOX_BLOCK, LANES), F32)]),
        out_shape=bshape(FOX_W),
        compiler_params=_params("arbitrary", "arbitrary"),
        name="fox",
    )(qn2.reshape(-1), kn2.reshape(-1), cend, fqvg, cq, kx, kx, fqvg, fqvg)

    out = pl.pallas_call(
        _out_kernel,
        grid=(T // OUT_ROWS,),
        in_specs=[rows(D_MODEL, OUT_ROWS), rows(GLA_OUT_W, OUT_ROWS), rows(FOX_W, OUT_ROWS),
                  pl.BlockSpec((OUT_ROWS, MEM_W), lambda i: (i, 0)),
                  pl.BlockSpec((OUT_ROWS, MEM_W), lambda i: (i, 1)),
                  once(mem.shape), once((1, D_MODEL)), once(w_mem_kv.shape), once(w_out.shape),
                  whole((1, D_MODEL))],
        out_specs=rows(D_MODEL, OUT_ROWS),
        out_shape=jax.ShapeDtypeStruct((T, D_MODEL), F32),
        scratch_shapes=[pltpu.VMEM((D_MODEL, D_MODEL), BF16),
                        pltpu.VMEM((M, MEM_W), BF16), pltpu.VMEM((M, MEM_W), BF16)],
        compiler_params=_params("arbitrary"),
        name="out",
    )(x, gla, fox, mqg, mqg, mem, mem_norm_g[None, :], w_mem_kv, w_out, out_g[None, :])
    return out


def kernel(x, mem, norm_g, w_in, w_alpha_up, b_alpha, b_forget, gla_norm_g, mem_norm_g,
           w_mem_kv, w_out, final_norm_g):
    assert x.shape[0] == 1 and mem.shape[0] == 1 and norm_g.shape[0] == 1
    assert x.shape[1] % max(PROJ_ROWS, FOX_BLOCK, OUT_ROWS) == 0
    out = _layer(x[0], mem[0], norm_g[0], w_in[0], w_alpha_up[0], b_alpha[0], b_forget[0],
                 gla_norm_g[0], mem_norm_g[0], w_mem_kv[0], w_out[0], final_norm_g)
    return out[None]
```

```python
import jax
import jax.numpy as jnp
from jax import lax
from jax.experimental import pallas as pl
from jax.experimental.pallas import tpu as pltpu

F32 = jnp.float32
BF16 = jnp.bfloat16

EPS = 1e-6
LANES = 128
SUBLANES = 8

D_MODEL = 1024
GLA_HEADS, GLA_DK, GLA_DV, GLA_RANK = 4, 48, 96, 16
GLA_DK_PAD = 64
GLA_DV_PAD = LANES
GLA_GATE_NORM = 16.0
GLA_CHUNK = 64
FOX_HEADS, FOX_DH = 6, 64
MEM_HEADS, MEM_DH = 4, 64
HEAD_PAIR = 2
GLA_QK_W = GLA_HEADS * GLA_DK_PAD
GLA_V_W = GLA_HEADS * GLA_DV_PAD
GLA_OUT_W = GLA_HEADS * GLA_DV
FOX_W = FOX_HEADS * FOX_DH
MEM_W = MEM_HEADS * MEM_DH
SMALL_W = LANES
FG_LANE0 = 0
LR_LANE0 = SUBLANES

_GROUPS = (("gq", GLA_QK_W), ("gk", GLA_QK_W), ("gv", GLA_OUT_W), ("gg", GLA_OUT_W),
           ("fq", FOX_W), ("fk", FOX_W), ("fv", FOX_W), ("fgate", FOX_W),
           ("mq", MEM_W), ("mg", MEM_W), ("small", SMALL_W))
_OFF = {}
_o = 0
for _n, _w in _GROUPS:
    _OFF[_n] = (_o, _o + _w)
    _o += _w
IN_COLS_PAD = _o

PROJ_ROWS = 1024
FOX_BLOCK = 4096
FOX_KEYS = 256
OUT_ROWS = 1024
VMEM_LIMIT = 56 * 1024 * 1024

NEG_BIG = -1e30
FOX_SKIP_NATS = 105.0
NORM_SLACK = 1.02
FOX_DIRECT_NORM2 = 3600.0
CX_HI, CX_MID, CX_LO, CX_ONE = 0, 8, 16, 24
CX_ROWS = 32


def _log_sigmoid(z):
    return jnp.minimum(z, 0.0) - jnp.log(1.0 + jnp.exp(-jnp.abs(z)))


def _silu(z):
    return z / (1.0 + jnp.exp(-z))


def _rms_scale(v, width):
    return lax.rsqrt(jnp.sum(v * v, axis=-1, keepdims=True) * (1.0 / width) + EPS)


def _w_in_segments():
    qk, gw = GLA_HEADS * GLA_DK, GLA_HEADS * GLA_DV
    src = {}
    o = 0
    for name, width in (("gq", qk), ("gk", qk), ("gv", gw), ("lr", GLA_RANK), ("gg", gw),
                        ("fq", FOX_W), ("fk", FOX_W), ("fv", FOX_W), ("fg", FOX_HEADS),
                        ("fgate", FOX_W), ("mq", MEM_W), ("mg", MEM_W)):
        src[name] = o
        o += width
    segs = []
    for name, d, d_pad in (("gq", GLA_DK, GLA_DK_PAD), ("gk", GLA_DK, GLA_DK_PAD)):
        segs += [(src[name] + h * d, _OFF[name][0] + h * d_pad, d) for h in range(GLA_HEADS)]
    segs += [(src[name], _OFF[name][0], _OFF[name][1] - _OFF[name][0])
             for name in ("gv", "gg", "fq", "fk", "fv", "fgate", "mq", "mg")]
    segs += [(src["fg"], _OFF["small"][0] + FG_LANE0, FOX_HEADS),
             (src["lr"], _OFF["small"][0] + LR_LANE0, GLA_RANK)]
    return tuple(segs)


def _proj_kernel(x_ref, g_ref, w_in_ref, w_alpha_ref, b_alpha_ref, b_forget_ref, gla_g_ref,
                 gla_ref, fqvg_ref, kx_ref, mqg_ref, crow_ref, cq_ref, qn2_ref, kn2_ref,
                 carry_ref, wt_ref, wa_ref, ba_ref, bf_ref, ng_ref, seg_ref,
                 gq_ref, gk_ref, gv_ref, gg_ref, loga_ref,
                 s_ref, lhs_ref, kv_ref, dec_ref, sprev_ref):
    rows = x_ref.shape[0]
    k_chunks = D_MODEL // LANES

    @pl.when(pl.program_id(0) == 0)
    def _():
        carry_ref[...] = jnp.zeros_like(carry_ref)
        s_ref[...] = jnp.zeros_like(s_ref)
        kn2_ref[...] = jnp.zeros_like(kn2_ref)
        gv_ref[...] = jnp.zeros_like(gv_ref)
        gg_ref[...] = jnp.zeros_like(gg_ref)
        wa_ref[...] = jnp.zeros_like(wa_ref)
        ba_ref[...] = jnp.zeros_like(ba_ref)
        bf_ref[...] = jnp.zeros_like(bf_ref)
        ng_ref[...] = jnp.zeros_like(ng_ref)
        for h in range(GLA_HEADS):
            src = slice(h * GLA_DK, (h + 1) * GLA_DK)
            dst = slice(h * GLA_DK_PAD, h * GLA_DK_PAD + GLA_DK)
            wa_ref[LR_LANE0:LR_LANE0 + GLA_RANK, dst] = w_alpha_ref[:, src]
            ba_ref[:, dst] = b_alpha_ref[:, src]
        bf_ref[:, FG_LANE0:FG_LANE0 + FOX_HEADS] = b_forget_ref[...]
        ng_ref[:, 0:GLA_DV] = gla_g_ref[...]
        seg_ref[...] = (lax.broadcasted_iota(jnp.int32, seg_ref.shape, 0) // FOX_DH
                        == lax.broadcasted_iota(jnp.int32, seg_ref.shape, 1)).astype(BF16)
        wt_ref[...] = jnp.zeros_like(wt_ref)
        for s0, d0, width in _w_in_segments():
            for c in range(k_chunks):
                wt_ref[d0:d0 + width, c * LANES:(c + 1) * LANES] = (
                    w_in_ref[pl.ds(s0 * k_chunks + c, width, stride=k_chunks), :].astype(BF16))

    x = x_ref[...]
    xn = (x * _rms_scale(x, D_MODEL) * g_ref[...]).astype(BF16)
    nt = (((1,), (1,)), ((), ()))

    def proj(first, last):
        lo, hi = _OFF[first][0], _OFF[last][1]
        y = lax.dot_general(xn, wt_ref[lo:hi, :], nt, preferred_element_type=F32)
        return lambda name: y[:, _OFF[name][0] - lo:_OFF[name][1] - lo]

    tail = proj("mq", "small")
    gla = proj("gq", "gg")
    small = tail("small")
    logf = _log_sigmoid(small + bf_ref[...])
    c = logf.T[0:SUBLANES, :]
    lane = lax.broadcasted_iota(jnp.int32, c.shape, 1)
    shift = 1
    while shift < rows:
        c = c + jnp.where(lane >= shift, pltpu.roll(c, shift, axis=1), 0.0)
        shift *= 2
    c = c + carry_ref[:, 0:1]
    crow_ref[...] = c
    carry_ref[...] = jnp.broadcast_to(c[:, rows - 1:rows], carry_ref.shape)
    neg = -c
    hi = neg.astype(BF16).astype(F32)
    mid = (neg - hi).astype(BF16).astype(F32)
    low = neg - hi - mid
    parts = jnp.concatenate(
        [hi, mid, low, jnp.ones_like(c), jnp.zeros((LANES - 4 * SUBLANES, rows), F32)], axis=0)
    kx_ref[FOX_W:, :] = parts.astype(BF16)
    cq_ref[...] = parts.T.astype(BF16)

    z = (jnp.dot(small.astype(BF16), wa_ref[...].astype(BF16), preferred_element_type=F32)
         + ba_ref[...])
    loga_ref[...] = _log_sigmoid(z) * (1.0 / GLA_GATE_NORM)
    gq_ref[...] = gla("gq").astype(BF16)
    gk_ref[...] = gla("gk").astype(BF16)
    gv, gg = gla("gv").astype(BF16), _silu(gla("gg")).astype(BF16)
    for h in range(GLA_HEADS):
        src = slice(h * GLA_DV, (h + 1) * GLA_DV)
        dst = slice(h * GLA_DV_PAD, h * GLA_DV_PAD + GLA_DV)
        gv_ref[:, dst] = gv[:, src]
        gg_ref[:, dst] = gg[:, src]
    gla_local, gla_scan, gla_output = _gla_block(
        gq_ref, gk_ref, gv_ref, loga_ref, gg_ref, ng_ref, gla_ref,
        s_ref, lhs_ref, kv_ref, dec_ref, sprev_ref)

    def max_sq_norm(v):
        v32 = v.astype(F32)
        n2 = jnp.dot((v32 * v32).astype(BF16), seg_ref[...], preferred_element_type=F32)
        return jnp.max(n2, axis=0, keepdims=True)

    gla_local()
    fox_qk = proj("fq", "fk")
    mqg_ref[:, :MEM_W] = (tail("mq") * MEM_DH ** -0.5).astype(BF16)
    mqg_ref[:, MEM_W:] = _silu(tail("mg")).astype(BF16)
    gla_scan()
    fq = (fox_qk("fq") * FOX_DH ** -0.5).astype(BF16)
    fk = fox_qk("fk").astype(BF16)
    fqvg_ref[:, :FOX_W] = fq
    kx_ref[:FOX_W, :] = fox_qk("fk").T.astype(BF16)
    gla_output()
    fox_vg = proj("fv", "fgate")
    fqvg_ref[:, FOX_W:2 * FOX_W] = fox_vg("fv").astype(BF16)
    fqvg_ref[:, 2 * FOX_W:] = _silu(fox_vg("fgate")).astype(BF16)
    qn2_ref[0] = max_sq_norm(fq)
    kn2_ref[0] = jnp.maximum(kn2_ref[0], max_sq_norm(fk))


def _memkv(mem_ref, g_ref, w_ref, mk_ref, mv_ref):
    m = mem_ref[...]
    mn = (m * _rms_scale(m, D_MODEL) * g_ref[...]).astype(BF16)
    kv = jnp.dot(mn, w_ref[...].astype(BF16), preferred_element_type=F32)
    mk_ref[...] = kv[:, :MEM_W].astype(BF16)
    mv_ref[...] = kv[:, MEM_W:].astype(BF16)


def _gla_block(q_ref, k_ref, v_ref, loga_ref, gate_ref, ng_ref, o_ref,
               s_ref, lhs_ref, kv_ref, dec_ref, sprev_ref):
    C = GLA_CHUNK
    W = HEAD_PAIR * GLA_DV_PAD
    n_chunks = q_ref.shape[0] // C

    row = lax.broadcasted_iota(jnp.int32, (C, LANES), 0)
    lane = lax.broadcasted_iota(jnp.int32, (C, LANES), 1)
    lo_k = lane < GLA_DK_PAD
    causal = row >= jnp.where(lo_k, lane, lane - GLA_DK_PAD)
    lo_v = lax.broadcasted_iota(jnp.int32, (C, W), 1) < GLA_DV_PAD
    st_row = lax.broadcasted_iota(jnp.int32, (LANES, W), 0)
    st_lane = lax.broadcasted_iota(jnp.int32, (LANES, W), 1)
    own = (st_row < GLA_DK_PAD) == (st_lane < GLA_DV_PAD)
    eye = (lax.broadcasted_iota(jnp.int32, (LANES, LANES), 0)
           == lax.broadcasted_iota(jnp.int32, (LANES, LANES), 1))
    scale = GLA_DK ** -0.5
    nt = (((1,), (1,)), ((), ()))
    tn = (((0,), (0,)), ((), ()))
    ng = jnp.concatenate([ng_ref[...]] * HEAD_PAIR, axis=1)

    pairs = range(GLA_HEADS // HEAD_PAIR)

    def local(ci):
        rs = slice(ci * C, (ci + 1) * C)
        for p in pairs:
            ls = slice(p * LANES, (p + 1) * LANES)
            vs = slice(p * W, (p + 1) * W)
            b = loga_ref[rs, ls]
            shift = 1
            while shift < C:
                b = b + jnp.where(row >= shift, pltpu.roll(b, shift, axis=0), 0.0)
                shift *= 2
            b_last = b[C - 1:C, :]
            k2 = k_ref[rs, ls].astype(F32)
            qd = (q_ref[rs, ls].astype(F32) * scale * jnp.exp(b)).astype(BF16)
            kd = (k2 * jnp.exp(-b)).astype(BF16)
            ke = (k2 * jnp.exp(b_last - b)).astype(BF16)
            zk = jnp.zeros_like(kd)
            kd_blk = jnp.concatenate([jnp.where(lo_k, kd, zk), jnp.where(lo_k, zk, kd)], axis=0)
            attn = lax.dot_general(qd, kd_blk, nt, preferred_element_type=F32)
            lhs_ref[rs, vs] = jnp.concatenate([jnp.where(causal, attn, 0.0).astype(BF16), qd], axis=1)
            kv = lax.dot_general(ke, v_ref[rs, vs], tn, preferred_element_type=F32)
            kv_ref[p, ci] = jnp.where(own, kv, 0.0)
            dcol = jnp.exp(jnp.sum(jnp.where(eye, jnp.broadcast_to(b_last, (LANES, LANES)), 0.0),
                                   axis=1, keepdims=True))
            dec_ref[p, ci] = jnp.broadcast_to(dcol, (LANES, LANES))

    def scan(ci):
        for p in pairs:
            s_prev = s_ref[p]
            sprev_ref[p, ci] = s_prev.astype(BF16)
            s_ref[p] = jnp.tile(dec_ref[p, ci], (1, HEAD_PAIR)) * s_prev + kv_ref[p, ci]

    def output(ci):
        rs = slice(ci * C, (ci + 1) * C)
        for p in pairs:
            vs = slice(p * W, (p + 1) * W)
            v2 = v_ref[rs, vs]
            zv = jnp.zeros_like(v2)
            v_blk = jnp.concatenate([jnp.where(lo_v, v2, zv), jnp.where(lo_v, zv, v2)], axis=0)
            o = jnp.dot(lhs_ref[rs, vs], jnp.concatenate([v_blk, sprev_ref[p, ci]], axis=0),
                        preferred_element_type=F32)
            o2 = o * o
            ms = jnp.where(lo_v, jnp.sum(o2[:, :GLA_DV_PAD], axis=1, keepdims=True),
                           jnp.sum(o2[:, GLA_DV_PAD:], axis=1, keepdims=True))
            on = o * lax.rsqrt(ms * (1.0 / GLA_DV) + EPS) * ng
            og = (on * gate_ref[rs, vs].astype(F32)).astype(BF16)
            for hh in range(HEAD_PAIR):
                c0 = (p * HEAD_PAIR + hh) * GLA_DV
                o_ref[rs, c0:c0 + GLA_DV] = og[:, hh * GLA_DV_PAD:hh * GLA_DV_PAD + GLA_DV]

    def all_chunks(phase):
        return lambda: [phase(ci) for ci in range(n_chunks)]

    return all_chunks(local), all_chunks(scan), all_chunks(output)


def _fox_kernel(qn2_ref, kn2_ref, cend_ref, q_ref, cq_ref, k_ref, cx_ref, v_ref, gate_ref, o_ref,
                m_ref, acc_ref):
    blk = FOX_KEYS
    streams = range(q_ref.shape[0] // blk)
    pair = pl.program_id(0)
    qi = pl.program_id(1)
    nblk = pl.num_programs(1) * len(streams)
    lane = lax.broadcasted_iota(jnp.int32, (1, LANES), 1)
    lo_lanes = lane < FOX_DH
    reps = blk // LANES
    diag = [qi * len(streams) + s for s in streams]
    heads = [pair * HEAD_PAIR + hh for hh in range(HEAD_PAIR)]

    q = q_ref[...]
    zero = jnp.zeros_like(q)
    q_lo, q_hi = jnp.where(lo_lanes, q, zero), jnp.where(lo_lanes, zero, q)
    q_stack = [jnp.concatenate([q_lo[s * blk:(s + 1) * blk], q_hi[s * blk:(s + 1) * blk]], axis=0)
               for s in streams]
    g_row = lax.broadcasted_iota(jnp.int32, (LANES, LANES), 0)
    g_col = lax.broadcasted_iota(jnp.int32, (LANES, LANES), 1)
    xlane = lax.broadcasted_iota(jnp.int32, (HEAD_PAIR * blk, LANES), 1)
    cq = cq_ref[...]

    def lane_map(h, shift):
        g = jnp.where((g_row == CX_ONE) & ((g_col == CX_HI + h) | (g_col == CX_MID + h)
                                           | (g_col == CX_LO + h)), 1.0, 0.0)
        if shift:
            for part, base in enumerate((CX_HI, CX_MID, CX_LO)):
                g = jnp.where((g_row == base + h) & (g_col == CX_ONE + part), -1.0, g)
        return g.astype(BF16)

    def q_aug(s, shift, void=None):
        cs = cq[s * blk:(s + 1) * blk]
        extra = jnp.concatenate([jnp.dot(cs, lane_map(h, shift), preferred_element_type=F32)
                                 for h in heads], axis=0)
        if void is not None:
            extra = jnp.where(jnp.logical_and(xlane == CX_ONE + 3, void), NEG_BIG, extra)
        return jnp.concatenate([q_stack[s], extra[:, :CX_ROWS].astype(BF16)], axis=1)

    acc_ref[...] = jnp.zeros_like(acc_ref)
    qpos = lax.broadcasted_iota(jnp.int32, (blk, blk), 0)
    kpos = lax.broadcasted_iota(jnp.int32, (blk, blk), 1)

    def block(j):
        ks = pl.ds(pl.multiple_of(jnp.maximum(j, 0) * blk, blk), blk)
        k_aug = jnp.concatenate([k_ref[:, ks], cx_ref[0:CX_ROWS, ks]], axis=0)
        vb = v_ref[ks, :]
        one = jnp.ones_like(vb)
        return k_aug, (jnp.where(lo_lanes, vb, one), jnp.where(lo_lanes, one, vb))

    def step_online(s, j, qa, masked):
        rows = slice(s * blk, (s + 1) * blk)
        k_aug, vaug = block(j)
        s_all = jnp.dot(qa, k_aug, preferred_element_type=F32)
        for hh in range(HEAD_PAIR):
            sc = s_all[hh * blk:(hh + 1) * blk]
            if masked:
                sc = jnp.where(kpos <= qpos, sc, NEG_BIG)
            m_prev = m_ref[hh, rows]
            m_new = jnp.maximum(m_prev, jnp.max(sc, axis=1, keepdims=True))
            p = jnp.exp(sc - jnp.tile(m_new, (1, reps)))
            alpha = jnp.exp(m_prev - m_new)
            pv = jnp.dot(p.astype(BF16), vaug[hh], preferred_element_type=F32)
            acc_ref[hh, rows] = alpha * acc_ref[hh, rows] + pv
            m_ref[hh, rows] = m_new

    def step_direct(s, j, qa, masked, only=None):
        rows = slice(s * blk, (s + 1) * blk)
        k_aug, vaug = block(j)
        hsel = range(HEAD_PAIR) if only is None else (only,)
        lhs = qa if only is None else qa[only * blk:(only + 1) * blk]
        s_all = jnp.dot(lhs, k_aug, preferred_element_type=F32)
        for n, hh in enumerate(hsel):
            sc = s_all[n * blk:(n + 1) * blk]
            if masked:
                sc = jnp.where(kpos <= qpos, sc, NEG_BIG)
            acc_ref[hh, rows] += jnp.dot(jnp.exp(sc).astype(BF16), vaug[hh],
                                         preferred_element_type=F32)

    k_max2 = [kn2_ref[h] for h in heads]

    def norm2(s, hh):
        stat = (diag[s] * blk // PROJ_ROWS) * LANES
        return (NORM_SLACK * NORM_SLACK) * qn2_ref[stat + heads[hh]] * k_max2[hh]

    n2 = [[norm2(s, hh) for hh in range(HEAD_PAIR)] for s in streams]
    gap0 = [[FOX_SKIP_NATS + cend_ref[heads[hh] * nblk + jnp.maximum(diag[s] - 1, 0)]
             for hh in range(HEAD_PAIR)] for s in streams]

    def live_head(t, hh, direct):
        keep = False
        for s in streams:
            j = diag[s] - 1 - t
            gap = gap0[s][hh] - cend_ref[heads[hh] * nblk + jnp.maximum(j, 0)]
            bound2 = n2[s][hh] if direct else 4.0 * n2[s][hh]
            dead = jnp.logical_and(gap <= 0.0, bound2 <= gap * gap)
            keep = jnp.logical_or(keep, jnp.logical_and(j >= 0, jnp.logical_not(dead)))
        return keep

    def sweeps(stepper, direct):
        def sweep(back, masked=False, only=None):
            for s in streams:
                stepper(s, diag[s] - back, masked, only)

        def loop(t0, cond, only=None):
            def body(t):
                sweep(t + 1, only=only)
                return t + 1
            return lax.while_loop(cond, body, t0)

        def live(t, hh):
            return live_head(t, hh, direct)

        sweep(0, masked=True)
        sweep(1)
        if direct:
            t_both = loop(1, lambda t: jnp.logical_and(live(t, 0), live(t, 1)))
            for hh in range(HEAD_PAIR):
                loop(t_both, lambda t, hh=hh: live(t, hh), only=hh)
        else:
            loop(1, lambda t: jnp.logical_or(live(t, 0), live(t, 1)))

    direct_ok = True
    for s in streams:
        for hh in range(HEAD_PAIR):
            direct_ok = jnp.logical_and(direct_ok, n2[s][hh] <= FOX_DIRECT_NORM2)

    @pl.when(direct_ok)
    def _():
        q_dir = [q_aug(s, shift=True) for s in streams]
        q_void = [q_aug(s, shift=True, void=True) for s in streams]
        sweeps(lambda s, j, masked, only: step_direct(
            s, j, q_dir[s] if masked else jnp.where(j >= 0, q_dir[s], q_void[s]), masked, only),
            direct=True)

    @pl.when(jnp.logical_not(direct_ok))
    def _():
        m_ref[...] = jnp.full_like(m_ref, NEG_BIG)
        sweeps(lambda s, j, masked, only: step_online(
            s, j, q_aug(s, shift=False, void=j < 0), masked), direct=False)

    outs = []
    for hh in range(HEAD_PAIR):
        acc = acc_ref[hh]
        outs.append(acc / pltpu.roll(acc, FOX_DH, axis=1))
    o = jnp.where(lo_lanes, outs[0], outs[1])
    o_ref[...] = (o * gate_ref[...].astype(F32)).astype(BF16)


def _out_kernel(x_ref, gla_ref, fox_ref, mq_ref, mg_ref, mem_ref, mem_g_ref, w_mem_ref,
                w_out_ref, fg_ref, o_ref, wo_ref, mk_ref, mv_ref):
    @pl.when(pl.program_id(0) == 0)
    def _():
        wo_ref[...] = w_out_ref[...].astype(BF16)
        _memkv(mem_ref, mem_g_ref, w_mem_ref, mk_ref, mv_ref)

    lane = lax.broadcasted_iota(jnp.int32, (1, LANES), 1)
    lo_lanes = lane < MEM_DH
    nt = (((1,), (1,)), ((), ()))
    mem_parts = []
    for p in range(MEM_HEADS // HEAD_PAIR):
        ls = slice(p * LANES, (p + 1) * LANES)
        q = mq_ref[:, ls]
        kb = mk_ref[:, ls]
        vb = mv_ref[:, ls]
        zero = jnp.zeros_like(q)
        one = jnp.ones_like(vb)
        qh = (jnp.where(lo_lanes, q, zero), jnp.where(lo_lanes, zero, q))
        vaug = (jnp.where(lo_lanes, vb, one), jnp.where(lo_lanes, one, vb))
        outs = []
        for hh in range(HEAD_PAIR):
            s = lax.dot_general(qh[hh], kb, nt, preferred_element_type=F32)
            pexp = jnp.exp(s - jnp.max(s, axis=1, keepdims=True))
            pv = jnp.dot(pexp.astype(BF16), vaug[hh], preferred_element_type=F32)
            outs.append(pv / pltpu.roll(pv, MEM_DH, axis=1))
        o = jnp.where(lo_lanes, outs[0], outs[1])
        mem_parts.append((o * mg_ref[:, ls].astype(F32)).astype(BF16))
    mixed = jnp.concatenate([gla_ref[...], fox_ref[...]] + mem_parts, axis=1)
    y = x_ref[...] + jnp.dot(mixed, wo_ref[...], preferred_element_type=F32)
    o_ref[...] = y * _rms_scale(y, D_MODEL) * fg_ref[...]


def _params(*sem):
    return pltpu.CompilerParams(dimension_semantics=sem, vmem_limit_bytes=VMEM_LIMIT)


def _layer(x, mem, norm_g, w_in, w_alpha_up, b_alpha, b_forget, gla_norm_g,
           mem_norm_g, w_mem_kv, w_out, out_g):
    w_in_t = jnp.transpose(w_in[None], (0, 2, 1)).reshape(-1, LANES)
    T = x.shape[0]
    M = mem.shape[0]

    def rows(width, n=PROJ_ROWS):
        return pl.BlockSpec((n, width), lambda i: (i, 0))

    def whole(shape):
        return pl.BlockSpec(shape, lambda i: (0,) * len(shape))

    def once(shape):
        return pl.BlockSpec(shape, lambda i: (0,) * len(shape), pipeline_mode=pl.Buffered(1))

    bshape = lambda w: jax.ShapeDtypeStruct((T, w), BF16)
    nproj = T // PROJ_ROWS
    stat_spec = pl.BlockSpec((1, 1, LANES), lambda i: (i, 0, 0))
    stat_shape = jax.ShapeDtypeStruct((nproj, 1, LANES), F32)
    gla_pairs, gla_chunks = GLA_HEADS // HEAD_PAIR, PROJ_ROWS // GLA_CHUNK
    pair_w = HEAD_PAIR * GLA_DV_PAD
    (gla, fqvg, kx, mqg, crow, cq, qn2, kn2) = pl.pallas_call(
        _proj_kernel,
        grid=(nproj,),
        in_specs=[rows(D_MODEL), whole((1, D_MODEL)),
                  pl.BlockSpec(w_in_t.shape, lambda i: (0, 0), pipeline_mode=pl.Buffered(1)),
                  whole(w_alpha_up.shape), whole((1, GLA_HEADS * GLA_DK)),
                  whole((1, FOX_HEADS)), whole((1, GLA_DV))],
        out_specs=[rows(GLA_OUT_W), rows(3 * FOX_W),
                   pl.BlockSpec((FOX_W + LANES, PROJ_ROWS), lambda i: (0, i)),
                   rows(2 * MEM_W),
                   pl.BlockSpec((SUBLANES, PROJ_ROWS), lambda i: (0, i)), rows(LANES),
                   stat_spec, pl.BlockSpec((1, 1, LANES), lambda i: (0, 0, 0))],
        out_shape=[bshape(GLA_OUT_W), bshape(3 * FOX_W),
                   jax.ShapeDtypeStruct((FOX_W + LANES, T), BF16),
                   bshape(2 * MEM_W),
                   jax.ShapeDtypeStruct((SUBLANES, T), F32), bshape(LANES),
                   stat_shape, jax.ShapeDtypeStruct((1, 1, LANES), F32)],
        scratch_shapes=[
            pltpu.VMEM((SUBLANES, LANES), F32),
            pltpu.VMEM((IN_COLS_PAD, D_MODEL), BF16),
            pltpu.VMEM((SMALL_W, GLA_QK_W), F32),
            pltpu.VMEM((1, GLA_QK_W), F32),
            pltpu.VMEM((1, SMALL_W), F32),
            pltpu.VMEM((1, GLA_DV_PAD), F32),
            pltpu.VMEM((FOX_W, LANES), BF16),
            pltpu.VMEM((PROJ_ROWS, GLA_QK_W), BF16),
            pltpu.VMEM((PROJ_ROWS, GLA_QK_W), BF16),
            pltpu.VMEM((PROJ_ROWS, GLA_V_W), BF16),
            pltpu.VMEM((PROJ_ROWS, GLA_V_W), BF16),
            pltpu.VMEM((PROJ_ROWS, GLA_QK_W), F32),
            pltpu.VMEM((gla_pairs, LANES, pair_w), F32),
            pltpu.VMEM((PROJ_ROWS, GLA_V_W), BF16),
            pltpu.VMEM((gla_pairs, gla_chunks, LANES, pair_w), F32),
            pltpu.VMEM((gla_pairs, gla_chunks, LANES, LANES), F32),
            pltpu.VMEM((gla_pairs, gla_chunks, LANES, pair_w), BF16)],
        compiler_params=_params("arbitrary"),
        name="proj",
    )(x, norm_g[None, :], w_in_t, w_alpha_up, b_alpha[None, :], b_forget[None, :],
      gla_norm_g[None, :])

    cend = crow[:FOX_HEADS, FOX_KEYS - 1::FOX_KEYS].reshape(-1)
    fox_pairs = FOX_HEADS // HEAD_PAIR
    pair_rows = pl.BlockSpec((FOX_BLOCK, LANES), lambda p, i, *_: (i, p))
    fox = pl.pallas_call(
        _fox_kernel,
        grid_spec=pltpu.PrefetchScalarGridSpec(
            num_scalar_prefetch=3,
            grid=(fox_pairs, T // FOX_BLOCK),
            in_specs=[pair_rows,
                      pl.BlockSpec((FOX_BLOCK, LANES), lambda p, i, *_: (i, 0)),
                      pl.BlockSpec((LANES, T), lambda p, i, *_: (p, 0)),
                      pl.BlockSpec((LANES, T), lambda p, i, *_: (fox_pairs, 0)),
                      pl.BlockSpec((T, LANES), lambda p, i, *_: (0, fox_pairs + p)),
                      pl.BlockSpec((FOX_BLOCK, LANES),
                                   lambda p, i, *_: (i, 2 * fox_pairs + p))],
            out_specs=pair_rows,
            scratch_shapes=[pltpu.VMEM((HEAD_PAIR, FOX_BLOCK, LANES), F32),
                            pltpu.VMEM((HEAD_PAIR, FOX_BLOCK, LANES), F32)]),
        out_shape=bshape(FOX_W),
        compiler_params=_params("arbitrary", "arbitrary"),
        name="fox",
    )(qn2.reshape(-1), kn2.reshape(-1), cend, fqvg, cq, kx, kx, fqvg, fqvg)

    out = pl.pallas_call(
        _out_kernel,
        grid=(T // OUT_ROWS,),
        in_specs=[rows(D_MODEL, OUT_ROWS), rows(GLA_OUT_W, OUT_ROWS), rows(FOX_W, OUT_ROWS),
                  pl.BlockSpec((OUT_ROWS, MEM_W), lambda i: (i, 0)),
                  pl.BlockSpec((OUT_ROWS, MEM_W), lambda i: (i, 1)),
                  once(mem.shape), once((1, D_MODEL)), once(w_mem_kv.shape), once(w_out.shape),
                  whole((1, D_MODEL))],
        out_specs=rows(D_MODEL, OUT_ROWS),
        out_shape=jax.ShapeDtypeStruct((T, D_MODEL), F32),
        scratch_shapes=[pltpu.VMEM((D_MODEL, D_MODEL), BF16),
                        pltpu.VMEM((M, MEM_W), BF16), pltpu.VMEM((M, MEM_W), BF16)],
        compiler_params=_params("arbitrary"),
        name="out",
    )(x, gla, fox, mqg, mqg, mem, mem_norm_g[None, :], w_mem_kv, w_out, out_g[None, :])
    return out


def kernel(x, mem, norm_g, w_in, w_alpha_up, b_alpha, b_forget, gla_norm_g, mem_norm_g,
           w_mem_kv, w_out, final_norm_g):
    assert x.shape[0] == 1 and mem.shape[0] == 1 and norm_g.shape[0] == 1
    assert x.shape[1] % max(PROJ_ROWS, FOX_BLOCK, OUT_ROWS) == 0
    out = _layer(x[0], mem[0], norm_g[0], w_in[0], w_alpha_up[0], b_alpha[0], b_forget[0],
                 gla_norm_g[0], mem_norm_g[0], w_mem_kv[0], w_out[0], final_norm_g)
    return out[None]
```

```python
import jax
import jax.numpy as jnp
from jax import lax
from jax.experimental import pallas as pl
from jax.experimental.pallas import tpu as pltpu

F32 = jnp.float32
BF16 = jnp.bfloat16

EPS = 1e-6
LANES = 128
SUBLANES = 8

D_MODEL = 1024
GLA_HEADS, GLA_DK, GLA_DV, GLA_RANK = 4, 48, 96, 16
GLA_DK_PAD = 64
GLA_DV_PAD = LANES
GLA_GATE_NORM = 16.0
GLA_CHUNK = 64
FOX_HEADS, FOX_DH = 6, 64
MEM_HEADS, MEM_DH = 4, 64
HEAD_PAIR = 2
GLA_QK_W = GLA_HEADS * GLA_DK_PAD
GLA_V_W = GLA_HEADS * GLA_DV_PAD
GLA_OUT_W = GLA_HEADS * GLA_DV
FOX_W = FOX_HEADS * FOX_DH
MEM_W = MEM_HEADS * MEM_DH
SMALL_W = LANES
FG_LANE0 = 0
LR_LANE0 = SUBLANES

_GROUPS = (("gq", GLA_QK_W), ("gk", GLA_QK_W), ("gv", GLA_OUT_W), ("gg", GLA_OUT_W),
           ("fq", FOX_W), ("fk", FOX_W), ("fv", FOX_W), ("fgate", FOX_W),
           ("mq", MEM_W), ("mg", MEM_W))
FG_COL = GLA_DK
LR_COL = GLA_DK_PAD + GLA_DK
_OFF = {}
_o = 0
for _n, _w in _GROUPS:
    _OFF[_n] = (_o, _o + _w)
    _o += _w
IN_COLS_PAD = _o

PROJ_ROWS = 1024
FOX_BLOCK = 4096
FOX_KEYS = 256
OUT_ROWS = 1024
VMEM_LIMIT = 56 * 1024 * 1024

NEG_BIG = -1e30
FOX_SKIP_NATS = 105.0
NORM_SLACK = 1.02
FOX_DIRECT_NORM2 = 3600.0
CX_HI, CX_MID, CX_LO, CX_ONE = 0, 8, 16, 24


def _log_sigmoid(z):
    return jnp.minimum(z, 0.0) - jnp.log(1.0 + jnp.exp(-jnp.abs(z)))


def _silu(z):
    return z / (1.0 + jnp.exp(-z))


def _rms_scale(v, width):
    return lax.rsqrt(jnp.sum(v * v, axis=-1, keepdims=True) * (1.0 / width) + EPS)


def _w_in_segments():
    qk, gw = GLA_HEADS * GLA_DK, GLA_HEADS * GLA_DV
    src = {}
    o = 0
    for name, width in (("gq", qk), ("gk", qk), ("gv", gw), ("lr", GLA_RANK), ("gg", gw),
                        ("fq", FOX_W), ("fk", FOX_W), ("fv", FOX_W), ("fg", FOX_HEADS),
                        ("fgate", FOX_W), ("mq", MEM_W), ("mg", MEM_W)):
        src[name] = o
        o += width
    segs = []
    for name, d, d_pad in (("gq", GLA_DK, GLA_DK_PAD), ("gk", GLA_DK, GLA_DK_PAD)):
        segs += [(src[name] + h * d, _OFF[name][0] + h * d_pad, d) for h in range(GLA_HEADS)]
    segs += [(src[name], _OFF[name][0], _OFF[name][1] - _OFF[name][0])
             for name in ("gv", "gg", "fq", "fk", "fv", "fgate", "mq", "mg")]
    segs += [(src["fg"], _OFF["gq"][0] + FG_COL, FOX_HEADS),
             (src["lr"], _OFF["gq"][0] + LR_COL, GLA_RANK)]
    return tuple(segs)


def _proj_kernel(x_ref, g_ref, w_in_ref, w_alpha_ref, b_alpha_ref, b_forget_ref, gla_g_ref,
                 gla_ref, fqvg_ref, kx_ref, mqg_ref, crow_ref, cq_ref, qn2_ref, kn2_ref,
                 carry_ref, wt_ref, wa_ref, ba_ref, bf_ref, ng_ref, seg_ref, small_ref,
                 gq_ref, gk_ref, gv_ref, gg_ref, loga_ref,
                 s_ref, lhs_ref, kv_ref, dec_ref, sprev_ref):
    rows = x_ref.shape[0]
    k_chunks = D_MODEL // LANES

    @pl.when(pl.program_id(0) == 0)
    def _():
        carry_ref[...] = jnp.zeros_like(carry_ref)
        s_ref[...] = jnp.zeros_like(s_ref)
        kn2_ref[...] = jnp.zeros_like(kn2_ref)
        gv_ref[...] = jnp.zeros_like(gv_ref)
        gg_ref[...] = jnp.zeros_like(gg_ref)
        small_ref[...] = jnp.zeros_like(small_ref)
        wa_ref[...] = jnp.zeros_like(wa_ref)
        ba_ref[...] = jnp.zeros_like(ba_ref)
        bf_ref[...] = jnp.zeros_like(bf_ref)
        ng_ref[...] = jnp.zeros_like(ng_ref)
        for h in range(GLA_HEADS):
            src = slice(h * GLA_DK, (h + 1) * GLA_DK)
            dst = slice(h * GLA_DK_PAD, h * GLA_DK_PAD + GLA_DK)
            wa_ref[LR_LANE0:LR_LANE0 + GLA_RANK, dst] = w_alpha_ref[:, src]
            ba_ref[:, dst] = b_alpha_ref[:, src]
        bf_ref[:, FG_LANE0:FG_LANE0 + FOX_HEADS] = b_forget_ref[...]
        ng_ref[:, 0:GLA_DV] = gla_g_ref[...]
        seg_ref[...] = (lax.broadcasted_iota(jnp.int32, seg_ref.shape, 0) // FOX_DH
                        == lax.broadcasted_iota(jnp.int32, seg_ref.shape, 1)).astype(BF16)
        wt_ref[...] = jnp.zeros_like(wt_ref)
        for s0, d0, width in _w_in_segments():
            for c in range(k_chunks):
                wt_ref[d0:d0 + width, c * LANES:(c + 1) * LANES] = (
                    w_in_ref[pl.ds(s0 * k_chunks + c, width, stride=k_chunks), :].astype(BF16))

    x = x_ref[...]
    xn = (x * _rms_scale(x, D_MODEL) * g_ref[...]).astype(BF16)
    nt = (((1,), (1,)), ((), ()))

    def proj(first, last):
        lo, hi = _OFF[first][0], _OFF[last][1]
        y = lax.dot_general(xn, wt_ref[lo:hi, :], nt, preferred_element_type=F32)
        return lambda name: y[:, _OFF[name][0] - lo:_OFF[name][1] - lo]

    gla = proj("gq", "gg")
    tail = proj("mq", "mg")
    gq_all = gla("gq")
    small_ref[:, FG_LANE0:FG_LANE0 + SUBLANES] = gq_all[:, FG_COL:FG_COL + SUBLANES]
    small_ref[:, LR_LANE0:LR_LANE0 + GLA_RANK] = gq_all[:, LR_COL:LR_COL + GLA_RANK]
    small = small_ref[...]
    logf = _log_sigmoid(small + bf_ref[...])
    c = logf.T[0:SUBLANES, :]
    lane = lax.broadcasted_iota(jnp.int32, c.shape, 1)
    shift = 1
    while shift < rows:
        c = c + jnp.where(lane >= shift, pltpu.roll(c, shift, axis=1), 0.0)
        shift *= 2
    c = c + carry_ref[:, 0:1]
    crow_ref[...] = c
    carry_ref[...] = jnp.broadcast_to(c[:, rows - 1:rows], carry_ref.shape)
    neg = -c
    hi = neg.astype(BF16).astype(F32)
    mid = (neg - hi).astype(BF16).astype(F32)
    low = neg - hi - mid
    parts = jnp.concatenate(
        [hi, mid, low, jnp.ones_like(c), jnp.zeros((LANES - 4 * SUBLANES, rows), F32)], axis=0)
    kx_ref[FOX_W:, :] = parts.astype(BF16)
    cq_ref[...] = parts.T.astype(BF16)

    z = (jnp.dot(small.astype(BF16), wa_ref[...].astype(BF16), preferred_element_type=F32)
         + ba_ref[...])
    loga_ref[...] = _log_sigmoid(z) * (1.0 / GLA_GATE_NORM)
    gq_ref[...] = gla("gq").astype(BF16)
    gk_ref[...] = gla("gk").astype(BF16)
    gv, gg = gla("gv").astype(BF16), _silu(gla("gg")).astype(BF16)
    for h in range(GLA_HEADS):
        src = slice(h * GLA_DV, (h + 1) * GLA_DV)
        dst = slice(h * GLA_DV_PAD, h * GLA_DV_PAD + GLA_DV)
        gv_ref[:, dst] = gv[:, src]
        gg_ref[:, dst] = gg[:, src]
    gla_local, gla_scan, gla_output = _gla_block(
        gq_ref, gk_ref, gv_ref, loga_ref, gg_ref, ng_ref, gla_ref,
        s_ref, lhs_ref, kv_ref, dec_ref, sprev_ref)

    def max_sq_norm(v):
        v32 = v.astype(F32)
        n2 = jnp.dot((v32 * v32).astype(BF16), seg_ref[...], preferred_element_type=F32)
        return jnp.max(n2, axis=0, keepdims=True)

    gla_local()
    fox_qk = proj("fq", "fk")
    mqg_ref[:, :MEM_W] = (tail("mq") * MEM_DH ** -0.5).astype(BF16)
    mqg_ref[:, MEM_W:] = _silu(tail("mg")).astype(BF16)
    gla_scan()
    fq = (fox_qk("fq") * FOX_DH ** -0.5).astype(BF16)
    fk = fox_qk("fk").astype(BF16)
    fqvg_ref[:, :FOX_W] = fq
    kx_ref[:FOX_W, :] = fox_qk("fk").T.astype(BF16)
    gla_output()
    fox_vg = proj("fv", "fgate")
    fqvg_ref[:, FOX_W:2 * FOX_W] = fox_vg("fv").astype(BF16)
    fqvg_ref[:, 2 * FOX_W:] = _silu(fox_vg("fgate")).astype(BF16)
    qn2_ref[0] = max_sq_norm(fq)
    kn2_ref[0] = jnp.maximum(kn2_ref[0], max_sq_norm(fk))


def _memkv(mem_ref, g_ref, w_ref, mk_ref, mv_ref):
    m = mem_ref[...]
    mn = (m * _rms_scale(m, D_MODEL) * g_ref[...]).astype(BF16)
    kv = jnp.dot(mn, w_ref[...].astype(BF16), preferred_element_type=F32)
    mk_ref[...] = kv[:, :MEM_W].astype(BF16)
    mv_ref[...] = kv[:, MEM_W:].astype(BF16)


def _gla_block(q_ref, k_ref, v_ref, loga_ref, gate_ref, ng_ref, o_ref,
               s_ref, lhs_ref, kv_ref, dec_ref, sprev_ref):
    C = GLA_CHUNK
    W = HEAD_PAIR * GLA_DV_PAD
    n_chunks = q_ref.shape[0] // C

    row = lax.broadcasted_iota(jnp.int32, (C, LANES), 0)
    lane = lax.broadcasted_iota(jnp.int32, (C, LANES), 1)
    lo_k = lane < GLA_DK_PAD
    causal = row >= jnp.where(lo_k, lane, lane - GLA_DK_PAD)
    lo_v = lax.broadcasted_iota(jnp.int32, (C, W), 1) < GLA_DV_PAD
    st_row = lax.broadcasted_iota(jnp.int32, (LANES, W), 0)
    st_lane = lax.broadcasted_iota(jnp.int32, (LANES, W), 1)
    own = (st_row < GLA_DK_PAD) == (st_lane < GLA_DV_PAD)
    eye = (lax.broadcasted_iota(jnp.int32, (LANES, LANES), 0)
           == lax.broadcasted_iota(jnp.int32, (LANES, LANES), 1))
    scale = GLA_DK ** -0.5
    nt = (((1,), (1,)), ((), ()))
    tn = (((0,), (0,)), ((), ()))
    ng = jnp.concatenate([ng_ref[...]] * HEAD_PAIR, axis=1)

    pairs = range(GLA_HEADS // HEAD_PAIR)

    def local(ci):
        rs = slice(ci * C, (ci + 1) * C)
        for p in pairs:
            ls = slice(p * LANES, (p + 1) * LANES)
            vs = slice(p * W, (p + 1) * W)
            b = loga_ref[rs, ls]
            shift = 1
            while shift < C:
                b = b + jnp.where(row >= shift, pltpu.roll(b, shift, axis=0), 0.0)
                shift *= 2
            b_last = b[C - 1:C, :]
            k2 = k_ref[rs, ls].astype(F32)
            qd = (q_ref[rs, ls].astype(F32) * scale * jnp.exp(b)).astype(BF16)
            kd = (k2 * jnp.exp(-b)).astype(BF16)
            ke = (k2 * jnp.exp(b_last - b)).astype(BF16)
            zk = jnp.zeros_like(kd)
            kd_blk = jnp.concatenate([jnp.where(lo_k, kd, zk), jnp.where(lo_k, zk, kd)], axis=0)
            attn = lax.dot_general(qd, kd_blk, nt, preferred_element_type=F32)
            lhs_ref[rs, vs] = jnp.concatenate([jnp.where(causal, attn, 0.0).astype(BF16), qd], axis=1)
            kv = lax.dot_general(ke, v_ref[rs, vs], tn, preferred_element_type=F32)
            kv_ref[p, ci] = jnp.where(own, kv, 0.0)
            dcol = jnp.exp(jnp.sum(jnp.where(eye, jnp.broadcast_to(b_last, (LANES, LANES)), 0.0),
                                   axis=1, keepdims=True))
            dec_ref[p, ci] = jnp.broadcast_to(dcol, (LANES, LANES))

    def scan(ci):
        for p in pairs:
            s_prev = s_ref[p]
            sprev_ref[p, ci] = s_prev.astype(BF16)
            s_ref[p] = jnp.tile(dec_ref[p, ci], (1, HEAD_PAIR)) * s_prev + kv_ref[p, ci]

    def output(ci):
        rs = slice(ci * C, (ci + 1) * C)
        for p in pairs:
            vs = slice(p * W, (p + 1) * W)
            v2 = v_ref[rs, vs]
            zv = jnp.zeros_like(v2)
            v_blk = jnp.concatenate([jnp.where(lo_v, v2, zv), jnp.where(lo_v, zv, v2)], axis=0)
            o = jnp.dot(lhs_ref[rs, vs], jnp.concatenate([v_blk, sprev_ref[p, ci]], axis=0),
                        preferred_element_type=F32)
            o2 = o * o
            ms = jnp.where(lo_v, jnp.sum(o2[:, :GLA_DV_PAD], axis=1, keepdims=True),
                           jnp.sum(o2[:, GLA_DV_PAD:], axis=1, keepdims=True))
            on = o * lax.rsqrt(ms * (1.0 / GLA_DV) + EPS) * ng
            og = (on * gate_ref[rs, vs].astype(F32)).astype(BF16)
            for hh in range(HEAD_PAIR):
                c0 = (p * HEAD_PAIR + hh) * GLA_DV
                o_ref[rs, c0:c0 + GLA_DV] = og[:, hh * GLA_DV_PAD:hh * GLA_DV_PAD + GLA_DV]

    def all_chunks(phase):
        return lambda: [phase(ci) for ci in range(n_chunks)]

    return all_chunks(local), all_chunks(scan), all_chunks(output)


def _fox_kernel(qn2_ref, kn2_ref, cend_ref, q_ref, cq_ref, k_ref, cx_ref, v_ref, gate_ref, o_ref,
                m_ref, acc_ref):
    blk = FOX_KEYS
    streams = range(q_ref.shape[0] // blk)
    pair = pl.program_id(0)
    qi = pl.program_id(1)
    nblk = pl.num_programs(1) * len(streams)
    lane = lax.broadcasted_iota(jnp.int32, (1, LANES), 1)
    lo_lanes = lane < FOX_DH
    reps = blk // LANES
    diag = [qi * len(streams) + s for s in streams]
    heads = [pair * HEAD_PAIR + hh for hh in range(HEAD_PAIR)]

    q = q_ref[...]
    zero = jnp.zeros_like(q)
    q_lo, q_hi = jnp.where(lo_lanes, q, zero), jnp.where(lo_lanes, zero, q)
    q_stack = [jnp.concatenate([q_lo[s * blk:(s + 1) * blk], q_hi[s * blk:(s + 1) * blk]], axis=0)
               for s in streams]
    g_row = lax.broadcasted_iota(jnp.int32, (LANES, LANES), 0)
    g_col = lax.broadcasted_iota(jnp.int32, (LANES, LANES), 1)
    xlane = lax.broadcasted_iota(jnp.int32, (HEAD_PAIR * blk, LANES), 1)
    cq = cq_ref[...]

    def lane_map(h, shift):
        g = jnp.where((g_row == CX_ONE) & ((g_col == CX_HI + h) | (g_col == CX_MID + h)
                                           | (g_col == CX_LO + h)), 1.0, 0.0)
        if shift:
            for part, base in enumerate((CX_HI, CX_MID, CX_LO)):
                g = jnp.where((g_row == base + h) & (g_col == CX_ONE + part), -1.0, g)
        return g.astype(BF16)

    def q_aug(s, shift, void=None):
        cs = cq[s * blk:(s + 1) * blk]
        extra = jnp.concatenate([jnp.dot(cs, lane_map(h, shift), preferred_element_type=F32)
                                 for h in heads], axis=0)
        if void is not None:
            extra = jnp.where(jnp.logical_and(xlane == CX_ONE + 3, void), NEG_BIG, extra)
        return jnp.concatenate([q_stack[s], extra.astype(BF16)], axis=1)

    acc_ref[...] = jnp.zeros_like(acc_ref)
    qpos = lax.broadcasted_iota(jnp.int32, (blk, blk), 0)
    kpos = lax.broadcasted_iota(jnp.int32, (blk, blk), 1)

    def block(j):
        ks = pl.ds(pl.multiple_of(jnp.maximum(j, 0) * blk, blk), blk)
        k_aug = jnp.concatenate([k_ref[:, ks], cx_ref[:, ks]], axis=0)
        vb = v_ref[ks, :]
        one = jnp.ones_like(vb)
        return k_aug, (jnp.where(lo_lanes, vb, one), jnp.where(lo_lanes, one, vb))

    def step_online(s, j, qa, masked):
        rows = slice(s * blk, (s + 1) * blk)
        k_aug, vaug = block(j)
        s_all = jnp.dot(qa, k_aug, preferred_element_type=F32)
        for hh in range(HEAD_PAIR):
            sc = s_all[hh * blk:(hh + 1) * blk]
            if masked:
                sc = jnp.where(kpos <= qpos, sc, NEG_BIG)
            m_prev = m_ref[hh, rows]
            m_new = jnp.maximum(m_prev, jnp.max(sc, axis=1, keepdims=True))
            p = jnp.exp(sc - jnp.tile(m_new, (1, reps)))
            alpha = jnp.exp(m_prev - m_new)
            pv = jnp.dot(p.astype(BF16), vaug[hh], preferred_element_type=F32)
            acc_ref[hh, rows] = alpha * acc_ref[hh, rows] + pv
            m_ref[hh, rows] = m_new

    def step_direct(s, j, qa, masked, only=None):
        rows = slice(s * blk, (s + 1) * blk)
        k_aug, vaug = block(j)
        hsel = range(HEAD_PAIR) if only is None else (only,)
        lhs = qa if only is None else qa[only * blk:(only + 1) * blk]
        s_all = jnp.dot(lhs, k_aug, preferred_element_type=F32)
        for n, hh in enumerate(hsel):
            sc = s_all[n * blk:(n + 1) * blk]
            if masked:
                sc = jnp.where(kpos <= qpos, sc, NEG_BIG)
            acc_ref[hh, rows] += jnp.dot(jnp.exp(sc).astype(BF16), vaug[hh],
                                         preferred_element_type=F32)

    k_max2 = [kn2_ref[h] for h in heads]

    def norm2(s, hh):
        stat = (diag[s] * blk // PROJ_ROWS) * LANES
        return (NORM_SLACK * NORM_SLACK) * qn2_ref[stat + heads[hh]] * k_max2[hh]

    n2 = [[norm2(s, hh) for hh in range(HEAD_PAIR)] for s in streams]
    gap0 = [[FOX_SKIP_NATS + cend_ref[heads[hh] * nblk + jnp.maximum(diag[s] - 1, 0)]
             for hh in range(HEAD_PAIR)] for s in streams]

    def live_head(t, hh, direct):
        keep = False
        for s in streams:
            j = diag[s] - 1 - t
            gap = gap0[s][hh] - cend_ref[heads[hh] * nblk + jnp.maximum(j, 0)]
            bound2 = n2[s][hh] if direct else 4.0 * n2[s][hh]
            dead = jnp.logical_and(gap <= 0.0, bound2 <= gap * gap)
            keep = jnp.logical_or(keep, jnp.logical_and(j >= 0, jnp.logical_not(dead)))
        return keep

    def sweeps(stepper, direct):
        def sweep(back, masked=False, only=None):
            for s in streams:
                stepper(s, diag[s] - back, masked, only)

        def loop(t0, cond, only=None):
            def body(t):
                sweep(t + 1, only=only)
                return t + 1
            return lax.while_loop(cond, body, t0)

        def live(t, hh):
            return live_head(t, hh, direct)

        sweep(0, masked=True)
        sweep(1)
        if direct:
            t_both = loop(1, lambda t: jnp.logical_and(live(t, 0), live(t, 1)))
            for hh in range(HEAD_PAIR):
                loop(t_both, lambda t, hh=hh: live(t, hh), only=hh)
        else:
            loop(1, lambda t: jnp.logical_or(live(t, 0), live(t, 1)))

    direct_ok = True
    for s in streams:
        for hh in range(HEAD_PAIR):
            direct_ok = jnp.logical_and(direct_ok, n2[s][hh] <= FOX_DIRECT_NORM2)

    @pl.when(direct_ok)
    def _():
        q_dir = [q_aug(s, shift=True) for s in streams]
        q_void = [q_aug(s, shift=True, void=True) for s in streams]
        sweeps(lambda s, j, masked, only: step_direct(
            s, j, q_dir[s] if masked else jnp.where(j >= 0, q_dir[s], q_void[s]), masked, only),
            direct=True)

    @pl.when(jnp.logical_not(direct_ok))
    def _():
        m_ref[...] = jnp.full_like(m_ref, NEG_BIG)
        sweeps(lambda s, j, masked, only: step_online(
            s, j, q_aug(s, shift=False, void=j < 0), masked), direct=False)

    outs = []
    for hh in range(HEAD_PAIR):
        acc = acc_ref[hh]
        outs.append(acc / pltpu.roll(acc, FOX_DH, axis=1))
    o = jnp.where(lo_lanes, outs[0], outs[1])
    o_ref[...] = (o * gate_ref[...].astype(F32)).astype(BF16)


def _out_kernel(x_ref, gla_ref, fox_ref, mq_ref, mg_ref, mem_ref, mem_g_ref, w_mem_ref,
                w_out_ref, fg_ref, o_ref, wo_ref, mk_ref, mv_ref):
    @pl.when(pl.program_id(0) == 0)
    def _():
        wo_ref[...] = w_out_ref[...].astype(BF16)
        _memkv(mem_ref, mem_g_ref, w_mem_ref, mk_ref, mv_ref)

    lane = lax.broadcasted_iota(jnp.int32, (1, LANES), 1)
    lo_lanes = lane < MEM_DH
    nt = (((1,), (1,)), ((), ()))
    mem_parts = []
    for p in range(MEM_HEADS // HEAD_PAIR):
        ls = slice(p * LANES, (p + 1) * LANES)
        q = mq_ref[:, ls]
        kb = mk_ref[:, ls]
        vb = mv_ref[:, ls]
        zero = jnp.zeros_like(q)
        one = jnp.ones_like(vb)
        qh = (jnp.where(lo_lanes, q, zero), jnp.where(lo_lanes, zero, q))
        vaug = (jnp.where(lo_lanes, vb, one), jnp.where(lo_lanes, one, vb))
        outs = []
        for hh in range(HEAD_PAIR):
            s = lax.dot_general(qh[hh], kb, nt, preferred_element_type=F32)
            pexp = jnp.exp(s - jnp.max(s, axis=1, keepdims=True))
            pv = jnp.dot(pexp.astype(BF16), vaug[hh], preferred_element_type=F32)
            outs.append(pv / pltpu.roll(pv, MEM_DH, axis=1))
        o = jnp.where(lo_lanes, outs[0], outs[1])
        mem_parts.append((o * mg_ref[:, ls].astype(F32)).astype(BF16))
    mixed = jnp.concatenate([gla_ref[...], fox_ref[...]] + mem_parts, axis=1)
    y = x_ref[...] + jnp.dot(mixed, wo_ref[...], preferred_element_type=F32)
    o_ref[...] = y * _rms_scale(y, D_MODEL) * fg_ref[...]


def _params(*sem):
    return pltpu.CompilerParams(dimension_semantics=sem, vmem_limit_bytes=VMEM_LIMIT)


def _layer(x, mem, norm_g, w_in, w_alpha_up, b_alpha, b_forget, gla_norm_g,
           mem_norm_g, w_mem_kv, w_out, out_g):
    w_in_t = jnp.transpose(w_in[None], (0, 2, 1)).reshape(-1, LANES)
    T = x.shape[0]
    M = mem.shape[0]

    def rows(width, n=PROJ_ROWS):
        return pl.BlockSpec((n, width), lambda i: (i, 0))

    def whole(shape):
        return pl.BlockSpec(shape, lambda i: (0,) * len(shape))

    def once(shape):
        return pl.BlockSpec(shape, lambda i: (0,) * len(shape), pipeline_mode=pl.Buffered(1))

    bshape = lambda w: jax.ShapeDtypeStruct((T, w), BF16)
    nproj = T // PROJ_ROWS
    stat_spec = pl.BlockSpec((1, 1, LANES), lambda i: (i, 0, 0))
    stat_shape = jax.ShapeDtypeStruct((nproj, 1, LANES), F32)
    gla_pairs, gla_chunks = GLA_HEADS // HEAD_PAIR, PROJ_ROWS // GLA_CHUNK
    pair_w = HEAD_PAIR * GLA_DV_PAD
    (gla, fqvg, kx, mqg, crow, cq, qn2, kn2) = pl.pallas_call(
        _proj_kernel,
        grid=(nproj,),
        in_specs=[rows(D_MODEL), whole((1, D_MODEL)),
                  pl.BlockSpec(w_in_t.shape, lambda i: (0, 0), pipeline_mode=pl.Buffered(1)),
                  whole(w_alpha_up.shape), whole((1, GLA_HEADS * GLA_DK)),
                  whole((1, FOX_HEADS)), whole((1, GLA_DV))],
        out_specs=[rows(GLA_OUT_W), rows(3 * FOX_W),
                   pl.BlockSpec((FOX_W + LANES, PROJ_ROWS), lambda i: (0, i)),
                   rows(2 * MEM_W),
                   pl.BlockSpec((SUBLANES, PROJ_ROWS), lambda i: (0, i)), rows(LANES),
                   stat_spec, pl.BlockSpec((1, 1, LANES), lambda i: (0, 0, 0))],
        out_shape=[bshape(GLA_OUT_W), bshape(3 * FOX_W),
                   jax.ShapeDtypeStruct((FOX_W + LANES, T), BF16),
                   bshape(2 * MEM_W),
                   jax.ShapeDtypeStruct((SUBLANES, T), F32), bshape(LANES),
                   stat_shape, jax.ShapeDtypeStruct((1, 1, LANES), F32)],
        scratch_shapes=[
            pltpu.VMEM((SUBLANES, LANES), F32),
            pltpu.VMEM((IN_COLS_PAD, D_MODEL), BF16),
            pltpu.VMEM((SMALL_W, GLA_QK_W), F32),
            pltpu.VMEM((1, GLA_QK_W), F32),
            pltpu.VMEM((1, SMALL_W), F32),
            pltpu.VMEM((1, GLA_DV_PAD), F32),
            pltpu.VMEM((FOX_W, LANES), BF16),
            pltpu.VMEM((PROJ_ROWS, SMALL_W), F32),
            pltpu.VMEM((PROJ_ROWS, GLA_QK_W), BF16),
            pltpu.VMEM((PROJ_ROWS, GLA_QK_W), BF16),
            pltpu.VMEM((PROJ_ROWS, GLA_V_W), BF16),
            pltpu.VMEM((PROJ_ROWS, GLA_V_W), BF16),
            pltpu.VMEM((PROJ_ROWS, GLA_QK_W), F32),
            pltpu.VMEM((gla_pairs, LANES, pair_w), F32),
            pltpu.VMEM((PROJ_ROWS, GLA_V_W), BF16),
            pltpu.VMEM((gla_pairs, gla_chunks, LANES, pair_w), F32),
            pltpu.VMEM((gla_pairs, gla_chunks, LANES, LANES), F32),
            pltpu.VMEM((gla_pairs, gla_chunks, LANES, pair_w), BF16)],
        compiler_params=_params("arbitrary"),
        name="proj",
    )(x, norm_g[None, :], w_in_t, w_alpha_up, b_alpha[None, :], b_forget[None, :],
      gla_norm_g[None, :])

    cend = crow[:FOX_HEADS, FOX_KEYS - 1::FOX_KEYS].reshape(-1)
    fox_pairs = FOX_HEADS // HEAD_PAIR
    pair_rows = pl.BlockSpec((FOX_BLOCK, LANES), lambda p, i, *_: (i, p))
    fox = pl.pallas_call(
        _fox_kernel,
        grid_spec=pltpu.PrefetchScalarGridSpec(
            num_scalar_prefetch=3,
            grid=(fox_pairs, T // FOX_BLOCK),
            in_specs=[pair_rows,
                      pl.BlockSpec((FOX_BLOCK, LANES), lambda p, i, *_: (i, 0)),
                      pl.BlockSpec((LANES, T), lambda p, i, *_: (p, 0)),
                      pl.BlockSpec((LANES, T), lambda p, i, *_: (fox_pairs, 0)),
                      pl.BlockSpec((T, LANES), lambda p, i, *_: (0, fox_pairs + p)),
                      pl.BlockSpec((FOX_BLOCK, LANES),
                                   lambda p, i, *_: (i, 2 * fox_pairs + p))],
            out_specs=pair_rows,
            scratch_shapes=[pltpu.VMEM((HEAD_PAIR, FOX_BLOCK, LANES), F32),
                            pltpu.VMEM((HEAD_PAIR, FOX_BLOCK, LANES), F32)]),
        out_shape=bshape(FOX_W),
        compiler_params=_params("arbitrary", "arbitrary"),
        name="fox",
    )(qn2.reshape(-1), kn2.reshape(-1), cend, fqvg, cq, kx, kx, fqvg, fqvg)

    out = pl.pallas_call(
        _out_kernel,
        grid=(T // OUT_ROWS,),
        in_specs=[rows(D_MODEL, OUT_ROWS), rows(GLA_OUT_W, OUT_ROWS), rows(FOX_W, OUT_ROWS),
                  pl.BlockSpec((OUT_ROWS, MEM_W), lambda i: (i, 0)),
                  pl.BlockSpec((OUT_ROWS, MEM_W), lambda i: (i, 1)),
                  once(mem.shape), once((1, D_MODEL)), once(w_mem_kv.shape), once(w_out.shape),
                  whole((1, D_MODEL))],
        out_specs=rows(D_MODEL, OUT_ROWS),
        out_shape=jax.ShapeDtypeStruct((T, D_MODEL), F32),
        scratch_shapes=[pltpu.VMEM((D_MODEL, D_MODEL), BF16),
                        pltpu.VMEM((M, MEM_W), BF16), pltpu.VMEM((M, MEM_W), BF16)],
        compiler_params=_params("arbitrary"),
        name="out",
    )(x, gla, fox, mqg, mqg, mem, mem_norm_g[None, :], w_mem_kv, w_out, out_g[None, :])
    return out


def kernel(x, mem, norm_g, w_in, w_alpha_up, b_alpha, b_forget, gla_norm_g, mem_norm_g,
           w_mem_kv, w_out, final_norm_g):
    assert x.shape[0] == 1 and mem.shape[0] == 1 and norm_g.shape[0] == 1
    assert x.shape[1] % max(PROJ_ROWS, FOX_BLOCK, OUT_ROWS) == 0
    out = _layer(x[0], mem[0], norm_g[0], w_in[0], w_alpha_up[0], b_alpha[0], b_forget[0],
                 gla_norm_g[0], mem_norm_g[0], w_mem_kv[0], w_out[0], final_norm_g)
    return out[None]
```

```python
import jax
import jax.numpy as jnp
from jax import lax
from jax.experimental import pallas as pl
from jax.experimental.pallas import tpu as pltpu

F32 = jnp.float32
BF16 = jnp.bfloat16

EPS = 1e-6
LANES = 128
SUBLANES = 8

D_MODEL = 1024
GLA_HEADS, GLA_DK, GLA_DV, GLA_RANK = 4, 48, 96, 16
GLA_DK_PAD = 64
GLA_DV_PAD = LANES
GLA_GATE_NORM = 16.0
GLA_CHUNK = 64
FOX_HEADS, FOX_DH = 6, 64
MEM_HEADS, MEM_DH = 4, 64
HEAD_PAIR = 2
GLA_QK_W = GLA_HEADS * GLA_DK_PAD
GLA_V_W = GLA_HEADS * GLA_DV_PAD
GLA_OUT_W = GLA_HEADS * GLA_DV
FOX_W = FOX_HEADS * FOX_DH
MEM_W = MEM_HEADS * MEM_DH
SMALL_W = LANES
FG_LANE0 = 0
LR_LANE0 = SUBLANES

_GROUPS = (("gq", GLA_QK_W), ("gk", GLA_QK_W), ("gv", GLA_OUT_W), ("gg", GLA_OUT_W),
           ("fq", FOX_W), ("fk", FOX_W), ("fv", FOX_W), ("fgate", FOX_W),
           ("mq", MEM_W), ("mg", MEM_W))
FG_COL = GLA_DK
LR_COL = GLA_DK_PAD + GLA_DK
_OFF = {}
_o = 0
for _n, _w in _GROUPS:
    _OFF[_n] = (_o, _o + _w)
    _o += _w
IN_COLS_PAD = _o

PROJ_ROWS = 1024
FOX_BLOCK = 4096
FOX_KEYS = 256
OUT_ROWS = 1024
VMEM_LIMIT = 56 * 1024 * 1024

NEG_BIG = -1e30
FOX_SKIP_NATS = 105.0
NORM_SLACK = 1.02
FOX_DIRECT_NORM2 = 3600.0
CX_HI, CX_MID, CX_LO, CX_ONE = 0, 8, 16, 24


def _log_sigmoid(z):
    return jnp.minimum(z, 0.0) - jnp.log(1.0 + jnp.exp(-jnp.abs(z)))


def _silu(z):
    return z / (1.0 + jnp.exp(-z))


def _rms_scale(v, width):
    return lax.rsqrt(jnp.sum(v * v, axis=-1, keepdims=True) * (1.0 / width) + EPS)


def _w_in_segments():
    qk, gw = GLA_HEADS * GLA_DK, GLA_HEADS * GLA_DV
    src = {}
    o = 0
    for name, width in (("gq", qk), ("gk", qk), ("gv", gw), ("lr", GLA_RANK), ("gg", gw),
                        ("fq", FOX_W), ("fk", FOX_W), ("fv", FOX_W), ("fg", FOX_HEADS),
                        ("fgate", FOX_W), ("mq", MEM_W), ("mg", MEM_W)):
        src[name] = o
        o += width
    segs = []
    for name, d, d_pad in (("gq", GLA_DK, GLA_DK_PAD), ("gk", GLA_DK, GLA_DK_PAD)):
        segs += [(src[name] + h * d, _OFF[name][0] + h * d_pad, d) for h in range(GLA_HEADS)]
    segs += [(src[name], _OFF[name][0], _OFF[name][1] - _OFF[name][0])
             for name in ("gv", "gg", "fq", "fk", "fv", "fgate", "mq", "mg")]
    segs += [(src["fg"], _OFF["gq"][0] + FG_COL, FOX_HEADS),
             (src["lr"], _OFF["gq"][0] + LR_COL, GLA_RANK)]
    return tuple(segs)


def _proj_kernel(x_ref, g_ref, w_in_ref, w_alpha_ref, b_alpha_ref, b_forget_ref, gla_g_ref,
                 gla_ref, fqvg_ref, kx_ref, mqg_ref, crow_ref, cq_ref, qn2_ref, kn2_ref,
                 carry_ref, wt_ref, wa_ref, ba_ref, bf_ref, ng_ref, small_ref,
                 gq_ref, gk_ref, gv_ref, gg_ref, loga_ref,
                 s_ref, lhs_ref, kv_ref, dec_ref, sprev_ref):
    rows = x_ref.shape[0]
    k_chunks = D_MODEL // LANES

    @pl.when(pl.program_id(0) == 0)
    def _():
        carry_ref[...] = jnp.zeros_like(carry_ref)
        s_ref[...] = jnp.zeros_like(s_ref)
        kn2_ref[...] = jnp.zeros_like(kn2_ref)
        gv_ref[...] = jnp.zeros_like(gv_ref)
        gg_ref[...] = jnp.zeros_like(gg_ref)
        small_ref[...] = jnp.zeros_like(small_ref)
        wa_ref[...] = jnp.zeros_like(wa_ref)
        ba_ref[...] = jnp.zeros_like(ba_ref)
        bf_ref[...] = jnp.zeros_like(bf_ref)
        ng_ref[...] = jnp.zeros_like(ng_ref)
        for h in range(GLA_HEADS):
            src = slice(h * GLA_DK, (h + 1) * GLA_DK)
            dst = slice(h * GLA_DK_PAD, h * GLA_DK_PAD + GLA_DK)
            wa_ref[LR_LANE0:LR_LANE0 + GLA_RANK, dst] = w_alpha_ref[:, src]
            ba_ref[:, dst] = b_alpha_ref[:, src]
        bf_ref[:, FG_LANE0:FG_LANE0 + FOX_HEADS] = b_forget_ref[...]
        ng_ref[:, 0:GLA_DV] = gla_g_ref[...]
        wt_ref[...] = jnp.zeros_like(wt_ref)
        for s0, d0, width in _w_in_segments():
            for c in range(k_chunks):
                wt_ref[d0:d0 + width, c * LANES:(c + 1) * LANES] = (
                    w_in_ref[pl.ds(s0 * k_chunks + c, width, stride=k_chunks), :].astype(BF16))

    x = x_ref[...]
    xn = (x * _rms_scale(x, D_MODEL) * g_ref[...]).astype(BF16)
    nt = (((1,), (1,)), ((), ()))

    def proj(first, last):
        lo, hi = _OFF[first][0], _OFF[last][1]
        y = lax.dot_general(xn, wt_ref[lo:hi, :], nt, preferred_element_type=F32)
        return lambda name: y[:, _OFF[name][0] - lo:_OFF[name][1] - lo]

    gla = proj("gq", "gg")
    tail = proj("mq", "mg")
    gq_all = gla("gq")
    small_ref[:, FG_LANE0:FG_LANE0 + SUBLANES] = gq_all[:, FG_COL:FG_COL + SUBLANES]
    small_ref[:, LR_LANE0:LR_LANE0 + GLA_RANK] = gq_all[:, LR_COL:LR_COL + GLA_RANK]
    small = small_ref[...]
    logf = _log_sigmoid(small + bf_ref[...])
    c = logf.T[0:SUBLANES, :]
    lane = lax.broadcasted_iota(jnp.int32, c.shape, 1)
    shift = 1
    while shift < rows:
        c = c + jnp.where(lane >= shift, pltpu.roll(c, shift, axis=1), 0.0)
        shift *= 2
    c = c + carry_ref[:, 0:1]
    crow_ref[...] = c
    carry_ref[...] = jnp.broadcast_to(c[:, rows - 1:rows], carry_ref.shape)
    neg = -c
    hi = neg.astype(BF16).astype(F32)
    mid = (neg - hi).astype(BF16).astype(F32)
    low = neg - hi - mid
    parts = jnp.concatenate(
        [hi, mid, low, jnp.ones_like(c), jnp.zeros((LANES - 4 * SUBLANES, rows), F32)], axis=0)
    kx_ref[FOX_W:, :] = parts.astype(BF16)
    cq_ref[...] = parts.T.astype(BF16)

    z = (jnp.dot(small.astype(BF16), wa_ref[...].astype(BF16), preferred_element_type=F32)
         + ba_ref[...])
    loga_ref[...] = _log_sigmoid(z) * (1.0 / GLA_GATE_NORM)
    gq_ref[...] = gla("gq").astype(BF16)
    gk_ref[...] = gla("gk").astype(BF16)
    gv, gg = gla("gv").astype(BF16), _silu(gla("gg")).astype(BF16)
    for h in range(GLA_HEADS):
        src = slice(h * GLA_DV, (h + 1) * GLA_DV)
        dst = slice(h * GLA_DV_PAD, h * GLA_DV_PAD + GLA_DV)
        gv_ref[:, dst] = gv[:, src]
        gg_ref[:, dst] = gg[:, src]
    gla_local, gla_scan, gla_output = _gla_block(
        gq_ref, gk_ref, gv_ref, loga_ref, gg_ref, ng_ref, gla_ref,
        s_ref, lhs_ref, kv_ref, dec_ref, sprev_ref)

    def max_sq_norm(v):
        v32 = v.astype(F32)
        sq = v32 * v32
        lane1 = lax.broadcasted_iota(jnp.int32, (1, LANES), 1)
        out = jnp.zeros((1, LANES), F32)
        for p in range(FOX_HEADS // HEAD_PAIR):
            slab = sq[:, p * LANES:(p + 1) * LANES]
            for hh in range(HEAD_PAIR):
                own = (lane1 < FOX_DH) if hh == 0 else (lane1 >= FOX_DH)
                norm2 = jnp.sum(jnp.where(own, slab, 0.0), axis=1, keepdims=True)
                out = jnp.where(lane1 == p * HEAD_PAIR + hh,
                                jnp.max(norm2, axis=0, keepdims=True), out)
        return out

    gla_local()
    fox_qk = proj("fq", "fk")
    mqg_ref[:, :MEM_W] = (tail("mq") * MEM_DH ** -0.5).astype(BF16)
    mqg_ref[:, MEM_W:] = _silu(tail("mg")).astype(BF16)
    gla_scan()
    fq = (fox_qk("fq") * FOX_DH ** -0.5).astype(BF16)
    fk = fox_qk("fk").astype(BF16)
    fqvg_ref[:, :FOX_W] = fq
    kx_ref[:FOX_W, :] = fox_qk("fk").T.astype(BF16)
    gla_output()
    fox_vg = proj("fv", "fgate")
    fqvg_ref[:, FOX_W:2 * FOX_W] = fox_vg("fv").astype(BF16)
    fqvg_ref[:, 2 * FOX_W:] = _silu(fox_vg("fgate")).astype(BF16)
    qn2_ref[0] = max_sq_norm(fq)
    kn2_ref[0] = jnp.maximum(kn2_ref[0], max_sq_norm(fk))


def _memkv(mem_ref, g_ref, w_ref, mk_ref, mv_ref):
    m = mem_ref[...]
    mn = (m * _rms_scale(m, D_MODEL) * g_ref[...]).astype(BF16)
    kv = jnp.dot(mn, w_ref[...].astype(BF16), preferred_element_type=F32)
    mk_ref[...] = kv[:, :MEM_W].astype(BF16)
    mv_ref[...] = kv[:, MEM_W:].astype(BF16)


def _gla_block(q_ref, k_ref, v_ref, loga_ref, gate_ref, ng_ref, o_ref,
               s_ref, lhs_ref, kv_ref, dec_ref, sprev_ref):
    C = GLA_CHUNK
    W = HEAD_PAIR * GLA_DV_PAD
    n_chunks = q_ref.shape[0] // C

    row = lax.broadcasted_iota(jnp.int32, (C, LANES), 0)
    lane = lax.broadcasted_iota(jnp.int32, (C, LANES), 1)
    lo_k = lane < GLA_DK_PAD
    causal = row >= jnp.where(lo_k, lane, lane - GLA_DK_PAD)
    lo_v = lax.broadcasted_iota(jnp.int32, (C, W), 1) < GLA_DV_PAD
    st_row = lax.broadcasted_iota(jnp.int32, (LANES, W), 0)
    st_lane = lax.broadcasted_iota(jnp.int32, (LANES, W), 1)
    own = (st_row < GLA_DK_PAD) == (st_lane < GLA_DV_PAD)
    eye = (lax.broadcasted_iota(jnp.int32, (LANES, LANES), 0)
           == lax.broadcasted_iota(jnp.int32, (LANES, LANES), 1))
    scale = GLA_DK ** -0.5
    nt = (((1,), (1,)), ((), ()))
    tn = (((0,), (0,)), ((), ()))
    ng = jnp.concatenate([ng_ref[...]] * HEAD_PAIR, axis=1)

    pairs = range(GLA_HEADS // HEAD_PAIR)

    def local(ci):
        rs = slice(ci * C, (ci + 1) * C)
        for p in pairs:
            ls = slice(p * LANES, (p + 1) * LANES)
            vs = slice(p * W, (p + 1) * W)
            b = loga_ref[rs, ls]
            shift = 1
            while shift < C:
                b = b + jnp.where(row >= shift, pltpu.roll(b, shift, axis=0), 0.0)
                shift *= 2
            b_last = b[C - 1:C, :]
            k2 = k_ref[rs, ls].astype(F32)
            qd = (q_ref[rs, ls].astype(F32) * scale * jnp.exp(b)).astype(BF16)
            kd = (k2 * jnp.exp(-b)).astype(BF16)
            ke = (k2 * jnp.exp(b_last - b)).astype(BF16)
            zk = jnp.zeros_like(kd)
            kd_blk = jnp.concatenate([jnp.where(lo_k, kd, zk), jnp.where(lo_k, zk, kd)], axis=0)
            attn = lax.dot_general(qd, kd_blk, nt, preferred_element_type=F32)
            lhs_ref[rs, vs] = jnp.concatenate([jnp.where(causal, attn, 0.0).astype(BF16), qd], axis=1)
            kv = lax.dot_general(ke, v_ref[rs, vs], tn, preferred_element_type=F32)
            kv_ref[p, ci] = jnp.where(own, kv, 0.0)
            dcol = jnp.exp(jnp.sum(jnp.where(eye, jnp.broadcast_to(b_last, (LANES, LANES)), 0.0),
                                   axis=1, keepdims=True))
            dec_ref[p, ci] = jnp.broadcast_to(dcol, (LANES, LANES))

    def scan(ci):
        for p in pairs:
            s_prev = s_ref[p]
            sprev_ref[p, ci] = s_prev.astype(BF16)
            s_ref[p] = jnp.tile(dec_ref[p, ci], (1, HEAD_PAIR)) * s_prev + kv_ref[p, ci]

    def output(ci):
        rs = slice(ci * C, (ci + 1) * C)
        for p in pairs:
            vs = slice(p * W, (p + 1) * W)
            v2 = v_ref[rs, vs]
            zv = jnp.zeros_like(v2)
            v_blk = jnp.concatenate([jnp.where(lo_v, v2, zv), jnp.where(lo_v, zv, v2)], axis=0)
            o = jnp.dot(lhs_ref[rs, vs], jnp.concatenate([v_blk, sprev_ref[p, ci]], axis=0),
                        preferred_element_type=F32)
            o2 = o * o
            ms = jnp.where(lo_v, jnp.sum(o2[:, :GLA_DV_PAD], axis=1, keepdims=True),
                           jnp.sum(o2[:, GLA_DV_PAD:], axis=1, keepdims=True))
            on = o * lax.rsqrt(ms * (1.0 / GLA_DV) + EPS) * ng
            og = (on * gate_ref[rs, vs].astype(F32)).astype(BF16)
            for hh in range(HEAD_PAIR):
                c0 = (p * HEAD_PAIR + hh) * GLA_DV
                o_ref[rs, c0:c0 + GLA_DV] = og[:, hh * GLA_DV_PAD:hh * GLA_DV_PAD + GLA_DV]

    def all_chunks(phase):
        return lambda: [phase(ci) for ci in range(n_chunks)]

    return all_chunks(local), all_chunks(scan), all_chunks(output)


def _fox_kernel(qn2_ref, kn2_ref, cend_ref, q_ref, cq_ref, k_ref, cx_ref, v_ref, gate_ref, o_ref,
                m_ref, acc_ref):
    blk = FOX_KEYS
    streams = range(q_ref.shape[0] // blk)
    pair = pl.program_id(0)
    qi = pl.program_id(1)
    nblk = pl.num_programs(1) * len(streams)
    lane = lax.broadcasted_iota(jnp.int32, (1, LANES), 1)
    lo_lanes = lane < FOX_DH
    reps = blk // LANES
    diag = [qi * len(streams) + s for s in streams]
    heads = [pair * HEAD_PAIR + hh for hh in range(HEAD_PAIR)]

    q = q_ref[...]
    zero = jnp.zeros_like(q)
    q_lo, q_hi = jnp.where(lo_lanes, q, zero), jnp.where(lo_lanes, zero, q)
    q_stack = [jnp.concatenate([q_lo[s * blk:(s + 1) * blk], q_hi[s * blk:(s + 1) * blk]], axis=0)
               for s in streams]
    g_row = lax.broadcasted_iota(jnp.int32, (LANES, LANES), 0)
    g_col = lax.broadcasted_iota(jnp.int32, (LANES, LANES), 1)
    xlane = lax.broadcasted_iota(jnp.int32, (HEAD_PAIR * blk, LANES), 1)
    cq = cq_ref[...]

    def lane_map(h, shift):
        g = jnp.where((g_row == CX_ONE) & ((g_col == CX_HI + h) | (g_col == CX_MID + h)
                                           | (g_col == CX_LO + h)), 1.0, 0.0)
        if shift:
            for part, base in enumerate((CX_HI, CX_MID, CX_LO)):
                g = jnp.where((g_row == base + h) & (g_col == CX_ONE + part), -1.0, g)
        return g.astype(BF16)

    def q_aug(s, shift, void=None):
        cs = cq[s * blk:(s + 1) * blk]
        extra = jnp.concatenate([jnp.dot(cs, lane_map(h, shift), preferred_element_type=F32)
                                 for h in heads], axis=0)
        if void is not None:
            extra = jnp.where(jnp.logical_and(xlane == CX_ONE + 3, void), NEG_BIG, extra)
        return jnp.concatenate([q_stack[s], extra.astype(BF16)], axis=1)

    acc_ref[...] = jnp.zeros_like(acc_ref)
    qpos = lax.broadcasted_iota(jnp.int32, (blk, blk), 0)
    kpos = lax.broadcasted_iota(jnp.int32, (blk, blk), 1)

    def block(j):
        ks = pl.ds(pl.multiple_of(jnp.maximum(j, 0) * blk, blk), blk)
        k_aug = jnp.concatenate([k_ref[:, ks], cx_ref[:, ks]], axis=0)
        vb = v_ref[ks, :]
        one = jnp.ones_like(vb)
        return k_aug, (jnp.where(lo_lanes, vb, one), jnp.where(lo_lanes, one, vb))

    def step_online(s, j, qa, masked):
        rows = slice(s * blk, (s + 1) * blk)
        k_aug, vaug = block(j)
        s_all = jnp.dot(qa, k_aug, preferred_element_type=F32)
        for hh in range(HEAD_PAIR):
            sc = s_all[hh * blk:(hh + 1) * blk]
            if masked:
                sc = jnp.where(kpos <= qpos, sc, NEG_BIG)
            m_prev = m_ref[hh, rows]
            m_new = jnp.maximum(m_prev, jnp.max(sc, axis=1, keepdims=True))
            p = jnp.exp(sc - jnp.tile(m_new, (1, reps)))
            alpha = jnp.exp(m_prev - m_new)
            pv = jnp.dot(p.astype(BF16), vaug[hh], preferred_element_type=F32)
            acc_ref[hh, rows] = alpha * acc_ref[hh, rows] + pv
            m_ref[hh, rows] = m_new

    def step_direct(s, j, qa, masked, only=None):
        rows = slice(s * blk, (s + 1) * blk)
        k_aug, vaug = block(j)
        hsel = range(HEAD_PAIR) if only is None else (only,)
        lhs = qa if only is None else qa[only * blk:(only + 1) * blk]
        s_all = jnp.dot(lhs, k_aug, preferred_element_type=F32)
        for n, hh in enumerate(hsel):
            sc = s_all[n * blk:(n + 1) * blk]
            if masked:
                sc = jnp.where(kpos <= qpos, sc, NEG_BIG)
            acc_ref[hh, rows] += jnp.dot(jnp.exp(sc).astype(BF16), vaug[hh],
                                         preferred_element_type=F32)

    k_max2 = [kn2_ref[h] for h in heads]

    def norm2(s, hh):
        stat = (diag[s] * blk // PROJ_ROWS) * LANES
        return (NORM_SLACK * NORM_SLACK) * qn2_ref[stat + heads[hh]] * k_max2[hh]

    n2 = [[norm2(s, hh) for hh in range(HEAD_PAIR)] for s in streams]
    gap0 = [[FOX_SKIP_NATS + cend_ref[heads[hh] * nblk + jnp.maximum(diag[s] - 1, 0)]
             for hh in range(HEAD_PAIR)] for s in streams]

    def live_head(t, hh, direct):
        keep = False
        for s in streams:
            j = diag[s] - 1 - t
            gap = gap0[s][hh] - cend_ref[heads[hh] * nblk + jnp.maximum(j, 0)]
            bound2 = n2[s][hh] if direct else 4.0 * n2[s][hh]
            dead = jnp.logical_and(gap <= 0.0, bound2 <= gap * gap)
            keep = jnp.logical_or(keep, jnp.logical_and(j >= 0, jnp.logical_not(dead)))
        return keep

    def sweeps(stepper, direct):
        def sweep(back, masked=False, only=None):
            for s in streams:
                stepper(s, diag[s] - back, masked, only)

        def loop(t0, cond, only=None):
            def body(t):
                sweep(t + 1, only=only)
                return t + 1
            return lax.while_loop(cond, body, t0)

        def live(t, hh):
            return live_head(t, hh, direct)

        sweep(0, masked=True)
        sweep(1)
        if direct:
            t_both = loop(1, lambda t: jnp.logical_and(live(t, 0), live(t, 1)))
            for hh in range(HEAD_PAIR):
                loop(t_both, lambda t, hh=hh: live(t, hh), only=hh)
        else:
            loop(1, lambda t: jnp.logical_or(live(t, 0), live(t, 1)))

    direct_ok = True
    for s in streams:
        for hh in range(HEAD_PAIR):
            direct_ok = jnp.logical_and(direct_ok, n2[s][hh] <= FOX_DIRECT_NORM2)

    @pl.when(direct_ok)
    def _():
        q_dir = [q_aug(s, shift=True) for s in streams]
        q_void = [q_aug(s, shift=True, void=True) for s in streams]
        sweeps(lambda s, j, masked, only: step_direct(
            s, j, q_dir[s] if masked else jnp.where(j >= 0, q_dir[s], q_void[s]), masked, only),
            direct=True)

    @pl.when(jnp.logical_not(direct_ok))
    def _():
        m_ref[...] = jnp.full_like(m_ref, NEG_BIG)
        sweeps(lambda s, j, masked, only: step_online(
            s, j, q_aug(s, shift=False, void=j < 0), masked), direct=False)

    outs = []
    for hh in range(HEAD_PAIR):
        acc = acc_ref[hh]
        outs.append(acc / pltpu.roll(acc, FOX_DH, axis=1))
    o = jnp.where(lo_lanes, outs[0], outs[1])
    o_ref[...] = (o * gate_ref[...].astype(F32)).astype(BF16)


def _out_kernel(x_ref, gla_ref, fox_ref, mq_ref, mg_ref, mem_ref, mem_g_ref, w_mem_ref,
                w_out_ref, fg_ref, o_ref, wo_ref, mk_ref, mv_ref):
    @pl.when(pl.program_id(0) == 0)
    def _():
        wo_ref[...] = w_out_ref[...].astype(BF16)
        _memkv(mem_ref, mem_g_ref, w_mem_ref, mk_ref, mv_ref)

    lane = lax.broadcasted_iota(jnp.int32, (1, LANES), 1)
    lo_lanes = lane < MEM_DH
    nt = (((1,), (1,)), ((), ()))
    mem_parts = []
    for p in range(MEM_HEADS // HEAD_PAIR):
        ls = slice(p * LANES, (p + 1) * LANES)
        q = mq_ref[:, ls]
        kb = mk_ref[:, ls]
        vb = mv_ref[:, ls]
        zero = jnp.zeros_like(q)
        one = jnp.ones_like(vb)
        qh = (jnp.where(lo_lanes, q, zero), jnp.where(lo_lanes, zero, q))
        vaug = (jnp.where(lo_lanes, vb, one), jnp.where(lo_lanes, one, vb))
        outs = []
        for hh in range(HEAD_PAIR):
            s = lax.dot_general(qh[hh], kb, nt, preferred_element_type=F32)
            pexp = jnp.exp(s - jnp.max(s, axis=1, keepdims=True))
            pv = jnp.dot(pexp.astype(BF16), vaug[hh], preferred_element_type=F32)
            outs.append(pv / pltpu.roll(pv, MEM_DH, axis=1))
        o = jnp.where(lo_lanes, outs[0], outs[1])
        mem_parts.append((o * mg_ref[:, ls].astype(F32)).astype(BF16))
    mixed = jnp.concatenate([gla_ref[...], fox_ref[...]] + mem_parts, axis=1)
    y = x_ref[...] + jnp.dot(mixed, wo_ref[...], preferred_element_type=F32)
    o_ref[...] = y * _rms_scale(y, D_MODEL) * fg_ref[...]


def _params(*sem):
    return pltpu.CompilerParams(dimension_semantics=sem, vmem_limit_bytes=VMEM_LIMIT)


def _layer(x, mem, norm_g, w_in, w_alpha_up, b_alpha, b_forget, gla_norm_g,
           mem_norm_g, w_mem_kv, w_out, out_g):
    w_in_t = jnp.transpose(w_in[None], (0, 2, 1)).reshape(-1, LANES)
    T = x.shape[0]
    M = mem.shape[0]

    def rows(width, n=PROJ_ROWS):
        return pl.BlockSpec((n, width), lambda i: (i, 0))

    def whole(shape):
        return pl.BlockSpec(shape, lambda i: (0,) * len(shape))

    def once(shape):
        return pl.BlockSpec(shape, lambda i: (0,) * len(shape), pipeline_mode=pl.Buffered(1))

    bshape = lambda w: jax.ShapeDtypeStruct((T, w), BF16)
    nproj = T // PROJ_ROWS
    stat_spec = pl.BlockSpec((1, 1, LANES), lambda i: (i, 0, 0))
    stat_shape = jax.ShapeDtypeStruct((nproj, 1, LANES), F32)
    gla_pairs, gla_chunks = GLA_HEADS // HEAD_PAIR, PROJ_ROWS // GLA_CHUNK
    pair_w = HEAD_PAIR * GLA_DV_PAD
    (gla, fqvg, kx, mqg, crow, cq, qn2, kn2) = pl.pallas_call(
        _proj_kernel,
        grid=(nproj,),
        in_specs=[rows(D_MODEL), whole((1, D_MODEL)),
                  pl.BlockSpec(w_in_t.shape, lambda i: (0, 0), pipeline_mode=pl.Buffered(1)),
                  whole(w_alpha_up.shape), whole((1, GLA_HEADS * GLA_DK)),
                  whole((1, FOX_HEADS)), whole((1, GLA_DV))],
        out_specs=[rows(GLA_OUT_W), rows(3 * FOX_W),
                   pl.BlockSpec((FOX_W + LANES, PROJ_ROWS), lambda i: (0, i)),
                   rows(2 * MEM_W),
                   pl.BlockSpec((SUBLANES, PROJ_ROWS), lambda i: (0, i)), rows(LANES),
                   stat_spec, pl.BlockSpec((1, 1, LANES), lambda i: (0, 0, 0))],
        out_shape=[bshape(GLA_OUT_W), bshape(3 * FOX_W),
                   jax.ShapeDtypeStruct((FOX_W + LANES, T), BF16),
                   bshape(2 * MEM_W),
                   jax.ShapeDtypeStruct((SUBLANES, T), F32), bshape(LANES),
                   stat_shape, jax.ShapeDtypeStruct((1, 1, LANES), F32)],
        scratch_shapes=[
            pltpu.VMEM((SUBLANES, LANES), F32),
            pltpu.VMEM((IN_COLS_PAD, D_MODEL), BF16),
            pltpu.VMEM((SMALL_W, GLA_QK_W), F32),
            pltpu.VMEM((1, GLA_QK_W), F32),
            pltpu.VMEM((1, SMALL_W), F32),
            pltpu.VMEM((1, GLA_DV_PAD), F32),
            pltpu.VMEM((PROJ_ROWS, SMALL_W), F32),
            pltpu.VMEM((PROJ_ROWS, GLA_QK_W), BF16),
            pltpu.VMEM((PROJ_ROWS, GLA_QK_W), BF16),
            pltpu.VMEM((PROJ_ROWS, GLA_V_W), BF16),
            pltpu.VMEM((PROJ_ROWS, GLA_V_W), BF16),
            pltpu.VMEM((PROJ_ROWS, GLA_QK_W), F32),
            pltpu.VMEM((gla_pairs, LANES, pair_w), F32),
            pltpu.VMEM((PROJ_ROWS, GLA_V_W), BF16),
            pltpu.VMEM((gla_pairs, gla_chunks, LANES, pair_w), F32),
            pltpu.VMEM((gla_pairs, gla_chunks, LANES, LANES), F32),
            pltpu.VMEM((gla_pairs, gla_chunks, LANES, pair_w), BF16)],
        compiler_params=_params("arbitrary"),
        name="proj",
    )(x, norm_g[None, :], w_in_t, w_alpha_up, b_alpha[None, :], b_forget[None, :],
      gla_norm_g[None, :])

    cend = crow[:FOX_HEADS, FOX_KEYS - 1::FOX_KEYS].reshape(-1)
    fox_pairs = FOX_HEADS // HEAD_PAIR
    pair_rows = pl.BlockSpec((FOX_BLOCK, LANES), lambda p, i, *_: (i, p))
    fox = pl.pallas_call(
        _fox_kernel,
        grid_spec=pltpu.PrefetchScalarGridSpec(
            num_scalar_prefetch=3,
            grid=(fox_pairs, T // FOX_BLOCK),
            in_specs=[pair_rows,
                      pl.BlockSpec((FOX_BLOCK, LANES), lambda p, i, *_: (i, 0)),
                      pl.BlockSpec((LANES, T), lambda p, i, *_: (p, 0)),
                      pl.BlockSpec((LANES, T), lambda p, i, *_: (fox_pairs, 0)),
                      pl.BlockSpec((T, LANES), lambda p, i, *_: (0, fox_pairs + p)),
                      pl.BlockSpec((FOX_BLOCK, LANES),
                                   lambda p, i, *_: (i, 2 * fox_pairs + p))],
            out_specs=pair_rows,
            scratch_shapes=[pltpu.VMEM((HEAD_PAIR, FOX_BLOCK, LANES), F32),
                            pltpu.VMEM((HEAD_PAIR, FOX_BLOCK, LANES), F32)]),
        out_shape=bshape(FOX_W),
        compiler_params=_params("arbitrary", "arbitrary"),
        name="fox",
    )(qn2.reshape(-1), kn2.reshape(-1), cend, fqvg, cq, kx, kx, fqvg, fqvg)

    out = pl.pallas_call(
        _out_kernel,
        grid=(T // OUT_ROWS,),
        in_specs=[rows(D_MODEL, OUT_ROWS), rows(GLA_OUT_W, OUT_ROWS), rows(FOX_W, OUT_ROWS),
                  pl.BlockSpec((OUT_ROWS, MEM_W), lambda i: (i, 0)),
                  pl.BlockSpec((OUT_ROWS, MEM_W), lambda i: (i, 1)),
                  once(mem.shape), once((1, D_MODEL)), once(w_mem_kv.shape), once(w_out.shape),
                  whole((1, D_MODEL))],
        out_specs=rows(D_MODEL, OUT_ROWS),
        out_shape=jax.ShapeDtypeStruct((T, D_MODEL), F32),
        scratch_shapes=[pltpu.VMEM((D_MODEL, D_MODEL), BF16),
                        pltpu.VMEM((M, MEM_W), BF16), pltpu.VMEM((M, MEM_W), BF16)],
        compiler_params=_params("arbitrary"),
        name="out",
    )(x, gla, fox, mqg, mqg, mem, mem_norm_g[None, :], w_mem_kv, w_out, out_g[None, :])
    return out


def kernel(x, mem, norm_g, w_in, w_alpha_up, b_alpha, b_forget, gla_norm_g, mem_norm_g,
           w_mem_kv, w_out, final_norm_g):
    assert x.shape[0] == 1 and mem.shape[0] == 1 and norm_g.shape[0] == 1
    assert x.shape[1] % max(PROJ_ROWS, FOX_BLOCK, OUT_ROWS) == 0
    out = _layer(x[0], mem[0], norm_g[0], w_in[0], w_alpha_up[0], b_alpha[0], b_forget[0],
                 gla_norm_g[0], mem_norm_g[0], w_mem_kv[0], w_out[0], final_norm_g)
    return out[None]
```

```python
import jax
import jax.numpy as jnp
from jax import lax
from jax.experimental import pallas as pl
from jax.experimental.pallas import tpu as pltpu

F32 = jnp.float32
BF16 = jnp.bfloat16

EPS = 1e-6
LANES = 128
SUBLANES = 8

D_MODEL = 1024
GLA_HEADS, GLA_DK, GLA_DV, GLA_RANK = 4, 48, 96, 16
GLA_DK_PAD = 64
GLA_DV_PAD = LANES
GLA_GATE_NORM = 16.0
GLA_CHUNK = 64
FOX_HEADS, FOX_DH = 6, 64
MEM_HEADS, MEM_DH = 4, 64
HEAD_PAIR = 2
GLA_QK_W = GLA_HEADS * GLA_DK_PAD
GLA_V_W = GLA_HEADS * GLA_DV_PAD
GLA_OUT_W = GLA_HEADS * GLA_DV
FOX_W = FOX_HEADS * FOX_DH
MEM_W = MEM_HEADS * MEM_DH
SMALL_W = LANES
FG_LANE0 = 0
LR_LANE0 = SUBLANES

_GROUPS = (("gq", GLA_QK_W), ("gk", GLA_QK_W), ("gv", GLA_OUT_W), ("gg", GLA_OUT_W),
           ("fq", FOX_W), ("fk", FOX_W), ("fv", FOX_W), ("fgate", FOX_W),
           ("mq", MEM_W), ("mg", MEM_W))
FG_COL = GLA_DK
LR_COL = GLA_DK_PAD + GLA_DK
_OFF = {}
_o = 0
for _n, _w in _GROUPS:
    _OFF[_n] = (_o, _o + _w)
    _o += _w
IN_COLS_PAD = _o

PROJ_ROWS = 1024
FOX_BLOCK = 4096
FOX_KEYS = 256
OUT_ROWS = 1024
VMEM_LIMIT = 56 * 1024 * 1024

NEG_BIG = -1e30
FOX_SKIP_NATS = 105.0
NORM_SLACK = 1.02
FOX_DIRECT_NORM2 = 3600.0
CX_HI, CX_MID, CX_LO, CX_ONE = 0, 8, 16, 24


def _log_sigmoid(z):
    return jnp.minimum(z, 0.0) - jnp.log(1.0 + jnp.exp(-jnp.abs(z)))


def _silu(z):
    return z / (1.0 + jnp.exp(-z))


def _rms_scale(v, width):
    return lax.rsqrt(jnp.sum(v * v, axis=-1, keepdims=True) * (1.0 / width) + EPS)


def _w_in_segments():
    qk, gw = GLA_HEADS * GLA_DK, GLA_HEADS * GLA_DV
    src = {}
    o = 0
    for name, width in (("gq", qk), ("gk", qk), ("gv", gw), ("lr", GLA_RANK), ("gg", gw),
                        ("fq", FOX_W), ("fk", FOX_W), ("fv", FOX_W), ("fg", FOX_HEADS),
                        ("fgate", FOX_W), ("mq", MEM_W), ("mg", MEM_W)):
        src[name] = o
        o += width
    segs = []
    for name, d, d_pad in (("gq", GLA_DK, GLA_DK_PAD), ("gk", GLA_DK, GLA_DK_PAD)):
        segs += [(src[name] + h * d, _OFF[name][0] + h * d_pad, d) for h in range(GLA_HEADS)]
    segs += [(src[name], _OFF[name][0], _OFF[name][1] - _OFF[name][0])
             for name in ("gv", "gg", "fq", "fk", "fv", "fgate", "mq", "mg")]
    segs += [(src["fg"], _OFF["gq"][0] + FG_COL, FOX_HEADS),
             (src["lr"], _OFF["gq"][0] + LR_COL, GLA_RANK)]
    return tuple(segs)


def _proj_kernel(x_ref, g_ref, w_in_ref, w_alpha_ref, b_alpha_ref, b_forget_ref, gla_g_ref,
                 gla_ref, fqvg_ref, kx_ref, mqg_ref, crow_ref, cq_ref, qn2_ref, kn2_ref,
                 carry_ref, wt_ref, wa_ref, ba_ref, bf_ref, ng_ref, seg_ref, small_ref,
                 gq_ref, gk_ref, gv_ref, gg_ref, loga_ref,
                 s_ref, lhs_ref, kv_ref, dec_ref, sprev_ref):
    rows = x_ref.shape[0]
    k_chunks = D_MODEL // LANES

    @pl.when(pl.program_id(0) == 0)
    def _():
        carry_ref[...] = jnp.zeros_like(carry_ref)
        s_ref[...] = jnp.zeros_like(s_ref)
        kn2_ref[...] = jnp.zeros_like(kn2_ref)
        gv_ref[...] = jnp.zeros_like(gv_ref)
        gg_ref[...] = jnp.zeros_like(gg_ref)
        small_ref[...] = jnp.zeros_like(small_ref)
        wa_ref[...] = jnp.zeros_like(wa_ref)
        ba_ref[...] = jnp.zeros_like(ba_ref)
        bf_ref[...] = jnp.zeros_like(bf_ref)
        ng_ref[...] = jnp.zeros_like(ng_ref)
        for h in range(GLA_HEADS):
            src = slice(h * GLA_DK, (h + 1) * GLA_DK)
            dst = slice(h * GLA_DK_PAD, h * GLA_DK_PAD + GLA_DK)
            wa_ref[LR_LANE0:LR_LANE0 + GLA_RANK, dst] = w_alpha_ref[:, src]
            ba_ref[:, dst] = b_alpha_ref[:, src]
        bf_ref[:, FG_LANE0:FG_LANE0 + FOX_HEADS] = b_forget_ref[...]
        ng_ref[:, 0:GLA_DV] = gla_g_ref[...]
        seg_ref[...] = (lax.broadcasted_iota(jnp.int32, seg_ref.shape, 0) // FOX_DH
                        == lax.broadcasted_iota(jnp.int32, seg_ref.shape, 1)).astype(BF16)
        wt_ref[...] = jnp.zeros_like(wt_ref)
        for s0, d0, width in _w_in_segments():
            for c in range(k_chunks):
                wt_ref[d0:d0 + width, c * LANES:(c + 1) * LANES] = (
                    w_in_ref[pl.ds(s0 * k_chunks + c, width, stride=k_chunks), :].astype(BF16))

    x = x_ref[...]
    xn = (x * _rms_scale(x, D_MODEL) * g_ref[...]).astype(BF16)
    nt = (((1,), (1,)), ((), ()))

    def proj(first, last):
        lo, hi = _OFF[first][0], _OFF[last][1]
        y = lax.dot_general(xn, wt_ref[lo:hi, :], nt, preferred_element_type=F32)
        return lambda name: y[:, _OFF[name][0] - lo:_OFF[name][1] - lo]

    gla = proj("gq", "gg")
    tail = proj("mq", "mg")
    gq_all = gla("gq")
    small_ref[:, FG_LANE0:FG_LANE0 + SUBLANES] = gq_all[:, FG_COL:FG_COL + SUBLANES]
    small_ref[:, LR_LANE0:LR_LANE0 + GLA_RANK] = gq_all[:, LR_COL:LR_COL + GLA_RANK]
    small = small_ref[...]
    logf = _log_sigmoid(small + bf_ref[...])
    c = logf.T[0:SUBLANES, :]
    lane = lax.broadcasted_iota(jnp.int32, c.shape, 1)
    shift = 1
    while shift < rows:
        c = c + jnp.where(lane >= shift, pltpu.roll(c, shift, axis=1), 0.0)
        shift *= 2
    c = c + carry_ref[:, 0:1]
    crow_ref[...] = c
    carry_ref[...] = jnp.broadcast_to(c[:, rows - 1:rows], carry_ref.shape)
    neg = -c
    hi = neg.astype(BF16).astype(F32)
    mid = (neg - hi).astype(BF16).astype(F32)
    low = neg - hi - mid
    parts = jnp.concatenate(
        [hi, mid, low, jnp.ones_like(c), jnp.zeros((LANES - 4 * SUBLANES, rows), F32)], axis=0)
    kx_ref[FOX_W:, :] = parts.astype(BF16)
    cq_ref[...] = parts.T.astype(BF16)

    z = (jnp.dot(small.astype(BF16), wa_ref[...].astype(BF16), preferred_element_type=F32)
         + ba_ref[...])
    loga_ref[...] = _log_sigmoid(z) * (1.0 / GLA_GATE_NORM)
    q_lane = lax.broadcasted_iota(jnp.int32, (1, GLA_QK_W), 1)
    gq_ref[...] = jnp.where(q_lane % GLA_DK_PAD < GLA_DK, gq_all, 0.0).astype(BF16)
    gk_ref[...] = gla("gk").astype(BF16)
    gv, gg = gla("gv").astype(BF16), _silu(gla("gg")).astype(BF16)
    for h in range(GLA_HEADS):
        src = slice(h * GLA_DV, (h + 1) * GLA_DV)
        dst = slice(h * GLA_DV_PAD, h * GLA_DV_PAD + GLA_DV)
        gv_ref[:, dst] = gv[:, src]
        gg_ref[:, dst] = gg[:, src]
    gla_local, gla_scan, gla_output = _gla_block(
        gq_ref, gk_ref, gv_ref, loga_ref, gg_ref, ng_ref, gla_ref,
        s_ref, lhs_ref, kv_ref, dec_ref, sprev_ref)

    def max_sq_norm(v):
        v32 = v.astype(F32)
        n2 = jnp.dot((v32 * v32).astype(BF16), seg_ref[...], preferred_element_type=F32)
        return jnp.max(n2, axis=0, keepdims=True)

    gla_local()
    fox_qk = proj("fq", "fk")
    mqg_ref[:, :MEM_W] = (tail("mq") * MEM_DH ** -0.5).astype(BF16)
    mqg_ref[:, MEM_W:] = _silu(tail("mg")).astype(BF16)
    gla_scan()
    fq = (fox_qk("fq") * FOX_DH ** -0.5).astype(BF16)
    fk = fox_qk("fk").astype(BF16)
    fqvg_ref[:, :FOX_W] = fq
    kx_ref[:FOX_W, :] = fox_qk("fk").T.astype(BF16)
    gla_output()
    fox_vg = proj("fv", "fgate")
    fqvg_ref[:, FOX_W:2 * FOX_W] = fox_vg("fv").astype(BF16)
    fqvg_ref[:, 2 * FOX_W:] = _silu(fox_vg("fgate")).astype(BF16)
    qn2_ref[0] = max_sq_norm(fq)
    kn2_ref[0] = jnp.maximum(kn2_ref[0], max_sq_norm(fk))


def _memkv(mem_ref, g_ref, w_ref, mk_ref, mv_ref):
    m = mem_ref[...]
    mn = (m * _rms_scale(m, D_MODEL) * g_ref[...]).astype(BF16)
    kv = jnp.dot(mn, w_ref[...].astype(BF16), preferred_element_type=F32)
    mk_ref[...] = kv[:, :MEM_W].astype(BF16)
    mv_ref[...] = kv[:, MEM_W:].astype(BF16)


def _gla_block(q_ref, k_ref, v_ref, loga_ref, gate_ref, ng_ref, o_ref,
               s_ref, lhs_ref, kv_ref, dec_ref, sprev_ref):
    C = GLA_CHUNK
    W = HEAD_PAIR * GLA_DV_PAD
    n_chunks = q_ref.shape[0] // C

    row = lax.broadcasted_iota(jnp.int32, (C, LANES), 0)
    lane = lax.broadcasted_iota(jnp.int32, (C, LANES), 1)
    lo_k = lane < GLA_DK_PAD
    causal = row >= jnp.where(lo_k, lane, lane - GLA_DK_PAD)
    lo_v = lax.broadcasted_iota(jnp.int32, (C, W), 1) < GLA_DV_PAD
    st_row = lax.broadcasted_iota(jnp.int32, (LANES, W), 0)
    st_lane = lax.broadcasted_iota(jnp.int32, (LANES, W), 1)
    own = (st_row < GLA_DK_PAD) == (st_lane < GLA_DV_PAD)
    eye = (lax.broadcasted_iota(jnp.int32, (LANES, LANES), 0)
           == lax.broadcasted_iota(jnp.int32, (LANES, LANES), 1))
    scale = GLA_DK ** -0.5
    nt = (((1,), (1,)), ((), ()))
    tn = (((0,), (0,)), ((), ()))
    ng = jnp.concatenate([ng_ref[...]] * HEAD_PAIR, axis=1)

    pairs = range(GLA_HEADS // HEAD_PAIR)

    def local(ci):
        rs = slice(ci * C, (ci + 1) * C)
        for p in pairs:
            ls = slice(p * LANES, (p + 1) * LANES)
            vs = slice(p * W, (p + 1) * W)
            b = loga_ref[rs, ls]
            shift = 1
            while shift < C:
                b = b + jnp.where(row >= shift, pltpu.roll(b, shift, axis=0), 0.0)
                shift *= 2
            b_last = b[C - 1:C, :]
            k2 = k_ref[rs, ls].astype(F32)
            qd = (q_ref[rs, ls].astype(F32) * scale * jnp.exp(b)).astype(BF16)
            kd = (k2 * jnp.exp(-b)).astype(BF16)
            ke = (k2 * jnp.exp(b_last - b)).astype(BF16)
            zk = jnp.zeros_like(kd)
            kd_blk = jnp.concatenate([jnp.where(lo_k, kd, zk), jnp.where(lo_k, zk, kd)], axis=0)
            attn = lax.dot_general(qd, kd_blk, nt, preferred_element_type=F32)
            lhs_ref[rs, vs] = jnp.concatenate([jnp.where(causal, attn, 0.0).astype(BF16), qd], axis=1)
            kv = lax.dot_general(ke, v_ref[rs, vs], tn, preferred_element_type=F32)
            kv_ref[p, ci] = jnp.where(own, kv, 0.0)
            dcol = jnp.exp(jnp.sum(jnp.where(eye, jnp.broadcast_to(b_last, (LANES, LANES)), 0.0),
                                   axis=1, keepdims=True))
            dec_ref[p, ci] = jnp.broadcast_to(dcol, (LANES, LANES))

    def scan(ci):
        for p in pairs:
            s_prev = s_ref[p]
            sprev_ref[p, ci] = s_prev.astype(BF16)
            s_ref[p] = jnp.tile(dec_ref[p, ci], (1, HEAD_PAIR)) * s_prev + kv_ref[p, ci]

    def output(ci):
        rs = slice(ci * C, (ci + 1) * C)
        for p in pairs:
            vs = slice(p * W, (p + 1) * W)
            v2 = v_ref[rs, vs]
            zv = jnp.zeros_like(v2)
            v_blk = jnp.concatenate([jnp.where(lo_v, v2, zv), jnp.where(lo_v, zv, v2)], axis=0)
            o = jnp.dot(lhs_ref[rs, vs], jnp.concatenate([v_blk, sprev_ref[p, ci]], axis=0),
                        preferred_element_type=F32)
            o2 = o * o
            ms = jnp.where(lo_v, jnp.sum(o2[:, :GLA_DV_PAD], axis=1, keepdims=True),
                           jnp.sum(o2[:, GLA_DV_PAD:], axis=1, keepdims=True))
            on = o * lax.rsqrt(ms * (1.0 / GLA_DV) + EPS) * ng
            og = (on * gate_ref[rs, vs].astype(F32)).astype(BF16)
            for hh in range(HEAD_PAIR):
                c0 = (p * HEAD_PAIR + hh) * GLA_DV
                o_ref[rs, c0:c0 + GLA_DV] = og[:, hh * GLA_DV_PAD:hh * GLA_DV_PAD + GLA_DV]

    def all_chunks(phase):
        return lambda: [phase(ci) for ci in range(n_chunks)]

    return all_chunks(local), all_chunks(scan), all_chunks(output)


def _fox_kernel(qn2_ref, kn2_ref, cend_ref, q_ref, cq_ref, k_ref, cx_ref, v_ref, gate_ref, o_ref,
                m_ref, acc_ref):
    blk = FOX_KEYS
    streams = range(q_ref.shape[0] // blk)
    pair = pl.program_id(0)
    qi = pl.program_id(1)
    nblk = pl.num_programs(1) * len(streams)
    lane = lax.broadcasted_iota(jnp.int32, (1, LANES), 1)
    lo_lanes = lane < FOX_DH
    reps = blk // LANES
    diag = [qi * len(streams) + s for s in streams]
    heads = [pair * HEAD_PAIR + hh for hh in range(HEAD_PAIR)]

    q = q_ref[...]
    zero = jnp.zeros_like(q)
    q_lo, q_hi = jnp.where(lo_lanes, q, zero), jnp.where(lo_lanes, zero, q)
    q_stack = [jnp.concatenate([q_lo[s * blk:(s + 1) * blk], q_hi[s * blk:(s + 1) * blk]], axis=0)
               for s in streams]
    g_row = lax.broadcasted_iota(jnp.int32, (LANES, LANES), 0)
    g_col = lax.broadcasted_iota(jnp.int32, (LANES, LANES), 1)
    xlane = lax.broadcasted_iota(jnp.int32, (HEAD_PAIR * blk, LANES), 1)
    cq = cq_ref[...]

    def lane_map(h, shift):
        g = jnp.where((g_row == CX_ONE) & ((g_col == CX_HI + h) | (g_col == CX_MID + h)
                                           | (g_col == CX_LO + h)), 1.0, 0.0)
        if shift:
            for part, base in enumerate((CX_HI, CX_MID, CX_LO)):
                g = jnp.where((g_row == base + h) & (g_col == CX_ONE + part), -1.0, g)
        return g.astype(BF16)

    def q_aug(s, shift, void=None):
        cs = cq[s * blk:(s + 1) * blk]
        extra = jnp.concatenate([jnp.dot(cs, lane_map(h, shift), preferred_element_type=F32)
                                 for h in heads], axis=0)
        if void is not None:
            extra = jnp.where(jnp.logical_and(xlane == CX_ONE + 3, void), NEG_BIG, extra)
        return jnp.concatenate([q_stack[s], extra.astype(BF16)], axis=1)

    acc_ref[...] = jnp.zeros_like(acc_ref)
    qpos = lax.broadcasted_iota(jnp.int32, (blk, blk), 0)
    kpos = lax.broadcasted_iota(jnp.int32, (blk, blk), 1)

    def block(j):
        ks = pl.ds(pl.multiple_of(jnp.maximum(j, 0) * blk, blk), blk)
        k_aug = jnp.concatenate([k_ref[:, ks], cx_ref[:, ks]], axis=0)
        vb = v_ref[ks, :]
        one = jnp.ones_like(vb)
        return k_aug, (jnp.where(lo_lanes, vb, one), jnp.where(lo_lanes, one, vb))

    def step_online(s, j, qa, masked):
        rows = slice(s * blk, (s + 1) * blk)
        k_aug, vaug = block(j)
        s_all = jnp.dot(qa, k_aug, preferred_element_type=F32)
        for hh in range(HEAD_PAIR):
            sc = s_all[hh * blk:(hh + 1) * blk]
            if masked:
                sc = jnp.where(kpos <= qpos, sc, NEG_BIG)
            m_prev = m_ref[hh, rows]
            m_new = jnp.maximum(m_prev, jnp.max(sc, axis=1, keepdims=True))
            p = jnp.exp(sc - jnp.tile(m_new, (1, reps)))
            alpha = jnp.exp(m_prev - m_new)
            pv = jnp.dot(p.astype(BF16), vaug[hh], preferred_element_type=F32)
            acc_ref[hh, rows] = alpha * acc_ref[hh, rows] + pv
            m_ref[hh, rows] = m_new

    def step_direct(s, j, qa, masked, only=None):
        rows = slice(s * blk, (s + 1) * blk)
        k_aug, vaug = block(j)
        hsel = range(HEAD_PAIR) if only is None else (only,)
        lhs = qa if only is None else qa[only * blk:(only + 1) * blk]
        s_all = jnp.dot(lhs, k_aug, preferred_element_type=F32)
        for n, hh in enumerate(hsel):
            sc = s_all[n * blk:(n + 1) * blk]
            if masked:
                sc = jnp.where(kpos <= qpos, sc, NEG_BIG)
            acc_ref[hh, rows] += jnp.dot(jnp.exp(sc).astype(BF16), vaug[hh],
                                         preferred_element_type=F32)

    k_max2 = [kn2_ref[h] for h in heads]

    def norm2(s, hh):
        stat = (diag[s] * blk // PROJ_ROWS) * LANES
        return (NORM_SLACK * NORM_SLACK) * qn2_ref[stat + heads[hh]] * k_max2[hh]

    n2 = [[norm2(s, hh) for hh in range(HEAD_PAIR)] for s in streams]
    gap0 = [[FOX_SKIP_NATS + cend_ref[heads[hh] * nblk + jnp.maximum(diag[s] - 1, 0)]
             for hh in range(HEAD_PAIR)] for s in streams]

    def live_head(t, hh, direct):
        keep = False
        for s in streams:
            j = diag[s] - 1 - t
            gap = gap0[s][hh] - cend_ref[heads[hh] * nblk + jnp.maximum(j, 0)]
            bound2 = n2[s][hh] if direct else 4.0 * n2[s][hh]
            dead = jnp.logical_and(gap <= 0.0, bound2 <= gap * gap)
            keep = jnp.logical_or(keep, jnp.logical_and(j >= 0, jnp.logical_not(dead)))
        return keep

    def sweeps(stepper, direct):
        def sweep(back, masked=False, only=None):
            for s in streams:
                stepper(s, diag[s] - back, masked, only)

        def loop(t0, cond, only=None):
            def body(t):
                sweep(t + 1, only=only)
                return t + 1
            return lax.while_loop(cond, body, t0)

        def live(t, hh):
            return live_head(t, hh, direct)

        sweep(0, masked=True)
        sweep(1)
        if direct:
            t_both = loop(1, lambda t: jnp.logical_and(live(t, 0), live(t, 1)))
            for hh in range(HEAD_PAIR):
                loop(t_both, lambda t, hh=hh: live(t, hh), only=hh)
        else:
            loop(1, lambda t: jnp.logical_or(live(t, 0), live(t, 1)))

    direct_ok = True
    for s in streams:
        for hh in range(HEAD_PAIR):
            direct_ok = jnp.logical_and(direct_ok, n2[s][hh] <= FOX_DIRECT_NORM2)

    @pl.when(direct_ok)
    def _():
        q_dir = [q_aug(s, shift=True) for s in streams]
        q_void = [q_aug(s, shift=True, void=True) for s in streams]
        sweeps(lambda s, j, masked, only: step_direct(
            s, j, q_dir[s] if masked else jnp.where(j >= 0, q_dir[s], q_void[s]), masked, only),
            direct=True)

    @pl.when(jnp.logical_not(direct_ok))
    def _():
        m_ref[...] = jnp.full_like(m_ref, NEG_BIG)
        sweeps(lambda s, j, masked, only: step_online(
            s, j, q_aug(s, shift=False, void=j < 0), masked), direct=False)

    outs = []
    for hh in range(HEAD_PAIR):
        acc = acc_ref[hh]
        outs.append(acc / pltpu.roll(acc, FOX_DH, axis=1))
    o = jnp.where(lo_lanes, outs[0], outs[1])
    o_ref[...] = (o * gate_ref[...].astype(F32)).astype(BF16)


def _out_kernel(x_ref, gla_ref, fox_ref, mq_ref, mg_ref, mem_ref, mem_g_ref, w_mem_ref,
                w_out_ref, fg_ref, o_ref, wo_ref, mk_ref, mv_ref):
    @pl.when(pl.program_id(0) == 0)
    def _():
        wo_ref[...] = w_out_ref[...].astype(BF16)
        _memkv(mem_ref, mem_g_ref, w_mem_ref, mk_ref, mv_ref)

    lane = lax.broadcasted_iota(jnp.int32, (1, LANES), 1)
    lo_lanes = lane < MEM_DH
    nt = (((1,), (1,)), ((), ()))
    mem_parts = []
    for p in range(MEM_HEADS // HEAD_PAIR):
        ls = slice(p * LANES, (p + 1) * LANES)
        q = mq_ref[:, ls]
        kb = mk_ref[:, ls]
        vb = mv_ref[:, ls]
        zero = jnp.zeros_like(q)
        one = jnp.ones_like(vb)
        qh = (jnp.where(lo_lanes, q, zero), jnp.where(lo_lanes, zero, q))
        vaug = (jnp.where(lo_lanes, vb, one), jnp.where(lo_lanes, one, vb))
        outs = []
        for hh in range(HEAD_PAIR):
            s = lax.dot_general(qh[hh], kb, nt, preferred_element_type=F32)
            pexp = jnp.exp(s - jnp.max(s, axis=1, keepdims=True))
            pv = jnp.dot(pexp.astype(BF16), vaug[hh], preferred_element_type=F32)
            outs.append(pv / pltpu.roll(pv, MEM_DH, axis=1))
        o = jnp.where(lo_lanes, outs[0], outs[1])
        mem_parts.append((o * mg_ref[:, ls].astype(F32)).astype(BF16))
    mixed = jnp.concatenate([gla_ref[...], fox_ref[...]] + mem_parts, axis=1)
    y = x_ref[...] + jnp.dot(mixed, wo_ref[...], preferred_element_type=F32)
    o_ref[...] = y * _rms_scale(y, D_MODEL) * fg_ref[...]


def _params(*sem):
    return pltpu.CompilerParams(dimension_semantics=sem, vmem_limit_bytes=VMEM_LIMIT)


def _layer(x, mem, norm_g, w_in, w_alpha_up, b_alpha, b_forget, gla_norm_g,
           mem_norm_g, w_mem_kv, w_out, out_g):
    w_in_t = jnp.transpose(w_in[None], (0, 2, 1)).reshape(-1, LANES)
    T = x.shape[0]
    M = mem.shape[0]

    def rows(width, n=PROJ_ROWS):
        return pl.BlockSpec((n, width), lambda i: (i, 0))

    def whole(shape):
        return pl.BlockSpec(shape, lambda i: (0,) * len(shape))

    def once(shape):
        return pl.BlockSpec(shape, lambda i: (0,) * len(shape), pipeline_mode=pl.Buffered(1))

    bshape = lambda w: jax.ShapeDtypeStruct((T, w), BF16)
    nproj = T // PROJ_ROWS
    stat_spec = pl.BlockSpec((1, 1, LANES), lambda i: (i, 0, 0))
    stat_shape = jax.ShapeDtypeStruct((nproj, 1, LANES), F32)
    gla_pairs, gla_chunks = GLA_HEADS // HEAD_PAIR, PROJ_ROWS // GLA_CHUNK
    pair_w = HEAD_PAIR * GLA_DV_PAD
    (gla, fqvg, kx, mqg, crow, cq, qn2, kn2) = pl.pallas_call(
        _proj_kernel,
        grid=(nproj,),
        in_specs=[rows(D_MODEL), whole((1, D_MODEL)),
                  pl.BlockSpec(w_in_t.shape, lambda i: (0, 0), pipeline_mode=pl.Buffered(1)),
                  whole(w_alpha_up.shape), whole((1, GLA_HEADS * GLA_DK)),
                  whole((1, FOX_HEADS)), whole((1, GLA_DV))],
        out_specs=[rows(GLA_OUT_W), rows(3 * FOX_W),
                   pl.BlockSpec((FOX_W + LANES, PROJ_ROWS), lambda i: (0, i)),
                   rows(2 * MEM_W),
                   pl.BlockSpec((SUBLANES, PROJ_ROWS), lambda i: (0, i)), rows(LANES),
                   stat_spec, pl.BlockSpec((1, 1, LANES), lambda i: (0, 0, 0))],
        out_shape=[bshape(GLA_OUT_W), bshape(3 * FOX_W),
                   jax.ShapeDtypeStruct((FOX_W + LANES, T), BF16),
                   bshape(2 * MEM_W),
                   jax.ShapeDtypeStruct((SUBLANES, T), F32), bshape(LANES),
                   stat_shape, jax.ShapeDtypeStruct((1, 1, LANES), F32)],
        scratch_shapes=[
            pltpu.VMEM((SUBLANES, LANES), F32),
            pltpu.VMEM((IN_COLS_PAD, D_MODEL), BF16),
            pltpu.VMEM((SMALL_W, GLA_QK_W), F32),
            pltpu.VMEM((1, GLA_QK_W), F32),
            pltpu.VMEM((1, SMALL_W), F32),
            pltpu.VMEM((1, GLA_DV_PAD), F32),
            pltpu.VMEM((FOX_W, LANES), BF16),
            pltpu.VMEM((PROJ_ROWS, SMALL_W), F32),
            pltpu.VMEM((PROJ_ROWS, GLA_QK_W), BF16),
            pltpu.VMEM((PROJ_ROWS, GLA_QK_W), BF16),
            pltpu.VMEM((PROJ_ROWS, GLA_V_W), BF16),
            pltpu.VMEM((PROJ_ROWS, GLA_V_W), BF16),
            pltpu.VMEM((PROJ_ROWS, GLA_QK_W), F32),
            pltpu.VMEM((gla_pairs, LANES, pair_w), F32),
            pltpu.VMEM((PROJ_ROWS, GLA_V_W), BF16),
            pltpu.VMEM((gla_pairs, gla_chunks, LANES, pair_w), F32),
            pltpu.VMEM((gla_pairs, gla_chunks, LANES, LANES), F32),
            pltpu.VMEM((gla_pairs, gla_chunks, LANES, pair_w), BF16)],
        compiler_params=_params("arbitrary"),
        name="proj",
    )(x, norm_g[None, :], w_in_t, w_alpha_up, b_alpha[None, :], b_forget[None, :],
      gla_norm_g[None, :])

    cend = crow[:FOX_HEADS, FOX_KEYS - 1::FOX_KEYS].reshape(-1)
    fox_pairs = FOX_HEADS // HEAD_PAIR
    pair_rows = pl.BlockSpec((FOX_BLOCK, LANES), lambda p, i, *_: (i, p))
    fox = pl.pallas_call(
        _fox_kernel,
        grid_spec=pltpu.PrefetchScalarGridSpec(
            num_scalar_prefetch=3,
            grid=(fox_pairs, T // FOX_BLOCK),
            in_specs=[pair_rows,
                      pl.BlockSpec((FOX_BLOCK, LANES), lambda p, i, *_: (i, 0)),
                      pl.BlockSpec((LANES, T), lambda p, i, *_: (p, 0)),
                      pl.BlockSpec((LANES, T), lambda p, i, *_: (fox_pairs, 0)),
                      pl.BlockSpec((T, LANES), lambda p, i, *_: (0, fox_pairs + p)),
                      pl.BlockSpec((FOX_BLOCK, LANES),
                                   lambda p, i, *_: (i, 2 * fox_pairs + p))],
            out_specs=pair_rows,
            scratch_shapes=[pltpu.VMEM((HEAD_PAIR, FOX_BLOCK, LANES), F32),
                            pltpu.VMEM((HEAD_PAIR, FOX_BLOCK, LANES), F32)]),
        out_shape=bshape(FOX_W),
        compiler_params=_params("arbitrary", "arbitrary"),
        name="fox",
    )(qn2.reshape(-1), kn2.reshape(-1), cend, fqvg, cq, kx, kx, fqvg, fqvg)

    out = pl.pallas_call(
        _out_kernel,
        grid=(T // OUT_ROWS,),
        in_specs=[rows(D_MODEL, OUT_ROWS), rows(GLA_OUT_W, OUT_ROWS), rows(FOX_W, OUT_ROWS),
                  pl.BlockSpec((OUT_ROWS, MEM_W), lambda i: (i, 0)),
                  pl.BlockSpec((OUT_ROWS, MEM_W), lambda i: (i, 1)),
                  once(mem.shape), once((1, D_MODEL)), once(w_mem_kv.shape), once(w_out.shape),
                  whole((1, D_MODEL))],
        out_specs=rows(D_MODEL, OUT_ROWS),
        out_shape=jax.ShapeDtypeStruct((T, D_MODEL), F32),
        scratch_shapes=[pltpu.VMEM((D_MODEL, D_MODEL), BF16),
                        pltpu.VMEM((M, MEM_W), BF16), pltpu.VMEM((M, MEM_W), BF16)],
        compiler_params=_params("arbitrary"),
        name="out",
    )(x, gla, fox, mqg, mqg, mem, mem_norm_g[None, :], w_mem_kv, w_out, out_g[None, :])
    return out


def kernel(x, mem, norm_g, w_in, w_alpha_up, b_alpha, b_forget, gla_norm_g, mem_norm_g,
           w_mem_kv, w_out, final_norm_g):
    assert x.shape[0] == 1 and mem.shape[0] == 1 and norm_g.shape[0] == 1
    assert x.shape[1] % max(PROJ_ROWS, FOX_BLOCK, OUT_ROWS) == 0
    out = _layer(x[0], mem[0], norm_g[0], w_in[0], w_alpha_up[0], b_alpha[0], b_forget[0],
                 gla_norm_g[0], mem_norm_g[0], w_mem_kv[0], w_out[0], final_norm_g)
    return out[None]
```

```python
import jax
import jax.numpy as jnp
from jax import lax
from jax.experimental import pallas as pl
from jax.experimental.pallas import tpu as pltpu

F32 = jnp.float32
BF16 = jnp.bfloat16

EPS = 1e-6
LANES = 128
SUBLANES = 8

D_MODEL = 1024
GLA_HEADS, GLA_DK, GLA_DV, GLA_RANK = 4, 48, 96, 16
GLA_DK_PAD = 64
GLA_DV_PAD = LANES
GLA_GATE_NORM = 16.0
GLA_CHUNK = 64
FOX_HEADS, FOX_DH = 6, 64
MEM_HEADS, MEM_DH = 4, 64
HEAD_PAIR = 2
GLA_QK_W = GLA_HEADS * GLA_DK_PAD
GLA_V_W = GLA_HEADS * GLA_DV_PAD
GLA_OUT_W = GLA_HEADS * GLA_DV
FOX_W = FOX_HEADS * FOX_DH
MEM_W = MEM_HEADS * MEM_DH
SMALL_W = LANES
FG_LANE0 = 0
LR_LANE0 = SUBLANES

_GROUPS = (("gq", GLA_QK_W), ("gk", GLA_QK_W), ("gv", GLA_OUT_W), ("gg", GLA_OUT_W),
           ("fq", FOX_W), ("fk", FOX_W), ("fv", FOX_W), ("fgate", FOX_W),
           ("mq", MEM_W), ("mg", MEM_W))
FG_COL = GLA_DK
LR_COL = GLA_DK_PAD + GLA_DK
_OFF = {}
_o = 0
for _n, _w in _GROUPS:
    _OFF[_n] = (_o, _o + _w)
    _o += _w
IN_COLS_PAD = _o

PROJ_ROWS = 1024
FOX_BLOCK = 4096
FOX_KEYS = 256
OUT_ROWS = 1024
VMEM_LIMIT = 56 * 1024 * 1024

NEG_BIG = -1e30
FOX_SKIP_NATS = 105.0
NORM_SLACK = 1.02
FOX_DIRECT_NORM2 = 3600.0
CX_HI, CX_MID, CX_LO, CX_ONE = 0, 8, 16, 24
CX_ROWS = 32


def _log_sigmoid(z):
    return jnp.minimum(z, 0.0) - jnp.log(1.0 + jnp.exp(-jnp.abs(z)))


def _silu(z):
    return z / (1.0 + jnp.exp(-z))


def _rms_scale(v, width):
    return lax.rsqrt(jnp.sum(v * v, axis=-1, keepdims=True) * (1.0 / width) + EPS)


def _w_in_segments():
    qk, gw = GLA_HEADS * GLA_DK, GLA_HEADS * GLA_DV
    src = {}
    o = 0
    for name, width in (("gq", qk), ("gk", qk), ("gv", gw), ("lr", GLA_RANK), ("gg", gw),
                        ("fq", FOX_W), ("fk", FOX_W), ("fv", FOX_W), ("fg", FOX_HEADS),
                        ("fgate", FOX_W), ("mq", MEM_W), ("mg", MEM_W)):
        src[name] = o
        o += width
    segs = []
    for name, d, d_pad in (("gq", GLA_DK, GLA_DK_PAD), ("gk", GLA_DK, GLA_DK_PAD)):
        segs += [(src[name] + h * d, _OFF[name][0] + h * d_pad, d) for h in range(GLA_HEADS)]
    segs += [(src[name], _OFF[name][0], _OFF[name][1] - _OFF[name][0])
             for name in ("gv", "gg", "fq", "fk", "fv", "fgate", "mq", "mg")]
    segs += [(src["fg"], _OFF["gq"][0] + FG_COL, FOX_HEADS),
             (src["lr"], _OFF["gq"][0] + LR_COL, GLA_RANK)]
    return tuple(segs)


def _proj_kernel(x_ref, g_ref, w_in_ref, w_alpha_ref, b_alpha_ref, b_forget_ref, gla_g_ref,
                 gla_ref, fqvg_ref, kx_ref, mqg_ref, crow_ref, cq_ref, qn2_ref, kn2_ref,
                 carry_ref, wt_ref, wa_ref, ba_ref, bf_ref, ng_ref, seg_ref, small_ref,
                 gq_ref, gk_ref, gv_ref, gg_ref, loga_ref,
                 s_ref, lhs_ref, kv_ref, dec_ref, sprev_ref):
    rows = x_ref.shape[0]
    k_chunks = D_MODEL // LANES

    @pl.when(pl.program_id(0) == 0)
    def _():
        carry_ref[...] = jnp.zeros_like(carry_ref)
        s_ref[...] = jnp.zeros_like(s_ref)
        kn2_ref[...] = jnp.zeros_like(kn2_ref)
        gv_ref[...] = jnp.zeros_like(gv_ref)
        gg_ref[...] = jnp.zeros_like(gg_ref)
        small_ref[...] = jnp.zeros_like(small_ref)
        wa_ref[...] = jnp.zeros_like(wa_ref)
        ba_ref[...] = jnp.zeros_like(ba_ref)
        bf_ref[...] = jnp.zeros_like(bf_ref)
        ng_ref[...] = jnp.zeros_like(ng_ref)
        for h in range(GLA_HEADS):
            src = slice(h * GLA_DK, (h + 1) * GLA_DK)
            dst = slice(h * GLA_DK_PAD, h * GLA_DK_PAD + GLA_DK)
            wa_ref[LR_LANE0:LR_LANE0 + GLA_RANK, dst] = w_alpha_ref[:, src]
            ba_ref[:, dst] = b_alpha_ref[:, src]
        bf_ref[:, FG_LANE0:FG_LANE0 + FOX_HEADS] = b_forget_ref[...]
        ng_ref[:, 0:GLA_DV] = gla_g_ref[...]
        seg_ref[...] = (lax.broadcasted_iota(jnp.int32, seg_ref.shape, 0) // FOX_DH
                        == lax.broadcasted_iota(jnp.int32, seg_ref.shape, 1)).astype(BF16)
        wt_ref[...] = jnp.zeros_like(wt_ref)
        for s0, d0, width in _w_in_segments():
            for c in range(k_chunks):
                wt_ref[d0:d0 + width, c * LANES:(c + 1) * LANES] = (
                    w_in_ref[pl.ds(s0 * k_chunks + c, width, stride=k_chunks), :].astype(BF16))

    x = x_ref[...]
    xn = (x * _rms_scale(x, D_MODEL) * g_ref[...]).astype(BF16)
    nt = (((1,), (1,)), ((), ()))

    def proj(first, last):
        lo, hi = _OFF[first][0], _OFF[last][1]
        y = lax.dot_general(xn, wt_ref[lo:hi, :], nt, preferred_element_type=F32)
        return lambda name: y[:, _OFF[name][0] - lo:_OFF[name][1] - lo]

    gla = proj("gq", "gg")
    tail = proj("mq", "mg")
    gq_all = gla("gq")
    small_ref[:, FG_LANE0:FG_LANE0 + SUBLANES] = gq_all[:, FG_COL:FG_COL + SUBLANES]
    small_ref[:, LR_LANE0:LR_LANE0 + GLA_RANK] = gq_all[:, LR_COL:LR_COL + GLA_RANK]
    small = small_ref[...]
    logf = _log_sigmoid(small + bf_ref[...])
    c = logf.T[0:SUBLANES, :]
    lane = lax.broadcasted_iota(jnp.int32, c.shape, 1)
    shift = 1
    while shift < rows:
        c = c + jnp.where(lane >= shift, pltpu.roll(c, shift, axis=1), 0.0)
        shift *= 2
    c = c + carry_ref[:, 0:1]
    crow_ref[...] = c
    carry_ref[...] = jnp.broadcast_to(c[:, rows - 1:rows], carry_ref.shape)
    neg = -c
    hi = neg.astype(BF16).astype(F32)
    mid = (neg - hi).astype(BF16).astype(F32)
    low = neg - hi - mid
    parts = jnp.concatenate(
        [hi, mid, low, jnp.ones_like(c), jnp.zeros((LANES - 4 * SUBLANES, rows), F32)], axis=0)
    kx_ref[FOX_W:, :] = parts.astype(BF16)
    cq_ref[...] = parts.T.astype(BF16)

    z = (jnp.dot(small.astype(BF16), wa_ref[...].astype(BF16), preferred_element_type=F32)
         + ba_ref[...])
    loga_ref[...] = _log_sigmoid(z) * (1.0 / GLA_GATE_NORM)
    q_lane = lax.broadcasted_iota(jnp.int32, (1, GLA_QK_W), 1)
    gq_ref[...] = jnp.where(q_lane % GLA_DK_PAD < GLA_DK, gq_all, 0.0).astype(BF16)
    gk_ref[...] = gla("gk").astype(BF16)
    gv, gg = gla("gv").astype(BF16), _silu(gla("gg")).astype(BF16)
    for h in range(GLA_HEADS):
        src = slice(h * GLA_DV, (h + 1) * GLA_DV)
        dst = slice(h * GLA_DV_PAD, h * GLA_DV_PAD + GLA_DV)
        gv_ref[:, dst] = gv[:, src]
        gg_ref[:, dst] = gg[:, src]
    gla_local, gla_scan, gla_output = _gla_block(
        gq_ref, gk_ref, gv_ref, loga_ref, gg_ref, ng_ref, gla_ref,
        s_ref, lhs_ref, kv_ref, dec_ref, sprev_ref)

    def max_sq_norm(v):
        v32 = v.astype(F32)
        n2 = jnp.dot((v32 * v32).astype(BF16), seg_ref[...], preferred_element_type=F32)
        return jnp.max(n2, axis=0, keepdims=True)

    gla_local()
    fox_qk = proj("fq", "fk")
    mqg_ref[:, :MEM_W] = (tail("mq") * MEM_DH ** -0.5).astype(BF16)
    mqg_ref[:, MEM_W:] = _silu(tail("mg")).astype(BF16)
    gla_scan()
    fq = (fox_qk("fq") * FOX_DH ** -0.5).astype(BF16)
    fk = fox_qk("fk").astype(BF16)
    fqvg_ref[:, :FOX_W] = fq
    kx_ref[:FOX_W, :] = fox_qk("fk").T.astype(BF16)
    gla_output()
    fox_vg = proj("fv", "fgate")
    fqvg_ref[:, FOX_W:2 * FOX_W] = fox_vg("fv").astype(BF16)
    fqvg_ref[:, 2 * FOX_W:] = _silu(fox_vg("fgate")).astype(BF16)
    qn2_ref[0] = max_sq_norm(fq)
    kn2_ref[0] = jnp.maximum(kn2_ref[0], max_sq_norm(fk))


def _memkv(mem_ref, g_ref, w_ref, mk_ref, mv_ref):
    m = mem_ref[...]
    mn = (m * _rms_scale(m, D_MODEL) * g_ref[...]).astype(BF16)
    kv = jnp.dot(mn, w_ref[...].astype(BF16), preferred_element_type=F32)
    mk_ref[...] = kv[:, :MEM_W].astype(BF16)
    mv_ref[...] = kv[:, MEM_W:].astype(BF16)


def _gla_block(q_ref, k_ref, v_ref, loga_ref, gate_ref, ng_ref, o_ref,
               s_ref, lhs_ref, kv_ref, dec_ref, sprev_ref):
    C = GLA_CHUNK
    W = HEAD_PAIR * GLA_DV_PAD
    n_chunks = q_ref.shape[0] // C

    row = lax.broadcasted_iota(jnp.int32, (C, LANES), 0)
    lane = lax.broadcasted_iota(jnp.int32, (C, LANES), 1)
    lo_k = lane < GLA_DK_PAD
    causal = row >= jnp.where(lo_k, lane, lane - GLA_DK_PAD)
    lo_v = lax.broadcasted_iota(jnp.int32, (C, W), 1) < GLA_DV_PAD
    st_row = lax.broadcasted_iota(jnp.int32, (LANES, W), 0)
    st_lane = lax.broadcasted_iota(jnp.int32, (LANES, W), 1)
    own = (st_row < GLA_DK_PAD) == (st_lane < GLA_DV_PAD)
    eye = (lax.broadcasted_iota(jnp.int32, (LANES, LANES), 0)
           == lax.broadcasted_iota(jnp.int32, (LANES, LANES), 1))
    scale = GLA_DK ** -0.5
    nt = (((1,), (1,)), ((), ()))
    tn = (((0,), (0,)), ((), ()))
    ng = jnp.concatenate([ng_ref[...]] * HEAD_PAIR, axis=1)

    pairs = range(GLA_HEADS // HEAD_PAIR)

    def local(ci):
        rs = slice(ci * C, (ci + 1) * C)
        for p in pairs:
            ls = slice(p * LANES, (p + 1) * LANES)
            vs = slice(p * W, (p + 1) * W)
            b = loga_ref[rs, ls]
            shift = 1
            while shift < C:
                b = b + jnp.where(row >= shift, pltpu.roll(b, shift, axis=0), 0.0)
                shift *= 2
            b_last = b[C - 1:C, :]
            k2 = k_ref[rs, ls].astype(F32)
            qd = (q_ref[rs, ls].astype(F32) * scale * jnp.exp(b)).astype(BF16)
            kd = (k2 * jnp.exp(-b)).astype(BF16)
            ke = (k2 * jnp.exp(b_last - b)).astype(BF16)
            zk = jnp.zeros_like(kd)
            kd_blk = jnp.concatenate([jnp.where(lo_k, kd, zk), jnp.where(lo_k, zk, kd)], axis=0)
            attn = lax.dot_general(qd, kd_blk, nt, preferred_element_type=F32)
            lhs_ref[rs, vs] = jnp.concatenate([jnp.where(causal, attn, 0.0).astype(BF16), qd], axis=1)
            kv = lax.dot_general(ke, v_ref[rs, vs], tn, preferred_element_type=F32)
            kv_ref[p, ci] = jnp.where(own, kv, 0.0)
            dcol = jnp.exp(jnp.sum(jnp.where(eye, jnp.broadcast_to(b_last, (LANES, LANES)), 0.0),
                                   axis=1, keepdims=True))
            dec_ref[p, ci] = jnp.broadcast_to(dcol, (LANES, LANES))

    def scan(ci):
        for p in pairs:
            s_prev = s_ref[p]
            sprev_ref[p, ci] = s_prev.astype(BF16)
            s_ref[p] = jnp.tile(dec_ref[p, ci], (1, HEAD_PAIR)) * s_prev + kv_ref[p, ci]

    def output(ci):
        rs = slice(ci * C, (ci + 1) * C)
        for p in pairs:
            vs = slice(p * W, (p + 1) * W)
            v2 = v_ref[rs, vs]
            zv = jnp.zeros_like(v2)
            v_blk = jnp.concatenate([jnp.where(lo_v, v2, zv), jnp.where(lo_v, zv, v2)], axis=0)
            o = jnp.dot(lhs_ref[rs, vs], jnp.concatenate([v_blk, sprev_ref[p, ci]], axis=0),
                        preferred_element_type=F32)
            o2 = o * o
            ms = jnp.where(lo_v, jnp.sum(o2[:, :GLA_DV_PAD], axis=1, keepdims=True),
                           jnp.sum(o2[:, GLA_DV_PAD:], axis=1, keepdims=True))
            on = o * lax.rsqrt(ms * (1.0 / GLA_DV) + EPS) * ng
            og = (on * gate_ref[rs, vs].astype(F32)).astype(BF16)
            for hh in range(HEAD_PAIR):
                c0 = (p * HEAD_PAIR + hh) * GLA_DV
                o_ref[rs, c0:c0 + GLA_DV] = og[:, hh * GLA_DV_PAD:hh * GLA_DV_PAD + GLA_DV]

    def all_chunks(phase):
        return lambda: [phase(ci) for ci in range(n_chunks)]

    return all_chunks(local), all_chunks(scan), all_chunks(output)


def _fox_kernel(qn2_ref, kn2_ref, cend_ref, q_ref, cq_ref, k_ref, cx_ref, v_ref, gate_ref, o_ref,
                m_ref, acc_ref):
    blk = FOX_KEYS
    streams = range(q_ref.shape[0] // blk)
    pair = pl.program_id(0)
    qi = pl.program_id(1)
    nblk = pl.num_programs(1) * len(streams)
    lane = lax.broadcasted_iota(jnp.int32, (1, LANES), 1)
    lo_lanes = lane < FOX_DH
    reps = blk // LANES
    diag = [qi * len(streams) + s for s in streams]
    heads = [pair * HEAD_PAIR + hh for hh in range(HEAD_PAIR)]

    q = q_ref[...]
    zero = jnp.zeros_like(q)
    q_lo, q_hi = jnp.where(lo_lanes, q, zero), jnp.where(lo_lanes, zero, q)
    q_stack = [jnp.concatenate([q_lo[s * blk:(s + 1) * blk], q_hi[s * blk:(s + 1) * blk]], axis=0)
               for s in streams]
    g_row = lax.broadcasted_iota(jnp.int32, (LANES, LANES), 0)
    g_col = lax.broadcasted_iota(jnp.int32, (LANES, LANES), 1)
    xlane = lax.broadcasted_iota(jnp.int32, (HEAD_PAIR * blk, LANES), 1)
    cq = cq_ref[...]

    def lane_map(h, shift):
        g = jnp.where((g_row == CX_ONE) & ((g_col == CX_HI + h) | (g_col == CX_MID + h)
                                           | (g_col == CX_LO + h)), 1.0, 0.0)
        if shift:
            for part, base in enumerate((CX_HI, CX_MID, CX_LO)):
                g = jnp.where((g_row == base + h) & (g_col == CX_ONE + part), -1.0, g)
        return g.astype(BF16)

    def q_aug(s, shift, void=None):
        cs = cq[s * blk:(s + 1) * blk]
        extra = jnp.concatenate([jnp.dot(cs, lane_map(h, shift), preferred_element_type=F32)
                                 for h in heads], axis=0)
        if void is not None:
            extra = jnp.where(jnp.logical_and(xlane == CX_ONE + 3, void), NEG_BIG, extra)
        return jnp.concatenate([q_stack[s], extra[:, :CX_ROWS].astype(BF16)], axis=1)

    acc_ref[...] = jnp.zeros_like(acc_ref)
    qpos = lax.broadcasted_iota(jnp.int32, (blk, blk), 0)
    kpos = lax.broadcasted_iota(jnp.int32, (blk, blk), 1)

    def block(j):
        ks = pl.ds(pl.multiple_of(jnp.maximum(j, 0) * blk, blk), blk)
        k_aug = jnp.concatenate([k_ref[:, ks], cx_ref[0:CX_ROWS, ks]], axis=0)
        vb = v_ref[ks, :]
        one = jnp.ones_like(vb)
        return k_aug, (jnp.where(lo_lanes, vb, one), jnp.where(lo_lanes, one, vb))

    def step_online(s, j, qa, masked):
        rows = slice(s * blk, (s + 1) * blk)
        k_aug, vaug = block(j)
        s_all = jnp.dot(qa, k_aug, preferred_element_type=F32)
        for hh in range(HEAD_PAIR):
            sc = s_all[hh * blk:(hh + 1) * blk]
            if masked:
                sc = jnp.where(kpos <= qpos, sc, NEG_BIG)
            m_prev = m_ref[hh, rows]
            m_new = jnp.maximum(m_prev, jnp.max(sc, axis=1, keepdims=True))
            p = jnp.exp(sc - jnp.tile(m_new, (1, reps)))
            alpha = jnp.exp(m_prev - m_new)
            pv = jnp.dot(p.astype(BF16), vaug[hh], preferred_element_type=F32)
            acc_ref[hh, rows] = alpha * acc_ref[hh, rows] + pv
            m_ref[hh, rows] = m_new

    def step_direct(s, j, qa, masked, only=None):
        rows = slice(s * blk, (s + 1) * blk)
        k_aug, vaug = block(j)
        hsel = range(HEAD_PAIR) if only is None else (only,)
        lhs = qa if only is None else qa[only * blk:(only + 1) * blk]
        s_all = jnp.dot(lhs, k_aug, preferred_element_type=F32)
        for n, hh in enumerate(hsel):
            sc = s_all[n * blk:(n + 1) * blk]
            if masked:
                sc = jnp.where(kpos <= qpos, sc, NEG_BIG)
            acc_ref[hh, rows] += jnp.dot(jnp.exp(sc).astype(BF16), vaug[hh],
                                         preferred_element_type=F32)

    k_max2 = [kn2_ref[h] for h in heads]

    def norm2(s, hh):
        stat = (diag[s] * blk // PROJ_ROWS) * LANES
        return (NORM_SLACK * NORM_SLACK) * qn2_ref[stat + heads[hh]] * k_max2[hh]

    n2 = [[norm2(s, hh) for hh in range(HEAD_PAIR)] for s in streams]
    gap0 = [[FOX_SKIP_NATS + cend_ref[heads[hh] * nblk + jnp.maximum(diag[s] - 1, 0)]
             for hh in range(HEAD_PAIR)] for s in streams]

    def live_head(t, hh, direct):
        keep = False
        for s in streams:
            j = diag[s] - 1 - t
            gap = gap0[s][hh] - cend_ref[heads[hh] * nblk + jnp.maximum(j, 0)]
            bound2 = n2[s][hh] if direct else 4.0 * n2[s][hh]
            dead = jnp.logical_and(gap <= 0.0, bound2 <= gap * gap)
            keep = jnp.logical_or(keep, jnp.logical_and(j >= 0, jnp.logical_not(dead)))
        return keep

    def sweeps(stepper, direct):
        def sweep(back, masked=False, only=None):
            for s in streams:
                stepper(s, diag[s] - back, masked, only)

        def loop(t0, cond, only=None):
            def body(t):
                sweep(t + 1, only=only)
                return t + 1
            return lax.while_loop(cond, body, t0)

        def live(t, hh):
            return live_head(t, hh, direct)

        sweep(0, masked=True)
        sweep(1)
        if direct:
            t_both = loop(1, lambda t: jnp.logical_and(live(t, 0), live(t, 1)))
            for hh in range(HEAD_PAIR):
                loop(t_both, lambda t, hh=hh: live(t, hh), only=hh)
        else:
            loop(1, lambda t: jnp.logical_or(live(t, 0), live(t, 1)))

    direct_ok = True
    for s in streams:
        for hh in range(HEAD_PAIR):
            direct_ok = jnp.logical_and(direct_ok, n2[s][hh] <= FOX_DIRECT_NORM2)

    @pl.when(direct_ok)
    def _():
        q_dir = [q_aug(s, shift=True) for s in streams]
        q_void = [q_aug(s, shift=True, void=True) for s in streams]
        sweeps(lambda s, j, masked, only: step_direct(
            s, j, q_dir[s] if masked else jnp.where(j >= 0, q_dir[s], q_void[s]), masked, only),
            direct=True)

    @pl.when(jnp.logical_not(direct_ok))
    def _():
        m_ref[...] = jnp.full_like(m_ref, NEG_BIG)
        sweeps(lambda s, j, masked, only: step_online(
            s, j, q_aug(s, shift=False, void=j < 0), masked), direct=False)

    outs = []
    for hh in range(HEAD_PAIR):
        acc = acc_ref[hh]
        outs.append(acc / pltpu.roll(acc, FOX_DH, axis=1))
    o = jnp.where(lo_lanes, outs[0], outs[1])
    o_ref[...] = (o * gate_ref[...].astype(F32)).astype(BF16)


def _out_kernel(x_ref, gla_ref, fox_ref, mq_ref, mg_ref, mem_ref, mem_g_ref, w_mem_ref,
                w_out_ref, fg_ref, o_ref, wo_ref, mk_ref, mv_ref):
    @pl.when(pl.program_id(0) == 0)
    def _():
        wo_ref[...] = w_out_ref[...].astype(BF16)
        _memkv(mem_ref, mem_g_ref, w_mem_ref, mk_ref, mv_ref)

    lane = lax.broadcasted_iota(jnp.int32, (1, LANES), 1)
    lo_lanes = lane < MEM_DH
    nt = (((1,), (1,)), ((), ()))
    mem_parts = []
    for p in range(MEM_HEADS // HEAD_PAIR):
        ls = slice(p * LANES, (p + 1) * LANES)
        q = mq_ref[:, ls]
        kb = mk_ref[:, ls]
        vb = mv_ref[:, ls]
        zero = jnp.zeros_like(q)
        one = jnp.ones_like(vb)
        qh = (jnp.where(lo_lanes, q, zero), jnp.where(lo_lanes, zero, q))
        vaug = (jnp.where(lo_lanes, vb, one), jnp.where(lo_lanes, one, vb))
        outs = []
        for hh in range(HEAD_PAIR):
            s = lax.dot_general(qh[hh], kb, nt, preferred_element_type=F32)
            pexp = jnp.exp(s - jnp.max(s, axis=1, keepdims=True))
            pv = jnp.dot(pexp.astype(BF16), vaug[hh], preferred_element_type=F32)
            outs.append(pv / pltpu.roll(pv, MEM_DH, axis=1))
        o = jnp.where(lo_lanes, outs[0], outs[1])
        mem_parts.append((o * mg_ref[:, ls].astype(F32)).astype(BF16))
    mixed = jnp.concatenate([gla_ref[...], fox_ref[...]] + mem_parts, axis=1)
    y = x_ref[...] + jnp.dot(mixed, wo_ref[...], preferred_element_type=F32)
    o_ref[...] = y * _rms_scale(y, D_MODEL) * fg_ref[...]


def _params(*sem):
    return pltpu.CompilerParams(dimension_semantics=sem, vmem_limit_bytes=VMEM_LIMIT)


def _layer(x, mem, norm_g, w_in, w_alpha_up, b_alpha, b_forget, gla_norm_g,
           mem_norm_g, w_mem_kv, w_out, out_g):
    w_in_t = jnp.transpose(w_in[None], (0, 2, 1)).reshape(-1, LANES)
    T = x.shape[0]
    M = mem.shape[0]

    def rows(width, n=PROJ_ROWS):
        return pl.BlockSpec((n, width), lambda i: (i, 0))

    def whole(shape):
        return pl.BlockSpec(shape, lambda i: (0,) * len(shape))

    def once(shape):
        return pl.BlockSpec(shape, lambda i: (0,) * len(shape), pipeline_mode=pl.Buffered(1))

    bshape = lambda w: jax.ShapeDtypeStruct((T, w), BF16)
    nproj = T // PROJ_ROWS
    stat_spec = pl.BlockSpec((1, 1, LANES), lambda i: (i, 0, 0))
    stat_shape = jax.ShapeDtypeStruct((nproj, 1, LANES), F32)
    gla_pairs, gla_chunks = GLA_HEADS // HEAD_PAIR, PROJ_ROWS // GLA_CHUNK
    pair_w = HEAD_PAIR * GLA_DV_PAD
    (gla, fqvg, kx, mqg, crow, cq, qn2, kn2) = pl.pallas_call(
        _proj_kernel,
        grid=(nproj,),
        in_specs=[rows(D_MODEL), whole((1, D_MODEL)),
                  pl.BlockSpec(w_in_t.shape, lambda i: (0, 0), pipeline_mode=pl.Buffered(1)),
                  whole(w_alpha_up.shape), whole((1, GLA_HEADS * GLA_DK)),
                  whole((1, FOX_HEADS)), whole((1, GLA_DV))],
        out_specs=[rows(GLA_OUT_W), rows(3 * FOX_W),
                   pl.BlockSpec((FOX_W + LANES, PROJ_ROWS), lambda i: (0, i)),
                   rows(2 * MEM_W),
                   pl.BlockSpec((SUBLANES, PROJ_ROWS), lambda i: (0, i)), rows(LANES),
                   stat_spec, pl.BlockSpec((1, 1, LANES), lambda i: (0, 0, 0))],
        out_shape=[bshape(GLA_OUT_W), bshape(3 * FOX_W),
                   jax.ShapeDtypeStruct((FOX_W + LANES, T), BF16),
                   bshape(2 * MEM_W),
                   jax.ShapeDtypeStruct((SUBLANES, T), F32), bshape(LANES),
                   stat_shape, jax.ShapeDtypeStruct((1, 1, LANES), F32)],
        scratch_shapes=[
            pltpu.VMEM((SUBLANES, LANES), F32),
            pltpu.VMEM((IN_COLS_PAD, D_MODEL), BF16),
            pltpu.VMEM((SMALL_W, GLA_QK_W), F32),
            pltpu.VMEM((1, GLA_QK_W), F32),
            pltpu.VMEM((1, SMALL_W), F32),
            pltpu.VMEM((1, GLA_DV_PAD), F32),
            pltpu.VMEM((FOX_W, LANES), BF16),
            pltpu.VMEM((PROJ_ROWS, SMALL_W), F32),
            pltpu.VMEM((PROJ_ROWS, GLA_QK_W), BF16),
            pltpu.VMEM((PROJ_ROWS, GLA_QK_W), BF16),
            pltpu.VMEM((PROJ_ROWS, GLA_V_W), BF16),
            pltpu.VMEM((PROJ_ROWS, GLA_V_W), BF16),
            pltpu.VMEM((PROJ_ROWS, GLA_QK_W), F32),
            pltpu.VMEM((gla_pairs, LANES, pair_w), F32),
            pltpu.VMEM((PROJ_ROWS, GLA_V_W), BF16),
            pltpu.VMEM((gla_pairs, gla_chunks, LANES, pair_w), F32),
            pltpu.VMEM((gla_pairs, gla_chunks, LANES, LANES), F32),
            pltpu.VMEM((gla_pairs, gla_chunks, LANES, pair_w), BF16)],
        compiler_params=_params("arbitrary"),
        name="proj",
    )(x, norm_g[None, :], w_in_t, w_alpha_up, b_alpha[None, :], b_forget[None, :],
      gla_norm_g[None, :])

    cend = crow[:FOX_HEADS, FOX_KEYS - 1::FOX_KEYS].reshape(-1)
    fox_pairs = FOX_HEADS // HEAD_PAIR
    pair_rows = pl.BlockSpec((FOX_BLOCK, LANES), lambda p, i, *_: (i, p))
    fox = pl.pallas_call(
        _fox_kernel,
        grid_spec=pltpu.PrefetchScalarGridSpec(
            num_scalar_prefetch=3,
            grid=(fox_pairs, T // FOX_BLOCK),
            in_specs=[pair_rows,
                      pl.BlockSpec((FOX_BLOCK, LANES), lambda p, i, *_: (i, 0)),
                      pl.BlockSpec((LANES, T), lambda p, i, *_: (p, 0)),
                      pl.BlockSpec((LANES, T), lambda p, i, *_: (fox_pairs, 0)),
                      pl.BlockSpec((T, LANES), lambda p, i, *_: (0, fox_pairs + p)),
                      pl.BlockSpec((FOX_BLOCK, LANES),
                                   lambda p, i, *_: (i, 2 * fox_pairs + p))],
            out_specs=pair_rows,
            scratch_shapes=[pltpu.VMEM((HEAD_PAIR, FOX_BLOCK, LANES), F32),
                            pltpu.VMEM((HEAD_PAIR, FOX_BLOCK, LANES), F32)]),
        out_shape=bshape(FOX_W),
        compiler_params=_params("arbitrary", "arbitrary"),
        name="fox",
    )(qn2.reshape(-1), kn2.reshape(-1), cend, fqvg, cq, kx, kx, fqvg, fqvg)

    out = pl.pallas_call(
        _out_kernel,
        grid=(T // OUT_ROWS,),
        in_specs=[rows(D_MODEL, OUT_ROWS), rows(GLA_OUT_W, OUT_ROWS), rows(FOX_W, OUT_ROWS),
                  pl.BlockSpec((OUT_ROWS, MEM_W), lambda i: (i, 0)),
                  pl.BlockSpec((OUT_ROWS, MEM_W), lambda i: (i, 1)),
                  once(mem.shape), once((1, D_MODEL)), once(w_mem_kv.shape), once(w_out.shape),
                  whole((1, D_MODEL))],
        out_specs=rows(D_MODEL, OUT_ROWS),
        out_shape=jax.ShapeDtypeStruct((T, D_MODEL), F32),
        scratch_shapes=[pltpu.VMEM((D_MODEL, D_MODEL), BF16),
                        pltpu.VMEM((M, MEM_W), BF16), pltpu.VMEM((M, MEM_W), BF16)],
        compiler_params=_params("arbitrary"),
        name="out",
    )(x, gla, fox, mqg, mqg, mem, mem_norm_g[None, :], w_mem_kv, w_out, out_g[None, :])
    return out


def kernel(x, mem, norm_g, w_in, w_alpha_up, b_alpha, b_forget, gla_norm_g, mem_norm_g,
           w_mem_kv, w_out, final_norm_g):
    assert x.shape[0] == 1 and mem.shape[0] == 1 and norm_g.shape[0] == 1
    assert x.shape[1] % max(PROJ_ROWS, FOX_BLOCK, OUT_ROWS) == 0
    out = _layer(x[0], mem[0], norm_g[0], w_in[0], w_alpha_up[0], b_alpha[0], b_forget[0],
                 gla_norm_g[0], mem_norm_g[0], w_mem_kv[0], w_out[0], final_norm_g)
    return out[None]
```

```python
import jax
import jax.numpy as jnp
from jax import lax
from jax.experimental import pallas as pl
from jax.experimental.pallas import tpu as pltpu

F32 = jnp.float32
BF16 = jnp.bfloat16

EPS = 1e-6
LANES = 128
SUBLANES = 8

D_MODEL = 1024
GLA_HEADS, GLA_DK, GLA_DV, GLA_RANK = 4, 48, 96, 16
GLA_DK_PAD = 64
GLA_DV_PAD = LANES
GLA_GATE_NORM = 16.0
GLA_CHUNK = 64
FOX_HEADS, FOX_DH = 6, 64
MEM_HEADS, MEM_DH = 4, 64
HEAD_PAIR = 2
GLA_QK_W = GLA_HEADS * GLA_DK_PAD
GLA_V_W = GLA_HEADS * GLA_DV_PAD
GLA_OUT_W = GLA_HEADS * GLA_DV
FOX_W = FOX_HEADS * FOX_DH
MEM_W = MEM_HEADS * MEM_DH
SMALL_W = LANES
FG_LANE0 = 0
LR_LANE0 = SUBLANES

_GROUPS = (("gq", GLA_QK_W), ("gk", GLA_QK_W), ("gv", GLA_OUT_W), ("gg", GLA_OUT_W),
           ("fq", FOX_W), ("fk", FOX_W), ("fv", FOX_W), ("fgate", FOX_W),
           ("mq", MEM_W), ("mg", MEM_W))
FG_COL = GLA_DK
LR_COL = GLA_DK_PAD + GLA_DK
_OFF = {}
_o = 0
for _n, _w in _GROUPS:
    _OFF[_n] = (_o, _o + _w)
    _o += _w
IN_COLS_PAD = _o

PROJ_ROWS = 1024
FOX_BLOCK = 4096
FOX_KEYS = 256
OUT_ROWS = 1024
VMEM_LIMIT = 56 * 1024 * 1024

NEG_BIG = -1e30
FOX_SKIP_NATS = 105.0
NORM_SLACK = 1.02
FOX_DIRECT_NORM2 = 3600.0
CX_HI, CX_MID, CX_LO, CX_ONE = 0, 8, 16, 24
CX_ROWS = 32


def _log_sigmoid(z):
    return jnp.minimum(z, 0.0) - jnp.log(1.0 + jnp.exp(-jnp.abs(z)))


def _silu(z):
    return z / (1.0 + jnp.exp(-z))


def _rms_scale(v, width):
    return lax.rsqrt(jnp.sum(v * v, axis=-1, keepdims=True) * (1.0 / width) + EPS)


def _w_in_segments():
    qk, gw = GLA_HEADS * GLA_DK, GLA_HEADS * GLA_DV
    src = {}
    o = 0
    for name, width in (("gq", qk), ("gk", qk), ("gv", gw), ("lr", GLA_RANK), ("gg", gw),
                        ("fq", FOX_W), ("fk", FOX_W), ("fv", FOX_W), ("fg", FOX_HEADS),
                        ("fgate", FOX_W), ("mq", MEM_W), ("mg", MEM_W)):
        src[name] = o
        o += width
    segs = []
    for name, d, d_pad in (("gq", GLA_DK, GLA_DK_PAD), ("gk", GLA_DK, GLA_DK_PAD)):
        segs += [(src[name] + h * d, _OFF[name][0] + h * d_pad, d) for h in range(GLA_HEADS)]
    segs += [(src[name], _OFF[name][0], _OFF[name][1] - _OFF[name][0])
             for name in ("gv", "gg", "fq", "fk", "fv", "fgate", "mq", "mg")]
    segs += [(src["fg"], _OFF["gq"][0] + FG_COL, FOX_HEADS),
             (src["lr"], _OFF["gq"][0] + LR_COL, GLA_RANK)]
    return tuple(segs)


def _proj_kernel(x_ref, g_ref, w_in_ref, w_alpha_ref, b_alpha_ref, b_forget_ref, gla_g_ref,
                 gla_ref, fqvg_ref, kx_ref, mqg_ref, crow_ref, cq_ref, qn2_ref, kn2_ref,
                 carry_ref, wt_ref, wa_ref, ba_ref, bf_ref, ng_ref, seg_ref, small_ref,
                 gq_ref, gk_ref, gv_ref, gg_ref, loga_ref,
                 s_ref, lhs_ref, kv_ref, dec_ref, sprev_ref):
    rows = x_ref.shape[0]
    k_chunks = D_MODEL // LANES

    @pl.when(pl.program_id(0) == 0)
    def _():
        carry_ref[...] = jnp.zeros_like(carry_ref)
        s_ref[...] = jnp.zeros_like(s_ref)
        kn2_ref[...] = jnp.zeros_like(kn2_ref)
        gv_ref[...] = jnp.zeros_like(gv_ref)
        gg_ref[...] = jnp.zeros_like(gg_ref)
        small_ref[...] = jnp.zeros_like(small_ref)
        wa_ref[...] = jnp.zeros_like(wa_ref)
        ba_ref[...] = jnp.zeros_like(ba_ref)
        bf_ref[...] = jnp.zeros_like(bf_ref)
        ng_ref[...] = jnp.zeros_like(ng_ref)
        for h in range(GLA_HEADS):
            src = slice(h * GLA_DK, (h + 1) * GLA_DK)
            dst = slice(h * GLA_DK_PAD, h * GLA_DK_PAD + GLA_DK)
            wa_ref[LR_LANE0:LR_LANE0 + GLA_RANK, dst] = w_alpha_ref[:, src]
            ba_ref[:, dst] = b_alpha_ref[:, src]
        bf_ref[:, FG_LANE0:FG_LANE0 + FOX_HEADS] = b_forget_ref[...]
        ng_ref[:, 0:GLA_DV] = gla_g_ref[...]
        seg_ref[...] = (lax.broadcasted_iota(jnp.int32, seg_ref.shape, 0) // FOX_DH
                        == lax.broadcasted_iota(jnp.int32, seg_ref.shape, 1)).astype(BF16)
        wt_ref[...] = jnp.zeros_like(wt_ref)
        for s0, d0, width in _w_in_segments():
            for c in range(k_chunks):
                wt_ref[d0:d0 + width, c * LANES:(c + 1) * LANES] = (
                    w_in_ref[pl.ds(s0 * k_chunks + c, width, stride=k_chunks), :].astype(BF16))

    x = x_ref[...]
    xn = (x * _rms_scale(x, D_MODEL) * g_ref[...]).astype(BF16)
    nt = (((1,), (1,)), ((), ()))

    def proj(first, last):
        lo, hi = _OFF[first][0], _OFF[last][1]
        y = lax.dot_general(xn, wt_ref[lo:hi, :], nt, preferred_element_type=F32)
        return lambda name: y[:, _OFF[name][0] - lo:_OFF[name][1] - lo]

    gla = proj("gq", "gg")
    tail = proj("mq", "mg")
    gq_all = gla("gq")
    small_ref[:, FG_LANE0:FG_LANE0 + SUBLANES] = gq_all[:, FG_COL:FG_COL + SUBLANES]
    small_ref[:, LR_LANE0:LR_LANE0 + GLA_RANK] = gq_all[:, LR_COL:LR_COL + GLA_RANK]
    small = small_ref[...]
    logf = _log_sigmoid(small + bf_ref[...])
    c = logf.T[0:SUBLANES, :]
    lane = lax.broadcasted_iota(jnp.int32, c.shape, 1)
    shift = 1
    while shift < rows:
        c = c + jnp.where(lane >= shift, pltpu.roll(c, shift, axis=1), 0.0)
        shift *= 2
    c = c + carry_ref[:, 0:1]
    crow_ref[...] = c
    carry_ref[...] = jnp.broadcast_to(c[:, rows - 1:rows], carry_ref.shape)
    neg = -c
    hi = neg.astype(BF16).astype(F32)
    mid = (neg - hi).astype(BF16).astype(F32)
    low = neg - hi - mid
    parts = jnp.concatenate(
        [hi, mid, low, jnp.ones_like(c), jnp.zeros((LANES - 4 * SUBLANES, rows), F32)], axis=0)
    kx_ref[FOX_W:, :] = parts.astype(BF16)
    cq_ref[...] = parts.T.astype(BF16)

    z = (jnp.dot(small.astype(BF16), wa_ref[...].astype(BF16), preferred_element_type=F32)
         + ba_ref[...])
    loga_ref[...] = _log_sigmoid(z) * (1.0 / GLA_GATE_NORM)
    q_lane = lax.broadcasted_iota(jnp.int32, (1, GLA_QK_W), 1)
    gq_ref[...] = jnp.where(q_lane % GLA_DK_PAD < GLA_DK, gq_all, 0.0).astype(BF16)
    gk_ref[...] = gla("gk").astype(BF16)
    gv, gg = gla("gv").astype(BF16), _silu(gla("gg")).astype(BF16)
    for h in range(GLA_HEADS):
        src = slice(h * GLA_DV, (h + 1) * GLA_DV)
        dst = slice(h * GLA_DV_PAD, h * GLA_DV_PAD + GLA_DV)
        gv_ref[:, dst] = gv[:, src]
        gg_ref[:, dst] = gg[:, src]
    gla_local, gla_scan, gla_output = _gla_block(
        gq_ref, gk_ref, gv_ref, loga_ref, gg_ref, ng_ref, gla_ref,
        s_ref, lhs_ref, kv_ref, dec_ref, sprev_ref)

    def max_sq_norm(v):
        v32 = v.astype(F32)
        n2 = jnp.dot((v32 * v32).astype(BF16), seg_ref[...], preferred_element_type=F32)
        return jnp.max(n2, axis=0, keepdims=True)

    gla_local()
    fox_qk = proj("fq", "fk")
    mqg_ref[:, :MEM_W] = (tail("mq") * MEM_DH ** -0.5).astype(BF16)
    mqg_ref[:, MEM_W:] = _silu(tail("mg")).astype(BF16)
    gla_scan()
    fq = (fox_qk("fq") * FOX_DH ** -0.5).astype(BF16)
    fk = fox_qk("fk").astype(BF16)
    fqvg_ref[:, :FOX_W] = fq
    kx_ref[:FOX_W, :] = fox_qk("fk").T.astype(BF16)
    gla_output()
    fox_vg = proj("fv", "fgate")
    fqvg_ref[:, FOX_W:2 * FOX_W] = fox_vg("fv").astype(BF16)
    fqvg_ref[:, 2 * FOX_W:] = _silu(fox_vg("fgate")).astype(BF16)
    qn2_ref[0] = max_sq_norm(fq)
    kn2_ref[0] = jnp.maximum(kn2_ref[0], max_sq_norm(fk))


def _memkv(mem_ref, g_ref, w_ref, mk_ref, mv_ref):
    m = mem_ref[...]
    mn = (m * _rms_scale(m, D_MODEL) * g_ref[...]).astype(BF16)
    kv = jnp.dot(mn, w_ref[...].astype(BF16), preferred_element_type=F32)
    mk_ref[...] = kv[:, :MEM_W].astype(BF16)
    mv_ref[...] = kv[:, MEM_W:].astype(BF16)


def _gla_block(q_ref, k_ref, v_ref, loga_ref, gate_ref, ng_ref, o_ref,
               s_ref, lhs_ref, kv_ref, dec_ref, sprev_ref):
    C = GLA_CHUNK
    W = HEAD_PAIR * GLA_DV_PAD
    n_chunks = q_ref.shape[0] // C

    row = lax.broadcasted_iota(jnp.int32, (C, LANES), 0)
    lane = lax.broadcasted_iota(jnp.int32, (C, LANES), 1)
    lo_k = lane < GLA_DK_PAD
    causal = row >= jnp.where(lo_k, lane, lane - GLA_DK_PAD)
    lo_v = lax.broadcasted_iota(jnp.int32, (C, W), 1) < GLA_DV_PAD
    st_row = lax.broadcasted_iota(jnp.int32, (LANES, W), 0)
    st_lane = lax.broadcasted_iota(jnp.int32, (LANES, W), 1)
    own = (st_row < GLA_DK_PAD) == (st_lane < GLA_DV_PAD)
    eye = (lax.broadcasted_iota(jnp.int32, (LANES, LANES), 0)
           == lax.broadcasted_iota(jnp.int32, (LANES, LANES), 1))
    scale = GLA_DK ** -0.5
    nt = (((1,), (1,)), ((), ()))
    tn = (((0,), (0,)), ((), ()))
    ng = jnp.concatenate([ng_ref[...]] * HEAD_PAIR, axis=1)

    pairs = range(GLA_HEADS // HEAD_PAIR)

    def local(ci):
        rs = slice(ci * C, (ci + 1) * C)
        for p in pairs:
            ls = slice(p * LANES, (p + 1) * LANES)
            vs = slice(p * W, (p + 1) * W)
            b = loga_ref[rs, ls]
            shift = 1
            while shift < C:
                b = b + jnp.where(row >= shift, pltpu.roll(b, shift, axis=0), 0.0)
                shift *= 2
            b_last = b[C - 1:C, :]
            k2 = k_ref[rs, ls].astype(F32)
            qd = (q_ref[rs, ls].astype(F32) * scale * jnp.exp(b)).astype(BF16)
            kd = (k2 * jnp.exp(-b)).astype(BF16)
            ke = (k2 * jnp.exp(b_last - b)).astype(BF16)
            zk = jnp.zeros_like(kd)
            kd_blk = jnp.concatenate([jnp.where(lo_k, kd, zk), jnp.where(lo_k, zk, kd)], axis=0)
            attn = lax.dot_general(qd, kd_blk, nt, preferred_element_type=F32)
            lhs_ref[rs, vs] = jnp.concatenate([jnp.where(causal, attn, 0.0).astype(BF16), qd], axis=1)
            kv = lax.dot_general(ke, v_ref[rs, vs], tn, preferred_element_type=F32)
            kv_ref[p, ci] = jnp.where(own, kv, 0.0)
            dcol = jnp.exp(jnp.sum(jnp.where(eye, jnp.broadcast_to(b_last, (LANES, LANES)), 0.0),
                                   axis=1, keepdims=True))
            dec_ref[p, ci] = jnp.broadcast_to(dcol, (LANES, LANES))

    def scan(ci):
        for p in pairs:
            s_prev = s_ref[p]
            sprev_ref[p, ci] = s_prev.astype(BF16)
            s_ref[p] = jnp.tile(dec_ref[p, ci], (1, HEAD_PAIR)) * s_prev + kv_ref[p, ci]

    def output(ci):
        rs = slice(ci * C, (ci + 1) * C)
        for p in pairs:
            vs = slice(p * W, (p + 1) * W)
            v2 = v_ref[rs, vs]
            zv = jnp.zeros_like(v2)
            v_blk = jnp.concatenate([jnp.where(lo_v, v2, zv), jnp.where(lo_v, zv, v2)], axis=0)
            o = jnp.dot(lhs_ref[rs, vs], jnp.concatenate([v_blk, sprev_ref[p, ci]], axis=0),
                        preferred_element_type=F32)
            o2 = o * o
            ms = jnp.where(lo_v, jnp.sum(o2[:, :GLA_DV_PAD], axis=1, keepdims=True),
                           jnp.sum(o2[:, GLA_DV_PAD:], axis=1, keepdims=True))
            on = o * lax.rsqrt(ms * (1.0 / GLA_DV) + EPS) * ng
            og = (on * gate_ref[rs, vs].astype(F32)).astype(BF16)
            for hh in range(HEAD_PAIR):
                c0 = (p * HEAD_PAIR + hh) * GLA_DV
                o_ref[rs, c0:c0 + GLA_DV] = og[:, hh * GLA_DV_PAD:hh * GLA_DV_PAD + GLA_DV]

    def all_chunks(phase):
        return lambda: [phase(ci) for ci in range(n_chunks)]

    return all_chunks(local), all_chunks(scan), all_chunks(output)


def _fox_kernel(qn2_ref, kn2_ref, cend_ref, q_ref, cq_ref, k_ref, cx_ref, v_ref, gate_ref, o_ref,
                m_ref, acc_ref):
    blk = FOX_KEYS
    streams = range(q_ref.shape[0] // blk)
    pair = pl.program_id(0)
    qi = pl.program_id(1)
    nblk = pl.num_programs(1) * len(streams)
    lane = lax.broadcasted_iota(jnp.int32, (1, LANES), 1)
    lo_lanes = lane < FOX_DH
    reps = blk // LANES
    diag = [qi * len(streams) + s for s in streams]
    heads = [pair * HEAD_PAIR + hh for hh in range(HEAD_PAIR)]

    q = q_ref[...]
    zero = jnp.zeros_like(q)
    q_lo, q_hi = jnp.where(lo_lanes, q, zero), jnp.where(lo_lanes, zero, q)
    q_stack = [jnp.concatenate([q_lo[s * blk:(s + 1) * blk], q_hi[s * blk:(s + 1) * blk]], axis=0)
               for s in streams]
    g_row = lax.broadcasted_iota(jnp.int32, (CX_ROWS, CX_ROWS), 0)
    g_col = lax.broadcasted_iota(jnp.int32, (CX_ROWS, CX_ROWS), 1)
    xlane = lax.broadcasted_iota(jnp.int32, (HEAD_PAIR * blk, CX_ROWS), 1)
    cq = cq_ref[:, 0:CX_ROWS]

    def lane_map(h, shift):
        g = jnp.where((g_row == CX_ONE) & ((g_col == CX_HI + h) | (g_col == CX_MID + h)
                                           | (g_col == CX_LO + h)), 1.0, 0.0)
        if shift:
            for part, base in enumerate((CX_HI, CX_MID, CX_LO)):
                g = jnp.where((g_row == base + h) & (g_col == CX_ONE + part), -1.0, g)
        return g.astype(BF16)

    def q_aug(s, shift, void=None):
        cs = cq[s * blk:(s + 1) * blk]
        extra = jnp.concatenate([jnp.dot(cs, lane_map(h, shift), preferred_element_type=F32)
                                 for h in heads], axis=0)
        if void is not None:
            extra = jnp.where(jnp.logical_and(xlane == CX_ONE + 3, void), NEG_BIG, extra)
        return jnp.concatenate([q_stack[s], extra.astype(BF16)], axis=1)

    acc_ref[...] = jnp.zeros_like(acc_ref)
    qpos = lax.broadcasted_iota(jnp.int32, (blk, blk), 0)
    kpos = lax.broadcasted_iota(jnp.int32, (blk, blk), 1)

    def block(j):
        ks = pl.ds(pl.multiple_of(jnp.maximum(j, 0) * blk, blk), blk)
        k_aug = jnp.concatenate([k_ref[:, ks], cx_ref[0:CX_ROWS, ks]], axis=0)
        vb = v_ref[ks, :]
        one = jnp.ones_like(vb)
        return k_aug, (jnp.where(lo_lanes, vb, one), jnp.where(lo_lanes, one, vb))

    def step_online(s, j, qa, masked):
        rows = slice(s * blk, (s + 1) * blk)
        k_aug, vaug = block(j)
        s_all = jnp.dot(qa, k_aug, preferred_element_type=F32)
        for hh in range(HEAD_PAIR):
            sc = s_all[hh * blk:(hh + 1) * blk]
            if masked:
                sc = jnp.where(kpos <= qpos, sc, NEG_BIG)
            m_prev = m_ref[hh, rows]
            m_new = jnp.maximum(m_prev, jnp.max(sc, axis=1, keepdims=True))
            p = jnp.exp(sc - jnp.tile(m_new, (1, reps)))
            alpha = jnp.exp(m_prev - m_new)
            pv = jnp.dot(p.astype(BF16), vaug[hh], preferred_element_type=F32)
            acc_ref[hh, rows] = alpha * acc_ref[hh, rows] + pv
            m_ref[hh, rows] = m_new

    def step_direct(s, j, qa, masked, only=None):
        rows = slice(s * blk, (s + 1) * blk)
        k_aug, vaug = block(j)
        hsel = range(HEAD_PAIR) if only is None else (only,)
        lhs = qa if only is None else qa[only * blk:(only + 1) * blk]
        s_all = jnp.dot(lhs, k_aug, preferred_element_type=F32)
        for n, hh in enumerate(hsel):
            sc = s_all[n * blk:(n + 1) * blk]
            if masked:
                sc = jnp.where(kpos <= qpos, sc, NEG_BIG)
            acc_ref[hh, rows] += jnp.dot(jnp.exp(sc).astype(BF16), vaug[hh],
                                         preferred_element_type=F32)

    k_max2 = [kn2_ref[h] for h in heads]

    def norm2(s, hh):
        stat = (diag[s] * blk // PROJ_ROWS) * LANES
        return (NORM_SLACK * NORM_SLACK) * qn2_ref[stat + heads[hh]] * k_max2[hh]

    n2 = [[norm2(s, hh) for hh in range(HEAD_PAIR)] for s in streams]
    gap0 = [[FOX_SKIP_NATS + cend_ref[heads[hh] * nblk + jnp.maximum(diag[s] - 1, 0)]
             for hh in range(HEAD_PAIR)] for s in streams]

    def live_head(t, hh, direct):
        keep = False
        for s in streams:
            j = diag[s] - 1 - t
            gap = gap0[s][hh] - cend_ref[heads[hh] * nblk + jnp.maximum(j, 0)]
            bound2 = n2[s][hh] if direct else 4.0 * n2[s][hh]
            dead = jnp.logical_and(gap <= 0.0, bound2 <= gap * gap)
            keep = jnp.logical_or(keep, jnp.logical_and(j >= 0, jnp.logical_not(dead)))
        return keep

    def sweeps(stepper, direct):
        def sweep(back, masked=False, only=None):
            for s in streams:
                stepper(s, diag[s] - back, masked, only)

        def loop(t0, cond, only=None):
            def body(t):
                sweep(t + 1, only=only)
                return t + 1
            return lax.while_loop(cond, body, t0)

        def live(t, hh):
            return live_head(t, hh, direct)

        sweep(0, masked=True)
        sweep(1)
        if direct:
            t_both = loop(1, lambda t: jnp.logical_and(live(t, 0), live(t, 1)))
            for hh in range(HEAD_PAIR):
                loop(t_both, lambda t, hh=hh: live(t, hh), only=hh)
        else:
            loop(1, lambda t: jnp.logical_or(live(t, 0), live(t, 1)))

    direct_ok = True
    for s in streams:
        for hh in range(HEAD_PAIR):
            direct_ok = jnp.logical_and(direct_ok, n2[s][hh] <= FOX_DIRECT_NORM2)

    @pl.when(direct_ok)
    def _():
        q_dir = [q_aug(s, shift=True) for s in streams]
        q_void = [q_aug(s, shift=True, void=True) for s in streams]
        sweeps(lambda s, j, masked, only: step_direct(
            s, j, q_dir[s] if masked else jnp.where(j >= 0, q_dir[s], q_void[s]), masked, only),
            direct=True)

    @pl.when(jnp.logical_not(direct_ok))
    def _():
        m_ref[...] = jnp.full_like(m_ref, NEG_BIG)
        sweeps(lambda s, j, masked, only: step_online(
            s, j, q_aug(s, shift=False, void=j < 0), masked), direct=False)

    outs = []
    for hh in range(HEAD_PAIR):
        acc = acc_ref[hh]
        outs.append(acc / pltpu.roll(acc, FOX_DH, axis=1))
    o = jnp.where(lo_lanes, outs[0], outs[1])
    o_ref[...] = (o * gate_ref[...].astype(F32)).astype(BF16)


def _out_kernel(x_ref, gla_ref, fox_ref, mq_ref, mg_ref, mem_ref, mem_g_ref, w_mem_ref,
                w_out_ref, fg_ref, o_ref, wo_ref, mk_ref, mv_ref):
    @pl.when(pl.program_id(0) == 0)
    def _():
        wo_ref[...] = w_out_ref[...].astype(BF16)
        _memkv(mem_ref, mem_g_ref, w_mem_ref, mk_ref, mv_ref)

    lane = lax.broadcasted_iota(jnp.int32, (1, LANES), 1)
    lo_lanes = lane < MEM_DH
    nt = (((1,), (1,)), ((), ()))
    mem_parts = []
    for p in range(MEM_HEADS // HEAD_PAIR):
        ls = slice(p * LANES, (p + 1) * LANES)
        q = mq_ref[:, ls]
        kb = mk_ref[:, ls]
        vb = mv_ref[:, ls]
        zero = jnp.zeros_like(q)
        one = jnp.ones_like(vb)
        qh = (jnp.where(lo_lanes, q, zero), jnp.where(lo_lanes, zero, q))
        vaug = (jnp.where(lo_lanes, vb, one), jnp.where(lo_lanes, one, vb))
        outs = []
        for hh in range(HEAD_PAIR):
            s = lax.dot_general(qh[hh], kb, nt, preferred_element_type=F32)
            pexp = jnp.exp(s - jnp.max(s, axis=1, keepdims=True))
            pv = jnp.dot(pexp.astype(BF16), vaug[hh], preferred_element_type=F32)
            outs.append(pv / pltpu.roll(pv, MEM_DH, axis=1))
        o = jnp.where(lo_lanes, outs[0], outs[1])
        mem_parts.append((o * mg_ref[:, ls].astype(F32)).astype(BF16))
    mixed = jnp.concatenate([gla_ref[...], fox_ref[...]] + mem_parts, axis=1)
    y = x_ref[...] + jnp.dot(mixed, wo_ref[...], preferred_element_type=F32)
    o_ref[...] = y * _rms_scale(y, D_MODEL) * fg_ref[...]


def _params(*sem):
    return pltpu.CompilerParams(dimension_semantics=sem, vmem_limit_bytes=VMEM_LIMIT)


def _layer(x, mem, norm_g, w_in, w_alpha_up, b_alpha, b_forget, gla_norm_g,
           mem_norm_g, w_mem_kv, w_out, out_g):
    w_in_t = jnp.transpose(w_in[None], (0, 2, 1)).reshape(-1, LANES)
    T = x.shape[0]
    M = mem.shape[0]

    def rows(width, n=PROJ_ROWS):
        return pl.BlockSpec((n, width), lambda i: (i, 0))

    def whole(shape):
        return pl.BlockSpec(shape, lambda i: (0,) * len(shape))

    def once(shape):
        return pl.BlockSpec(shape, lambda i: (0,) * len(shape), pipeline_mode=pl.Buffered(1))

    bshape = lambda w: jax.ShapeDtypeStruct((T, w), BF16)
    nproj = T // PROJ_ROWS
    stat_spec = pl.BlockSpec((1, 1, LANES), lambda i: (i, 0, 0))
    stat_shape = jax.ShapeDtypeStruct((nproj, 1, LANES), F32)
    gla_pairs, gla_chunks = GLA_HEADS // HEAD_PAIR, PROJ_ROWS // GLA_CHUNK
    pair_w = HEAD_PAIR * GLA_DV_PAD
    (gla, fqvg, kx, mqg, crow, cq, qn2, kn2) = pl.pallas_call(
        _proj_kernel,
        grid=(nproj,),
        in_specs=[rows(D_MODEL), whole((1, D_MODEL)),
                  pl.BlockSpec(w_in_t.shape, lambda i: (0, 0), pipeline_mode=pl.Buffered(1)),
                  whole(w_alpha_up.shape), whole((1, GLA_HEADS * GLA_DK)),
                  whole((1, FOX_HEADS)), whole((1, GLA_DV))],
        out_specs=[rows(GLA_OUT_W), rows(3 * FOX_W),
                   pl.BlockSpec((FOX_W + LANES, PROJ_ROWS), lambda i: (0, i)),
                   rows(2 * MEM_W),
                   pl.BlockSpec((SUBLANES, PROJ_ROWS), lambda i: (0, i)), rows(LANES),
                   stat_spec, pl.BlockSpec((1, 1, LANES), lambda i: (0, 0, 0))],
        out_shape=[bshape(GLA_OUT_W), bshape(3 * FOX_W),
                   jax.ShapeDtypeStruct((FOX_W + LANES, T), BF16),
                   bshape(2 * MEM_W),
                   jax.ShapeDtypeStruct((SUBLANES, T), F32), bshape(LANES),
                   stat_shape, jax.ShapeDtypeStruct((1, 1, LANES), F32)],
        scratch_shapes=[
            pltpu.VMEM((SUBLANES, LANES), F32),
            pltpu.VMEM((IN_COLS_PAD, D_MODEL), BF16),
            pltpu.VMEM((SMALL_W, GLA_QK_W), F32),
            pltpu.VMEM((1, GLA_QK_W), F32),
            pltpu.VMEM((1, SMALL_W), F32),
            pltpu.VMEM((1, GLA_DV_PAD), F32),
            pltpu.VMEM((FOX_W, LANES), BF16),
            pltpu.VMEM((PROJ_ROWS, SMALL_W), F32),
            pltpu.VMEM((PROJ_ROWS, GLA_QK_W), BF16),
            pltpu.VMEM((PROJ_ROWS, GLA_QK_W), BF16),
            pltpu.VMEM((PROJ_ROWS, GLA_V_W), BF16),
            pltpu.VMEM((PROJ_ROWS, GLA_V_W), BF16),
            pltpu.VMEM((PROJ_ROWS, GLA_QK_W), F32),
            pltpu.VMEM((gla_pairs, LANES, pair_w), F32),
            pltpu.VMEM((PROJ_ROWS, GLA_V_W), BF16),
            pltpu.VMEM((gla_pairs, gla_chunks, LANES, pair_w), F32),
            pltpu.VMEM((gla_pairs, gla_chunks, LANES, LANES), F32),
            pltpu.VMEM((gla_pairs, gla_chunks, LANES, pair_w), BF16)],
        compiler_params=_params("arbitrary"),
        name="proj",
    )(x, norm_g[None, :], w_in_t, w_alpha_up, b_alpha[None, :], b_forget[None, :],
      gla_norm_g[None, :])

    cend = crow[:FOX_HEADS, FOX_KEYS - 1::FOX_KEYS].reshape(-1)
    fox_pairs = FOX_HEADS // HEAD_PAIR
    pair_rows = pl.BlockSpec((FOX_BLOCK, LANES), lambda p, i, *_: (i, p))
    fox = pl.pallas_call(
        _fox_kernel,
        grid_spec=pltpu.PrefetchScalarGridSpec(
            num_scalar_prefetch=3,
            grid=(fox_pairs, T // FOX_BLOCK),
            in_specs=[pair_rows,
                      pl.BlockSpec((FOX_BLOCK, LANES), lambda p, i, *_: (i, 0)),
                      pl.BlockSpec((LANES, T), lambda p, i, *_: (p, 0)),
                      pl.BlockSpec((LANES, T), lambda p, i, *_: (fox_pairs, 0)),
                      pl.BlockSpec((T, LANES), lambda p, i, *_: (0, fox_pairs + p)),
                      pl.BlockSpec((FOX_BLOCK, LANES),
                                   lambda p, i, *_: (i, 2 * fox_pairs + p))],
            out_specs=pair_rows,
            scratch_shapes=[pltpu.VMEM((HEAD_PAIR, FOX_BLOCK, LANES), F32),
                            pltpu.VMEM((HEAD_PAIR, FOX_BLOCK, LANES), F32)]),
        out_shape=bshape(FOX_W),
        compiler_params=_params("arbitrary", "arbitrary"),
        name="fox",
    )(qn2.reshape(-1), kn2.reshape(-1), cend, fqvg, cq, kx, kx, fqvg, fqvg)

    out = pl.pallas_call(
        _out_kernel,
        grid=(T // OUT_ROWS,),
        in_specs=[rows(D_MODEL, OUT_ROWS), rows(GLA_OUT_W, OUT_ROWS), rows(FOX_W, OUT_ROWS),
                  pl.BlockSpec((OUT_ROWS, MEM_W), lambda i: (i, 0)),
                  pl.BlockSpec((OUT_ROWS, MEM_W), lambda i: (i, 1)),
                  once(mem.shape), once((1, D_MODEL)), once(w_mem_kv.shape), once(w_out.shape),
                  whole((1, D_MODEL))],
        out_specs=rows(D_MODEL, OUT_ROWS),
        out_shape=jax.ShapeDtypeStruct((T, D_MODEL), F32),
        scratch_shapes=[pltpu.VMEM((D_MODEL, D_MODEL), BF16),
                        pltpu.VMEM((M, MEM_W), BF16), pltpu.VMEM((M, MEM_W), BF16)],
        compiler_params=_params("arbitrary"),
        name="out",
    )(x, gla, fox, mqg, mqg, mem, mem_norm_g[None, :], w_mem_kv, w_out, out_g[None, :])
    return out


def kernel(x, mem, norm_g, w_in, w_alpha_up, b_alpha, b_forget, gla_norm_g, mem_norm_g,
           w_mem_kv, w_out, final_norm_g):
    assert x.shape[0] == 1 and mem.shape[0] == 1 and norm_g.shape[0] == 1
    assert x.shape[1] % max(PROJ_ROWS, FOX_BLOCK, OUT_ROWS) == 0
    out = _layer(x[0], mem[0], norm_g[0], w_in[0], w_alpha_up[0], b_alpha[0], b_forget[0],
                 gla_norm_g[0], mem_norm_g[0], w_mem_kv[0], w_out[0], final_norm_g)
    return out[None]
```

```python
import jax
import jax.numpy as jnp
from jax import lax
from jax.experimental import pallas as pl
from jax.experimental.pallas import tpu as pltpu

F32 = jnp.float32
BF16 = jnp.bfloat16

EPS = 1e-6
LANES = 128
SUBLANES = 8

D_MODEL = 1024
GLA_HEADS, GLA_DK, GLA_DV, GLA_RANK = 4, 48, 96, 16
GLA_DK_PAD = 64
GLA_DV_PAD = LANES
GLA_GATE_NORM = 16.0
GLA_CHUNK = 64
FOX_HEADS, FOX_DH = 6, 64
MEM_HEADS, MEM_DH = 4, 64
HEAD_PAIR = 2
GLA_QK_W = GLA_HEADS * GLA_DK_PAD
GLA_V_W = GLA_HEADS * GLA_DV_PAD
GLA_OUT_W = GLA_HEADS * GLA_DV
FOX_W = FOX_HEADS * FOX_DH
MEM_W = MEM_HEADS * MEM_DH
SMALL_W = LANES
FG_LANE0 = 0
LR_LANE0 = SUBLANES

_GROUPS = (("gq", GLA_QK_W), ("gk", GLA_QK_W), ("gv", GLA_OUT_W), ("gg", GLA_OUT_W),
           ("fq", FOX_W), ("fk", FOX_W), ("fv", FOX_W), ("fgate", FOX_W),
           ("mq", MEM_W), ("mg", MEM_W))
FG_COL = GLA_DK
LR_COL = GLA_DK_PAD + GLA_DK
_OFF = {}
_o = 0
for _n, _w in _GROUPS:
    _OFF[_n] = (_o, _o + _w)
    _o += _w
IN_COLS_PAD = _o

PROJ_ROWS = 1024
FOX_BLOCK = 4096
FOX_KEYS = 256
OUT_ROWS = 1024
X_SLOTS = 3
VMEM_LIMIT = 56 * 1024 * 1024

NEG_BIG = -1e30
FOX_SKIP_NATS = 105.0
NORM_SLACK = 1.02
FOX_DIRECT_NORM2 = 3600.0
CX_HI, CX_MID, CX_LO, CX_ONE = 0, 8, 16, 24
CX_ROWS = 32


def _log_sigmoid(z):
    return jnp.minimum(z, 0.0) - jnp.log(1.0 + jnp.exp(-jnp.abs(z)))


def _silu(z):
    return z / (1.0 + jnp.exp(-z))


def _rms_scale(v, width):
    return lax.rsqrt(jnp.sum(v * v, axis=-1, keepdims=True) * (1.0 / width) + EPS)


def _w_in_segments():
    qk, gw = GLA_HEADS * GLA_DK, GLA_HEADS * GLA_DV
    src = {}
    o = 0
    for name, width in (("gq", qk), ("gk", qk), ("gv", gw), ("lr", GLA_RANK), ("gg", gw),
                        ("fq", FOX_W), ("fk", FOX_W), ("fv", FOX_W), ("fg", FOX_HEADS),
                        ("fgate", FOX_W), ("mq", MEM_W), ("mg", MEM_W)):
        src[name] = o
        o += width
    segs = []
    for name, d, d_pad in (("gq", GLA_DK, GLA_DK_PAD), ("gk", GLA_DK, GLA_DK_PAD)):
        segs += [(src[name] + h * d, _OFF[name][0] + h * d_pad, d) for h in range(GLA_HEADS)]
    segs += [(src[name], _OFF[name][0], _OFF[name][1] - _OFF[name][0])
             for name in ("gv", "gg", "fq", "fk", "fv", "fgate", "mq", "mg")]
    segs += [(src["fg"], _OFF["gq"][0] + FG_COL, FOX_HEADS),
             (src["lr"], _OFF["gq"][0] + LR_COL, GLA_RANK)]
    return tuple(segs)


def _proj_kernel(x_ref, g_ref, w_in_ref, w_alpha_ref, b_alpha_ref, b_forget_ref, gla_g_ref,
                 gla_ref, fqvg_ref, kx_ref, mqg_ref, crow_ref, cq_ref, qn2_ref, kn2_ref,
                 carry_ref, wt_ref, wa_ref, ba_ref, bf_ref, ng_ref, seg_ref, small_ref,
                 gq_ref, gk_ref, gv_ref, gg_ref, loga_ref,
                 s_ref, lhs_ref, kv_ref, dec_ref, sprev_ref):
    rows = x_ref.shape[0]
    k_chunks = D_MODEL // LANES

    @pl.when(pl.program_id(0) == 0)
    def _():
        carry_ref[...] = jnp.zeros_like(carry_ref)
        s_ref[...] = jnp.zeros_like(s_ref)
        kn2_ref[...] = jnp.zeros_like(kn2_ref)
        gv_ref[...] = jnp.zeros_like(gv_ref)
        gg_ref[...] = jnp.zeros_like(gg_ref)
        small_ref[...] = jnp.zeros_like(small_ref)
        wa_ref[...] = jnp.zeros_like(wa_ref)
        ba_ref[...] = jnp.zeros_like(ba_ref)
        bf_ref[...] = jnp.zeros_like(bf_ref)
        ng_ref[...] = jnp.zeros_like(ng_ref)
        for h in range(GLA_HEADS):
            src = slice(h * GLA_DK, (h + 1) * GLA_DK)
            dst = slice(h * GLA_DK_PAD, h * GLA_DK_PAD + GLA_DK)
            wa_ref[LR_LANE0:LR_LANE0 + GLA_RANK, dst] = w_alpha_ref[:, src]
            ba_ref[:, dst] = b_alpha_ref[:, src]
        bf_ref[:, FG_LANE0:FG_LANE0 + FOX_HEADS] = b_forget_ref[...]
        ng_ref[:, 0:GLA_DV] = gla_g_ref[...]
        seg_ref[...] = (lax.broadcasted_iota(jnp.int32, seg_ref.shape, 0) // FOX_DH
                        == lax.broadcasted_iota(jnp.int32, seg_ref.shape, 1)).astype(BF16)
        wt_ref[...] = jnp.zeros_like(wt_ref)
        for s0, d0, width in _w_in_segments():
            for c in range(k_chunks):
                wt_ref[d0:d0 + width, c * LANES:(c + 1) * LANES] = (
                    w_in_ref[pl.ds(s0 * k_chunks + c, width, stride=k_chunks), :].astype(BF16))

    x = x_ref[...]
    xn = (x * _rms_scale(x, D_MODEL) * g_ref[...]).astype(BF16)
    nt = (((1,), (1,)), ((), ()))

    def proj(first, last):
        lo, hi = _OFF[first][0], _OFF[last][1]
        y = lax.dot_general(xn, wt_ref[lo:hi, :], nt, preferred_element_type=F32)
        return lambda name: y[:, _OFF[name][0] - lo:_OFF[name][1] - lo]

    gla = proj("gq", "gg")
    tail = proj("mq", "mg")
    gq_all = gla("gq")
    small_ref[:, FG_LANE0:FG_LANE0 + SUBLANES] = gq_all[:, FG_COL:FG_COL + SUBLANES]
    small_ref[:, LR_LANE0:LR_LANE0 + GLA_RANK] = gq_all[:, LR_COL:LR_COL + GLA_RANK]
    small = small_ref[...]
    logf = _log_sigmoid(small + bf_ref[...])
    c = logf.T[0:SUBLANES, :]
    lane = lax.broadcasted_iota(jnp.int32, c.shape, 1)
    shift = 1
    while shift < rows:
        c = c + jnp.where(lane >= shift, pltpu.roll(c, shift, axis=1), 0.0)
        shift *= 2
    c = c + carry_ref[:, 0:1]
    crow_ref[...] = c
    carry_ref[...] = jnp.broadcast_to(c[:, rows - 1:rows], carry_ref.shape)
    neg = -c
    hi = neg.astype(BF16).astype(F32)
    mid = (neg - hi).astype(BF16).astype(F32)
    low = neg - hi - mid
    parts = jnp.concatenate(
        [hi, mid, low, jnp.ones_like(c), jnp.zeros((LANES - 4 * SUBLANES, rows), F32)], axis=0)
    kx_ref[FOX_W:, :] = parts.astype(BF16)
    cq_ref[...] = parts.T.astype(BF16)

    z = (jnp.dot(small.astype(BF16), wa_ref[...].astype(BF16), preferred_element_type=F32)
         + ba_ref[...])
    loga_ref[...] = _log_sigmoid(z) * (1.0 / GLA_GATE_NORM)
    q_lane = lax.broadcasted_iota(jnp.int32, (1, GLA_QK_W), 1)
    gq_ref[...] = jnp.where(q_lane % GLA_DK_PAD < GLA_DK, gq_all, 0.0).astype(BF16)
    gk_ref[...] = gla("gk").astype(BF16)
    gv, gg = gla("gv").astype(BF16), _silu(gla("gg")).astype(BF16)
    for h in range(GLA_HEADS):
        src = slice(h * GLA_DV, (h + 1) * GLA_DV)
        dst = slice(h * GLA_DV_PAD, h * GLA_DV_PAD + GLA_DV)
        gv_ref[:, dst] = gv[:, src]
        gg_ref[:, dst] = gg[:, src]
    gla_local, gla_scan, gla_output = _gla_block(
        gq_ref, gk_ref, gv_ref, loga_ref, gg_ref, ng_ref, gla_ref,
        s_ref, lhs_ref, kv_ref, dec_ref, sprev_ref)

    def max_sq_norm(v):
        v32 = v.astype(F32)
        n2 = jnp.dot((v32 * v32).astype(BF16), seg_ref[...], preferred_element_type=F32)
        return jnp.max(n2, axis=0, keepdims=True)

    gla_local()
    fox_qk = proj("fq", "fk")
    mqg_ref[:, :MEM_W] = (tail("mq") * MEM_DH ** -0.5).astype(BF16)
    mqg_ref[:, MEM_W:] = _silu(tail("mg")).astype(BF16)
    gla_scan()
    fq = (fox_qk("fq") * FOX_DH ** -0.5).astype(BF16)
    fk = fox_qk("fk").astype(BF16)
    fqvg_ref[:, :FOX_W] = fq
    kx_ref[:FOX_W, :] = fox_qk("fk").T.astype(BF16)
    gla_output()
    fox_vg = proj("fv", "fgate")
    fqvg_ref[:, FOX_W:2 * FOX_W] = fox_vg("fv").astype(BF16)
    fqvg_ref[:, 2 * FOX_W:] = _silu(fox_vg("fgate")).astype(BF16)
    qn2_ref[0] = max_sq_norm(fq)
    kn2_ref[0] = jnp.maximum(kn2_ref[0], max_sq_norm(fk))


def _memkv(mem_ref, g_ref, w_ref, mk_ref, mv_ref):
    m = mem_ref[...]
    mn = (m * _rms_scale(m, D_MODEL) * g_ref[...]).astype(BF16)
    kv = jnp.dot(mn, w_ref[...].astype(BF16), preferred_element_type=F32)
    mk_ref[...] = kv[:, :MEM_W].astype(BF16)
    mv_ref[...] = kv[:, MEM_W:].astype(BF16)


def _gla_block(q_ref, k_ref, v_ref, loga_ref, gate_ref, ng_ref, o_ref,
               s_ref, lhs_ref, kv_ref, dec_ref, sprev_ref):
    C = GLA_CHUNK
    W = HEAD_PAIR * GLA_DV_PAD
    n_chunks = q_ref.shape[0] // C

    row = lax.broadcasted_iota(jnp.int32, (C, LANES), 0)
    lane = lax.broadcasted_iota(jnp.int32, (C, LANES), 1)
    lo_k = lane < GLA_DK_PAD
    causal = row >= jnp.where(lo_k, lane, lane - GLA_DK_PAD)
    lo_v = lax.broadcasted_iota(jnp.int32, (C, W), 1) < GLA_DV_PAD
    st_row = lax.broadcasted_iota(jnp.int32, (LANES, W), 0)
    st_lane = lax.broadcasted_iota(jnp.int32, (LANES, W), 1)
    own = (st_row < GLA_DK_PAD) == (st_lane < GLA_DV_PAD)
    eye = (lax.broadcasted_iota(jnp.int32, (LANES, LANES), 0)
           == lax.broadcasted_iota(jnp.int32, (LANES, LANES), 1))
    scale = GLA_DK ** -0.5
    nt = (((1,), (1,)), ((), ()))
    tn = (((0,), (0,)), ((), ()))
    ng = jnp.concatenate([ng_ref[...]] * HEAD_PAIR, axis=1)

    pairs = range(GLA_HEADS // HEAD_PAIR)

    def local(ci):
        rs = slice(ci * C, (ci + 1) * C)
        for p in pairs:
            ls = slice(p * LANES, (p + 1) * LANES)
            vs = slice(p * W, (p + 1) * W)
            b = loga_ref[rs, ls]
            shift = 1
            while shift < C:
                b = b + jnp.where(row >= shift, pltpu.roll(b, shift, axis=0), 0.0)
                shift *= 2
            b_last = b[C - 1:C, :]
            k2 = k_ref[rs, ls].astype(F32)
            qd = (q_ref[rs, ls].astype(F32) * scale * jnp.exp(b)).astype(BF16)
            kd = (k2 * jnp.exp(-b)).astype(BF16)
            ke = (k2 * jnp.exp(b_last - b)).astype(BF16)
            zk = jnp.zeros_like(kd)
            kd_blk = jnp.concatenate([jnp.where(lo_k, kd, zk), jnp.where(lo_k, zk, kd)], axis=0)
            attn = lax.dot_general(qd, kd_blk, nt, preferred_element_type=F32)
            lhs_ref[rs, vs] = jnp.concatenate([jnp.where(causal, attn, 0.0).astype(BF16), qd], axis=1)
            kv = lax.dot_general(ke, v_ref[rs, vs], tn, preferred_element_type=F32)
            kv_ref[p, ci] = jnp.where(own, kv, 0.0)
            dcol = jnp.exp(jnp.sum(jnp.where(eye, jnp.broadcast_to(b_last, (LANES, LANES)), 0.0),
                                   axis=1, keepdims=True))
            dec_ref[p, ci] = jnp.broadcast_to(dcol, (LANES, LANES))

    def scan(ci):
        for p in pairs:
            s_prev = s_ref[p]
            sprev_ref[p, ci] = s_prev.astype(BF16)
            s_ref[p] = jnp.tile(dec_ref[p, ci], (1, HEAD_PAIR)) * s_prev + kv_ref[p, ci]

    def output(ci):
        rs = slice(ci * C, (ci + 1) * C)
        for p in pairs:
            vs = slice(p * W, (p + 1) * W)
            v2 = v_ref[rs, vs]
            zv = jnp.zeros_like(v2)
            v_blk = jnp.concatenate([jnp.where(lo_v, v2, zv), jnp.where(lo_v, zv, v2)], axis=0)
            o = jnp.dot(lhs_ref[rs, vs], jnp.concatenate([v_blk, sprev_ref[p, ci]], axis=0),
                        preferred_element_type=F32)
            o2 = o * o
            ms = jnp.where(lo_v, jnp.sum(o2[:, :GLA_DV_PAD], axis=1, keepdims=True),
                           jnp.sum(o2[:, GLA_DV_PAD:], axis=1, keepdims=True))
            on = o * lax.rsqrt(ms * (1.0 / GLA_DV) + EPS) * ng
            og = (on * gate_ref[rs, vs].astype(F32)).astype(BF16)
            for hh in range(HEAD_PAIR):
                c0 = (p * HEAD_PAIR + hh) * GLA_DV
                o_ref[rs, c0:c0 + GLA_DV] = og[:, hh * GLA_DV_PAD:hh * GLA_DV_PAD + GLA_DV]

    def all_chunks(phase):
        return lambda: [phase(ci) for ci in range(n_chunks)]

    return all_chunks(local), all_chunks(scan), all_chunks(output)


def _fox_kernel(qn2_ref, kn2_ref, cend_ref, q_ref, cq_ref, k_ref, cx_ref, v_ref, gate_ref, o_ref,
                m_ref, acc_ref):
    blk = FOX_KEYS
    streams = range(q_ref.shape[0] // blk)
    pair = pl.program_id(0)
    qi = pl.program_id(1)
    nblk = pl.num_programs(1) * len(streams)
    lane = lax.broadcasted_iota(jnp.int32, (1, LANES), 1)
    lo_lanes = lane < FOX_DH
    reps = blk // LANES
    diag = [qi * len(streams) + s for s in streams]
    heads = [pair * HEAD_PAIR + hh for hh in range(HEAD_PAIR)]

    q = q_ref[...]
    zero = jnp.zeros_like(q)
    q_lo, q_hi = jnp.where(lo_lanes, q, zero), jnp.where(lo_lanes, zero, q)
    q_stack = [jnp.concatenate([q_lo[s * blk:(s + 1) * blk], q_hi[s * blk:(s + 1) * blk]], axis=0)
               for s in streams]
    g_row = lax.broadcasted_iota(jnp.int32, (CX_ROWS, CX_ROWS), 0)
    g_col = lax.broadcasted_iota(jnp.int32, (CX_ROWS, CX_ROWS), 1)
    xlane = lax.broadcasted_iota(jnp.int32, (HEAD_PAIR * blk, CX_ROWS), 1)
    cq = cq_ref[:, 0:CX_ROWS]

    def lane_map(h, shift):
        g = jnp.where((g_row == CX_ONE) & ((g_col == CX_HI + h) | (g_col == CX_MID + h)
                                           | (g_col == CX_LO + h)), 1.0, 0.0)
        if shift:
            for part, base in enumerate((CX_HI, CX_MID, CX_LO)):
                g = jnp.where((g_row == base + h) & (g_col == CX_ONE + part), -1.0, g)
        return g.astype(BF16)

    def q_aug(s, shift, void=None):
        cs = cq[s * blk:(s + 1) * blk]
        extra = jnp.concatenate([jnp.dot(cs, lane_map(h, shift), preferred_element_type=F32)
                                 for h in heads], axis=0)
        if void is not None:
            extra = jnp.where(jnp.logical_and(xlane == CX_ONE + 3, void), NEG_BIG, extra)
        return jnp.concatenate([q_stack[s], extra.astype(BF16)], axis=1)

    acc_ref[...] = jnp.zeros_like(acc_ref)
    qpos = lax.broadcasted_iota(jnp.int32, (blk, blk), 0)
    kpos = lax.broadcasted_iota(jnp.int32, (blk, blk), 1)

    def block(j):
        ks = pl.ds(pl.multiple_of(jnp.maximum(j, 0) * blk, blk), blk)
        k_aug = jnp.concatenate([k_ref[:, ks], cx_ref[0:CX_ROWS, ks]], axis=0)
        vb = v_ref[ks, :]
        one = jnp.ones_like(vb)
        return k_aug, (jnp.where(lo_lanes, vb, one), jnp.where(lo_lanes, one, vb))

    def step_online(s, j, qa, masked):
        rows = slice(s * blk, (s + 1) * blk)
        k_aug, vaug = block(j)
        s_all = jnp.dot(qa, k_aug, preferred_element_type=F32)
        for hh in range(HEAD_PAIR):
            sc = s_all[hh * blk:(hh + 1) * blk]
            if masked:
                sc = jnp.where(kpos <= qpos, sc, NEG_BIG)
            m_prev = m_ref[hh, rows]
            m_new = jnp.maximum(m_prev, jnp.max(sc, axis=1, keepdims=True))
            p = jnp.exp(sc - jnp.tile(m_new, (1, reps)))
            alpha = jnp.exp(m_prev - m_new)
            pv = jnp.dot(p.astype(BF16), vaug[hh], preferred_element_type=F32)
            acc_ref[hh, rows] = alpha * acc_ref[hh, rows] + pv
            m_ref[hh, rows] = m_new

    def step_direct(s, j, qa, masked, only=None):
        rows = slice(s * blk, (s + 1) * blk)
        k_aug, vaug = block(j)
        hsel = range(HEAD_PAIR) if only is None else (only,)
        lhs = qa if only is None else qa[only * blk:(only + 1) * blk]
        s_all = jnp.dot(lhs, k_aug, preferred_element_type=F32)
        for n, hh in enumerate(hsel):
            sc = s_all[n * blk:(n + 1) * blk]
            if masked:
                sc = jnp.where(kpos <= qpos, sc, NEG_BIG)
            acc_ref[hh, rows] += jnp.dot(jnp.exp(sc).astype(BF16), vaug[hh],
                                         preferred_element_type=F32)

    k_max2 = [kn2_ref[h] for h in heads]

    def norm2(s, hh):
        stat = (diag[s] * blk // PROJ_ROWS) * LANES
        return (NORM_SLACK * NORM_SLACK) * qn2_ref[stat + heads[hh]] * k_max2[hh]

    n2 = [[norm2(s, hh) for hh in range(HEAD_PAIR)] for s in streams]
    gap0 = [[FOX_SKIP_NATS + cend_ref[heads[hh] * nblk + jnp.maximum(diag[s] - 1, 0)]
             for hh in range(HEAD_PAIR)] for s in streams]

    def live_head(t, hh, direct):
        keep = False
        for s in streams:
            j = diag[s] - 1 - t
            gap = gap0[s][hh] - cend_ref[heads[hh] * nblk + jnp.maximum(j, 0)]
            bound2 = n2[s][hh] if direct else 4.0 * n2[s][hh]
            dead = jnp.logical_and(gap <= 0.0, bound2 <= gap * gap)
            keep = jnp.logical_or(keep, jnp.logical_and(j >= 0, jnp.logical_not(dead)))
        return keep

    def sweeps(stepper, direct):
        def sweep(back, masked=False, only=None):
            for s in streams:
                stepper(s, diag[s] - back, masked, only)

        def loop(t0, cond, only=None):
            def body(t):
                sweep(t + 1, only=only)
                return t + 1
            return lax.while_loop(cond, body, t0)

        def live(t, hh):
            return live_head(t, hh, direct)

        sweep(0, masked=True)
        sweep(1)
        if direct:
            t_both = loop(1, lambda t: jnp.logical_and(live(t, 0), live(t, 1)))
            for hh in range(HEAD_PAIR):
                loop(t_both, lambda t, hh=hh: live(t, hh), only=hh)
        else:
            loop(1, lambda t: jnp.logical_or(live(t, 0), live(t, 1)))

    direct_ok = True
    for s in streams:
        for hh in range(HEAD_PAIR):
            direct_ok = jnp.logical_and(direct_ok, n2[s][hh] <= FOX_DIRECT_NORM2)

    @pl.when(direct_ok)
    def _():
        q_dir = [q_aug(s, shift=True) for s in streams]
        q_void = [q_aug(s, shift=True, void=True) for s in streams]
        sweeps(lambda s, j, masked, only: step_direct(
            s, j, q_dir[s] if masked else jnp.where(j >= 0, q_dir[s], q_void[s]), masked, only),
            direct=True)

    @pl.when(jnp.logical_not(direct_ok))
    def _():
        m_ref[...] = jnp.full_like(m_ref, NEG_BIG)
        sweeps(lambda s, j, masked, only: step_online(
            s, j, q_aug(s, shift=False, void=j < 0), masked), direct=False)

    outs = []
    for hh in range(HEAD_PAIR):
        acc = acc_ref[hh]
        outs.append(acc / pltpu.roll(acc, FOX_DH, axis=1))
    o = jnp.where(lo_lanes, outs[0], outs[1])
    o_ref[...] = (o * gate_ref[...].astype(F32)).astype(BF16)


def _out_kernel(x_hbm, gla_ref, fox_ref, mq_ref, mg_ref, mem_ref, mem_g_ref, w_mem_ref,
                w_out_ref, fg_ref, o_ref, wo_ref, mk_ref, mv_ref, x_buf, x_sem):
    step = pl.program_id(0)
    n_steps = pl.num_programs(0)
    rows = o_ref.shape[0]

    def x_copy(s):
        slot = lax.rem(s, X_SLOTS)
        return pltpu.make_async_copy(x_hbm.at[pl.ds(pl.multiple_of(s * rows, rows), rows)],
                                     x_buf.at[slot], x_sem.at[slot])

    @pl.when(step == 0)
    def _():
        for s in range(X_SLOTS - 1):
            x_copy(s).start()
        wo_ref[...] = w_out_ref[...].astype(BF16)
        _memkv(mem_ref, mem_g_ref, w_mem_ref, mk_ref, mv_ref)

    @pl.when(step + (X_SLOTS - 1) < n_steps)
    def _():
        x_copy(step + (X_SLOTS - 1)).start()

    lane = lax.broadcasted_iota(jnp.int32, (1, LANES), 1)
    lo_lanes = lane < MEM_DH
    nt = (((1,), (1,)), ((), ()))
    mem_parts = []
    for p in range(MEM_HEADS // HEAD_PAIR):
        ls = slice(p * LANES, (p + 1) * LANES)
        q = mq_ref[:, ls]
        kb = mk_ref[:, ls]
        vb = mv_ref[:, ls]
        zero = jnp.zeros_like(q)
        one = jnp.ones_like(vb)
        qh = (jnp.where(lo_lanes, q, zero), jnp.where(lo_lanes, zero, q))
        vaug = (jnp.where(lo_lanes, vb, one), jnp.where(lo_lanes, one, vb))
        outs = []
        for hh in range(HEAD_PAIR):
            s = lax.dot_general(qh[hh], kb, nt, preferred_element_type=F32)
            pexp = jnp.exp(s - jnp.max(s, axis=1, keepdims=True))
            pv = jnp.dot(pexp.astype(BF16), vaug[hh], preferred_element_type=F32)
            outs.append(pv / pltpu.roll(pv, MEM_DH, axis=1))
        o = jnp.where(lo_lanes, outs[0], outs[1])
        mem_parts.append((o * mg_ref[:, ls].astype(F32)).astype(BF16))
    mixed = jnp.concatenate([gla_ref[...], fox_ref[...]] + mem_parts, axis=1)
    x_copy(step).wait()
    y = x_buf[lax.rem(step, X_SLOTS)] + jnp.dot(mixed, wo_ref[...], preferred_element_type=F32)
    o_ref[...] = y * _rms_scale(y, D_MODEL) * fg_ref[...]


def _params(*sem):
    return pltpu.CompilerParams(dimension_semantics=sem, vmem_limit_bytes=VMEM_LIMIT)


def _layer(x, mem, norm_g, w_in, w_alpha_up, b_alpha, b_forget, gla_norm_g,
           mem_norm_g, w_mem_kv, w_out, out_g):
    w_in_t = jnp.transpose(w_in[None], (0, 2, 1)).reshape(-1, LANES)
    T = x.shape[0]
    M = mem.shape[0]

    def rows(width, n=PROJ_ROWS):
        return pl.BlockSpec((n, width), lambda i: (i, 0))

    def whole(shape):
        return pl.BlockSpec(shape, lambda i: (0,) * len(shape))

    def once(shape):
        return pl.BlockSpec(shape, lambda i: (0,) * len(shape), pipeline_mode=pl.Buffered(1))

    bshape = lambda w: jax.ShapeDtypeStruct((T, w), BF16)
    nproj = T // PROJ_ROWS
    stat_spec = pl.BlockSpec((1, 1, LANES), lambda i: (i, 0, 0))
    stat_shape = jax.ShapeDtypeStruct((nproj, 1, LANES), F32)
    gla_pairs, gla_chunks = GLA_HEADS // HEAD_PAIR, PROJ_ROWS // GLA_CHUNK
    pair_w = HEAD_PAIR * GLA_DV_PAD
    (gla, fqvg, kx, mqg, crow, cq, qn2, kn2) = pl.pallas_call(
        _proj_kernel,
        grid=(nproj,),
        in_specs=[rows(D_MODEL), whole((1, D_MODEL)),
                  pl.BlockSpec(w_in_t.shape, lambda i: (0, 0), pipeline_mode=pl.Buffered(1)),
                  whole(w_alpha_up.shape), whole((1, GLA_HEADS * GLA_DK)),
                  whole((1, FOX_HEADS)), whole((1, GLA_DV))],
        out_specs=[rows(GLA_OUT_W), rows(3 * FOX_W),
                   pl.BlockSpec((FOX_W + LANES, PROJ_ROWS), lambda i: (0, i)),
                   rows(2 * MEM_W),
                   pl.BlockSpec((SUBLANES, PROJ_ROWS), lambda i: (0, i)), rows(LANES),
                   stat_spec, pl.BlockSpec((1, 1, LANES), lambda i: (0, 0, 0))],
        out_shape=[bshape(GLA_OUT_W), bshape(3 * FOX_W),
                   jax.ShapeDtypeStruct((FOX_W + LANES, T), BF16),
                   bshape(2 * MEM_W),
                   jax.ShapeDtypeStruct((SUBLANES, T), F32), bshape(LANES),
                   stat_shape, jax.ShapeDtypeStruct((1, 1, LANES), F32)],
        scratch_shapes=[
            pltpu.VMEM((SUBLANES, LANES), F32),
            pltpu.VMEM((IN_COLS_PAD, D_MODEL), BF16),
            pltpu.VMEM((SMALL_W, GLA_QK_W), F32),
            pltpu.VMEM((1, GLA_QK_W), F32),
            pltpu.VMEM((1, SMALL_W), F32),
            pltpu.VMEM((1, GLA_DV_PAD), F32),
            pltpu.VMEM((FOX_W, LANES), BF16),
            pltpu.VMEM((PROJ_ROWS, SMALL_W), F32),
            pltpu.VMEM((PROJ_ROWS, GLA_QK_W), BF16),
            pltpu.VMEM((PROJ_ROWS, GLA_QK_W), BF16),
            pltpu.VMEM((PROJ_ROWS, GLA_V_W), BF16),
            pltpu.VMEM((PROJ_ROWS, GLA_V_W), BF16),
            pltpu.VMEM((PROJ_ROWS, GLA_QK_W), F32),
            pltpu.VMEM((gla_pairs, LANES, pair_w), F32),
            pltpu.VMEM((PROJ_ROWS, GLA_V_W), BF16),
            pltpu.VMEM((gla_pairs, gla_chunks, LANES, pair_w), F32),
            pltpu.VMEM((gla_pairs, gla_chunks, LANES, LANES), F32),
            pltpu.VMEM((gla_pairs, gla_chunks, LANES, pair_w), BF16)],
        compiler_params=_params("arbitrary"),
        name="proj",
    )(x, norm_g[None, :], w_in_t, w_alpha_up, b_alpha[None, :], b_forget[None, :],
      gla_norm_g[None, :])

    cend = crow[:FOX_HEADS, FOX_KEYS - 1::FOX_KEYS].reshape(-1)
    fox_pairs = FOX_HEADS // HEAD_PAIR
    pair_rows = pl.BlockSpec((FOX_BLOCK, LANES), lambda p, i, *_: (i, p))
    fox = pl.pallas_call(
        _fox_kernel,
        grid_spec=pltpu.PrefetchScalarGridSpec(
            num_scalar_prefetch=3,
            grid=(fox_pairs, T // FOX_BLOCK),
            in_specs=[pair_rows,
                      pl.BlockSpec((FOX_BLOCK, LANES), lambda p, i, *_: (i, 0)),
                      pl.BlockSpec((LANES, T), lambda p, i, *_: (p, 0)),
                      pl.BlockSpec((LANES, T), lambda p, i, *_: (fox_pairs, 0)),
                      pl.BlockSpec((T, LANES), lambda p, i, *_: (0, fox_pairs + p)),
                      pl.BlockSpec((FOX_BLOCK, LANES),
                                   lambda p, i, *_: (i, 2 * fox_pairs + p))],
            out_specs=pair_rows,
            scratch_shapes=[pltpu.VMEM((HEAD_PAIR, FOX_BLOCK, LANES), F32),
                            pltpu.VMEM((HEAD_PAIR, FOX_BLOCK, LANES), F32)]),
        out_shape=bshape(FOX_W),
        compiler_params=_params("arbitrary", "arbitrary"),
        name="fox",
    )(qn2.reshape(-1), kn2.reshape(-1), cend, fqvg, cq, kx, kx, fqvg, fqvg)

    out = pl.pallas_call(
        _out_kernel,
        grid=(T // OUT_ROWS,),
        in_specs=[pl.BlockSpec(memory_space=pl.ANY),
                  rows(GLA_OUT_W, OUT_ROWS), rows(FOX_W, OUT_ROWS),
                  pl.BlockSpec((OUT_ROWS, MEM_W), lambda i: (i, 0)),
                  pl.BlockSpec((OUT_ROWS, MEM_W), lambda i: (i, 1)),
                  once(mem.shape), once((1, D_MODEL)), once(w_mem_kv.shape), once(w_out.shape),
                  whole((1, D_MODEL))],
        out_specs=rows(D_MODEL, OUT_ROWS),
        out_shape=jax.ShapeDtypeStruct((T, D_MODEL), F32),
        scratch_shapes=[pltpu.VMEM((D_MODEL, D_MODEL), BF16),
                        pltpu.VMEM((M, MEM_W), BF16), pltpu.VMEM((M, MEM_W), BF16),
                        pltpu.VMEM((X_SLOTS, OUT_ROWS, D_MODEL), F32),
                        pltpu.SemaphoreType.DMA((X_SLOTS,))],
        compiler_params=_params("arbitrary"),
        name="out",
    )(x, gla, fox, mqg, mqg, mem, mem_norm_g[None, :], w_mem_kv, w_out, out_g[None, :])
    return out


def kernel(x, mem, norm_g, w_in, w_alpha_up, b_alpha, b_forget, gla_norm_g, mem_norm_g,
           w_mem_kv, w_out, final_norm_g):
    assert x.shape[0] == 1 and mem.shape[0] == 1 and norm_g.shape[0] == 1
    assert x.shape[1] % max(PROJ_ROWS, FOX_BLOCK, OUT_ROWS) == 0
    out = _layer(x[0], mem[0], norm_g[0], w_in[0], w_alpha_up[0], b_alpha[0], b_forget[0],
                 gla_norm_g[0], mem_norm_g[0], w_mem_kv[0], w_out[0], final_norm_g)
    return out[None]
```

```python
import jax
import jax.numpy as jnp
from jax import lax
from jax.experimental import pallas as pl
from jax.experimental.pallas import tpu as pltpu

F32 = jnp.float32
BF16 = jnp.bfloat16

EPS = 1e-6
LANES = 128
SUBLANES = 8

D_MODEL = 1024
GLA_HEADS, GLA_DK, GLA_DV, GLA_RANK = 4, 48, 96, 16
GLA_DK_PAD = 64
GLA_DV_PAD = LANES
GLA_GATE_NORM = 16.0
GLA_CHUNK = 64
FOX_HEADS, FOX_DH = 6, 64
MEM_HEADS, MEM_DH = 4, 64
HEAD_PAIR = 2
GLA_QK_W = GLA_HEADS * GLA_DK_PAD
GLA_V_W = GLA_HEADS * GLA_DV_PAD
GLA_OUT_W = GLA_HEADS * GLA_DV
FOX_W = FOX_HEADS * FOX_DH
MEM_W = MEM_HEADS * MEM_DH
SMALL_W = LANES
FG_LANE0 = 0
LR_LANE0 = SUBLANES

_GROUPS = (("gq", GLA_QK_W), ("gk", GLA_QK_W), ("gv", GLA_OUT_W), ("gg", GLA_OUT_W),
           ("fq", FOX_W), ("fk", FOX_W), ("fv", FOX_W), ("fgate", FOX_W),
           ("mq", MEM_W), ("mg", MEM_W))
FG_COL = GLA_DK
LR_COL = GLA_DK_PAD + GLA_DK
_OFF = {}
_o = 0
for _n, _w in _GROUPS:
    _OFF[_n] = (_o, _o + _w)
    _o += _w
IN_COLS_PAD = _o

PROJ_ROWS = 1024
FOX_BLOCK = 4096
FOX_KEYS = 256
OUT_ROWS = 1024
X_SLOTS = 3
X_CHUNKS = 8
VMEM_LIMIT = 56 * 1024 * 1024

NEG_BIG = -1e30
FOX_SKIP_NATS = 105.0
NORM_SLACK = 1.02
FOX_DIRECT_NORM2 = 3600.0
CX_HI, CX_MID, CX_LO, CX_ONE = 0, 8, 16, 24
CX_ROWS = 32


def _log_sigmoid(z):
    return jnp.minimum(z, 0.0) - jnp.log(1.0 + jnp.exp(-jnp.abs(z)))


def _silu(z):
    return z / (1.0 + jnp.exp(-z))


def _rms_scale(v, width):
    return lax.rsqrt(jnp.sum(v * v, axis=-1, keepdims=True) * (1.0 / width) + EPS)


def _w_in_segments():
    qk, gw = GLA_HEADS * GLA_DK, GLA_HEADS * GLA_DV
    src = {}
    o = 0
    for name, width in (("gq", qk), ("gk", qk), ("gv", gw), ("lr", GLA_RANK), ("gg", gw),
                        ("fq", FOX_W), ("fk", FOX_W), ("fv", FOX_W), ("fg", FOX_HEADS),
                        ("fgate", FOX_W), ("mq", MEM_W), ("mg", MEM_W)):
        src[name] = o
        o += width
    segs = []
    for name, d, d_pad in (("gq", GLA_DK, GLA_DK_PAD), ("gk", GLA_DK, GLA_DK_PAD)):
        segs += [(src[name] + h * d, _OFF[name][0] + h * d_pad, d) for h in range(GLA_HEADS)]
    segs += [(src[name], _OFF[name][0], _OFF[name][1] - _OFF[name][0])
             for name in ("gv", "gg", "fq", "fk", "fv", "fgate", "mq", "mg")]
    segs += [(src["fg"], _OFF["gq"][0] + FG_COL, FOX_HEADS),
             (src["lr"], _OFF["gq"][0] + LR_COL, GLA_RANK)]
    return tuple(segs)


def _proj_kernel(x_ref, g_ref, w_in_ref, w_alpha_ref, b_alpha_ref, b_forget_ref, gla_g_ref,
                 gla_ref, fqvg_ref, kx_ref, mqg_ref, crow_ref, cq_ref, qn2_ref, kn2_ref,
                 carry_ref, wt_ref, wa_ref, ba_ref, bf_ref, ng_ref, seg_ref, small_ref,
                 gq_ref, gk_ref, gv_ref, gg_ref, loga_ref,
                 s_ref, lhs_ref, kv_ref, dec_ref, sprev_ref):
    rows = x_ref.shape[0]
    k_chunks = D_MODEL // LANES

    @pl.when(pl.program_id(0) == 0)
    def _():
        carry_ref[...] = jnp.zeros_like(carry_ref)
        s_ref[...] = jnp.zeros_like(s_ref)
        kn2_ref[...] = jnp.zeros_like(kn2_ref)
        gv_ref[...] = jnp.zeros_like(gv_ref)
        gg_ref[...] = jnp.zeros_like(gg_ref)
        small_ref[...] = jnp.zeros_like(small_ref)
        wa_ref[...] = jnp.zeros_like(wa_ref)
        ba_ref[...] = jnp.zeros_like(ba_ref)
        bf_ref[...] = jnp.zeros_like(bf_ref)
        ng_ref[...] = jnp.zeros_like(ng_ref)
        for h in range(GLA_HEADS):
            src = slice(h * GLA_DK, (h + 1) * GLA_DK)
            dst = slice(h * GLA_DK_PAD, h * GLA_DK_PAD + GLA_DK)
            wa_ref[LR_LANE0:LR_LANE0 + GLA_RANK, dst] = w_alpha_ref[:, src]
            ba_ref[:, dst] = b_alpha_ref[:, src]
        bf_ref[:, FG_LANE0:FG_LANE0 + FOX_HEADS] = b_forget_ref[...]
        ng_ref[:, 0:GLA_DV] = gla_g_ref[...]
        seg_ref[...] = (lax.broadcasted_iota(jnp.int32, seg_ref.shape, 0) // FOX_DH
                        == lax.broadcasted_iota(jnp.int32, seg_ref.shape, 1)).astype(BF16)
        wt_ref[...] = jnp.zeros_like(wt_ref)
        for s0, d0, width in _w_in_segments():
            for c in range(k_chunks):
                wt_ref[d0:d0 + width, c * LANES:(c + 1) * LANES] = (
                    w_in_ref[pl.ds(s0 * k_chunks + c, width, stride=k_chunks), :].astype(BF16))

    x = x_ref[...]
    xn = (x * _rms_scale(x, D_MODEL) * g_ref[...]).astype(BF16)
    nt = (((1,), (1,)), ((), ()))

    def proj(first, last):
        lo, hi = _OFF[first][0], _OFF[last][1]
        y = lax.dot_general(xn, wt_ref[lo:hi, :], nt, preferred_element_type=F32)
        return lambda name: y[:, _OFF[name][0] - lo:_OFF[name][1] - lo]

    gla = proj("gq", "gg")
    tail = proj("mq", "mg")
    gq_all = gla("gq")
    small_ref[:, FG_LANE0:FG_LANE0 + SUBLANES] = gq_all[:, FG_COL:FG_COL + SUBLANES]
    small_ref[:, LR_LANE0:LR_LANE0 + GLA_RANK] = gq_all[:, LR_COL:LR_COL + GLA_RANK]
    small = small_ref[...]
    logf = _log_sigmoid(small + bf_ref[...])
    c = logf.T[0:SUBLANES, :]
    lane = lax.broadcasted_iota(jnp.int32, c.shape, 1)
    shift = 1
    while shift < rows:
        c = c + jnp.where(lane >= shift, pltpu.roll(c, shift, axis=1), 0.0)
        shift *= 2
    c = c + carry_ref[:, 0:1]
    crow_ref[...] = c
    carry_ref[...] = jnp.broadcast_to(c[:, rows - 1:rows], carry_ref.shape)
    neg = -c
    hi = neg.astype(BF16).astype(F32)
    mid = (neg - hi).astype(BF16).astype(F32)
    low = neg - hi - mid
    parts = jnp.concatenate(
        [hi, mid, low, jnp.ones_like(c), jnp.zeros((LANES - 4 * SUBLANES, rows), F32)], axis=0)
    kx_ref[FOX_W:, :] = parts.astype(BF16)
    cq_ref[...] = parts.T.astype(BF16)

    z = (jnp.dot(small.astype(BF16), wa_ref[...].astype(BF16), preferred_element_type=F32)
         + ba_ref[...])
    loga_ref[...] = _log_sigmoid(z) * (1.0 / GLA_GATE_NORM)
    q_lane = lax.broadcasted_iota(jnp.int32, (1, GLA_QK_W), 1)
    gq_ref[...] = jnp.where(q_lane % GLA_DK_PAD < GLA_DK, gq_all, 0.0).astype(BF16)
    gk_ref[...] = gla("gk").astype(BF16)
    gv, gg = gla("gv").astype(BF16), _silu(gla("gg")).astype(BF16)
    for h in range(GLA_HEADS):
        src = slice(h * GLA_DV, (h + 1) * GLA_DV)
        dst = slice(h * GLA_DV_PAD, h * GLA_DV_PAD + GLA_DV)
        gv_ref[:, dst] = gv[:, src]
        gg_ref[:, dst] = gg[:, src]
    gla_local, gla_scan, gla_output = _gla_block(
        gq_ref, gk_ref, gv_ref, loga_ref, gg_ref, ng_ref, gla_ref,
        s_ref, lhs_ref, kv_ref, dec_ref, sprev_ref)

    def max_sq_norm(v):
        v32 = v.astype(F32)
        n2 = jnp.dot((v32 * v32).astype(BF16), seg_ref[...], preferred_element_type=F32)
        return jnp.max(n2, axis=0, keepdims=True)

    gla_local()
    fox_qk = proj("fq", "fk")
    mqg_ref[:, :MEM_W] = (tail("mq") * MEM_DH ** -0.5).astype(BF16)
    mqg_ref[:, MEM_W:] = _silu(tail("mg")).astype(BF16)
    gla_scan()
    fq = (fox_qk("fq") * FOX_DH ** -0.5).astype(BF16)
    fk = fox_qk("fk").astype(BF16)
    fqvg_ref[:, :FOX_W] = fq
    kx_ref[:FOX_W, :] = fox_qk("fk").T.astype(BF16)
    gla_output()
    fox_vg = proj("fv", "fgate")
    fqvg_ref[:, FOX_W:2 * FOX_W] = fox_vg("fv").astype(BF16)
    fqvg_ref[:, 2 * FOX_W:] = _silu(fox_vg("fgate")).astype(BF16)
    qn2_ref[0] = max_sq_norm(fq)
    kn2_ref[0] = jnp.maximum(kn2_ref[0], max_sq_norm(fk))


def _memkv(mem_ref, g_ref, w_ref, mk_ref, mv_ref):
    m = mem_ref[...]
    mn = (m * _rms_scale(m, D_MODEL) * g_ref[...]).astype(BF16)
    kv = jnp.dot(mn, w_ref[...].astype(BF16), preferred_element_type=F32)
    mk_ref[...] = kv[:, :MEM_W].astype(BF16)
    mv_ref[...] = kv[:, MEM_W:].astype(BF16)


def _gla_block(q_ref, k_ref, v_ref, loga_ref, gate_ref, ng_ref, o_ref,
               s_ref, lhs_ref, kv_ref, dec_ref, sprev_ref):
    C = GLA_CHUNK
    W = HEAD_PAIR * GLA_DV_PAD
    n_chunks = q_ref.shape[0] // C

    row = lax.broadcasted_iota(jnp.int32, (C, LANES), 0)
    lane = lax.broadcasted_iota(jnp.int32, (C, LANES), 1)
    lo_k = lane < GLA_DK_PAD
    causal = row >= jnp.where(lo_k, lane, lane - GLA_DK_PAD)
    lo_v = lax.broadcasted_iota(jnp.int32, (C, W), 1) < GLA_DV_PAD
    st_row = lax.broadcasted_iota(jnp.int32, (LANES, W), 0)
    st_lane = lax.broadcasted_iota(jnp.int32, (LANES, W), 1)
    own = (st_row < GLA_DK_PAD) == (st_lane < GLA_DV_PAD)
    eye = (lax.broadcasted_iota(jnp.int32, (LANES, LANES), 0)
           == lax.broadcasted_iota(jnp.int32, (LANES, LANES), 1))
    scale = GLA_DK ** -0.5
    nt = (((1,), (1,)), ((), ()))
    tn = (((0,), (0,)), ((), ()))
    ng = jnp.concatenate([ng_ref[...]] * HEAD_PAIR, axis=1)

    pairs = range(GLA_HEADS // HEAD_PAIR)

    def local(ci):
        rs = slice(ci * C, (ci + 1) * C)
        for p in pairs:
            ls = slice(p * LANES, (p + 1) * LANES)
            vs = slice(p * W, (p + 1) * W)
            b = loga_ref[rs, ls]
            shift = 1
            while shift < C:
                b = b + jnp.where(row >= shift, pltpu.roll(b, shift, axis=0), 0.0)
                shift *= 2
            b_last = b[C - 1:C, :]
            k2 = k_ref[rs, ls].astype(F32)
            qd = (q_ref[rs, ls].astype(F32) * scale * jnp.exp(b)).astype(BF16)
            kd = (k2 * jnp.exp(-b)).astype(BF16)
            ke = (k2 * jnp.exp(b_last - b)).astype(BF16)
            zk = jnp.zeros_like(kd)
            kd_blk = jnp.concatenate([jnp.where(lo_k, kd, zk), jnp.where(lo_k, zk, kd)], axis=0)
            attn = lax.dot_general(qd, kd_blk, nt, preferred_element_type=F32)
            lhs_ref[rs, vs] = jnp.concatenate([jnp.where(causal, attn, 0.0).astype(BF16), qd], axis=1)
            kv = lax.dot_general(ke, v_ref[rs, vs], tn, preferred_element_type=F32)
            kv_ref[p, ci] = jnp.where(own, kv, 0.0)
            dcol = jnp.exp(jnp.sum(jnp.where(eye, jnp.broadcast_to(b_last, (LANES, LANES)), 0.0),
                                   axis=1, keepdims=True))
            dec_ref[p, ci] = jnp.broadcast_to(dcol, (LANES, LANES))

    def scan(ci):
        for p in pairs:
            s_prev = s_ref[p]
            sprev_ref[p, ci] = s_prev.astype(BF16)
            s_ref[p] = jnp.tile(dec_ref[p, ci], (1, HEAD_PAIR)) * s_prev + kv_ref[p, ci]

    def output(ci):
        rs = slice(ci * C, (ci + 1) * C)
        for p in pairs:
            vs = slice(p * W, (p + 1) * W)
            v2 = v_ref[rs, vs]
            zv = jnp.zeros_like(v2)
            v_blk = jnp.concatenate([jnp.where(lo_v, v2, zv), jnp.where(lo_v, zv, v2)], axis=0)
            o = jnp.dot(lhs_ref[rs, vs], jnp.concatenate([v_blk, sprev_ref[p, ci]], axis=0),
                        preferred_element_type=F32)
            o2 = o * o
            ms = jnp.where(lo_v, jnp.sum(o2[:, :GLA_DV_PAD], axis=1, keepdims=True),
                           jnp.sum(o2[:, GLA_DV_PAD:], axis=1, keepdims=True))
            on = o * lax.rsqrt(ms * (1.0 / GLA_DV) + EPS) * ng
            og = (on * gate_ref[rs, vs].astype(F32)).astype(BF16)
            for hh in range(HEAD_PAIR):
                c0 = (p * HEAD_PAIR + hh) * GLA_DV
                o_ref[rs, c0:c0 + GLA_DV] = og[:, hh * GLA_DV_PAD:hh * GLA_DV_PAD + GLA_DV]

    def all_chunks(phase):
        return lambda: [phase(ci) for ci in range(n_chunks)]

    return all_chunks(local), all_chunks(scan), all_chunks(output)


def _fox_kernel(qn2_ref, kn2_ref, cend_ref, q_ref, cq_ref, k_ref, cx_ref, v_ref, gate_ref, o_ref,
                m_ref, acc_ref):
    blk = FOX_KEYS
    streams = range(q_ref.shape[0] // blk)
    pair = pl.program_id(0)
    qi = pl.program_id(1)
    nblk = pl.num_programs(1) * len(streams)
    lane = lax.broadcasted_iota(jnp.int32, (1, LANES), 1)
    lo_lanes = lane < FOX_DH
    reps = blk // LANES
    diag = [qi * len(streams) + s for s in streams]
    heads = [pair * HEAD_PAIR + hh for hh in range(HEAD_PAIR)]

    q = q_ref[...]
    zero = jnp.zeros_like(q)
    q_lo, q_hi = jnp.where(lo_lanes, q, zero), jnp.where(lo_lanes, zero, q)
    q_stack = [jnp.concatenate([q_lo[s * blk:(s + 1) * blk], q_hi[s * blk:(s + 1) * blk]], axis=0)
               for s in streams]
    g_row = lax.broadcasted_iota(jnp.int32, (CX_ROWS, CX_ROWS), 0)
    g_col = lax.broadcasted_iota(jnp.int32, (CX_ROWS, CX_ROWS), 1)
    xlane = lax.broadcasted_iota(jnp.int32, (HEAD_PAIR * blk, CX_ROWS), 1)
    cq = cq_ref[:, 0:CX_ROWS]

    def lane_map(h, shift):
        g = jnp.where((g_row == CX_ONE) & ((g_col == CX_HI + h) | (g_col == CX_MID + h)
                                           | (g_col == CX_LO + h)), 1.0, 0.0)
        if shift:
            for part, base in enumerate((CX_HI, CX_MID, CX_LO)):
                g = jnp.where((g_row == base + h) & (g_col == CX_ONE + part), -1.0, g)
        return g.astype(BF16)

    def q_aug(s, shift, void=None):
        cs = cq[s * blk:(s + 1) * blk]
        extra = jnp.concatenate([jnp.dot(cs, lane_map(h, shift), preferred_element_type=F32)
                                 for h in heads], axis=0)
        if void is not None:
            extra = jnp.where(jnp.logical_and(xlane == CX_ONE + 3, void), NEG_BIG, extra)
        return jnp.concatenate([q_stack[s], extra.astype(BF16)], axis=1)

    acc_ref[...] = jnp.zeros_like(acc_ref)
    qpos = lax.broadcasted_iota(jnp.int32, (blk, blk), 0)
    kpos = lax.broadcasted_iota(jnp.int32, (blk, blk), 1)

    def block(j):
        ks = pl.ds(pl.multiple_of(jnp.maximum(j, 0) * blk, blk), blk)
        k_aug = jnp.concatenate([k_ref[:, ks], cx_ref[0:CX_ROWS, ks]], axis=0)
        vb = v_ref[ks, :]
        one = jnp.ones_like(vb)
        return k_aug, (jnp.where(lo_lanes, vb, one), jnp.where(lo_lanes, one, vb))

    def step_online(s, j, qa, masked):
        rows = slice(s * blk, (s + 1) * blk)
        k_aug, vaug = block(j)
        s_all = jnp.dot(qa, k_aug, preferred_element_type=F32)
        for hh in range(HEAD_PAIR):
            sc = s_all[hh * blk:(hh + 1) * blk]
            if masked:
                sc = jnp.where(kpos <= qpos, sc, NEG_BIG)
            m_prev = m_ref[hh, rows]
            m_new = jnp.maximum(m_prev, jnp.max(sc, axis=1, keepdims=True))
            p = jnp.exp(sc - jnp.tile(m_new, (1, reps)))
            alpha = jnp.exp(m_prev - m_new)
            pv = jnp.dot(p.astype(BF16), vaug[hh], preferred_element_type=F32)
            acc_ref[hh, rows] = alpha * acc_ref[hh, rows] + pv
            m_ref[hh, rows] = m_new

    def step_direct(s, j, qa, masked, only=None):
        rows = slice(s * blk, (s + 1) * blk)
        k_aug, vaug = block(j)
        hsel = range(HEAD_PAIR) if only is None else (only,)
        lhs = qa if only is None else qa[only * blk:(only + 1) * blk]
        s_all = jnp.dot(lhs, k_aug, preferred_element_type=F32)
        for n, hh in enumerate(hsel):
            sc = s_all[n * blk:(n + 1) * blk]
            if masked:
                sc = jnp.where(kpos <= qpos, sc, NEG_BIG)
            acc_ref[hh, rows] += jnp.dot(jnp.exp(sc).astype(BF16), vaug[hh],
                                         preferred_element_type=F32)

    k_max2 = [kn2_ref[h] for h in heads]

    def norm2(s, hh):
        stat = (diag[s] * blk // PROJ_ROWS) * LANES
        return (NORM_SLACK * NORM_SLACK) * qn2_ref[stat + heads[hh]] * k_max2[hh]

    n2 = [[norm2(s, hh) for hh in range(HEAD_PAIR)] for s in streams]
    gap0 = [[FOX_SKIP_NATS + cend_ref[heads[hh] * nblk + jnp.maximum(diag[s] - 1, 0)]
             for hh in range(HEAD_PAIR)] for s in streams]

    def live_head(t, hh, direct):
        keep = False
        for s in streams:
            j = diag[s] - 1 - t
            gap = gap0[s][hh] - cend_ref[heads[hh] * nblk + jnp.maximum(j, 0)]
            bound2 = n2[s][hh] if direct else 4.0 * n2[s][hh]
            dead = jnp.logical_and(gap <= 0.0, bound2 <= gap * gap)
            keep = jnp.logical_or(keep, jnp.logical_and(j >= 0, jnp.logical_not(dead)))
        return keep

    def sweeps(stepper, direct):
        def sweep(back, masked=False, only=None):
            for s in streams:
                stepper(s, diag[s] - back, masked, only)

        def loop(t0, cond, only=None):
            def body(t):
                sweep(t + 1, only=only)
                return t + 1
            return lax.while_loop(cond, body, t0)

        def live(t, hh):
            return live_head(t, hh, direct)

        sweep(0, masked=True)
        sweep(1)
        if direct:
            t_both = loop(1, lambda t: jnp.logical_and(live(t, 0), live(t, 1)))
            for hh in range(HEAD_PAIR):
                loop(t_both, lambda t, hh=hh: live(t, hh), only=hh)
        else:
            loop(1, lambda t: jnp.logical_or(live(t, 0), live(t, 1)))

    direct_ok = True
    for s in streams:
        for hh in range(HEAD_PAIR):
            direct_ok = jnp.logical_and(direct_ok, n2[s][hh] <= FOX_DIRECT_NORM2)

    @pl.when(direct_ok)
    def _():
        q_dir = [q_aug(s, shift=True) for s in streams]
        q_void = [q_aug(s, shift=True, void=True) for s in streams]
        sweeps(lambda s, j, masked, only: step_direct(
            s, j, q_dir[s] if masked else jnp.where(j >= 0, q_dir[s], q_void[s]), masked, only),
            direct=True)

    @pl.when(jnp.logical_not(direct_ok))
    def _():
        m_ref[...] = jnp.full_like(m_ref, NEG_BIG)
        sweeps(lambda s, j, masked, only: step_online(
            s, j, q_aug(s, shift=False, void=j < 0), masked), direct=False)

    outs = []
    for hh in range(HEAD_PAIR):
        acc = acc_ref[hh]
        outs.append(acc / pltpu.roll(acc, FOX_DH, axis=1))
    o = jnp.where(lo_lanes, outs[0], outs[1])
    o_ref[...] = (o * gate_ref[...].astype(F32)).astype(BF16)


def _out_kernel(x_hbm, gla_ref, fox_ref, mq_ref, mg_ref, mem_ref, mem_g_ref, w_mem_ref,
                w_out_ref, fg_ref, o_ref, wo_ref, mk_ref, mv_ref, x_buf, x_sem):
    step = pl.program_id(0)
    n_steps = pl.num_programs(0)
    rows = o_ref.shape[0]

    chunk = rows // X_CHUNKS

    def x_copies(s):
        slot = lax.rem(s, X_SLOTS)
        return [pltpu.make_async_copy(
            x_hbm.at[pl.ds(pl.multiple_of(s * rows + c * chunk, chunk), chunk)],
            x_buf.at[slot, pl.ds(c * chunk, chunk)], x_sem.at[slot, c]) for c in range(X_CHUNKS)]

    @pl.when(step == 0)
    def _():
        for s in range(X_SLOTS - 1):
            for cp in x_copies(s):
                cp.start()
        wo_ref[...] = w_out_ref[...].astype(BF16)
        _memkv(mem_ref, mem_g_ref, w_mem_ref, mk_ref, mv_ref)

    @pl.when(step + (X_SLOTS - 1) < n_steps)
    def _():
        for cp in x_copies(step + (X_SLOTS - 1)):
            cp.start()

    lane = lax.broadcasted_iota(jnp.int32, (1, LANES), 1)
    lo_lanes = lane < MEM_DH
    nt = (((1,), (1,)), ((), ()))
    mem_parts = []
    for p in range(MEM_HEADS // HEAD_PAIR):
        ls = slice(p * LANES, (p + 1) * LANES)
        q = mq_ref[:, ls]
        kb = mk_ref[:, ls]
        vb = mv_ref[:, ls]
        zero = jnp.zeros_like(q)
        one = jnp.ones_like(vb)
        qh = (jnp.where(lo_lanes, q, zero), jnp.where(lo_lanes, zero, q))
        vaug = (jnp.where(lo_lanes, vb, one), jnp.where(lo_lanes, one, vb))
        outs = []
        for hh in range(HEAD_PAIR):
            s = lax.dot_general(qh[hh], kb, nt, preferred_element_type=F32)
            pexp = jnp.exp(s - jnp.max(s, axis=1, keepdims=True))
            pv = jnp.dot(pexp.astype(BF16), vaug[hh], preferred_element_type=F32)
            outs.append(pv / pltpu.roll(pv, MEM_DH, axis=1))
        o = jnp.where(lo_lanes, outs[0], outs[1])
        mem_parts.append((o * mg_ref[:, ls].astype(F32)).astype(BF16))
    mixed = jnp.concatenate([gla_ref[...], fox_ref[...]] + mem_parts, axis=1)
    for cp in x_copies(step):
        cp.wait()
    y = x_buf[lax.rem(step, X_SLOTS)] + jnp.dot(mixed, wo_ref[...], preferred_element_type=F32)
    o_ref[...] = y * _rms_scale(y, D_MODEL) * fg_ref[...]


def _params(*sem):
    return pltpu.CompilerParams(dimension_semantics=sem, vmem_limit_bytes=VMEM_LIMIT)


def _layer(x, mem, norm_g, w_in, w_alpha_up, b_alpha, b_forget, gla_norm_g,
           mem_norm_g, w_mem_kv, w_out, out_g):
    w_in_t = jnp.transpose(w_in[None], (0, 2, 1)).reshape(-1, LANES)
    T = x.shape[0]
    M = mem.shape[0]

    def rows(width, n=PROJ_ROWS):
        return pl.BlockSpec((n, width), lambda i: (i, 0))

    def whole(shape):
        return pl.BlockSpec(shape, lambda i: (0,) * len(shape))

    def once(shape):
        return pl.BlockSpec(shape, lambda i: (0,) * len(shape), pipeline_mode=pl.Buffered(1))

    bshape = lambda w: jax.ShapeDtypeStruct((T, w), BF16)
    nproj = T // PROJ_ROWS
    stat_spec = pl.BlockSpec((1, 1, LANES), lambda i: (i, 0, 0))
    stat_shape = jax.ShapeDtypeStruct((nproj, 1, LANES), F32)
    gla_pairs, gla_chunks = GLA_HEADS // HEAD_PAIR, PROJ_ROWS // GLA_CHUNK
    pair_w = HEAD_PAIR * GLA_DV_PAD
    (gla, fqvg, kx, mqg, crow, cq, qn2, kn2) = pl.pallas_call(
        _proj_kernel,
        grid=(nproj,),
        in_specs=[rows(D_MODEL), whole((1, D_MODEL)),
                  pl.BlockSpec(w_in_t.shape, lambda i: (0, 0), pipeline_mode=pl.Buffered(1)),
                  whole(w_alpha_up.shape), whole((1, GLA_HEADS * GLA_DK)),
                  whole((1, FOX_HEADS)), whole((1, GLA_DV))],
        out_specs=[rows(GLA_OUT_W), rows(3 * FOX_W),
                   pl.BlockSpec((FOX_W + LANES, PROJ_ROWS), lambda i: (0, i)),
                   rows(2 * MEM_W),
                   pl.BlockSpec((SUBLANES, PROJ_ROWS), lambda i: (0, i)), rows(LANES),
                   stat_spec, pl.BlockSpec((1, 1, LANES), lambda i: (0, 0, 0))],
        out_shape=[bshape(GLA_OUT_W), bshape(3 * FOX_W),
                   jax.ShapeDtypeStruct((FOX_W + LANES, T), BF16),
                   bshape(2 * MEM_W),
                   jax.ShapeDtypeStruct((SUBLANES, T), F32), bshape(LANES),
                   stat_shape, jax.ShapeDtypeStruct((1, 1, LANES), F32)],
        scratch_shapes=[
            pltpu.VMEM((SUBLANES, LANES), F32),
            pltpu.VMEM((IN_COLS_PAD, D_MODEL), BF16),
            pltpu.VMEM((SMALL_W, GLA_QK_W), F32),
            pltpu.VMEM((1, GLA_QK_W), F32),
            pltpu.VMEM((1, SMALL_W), F32),
            pltpu.VMEM((1, GLA_DV_PAD), F32),
            pltpu.VMEM((FOX_W, LANES), BF16),
            pltpu.VMEM((PROJ_ROWS, SMALL_W), F32),
            pltpu.VMEM((PROJ_ROWS, GLA_QK_W), BF16),
            pltpu.VMEM((PROJ_ROWS, GLA_QK_W), BF16),
            pltpu.VMEM((PROJ_ROWS, GLA_V_W), BF16),
            pltpu.VMEM((PROJ_ROWS, GLA_V_W), BF16),
            pltpu.VMEM((PROJ_ROWS, GLA_QK_W), F32),
            pltpu.VMEM((gla_pairs, LANES, pair_w), F32),
            pltpu.VMEM((PROJ_ROWS, GLA_V_W), BF16),
            pltpu.VMEM((gla_pairs, gla_chunks, LANES, pair_w), F32),
            pltpu.VMEM((gla_pairs, gla_chunks, LANES, LANES), F32),
            pltpu.VMEM((gla_pairs, gla_chunks, LANES, pair_w), BF16)],
        compiler_params=_params("arbitrary"),
        name="proj",
    )(x, norm_g[None, :], w_in_t, w_alpha_up, b_alpha[None, :], b_forget[None, :],
      gla_norm_g[None, :])

    cend = crow[:FOX_HEADS, FOX_KEYS - 1::FOX_KEYS].reshape(-1)
    fox_pairs = FOX_HEADS // HEAD_PAIR
    pair_rows = pl.BlockSpec((FOX_BLOCK, LANES), lambda p, i, *_: (i, p))
    fox = pl.pallas_call(
        _fox_kernel,
        grid_spec=pltpu.PrefetchScalarGridSpec(
            num_scalar_prefetch=3,
            grid=(fox_pairs, T // FOX_BLOCK),
            in_specs=[pair_rows,
                      pl.BlockSpec((FOX_BLOCK, LANES), lambda p, i, *_: (i, 0)),
                      pl.BlockSpec((LANES, T), lambda p, i, *_: (p, 0)),
                      pl.BlockSpec((LANES, T), lambda p, i, *_: (fox_pairs, 0)),
                      pl.BlockSpec((T, LANES), lambda p, i, *_: (0, fox_pairs + p)),
                      pl.BlockSpec((FOX_BLOCK, LANES),
                                   lambda p, i, *_: (i, 2 * fox_pairs + p))],
            out_specs=pair_rows,
            scratch_shapes=[pltpu.VMEM((HEAD_PAIR, FOX_BLOCK, LANES), F32),
                            pltpu.VMEM((HEAD_PAIR, FOX_BLOCK, LANES), F32)]),
        out_shape=bshape(FOX_W),
        compiler_params=_params("arbitrary", "arbitrary"),
        name="fox",
    )(qn2.reshape(-1), kn2.reshape(-1), cend, fqvg, cq, kx, kx, fqvg, fqvg)

    out = pl.pallas_call(
        _out_kernel,
        grid=(T // OUT_ROWS,),
        in_specs=[pl.BlockSpec(memory_space=pl.ANY),
                  rows(GLA_OUT_W, OUT_ROWS), rows(FOX_W, OUT_ROWS),
                  pl.BlockSpec((OUT_ROWS, MEM_W), lambda i: (i, 0)),
                  pl.BlockSpec((OUT_ROWS, MEM_W), lambda i: (i, 1)),
                  once(mem.shape), once((1, D_MODEL)), once(w_mem_kv.shape), once(w_out.shape),
                  whole((1, D_MODEL))],
        out_specs=rows(D_MODEL, OUT_ROWS),
        out_shape=jax.ShapeDtypeStruct((T, D_MODEL), F32),
        scratch_shapes=[pltpu.VMEM((D_MODEL, D_MODEL), BF16),
                        pltpu.VMEM((M, MEM_W), BF16), pltpu.VMEM((M, MEM_W), BF16),
                        pltpu.VMEM((X_SLOTS, OUT_ROWS, D_MODEL), F32),
                        pltpu.SemaphoreType.DMA((X_SLOTS, X_CHUNKS))],
        compiler_params=_params("arbitrary"),
        name="out",
    )(x, gla, fox, mqg, mqg, mem, mem_norm_g[None, :], w_mem_kv, w_out, out_g[None, :])
    return out


def kernel(x, mem, norm_g, w_in, w_alpha_up, b_alpha, b_forget, gla_norm_g, mem_norm_g,
           w_mem_kv, w_out, final_norm_g):
    assert x.shape[0] == 1 and mem.shape[0] == 1 and norm_g.shape[0] == 1
    assert x.shape[1] % max(PROJ_ROWS, FOX_BLOCK, OUT_ROWS) == 0
    out = _layer(x[0], mem[0], norm_g[0], w_in[0], w_alpha_up[0], b_alpha[0], b_forget[0],
                 gla_norm_g[0], mem_norm_g[0], w_mem_kv[0], w_out[0], final_norm_g)
    return out[None]
```

```python
import jax
import jax.numpy as jnp
from jax import lax
from jax.experimental import pallas as pl
from jax.experimental.pallas import tpu as pltpu

F32 = jnp.float32
BF16 = jnp.bfloat16

EPS = 1e-6
LANES = 128
SUBLANES = 8

D_MODEL = 1024
GLA_HEADS, GLA_DK, GLA_DV, GLA_RANK = 4, 48, 96, 16
GLA_DK_PAD = 64
GLA_DV_PAD = LANES
GLA_GATE_NORM = 16.0
GLA_CHUNK = 64
FOX_HEADS, FOX_DH = 6, 64
MEM_HEADS, MEM_DH = 4, 64
HEAD_PAIR = 2
GLA_QK_W = GLA_HEADS * GLA_DK_PAD
GLA_V_W = GLA_HEADS * GLA_DV_PAD
GLA_OUT_W = GLA_HEADS * GLA_DV
FOX_W = FOX_HEADS * FOX_DH
MEM_W = MEM_HEADS * MEM_DH
SMALL_W = LANES
FG_LANE0 = 0
LR_LANE0 = SUBLANES

_GROUPS = (("gq", GLA_QK_W), ("gk", GLA_QK_W), ("gv", GLA_OUT_W), ("gg", GLA_OUT_W),
           ("fq", FOX_W), ("fk", FOX_W), ("fv", FOX_W), ("fgate", FOX_W),
           ("mq", MEM_W), ("mg", MEM_W))
FG_COL = GLA_DK
LR_COL = GLA_DK_PAD + GLA_DK
_OFF = {}
_o = 0
for _n, _w in _GROUPS:
    _OFF[_n] = (_o, _o + _w)
    _o += _w
IN_COLS_PAD = _o

PROJ_ROWS = 1024
FOX_BLOCK = 4096
FOX_KEYS = 256
OUT_ROWS = 1024
X_SLOTS = 3
X_CHUNKS = 8
VMEM_LIMIT = 56 * 1024 * 1024

NEG_BIG = -1e30
FOX_SKIP_NATS = 105.0
NORM_SLACK = 1.02
FOX_DIRECT_NORM2 = 3600.0
CX_HI, CX_MID, CX_LO, CX_ONE = 0, 8, 16, 24
CX_ROWS = 32


def _log_sigmoid(z):
    return jnp.minimum(z, 0.0) - jnp.log(1.0 + jnp.exp(-jnp.abs(z)))


def _silu(z):
    return z / (1.0 + jnp.exp(-z))


def _rms_scale(v, width):
    return lax.rsqrt(jnp.sum(v * v, axis=-1, keepdims=True) * (1.0 / width) + EPS)


def _w_in_segments():
    qk, gw = GLA_HEADS * GLA_DK, GLA_HEADS * GLA_DV
    src = {}
    o = 0
    for name, width in (("gq", qk), ("gk", qk), ("gv", gw), ("lr", GLA_RANK), ("gg", gw),
                        ("fq", FOX_W), ("fk", FOX_W), ("fv", FOX_W), ("fg", FOX_HEADS),
                        ("fgate", FOX_W), ("mq", MEM_W), ("mg", MEM_W)):
        src[name] = o
        o += width
    segs = []
    for name, d, d_pad in (("gq", GLA_DK, GLA_DK_PAD), ("gk", GLA_DK, GLA_DK_PAD)):
        segs += [(src[name] + h * d, _OFF[name][0] + h * d_pad, d) for h in range(GLA_HEADS)]
    segs += [(src[name], _OFF[name][0], _OFF[name][1] - _OFF[name][0])
             for name in ("gv", "gg", "fq", "fk", "fv", "fgate", "mq", "mg")]
    segs += [(src["fg"], _OFF["gq"][0] + FG_COL, FOX_HEADS),
             (src["lr"], _OFF["gq"][0] + LR_COL, GLA_RANK)]
    return tuple(segs)


def _proj_kernel(x_ref, g_ref, w_in_ref, w_alpha_ref, b_alpha_ref, b_forget_ref, gla_g_ref,
                 gla_ref, fqvg_ref, kx_ref, mqg_ref, crow_ref, cq_ref, qn2_ref, kn2_ref,
                 carry_ref, wt_ref, wa_ref, ba_ref, bf_ref, ng_ref, seg_ref, small_ref,
                 gq_ref, gk_ref, gv_ref, gg_ref, loga_ref,
                 s_ref, lhs_ref, kv_ref, dec_ref, sprev_ref):
    rows = x_ref.shape[0]
    k_chunks = D_MODEL // LANES

    @pl.when(pl.program_id(0) == 0)
    def _():
        carry_ref[...] = jnp.zeros_like(carry_ref)
        s_ref[...] = jnp.zeros_like(s_ref)
        kn2_ref[...] = jnp.zeros_like(kn2_ref)
        gv_ref[...] = jnp.zeros_like(gv_ref)
        gg_ref[...] = jnp.zeros_like(gg_ref)
        small_ref[...] = jnp.zeros_like(small_ref)
        wa_ref[...] = jnp.zeros_like(wa_ref)
        ba_ref[...] = jnp.zeros_like(ba_ref)
        bf_ref[...] = jnp.zeros_like(bf_ref)
        ng_ref[...] = jnp.zeros_like(ng_ref)
        for h in range(GLA_HEADS):
            src = slice(h * GLA_DK, (h + 1) * GLA_DK)
            dst = slice(h * GLA_DK_PAD, h * GLA_DK_PAD + GLA_DK)
            wa_ref[LR_LANE0:LR_LANE0 + GLA_RANK, dst] = w_alpha_ref[:, src]
            ba_ref[:, dst] = b_alpha_ref[:, src]
        bf_ref[:, FG_LANE0:FG_LANE0 + FOX_HEADS] = b_forget_ref[...]
        ng_ref[:, 0:GLA_DV] = gla_g_ref[...]
        seg_ref[...] = (lax.broadcasted_iota(jnp.int32, seg_ref.shape, 0) // FOX_DH
                        == lax.broadcasted_iota(jnp.int32, seg_ref.shape, 1)).astype(BF16)
        wt_ref[...] = jnp.zeros_like(wt_ref)
        for s0, d0, width in _w_in_segments():
            for c in range(k_chunks):
                wt_ref[d0:d0 + width, c * LANES:(c + 1) * LANES] = (
                    w_in_ref[pl.ds(s0 * k_chunks + c, width, stride=k_chunks), :].astype(BF16))

    x = x_ref[...]
    xn = (x * _rms_scale(x, D_MODEL) * g_ref[...]).astype(BF16)
    nt = (((1,), (1,)), ((), ()))

    def proj(first, last):
        lo, hi = _OFF[first][0], _OFF[last][1]
        y = lax.dot_general(xn, wt_ref[lo:hi, :], nt, preferred_element_type=F32)
        return lambda name: y[:, _OFF[name][0] - lo:_OFF[name][1] - lo]

    gla = proj("gq", "gg")
    tail = proj("mq", "mg")
    gq_all = gla("gq")
    small_ref[:, FG_LANE0:FG_LANE0 + SUBLANES] = gq_all[:, FG_COL:FG_COL + SUBLANES]
    small_ref[:, LR_LANE0:LR_LANE0 + GLA_RANK] = gq_all[:, LR_COL:LR_COL + GLA_RANK]
    small = small_ref[...]
    logf = _log_sigmoid(small + bf_ref[...])
    c = logf.T[0:SUBLANES, :]
    lane = lax.broadcasted_iota(jnp.int32, c.shape, 1)
    shift = 1
    while shift < rows:
        c = c + jnp.where(lane >= shift, pltpu.roll(c, shift, axis=1), 0.0)
        shift *= 2
    c = c + carry_ref[:, 0:1]
    crow_ref[...] = c
    carry_ref[...] = jnp.broadcast_to(c[:, rows - 1:rows], carry_ref.shape)
    neg = -c
    hi = neg.astype(BF16).astype(F32)
    mid = (neg - hi).astype(BF16).astype(F32)
    low = neg - hi - mid
    parts = jnp.concatenate(
        [hi, mid, low, jnp.ones_like(c), jnp.zeros((LANES - 4 * SUBLANES, rows), F32)], axis=0)
    kx_ref[FOX_W:, :] = parts.astype(BF16)
    cq_ref[...] = parts.T.astype(BF16)

    z = (jnp.dot(small.astype(BF16), wa_ref[...].astype(BF16), preferred_element_type=F32)
         + ba_ref[...])
    loga_ref[...] = _log_sigmoid(z) * (1.0 / GLA_GATE_NORM)
    q_lane = lax.broadcasted_iota(jnp.int32, (1, GLA_QK_W), 1)
    gq_ref[...] = jnp.where(q_lane % GLA_DK_PAD < GLA_DK, gq_all, 0.0).astype(BF16)
    gk_ref[...] = gla("gk").astype(BF16)
    gv, gg = gla("gv").astype(BF16), _silu(gla("gg")).astype(BF16)
    for h in range(GLA_HEADS):
        src = slice(h * GLA_DV, (h + 1) * GLA_DV)
        dst = slice(h * GLA_DV_PAD, h * GLA_DV_PAD + GLA_DV)
        gv_ref[:, dst] = gv[:, src]
        gg_ref[:, dst] = gg[:, src]
    gla_local, gla_scan, gla_output = _gla_block(
        gq_ref, gk_ref, gv_ref, loga_ref, gg_ref, ng_ref, gla_ref,
        s_ref, lhs_ref, kv_ref, dec_ref, sprev_ref)

    def max_sq_norm(v):
        v32 = v.astype(F32)
        n2 = jnp.dot((v32 * v32).astype(BF16), seg_ref[...], preferred_element_type=F32)
        return jnp.max(n2, axis=0, keepdims=True)

    gla_local()
    fox_qk = proj("fq", "fk")
    mqg_ref[:, :MEM_W] = (tail("mq") * MEM_DH ** -0.5).astype(BF16)
    mqg_ref[:, MEM_W:] = _silu(tail("mg")).astype(BF16)
    gla_scan()
    fq = (fox_qk("fq") * FOX_DH ** -0.5).astype(BF16)
    fk = fox_qk("fk").astype(BF16)
    fqvg_ref[:, :FOX_W] = fq
    kx_ref[:FOX_W, :] = fox_qk("fk").T.astype(BF16)
    gla_output()
    fox_vg = proj("fv", "fgate")
    fqvg_ref[:, FOX_W:2 * FOX_W] = fox_vg("fv").astype(BF16)
    fqvg_ref[:, 2 * FOX_W:] = _silu(fox_vg("fgate")).astype(BF16)
    qn2_ref[0] = max_sq_norm(fq)
    kn2_ref[0] = jnp.maximum(kn2_ref[0], max_sq_norm(fk))


def _memkv(mem_ref, g_ref, w_ref, mk_ref, mv_ref):
    m = mem_ref[...]
    mn = (m * _rms_scale(m, D_MODEL) * g_ref[...]).astype(BF16)
    kv = jnp.dot(mn, w_ref[...].astype(BF16), preferred_element_type=F32)
    mk_ref[...] = kv[:, :MEM_W].astype(BF16)
    mv_ref[...] = kv[:, MEM_W:].astype(BF16)


def _gla_block(q_ref, k_ref, v_ref, loga_ref, gate_ref, ng_ref, o_ref,
               s_ref, lhs_ref, kv_ref, dec_ref, sprev_ref):
    C = GLA_CHUNK
    W = HEAD_PAIR * GLA_DV_PAD
    n_chunks = q_ref.shape[0] // C

    row = lax.broadcasted_iota(jnp.int32, (C, LANES), 0)
    lane = lax.broadcasted_iota(jnp.int32, (C, LANES), 1)
    lo_k = lane < GLA_DK_PAD
    causal = row >= jnp.where(lo_k, lane, lane - GLA_DK_PAD)
    lo_v = lax.broadcasted_iota(jnp.int32, (C, W), 1) < GLA_DV_PAD
    st_row = lax.broadcasted_iota(jnp.int32, (LANES, W), 0)
    st_lane = lax.broadcasted_iota(jnp.int32, (LANES, W), 1)
    own = (st_row < GLA_DK_PAD) == (st_lane < GLA_DV_PAD)
    eye = (lax.broadcasted_iota(jnp.int32, (LANES, LANES), 0)
           == lax.broadcasted_iota(jnp.int32, (LANES, LANES), 1))
    scale = GLA_DK ** -0.5
    nt = (((1,), (1,)), ((), ()))
    tn = (((0,), (0,)), ((), ()))
    ng = jnp.concatenate([ng_ref[...]] * HEAD_PAIR, axis=1)

    pairs = range(GLA_HEADS // HEAD_PAIR)

    def local(ci):
        rs = slice(ci * C, (ci + 1) * C)
        for p in pairs:
            ls = slice(p * LANES, (p + 1) * LANES)
            vs = slice(p * W, (p + 1) * W)
            b = loga_ref[rs, ls]
            shift = 1
            while shift < C:
                b = b + jnp.where(row >= shift, pltpu.roll(b, shift, axis=0), 0.0)
                shift *= 2
            b_last = b[C - 1:C, :]
            k2 = k_ref[rs, ls].astype(F32)
            qd = (q_ref[rs, ls].astype(F32) * scale * jnp.exp(b)).astype(BF16)
            kd = (k2 * jnp.exp(-b)).astype(BF16)
            ke = (k2 * jnp.exp(b_last - b)).astype(BF16)
            zk = jnp.zeros_like(kd)
            kd_blk = jnp.concatenate([jnp.where(lo_k, kd, zk), jnp.where(lo_k, zk, kd)], axis=0)
            attn = lax.dot_general(qd, kd_blk, nt, preferred_element_type=F32)
            lhs_ref[rs, vs] = jnp.concatenate([jnp.where(causal, attn, 0.0).astype(BF16), qd], axis=1)
            kv = lax.dot_general(ke, v_ref[rs, vs], tn, preferred_element_type=F32)
            kv_ref[p, ci] = jnp.where(own, kv, 0.0)
            dcol = jnp.exp(jnp.sum(jnp.where(eye, jnp.broadcast_to(b_last, (LANES, LANES)), 0.0),
                                   axis=1, keepdims=True))
            dec_ref[p, ci] = jnp.broadcast_to(dcol, (LANES, LANES))

    def scan(ci):
        for p in pairs:
            s_prev = s_ref[p]
            sprev_ref[p, ci] = s_prev.astype(BF16)
            s_ref[p] = jnp.tile(dec_ref[p, ci], (1, HEAD_PAIR)) * s_prev + kv_ref[p, ci]

    def output(ci):
        rs = slice(ci * C, (ci + 1) * C)
        for p in pairs:
            vs = slice(p * W, (p + 1) * W)
            v2 = v_ref[rs, vs]
            zv = jnp.zeros_like(v2)
            v_blk = jnp.concatenate([jnp.where(lo_v, v2, zv), jnp.where(lo_v, zv, v2)], axis=0)
            o = jnp.dot(lhs_ref[rs, vs], jnp.concatenate([v_blk, sprev_ref[p, ci]], axis=0),
                        preferred_element_type=F32)
            o2 = o * o
            ms = jnp.where(lo_v, jnp.sum(o2[:, :GLA_DV_PAD], axis=1, keepdims=True),
                           jnp.sum(o2[:, GLA_DV_PAD:], axis=1, keepdims=True))
            on = o * lax.rsqrt(ms * (1.0 / GLA_DV) + EPS) * ng
            og = (on * gate_ref[rs, vs].astype(F32)).astype(BF16)
            for hh in range(HEAD_PAIR):
                c0 = (p * HEAD_PAIR + hh) * GLA_DV
                o_ref[rs, c0:c0 + GLA_DV] = og[:, hh * GLA_DV_PAD:hh * GLA_DV_PAD + GLA_DV]

    def all_chunks(phase):
        return lambda: [phase(ci) for ci in range(n_chunks)]

    return all_chunks(local), all_chunks(scan), all_chunks(output)


def _fox_kernel(qn2_ref, kn2_ref, cend_ref, q_ref, cq_ref, k_ref, cx_ref, v_ref, gate_ref, o_ref,
                m_ref, acc_ref):
    blk = FOX_KEYS
    streams = range(q_ref.shape[0] // blk)
    pair = pl.program_id(0)
    qi = pl.program_id(1)
    nblk = pl.num_programs(1) * len(streams)
    lane = lax.broadcasted_iota(jnp.int32, (1, LANES), 1)
    lo_lanes = lane < FOX_DH
    reps = blk // LANES
    diag = [qi * len(streams) + s for s in streams]
    heads = [pair * HEAD_PAIR + hh for hh in range(HEAD_PAIR)]

    q = q_ref[...]
    zero = jnp.zeros_like(q)
    q_lo, q_hi = jnp.where(lo_lanes, q, zero), jnp.where(lo_lanes, zero, q)
    q_stack = [jnp.concatenate([q_lo[s * blk:(s + 1) * blk], q_hi[s * blk:(s + 1) * blk]], axis=0)
               for s in streams]
    g_row = lax.broadcasted_iota(jnp.int32, (CX_ROWS, CX_ROWS), 0)
    g_col = lax.broadcasted_iota(jnp.int32, (CX_ROWS, CX_ROWS), 1)
    xlane = lax.broadcasted_iota(jnp.int32, (HEAD_PAIR * blk, CX_ROWS), 1)
    cq = cq_ref[:, 0:CX_ROWS]

    def lane_map(h, shift):
        g = jnp.where((g_row == CX_ONE) & ((g_col == CX_HI + h) | (g_col == CX_MID + h)
                                           | (g_col == CX_LO + h)), 1.0, 0.0)
        if shift:
            for part, base in enumerate((CX_HI, CX_MID, CX_LO)):
                g = jnp.where((g_row == base + h) & (g_col == CX_ONE + part), -1.0, g)
        return g.astype(BF16)

    def q_aug(s, shift, void=None):
        cs = cq[s * blk:(s + 1) * blk]
        extra = jnp.concatenate([jnp.dot(cs, lane_map(h, shift), preferred_element_type=F32)
                                 for h in heads], axis=0)
        if void is not None:
            extra = jnp.where(jnp.logical_and(xlane == CX_ONE + 3, void), NEG_BIG, extra)
        return jnp.concatenate([q_stack[s], extra.astype(BF16)], axis=1)

    acc_ref[...] = jnp.zeros_like(acc_ref)
    qpos = lax.broadcasted_iota(jnp.int32, (blk, blk), 0)
    kpos = lax.broadcasted_iota(jnp.int32, (blk, blk), 1)

    def block(j):
        ks = pl.ds(pl.multiple_of(jnp.maximum(j, 0) * blk, blk), blk)
        k_aug = jnp.concatenate([k_ref[:, ks], cx_ref[0:CX_ROWS, ks]], axis=0)
        vb = v_ref[ks, :]
        one = jnp.ones_like(vb)
        return k_aug, (jnp.where(lo_lanes, vb, one), jnp.where(lo_lanes, one, vb))

    def step_online(s, j, qa, masked):
        rows = slice(s * blk, (s + 1) * blk)
        k_aug, vaug = block(j)
        s_all = jnp.dot(qa, k_aug, preferred_element_type=F32)
        for hh in range(HEAD_PAIR):
            sc = s_all[hh * blk:(hh + 1) * blk]
            if masked:
                sc = jnp.where(kpos <= qpos, sc, NEG_BIG)
            m_prev = m_ref[hh, rows]
            m_new = jnp.maximum(m_prev, jnp.max(sc, axis=1, keepdims=True))
            p = jnp.exp(sc - jnp.tile(m_new, (1, reps)))
            alpha = jnp.exp(m_prev - m_new)
            pv = jnp.dot(p.astype(BF16), vaug[hh], preferred_element_type=F32)
            acc_ref[hh, rows] = alpha * acc_ref[hh, rows] + pv
            m_ref[hh, rows] = m_new

    def step_direct(s, j, qa, masked, only=None):
        rows = slice(s * blk, (s + 1) * blk)
        k_aug, vaug = block(j)
        hsel = range(HEAD_PAIR) if only is None else (only,)
        lhs = qa if only is None else qa[only * blk:(only + 1) * blk]
        s_all = jnp.dot(lhs, k_aug, preferred_element_type=F32)
        for n, hh in enumerate(hsel):
            sc = s_all[n * blk:(n + 1) * blk]
            if masked:
                sc = jnp.where(kpos <= qpos, sc, NEG_BIG)
            acc_ref[hh, rows] += jnp.dot(jnp.exp(sc).astype(BF16), vaug[hh],
                                         preferred_element_type=F32)

    k_max2 = [kn2_ref[h] for h in heads]

    def norm2(s, hh):
        stat = (diag[s] * blk // PROJ_ROWS) * LANES
        return (NORM_SLACK * NORM_SLACK) * qn2_ref[stat + heads[hh]] * k_max2[hh]

    n2 = [[norm2(s, hh) for hh in range(HEAD_PAIR)] for s in streams]
    gap0 = [[FOX_SKIP_NATS + cend_ref[heads[hh] * nblk + jnp.maximum(diag[s] - 1, 0)]
             for hh in range(HEAD_PAIR)] for s in streams]

    def live_head(t, hh, direct):
        keep = False
        for s in streams:
            j = diag[s] - 1 - t
            gap = gap0[s][hh] - cend_ref[heads[hh] * nblk + jnp.maximum(j, 0)]
            bound2 = n2[s][hh] if direct else 4.0 * n2[s][hh]
            dead = jnp.logical_and(gap <= 0.0, bound2 <= gap * gap)
            keep = jnp.logical_or(keep, jnp.logical_and(j >= 0, jnp.logical_not(dead)))
        return keep

    def sweeps(stepper, direct):
        def sweep(back, masked=False, only=None):
            for s in streams:
                stepper(s, diag[s] - back, masked, only)

        def loop(t0, cond, only=None):
            def body(t):
                sweep(t + 1, only=only)
                return t + 1
            return lax.while_loop(cond, body, t0)

        def live(t, hh):
            return live_head(t, hh, direct)

        sweep(0, masked=True)
        sweep(1)
        if direct:
            t_both = loop(1, lambda t: jnp.logical_and(live(t, 0), live(t, 1)))
            for hh in range(HEAD_PAIR):
                loop(t_both, lambda t, hh=hh: live(t, hh), only=hh)
        else:
            loop(1, lambda t: jnp.logical_or(live(t, 0), live(t, 1)))

    direct_ok = True
    for s in streams:
        for hh in range(HEAD_PAIR):
            direct_ok = jnp.logical_and(direct_ok, n2[s][hh] <= FOX_DIRECT_NORM2)

    @pl.when(direct_ok)
    def _():
        q_dir = [q_aug(s, shift=True) for s in streams]
        q_void = [q_aug(s, shift=True, void=True) for s in streams]
        sweeps(lambda s, j, masked, only: step_direct(
            s, j, q_dir[s] if masked else jnp.where(j >= 0, q_dir[s], q_void[s]), masked, only),
            direct=True)

    @pl.when(jnp.logical_not(direct_ok))
    def _():
        m_ref[...] = jnp.full_like(m_ref, NEG_BIG)
        sweeps(lambda s, j, masked, only: step_online(
            s, j, q_aug(s, shift=False, void=j < 0), masked), direct=False)

    outs = []
    for hh in range(HEAD_PAIR):
        acc = acc_ref[hh]
        outs.append(acc / pltpu.roll(acc, FOX_DH, axis=1))
    o = jnp.where(lo_lanes, outs[0], outs[1])
    o_ref[...] = (o * gate_ref[...].astype(F32)).astype(BF16)


def _out_kernel(x_hbm, gla_ref, fox_ref, mq_ref, mg_ref, mem_ref, mem_g_ref, w_mem_ref,
                w_out_ref, fg_ref, o_ref, wo_ref, mk_ref, mv_ref, x_buf, x_sem):
    step = pl.program_id(0)
    n_steps = pl.num_programs(0)
    rows = o_ref.shape[0]

    chunk = rows // X_CHUNKS

    def x_copies(s):
        slot = lax.rem(s, X_SLOTS)
        return [pltpu.make_async_copy(
            x_hbm.at[pl.ds(pl.multiple_of(s * rows + c * chunk, chunk), chunk)],
            x_buf.at[slot, pl.ds(c * chunk, chunk)], x_sem.at[slot, c]) for c in range(X_CHUNKS)]

    @pl.when(step == 0)
    def _():
        for s in range(X_SLOTS - 1):
            for cp in x_copies(s):
                cp.start()
        wo_ref[...] = w_out_ref[...].astype(BF16)
        _memkv(mem_ref, mem_g_ref, w_mem_ref, mk_ref, mv_ref)

    @pl.when(step + (X_SLOTS - 1) < n_steps)
    def _():
        for cp in x_copies(step + (X_SLOTS - 1)):
            cp.start()

    for cp in x_copies(step):
        cp.wait()

    lane = lax.broadcasted_iota(jnp.int32, (1, LANES), 1)
    lo_lanes = lane < MEM_DH
    nt = (((1,), (1,)), ((), ()))
    mem_parts = []
    for p in range(MEM_HEADS // HEAD_PAIR):
        ls = slice(p * LANES, (p + 1) * LANES)
        q = mq_ref[:, ls]
        kb = mk_ref[:, ls]
        vb = mv_ref[:, ls]
        zero = jnp.zeros_like(q)
        one = jnp.ones_like(vb)
        qh = (jnp.where(lo_lanes, q, zero), jnp.where(lo_lanes, zero, q))
        vaug = (jnp.where(lo_lanes, vb, one), jnp.where(lo_lanes, one, vb))
        outs = []
        for hh in range(HEAD_PAIR):
            s = lax.dot_general(qh[hh], kb, nt, preferred_element_type=F32)
            pexp = jnp.exp(s - jnp.max(s, axis=1, keepdims=True))
            pv = jnp.dot(pexp.astype(BF16), vaug[hh], preferred_element_type=F32)
            outs.append(pv / pltpu.roll(pv, MEM_DH, axis=1))
        o = jnp.where(lo_lanes, outs[0], outs[1])
        mem_parts.append((o * mg_ref[:, ls].astype(F32)).astype(BF16))
    mixed = jnp.concatenate([gla_ref[...], fox_ref[...]] + mem_parts, axis=1)
    y = x_buf[lax.rem(step, X_SLOTS)] + jnp.dot(mixed, wo_ref[...], preferred_element_type=F32)
    o_ref[...] = y * _rms_scale(y, D_MODEL) * fg_ref[...]


def _params(*sem):
    return pltpu.CompilerParams(dimension_semantics=sem, vmem_limit_bytes=VMEM_LIMIT)


def _layer(x, mem, norm_g, w_in, w_alpha_up, b_alpha, b_forget, gla_norm_g,
           mem_norm_g, w_mem_kv, w_out, out_g):
    w_in_t = jnp.transpose(w_in[None], (0, 2, 1)).reshape(-1, LANES)
    T = x.shape[0]
    M = mem.shape[0]

    def rows(width, n=PROJ_ROWS):
        return pl.BlockSpec((n, width), lambda i: (i, 0))

    def whole(shape):
        return pl.BlockSpec(shape, lambda i: (0,) * len(shape))

    def once(shape):
        return pl.BlockSpec(shape, lambda i: (0,) * len(shape), pipeline_mode=pl.Buffered(1))

    bshape = lambda w: jax.ShapeDtypeStruct((T, w), BF16)
    nproj = T // PROJ_ROWS
    stat_spec = pl.BlockSpec((1, 1, LANES), lambda i: (i, 0, 0))
    stat_shape = jax.ShapeDtypeStruct((nproj, 1, LANES), F32)
    gla_pairs, gla_chunks = GLA_HEADS // HEAD_PAIR, PROJ_ROWS // GLA_CHUNK
    pair_w = HEAD_PAIR * GLA_DV_PAD
    (gla, fqvg, kx, mqg, crow, cq, qn2, kn2) = pl.pallas_call(
        _proj_kernel,
        grid=(nproj,),
        in_specs=[rows(D_MODEL), whole((1, D_MODEL)),
                  pl.BlockSpec(w_in_t.shape, lambda i: (0, 0), pipeline_mode=pl.Buffered(1)),
                  whole(w_alpha_up.shape), whole((1, GLA_HEADS * GLA_DK)),
                  whole((1, FOX_HEADS)), whole((1, GLA_DV))],
        out_specs=[rows(GLA_OUT_W), rows(3 * FOX_W),
                   pl.BlockSpec((FOX_W + LANES, PROJ_ROWS), lambda i: (0, i)),
                   rows(2 * MEM_W),
                   pl.BlockSpec((SUBLANES, PROJ_ROWS), lambda i: (0, i)), rows(LANES),
                   stat_spec, pl.BlockSpec((1, 1, LANES), lambda i: (0, 0, 0))],
        out_shape=[bshape(GLA_OUT_W), bshape(3 * FOX_W),
                   jax.ShapeDtypeStruct((FOX_W + LANES, T), BF16),
                   bshape(2 * MEM_W),
                   jax.ShapeDtypeStruct((SUBLANES, T), F32), bshape(LANES),
                   stat_shape, jax.ShapeDtypeStruct((1, 1, LANES), F32)],
        scratch_shapes=[
            pltpu.VMEM((SUBLANES, LANES), F32),
            pltpu.VMEM((IN_COLS_PAD, D_MODEL), BF16),
            pltpu.VMEM((SMALL_W, GLA_QK_W), F32),
            pltpu.VMEM((1, GLA_QK_W), F32),
            pltpu.VMEM((1, SMALL_W), F32),
            pltpu.VMEM((1, GLA_DV_PAD), F32),
            pltpu.VMEM((FOX_W, LANES), BF16),
            pltpu.VMEM((PROJ_ROWS, SMALL_W), F32),
            pltpu.VMEM((PROJ_ROWS, GLA_QK_W), BF16),
            pltpu.VMEM((PROJ_ROWS, GLA_QK_W), BF16),
            pltpu.VMEM((PROJ_ROWS, GLA_V_W), BF16),
            pltpu.VMEM((PROJ_ROWS, GLA_V_W), BF16),
            pltpu.VMEM((PROJ_ROWS, GLA_QK_W), F32),
            pltpu.VMEM((gla_pairs, LANES, pair_w), F32),
            pltpu.VMEM((PROJ_ROWS, GLA_V_W), BF16),
            pltpu.VMEM((gla_pairs, gla_chunks, LANES, pair_w), F32),
            pltpu.VMEM((gla_pairs, gla_chunks, LANES, LANES), F32),
            pltpu.VMEM((gla_pairs, gla_chunks, LANES, pair_w), BF16)],
        compiler_params=_params("arbitrary"),
        name="proj",
    )(x, norm_g[None, :], w_in_t, w_alpha_up, b_alpha[None, :], b_forget[None, :],
      gla_norm_g[None, :])

    cend = crow[:FOX_HEADS, FOX_KEYS - 1::FOX_KEYS].reshape(-1)
    fox_pairs = FOX_HEADS // HEAD_PAIR
    pair_rows = pl.BlockSpec((FOX_BLOCK, LANES), lambda p, i, *_: (i, p))
    fox = pl.pallas_call(
        _fox_kernel,
        grid_spec=pltpu.PrefetchScalarGridSpec(
            num_scalar_prefetch=3,
            grid=(fox_pairs, T // FOX_BLOCK),
            in_specs=[pair_rows,
                      pl.BlockSpec((FOX_BLOCK, LANES), lambda p, i, *_: (i, 0)),
                      pl.BlockSpec((LANES, T), lambda p, i, *_: (p, 0)),
                      pl.BlockSpec((LANES, T), lambda p, i, *_: (fox_pairs, 0)),
                      pl.BlockSpec((T, LANES), lambda p, i, *_: (0, fox_pairs + p)),
                      pl.BlockSpec((FOX_BLOCK, LANES),
                                   lambda p, i, *_: (i, 2 * fox_pairs + p))],
            out_specs=pair_rows,
            scratch_shapes=[pltpu.VMEM((HEAD_PAIR, FOX_BLOCK, LANES), F32),
                            pltpu.VMEM((HEAD_PAIR, FOX_BLOCK, LANES), F32)]),
        out_shape=bshape(FOX_W),
        compiler_params=_params("arbitrary", "arbitrary"),
        name="fox",
    )(qn2.reshape(-1), kn2.reshape(-1), cend, fqvg, cq, kx, kx, fqvg, fqvg)

    out = pl.pallas_call(
        _out_kernel,
        grid=(T // OUT_ROWS,),
        in_specs=[pl.BlockSpec(memory_space=pl.ANY),
                  rows(GLA_OUT_W, OUT_ROWS), rows(FOX_W, OUT_ROWS),
                  pl.BlockSpec((OUT_ROWS, MEM_W), lambda i: (i, 0)),
                  pl.BlockSpec((OUT_ROWS, MEM_W), lambda i: (i, 1)),
                  once(mem.shape), once((1, D_MODEL)), once(w_mem_kv.shape), once(w_out.shape),
                  whole((1, D_MODEL))],
        out_specs=rows(D_MODEL, OUT_ROWS),
        out_shape=jax.ShapeDtypeStruct((T, D_MODEL), F32),
        scratch_shapes=[pltpu.VMEM((D_MODEL, D_MODEL), BF16),
                        pltpu.VMEM((M, MEM_W), BF16), pltpu.VMEM((M, MEM_W), BF16),
                        pltpu.VMEM((X_SLOTS, OUT_ROWS, D_MODEL), F32),
                        pltpu.SemaphoreType.DMA((X_SLOTS, X_CHUNKS))],
        compiler_params=_params("arbitrary"),
        name="out",
    )(x, gla, fox, mqg, mqg, mem, mem_norm_g[None, :], w_mem_kv, w_out, out_g[None, :])
    return out


def kernel(x, mem, norm_g, w_in, w_alpha_up, b_alpha, b_forget, gla_norm_g, mem_norm_g,
           w_mem_kv, w_out, final_norm_g):
    assert x.shape[0] == 1 and mem.shape[0] == 1 and norm_g.shape[0] == 1
    assert x.shape[1] % max(PROJ_ROWS, FOX_BLOCK, OUT_ROWS) == 0
    out = _layer(x[0], mem[0], norm_g[0], w_in[0], w_alpha_up[0], b_alpha[0], b_forget[0],
                 gla_norm_g[0], mem_norm_g[0], w_mem_kv[0], w_out[0], final_norm_g)
    return out[None]
```

```python
import jax
import jax.numpy as jnp
from jax import lax
from jax.experimental import pallas as pl
from jax.experimental.pallas import tpu as pltpu

F32 = jnp.float32
BF16 = jnp.bfloat16

EPS = 1e-6
LANES = 128
SUBLANES = 8

D_MODEL = 1024
GLA_HEADS, GLA_DK, GLA_DV, GLA_RANK = 4, 48, 96, 16
GLA_DK_PAD = 64
GLA_DV_PAD = LANES
GLA_GATE_NORM = 16.0
GLA_CHUNK = 64
FOX_HEADS, FOX_DH = 6, 64
MEM_HEADS, MEM_DH = 4, 64
HEAD_PAIR = 2
GLA_QK_W = GLA_HEADS * GLA_DK_PAD
GLA_V_W = GLA_HEADS * GLA_DV_PAD
GLA_OUT_W = GLA_HEADS * GLA_DV
FOX_W = FOX_HEADS * FOX_DH
MEM_W = MEM_HEADS * MEM_DH
SMALL_W = LANES
FG_LANE0 = 0
LR_LANE0 = SUBLANES

_GROUPS = (("gq", GLA_QK_W), ("gk", GLA_QK_W), ("gv", GLA_OUT_W), ("gg", GLA_OUT_W),
           ("fq", FOX_W), ("fk", FOX_W), ("fv", FOX_W), ("fgate", FOX_W),
           ("mq", MEM_W), ("mg", MEM_W))
FG_COL = GLA_DK
LR_COL = GLA_DK_PAD + GLA_DK
_OFF = {}
_o = 0
for _n, _w in _GROUPS:
    _OFF[_n] = (_o, _o + _w)
    _o += _w
IN_COLS_PAD = _o

PROJ_ROWS = 1024
FOX_BLOCK = 4096
FOX_KEYS = 256
OUT_ROWS = 1024
W_CHUNKS = 6
X_SLOTS = 3
X_CHUNKS = 8
VMEM_LIMIT = 56 * 1024 * 1024

NEG_BIG = -1e30
FOX_SKIP_NATS = 105.0
NORM_SLACK = 1.02
FOX_DIRECT_NORM2 = 3600.0
CX_HI, CX_MID, CX_LO, CX_ONE = 0, 8, 16, 24
CX_ROWS = 32


def _log_sigmoid(z):
    return jnp.minimum(z, 0.0) - jnp.log(1.0 + jnp.exp(-jnp.abs(z)))


def _silu(z):
    return z / (1.0 + jnp.exp(-z))


def _rms_scale(v, width):
    return lax.rsqrt(jnp.sum(v * v, axis=-1, keepdims=True) * (1.0 / width) + EPS)


def _w_in_segments():
    qk, gw = GLA_HEADS * GLA_DK, GLA_HEADS * GLA_DV
    src = {}
    o = 0
    for name, width in (("gq", qk), ("gk", qk), ("gv", gw), ("lr", GLA_RANK), ("gg", gw),
                        ("fq", FOX_W), ("fk", FOX_W), ("fv", FOX_W), ("fg", FOX_HEADS),
                        ("fgate", FOX_W), ("mq", MEM_W), ("mg", MEM_W)):
        src[name] = o
        o += width
    segs = []
    for name, d, d_pad in (("gq", GLA_DK, GLA_DK_PAD), ("gk", GLA_DK, GLA_DK_PAD)):
        segs += [(src[name] + h * d, _OFF[name][0] + h * d_pad, d) for h in range(GLA_HEADS)]
    segs += [(src[name], _OFF[name][0], _OFF[name][1] - _OFF[name][0])
             for name in ("gv", "gg", "fq", "fk", "fv", "fgate", "mq", "mg")]
    segs += [(src["fg"], _OFF["gq"][0] + FG_COL, FOX_HEADS),
             (src["lr"], _OFF["gq"][0] + LR_COL, GLA_RANK)]
    return tuple(segs)


def _proj_kernel(x_ref, g_ref, w_in_hbm, w_alpha_ref, b_alpha_ref, b_forget_ref, gla_g_ref,
                 gla_ref, fqvg_ref, kx_ref, mqg_ref, crow_ref, cq_ref, qn2_ref, kn2_ref,
                 carry_ref, wt_ref, wa_ref, ba_ref, bf_ref, ng_ref, seg_ref, small_ref,
                 gq_ref, gk_ref, gv_ref, gg_ref, loga_ref,
                 s_ref, lhs_ref, kv_ref, dec_ref, sprev_ref, w_in_ref, w_sem):
    rows = x_ref.shape[0]
    k_chunks = D_MODEL // LANES

    in_cols = w_in_hbm.shape[0] // k_chunks
    assert in_cols % W_CHUNKS == 0
    chunk_cols = in_cols // W_CHUNKS
    chunk_rows = chunk_cols * k_chunks

    def w_copy(k):
        part = pl.ds(k * chunk_rows, chunk_rows)
        return pltpu.make_async_copy(w_in_hbm.at[part], w_in_ref.at[part], w_sem.at[k])

    @pl.when(pl.program_id(0) == 0)
    def _():
        for k in range(W_CHUNKS):
            w_copy(k).start()
        carry_ref[...] = jnp.zeros_like(carry_ref)
        s_ref[...] = jnp.zeros_like(s_ref)
        kn2_ref[...] = jnp.zeros_like(kn2_ref)
        gv_ref[...] = jnp.zeros_like(gv_ref)
        gg_ref[...] = jnp.zeros_like(gg_ref)
        small_ref[...] = jnp.zeros_like(small_ref)
        wa_ref[...] = jnp.zeros_like(wa_ref)
        ba_ref[...] = jnp.zeros_like(ba_ref)
        bf_ref[...] = jnp.zeros_like(bf_ref)
        ng_ref[...] = jnp.zeros_like(ng_ref)
        for h in range(GLA_HEADS):
            src = slice(h * GLA_DK, (h + 1) * GLA_DK)
            dst = slice(h * GLA_DK_PAD, h * GLA_DK_PAD + GLA_DK)
            wa_ref[LR_LANE0:LR_LANE0 + GLA_RANK, dst] = w_alpha_ref[:, src]
            ba_ref[:, dst] = b_alpha_ref[:, src]
        bf_ref[:, FG_LANE0:FG_LANE0 + FOX_HEADS] = b_forget_ref[...]
        ng_ref[:, 0:GLA_DV] = gla_g_ref[...]
        seg_ref[...] = (lax.broadcasted_iota(jnp.int32, seg_ref.shape, 0) // FOX_DH
                        == lax.broadcasted_iota(jnp.int32, seg_ref.shape, 1)).astype(BF16)
        wt_ref[...] = jnp.zeros_like(wt_ref)
        arrived = set()
        for s0, d0, width in _w_in_segments():
            for k in range(s0 // chunk_cols, (s0 + width - 1) // chunk_cols + 1):
                if k not in arrived:
                    w_copy(k).wait()
                    arrived.add(k)
            for c in range(k_chunks):
                wt_ref[d0:d0 + width, c * LANES:(c + 1) * LANES] = (
                    w_in_ref[pl.ds(s0 * k_chunks + c, width, stride=k_chunks), :].astype(BF16))
        assert len(arrived) == W_CHUNKS

    x = x_ref[...]
    xn = (x * _rms_scale(x, D_MODEL) * g_ref[...]).astype(BF16)
    nt = (((1,), (1,)), ((), ()))

    def proj(first, last):
        lo, hi = _OFF[first][0], _OFF[last][1]
        y = lax.dot_general(xn, wt_ref[lo:hi, :], nt, preferred_element_type=F32)
        return lambda name: y[:, _OFF[name][0] - lo:_OFF[name][1] - lo]

    gla = proj("gq", "gg")
    tail = proj("mq", "mg")
    gq_all = gla("gq")
    small_ref[:, FG_LANE0:FG_LANE0 + SUBLANES] = gq_all[:, FG_COL:FG_COL + SUBLANES]
    small_ref[:, LR_LANE0:LR_LANE0 + GLA_RANK] = gq_all[:, LR_COL:LR_COL + GLA_RANK]
    small = small_ref[...]
    logf = _log_sigmoid(small + bf_ref[...])
    c = logf.T[0:SUBLANES, :]
    lane = lax.broadcasted_iota(jnp.int32, c.shape, 1)
    shift = 1
    while shift < rows:
        c = c + jnp.where(lane >= shift, pltpu.roll(c, shift, axis=1), 0.0)
        shift *= 2
    c = c + carry_ref[:, 0:1]
    crow_ref[...] = c
    carry_ref[...] = jnp.broadcast_to(c[:, rows - 1:rows], carry_ref.shape)
    neg = -c
    hi = neg.astype(BF16).astype(F32)
    mid = (neg - hi).astype(BF16).astype(F32)
    low = neg - hi - mid
    parts = jnp.concatenate(
        [hi, mid, low, jnp.ones_like(c), jnp.zeros((LANES - 4 * SUBLANES, rows), F32)], axis=0)
    kx_ref[FOX_W:, :] = parts.astype(BF16)
    cq_ref[...] = parts.T.astype(BF16)

    z = (jnp.dot(small.astype(BF16), wa_ref[...].astype(BF16), preferred_element_type=F32)
         + ba_ref[...])
    loga_ref[...] = _log_sigmoid(z) * (1.0 / GLA_GATE_NORM)
    q_lane = lax.broadcasted_iota(jnp.int32, (1, GLA_QK_W), 1)
    gq_ref[...] = jnp.where(q_lane % GLA_DK_PAD < GLA_DK, gq_all, 0.0).astype(BF16)
    gk_ref[...] = gla("gk").astype(BF16)
    gv, gg = gla("gv").astype(BF16), _silu(gla("gg")).astype(BF16)
    for h in range(GLA_HEADS):
        src = slice(h * GLA_DV, (h + 1) * GLA_DV)
        dst = slice(h * GLA_DV_PAD, h * GLA_DV_PAD + GLA_DV)
        gv_ref[:, dst] = gv[:, src]
        gg_ref[:, dst] = gg[:, src]
    gla_local, gla_scan, gla_output = _gla_block(
        gq_ref, gk_ref, gv_ref, loga_ref, gg_ref, ng_ref, gla_ref,
        s_ref, lhs_ref, kv_ref, dec_ref, sprev_ref)

    def max_sq_norm(v):
        v32 = v.astype(F32)
        n2 = jnp.dot((v32 * v32).astype(BF16), seg_ref[...], preferred_element_type=F32)
        return jnp.max(n2, axis=0, keepdims=True)

    gla_local()
    fox_qk = proj("fq", "fk")
    mqg_ref[:, :MEM_W] = (tail("mq") * MEM_DH ** -0.5).astype(BF16)
    mqg_ref[:, MEM_W:] = _silu(tail("mg")).astype(BF16)
    gla_scan()
    fq = (fox_qk("fq") * FOX_DH ** -0.5).astype(BF16)
    fk = fox_qk("fk").astype(BF16)
    fqvg_ref[:, :FOX_W] = fq
    kx_ref[:FOX_W, :] = fox_qk("fk").T.astype(BF16)
    gla_output()
    fox_vg = proj("fv", "fgate")
    fqvg_ref[:, FOX_W:2 * FOX_W] = fox_vg("fv").astype(BF16)
    fqvg_ref[:, 2 * FOX_W:] = _silu(fox_vg("fgate")).astype(BF16)
    qn2_ref[0] = max_sq_norm(fq)
    kn2_ref[0] = jnp.maximum(kn2_ref[0], max_sq_norm(fk))


def _memkv(mem_ref, g_ref, w_ref, mk_ref, mv_ref):
    m = mem_ref[...]
    mn = (m * _rms_scale(m, D_MODEL) * g_ref[...]).astype(BF16)
    kv = jnp.dot(mn, w_ref[...].astype(BF16), preferred_element_type=F32)
    mk_ref[...] = kv[:, :MEM_W].astype(BF16)
    mv_ref[...] = kv[:, MEM_W:].astype(BF16)


def _gla_block(q_ref, k_ref, v_ref, loga_ref, gate_ref, ng_ref, o_ref,
               s_ref, lhs_ref, kv_ref, dec_ref, sprev_ref):
    C = GLA_CHUNK
    W = HEAD_PAIR * GLA_DV_PAD
    n_chunks = q_ref.shape[0] // C

    row = lax.broadcasted_iota(jnp.int32, (C, LANES), 0)
    lane = lax.broadcasted_iota(jnp.int32, (C, LANES), 1)
    lo_k = lane < GLA_DK_PAD
    causal = row >= jnp.where(lo_k, lane, lane - GLA_DK_PAD)
    lo_v = lax.broadcasted_iota(jnp.int32, (C, W), 1) < GLA_DV_PAD
    st_row = lax.broadcasted_iota(jnp.int32, (LANES, W), 0)
    st_lane = lax.broadcasted_iota(jnp.int32, (LANES, W), 1)
    own = (st_row < GLA_DK_PAD) == (st_lane < GLA_DV_PAD)
    eye = (lax.broadcasted_iota(jnp.int32, (LANES, LANES), 0)
           == lax.broadcasted_iota(jnp.int32, (LANES, LANES), 1))
    scale = GLA_DK ** -0.5
    nt = (((1,), (1,)), ((), ()))
    tn = (((0,), (0,)), ((), ()))
    ng = jnp.concatenate([ng_ref[...]] * HEAD_PAIR, axis=1)

    pairs = range(GLA_HEADS // HEAD_PAIR)

    def local(ci):
        rs = slice(ci * C, (ci + 1) * C)
        for p in pairs:
            ls = slice(p * LANES, (p + 1) * LANES)
            vs = slice(p * W, (p + 1) * W)
            b = loga_ref[rs, ls]
            shift = 1
            while shift < C:
                b = b + jnp.where(row >= shift, pltpu.roll(b, shift, axis=0), 0.0)
                shift *= 2
            b_last = b[C - 1:C, :]
            k2 = k_ref[rs, ls].astype(F32)
            qd = (q_ref[rs, ls].astype(F32) * scale * jnp.exp(b)).astype(BF16)
            kd = (k2 * jnp.exp(-b)).astype(BF16)
            ke = (k2 * jnp.exp(b_last - b)).astype(BF16)
            zk = jnp.zeros_like(kd)
            kd_blk = jnp.concatenate([jnp.where(lo_k, kd, zk), jnp.where(lo_k, zk, kd)], axis=0)
            attn = lax.dot_general(qd, kd_blk, nt, preferred_element_type=F32)
            lhs_ref[rs, vs] = jnp.concatenate([jnp.where(causal, attn, 0.0).astype(BF16), qd], axis=1)
            kv = lax.dot_general(ke, v_ref[rs, vs], tn, preferred_element_type=F32)
            kv_ref[p, ci] = jnp.where(own, kv, 0.0)
            dcol = jnp.exp(jnp.sum(jnp.where(eye, jnp.broadcast_to(b_last, (LANES, LANES)), 0.0),
                                   axis=1, keepdims=True))
            dec_ref[p, ci] = jnp.broadcast_to(dcol, (LANES, LANES))

    def scan(ci):
        for p in pairs:
            s_prev = s_ref[p]
            sprev_ref[p, ci] = s_prev.astype(BF16)
            s_ref[p] = jnp.tile(dec_ref[p, ci], (1, HEAD_PAIR)) * s_prev + kv_ref[p, ci]

    def output(ci):
        rs = slice(ci * C, (ci + 1) * C)
        for p in pairs:
            vs = slice(p * W, (p + 1) * W)
            v2 = v_ref[rs, vs]
            zv = jnp.zeros_like(v2)
            v_blk = jnp.concatenate([jnp.where(lo_v, v2, zv), jnp.where(lo_v, zv, v2)], axis=0)
            o = jnp.dot(lhs_ref[rs, vs], jnp.concatenate([v_blk, sprev_ref[p, ci]], axis=0),
                        preferred_element_type=F32)
            o2 = o * o
            ms = jnp.where(lo_v, jnp.sum(o2[:, :GLA_DV_PAD], axis=1, keepdims=True),
                           jnp.sum(o2[:, GLA_DV_PAD:], axis=1, keepdims=True))
            on = o * lax.rsqrt(ms * (1.0 / GLA_DV) + EPS) * ng
            og = (on * gate_ref[rs, vs].astype(F32)).astype(BF16)
            for hh in range(HEAD_PAIR):
                c0 = (p * HEAD_PAIR + hh) * GLA_DV
                o_ref[rs, c0:c0 + GLA_DV] = og[:, hh * GLA_DV_PAD:hh * GLA_DV_PAD + GLA_DV]

    def all_chunks(phase):
        return lambda: [phase(ci) for ci in range(n_chunks)]

    return all_chunks(local), all_chunks(scan), all_chunks(output)


def _fox_kernel(qn2_ref, kn2_ref, cend_ref, q_ref, cq_ref, k_ref, cx_ref, v_ref, gate_ref, o_ref,
                m_ref, acc_ref):
    blk = FOX_KEYS
    streams = range(q_ref.shape[0] // blk)
    pair = pl.program_id(0)
    qi = pl.program_id(1)
    nblk = pl.num_programs(1) * len(streams)
    lane = lax.broadcasted_iota(jnp.int32, (1, LANES), 1)
    lo_lanes = lane < FOX_DH
    reps = blk // LANES
    diag = [qi * len(streams) + s for s in streams]
    heads = [pair * HEAD_PAIR + hh for hh in range(HEAD_PAIR)]

    q = q_ref[...]
    zero = jnp.zeros_like(q)
    q_lo, q_hi = jnp.where(lo_lanes, q, zero), jnp.where(lo_lanes, zero, q)
    q_stack = [jnp.concatenate([q_lo[s * blk:(s + 1) * blk], q_hi[s * blk:(s + 1) * blk]], axis=0)
               for s in streams]
    g_row = lax.broadcasted_iota(jnp.int32, (CX_ROWS, CX_ROWS), 0)
    g_col = lax.broadcasted_iota(jnp.int32, (CX_ROWS, CX_ROWS), 1)
    xlane = lax.broadcasted_iota(jnp.int32, (HEAD_PAIR * blk, CX_ROWS), 1)
    cq = cq_ref[:, 0:CX_ROWS]

    def lane_map(h, shift):
        g = jnp.where((g_row == CX_ONE) & ((g_col == CX_HI + h) | (g_col == CX_MID + h)
                                           | (g_col == CX_LO + h)), 1.0, 0.0)
        if shift:
            for part, base in enumerate((CX_HI, CX_MID, CX_LO)):
                g = jnp.where((g_row == base + h) & (g_col == CX_ONE + part), -1.0, g)
        return g.astype(BF16)

    def q_aug(s, shift, void=None):
        cs = cq[s * blk:(s + 1) * blk]
        extra = jnp.concatenate([jnp.dot(cs, lane_map(h, shift), preferred_element_type=F32)
                                 for h in heads], axis=0)
        if void is not None:
            extra = jnp.where(jnp.logical_and(xlane == CX_ONE + 3, void), NEG_BIG, extra)
        return jnp.concatenate([q_stack[s], extra.astype(BF16)], axis=1)

    acc_ref[...] = jnp.zeros_like(acc_ref)
    qpos = lax.broadcasted_iota(jnp.int32, (blk, blk), 0)
    kpos = lax.broadcasted_iota(jnp.int32, (blk, blk), 1)

    def block(j):
        ks = pl.ds(pl.multiple_of(jnp.maximum(j, 0) * blk, blk), blk)
        k_aug = jnp.concatenate([k_ref[:, ks], cx_ref[0:CX_ROWS, ks]], axis=0)
        vb = v_ref[ks, :]
        one = jnp.ones_like(vb)
        return k_aug, (jnp.where(lo_lanes, vb, one), jnp.where(lo_lanes, one, vb))

    def step_online(s, j, qa, masked):
        rows = slice(s * blk, (s + 1) * blk)
        k_aug, vaug = block(j)
        s_all = jnp.dot(qa, k_aug, preferred_element_type=F32)
        for hh in range(HEAD_PAIR):
            sc = s_all[hh * blk:(hh + 1) * blk]
            if masked:
                sc = jnp.where(kpos <= qpos, sc, NEG_BIG)
            m_prev = m_ref[hh, rows]
            m_new = jnp.maximum(m_prev, jnp.max(sc, axis=1, keepdims=True))
            p = jnp.exp(sc - jnp.tile(m_new, (1, reps)))
            alpha = jnp.exp(m_prev - m_new)
            pv = jnp.dot(p.astype(BF16), vaug[hh], preferred_element_type=F32)
            acc_ref[hh, rows] = alpha * acc_ref[hh, rows] + pv
            m_ref[hh, rows] = m_new

    def step_direct(s, j, qa, masked, only=None):
        rows = slice(s * blk, (s + 1) * blk)
        k_aug, vaug = block(j)
        hsel = range(HEAD_PAIR) if only is None else (only,)
        lhs = qa if only is None else qa[only * blk:(only + 1) * blk]
        s_all = jnp.dot(lhs, k_aug, preferred_element_type=F32)
        for n, hh in enumerate(hsel):
            sc = s_all[n * blk:(n + 1) * blk]
            if masked:
                sc = jnp.where(kpos <= qpos, sc, NEG_BIG)
            acc_ref[hh, rows] += jnp.dot(jnp.exp(sc).astype(BF16), vaug[hh],
                                         preferred_element_type=F32)

    k_max2 = [kn2_ref[h] for h in heads]

    def norm2(s, hh):
        stat = (diag[s] * blk // PROJ_ROWS) * LANES
        return (NORM_SLACK * NORM_SLACK) * qn2_ref[stat + heads[hh]] * k_max2[hh]

    n2 = [[norm2(s, hh) for hh in range(HEAD_PAIR)] for s in streams]
    gap0 = [[FOX_SKIP_NATS + cend_ref[heads[hh] * nblk + jnp.maximum(diag[s] - 1, 0)]
             for hh in range(HEAD_PAIR)] for s in streams]

    def live_head(t, hh, direct):
        keep = False
        for s in streams:
            j = diag[s] - 1 - t
            gap = gap0[s][hh] - cend_ref[heads[hh] * nblk + jnp.maximum(j, 0)]
            bound2 = n2[s][hh] if direct else 4.0 * n2[s][hh]
            dead = jnp.logical_and(gap <= 0.0, bound2 <= gap * gap)
            keep = jnp.logical_or(keep, jnp.logical_and(j >= 0, jnp.logical_not(dead)))
        return keep

    def sweeps(stepper, direct):
        def sweep(back, masked=False, only=None):
            for s in streams:
                stepper(s, diag[s] - back, masked, only)

        def loop(t0, cond, only=None):
            def body(t):
                sweep(t + 1, only=only)
                return t + 1
            return lax.while_loop(cond, body, t0)

        def live(t, hh):
            return live_head(t, hh, direct)

        sweep(0, masked=True)
        sweep(1)
        if direct:
            t_both = loop(1, lambda t: jnp.logical_and(live(t, 0), live(t, 1)))
            for hh in range(HEAD_PAIR):
                loop(t_both, lambda t, hh=hh: live(t, hh), only=hh)
        else:
            loop(1, lambda t: jnp.logical_or(live(t, 0), live(t, 1)))

    direct_ok = True
    for s in streams:
        for hh in range(HEAD_PAIR):
            direct_ok = jnp.logical_and(direct_ok, n2[s][hh] <= FOX_DIRECT_NORM2)

    @pl.when(direct_ok)
    def _():
        q_dir = [q_aug(s, shift=True) for s in streams]
        q_void = [q_aug(s, shift=True, void=True) for s in streams]
        sweeps(lambda s, j, masked, only: step_direct(
            s, j, q_dir[s] if masked else jnp.where(j >= 0, q_dir[s], q_void[s]), masked, only),
            direct=True)

    @pl.when(jnp.logical_not(direct_ok))
    def _():
        m_ref[...] = jnp.full_like(m_ref, NEG_BIG)
        sweeps(lambda s, j, masked, only: step_online(
            s, j, q_aug(s, shift=False, void=j < 0), masked), direct=False)

    outs = []
    for hh in range(HEAD_PAIR):
        acc = acc_ref[hh]
        outs.append(acc / pltpu.roll(acc, FOX_DH, axis=1))
    o = jnp.where(lo_lanes, outs[0], outs[1])
    o_ref[...] = (o * gate_ref[...].astype(F32)).astype(BF16)


def _out_kernel(x_hbm, gla_ref, fox_ref, mq_ref, mg_ref, mem_ref, mem_g_ref, w_mem_ref,
                w_out_ref, fg_ref, o_ref, wo_ref, mk_ref, mv_ref, x_buf, x_sem):
    step = pl.program_id(0)
    n_steps = pl.num_programs(0)
    rows = o_ref.shape[0]

    chunk = rows // X_CHUNKS

    def x_copies(s):
        slot = lax.rem(s, X_SLOTS)
        return [pltpu.make_async_copy(
            x_hbm.at[pl.ds(pl.multiple_of(s * rows + c * chunk, chunk), chunk)],
            x_buf.at[slot, pl.ds(c * chunk, chunk)], x_sem.at[slot, c]) for c in range(X_CHUNKS)]

    @pl.when(step == 0)
    def _():
        for s in range(X_SLOTS - 1):
            for cp in x_copies(s):
                cp.start()
        wo_ref[...] = w_out_ref[...].astype(BF16)
        _memkv(mem_ref, mem_g_ref, w_mem_ref, mk_ref, mv_ref)

    @pl.when(step + (X_SLOTS - 1) < n_steps)
    def _():
        for cp in x_copies(step + (X_SLOTS - 1)):
            cp.start()

    for cp in x_copies(step):
        cp.wait()

    lane = lax.broadcasted_iota(jnp.int32, (1, LANES), 1)
    lo_lanes = lane < MEM_DH
    nt = (((1,), (1,)), ((), ()))
    mem_parts = []
    for p in range(MEM_HEADS // HEAD_PAIR):
        ls = slice(p * LANES, (p + 1) * LANES)
        q = mq_ref[:, ls]
        kb = mk_ref[:, ls]
        vb = mv_ref[:, ls]
        zero = jnp.zeros_like(q)
        one = jnp.ones_like(vb)
        qh = (jnp.where(lo_lanes, q, zero), jnp.where(lo_lanes, zero, q))
        vaug = (jnp.where(lo_lanes, vb, one), jnp.where(lo_lanes, one, vb))
        outs = []
        for hh in range(HEAD_PAIR):
            s = lax.dot_general(qh[hh], kb, nt, preferred_element_type=F32)
            pexp = jnp.exp(s - jnp.max(s, axis=1, keepdims=True))
            pv = jnp.dot(pexp.astype(BF16), vaug[hh], preferred_element_type=F32)
            outs.append(pv / pltpu.roll(pv, MEM_DH, axis=1))
        o = jnp.where(lo_lanes, outs[0], outs[1])
        mem_parts.append((o * mg_ref[:, ls].astype(F32)).astype(BF16))
    mixed = jnp.concatenate([gla_ref[...], fox_ref[...]] + mem_parts, axis=1)
    y = x_buf[lax.rem(step, X_SLOTS)] + jnp.dot(mixed, wo_ref[...], preferred_element_type=F32)
    o_ref[...] = y * _rms_scale(y, D_MODEL) * fg_ref[...]


def _params(*sem):
    return pltpu.CompilerParams(dimension_semantics=sem, vmem_limit_bytes=VMEM_LIMIT)


def _layer(x, mem, norm_g, w_in, w_alpha_up, b_alpha, b_forget, gla_norm_g,
           mem_norm_g, w_mem_kv, w_out, out_g):
    w_in_t = jnp.transpose(w_in[None], (0, 2, 1)).reshape(-1, LANES)
    T = x.shape[0]
    M = mem.shape[0]

    def rows(width, n=PROJ_ROWS):
        return pl.BlockSpec((n, width), lambda i: (i, 0))

    def whole(shape):
        return pl.BlockSpec(shape, lambda i: (0,) * len(shape))

    def once(shape):
        return pl.BlockSpec(shape, lambda i: (0,) * len(shape), pipeline_mode=pl.Buffered(1))

    bshape = lambda w: jax.ShapeDtypeStruct((T, w), BF16)
    nproj = T // PROJ_ROWS
    stat_spec = pl.BlockSpec((1, 1, LANES), lambda i: (i, 0, 0))
    stat_shape = jax.ShapeDtypeStruct((nproj, 1, LANES), F32)
    gla_pairs, gla_chunks = GLA_HEADS // HEAD_PAIR, PROJ_ROWS // GLA_CHUNK
    pair_w = HEAD_PAIR * GLA_DV_PAD
    (gla, fqvg, kx, mqg, crow, cq, qn2, kn2) = pl.pallas_call(
        _proj_kernel,
        grid=(nproj,),
        in_specs=[rows(D_MODEL), whole((1, D_MODEL)),
                  pl.BlockSpec(memory_space=pl.ANY),
                  whole(w_alpha_up.shape), whole((1, GLA_HEADS * GLA_DK)),
                  whole((1, FOX_HEADS)), whole((1, GLA_DV))],
        out_specs=[rows(GLA_OUT_W), rows(3 * FOX_W),
                   pl.BlockSpec((FOX_W + LANES, PROJ_ROWS), lambda i: (0, i)),
                   rows(2 * MEM_W),
                   pl.BlockSpec((SUBLANES, PROJ_ROWS), lambda i: (0, i)), rows(LANES),
                   stat_spec, pl.BlockSpec((1, 1, LANES), lambda i: (0, 0, 0))],
        out_shape=[bshape(GLA_OUT_W), bshape(3 * FOX_W),
                   jax.ShapeDtypeStruct((FOX_W + LANES, T), BF16),
                   bshape(2 * MEM_W),
                   jax.ShapeDtypeStruct((SUBLANES, T), F32), bshape(LANES),
                   stat_shape, jax.ShapeDtypeStruct((1, 1, LANES), F32)],
        scratch_shapes=[
            pltpu.VMEM((SUBLANES, LANES), F32),
            pltpu.VMEM((IN_COLS_PAD, D_MODEL), BF16),
            pltpu.VMEM((SMALL_W, GLA_QK_W), F32),
            pltpu.VMEM((1, GLA_QK_W), F32),
            pltpu.VMEM((1, SMALL_W), F32),
            pltpu.VMEM((1, GLA_DV_PAD), F32),
            pltpu.VMEM((FOX_W, LANES), BF16),
            pltpu.VMEM((PROJ_ROWS, SMALL_W), F32),
            pltpu.VMEM((PROJ_ROWS, GLA_QK_W), BF16),
            pltpu.VMEM((PROJ_ROWS, GLA_QK_W), BF16),
            pltpu.VMEM((PROJ_ROWS, GLA_V_W), BF16),
            pltpu.VMEM((PROJ_ROWS, GLA_V_W), BF16),
            pltpu.VMEM((PROJ_ROWS, GLA_QK_W), F32),
            pltpu.VMEM((gla_pairs, LANES, pair_w), F32),
            pltpu.VMEM((PROJ_ROWS, GLA_V_W), BF16),
            pltpu.VMEM((gla_pairs, gla_chunks, LANES, pair_w), F32),
            pltpu.VMEM((gla_pairs, gla_chunks, LANES, LANES), F32),
            pltpu.VMEM((gla_pairs, gla_chunks, LANES, pair_w), BF16),
            pltpu.VMEM(w_in_t.shape, F32),
            pltpu.SemaphoreType.DMA((W_CHUNKS,))],
        compiler_params=_params("arbitrary"),
        name="proj",
    )(x, norm_g[None, :], w_in_t, w_alpha_up, b_alpha[None, :], b_forget[None, :],
      gla_norm_g[None, :])

    cend = crow[:FOX_HEADS, FOX_KEYS - 1::FOX_KEYS].reshape(-1)
    fox_pairs = FOX_HEADS // HEAD_PAIR
    pair_rows = pl.BlockSpec((FOX_BLOCK, LANES), lambda p, i, *_: (i, p))
    fox = pl.pallas_call(
        _fox_kernel,
        grid_spec=pltpu.PrefetchScalarGridSpec(
            num_scalar_prefetch=3,
            grid=(fox_pairs, T // FOX_BLOCK),
            in_specs=[pair_rows,
                      pl.BlockSpec((FOX_BLOCK, LANES), lambda p, i, *_: (i, 0)),
                      pl.BlockSpec((LANES, T), lambda p, i, *_: (p, 0)),
                      pl.BlockSpec((LANES, T), lambda p, i, *_: (fox_pairs, 0)),
                      pl.BlockSpec((T, LANES), lambda p, i, *_: (0, fox_pairs + p)),
                      pl.BlockSpec((FOX_BLOCK, LANES),
                                   lambda p, i, *_: (i, 2 * fox_pairs + p))],
            out_specs=pair_rows,
            scratch_shapes=[pltpu.VMEM((HEAD_PAIR, FOX_BLOCK, LANES), F32),
                            pltpu.VMEM((HEAD_PAIR, FOX_BLOCK, LANES), F32)]),
        out_shape=bshape(FOX_W),
        compiler_params=_params("arbitrary", "arbitrary"),
        name="fox",
    )(qn2.reshape(-1), kn2.reshape(-1), cend, fqvg, cq, kx, kx, fqvg, fqvg)

    out = pl.pallas_call(
        _out_kernel,
        grid=(T // OUT_ROWS,),
        in_specs=[pl.BlockSpec(memory_space=pl.ANY),
                  rows(GLA_OUT_W, OUT_ROWS), rows(FOX_W, OUT_ROWS),
                  pl.BlockSpec((OUT_ROWS, MEM_W), lambda i: (i, 0)),
                  pl.BlockSpec((OUT_ROWS, MEM_W), lambda i: (i, 1)),
                  once(mem.shape), once((1, D_MODEL)), once(w_mem_kv.shape), once(w_out.shape),
                  whole((1, D_MODEL))],
        out_specs=rows(D_MODEL, OUT_ROWS),
        out_shape=jax.ShapeDtypeStruct((T, D_MODEL), F32),
        scratch_shapes=[pltpu.VMEM((D_MODEL, D_MODEL), BF16),
                        pltpu.VMEM((M, MEM_W), BF16), pltpu.VMEM((M, MEM_W), BF16),
                        pltpu.VMEM((X_SLOTS, OUT_ROWS, D_MODEL), F32),
                        pltpu.SemaphoreType.DMA((X_SLOTS, X_CHUNKS))],
        compiler_params=_params("arbitrary"),
        name="out",
    )(x, gla, fox, mqg, mqg, mem, mem_norm_g[None, :], w_mem_kv, w_out, out_g[None, :])
    return out


def kernel(x, mem, norm_g, w_in, w_alpha_up, b_alpha, b_forget, gla_norm_g, mem_norm_g,
           w_mem_kv, w_out, final_norm_g):
    assert x.shape[0] == 1 and mem.shape[0] == 1 and norm_g.shape[0] == 1
    assert x.shape[1] % max(PROJ_ROWS, FOX_BLOCK, OUT_ROWS) == 0
    out = _layer(x[0], mem[0], norm_g[0], w_in[0], w_alpha_up[0], b_alpha[0], b_forget[0],
                 gla_norm_g[0], mem_norm_g[0], w_mem_kv[0], w_out[0], final_norm_g)
    return out[None]
```
